```python
import jax, jax.numpy as jnp
from jax import lax
import numpy as np

D_MODEL = 2048
BATCH = 8
SEQ = 4096
DEPTH = 1

HEAD_DIM = D_MODEL // 16
FOX_HEADS = 6
SWA_HEADS = 6
SWA_KV_HEADS = 2
MEM_HEADS = 4
MEM_LEN = 256
WINDOW = 128
Q_BLOCK = 128
D_FF = 5632
EPS = 1e-6
NEG_INF = -1e30

FOX_W = FOX_HEADS * HEAD_DIM
SWA_Q_W = SWA_HEADS * HEAD_DIM
SWA_KV_W = SWA_KV_HEADS * HEAD_DIM
MEM_W = MEM_HEADS * HEAD_DIM
MIX_WIDTH = FOX_W + SWA_Q_W + MEM_W
IN_SPLITS = [FOX_W, FOX_W, FOX_W, FOX_HEADS, SWA_Q_W, SWA_KV_W, SWA_KV_W, MEM_W]
IN_WIDTH = int(sum(IN_SPLITS))
IN_CUTS = [int(c) for c in np.cumsum(IN_SPLITS)[:-1]]

kernel_name = "hybrid_fox_swa_memory_macaron"


def rms_norm(x, g):
    xf = x.astype(jnp.float32)
    y = xf * lax.rsqrt(jnp.mean(xf * xf, axis=-1, keepdims=True) + EPS)
    return (y * g.astype(jnp.float32)).astype(x.dtype)


def swiglu(x, w_gate, w_up, w_down):
    return (jax.nn.silu(x @ w_gate) * (x @ w_up)) @ w_down


def alibi_slopes(n):
    return jnp.asarray(2.0 ** (-8.0 * np.arange(1, n + 1) / n), dtype=jnp.float32)


def forgetting_attention(q, k, v, log_f):
    B, S, H, D = q.shape
    nb = S // Q_BLOCK
    scale = D ** -0.5
    c = jnp.cumsum(log_f, axis=1)
    c_k = c.transpose(0, 2, 1)
    kf = k.astype(jnp.float32)
    vf = v.astype(jnp.float32)
    qb = q.astype(jnp.float32).reshape(B, nb, Q_BLOCK, H, D).transpose(1, 0, 2, 3, 4)
    cb = c.reshape(B, nb, Q_BLOCK, H).transpose(1, 0, 3, 2)
    kpos = jnp.arange(S)

    def block(args):
        qi, ci, i = args
        qpos = i * Q_BLOCK + jnp.arange(Q_BLOCK)
        s = jnp.einsum('bqhd,bkhd->bhqk', qi, kf) * scale
        s = s + ci[..., None] - c_k[:, :, None, :]
        causal = kpos[None, :] <= qpos[:, None]
        s = jnp.where(causal, s, NEG_INF)
        p = jax.nn.softmax(s, axis=-1)
        return jnp.einsum('bhqk,bkhd->bqhd', p, vf)

    out = lax.map(block, (qb, cb, jnp.arange(nb)))
    return out.transpose(1, 0, 2, 3, 4).reshape(B, S, H, D).astype(q.dtype)


def sliding_window_sink_attention(q, k, v, sinks, slopes):
    B, S, Hq, D = q.shape
    Hkv = k.shape[2]
    G = Hq // Hkv
    nb = S // WINDOW
    scale = D ** -0.5
    qb = q.astype(jnp.float32).reshape(B, nb, WINDOW, Hkv, G, D)

    def band(t):
        tb = t.astype(jnp.float32).reshape(B, nb, WINDOW, Hkv, D)
        prev = jnp.pad(tb[:, :-1], ((0, 0), (1, 0), (0, 0), (0, 0), (0, 0)))
        return jnp.concatenate([prev, tb], axis=2)

    kb, vb = band(k), band(v)
    s = jnp.einsum('bnqhgd,bnkhd->bnhgqk', qb, kb) * scale
    r = jnp.arange(WINDOW)[:, None]
    j = jnp.arange(2 * WINDOW)[None, :]
    dist = WINDOW + r - j
    in_window = (dist >= 0) & (dist < WINDOW)
    valid = in_window[None] & ((jnp.arange(nb)[:, None, None] > 0) | (j[None] >= WINDOW))
    alibi = -slopes.astype(jnp.float32).reshape(Hkv, G)[:, :, None, None] * dist.astype(jnp.float32)
    s = s + alibi[None, None]
    s = jnp.where(valid[None, :, None, None], s, NEG_INF)
    sink = jnp.broadcast_to(
        sinks.astype(jnp.float32).reshape(Hkv, G)[None, None, :, :, None, None],
        s.shape[:-1] + (1,))
    p = jax.nn.softmax(jnp.concatenate([s, sink], axis=-1), axis=-1)[..., :-1]
    out = jnp.einsum('bnhgqk,bnkhd->bnqhgd', p, vb)
    return out.reshape(B, S, Hq, D).astype(q.dtype)


def memory_attention(q, mk, mv):
    scale = q.shape[-1] ** -0.5
    s = jnp.einsum('bqhd,bmhd->bhqm', q.astype(jnp.float32), mk.astype(jnp.float32)) * scale
    p = jax.nn.softmax(s, axis=-1)
    return jnp.einsum('bhqm,bmhd->bqhd', p, mv.astype(jnp.float32)).astype(q.dtype)


def _fwd_setup_inputs(seed: int = 0) -> dict:
    key = jax.random.key(seed)
    ks = jax.random.split(key, 32)
    f32 = jnp.float32

    def w(k, shape, fan_in):
        return jax.random.normal(k, shape, f32) * (fan_in ** -0.5)

    def gain(k, shape):
        return 1.0 + 0.02 * jax.random.normal(k, shape, f32)

    L = DEPTH
    return {
        "x": jax.random.normal(ks[0], (BATCH, SEQ, D_MODEL), f32),
        "mem": jax.random.normal(ks[1], (BATCH, MEM_LEN, D_MODEL), f32),
        "ffn1_norm": gain(ks[2], (L, D_MODEL)),
        "ffn1_gate": w(ks[3], (L, D_MODEL, D_FF), D_MODEL),
        "ffn1_up": w(ks[4], (L, D_MODEL, D_FF), D_MODEL),
        "ffn1_down": w(ks[5], (L, D_FF, D_MODEL), D_FF),
        "mix_norm": gain(ks[6], (L, D_MODEL)),
        "mem_norm": gain(ks[7], (L, D_MODEL)),
        "w_in": w(ks[8], (L, D_MODEL, IN_WIDTH), D_MODEL),
        "forget_bias": jax.random.uniform(ks[9], (L, FOX_HEADS), f32, 1.0, 4.0),
        "w_mem_k": w(ks[10], (L, D_MODEL, MEM_W), D_MODEL),
        "w_mem_v": w(ks[11], (L, D_MODEL, MEM_W), D_MODEL),
        "fox_q_gain": gain(ks[12], (L, HEAD_DIM)),
        "fox_k_gain": gain(ks[13], (L, HEAD_DIM)),
        "swa_q_gain": gain(ks[14], (L, HEAD_DIM)),
        "swa_k_gain": gain(ks[15], (L, HEAD_DIM)),
        "swa_sinks": jax.random.normal(ks[16], (L, SWA_HEADS), f32),
        "mem_q_gain": gain(ks[17], (L, HEAD_DIM)),
        "mem_k_gain": gain(ks[18], (L, HEAD_DIM)),
        "w_out": w(ks[19], (L, MIX_WIDTH, D_MODEL), MIX_WIDTH),
        "ffn2_norm": gain(ks[20], (L, D_MODEL)),
        "ffn2_gate": w(ks[21], (L, D_MODEL, D_FF), D_MODEL),
        "ffn2_up": w(ks[22], (L, D_MODEL, D_FF), D_MODEL),
        "ffn2_down": w(ks[23], (L, D_FF, D_MODEL), D_FF),
    }


def _fwd_reference(x, mem, ffn1_norm, ffn1_gate, ffn1_up, ffn1_down, mix_norm, mem_norm, w_in,
              forget_bias, w_mem_k, w_mem_v, fox_q_gain, fox_k_gain, swa_q_gain, swa_k_gain,
              swa_sinks, mem_q_gain, mem_k_gain, w_out, ffn2_norm, ffn2_gate, ffn2_up, ffn2_down):
    B, S, _ = x.shape
    M = mem.shape[1]
    slopes = alibi_slopes(SWA_HEADS).astype(x.dtype)
    for l in range(DEPTH):
        x = x + 0.5 * swiglu(rms_norm(x, ffn1_norm[l]), ffn1_gate[l], ffn1_up[l], ffn1_down[l])

        h = rms_norm(x, mix_norm[l])
        proj = h @ w_in[l]
        fq, fk, fv, f_logit, sq, sk, sv, mq = jnp.split(proj, IN_CUTS, axis=-1)

        fq = rms_norm(fq.reshape(B, S, FOX_HEADS, HEAD_DIM), fox_q_gain[l])
        fk = rms_norm(fk.reshape(B, S, FOX_HEADS, HEAD_DIM), fox_k_gain[l])
        fv = fv.reshape(B, S, FOX_HEADS, HEAD_DIM)
        log_f = jax.nn.log_sigmoid(f_logit.astype(jnp.float32) + forget_bias[l].astype(jnp.float32))
        out_a = forgetting_attention(fq, fk, fv, log_f)

        sq = rms_norm(sq.reshape(B, S, SWA_HEADS, HEAD_DIM), swa_q_gain[l])
        sk = rms_norm(sk.reshape(B, S, SWA_KV_HEADS, HEAD_DIM), swa_k_gain[l])
        sv = sv.reshape(B, S, SWA_KV_HEADS, HEAD_DIM)
        out_b = sliding_window_sink_attention(sq, sk, sv, swa_sinks[l], slopes)

        mn = rms_norm(mem, mem_norm[l])
        mk = rms_norm((mn @ w_mem_k[l]).reshape(B, M, MEM_HEADS, HEAD_DIM), mem_k_gain[l])
        mv = (mn @ w_mem_v[l]).reshape(B, M, MEM_HEADS, HEAD_DIM)
        mq = rms_norm(mq.reshape(B, S, MEM_HEADS, HEAD_DIM), mem_q_gain[l])
        out_c = memory_attention(mq, mk, mv)

        mixed = jnp.concatenate([out_a.reshape(B, S, FOX_W), out_b.reshape(B, S, SWA_Q_W),
                                 out_c.reshape(B, S, MEM_W)], axis=-1)
        x = x + mixed @ w_out[l]

        x = x + 0.5 * swiglu(rms_norm(x, ffn2_norm[l]), ffn2_gate[l], ffn2_up[l], ffn2_down[l])
    return x


import jax as _jax
import jax.numpy as _jnp

TWIN_FORMAT = 'train_step'
FWD_PARAMS = ['x', 'mem', 'ffn1_norm', 'ffn1_gate', 'ffn1_up', 'ffn1_down', 'mix_norm', 'mem_norm', 'w_in', 'forget_bias', 'w_mem_k', 'w_mem_v', 'fox_q_gain', 'fox_k_gain', 'swa_q_gain', 'swa_k_gain', 'swa_sinks', 'mem_q_gain', 'mem_k_gain', 'w_out', 'ffn2_norm', 'ffn2_gate', 'ffn2_up', 'ffn2_down']
TWIN_WEIGHTS = ['ffn1_norm', 'ffn1_gate', 'ffn1_up', 'ffn1_down', 'mix_norm', 'mem_norm', 'w_in', 'forget_bias', 'w_mem_k', 'w_mem_v', 'fox_q_gain', 'fox_k_gain', 'swa_q_gain', 'swa_k_gain', 'swa_sinks', 'mem_q_gain', 'mem_k_gain', 'w_out', 'ffn2_norm', 'ffn2_gate', 'ffn2_up', 'ffn2_down']
TWIN_DIFF_INPUT = 'x'
TWIN_INPUTS = ['x', 'mem', 'ffn1_norm', 'ffn1_gate', 'ffn1_up', 'ffn1_down', 'mix_norm', 'mem_norm', 'w_in', 'forget_bias', 'w_mem_k', 'w_mem_v', 'fox_q_gain', 'fox_k_gain', 'swa_q_gain', 'swa_k_gain', 'swa_sinks', 'mem_q_gain', 'mem_k_gain', 'w_out', 'ffn2_norm', 'ffn2_gate', 'ffn2_up', 'ffn2_down', 'loss_target', 'm_ffn1_norm', 'm_ffn1_gate', 'm_ffn1_up', 'm_ffn1_down', 'm_mix_norm', 'm_mem_norm', 'm_w_in', 'm_forget_bias', 'm_w_mem_k', 'm_w_mem_v', 'm_fox_q_gain', 'm_fox_k_gain', 'm_swa_q_gain', 'm_swa_k_gain', 'm_swa_sinks', 'm_mem_q_gain', 'm_mem_k_gain', 'm_w_out', 'm_ffn2_norm', 'm_ffn2_gate', 'm_ffn2_up', 'm_ffn2_down', 'v_ffn1_norm', 'v_ffn1_gate', 'v_ffn1_up', 'v_ffn1_down', 'v_mix_norm', 'v_mem_norm', 'v_w_in', 'v_forget_bias', 'v_w_mem_k', 'v_w_mem_v', 'v_fox_q_gain', 'v_fox_k_gain', 'v_swa_q_gain', 'v_swa_k_gain', 'v_swa_sinks', 'v_mem_q_gain', 'v_mem_k_gain', 'v_w_out', 'v_ffn2_norm', 'v_ffn2_gate', 'v_ffn2_up', 'v_ffn2_down']
TWIN_OUTPUTS = ['loss', 'grad_x', 'grad_ffn1_norm', 'grad_ffn1_gate', 'grad_ffn1_up', 'grad_ffn1_down', 'grad_mix_norm', 'grad_mem_norm', 'grad_w_in', 'grad_forget_bias', 'grad_w_mem_k', 'grad_w_mem_v', 'grad_fox_q_gain', 'grad_fox_k_gain', 'grad_swa_q_gain', 'grad_swa_k_gain', 'grad_swa_sinks', 'grad_mem_q_gain', 'grad_mem_k_gain', 'grad_w_out', 'grad_ffn2_norm', 'grad_ffn2_gate', 'grad_ffn2_up', 'grad_ffn2_down', 'delta_ffn1_norm', 'delta_ffn1_gate', 'delta_ffn1_up', 'delta_ffn1_down', 'delta_mix_norm', 'delta_mem_norm', 'delta_w_in', 'delta_forget_bias', 'delta_w_mem_k', 'delta_w_mem_v', 'delta_fox_q_gain', 'delta_fox_k_gain', 'delta_swa_q_gain', 'delta_swa_k_gain', 'delta_swa_sinks', 'delta_mem_q_gain', 'delta_mem_k_gain', 'delta_w_out', 'delta_ffn2_norm', 'delta_ffn2_gate', 'delta_ffn2_up', 'delta_ffn2_down', 'new_m_ffn1_norm', 'new_m_ffn1_gate', 'new_m_ffn1_up', 'new_m_ffn1_down', 'new_m_mix_norm', 'new_m_mem_norm', 'new_m_w_in', 'new_m_forget_bias', 'new_m_w_mem_k', 'new_m_w_mem_v', 'new_m_fox_q_gain', 'new_m_fox_k_gain', 'new_m_swa_q_gain', 'new_m_swa_k_gain', 'new_m_swa_sinks', 'new_m_mem_q_gain', 'new_m_mem_k_gain', 'new_m_w_out', 'new_m_ffn2_norm', 'new_m_ffn2_gate', 'new_m_ffn2_up', 'new_m_ffn2_down', 'new_v_ffn1_norm', 'new_v_ffn1_gate', 'new_v_ffn1_up', 'new_v_ffn1_down', 'new_v_mix_norm', 'new_v_mem_norm', 'new_v_w_in', 'new_v_forget_bias', 'new_v_w_mem_k', 'new_v_w_mem_v', 'new_v_fox_q_gain', 'new_v_fox_k_gain', 'new_v_swa_q_gain', 'new_v_swa_k_gain', 'new_v_swa_sinks', 'new_v_mem_q_gain', 'new_v_mem_k_gain', 'new_v_w_out', 'new_v_ffn2_norm', 'new_v_ffn2_gate', 'new_v_ffn2_up', 'new_v_ffn2_down']
TWIN_LEAF_KINDS = {'loss': 'loss', 'grad_x': 'grad_x', 'grad_ffn1_norm': 'grad_w', 'grad_ffn1_gate': 'grad_w', 'grad_ffn1_up': 'grad_w', 'grad_ffn1_down': 'grad_w', 'grad_mix_norm': 'grad_w', 'grad_mem_norm': 'grad_w', 'grad_w_in': 'grad_w', 'grad_forget_bias': 'grad_w', 'grad_w_mem_k': 'grad_w', 'grad_w_mem_v': 'grad_w', 'grad_fox_q_gain': 'grad_w', 'grad_fox_k_gain': 'grad_w', 'grad_swa_q_gain': 'grad_w', 'grad_swa_k_gain': 'grad_w', 'grad_swa_sinks': 'grad_w', 'grad_mem_q_gain': 'grad_w', 'grad_mem_k_gain': 'grad_w', 'grad_w_out': 'grad_w', 'grad_ffn2_norm': 'grad_w', 'grad_ffn2_gate': 'grad_w', 'grad_ffn2_up': 'grad_w', 'grad_ffn2_down': 'grad_w', 'delta_ffn1_norm': 'delta_w', 'delta_ffn1_gate': 'delta_w', 'delta_ffn1_up': 'delta_w', 'delta_ffn1_down': 'delta_w', 'delta_mix_norm': 'delta_w', 'delta_mem_norm': 'delta_w', 'delta_w_in': 'delta_w', 'delta_forget_bias': 'delta_w', 'delta_w_mem_k': 'delta_w', 'delta_w_mem_v': 'delta_w', 'delta_fox_q_gain': 'delta_w', 'delta_fox_k_gain': 'delta_w', 'delta_swa_q_gain': 'delta_w', 'delta_swa_k_gain': 'delta_w', 'delta_swa_sinks': 'delta_w', 'delta_mem_q_gain': 'delta_w', 'delta_mem_k_gain': 'delta_w', 'delta_w_out': 'delta_w', 'delta_ffn2_norm': 'delta_w', 'delta_ffn2_gate': 'delta_w', 'delta_ffn2_up': 'delta_w', 'delta_ffn2_down': 'delta_w', 'new_m_ffn1_norm': 'new_m', 'new_m_ffn1_gate': 'new_m', 'new_m_ffn1_up': 'new_m', 'new_m_ffn1_down': 'new_m', 'new_m_mix_norm': 'new_m', 'new_m_mem_norm': 'new_m', 'new_m_w_in': 'new_m', 'new_m_forget_bias': 'new_m', 'new_m_w_mem_k': 'new_m', 'new_m_w_mem_v': 'new_m', 'new_m_fox_q_gain': 'new_m', 'new_m_fox_k_gain': 'new_m', 'new_m_swa_q_gain': 'new_m', 'new_m_swa_k_gain': 'new_m', 'new_m_swa_sinks': 'new_m', 'new_m_mem_q_gain': 'new_m', 'new_m_mem_k_gain': 'new_m', 'new_m_w_out': 'new_m', 'new_m_ffn2_norm': 'new_m', 'new_m_ffn2_gate': 'new_m', 'new_m_ffn2_up': 'new_m', 'new_m_ffn2_down': 'new_m', 'new_v_ffn1_norm': 'new_v', 'new_v_ffn1_gate': 'new_v', 'new_v_ffn1_up': 'new_v', 'new_v_ffn1_down': 'new_v', 'new_v_mix_norm': 'new_v', 'new_v_mem_norm': 'new_v', 'new_v_w_in': 'new_v', 'new_v_forget_bias': 'new_v', 'new_v_w_mem_k': 'new_v', 'new_v_w_mem_v': 'new_v', 'new_v_fox_q_gain': 'new_v', 'new_v_fox_k_gain': 'new_v', 'new_v_swa_q_gain': 'new_v', 'new_v_swa_k_gain': 'new_v', 'new_v_swa_sinks': 'new_v', 'new_v_mem_q_gain': 'new_v', 'new_v_mem_k_gain': 'new_v', 'new_v_w_out': 'new_v', 'new_v_ffn2_norm': 'new_v', 'new_v_ffn2_gate': 'new_v', 'new_v_ffn2_up': 'new_v', 'new_v_ffn2_down': 'new_v'}


def _forward(args):
    return _fwd_reference(*[args[k] for k in FWD_PARAMS])


def _output_shape():
    def fwd():
        inp = _fwd_setup_inputs(0)
        return _fwd_reference(*[inp[k] for k in FWD_PARAMS])
    out = _jax.eval_shape(fwd)
    return out.shape, out.dtype

N_MICROBATCH = 1
ADAM_LR = 0.001
ADAM_B1 = 0.9
ADAM_B2 = 0.999
ADAM_EPS = 1e-08
ADAM_WD = 0.01
ADAM_STEP = 10
PER_EXAMPLE_BATCH_AXIS = {'x': 0, 'mem': 0, 'loss_target': 0}
SHARED_INPUTS = []
_WEIGHT_DTYPES = {'ffn1_norm': _jnp.float32, 'ffn1_gate': _jnp.float32, 'ffn1_up': _jnp.float32, 'ffn1_down': _jnp.float32, 'mix_norm': _jnp.float32, 'mem_norm': _jnp.float32, 'w_in': _jnp.float32, 'forget_bias': _jnp.float32, 'w_mem_k': _jnp.float32, 'w_mem_v': _jnp.float32, 'fox_q_gain': _jnp.float32, 'fox_k_gain': _jnp.float32, 'swa_q_gain': _jnp.float32, 'swa_k_gain': _jnp.float32, 'swa_sinks': _jnp.float32, 'mem_q_gain': _jnp.float32, 'mem_k_gain': _jnp.float32, 'w_out': _jnp.float32, 'ffn2_norm': _jnp.float32, 'ffn2_gate': _jnp.float32, 'ffn2_up': _jnp.float32, 'ffn2_down': _jnp.float32}
MOMENT_SCALE = {'ffn1_norm': 3.079122e+00, 'ffn1_gate': 3.538857e-02, 'ffn1_up': 3.772447e-02, 'ffn1_down': 6.169150e-02, 'mix_norm': 1.389861e+00, 'mem_norm': 4.050807e-02, 'w_in': 9.022514e-02, 'forget_bias': 8.637391e+01, 'w_mem_k': 1.683703e-02, 'w_mem_v': 2.851491e-02, 'fox_q_gain': 4.922546e+00, 'fox_k_gain': 4.927648e+00, 'swa_q_gain': 3.481294e+00, 'swa_k_gain': 3.488707e+00, 'swa_sinks': 1.426673e+01, 'mem_q_gain': 5.915524e-01, 'mem_k_gain': 5.914568e-01, 'w_out': 5.976211e-02, 'ffn2_norm': 3.094834e+00, 'ffn2_gate': 2.795989e-02, 'ffn2_up': 3.140753e-02, 'ffn2_down': 5.051951e-02}


def _to_microbatches(a, axis):
    t = _jnp.moveaxis(a, axis, 0)
    t = t.reshape((N_MICROBATCH, t.shape[0] // N_MICROBATCH) + t.shape[1:])
    return _jnp.moveaxis(t, 1, axis + 1)


def setup_inputs(seed: int = 0) -> dict:
    inp = _fwd_setup_inputs(seed)
    key = _jax.random.fold_in(_jax.random.key(seed), 7919)
    shape, _ = _output_shape()
    out = dict(inp)
    out["loss_target"] = _jax.random.normal(_jax.random.fold_in(key, 0), shape, _jnp.float32)
    for i, name in enumerate(TWIN_WEIGHTS):
        w = inp[name].astype(_jnp.float32)
        if MOMENT_SCALE is None:
            s = _jnp.sqrt(_jnp.mean(_jnp.square(w)) + 1e-30)
        else:
            s = MOMENT_SCALE[name]
        km, kv = _jax.random.split(_jax.random.fold_in(key, i + 1))
        out[name] = w
        out["m_" + name] = s * _jax.random.normal(km, w.shape, _jnp.float32)
        out["v_" + name] = (s * s) * _jax.random.uniform(kv, w.shape, _jnp.float32, 0.5, 1.5)
    if N_MICROBATCH > 1:
        for name, axis in PER_EXAMPLE_BATCH_AXIS.items():
            out[name] = _to_microbatches(out[name], axis)
    return {'x': out['x'], 'mem': out['mem'], 'ffn1_norm': out['ffn1_norm'], 'ffn1_gate': out['ffn1_gate'], 'ffn1_up': out['ffn1_up'], 'ffn1_down': out['ffn1_down'], 'mix_norm': out['mix_norm'], 'mem_norm': out['mem_norm'], 'w_in': out['w_in'], 'forget_bias': out['forget_bias'], 'w_mem_k': out['w_mem_k'], 'w_mem_v': out['w_mem_v'], 'fox_q_gain': out['fox_q_gain'], 'fox_k_gain': out['fox_k_gain'], 'swa_q_gain': out['swa_q_gain'], 'swa_k_gain': out['swa_k_gain'], 'swa_sinks': out['swa_sinks'], 'mem_q_gain': out['mem_q_gain'], 'mem_k_gain': out['mem_k_gain'], 'w_out': out['w_out'], 'ffn2_norm': out['ffn2_norm'], 'ffn2_gate': out['ffn2_gate'], 'ffn2_up': out['ffn2_up'], 'ffn2_down': out['ffn2_down'], 'loss_target': out['loss_target'], 'm_ffn1_norm': out['m_ffn1_norm'], 'm_ffn1_gate': out['m_ffn1_gate'], 'm_ffn1_up': out['m_ffn1_up'], 'm_ffn1_down': out['m_ffn1_down'], 'm_mix_norm': out['m_mix_norm'], 'm_mem_norm': out['m_mem_norm'], 'm_w_in': out['m_w_in'], 'm_forget_bias': out['m_forget_bias'], 'm_w_mem_k': out['m_w_mem_k'], 'm_w_mem_v': out['m_w_mem_v'], 'm_fox_q_gain': out['m_fox_q_gain'], 'm_fox_k_gain': out['m_fox_k_gain'], 'm_swa_q_gain': out['m_swa_q_gain'], 'm_swa_k_gain': out['m_swa_k_gain'], 'm_swa_sinks': out['m_swa_sinks'], 'm_mem_q_gain': out['m_mem_q_gain'], 'm_mem_k_gain': out['m_mem_k_gain'], 'm_w_out': out['m_w_out'], 'm_ffn2_norm': out['m_ffn2_norm'], 'm_ffn2_gate': out['m_ffn2_gate'], 'm_ffn2_up': out['m_ffn2_up'], 'm_ffn2_down': out['m_ffn2_down'], 'v_ffn1_norm': out['v_ffn1_norm'], 'v_ffn1_gate': out['v_ffn1_gate'], 'v_ffn1_up': out['v_ffn1_up'], 'v_ffn1_down': out['v_ffn1_down'], 'v_mix_norm': out['v_mix_norm'], 'v_mem_norm': out['v_mem_norm'], 'v_w_in': out['v_w_in'], 'v_forget_bias': out['v_forget_bias'], 'v_w_mem_k': out['v_w_mem_k'], 'v_w_mem_v': out['v_w_mem_v'], 'v_fox_q_gain': out['v_fox_q_gain'], 'v_fox_k_gain': out['v_fox_k_gain'], 'v_swa_q_gain': out['v_swa_q_gain'], 'v_swa_k_gain': out['v_swa_k_gain'], 'v_swa_sinks': out['v_swa_sinks'], 'v_mem_q_gain': out['v_mem_q_gain'], 'v_mem_k_gain': out['v_mem_k_gain'], 'v_w_out': out['v_w_out'], 'v_ffn2_norm': out['v_ffn2_norm'], 'v_ffn2_gate': out['v_ffn2_gate'], 'v_ffn2_up': out['v_ffn2_up'], 'v_ffn2_down': out['v_ffn2_down']}


def _loss(weights, diff, rest, loss_target):
    with _jax.named_scope("forward"):
        args = {**rest, TWIN_DIFF_INPUT: diff, **{k: w.astype(_WEIGHT_DTYPES[k]) for k, w in weights.items()}}
        y = _forward(args)
    with _jax.named_scope("loss_head"):
        err = _jnp.square(y.astype(_jnp.float32) - loss_target)
        return 0.5 * _jnp.sum(_jnp.mean(err, axis=-1)) if err.ndim else 0.5 * err


def _adamw(w, g, m, v):
    m = ADAM_B1 * m + (1.0 - ADAM_B1) * g
    v = ADAM_B2 * v + (1.0 - ADAM_B2) * _jnp.square(g)
    m_hat = m / (1.0 - ADAM_B1 ** ADAM_STEP)
    v_hat = v / (1.0 - ADAM_B2 ** ADAM_STEP)
    delta = -ADAM_LR * (m_hat / (_jnp.sqrt(v_hat) + ADAM_EPS) + ADAM_WD * w)
    return delta, m, v


def reference(x, mem, ffn1_norm, ffn1_gate, ffn1_up, ffn1_down, mix_norm, mem_norm, w_in, forget_bias, w_mem_k, w_mem_v, fox_q_gain, fox_k_gain, swa_q_gain, swa_k_gain, swa_sinks, mem_q_gain, mem_k_gain, w_out, ffn2_norm, ffn2_gate, ffn2_up, ffn2_down, loss_target, m_ffn1_norm, m_ffn1_gate, m_ffn1_up, m_ffn1_down, m_mix_norm, m_mem_norm, m_w_in, m_forget_bias, m_w_mem_k, m_w_mem_v, m_fox_q_gain, m_fox_k_gain, m_swa_q_gain, m_swa_k_gain, m_swa_sinks, m_mem_q_gain, m_mem_k_gain, m_w_out, m_ffn2_norm, m_ffn2_gate, m_ffn2_up, m_ffn2_down, v_ffn1_norm, v_ffn1_gate, v_ffn1_up, v_ffn1_down, v_mix_norm, v_mem_norm, v_w_in, v_forget_bias, v_w_mem_k, v_w_mem_v, v_fox_q_gain, v_fox_k_gain, v_swa_q_gain, v_swa_k_gain, v_swa_sinks, v_mem_q_gain, v_mem_k_gain, v_w_out, v_ffn2_norm, v_ffn2_gate, v_ffn2_up, v_ffn2_down):
    given = dict(x=x, mem=mem, ffn1_norm=ffn1_norm, ffn1_gate=ffn1_gate, ffn1_up=ffn1_up, ffn1_down=ffn1_down, mix_norm=mix_norm, mem_norm=mem_norm, w_in=w_in, forget_bias=forget_bias, w_mem_k=w_mem_k, w_mem_v=w_mem_v, fox_q_gain=fox_q_gain, fox_k_gain=fox_k_gain, swa_q_gain=swa_q_gain, swa_k_gain=swa_k_gain, swa_sinks=swa_sinks, mem_q_gain=mem_q_gain, mem_k_gain=mem_k_gain, w_out=w_out, ffn2_norm=ffn2_norm, ffn2_gate=ffn2_gate, ffn2_up=ffn2_up, ffn2_down=ffn2_down, loss_target=loss_target, m_ffn1_norm=m_ffn1_norm, m_ffn1_gate=m_ffn1_gate, m_ffn1_up=m_ffn1_up, m_ffn1_down=m_ffn1_down, m_mix_norm=m_mix_norm, m_mem_norm=m_mem_norm, m_w_in=m_w_in, m_forget_bias=m_forget_bias, m_w_mem_k=m_w_mem_k, m_w_mem_v=m_w_mem_v, m_fox_q_gain=m_fox_q_gain, m_fox_k_gain=m_fox_k_gain, m_swa_q_gain=m_swa_q_gain, m_swa_k_gain=m_swa_k_gain, m_swa_sinks=m_swa_sinks, m_mem_q_gain=m_mem_q_gain, m_mem_k_gain=m_mem_k_gain, m_w_out=m_w_out, m_ffn2_norm=m_ffn2_norm, m_ffn2_gate=m_ffn2_gate, m_ffn2_up=m_ffn2_up, m_ffn2_down=m_ffn2_down, v_ffn1_norm=v_ffn1_norm, v_ffn1_gate=v_ffn1_gate, v_ffn1_up=v_ffn1_up, v_ffn1_down=v_ffn1_down, v_mix_norm=v_mix_norm, v_mem_norm=v_mem_norm, v_w_in=v_w_in, v_forget_bias=v_forget_bias, v_w_mem_k=v_w_mem_k, v_w_mem_v=v_w_mem_v, v_fox_q_gain=v_fox_q_gain, v_fox_k_gain=v_fox_k_gain, v_swa_q_gain=v_swa_q_gain, v_swa_k_gain=v_swa_k_gain, v_swa_sinks=v_swa_sinks, v_mem_q_gain=v_mem_q_gain, v_mem_k_gain=v_mem_k_gain, v_w_out=v_w_out, v_ffn2_norm=v_ffn2_norm, v_ffn2_gate=v_ffn2_gate, v_ffn2_up=v_ffn2_up, v_ffn2_down=v_ffn2_down)
    weights = {n: given[n] for n in TWIN_WEIGHTS}
    shared = {n: given[n] for n in SHARED_INPUTS}
    per_example = {n: given[n] for n in ['x', 'mem']}
    grad_fn = _jax.value_and_grad(_loss, argnums=(0, 1))

    def one_microbatch(ex, loss_target):
        ex = dict(ex)
        diff = ex.pop(TWIN_DIFF_INPUT)
        return grad_fn(weights, diff, {**shared, **ex}, loss_target)

    if N_MICROBATCH == 1:
        loss, (grad_w, grad_x) = one_microbatch(per_example, given["loss_target"])
    else:
        def body(carry, xs):
            loss_sum, grad_sum = carry
            l_k, (gw_k, gx_k) = one_microbatch(xs[0], xs[1])
            with _jax.named_scope("update"):
                return (loss_sum + l_k, _jax.tree.map(_jnp.add, grad_sum, gw_k)), gx_k

        init = (_jnp.zeros((), _jnp.float32), _jax.tree.map(_jnp.zeros_like, weights))
        (loss, grad_w), grad_x = _jax.lax.scan(body, init, (per_example, given["loss_target"]))
    with _jax.named_scope("update"):
        delta_w, new_m, new_v = {}, {}, {}
        for n in TWIN_WEIGHTS:
            delta_w[n], new_m[n], new_v[n] = _adamw(weights[n], grad_w[n], given["m_" + n], given["v_" + n])
    return (loss, grad_x, *[grad_w[n] for n in TWIN_WEIGHTS], *[delta_w[n] for n in TWIN_WEIGHTS],
            *[new_m[n] for n in TWIN_WEIGHTS], *[new_v[n] for n in TWIN_WEIGHTS])
```

```python
import functools

import jax
import jax.numpy as jnp
from jax import lax
from jax.experimental import pallas as pl
from jax.experimental.pallas import tpu as pltpu

F32 = jnp.float32
BF16 = jnp.bfloat16
MESH = pl.DeviceIdType.MESH
ANY = pl.BlockSpec(memory_space=pl.ANY)

N_DEV = 8
EPS = 1e-6
NEG_INF = -1e30
HEAD = 128
FOX_H, SWA_H, SWA_KV, MEM_H = 6, 6, 2, 4
FOX_W, SWA_W, SWA_KV_W, MEM_W = FOX_H * HEAD, SWA_H * HEAD, SWA_KV * HEAD, MEM_H * HEAD
SCALE = HEAD ** -0.5
SWA_BLOCK = 128
C_FQ, C_FK, C_FV = 0, FOX_W, 2 * FOX_W
C_SQ = 3 * FOX_W
C_SK = C_SQ + SWA_W
C_SV = C_SK + SWA_KV_W
C_MQ = C_SV + SWA_KV_W
C_FL = C_MQ + MEM_W
IN_W = C_FL + HEAD
N_LOGIT = FOX_H
R_FQ, R_FK, R_SQ, R_SK, R_MQ, R_MK, R_FB, R_SINK = range(8)
ADAM_LR, ADAM_B1, ADAM_B2, ADAM_EPS, ADAM_WD, ADAM_STEP = 0.001, 0.9, 0.999, 1e-08, 0.01, 10
VMEM_BYTES = 56 * 1024 * 1024

DN = {
    "nn": (((1,), (0,)), ((), ())),
    "nt": (((1,), (1,)), ((), ())),
    "tn": (((0,), (0,)), ((), ())),
}


def _params(n_axes):
    return pltpu.CompilerParams(dimension_semantics=("arbitrary",) * n_axes, vmem_limit_bytes=VMEM_BYTES)


def _dot(a, b, dims="nn"):
    return lax.dot_general(a.astype(BF16), b.astype(BF16), DN[dims], preferred_element_type=F32)


def _sigmoid(x):
    return 1.0 / (1.0 + jnp.exp(-x))


def _me():
    return lax.axis_index("x"), lax.axis_index("y"), lax.axis_index("c")


def _lin(p):
    return 4 * p[0] + 2 * p[1] + p[2]


def _rows_tile(rows, row_bytes, budget=4 << 20, mult=16):
    best = None
    for k in range(1, rows + 1):
        if rows % k == 0 and (rows // k) % mult == 0 and (rows // k) * row_bytes <= budget:
            best = rows // k
            break
    assert best is not None, (rows, row_bytes)
    return best


def _mm(name, pairs, dims, grid, out_shape, out_spec, res=None, res_spec=None, alpha=1.0):
    n = len(pairs)
    nk = grid[-1]
    kax = len(grid) - 1
    acc_shape = tuple(d for d in out_spec.block_shape if d is not None)

    def body(*refs):
        pos = 2 * n
        r_ref = None
        if res is not None:
            r_ref = refs[pos]
            pos += 1
        o_ref = refs[pos]
        part = None
        for p in range(n):
            d = _dot(refs[2 * p][...], refs[2 * p + 1][...], dims)
            part = d if part is None else part + d

        def finish(acc):
            if alpha != 1.0:
                acc = acc * alpha
            if r_ref is not None:
                acc = r_ref[...] + acc
            o_ref[...] = acc.astype(o_ref.dtype)

        if nk == 1:
            finish(part)
        else:
            acc_ref = refs[pos + 1]
            k = pl.program_id(kax)

            @pl.when(k == 0)
            def _():
                acc_ref[...] = part

            @pl.when(k > 0)
            def _():
                acc_ref[...] += part

            @pl.when(k == nk - 1)
            def _():
                finish(acc_ref[...])

    operands, in_specs = [], []
    for a, a_spec, b, b_spec in pairs:
        operands += [a, b]
        in_specs += [a_spec, b_spec]
    if res is not None:
        operands.append(res)
        in_specs.append(res_spec)
    return pl.pallas_call(
        body, grid=grid, in_specs=in_specs, out_specs=out_spec, out_shape=out_shape,
        scratch_shapes=[pltpu.VMEM(acc_shape, F32)] if nk > 1 else [],
        compiler_params=_params(len(grid)), name=name,
    )(*operands)


def _cast_bf16(x, name):
    rows, cols = x.shape
    tm = _rows_tile(rows, cols * 4)

    def body(x_ref, o_ref):
        o_ref[...] = x_ref[...].astype(BF16)

    spec = pl.BlockSpec((tm, cols), lambda i: (i, 0))
    return pl.pallas_call(body, grid=(rows // tm,), in_specs=[spec], out_specs=spec,
                          out_shape=jax.ShapeDtypeStruct(x.shape, BF16), compiler_params=_params(1), name=name)(x)


def _rms_fwd(x, gain, name):
    rows, d = x.shape
    tm = min(rows, 512)

    def body(x_ref, g_ref, o_ref):
        xv = x_ref[...]
        r = lax.rsqrt(jnp.mean(xv * xv, axis=-1, keepdims=True) + EPS)
        o_ref[...] = (xv * r * g_ref[...]).astype(BF16)

    spec = pl.BlockSpec((tm, d), lambda i: (i, 0))
    return pl.pallas_call(body, grid=(rows // tm,), in_specs=[spec, pl.BlockSpec((1, d), lambda i: (0, 0))],
                          out_specs=spec, out_shape=jax.ShapeDtypeStruct(x.shape, BF16),
                          compiler_params=_params(1), name=name)(x, gain)


def _rms_bwd(x, gain, dxn, dres, name):
    rows, d = x.shape
    tm = min(rows, 256)
    with_res = dres is not None

    def body(*refs):
        if with_res:
            x_ref, g_ref, dy_ref, r_ref, dx_ref, dxb_ref, dg_ref = refs
        else:
            x_ref, g_ref, dy_ref, dx_ref, dxb_ref, dg_ref = refs
        xv = x_ref[...]
        r = lax.rsqrt(jnp.mean(xv * xv, axis=-1, keepdims=True) + EPS)
        xh = xv * r
        dy = dy_ref[...]
        dxh = dy * g_ref[...]
        dx = r * (dxh - xh * jnp.mean(dxh * xh, axis=-1, keepdims=True))
        if with_res:
            dx = dx + r_ref[...]
        dx_ref[...] = dx
        dxb_ref[...] = dx.astype(BF16)
        part = jnp.sum(dy * xh, axis=0, keepdims=True)

        @pl.when(pl.program_id(0) == 0)
        def _():
            dg_ref[...] = part

        @pl.when(pl.program_id(0) > 0)
        def _():
            dg_ref[...] += part

    spec = pl.BlockSpec((tm, d), lambda i: (i, 0))
    vec = pl.BlockSpec((1, d), lambda i: (0, 0))
    ops = [x, gain, dxn] + ([dres] if with_res else [])
    return pl.pallas_call(
        body, grid=(rows // tm,), in_specs=[spec, vec, spec] + ([spec] if with_res else []),
        out_specs=[spec, spec, vec],
        out_shape=[jax.ShapeDtypeStruct(x.shape, F32), jax.ShapeDtypeStruct(x.shape, BF16),
                   jax.ShapeDtypeStruct((1, d), F32)],
        compiler_params=_params(1), name=name)(*ops)


def _loss_head(y, target, name):
    rows, d = y.shape
    tm = min(rows, 256)

    def body(y_ref, t_ref, dy_ref, dyb_ref, acc_ref):
        err = y_ref[...] - t_ref[...]
        dy = err * (1.0 / d)
        dy_ref[...] = dy
        dyb_ref[...] = dy.astype(BF16)
        part = jnp.zeros((8, 128), F32) + jnp.sum(err * err)

        @pl.when(pl.program_id(0) == 0)
        def _():
            acc_ref[...] = part

        @pl.when(pl.program_id(0) > 0)
        def _():
            acc_ref[...] += part

    spec = pl.BlockSpec((tm, d), lambda i: (i, 0))
    return pl.pallas_call(
        body, grid=(rows // tm,), in_specs=[spec, spec],
        out_specs=[spec, spec, pl.BlockSpec((8, 128), lambda i: (0, 0))],
        out_shape=[jax.ShapeDtypeStruct(y.shape, F32), jax.ShapeDtypeStruct(y.shape, BF16),
                   jax.ShapeDtypeStruct((8, 128), F32)],
        compiler_params=_params(1), name=name)(y, target)


def _ffn_forward(x, xn, wg, wu, wd, tag):
    t, d = xn.shape
    nd, _, fs = wg.shape
    tm = min(t, 512)

    def body(x_ref, wg_ref, wu_ref, g_ref, u_ref, h_ref):
        xv = x_ref[...]
        g = _dot(xv, wg_ref[...])
        u = _dot(xv, wu_ref[...])
        g_ref[...] = g.astype(BF16)
        u_ref[...] = u.astype(BF16)
        h_ref[...] = (g * _sigmoid(g) * u).astype(BF16)

    wspec = pl.BlockSpec((None, d, fs), lambda j, i: (j, 0, 0))
    hspec = pl.BlockSpec((None, tm, fs), lambda j, i: (j, i, 0))
    hid = jax.ShapeDtypeStruct((nd, t, fs), BF16)
    g, u, h = pl.pallas_call(
        body, grid=(nd, t // tm), in_specs=[pl.BlockSpec((tm, d), lambda j, i: (i, 0)), wspec, wspec],
        out_specs=[hspec] * 3, out_shape=[hid] * 3, compiler_params=_params(2), name=f"{tag}_up")(xn, wg, wu)
    xspec = pl.BlockSpec((tm, d), lambda i, k: (i, 0))
    out = _mm(f"{tag}_down",
              [(h, pl.BlockSpec((None, tm, fs), lambda i, k: (k, i, 0)),
                wd, pl.BlockSpec((None, fs, d), lambda i, k: (k, 0, 0)))],
              "nn", (t // tm, nd), jax.ShapeDtypeStruct((t, d), F32), xspec, res=x, res_spec=xspec, alpha=0.5)
    return out, (g, u, h)


def _ffn_backward(dyb, xn, saved, wg, wu, wd, tag):
    g, u, h = saved
    t, d = xn.shape
    nd, _, fs = wg.shape
    tm = min(t, 512)

    def body(dy_ref, wd_ref, g_ref, u_ref, dg_ref, du_ref):
        dh = 0.5 * _dot(dy_ref[...], wd_ref[...], "nt")
        gv = g_ref[...].astype(F32)
        uv = u_ref[...].astype(F32)
        sig = _sigmoid(gv)
        du_ref[...] = (dh * gv * sig).astype(BF16)
        dg_ref[...] = (dh * uv * sig * (1.0 + gv * (1.0 - sig))).astype(BF16)

    hspec = pl.BlockSpec((None, tm, fs), lambda j, i: (j, i, 0))
    hid = jax.ShapeDtypeStruct((nd, t, fs), BF16)
    dg, du = pl.pallas_call(
        body, grid=(nd, t // tm),
        in_specs=[pl.BlockSpec((tm, d), lambda j, i: (i, 0)), pl.BlockSpec((None, fs, d), lambda j, i: (j, 0, 0)),
                  hspec, hspec],
        out_specs=[hspec] * 2, out_shape=[hid] * 2, compiler_params=_params(2), name=f"{tag}_dact")(dyb, wd, g, u)

    tk = min(t, 512)
    hk = pl.BlockSpec((None, tk, fs), lambda j, k: (j, k, 0))
    xk = pl.BlockSpec((tk, d), lambda j, k: (k, 0))
    d_wd = _mm(f"{tag}_dwd", [(h, hk, dyb, xk)], "tn", (nd, t // tk),
               jax.ShapeDtypeStruct((nd, fs, d), BF16), pl.BlockSpec((None, fs, d), lambda j, k: (j, 0, 0)), alpha=0.5)
    wshape = jax.ShapeDtypeStruct((nd, d, fs), BF16)
    wout = pl.BlockSpec((None, d, fs), lambda j, k: (j, 0, 0))
    d_wg = _mm(f"{tag}_dwg", [(xn, xk, dg, hk)], "tn", (nd, t // tk), wshape, wout)
    d_wu = _mm(f"{tag}_dwu", [(xn, xk, du, hk)], "tn", (nd, t // tk), wshape, wout)
    am = pl.BlockSpec((None, tm, fs), lambda i, k: (k, i, 0))
    wk = pl.BlockSpec((None, d, fs), lambda i, k: (k, 0, 0))
    dxn = _mm(f"{tag}_dxn", [(dg, am, wg, wk), (du, am, wu, wk)], "nt", (t // tm, nd),
              jax.ShapeDtypeStruct((t, d), F32), pl.BlockSpec((tm, d), lambda i, k: (i, 0)))
    return dxn, d_wg, d_wu, d_wd


def _head_norm(x, gain):
    r = lax.rsqrt(jnp.mean(x * x, axis=-1, keepdims=True) + EPS)
    return x * r * gain


def _head_norm_bwd(x, gain, dy):
    r = lax.rsqrt(jnp.mean(x * x, axis=-1, keepdims=True) + EPS)
    xh = x * r
    dxh = dy * gain
    dx = r * (dxh - xh * jnp.mean(dxh * xh, axis=-1, keepdims=True))
    return dx, jnp.sum(dy * xh, axis=0, keepdims=True)


def _hs(h, base=0):
    return slice(base + h * HEAD, base + (h + 1) * HEAD)


def _tri(n, lower):
    r = lax.broadcasted_iota(jnp.int32, (n, n), 0)
    c = lax.broadcasted_iota(jnp.int32, (n, n), 1)
    return ((r >= c) if lower else (r <= c)).astype(F32)


def _attn_pre(proj, sp, name):
    t = proj.shape[0]
    tm = min(t, 256)

    def body(p_ref, sp_ref, fq, fk, fv, sq, sk, sv, mq, cc, carry):
        @pl.when(pl.program_id(0) == 0)
        def _():
            carry[...] = jnp.zeros_like(carry)

        for h in range(FOX_H):
            fq[:, _hs(h)] = _head_norm(p_ref[:, _hs(h, C_FQ)], sp_ref[R_FQ:R_FQ + 1, :]).astype(BF16)
            fk[:, _hs(h)] = _head_norm(p_ref[:, _hs(h, C_FK)], sp_ref[R_FK:R_FK + 1, :]).astype(BF16)
        fv[...] = p_ref[:, C_FV:C_FV + FOX_W].astype(BF16)
        for h in range(SWA_H):
            sq[:, _hs(h)] = _head_norm(p_ref[:, _hs(h, C_SQ)], sp_ref[R_SQ:R_SQ + 1, :]).astype(BF16)
        for h in range(SWA_KV):
            sk[:, _hs(h)] = _head_norm(p_ref[:, _hs(h, C_SK)], sp_ref[R_SK:R_SK + 1, :]).astype(BF16)
        sv[...] = p_ref[:, C_SV:C_SV + SWA_KV_W].astype(BF16)
        for h in range(MEM_H):
            mq[:, _hs(h)] = _head_norm(p_ref[:, _hs(h, C_MQ)], sp_ref[R_MQ:R_MQ + 1, :]).astype(BF16)
        z = p_ref[:, C_FL:C_FL + HEAD] + sp_ref[R_FB:R_FB + 1, :]
        lane = lax.broadcasted_iota(jnp.int32, z.shape, 1)
        log_f = jnp.minimum(z, 0.0) - jnp.log(1.0 + jnp.exp(-jnp.abs(z)))
        log_f = jnp.where(lane < N_LOGIT, log_f, 0.0)
        c = jnp.dot(_tri(tm, True), log_f, precision=lax.Precision.HIGHEST, preferred_element_type=F32)
        c = c + carry[0:1, :]
        cc[...] = c
        carry[...] = jnp.broadcast_to(c[tm - 1:tm, :], carry.shape)

    def rows(w):
        return pl.BlockSpec((tm, w), lambda i: (i, 0))

    def shape(w, dt):
        return jax.ShapeDtypeStruct((t, w), dt)

    widths = [FOX_W, FOX_W, FOX_W, SWA_W, SWA_KV_W, SWA_KV_W, MEM_W]
    return pl.pallas_call(
        body, grid=(t // tm,), in_specs=[rows(IN_W), pl.BlockSpec((16, 128), lambda i: (0, 0))],
        out_specs=[rows(w) for w in widths] + [rows(HEAD)],
        out_shape=[shape(w, BF16) for w in widths] + [shape(HEAD, F32)],
        scratch_shapes=[pltpu.VMEM((8, 128), F32)], compiler_params=_params(1), name=name)(proj, sp)


def _attn_post_bwd(proj, sp, dfq, dfk, dfv, dsq, dsk, dsv, dmq, dc_col, dc_row_t, name):
    t = proj.shape[0]
    tm = min(t, 256)
    nb = t // tm

    def body(p_ref, sp_ref, dfq_r, dfk_r, dfv_r, dsq_r, dsk_r, dsv_r, dmq_r, dcc_r, dcr_r, dp_ref, dsp_ref, carry):
        @pl.when(pl.program_id(0) == 0)
        def _():
            carry[...] = jnp.zeros_like(carry)
            dsp_ref[...] = jnp.zeros_like(dsp_ref)

        def group(n_heads, col, row, d_ref):
            total = None
            for h in range(n_heads):
                dx, dg = _head_norm_bwd(p_ref[:, _hs(h, col)], sp_ref[row:row + 1, :], d_ref[:, _hs(h)])
                dp_ref[:, _hs(h, col)] = dx.astype(BF16)
                total = dg if total is None else total + dg
            dsp_ref[row:row + 1, :] += total

        group(FOX_H, C_FQ, R_FQ, dfq_r)
        group(FOX_H, C_FK, R_FK, dfk_r)
        dp_ref[:, C_FV:C_FV + FOX_W] = dfv_r[...].astype(BF16)
        group(SWA_H, C_SQ, R_SQ, dsq_r)
        group(SWA_KV, C_SK, R_SK, dsk_r)
        dp_ref[:, C_SV:C_SV + SWA_KV_W] = dsv_r[...].astype(BF16)
        group(MEM_H, C_MQ, R_MQ, dmq_r)
        dc = dcc_r[...] - dcr_r[...]
        rc = jnp.dot(_tri(tm, False), dc, precision=lax.Precision.HIGHEST, preferred_element_type=F32)
        rc = rc + carry[0:1, :]
        carry[...] = jnp.broadcast_to(rc[0:1, :], carry.shape)
        z = p_ref[:, C_FL:C_FL + HEAD] + sp_ref[R_FB:R_FB + 1, :]
        dz = rc * _sigmoid(-z)
        dp_ref[:, C_FL:C_FL + HEAD] = dz.astype(BF16)
        dsp_ref[R_FB:R_FB + 1, :] += jnp.sum(dz, axis=0, keepdims=True)

    def rows(w):
        return pl.BlockSpec((tm, w), lambda i: (nb - 1 - i, 0))

    small = pl.BlockSpec((16, 128), lambda i: (0, 0))
    widths = [FOX_W, FOX_W, FOX_W, SWA_W, SWA_KV_W, SWA_KV_W, MEM_W, HEAD, HEAD]
    return pl.pallas_call(
        body, grid=(nb,), in_specs=[rows(IN_W), small] + [rows(w) for w in widths],
        out_specs=[rows(IN_W), small],
        out_shape=[jax.ShapeDtypeStruct((t, IN_W), BF16), jax.ShapeDtypeStruct((16, 128), F32)],
        scratch_shapes=[pltpu.VMEM((8, 128), F32)], compiler_params=_params(1), name=name,
    )(proj, sp, dfq, dfk, dfv, dsq, dsk, dsv, dmq, dc_col, dc_row_t)


def _head_column(values):
    rows = values[0].shape[0]
    lane = lax.broadcasted_iota(jnp.int32, (rows, HEAD), 1)
    out = jnp.zeros((rows, HEAD), F32)
    for h, v in enumerate(values):
        out = jnp.where(lane == h, v, out)
    return out


def _head_row(values, n_rows=8):
    cols = values[0].shape[1]
    sub = lax.broadcasted_iota(jnp.int32, (n_rows, cols), 0)
    out = jnp.zeros((n_rows, cols), F32)
    for h, v in enumerate(values):
        out = jnp.where(sub == h, v, out)
    return out


def _delta(dmixed, o_a, o_b, o_c, name):
    t = dmixed.shape[0]
    tm = min(t, 512)

    def body(d_ref, a_ref, b_ref, c_ref, o_ref):
        cols = []
        for ref, n_heads, base in ((a_ref, FOX_H, 0), (b_ref, SWA_H, FOX_W), (c_ref, MEM_H, FOX_W + SWA_W)):
            for h in range(n_heads):
                cols.append(jnp.sum(d_ref[:, _hs(h, base)] * ref[:, _hs(h)], axis=-1, keepdims=True))
        o_ref[...] = _head_column(cols)

    def rows(w):
        return pl.BlockSpec((tm, w), lambda i: (i, 0))

    return pl.pallas_call(body, grid=(t // tm,), in_specs=[rows(dmixed.shape[1]), rows(FOX_W), rows(SWA_W), rows(MEM_W)],
                          out_specs=rows(HEAD), out_shape=jax.ShapeDtypeStruct((t, HEAD), F32),
                          compiler_params=_params(1), name=name)(dmixed, o_a, o_b, o_c)


def _fox_fwd(fq, fk, fv, c_col, c_row, name):
    t = fq.shape[0]
    tb = min(t, 512)
    nb = t // tb

    def body(q_ref, k_ref, v_ref, cc_ref, cr_ref, o_ref, lse_ref, m_s, l_s, acc_s):
        qi, ki = pl.program_id(0), pl.program_id(1)

        @pl.when(ki == 0)
        def _():
            m_s[...] = jnp.full_like(m_s, NEG_INF)
            l_s[...] = jnp.zeros_like(l_s)
            acc_s[...] = jnp.zeros_like(acc_s)

        def step(diagonal):
            if diagonal:
                r = lax.broadcasted_iota(jnp.int32, (tb, tb), 0)
                c = lax.broadcasted_iota(jnp.int32, (tb, tb), 1)
                keep = r >= c
            for h in range(FOX_H):
                s = _dot(q_ref[:, _hs(h)], k_ref[:, _hs(h)], "nt") * SCALE
                s = s + cc_ref[:, h:h + 1] - cr_ref[h:h + 1, :]
                if diagonal:
                    s = jnp.where(keep, s, NEG_INF)
                m_prev = m_s[h]
                m_new = jnp.maximum(m_prev, jnp.max(s, axis=-1, keepdims=True))
                alpha = jnp.exp(m_prev - m_new)
                p = jnp.exp(s - m_new)
                l_s[h] = alpha * l_s[h] + jnp.sum(p, axis=-1, keepdims=True)
                acc_s[:, _hs(h)] = alpha * acc_s[:, _hs(h)] + _dot(p, v_ref[:, _hs(h)])
                m_s[h] = m_new

        @pl.when(ki < qi)
        def _():
            step(False)

        @pl.when(ki == qi)
        def _():
            step(True)
            lses = []
            for h in range(FOX_H):
                o_ref[:, _hs(h)] = acc_s[:, _hs(h)] / l_s[h]
                lses.append(m_s[h] + jnp.log(l_s[h]))
            lse_ref[...] = _head_column(lses)

    qspec = pl.BlockSpec((tb, FOX_W), lambda i, j: (i, 0))
    kspec = pl.BlockSpec((tb, FOX_W), lambda i, j: (jnp.minimum(i, j), 0))
    return pl.pallas_call(
        body, grid=(nb, nb),
        in_specs=[qspec, kspec, kspec, pl.BlockSpec((tb, HEAD), lambda i, j: (i, 0)),
                  pl.BlockSpec((8, tb), lambda i, j: (0, jnp.minimum(i, j)))],
        out_specs=[qspec, pl.BlockSpec((tb, HEAD), lambda i, j: (i, 0))],
        out_shape=[jax.ShapeDtypeStruct((t, FOX_W), F32), jax.ShapeDtypeStruct((t, HEAD), F32)],
        scratch_shapes=[pltpu.VMEM((FOX_H, tb, 1), F32), pltpu.VMEM((FOX_H, tb, 1), F32), pltpu.VMEM((tb, FOX_W), F32)],
        compiler_params=_params(2), name=name)(fq, fk, fv, c_col, c_row)


def _fox_bwd(fq, fk, fv, c_col, c_row, dmixed, lse, delta, name):
    t = fq.shape[0]
    tb = min(t, 512)
    nb = t // tb

    def body(q_ref, k_ref, v_ref, cc_ref, cr_ref, do_ref, lse_ref, dl_ref,
             dq_ref, dk_ref, dv_ref, dcc_ref, dcr_ref):
        ki, qi = pl.program_id(0), pl.program_id(1)

        @pl.when((ki == 0) & (qi == 0))
        def _():
            dq_ref[...] = jnp.zeros_like(dq_ref)
            dcc_ref[...] = jnp.zeros_like(dcc_ref)

        @pl.when(qi == 0)
        def _():
            dk_ref[...] = jnp.zeros_like(dk_ref)
            dv_ref[...] = jnp.zeros_like(dv_ref)
            dcr_ref[...] = jnp.zeros_like(dcr_ref)

        def step(diagonal):
            rows = pl.ds(pl.multiple_of(qi * tb, tb), tb)
            if diagonal:
                r = lax.broadcasted_iota(jnp.int32, (tb, tb), 0)
                c = lax.broadcasted_iota(jnp.int32, (tb, tb), 1)
                keep = r >= c
            row_sums, col_sums = [], []
            for h in range(FOX_H):
                q, k, v, do = q_ref[:, _hs(h)], k_ref[:, _hs(h)], v_ref[:, _hs(h)], do_ref[:, _hs(h)]
                s = _dot(q, k, "nt") * SCALE + cc_ref[:, h:h + 1] - cr_ref[h:h + 1, :]
                if diagonal:
                    s = jnp.where(keep, s, NEG_INF)
                p = jnp.exp(s - lse_ref[:, h:h + 1])
                dp = _dot(do, v, "nt")
                ds = p * (dp - dl_ref[:, h:h + 1])
                dv_ref[:, _hs(h)] += _dot(p, do, "tn")
                dk_ref[:, _hs(h)] += _dot(ds, q, "tn") * SCALE
                dq_ref[rows, _hs(h)] += _dot(ds, k) * SCALE
                row_sums.append(jnp.sum(ds, axis=1, keepdims=True))
                col_sums.append(jnp.sum(ds, axis=0, keepdims=True))
            dcc_ref[rows, :] += _head_column(row_sums)
            dcr_ref[...] += _head_row(col_sums)

        @pl.when(qi > ki)
        def _():
            step(False)

        @pl.when(qi == ki)
        def _():
            step(True)

    def qmap(j, i):
        return (jnp.maximum(i, j), 0)

    qspec = pl.BlockSpec((tb, FOX_W), qmap)
    kspec = pl.BlockSpec((tb, FOX_W), lambda j, i: (j, 0))
    colspec = pl.BlockSpec((tb, HEAD), qmap)
    rowspec = pl.BlockSpec((8, tb), lambda j, i: (0, j))
    return pl.pallas_call(
        body, grid=(nb, nb),
        in_specs=[qspec, kspec, kspec, colspec, rowspec, qspec, colspec, colspec],
        out_specs=[pl.BlockSpec((t, FOX_W), lambda j, i: (0, 0)), kspec, kspec,
                   pl.BlockSpec((t, HEAD), lambda j, i: (0, 0)), rowspec],
        out_shape=[jax.ShapeDtypeStruct((t, FOX_W), F32)] * 3 + [jax.ShapeDtypeStruct((t, HEAD), F32),
                                                                jax.ShapeDtypeStruct((8, t), F32)],
        compiler_params=_params(2), name=name)(fq, fk, fv, c_col, c_row, dmixed, lse, delta)


def _swa_logits(q, k_cur, k_prev, slope, first_block):
    w = SWA_BLOCK
    r = lax.broadcasted_iota(jnp.int32, (w, w), 0)
    j = lax.broadcasted_iota(jnp.int32, (w, w), 1)
    dist_cur = r - j
    dist_prev = w + r - j
    s_cur = _dot(q, k_cur, "nt") * SCALE - slope * dist_cur.astype(F32)
    s_cur = jnp.where(dist_cur >= 0, s_cur, NEG_INF)
    s_prev = _dot(q, k_prev, "nt") * SCALE - slope * dist_prev.astype(F32)
    s_prev = jnp.where((j > r) & jnp.logical_not(first_block), s_prev, NEG_INF)
    return s_cur, s_prev


def _slope(h):
    return float(2.0 ** (-8.0 * (h + 1) / SWA_H))


def _swa_fwd(sq, sk, sv, sp, name):
    t = sq.shape[0]
    w = SWA_BLOCK
    nb = t // w
    group = SWA_H // SWA_KV

    def body(q_ref, kp_ref, kc_ref, vp_ref, vc_ref, sp_ref, o_ref, lse_ref):
        first = pl.program_id(0) == 0
        lses = []
        for h in range(SWA_H):
            kv = h // group
            s_cur, s_prev = _swa_logits(q_ref[:, _hs(h)], kc_ref[:, _hs(kv)], kp_ref[:, _hs(kv)], _slope(h), first)
            sink = sp_ref[R_SINK:R_SINK + 1, h:h + 1]
            m = jnp.maximum(jnp.maximum(jnp.max(s_cur, axis=-1, keepdims=True),
                                        jnp.max(s_prev, axis=-1, keepdims=True)), sink)
            p_cur = jnp.exp(s_cur - m)
            p_prev = jnp.exp(s_prev - m)
            l = jnp.sum(p_cur, axis=-1, keepdims=True) + jnp.sum(p_prev, axis=-1, keepdims=True) + jnp.exp(sink - m)
            o_ref[:, _hs(h)] = (_dot(p_cur, vc_ref[:, _hs(kv)]) + _dot(p_prev, vp_ref[:, _hs(kv)])) / l
            lses.append(m + jnp.log(l))
        lse_ref[...] = _head_column(lses)

    qspec = pl.BlockSpec((w, SWA_W), lambda n: (n, 0))
    cur = pl.BlockSpec((w, SWA_KV_W), lambda n: (n, 0))
    prev = pl.BlockSpec((w, SWA_KV_W), lambda n: (jnp.maximum(n - 1, 0), 0))
    return pl.pallas_call(
        body, grid=(nb,), in_specs=[qspec, prev, cur, prev, cur, pl.BlockSpec((16, 128), lambda n: (0, 0))],
        out_specs=[qspec, pl.BlockSpec((w, HEAD), lambda n: (n, 0))],
        out_shape=[jax.ShapeDtypeStruct((t, SWA_W), F32), jax.ShapeDtypeStruct((t, HEAD), F32)],
        compiler_params=_params(1), name=name)(sq, sk, sk, sv, sv, sp)


def _swa_bwd(sq, sk, sv, sp, dmixed, lse, delta, name):
    t = sq.shape[0]
    w = SWA_BLOCK
    nb = t // w
    group = SWA_H // SWA_KV
    do_block = FOX_W // SWA_W
    assert FOX_W % SWA_W == 0

    def body(q_ref, kp_ref, kc_ref, vp_ref, vc_ref, sp_ref, do_ref, lse_ref, dl_ref,
             dq_ref, dk_ref, dv_ref, dsp_ref, ck, cv):
        step = pl.program_id(0)
        first = step == nb - 1

        @pl.when(step == 0)
        def _():
            ck[...] = jnp.zeros_like(ck)
            cv[...] = jnp.zeros_like(cv)
            dsp_ref[...] = jnp.zeros_like(dsp_ref)

        dk_cur = [None] * SWA_KV
        dk_prev = [None] * SWA_KV
        dv_cur = [None] * SWA_KV
        dv_prev = [None] * SWA_KV
        dsinks = []

        def add(lst, i, v):
            lst[i] = v if lst[i] is None else lst[i] + v

        for h in range(SWA_H):
            kv = h // group
            q, do = q_ref[:, _hs(h)], do_ref[:, _hs(h)]
            kc, kp, vc, vp = kc_ref[:, _hs(kv)], kp_ref[:, _hs(kv)], vc_ref[:, _hs(kv)], vp_ref[:, _hs(kv)]
            s_cur, s_prev = _swa_logits(q, kc, kp, _slope(h), first)
            lse_h = lse_ref[:, h:h + 1]
            dl_h = dl_ref[:, FOX_H + h:FOX_H + h + 1]
            p_cur = jnp.exp(s_cur - lse_h)
            p_prev = jnp.exp(s_prev - lse_h)
            p_sink = jnp.exp(sp_ref[R_SINK:R_SINK + 1, h:h + 1] - lse_h)
            ds_cur = p_cur * (_dot(do, vc, "nt") - dl_h)
            ds_prev = p_prev * (_dot(do, vp, "nt") - dl_h)
            dq_ref[:, _hs(h)] = (_dot(ds_cur, kc) + _dot(ds_prev, kp)) * SCALE
            add(dk_cur, kv, _dot(ds_cur, q, "tn") * SCALE)
            add(dk_prev, kv, _dot(ds_prev, q, "tn") * SCALE)
            add(dv_cur, kv, _dot(p_cur, do, "tn"))
            add(dv_prev, kv, _dot(p_prev, do, "tn"))
            dsinks.append(-jnp.sum(p_sink * dl_h, axis=0, keepdims=True))
        for kv in range(SWA_KV):
            dk_ref[:, _hs(kv)] = dk_cur[kv] + ck[:, _hs(kv)]
            dv_ref[:, _hs(kv)] = dv_cur[kv] + cv[:, _hs(kv)]
            ck[:, _hs(kv)] = dk_prev[kv]
            cv[:, _hs(kv)] = dv_prev[kv]
        lane = lax.broadcasted_iota(jnp.int32, (1, HEAD), 1)
        row = jnp.zeros((1, HEAD), F32)
        for h in range(SWA_H):
            row = jnp.where(lane == h, dsinks[h], row)
        dsp_ref[R_SINK:R_SINK + 1, :] += row

    def rev(n):
        return nb - 1 - n

    qspec = pl.BlockSpec((w, SWA_W), lambda n: (rev(n), 0))
    cur = pl.BlockSpec((w, SWA_KV_W), lambda n: (rev(n), 0))
    prev = pl.BlockSpec((w, SWA_KV_W), lambda n: (jnp.maximum(rev(n) - 1, 0), 0))
    col = pl.BlockSpec((w, HEAD), lambda n: (rev(n), 0))
    small = pl.BlockSpec((16, 128), lambda n: (0, 0))
    return pl.pallas_call(
        body, grid=(nb,),
        in_specs=[qspec, prev, cur, prev, cur, small, pl.BlockSpec((w, SWA_W), lambda n: (rev(n), do_block)), col, col],
        out_specs=[qspec, cur, cur, small],
        out_shape=[jax.ShapeDtypeStruct((t, SWA_W), F32), jax.ShapeDtypeStruct((t, SWA_KV_W), F32),
                   jax.ShapeDtypeStruct((t, SWA_KV_W), F32), jax.ShapeDtypeStruct((16, 128), F32)],
        scratch_shapes=[pltpu.VMEM((w, SWA_KV_W), F32), pltpu.VMEM((w, SWA_KV_W), F32)],
        compiler_params=_params(1), name=name)(sq, sk, sk, sv, sv, sp, dmixed, lse, delta)


def _mem_pre(mkv, sp, name):
    m = mkv.shape[0]

    def body(x_ref, sp_ref, k_ref, v_ref):
        for h in range(MEM_H):
            k_ref[:, _hs(h)] = _head_norm(x_ref[:, _hs(h)], sp_ref[R_MK:R_MK + 1, :]).astype(BF16)
        v_ref[...] = x_ref[:, MEM_W:2 * MEM_W].astype(BF16)

    out = jax.ShapeDtypeStruct((m, MEM_W), BF16)
    return pl.pallas_call(body, out_shape=[out, out], name=name)(mkv, sp)


def _mem_post_bwd(mkv, sp, dmk, dmv, name):
    m = mkv.shape[0]

    def body(x_ref, sp_ref, dk_ref, dv_ref, d_ref, dsp_ref):
        dsp_ref[...] = jnp.zeros_like(dsp_ref)
        total = None
        for h in range(MEM_H):
            dx, dg = _head_norm_bwd(x_ref[:, _hs(h)], sp_ref[R_MK:R_MK + 1, :], dk_ref[:, _hs(h)])
            d_ref[:, _hs(h)] = dx.astype(BF16)
            total = dg if total is None else total + dg
        d_ref[:, MEM_W:2 * MEM_W] = dv_ref[...].astype(BF16)
        dsp_ref[R_MK:R_MK + 1, :] = total

    return pl.pallas_call(body, out_shape=[jax.ShapeDtypeStruct((m, 2 * MEM_W), BF16),
                                           jax.ShapeDtypeStruct((16, 128), F32)], name=name)(mkv, sp, dmk, dmv)


def _mem_fwd(mq, mk, mv, name):
    t = mq.shape[0]
    m = mk.shape[0]
    tq = min(t, 512)

    def body(q_ref, k_ref, v_ref, o_ref, lse_ref):
        lses = []
        for h in range(MEM_H):
            s = _dot(q_ref[:, _hs(h)], k_ref[:, _hs(h)], "nt") * SCALE
            mx = jnp.max(s, axis=-1, keepdims=True)
            p = jnp.exp(s - mx)
            l = jnp.sum(p, axis=-1, keepdims=True)
            o_ref[:, _hs(h)] = _dot(p, v_ref[:, _hs(h)]) / l
            lses.append(mx + jnp.log(l))
        lse_ref[...] = _head_column(lses)

    qspec = pl.BlockSpec((tq, MEM_W), lambda i: (i, 0))
    kspec = pl.BlockSpec((m, MEM_W), lambda i: (0, 0))
    return pl.pallas_call(
        body, grid=(t // tq,), in_specs=[qspec, kspec, kspec],
        out_specs=[qspec, pl.BlockSpec((tq, HEAD), lambda i: (i, 0))],
        out_shape=[jax.ShapeDtypeStruct((t, MEM_W), F32), jax.ShapeDtypeStruct((t, HEAD), F32)],
        compiler_params=_params(1), name=name)(mq, mk, mv)


def _mem_bwd(mq, mk, mv, dmixed, lse, delta, name):
    t = mq.shape[0]
    m = mk.shape[0]
    tq = min(t, 512)
    do_block = (FOX_W + SWA_W) // MEM_W
    assert (FOX_W + SWA_W) % MEM_W == 0

    def body(q_ref, k_ref, v_ref, do_ref, lse_ref, dl_ref, dq_ref, dk_ref, dv_ref):
        @pl.when(pl.program_id(0) == 0)
        def _():
            dk_ref[...] = jnp.zeros_like(dk_ref)
            dv_ref[...] = jnp.zeros_like(dv_ref)

        for h in range(MEM_H):
            q, k, v, do = q_ref[:, _hs(h)], k_ref[:, _hs(h)], v_ref[:, _hs(h)], do_ref[:, _hs(h)]
            s = _dot(q, k, "nt") * SCALE
            p = jnp.exp(s - lse_ref[:, h:h + 1])
            col = FOX_H + SWA_H + h
            ds = p * (_dot(do, v, "nt") - dl_ref[:, col:col + 1])
            dq_ref[:, _hs(h)] = _dot(ds, k) * SCALE
            dk_ref[:, _hs(h)] += _dot(ds, q, "tn") * SCALE
            dv_ref[:, _hs(h)] += _dot(p, do, "tn")

    qspec = pl.BlockSpec((tq, MEM_W), lambda i: (i, 0))
    kspec = pl.BlockSpec((m, MEM_W), lambda i: (0, 0))
    col = pl.BlockSpec((tq, HEAD), lambda i: (i, 0))
    return pl.pallas_call(
        body, grid=(t // tq,),
        in_specs=[qspec, kspec, kspec, pl.BlockSpec((tq, MEM_W), lambda i: (i, do_block)), col, col],
        out_specs=[qspec, kspec, kspec],
        out_shape=[jax.ShapeDtypeStruct((t, MEM_W), F32), jax.ShapeDtypeStruct((m, MEM_W), F32),
                   jax.ShapeDtypeStruct((m, MEM_W), F32)],
        compiler_params=_params(1), name=name)(mq, mk, mv, dmixed, lse, delta)


def _all_gather(xs, name):
    n = len(xs)

    def body(*refs):
        x_refs, o_refs = refs[:n], refs[n:2 * n]
        send_sems, recv_sems, local_sems = refs[2 * n:]
        x, y, c = _me()
        me, sibling = (x, y, c), (x, y, 1 - c)
        chips = [(1 - x, y), (x, 1 - y), (1 - x, 1 - y)]

        def copy(a, k, block, to, src=None):
            slot = o_refs[a].at[_lin(block)]
            return pltpu.make_async_remote_copy(
                src_ref=slot if src is None else src, dst_ref=slot, send_sem=send_sems.at[a, k],
                recv_sem=recv_sems.at[a, k], device_id=to, device_id_type=MESH)

        mine = [pltpu.make_async_copy(x_refs[a], o_refs[a].at[_lin(me)], local_sems.at[a]) for a in range(n)]
        for cp in mine:
            cp.start()
        first = []
        for a in range(n):
            first.append(copy(a, 0, me, sibling, src=x_refs[a]))
            first += [copy(a, 1 + j, me, (*chip, c), src=x_refs[a]) for j, chip in enumerate(chips)]
        for cp in first:
            cp.start()
        passed = []
        for j, chip in enumerate(chips):
            for a in range(n):
                copy(a, 1 + j, (*chip, c), me).wait_recv()
                cp = copy(a, 4 + j, (*chip, c), sibling)
                cp.start()
                passed.append(cp)
        for a in range(n):
            copy(a, 0, sibling, me).wait_recv()
            for j, chip in enumerate(chips):
                copy(a, 4 + j, (*chip, 1 - c), me).wait_recv()
        for cp in first + passed:
            cp.wait_send()
        for cp in mine:
            cp.wait()

    return pl.pallas_call(
        body, in_specs=[ANY] * n, out_specs=[ANY] * n,
        out_shape=[jax.ShapeDtypeStruct((N_DEV,) + x.shape, x.dtype) for x in xs],
        scratch_shapes=[pltpu.SemaphoreType.DMA((n, 7)), pltpu.SemaphoreType.DMA((n, 7)),
                        pltpu.SemaphoreType.DMA((n,))],
        name=name)(*xs)


def _peers():
    x, y, c = _me()
    out = []
    for k in range(1, N_DEV):
        kx, ky, kc = (k >> 2) & 1, (k >> 1) & 1, k & 1
        out.append(((1 - x) if kx else x, (1 - y) if ky else y, (1 - c) if kc else c))
    return out


def _scatter_partials(parts, name):
    n = len(parts)

    def body(*refs):
        p_refs, o_refs = refs[:n], refs[n:2 * n]
        send_sems, recv_sems, local_sems = refs[2 * n:]
        me = _lin(_me())
        peers = _peers()
        mine = [pltpu.make_async_copy(p_refs[a].at[me], o_refs[a].at[me], local_sems.at[a]) for a in range(n)]
        for cp in mine:
            cp.start()
        sends = []
        for a in range(n):
            for k, peer in enumerate(peers):
                sends.append(pltpu.make_async_remote_copy(
                    src_ref=p_refs[a].at[_lin(peer)], dst_ref=o_refs[a].at[me], send_sem=send_sems.at[a, k],
                    recv_sem=recv_sems.at[a, k], device_id=peer, device_id_type=MESH))
        for cp in sends:
            cp.start()
        for a in range(n):
            for k, peer in enumerate(peers):
                pltpu.make_async_remote_copy(
                    src_ref=p_refs[a].at[me], dst_ref=o_refs[a].at[_lin(peer)], send_sem=send_sems.at[a, k],
                    recv_sem=recv_sems.at[a, k], device_id=peer, device_id_type=MESH).wait_recv()
        for cp in sends:
            cp.wait_send()
        for cp in mine:
            cp.wait()

    return pl.pallas_call(
        body, in_specs=[ANY] * n, out_specs=[ANY] * n,
        out_shape=[jax.ShapeDtypeStruct(p.shape, p.dtype) for p in parts],
        scratch_shapes=[pltpu.SemaphoreType.DMA((n, 7)), pltpu.SemaphoreType.DMA((n, 7)),
                        pltpu.SemaphoreType.DMA((n,))],
        name=name)(*parts)


def _all_reduce_small(xs, name):
    n = len(xs)

    def body(*refs):
        x_refs, o_refs = refs[:n], refs[n:2 * n]
        bufs = refs[2 * n:3 * n]
        send_sems, recv_sems = refs[3 * n:]
        me = _lin(_me())
        peers = _peers()
        for a in range(n):
            bufs[a][me] = x_refs[a][...]
        sends = []
        for a in range(n):
            for k, peer in enumerate(peers):
                sends.append(pltpu.make_async_remote_copy(
                    src_ref=bufs[a].at[me], dst_ref=bufs[a].at[me], send_sem=send_sems.at[a, k],
                    recv_sem=recv_sems.at[a, k], device_id=peer, device_id_type=MESH))
        for cp in sends:
            cp.start()
        for a in range(n):
            for k, peer in enumerate(peers):
                pltpu.make_async_remote_copy(
                    src_ref=bufs[a].at[me], dst_ref=bufs[a].at[_lin(peer)], send_sem=send_sems.at[a, k],
                    recv_sem=recv_sems.at[a, k], device_id=peer, device_id_type=MESH).wait_recv()
        for cp in sends:
            cp.wait_send()
        for a in range(n):
            total = bufs[a][0]
            for q in range(1, N_DEV):
                total = total + bufs[a][q]
            o_refs[a][...] = total

    vmem = pl.BlockSpec(memory_space=pltpu.VMEM)
    return pl.pallas_call(
        body, in_specs=[vmem] * n, out_specs=[vmem] * n,
        out_shape=[jax.ShapeDtypeStruct(x.shape, F32) for x in xs],
        scratch_shapes=[pltpu.VMEM((N_DEV,) + x.shape, F32) for x in xs]
        + [pltpu.SemaphoreType.DMA((n, 7)), pltpu.SemaphoreType.DMA((n, 7))],
        name=name)(*xs)


def _sum_partials(recv, name):
    _, rows, cols = recv.shape
    tm = _rows_tile(rows, cols * 2 * N_DEV)

    def body(r_ref, o_ref):
        total = r_ref[0].astype(F32)
        for q in range(1, N_DEV):
            total = total + r_ref[q].astype(F32)
        o_ref[...] = total

    return pl.pallas_call(
        body, grid=(rows // tm,), in_specs=[pl.BlockSpec((N_DEV, tm, cols), lambda i: (0, i, 0))],
        out_specs=pl.BlockSpec((tm, cols), lambda i: (i, 0)), out_shape=jax.ShapeDtypeStruct((rows, cols), F32),
        compiler_params=_params(1), name=name)(recv)


def _adamw(w, g, m, v, name):
    rows, cols = w.shape
    tm = _rows_tile(rows, cols * 4, budget=2 << 20, mult=8)

    def body(w_ref, g_ref, m_ref, v_ref, d_ref, nm_ref, nv_ref):
        gv = g_ref[...]
        nm = ADAM_B1 * m_ref[...] + (1.0 - ADAM_B1) * gv
        nv = ADAM_B2 * v_ref[...] + (1.0 - ADAM_B2) * (gv * gv)
        m_hat = nm / (1.0 - ADAM_B1 ** ADAM_STEP)
        v_hat = nv / (1.0 - ADAM_B2 ** ADAM_STEP)
        d_ref[...] = -ADAM_LR * (m_hat / (jnp.sqrt(v_hat) + ADAM_EPS) + ADAM_WD * w_ref[...])
        nm_ref[...] = nm
        nv_ref[...] = nv

    spec = pl.BlockSpec((tm, cols), lambda i: (i, 0))
    out = jax.ShapeDtypeStruct(w.shape, F32)
    return pl.pallas_call(body, grid=(rows // tm,), in_specs=[spec] * 4, out_specs=[spec] * 3,
                          out_shape=[out] * 3, compiler_params=_params(1), name=name)(w, g, m, v)


def _permute_in(w):
    logit0 = 3 * FOX_W
    pad = jnp.zeros(w.shape[:-1] + (HEAD - N_LOGIT,), w.dtype)
    return jnp.concatenate([w[..., :logit0], w[..., logit0 + N_LOGIT:], w[..., logit0:logit0 + N_LOGIT], pad], axis=-1)


def _unpermute_in(w):
    logit0 = 3 * FOX_W
    return jnp.concatenate([w[..., :logit0], w[..., C_FL:C_FL + N_LOGIT], w[..., logit0:C_FL]], axis=-1)


def _pad_row(v, width):
    return jnp.pad(v, ((0, 0), (0, width - v.shape[1])))


def _pack_small(fq, fk, sq, sk, mq, mk, fb, sinks):
    rows = [fq, fk, sq, sk, mq, mk, _pad_row(fb, HEAD), _pad_row(sinks, HEAD)]
    return jnp.concatenate(rows + [jnp.zeros((8, HEAD), F32)], axis=0)


def _pack_norms(a, b, c, d):
    return jnp.concatenate([a, b, c, d, jnp.zeros((4, a.shape[1]), F32)], axis=0)


def kernel(x, mem, ffn1_norm, ffn1_gate, ffn1_up, ffn1_down, mix_norm, mem_norm, w_in, forget_bias, w_mem_k, w_mem_v, fox_q_gain, fox_k_gain, swa_q_gain, swa_k_gain, swa_sinks, mem_q_gain, mem_k_gain, w_out, ffn2_norm, ffn2_gate, ffn2_up, ffn2_down, loss_target, m_ffn1_norm, m_ffn1_gate, m_ffn1_up, m_ffn1_down, m_mix_norm, m_mem_norm, m_w_in, m_forget_bias, m_w_mem_k, m_w_mem_v, m_fox_q_gain, m_fox_k_gain, m_swa_q_gain, m_swa_k_gain, m_swa_sinks, m_mem_q_gain, m_mem_k_gain, m_w_out, m_ffn2_norm, m_ffn2_gate, m_ffn2_up, m_ffn2_down, v_ffn1_norm, v_ffn1_gate, v_ffn1_up, v_ffn1_down, v_mix_norm, v_mem_norm, v_w_in, v_forget_bias, v_w_mem_k, v_w_mem_v, v_fox_q_gain, v_fox_k_gain, v_swa_q_gain, v_swa_k_gain, v_swa_sinks, v_mem_q_gain, v_mem_k_gain, v_w_out, v_ffn2_norm, v_ffn2_gate, v_ffn2_up, v_ffn2_down):
    x0 = x[0]
    mem0 = mem[0]
    target = loss_target[0]
    t, d = x0.shape
    d_shard = w_in.shape[1]

    local = {
        "g1": ffn1_gate[0], "u1": ffn1_up[0], "d1": ffn1_down[0],
        "g2": ffn2_gate[0], "u2": ffn2_up[0], "d2": ffn2_down[0],
        "in": _permute_in(w_in[0]), "out": w_out[0],
        "mkv": jnp.concatenate([w_mem_k[0], w_mem_v[0]], axis=1),
    }
    names = list(local)
    gathered = _all_gather([_cast_bf16(local[k], f"cast_{k}") for k in names], "gather_weights")
    wt = dict(zip(names, gathered))
    w_in_full = wt["in"].reshape(d, IN_W)
    w_out_full = wt["out"].reshape(d, d)
    w_mkv_full = wt["mkv"].reshape(d, 2 * MEM_W)

    sp = _pack_small(fox_q_gain, fox_k_gain, swa_q_gain, swa_k_gain, mem_q_gain, mem_k_gain, forget_bias, swa_sinks)

    xn1 = _rms_fwd(x0, ffn1_norm, "ffn1_norm")
    x1, saved1 = _ffn_forward(x0, xn1, wt["g1"], wt["u1"], wt["d1"], "ffn1")

    hn = _rms_fwd(x1, mix_norm, "mix_norm")
    tm = min(t, 512)
    tn = IN_W // 3
    proj = _mm("proj", [(hn, pl.BlockSpec((tm, d), lambda n, i, k: (i, 0)),
                         w_in_full, pl.BlockSpec((d, tn), lambda n, i, k: (0, n)))],
               "nn", (3, t // tm, 1), jax.ShapeDtypeStruct((t, IN_W), F32),
               pl.BlockSpec((tm, tn), lambda n, i, k: (i, n)))
    fq, fk, fv, sq, sk, sv, mq, c_col = _attn_pre(proj, sp, "attn_pre")
    c_row = jnp.transpose(c_col[:, :8])

    mn = _rms_fwd(mem0, mem_norm, "mem_norm")
    m_len = mem0.shape[0]
    mkv = _mm("mem_kv", [(mn, pl.BlockSpec((m_len, d), lambda k: (0, 0)),
                          w_mkv_full, pl.BlockSpec((d, 2 * MEM_W), lambda k: (0, 0)))],
              "nn", (1,), jax.ShapeDtypeStruct((m_len, 2 * MEM_W), F32),
              pl.BlockSpec((m_len, 2 * MEM_W), lambda k: (0, 0)))
    mk, mv = _mem_pre(mkv, sp, "mem_pre")

    o_a, lse_a = _fox_fwd(fq, fk, fv, c_col, c_row, "fox_fwd")
    o_b, lse_b = _swa_fwd(sq, sk, sv, sp, "swa_fwd")
    o_c, lse_c = _mem_fwd(mq, mk, mv, "mem_fwd")

    def rows_spec(width):
        return pl.BlockSpec((tm, width), lambda i, k: (i, 0))

    def wout_rows(first, width):
        assert first % width == 0
        return pl.BlockSpec((width, d), lambda i, k: (first // width, 0))

    xspec = pl.BlockSpec((tm, d), lambda i, k: (i, 0))
    x2 = _mm("mix_out",
             [(o_a, rows_spec(FOX_W), w_out_full, wout_rows(0, FOX_W)),
              (o_b, rows_spec(SWA_W), w_out_full, wout_rows(FOX_W, SWA_W)),
              (o_c, rows_spec(MEM_W), w_out_full, wout_rows(FOX_W + SWA_W, MEM_W))],
             "nn", (t // tm, 1), jax.ShapeDtypeStruct((t, d), F32), xspec, res=x1, res_spec=xspec)

    xn2 = _rms_fwd(x2, ffn2_norm, "ffn2_norm")
    x3, saved2 = _ffn_forward(x2, xn2, wt["g2"], wt["u2"], wt["d2"], "ffn2")

    dy, dyb, sq_err = _loss_head(x3, target, "loss_head")
    loss = lax.psum(0.5 * sq_err[0, 0] / d, ("x", "y", "c"))

    dxn2, d_g2, d_u2, d_d2 = _ffn_backward(dyb, xn2, saved2, wt["g2"], wt["u2"], wt["d2"], "ffn2")
    dx2, dx2b, dgain_ffn2 = _rms_bwd(x2, ffn2_norm, dxn2, dy, "ffn2_norm_bwd")

    dmixed = _mm("mix_out_dx", [(dx2b, xspec, w_out_full, pl.BlockSpec((d, d), lambda i, k: (0, 0)))],
                 "nt", (t // tm, 1), jax.ShapeDtypeStruct((t, d), F32), xspec)
    tk = min(t, 512)

    def k_rows(width):
        return pl.BlockSpec((tk, width), lambda j, k: (k, 0))

    d_wout = [
        _mm(f"mix_out_dw{i}", [(o, k_rows(width), dx2b, k_rows(d))], "tn", (1, t // tk),
            jax.ShapeDtypeStruct((width, d), BF16), pl.BlockSpec((width, d), lambda j, k: (0, 0)))
        for i, (o, width) in enumerate(((o_a, FOX_W), (o_b, SWA_W), (o_c, MEM_W)))
    ]
    d_wout = jnp.concatenate(d_wout, axis=0).reshape(N_DEV, d_shard, d)

    delta = _delta(dmixed, o_a, o_b, o_c, "attn_delta")
    dfq, dfk, dfv, dc_col, dc_row = _fox_bwd(fq, fk, fv, c_col, c_row, dmixed, lse_a, delta, "fox_bwd")
    dsq, dsk, dsv, dsp_sink = _swa_bwd(sq, sk, sv, sp, dmixed, lse_b, delta, "swa_bwd")
    dmq, dmk, dmv = _mem_bwd(mq, mk, mv, dmixed, lse_c, delta, "mem_bwd")

    dmkv, dsp_mem = _mem_post_bwd(mkv, sp, dmk, dmv, "mem_post_bwd")
    d_wmkv = _mm("mem_kv_dw", [(mn, pl.BlockSpec((m_len, d), lambda k: (0, 0)),
                                dmkv, pl.BlockSpec((m_len, 2 * MEM_W), lambda k: (0, 0)))],
                 "tn", (1,), jax.ShapeDtypeStruct((d, 2 * MEM_W), BF16),
                 pl.BlockSpec((d, 2 * MEM_W), lambda k: (0, 0))).reshape(N_DEV, d_shard, 2 * MEM_W)
    dmn = _mm("mem_kv_dx", [(dmkv, pl.BlockSpec((m_len, 2 * MEM_W), lambda k: (0, 0)),
                             w_mkv_full, pl.BlockSpec((d, 2 * MEM_W), lambda k: (0, 0)))],
              "nt", (1,), jax.ShapeDtypeStruct((m_len, d), F32), pl.BlockSpec((m_len, d), lambda k: (0, 0)))
    _, _, dgain_mem = _rms_bwd(mem0, mem_norm, dmn, None, "mem_norm_bwd")

    dc_row_t = _pad_row(jnp.transpose(dc_row), HEAD)
    dproj, dsp_attn = _attn_post_bwd(proj, sp, dfq, dfk, dfv, dsq, dsk, dsv, dmq, dc_col, dc_row_t, "attn_post_bwd")
    dhn = _mm("proj_dx", [(dproj, pl.BlockSpec((tm, IN_W), lambda i, k: (i, 0)),
                           w_in_full, pl.BlockSpec((d, IN_W), lambda i, k: (0, 0)))],
              "nt", (t // tm, 1), jax.ShapeDtypeStruct((t, d), F32), xspec)
    d_win = _mm("proj_dw", [(hn, pl.BlockSpec((tk, d), lambda n, k: (k, 0)),
                             dproj, pl.BlockSpec((tk, tn), lambda n, k: (k, n)))],
                "tn", (3, t // tk), jax.ShapeDtypeStruct((d, IN_W), BF16),
                pl.BlockSpec((d, tn), lambda n, k: (0, n))).reshape(N_DEV, d_shard, IN_W)
    dx1, dx1b, dgain_mix = _rms_bwd(x1, mix_norm, dhn, dx2, "mix_norm_bwd")

    dxn1, d_g1, d_u1, d_d1 = _ffn_backward(dx1b, xn1, saved1, wt["g1"], wt["u1"], wt["d1"], "ffn1")
    grad_x, _, dgain_ffn1 = _rms_bwd(x0, ffn1_norm, dxn1, dx1, "ffn1_norm_bwd")

    partial = {"g1": d_g1, "u1": d_u1, "d1": d_d1, "g2": d_g2, "u2": d_u2, "d2": d_d2,
               "in": d_win, "out": d_wout, "mkv": d_wmkv}
    received = _scatter_partials([partial[k] for k in names], "scatter_grads")
    grad = {k: _sum_partials(r, f"sum_{k}") for k, r in zip(names, received)}
    grad["in"] = _unpermute_in(grad["in"])
    grad["mk"], grad["mv"] = grad["mkv"][:, :MEM_W], grad["mkv"][:, MEM_W:]

    norms_sum, small_sum = _all_reduce_small(
        [_pack_norms(dgain_ffn1, dgain_mix, dgain_mem, dgain_ffn2), dsp_attn + dsp_sink + dsp_mem], "reduce_small")

    big = {
        "ffn1_gate": (ffn1_gate, grad["g1"], m_ffn1_gate, v_ffn1_gate),
        "ffn1_up": (ffn1_up, grad["u1"], m_ffn1_up, v_ffn1_up),
        "ffn1_down": (ffn1_down, grad["d1"], m_ffn1_down, v_ffn1_down),
        "w_in": (w_in, grad["in"], m_w_in, v_w_in),
        "w_mem_k": (w_mem_k, grad["mk"], m_w_mem_k, v_w_mem_k),
        "w_mem_v": (w_mem_v, grad["mv"], m_w_mem_v, v_w_mem_v),
        "w_out": (w_out, grad["out"], m_w_out, v_w_out),
        "ffn2_gate": (ffn2_gate, grad["g2"], m_ffn2_gate, v_ffn2_gate),
        "ffn2_up": (ffn2_up, grad["u2"], m_ffn2_up, v_ffn2_up),
        "ffn2_down": (ffn2_down, grad["d2"], m_ffn2_down, v_ffn2_down),
    }
    result = {}
    for k, (w, g, m, v) in big.items():
        delta_w, new_m, new_v = _adamw(w[0], g, m[0], v[0], f"adamw_{k}")
        result[k] = (g[None], delta_w[None], new_m[None], new_v[None])

    norm_names = ["ffn1_norm", "mix_norm", "mem_norm", "ffn2_norm"]
    norm_w = _pack_norms(ffn1_norm, mix_norm, mem_norm, ffn2_norm)
    norm_m = _pack_norms(m_ffn1_norm, m_mix_norm, m_mem_norm, m_ffn2_norm)
    norm_v = _pack_norms(v_ffn1_norm, v_mix_norm, v_mem_norm, v_ffn2_norm)
    outs = (norms_sum,) + tuple(_adamw(norm_w, norms_sum, norm_m, norm_v, "adamw_norms"))
    for i, k in enumerate(norm_names):
        result[k] = tuple(o[i:i + 1] for o in outs)

    small_names = ["fox_q_gain", "fox_k_gain", "swa_q_gain", "swa_k_gain", "mem_q_gain", "mem_k_gain",
                   "forget_bias", "swa_sinks"]
    small_w = sp
    small_m = _pack_small(m_fox_q_gain, m_fox_k_gain, m_swa_q_gain, m_swa_k_gain, m_mem_q_gain, m_mem_k_gain,
                          m_forget_bias, m_swa_sinks)
    small_v = _pack_small(v_fox_q_gain, v_fox_k_gain, v_swa_q_gain, v_swa_k_gain, v_mem_q_gain, v_mem_k_gain,
                          v_forget_bias, v_swa_sinks)
    outs = (small_sum,) + tuple(_adamw(small_w, small_sum, small_m, small_v, "adamw_small"))
    for i, k in enumerate(small_names):
        width = N_LOGIT if k in ("forget_bias", "swa_sinks") else HEAD
        result[k] = tuple(o[i:i + 1, :width] for o in outs)

    order = ["ffn1_norm", "ffn1_gate", "ffn1_up", "ffn1_down", "mix_norm", "mem_norm", "w_in", "forget_bias",
             "w_mem_k", "w_mem_v", "fox_q_gain", "fox_k_gain", "swa_q_gain", "swa_k_gain", "swa_sinks",
             "mem_q_gain", "mem_k_gain", "w_out", "ffn2_norm", "ffn2_gate", "ffn2_up", "ffn2_down"]
    flat = [loss, grad_x[None]]
    for kind in range(4):
        flat += [result[k][kind] for k in order]
    return tuple(flat)
```

```python
import functools

import jax
import jax.numpy as jnp
from jax import lax
from jax.experimental import pallas as pl
from jax.experimental.pallas import tpu as pltpu

F32 = jnp.float32
BF16 = jnp.bfloat16
MESH = pl.DeviceIdType.MESH
ANY = pl.BlockSpec(memory_space=pl.ANY)

N_DEV = 8
EPS = 1e-6
NEG_INF = -1e30
HEAD = 128
FOX_H, SWA_H, SWA_KV, MEM_H = 6, 6, 2, 4
FOX_W, SWA_W, SWA_KV_W, MEM_W = FOX_H * HEAD, SWA_H * HEAD, SWA_KV * HEAD, MEM_H * HEAD
SCALE = HEAD ** -0.5
SWA_BLOCK = 128
C_FQ, C_FK, C_FV = 0, FOX_W, 2 * FOX_W
C_SQ = 3 * FOX_W
C_SK = C_SQ + SWA_W
C_SV = C_SK + SWA_KV_W
C_MQ = C_SV + SWA_KV_W
C_FL = C_MQ + MEM_W
IN_W = C_FL + HEAD
N_LOGIT = FOX_H
R_FQ, R_FK, R_SQ, R_SK, R_MQ, R_MK, R_FB, R_SINK = range(8)
ADAM_LR, ADAM_B1, ADAM_B2, ADAM_EPS, ADAM_WD, ADAM_STEP = 0.001, 0.9, 0.999, 1e-08, 0.01, 10
VMEM_BYTES = 56 * 1024 * 1024

DN = {
    "nn": (((1,), (0,)), ((), ())),
    "nt": (((1,), (1,)), ((), ())),
    "tn": (((0,), (0,)), ((), ())),
}


def _params(n_axes):
    return pltpu.CompilerParams(dimension_semantics=("arbitrary",) * n_axes, vmem_limit_bytes=VMEM_BYTES)


def _dot(a, b, dims="nn"):
    return lax.dot_general(a.astype(BF16), b.astype(BF16), DN[dims], preferred_element_type=F32)


def _sigmoid(x):
    return 1.0 / (1.0 + jnp.exp(-x))


def _me():
    return lax.axis_index("x"), lax.axis_index("y"), lax.axis_index("c")


def _lin(p):
    return 4 * p[0] + 2 * p[1] + p[2]


def _rows_tile(rows, row_bytes, budget=4 << 20, mult=16):
    best = None
    for k in range(1, rows + 1):
        if rows % k == 0 and (rows // k) % mult == 0 and (rows // k) * row_bytes <= budget:
            best = rows // k
            break
    assert best is not None, (rows, row_bytes)
    return best


class _Ride:
    def __init__(self, inputs, out_shapes, aliases, n_remote, n_local, start, wait):
        self.inputs, self.out_shapes, self.aliases = list(inputs), list(out_shapes), dict(aliases)
        self.n_remote, self.n_local, self.start, self.wait = n_remote, n_local, start, wait


def _remote(src, dst, send, recv, k, to):
    return pltpu.make_async_remote_copy(src_ref=src, dst_ref=dst, send_sem=send.at[k], recv_sem=recv.at[k],
                                        device_id=to, device_id_type=MESH)


def _other_chips(x, y):
    return [(1 - x, y), (x, 1 - y), (1 - x, 1 - y)]


ALL_CHIPS = [(0, 0), (0, 1), (1, 0), (1, 1)]


def _ride_gather_chips(xs):
    n = len(xs)

    def copies(ins, outs, send, recv):
        x, y, c = _me()
        out = []
        for a in range(n):
            for j, chip in enumerate(_other_chips(x, y)):
                peer = (*chip, c)
                out.append((_remote(ins[a], outs[a].at[_lin((x, y, c))], send, recv, 3 * a + j, peer),
                            _remote(ins[a], outs[a].at[_lin(peer)], send, recv, 3 * a + j, peer)))
        return out

    def mine(ins, outs, local):
        me = _lin(_me())
        return [pltpu.make_async_copy(ins[a], outs[a].at[me], local.at[a]) for a in range(n)]

    def start(ins, outs, send, recv, local):
        for cp in mine(ins, outs, local):
            cp.start()
        for sent, _ in copies(ins, outs, send, recv):
            sent.start()

    def wait(ins, outs, send, recv, local):
        for sent, landed in copies(ins, outs, send, recv):
            landed.wait_recv()
            sent.wait_send()
        for cp in mine(ins, outs, local):
            cp.wait()

    shapes = [jax.ShapeDtypeStruct((N_DEV,) + x.shape, x.dtype) for x in xs]
    return _Ride(xs, shapes, {}, 3 * n, n, start, wait)


def _ride_gather_sibling(bufs):
    n = len(bufs)

    def copies(outs, send, recv):
        x, y, c = _me()
        out = []
        for a in range(n):
            for q, (px, py) in enumerate(ALL_CHIPS):
                there = outs[a].at[4 * px + 2 * py + c]
                here = outs[a].at[4 * px + 2 * py + 1 - c]
                out.append((_remote(there, there, send, recv, 4 * a + q, (x, y, 1 - c)),
                            _remote(here, here, send, recv, 4 * a + q, (x, y, 1 - c))))
        return out

    def start(ins, outs, send, recv, local):
        for sent, _ in copies(outs, send, recv):
            sent.start()

    def wait(ins, outs, send, recv, local):
        for sent, landed in copies(outs, send, recv):
            landed.wait_recv()
            sent.wait_send()

    shapes = [jax.ShapeDtypeStruct(b.shape, b.dtype) for b in bufs]
    return _Ride(bufs, shapes, {a: a for a in range(n)}, 4 * n, 0, start, wait)


def _ride_scatter_sibling(parts):
    n = len(parts)

    def copies(ins, outs, send, recv):
        x, y, c = _me()
        out = []
        for a in range(n):
            for q, (px, py) in enumerate(ALL_CHIPS):
                cp = _remote(ins[a].at[4 * px + 2 * py + 1 - c], outs[a].at[q], send, recv, 4 * a + q, (x, y, 1 - c))
                out.append(cp)
        return out

    def start(ins, outs, send, recv, local):
        for cp in copies(ins, outs, send, recv):
            cp.start()

    def wait(ins, outs, send, recv, local):
        for cp in copies(ins, outs, send, recv):
            cp.wait_recv()
            cp.wait_send()

    shapes = [jax.ShapeDtypeStruct((4,) + p.shape[1:], p.dtype) for p in parts]
    return _Ride(parts, shapes, {}, 4 * n, 0, start, wait)


def _ride_scatter_chips(pairs):
    n = len(pairs)

    def copies(ins, outs, send, recv):
        x, y, c = _me()
        out = []
        for a in range(n):
            for j, (px, py) in enumerate(_other_chips(x, y)):
                peer = (px, py, c)
                out.append((_remote(ins[a].at[2 * px + py], outs[a].at[2 * x + y], send, recv, 3 * a + j, peer),
                            _remote(ins[a].at[2 * px + py], outs[a].at[2 * px + py], send, recv, 3 * a + j, peer)))
        return out

    def mine(ins, outs, local):
        x, y, _ = _me()
        return [pltpu.make_async_copy(ins[a].at[2 * x + y], outs[a].at[2 * x + y], local.at[a]) for a in range(n)]

    def start(ins, outs, send, recv, local):
        for cp in mine(ins, outs, local):
            cp.start()
        for sent, _ in copies(ins, outs, send, recv):
            sent.start()

    def wait(ins, outs, send, recv, local):
        for sent, landed in copies(ins, outs, send, recv):
            landed.wait_recv()
            sent.wait_send()
        for cp in mine(ins, outs, local):
            cp.wait()

    shapes = [jax.ShapeDtypeStruct(p.shape, p.dtype) for p in pairs]
    return _Ride(pairs, shapes, {}, 3 * n, n, start, wait)


def _call(name, body, grid, in_specs, out_specs, out_shape, operands, scratch=(), rides=()):
    n_in, n_out, n_scr = len(operands), len(out_shape), len(scratch)
    ride_in, ride_out, ride_scr, aliases, spans = [], [], [], {}, []
    for r in rides:
        for i, o in r.aliases.items():
            aliases[n_in + len(ride_in) + i] = n_out + len(ride_out) + o
        spans.append((len(ride_in), len(r.inputs), len(ride_out), len(r.out_shapes)))
        ride_in += r.inputs
        ride_out += r.out_shapes
        ride_scr += [pltpu.SemaphoreType.DMA((r.n_remote,)), pltpu.SemaphoreType.DMA((r.n_remote,)),
                     pltpu.SemaphoreType.DMA((max(r.n_local, 1),))]

    def wrapped(*refs):
        c_in, r_in = refs[:n_in], refs[n_in:n_in + len(ride_in)]
        p = n_in + len(ride_in)
        c_out, r_out = refs[p:p + n_out], refs[p + n_out:p + n_out + len(ride_out)]
        p += n_out + len(ride_out)
        c_scr, r_scr = refs[p:p + n_scr], refs[p + n_scr:]

        def each(method, at):
            def run():
                for k, (r, (i0, ni, o0, no)) in enumerate(zip(rides, spans)):
                    getattr(r, method)(r_in[i0:i0 + ni], r_out[o0:o0 + no], *r_scr[3 * k:3 * k + 3])
            if not rides:
                return
            if grid:
                cond = functools.reduce(jnp.logical_and, [pl.program_id(ax) == at(n) for ax, n in enumerate(grid)])
                pl.when(cond)(run)
            else:
                run()

        each("start", lambda n: 0)
        body(*c_in, *c_out, *c_scr)
        each("wait", lambda n: n - 1)

    outs = pl.pallas_call(
        wrapped, grid=grid, in_specs=list(in_specs) + [ANY] * len(ride_in),
        out_specs=list(out_specs) + [ANY] * len(ride_out), out_shape=list(out_shape) + ride_out,
        scratch_shapes=list(scratch) + ride_scr, input_output_aliases=aliases,
        compiler_params=_params(len(grid)), name=name)(*operands, *ride_in)
    outs = list(outs)
    ride_results = [outs[n_out + o0:n_out + o0 + no] for (_, _, o0, no) in spans]
    return outs[:n_out], ride_results


def _only_copies(name, rides):
    return _call(name, lambda: None, (), [], [], [], [], rides=rides)[1]


def _mm(name, pairs, dims, grid, out_shape, out_spec, res=None, res_spec=None, alpha=1.0, rides=()):
    n = len(pairs)
    nk = grid[-1]
    kax = len(grid) - 1
    acc_shape = tuple(d for d in out_spec.block_shape if d is not None)

    def body(*refs):
        pos = 2 * n
        r_ref = None
        if res is not None:
            r_ref = refs[pos]
            pos += 1
        o_ref = refs[pos]
        part = None
        for p in range(n):
            d = _dot(refs[2 * p][...], refs[2 * p + 1][...], dims)
            part = d if part is None else part + d

        def finish(acc):
            if alpha != 1.0:
                acc = acc * alpha
            if r_ref is not None:
                acc = r_ref[...] + acc
            o_ref[...] = acc.astype(o_ref.dtype)

        if nk == 1:
            finish(part)
        else:
            acc_ref = refs[pos + 1]
            k = pl.program_id(kax)

            @pl.when(k == 0)
            def _():
                acc_ref[...] = part

            @pl.when(k > 0)
            def _():
                acc_ref[...] += part

            @pl.when(k == nk - 1)
            def _():
                finish(acc_ref[...])

    operands, in_specs = [], []
    for a, a_spec, b, b_spec in pairs:
        operands += [a, b]
        in_specs += [a_spec, b_spec]
    if res is not None:
        operands.append(res)
        in_specs.append(res_spec)
    (out,), ride_results = _call(name, body, grid, in_specs, [out_spec], [out_shape], operands,
                                 scratch=[pltpu.VMEM(acc_shape, F32)] if nk > 1 else [], rides=rides)
    return (out, ride_results) if rides else out


def _cast_bf16(x, name):
    rows, cols = x.shape
    tm = _rows_tile(rows, cols * 4)

    def body(x_ref, o_ref):
        o_ref[...] = x_ref[...].astype(BF16)

    spec = pl.BlockSpec((tm, cols), lambda i: (i, 0))
    return pl.pallas_call(body, grid=(rows // tm,), in_specs=[spec], out_specs=spec,
                          out_shape=jax.ShapeDtypeStruct(x.shape, BF16), compiler_params=_params(1), name=name)(x)


def _rms_fwd(x, gain, name):
    rows, d = x.shape
    tm = min(rows, 512)

    def body(x_ref, g_ref, o_ref):
        xv = x_ref[...]
        r = lax.rsqrt(jnp.mean(xv * xv, axis=-1, keepdims=True) + EPS)
        o_ref[...] = (xv * r * g_ref[...]).astype(BF16)

    spec = pl.BlockSpec((tm, d), lambda i: (i, 0))
    return pl.pallas_call(body, grid=(rows // tm,), in_specs=[spec, pl.BlockSpec((1, d), lambda i: (0, 0))],
                          out_specs=spec, out_shape=jax.ShapeDtypeStruct(x.shape, BF16),
                          compiler_params=_params(1), name=name)(x, gain)


def _rms_bwd(x, gain, dxn, dres, name):
    rows, d = x.shape
    tm = min(rows, 256)
    with_res = dres is not None

    def body(*refs):
        if with_res:
            x_ref, g_ref, dy_ref, r_ref, dx_ref, dxb_ref, dg_ref = refs
        else:
            x_ref, g_ref, dy_ref, dx_ref, dxb_ref, dg_ref = refs
        xv = x_ref[...]
        r = lax.rsqrt(jnp.mean(xv * xv, axis=-1, keepdims=True) + EPS)
        xh = xv * r
        dy = dy_ref[...]
        dxh = dy * g_ref[...]
        dx = r * (dxh - xh * jnp.mean(dxh * xh, axis=-1, keepdims=True))
        if with_res:
            dx = dx + r_ref[...]
        dx_ref[...] = dx
        dxb_ref[...] = dx.astype(BF16)
        part = jnp.sum(dy * xh, axis=0, keepdims=True)

        @pl.when(pl.program_id(0) == 0)
        def _():
            dg_ref[...] = part

        @pl.when(pl.program_id(0) > 0)
        def _():
            dg_ref[...] += part

    spec = pl.BlockSpec((tm, d), lambda i: (i, 0))
    vec = pl.BlockSpec((1, d), lambda i: (0, 0))
    ops = [x, gain, dxn] + ([dres] if with_res else [])
    return pl.pallas_call(
        body, grid=(rows // tm,), in_specs=[spec, vec, spec] + ([spec] if with_res else []),
        out_specs=[spec, spec, vec],
        out_shape=[jax.ShapeDtypeStruct(x.shape, F32), jax.ShapeDtypeStruct(x.shape, BF16),
                   jax.ShapeDtypeStruct((1, d), F32)],
        compiler_params=_params(1), name=name)(*ops)


def _loss_head(y, target, name):
    rows, d = y.shape
    tm = min(rows, 256)

    def body(y_ref, t_ref, dy_ref, dyb_ref, acc_ref):
        err = y_ref[...] - t_ref[...]
        dy = err * (1.0 / d)
        dy_ref[...] = dy
        dyb_ref[...] = dy.astype(BF16)
        part = jnp.zeros((8, 128), F32) + jnp.sum(err * err)

        @pl.when(pl.program_id(0) == 0)
        def _():
            acc_ref[...] = part

        @pl.when(pl.program_id(0) > 0)
        def _():
            acc_ref[...] += part

    spec = pl.BlockSpec((tm, d), lambda i: (i, 0))
    return pl.pallas_call(
        body, grid=(rows // tm,), in_specs=[spec, spec],
        out_specs=[spec, spec, pl.BlockSpec((8, 128), lambda i: (0, 0))],
        out_shape=[jax.ShapeDtypeStruct(y.shape, F32), jax.ShapeDtypeStruct(y.shape, BF16),
                   jax.ShapeDtypeStruct((8, 128), F32)],
        compiler_params=_params(1), name=name)(y, target)


def _ffn_up(xn, wg, wu, tag, rides=()):
    t, d = xn.shape
    nd, _, fs = wg.shape
    tm = min(t, 512)

    def body(x_ref, wg_ref, wu_ref, g_ref, u_ref, h_ref):
        xv = x_ref[...]
        g = _dot(xv, wg_ref[...])
        u = _dot(xv, wu_ref[...])
        g_ref[...] = g.astype(BF16)
        u_ref[...] = u.astype(BF16)
        h_ref[...] = (g * _sigmoid(g) * u).astype(BF16)

    wspec = pl.BlockSpec((None, d, fs), lambda j, i: (j, 0, 0))
    hspec = pl.BlockSpec((None, tm, fs), lambda j, i: (j, i, 0))
    hid = jax.ShapeDtypeStruct((nd, t, fs), BF16)
    return _call(f"{tag}_up", body, (nd, t // tm), [pl.BlockSpec((tm, d), lambda j, i: (i, 0)), wspec, wspec],
                 [hspec] * 3, [hid] * 3, [xn, wg, wu], rides=rides)


def _ffn_down(x, h, wd, tag, rides=()):
    nd, t, fs = h.shape
    d = x.shape[1]
    tm = min(t, 512)
    xspec = pl.BlockSpec((tm, d), lambda i, k: (i, 0))
    return _mm(f"{tag}_down",
               [(h, pl.BlockSpec((None, tm, fs), lambda i, k: (k, i, 0)),
                 wd, pl.BlockSpec((None, fs, d), lambda i, k: (k, 0, 0)))],
               "nn", (t // tm, nd), jax.ShapeDtypeStruct((t, d), F32), xspec, res=x, res_spec=xspec, alpha=0.5,
               rides=rides)


def _ffn_dact(dyb, wd, g, u, tag, rides=()):
    nd, t, fs = g.shape
    d = dyb.shape[1]
    tm = min(t, 512)

    def body(dy_ref, wd_ref, g_ref, u_ref, dg_ref, du_ref):
        dh = 0.5 * _dot(dy_ref[...], wd_ref[...], "nt")
        gv = g_ref[...].astype(F32)
        uv = u_ref[...].astype(F32)
        sig = _sigmoid(gv)
        du_ref[...] = (dh * gv * sig).astype(BF16)
        dg_ref[...] = (dh * uv * sig * (1.0 + gv * (1.0 - sig))).astype(BF16)

    hspec = pl.BlockSpec((None, tm, fs), lambda j, i: (j, i, 0))
    hid = jax.ShapeDtypeStruct((nd, t, fs), BF16)
    return _call(f"{tag}_dact", body, (nd, t // tm),
                 [pl.BlockSpec((tm, d), lambda j, i: (i, 0)), pl.BlockSpec((None, fs, d), lambda j, i: (j, 0, 0)),
                  hspec, hspec], [hspec] * 2, [hid] * 2, [dyb, wd, g, u], rides=rides)


def _ffn_dwd(h, dyb, name, rides=()):
    nd, t, fs = h.shape
    d = dyb.shape[1]
    tk = min(t, 512)
    return _mm(name, [(h, pl.BlockSpec((None, tk, fs), lambda j, k: (j, k, 0)),
                       dyb, pl.BlockSpec((tk, d), lambda j, k: (k, 0)))], "tn", (nd, t // tk),
               jax.ShapeDtypeStruct((nd, fs, d), BF16), pl.BlockSpec((None, fs, d), lambda j, k: (j, 0, 0)), alpha=0.5,
               rides=rides)


def _ffn_dwin(xn, dh, name, rides=()):
    nd, t, fs = dh.shape
    d = xn.shape[1]
    tk = min(t, 512)
    return _mm(name, [(xn, pl.BlockSpec((tk, d), lambda j, k: (k, 0)),
                       dh, pl.BlockSpec((None, tk, fs), lambda j, k: (j, k, 0)))], "tn", (nd, t // tk),
               jax.ShapeDtypeStruct((nd, d, fs), BF16), pl.BlockSpec((None, d, fs), lambda j, k: (j, 0, 0)),
               rides=rides)


def _ffn_dxn(dg, du, wg, wu, name, rides=()):
    nd, t, fs = dg.shape
    d = wg.shape[1]
    tm = min(t, 512)
    am = pl.BlockSpec((None, tm, fs), lambda i, k: (k, i, 0))
    wk = pl.BlockSpec((None, d, fs), lambda i, k: (k, 0, 0))
    return _mm(name, [(dg, am, wg, wk), (du, am, wu, wk)], "nt", (t // tm, nd),
               jax.ShapeDtypeStruct((t, d), F32), pl.BlockSpec((tm, d), lambda i, k: (i, 0)), rides=rides)


def _head_norm(x, gain):
    r = lax.rsqrt(jnp.mean(x * x, axis=-1, keepdims=True) + EPS)
    return x * r * gain


def _head_norm_bwd(x, gain, dy):
    r = lax.rsqrt(jnp.mean(x * x, axis=-1, keepdims=True) + EPS)
    xh = x * r
    dxh = dy * gain
    dx = r * (dxh - xh * jnp.mean(dxh * xh, axis=-1, keepdims=True))
    return dx, jnp.sum(dy * xh, axis=0, keepdims=True)


def _hs(h, base=0):
    return slice(base + h * HEAD, base + (h + 1) * HEAD)


def _tri(n, lower):
    r = lax.broadcasted_iota(jnp.int32, (n, n), 0)
    c = lax.broadcasted_iota(jnp.int32, (n, n), 1)
    return ((r >= c) if lower else (r <= c)).astype(F32)


def _attn_pre(proj, sp, name):
    t = proj.shape[0]
    tm = min(t, 256)

    def body(p_ref, sp_ref, fq, fk, fv, sq, sk, sv, mq, cc, carry):
        @pl.when(pl.program_id(0) == 0)
        def _():
            carry[...] = jnp.zeros_like(carry)

        for h in range(FOX_H):
            fq[:, _hs(h)] = _head_norm(p_ref[:, _hs(h, C_FQ)], sp_ref[R_FQ:R_FQ + 1, :]).astype(BF16)
            fk[:, _hs(h)] = _head_norm(p_ref[:, _hs(h, C_FK)], sp_ref[R_FK:R_FK + 1, :]).astype(BF16)
        fv[...] = p_ref[:, C_FV:C_FV + FOX_W].astype(BF16)
        for h in range(SWA_H):
            sq[:, _hs(h)] = _head_norm(p_ref[:, _hs(h, C_SQ)], sp_ref[R_SQ:R_SQ + 1, :]).astype(BF16)
        for h in range(SWA_KV):
            sk[:, _hs(h)] = _head_norm(p_ref[:, _hs(h, C_SK)], sp_ref[R_SK:R_SK + 1, :]).astype(BF16)
        sv[...] = p_ref[:, C_SV:C_SV + SWA_KV_W].astype(BF16)
        for h in range(MEM_H):
            mq[:, _hs(h)] = _head_norm(p_ref[:, _hs(h, C_MQ)], sp_ref[R_MQ:R_MQ + 1, :]).astype(BF16)
        z = p_ref[:, C_FL:C_FL + HEAD] + sp_ref[R_FB:R_FB + 1, :]
        lane = lax.broadcasted_iota(jnp.int32, z.shape, 1)
        log_f = jnp.minimum(z, 0.0) - jnp.log(1.0 + jnp.exp(-jnp.abs(z)))
        log_f = jnp.where(lane < N_LOGIT, log_f, 0.0)
        c = jnp.dot(_tri(tm, True), log_f, precision=lax.Precision.HIGHEST, preferred_element_type=F32)
        c = c + carry[0:1, :]
        cc[...] = c
        carry[...] = jnp.broadcast_to(c[tm - 1:tm, :], carry.shape)

    def rows(w):
        return pl.BlockSpec((tm, w), lambda i: (i, 0))

    def shape(w, dt):
        return jax.ShapeDtypeStruct((t, w), dt)

    widths = [FOX_W, FOX_W, FOX_W, SWA_W, SWA_KV_W, SWA_KV_W, MEM_W]
    return pl.pallas_call(
        body, grid=(t // tm,), in_specs=[rows(IN_W), pl.BlockSpec((16, 128), lambda i: (0, 0))],
        out_specs=[rows(w) for w in widths] + [rows(HEAD)],
        out_shape=[shape(w, BF16) for w in widths] + [shape(HEAD, F32)],
        scratch_shapes=[pltpu.VMEM((8, 128), F32)], compiler_params=_params(1), name=name)(proj, sp)


def _attn_post_bwd(proj, sp, dfq, dfk, dfv, dsq, dsk, dsv, dmq, dc_col, dc_row_t, name):
    t = proj.shape[0]
    tm = min(t, 256)
    nb = t // tm

    def body(p_ref, sp_ref, dfq_r, dfk_r, dfv_r, dsq_r, dsk_r, dsv_r, dmq_r, dcc_r, dcr_r, dp_ref, dsp_ref, carry):
        @pl.when(pl.program_id(0) == 0)
        def _():
            carry[...] = jnp.zeros_like(carry)
            dsp_ref[...] = jnp.zeros_like(dsp_ref)

        def group(n_heads, col, row, d_ref):
            total = None
            for h in range(n_heads):
                dx, dg = _head_norm_bwd(p_ref[:, _hs(h, col)], sp_ref[row:row + 1, :], d_ref[:, _hs(h)])
                dp_ref[:, _hs(h, col)] = dx.astype(BF16)
                total = dg if total is None else total + dg
            dsp_ref[row:row + 1, :] += total

        group(FOX_H, C_FQ, R_FQ, dfq_r)
        group(FOX_H, C_FK, R_FK, dfk_r)
        dp_ref[:, C_FV:C_FV + FOX_W] = dfv_r[...].astype(BF16)
        group(SWA_H, C_SQ, R_SQ, dsq_r)
        group(SWA_KV, C_SK, R_SK, dsk_r)
        dp_ref[:, C_SV:C_SV + SWA_KV_W] = dsv_r[...].astype(BF16)
        group(MEM_H, C_MQ, R_MQ, dmq_r)
        dc = dcc_r[...] - dcr_r[...]
        rc = jnp.dot(_tri(tm, False), dc, precision=lax.Precision.HIGHEST, preferred_element_type=F32)
        rc = rc + carry[0:1, :]
        carry[...] = jnp.broadcast_to(rc[0:1, :], carry.shape)
        z = p_ref[:, C_FL:C_FL + HEAD] + sp_ref[R_FB:R_FB + 1, :]
        dz = rc * _sigmoid(-z)
        dp_ref[:, C_FL:C_FL + HEAD] = dz.astype(BF16)
        dsp_ref[R_FB:R_FB + 1, :] += jnp.sum(dz, axis=0, keepdims=True)

    def rows(w):
        return pl.BlockSpec((tm, w), lambda i: (nb - 1 - i, 0))

    small = pl.BlockSpec((16, 128), lambda i: (0, 0))
    widths = [FOX_W, FOX_W, FOX_W, SWA_W, SWA_KV_W, SWA_KV_W, MEM_W, HEAD, HEAD]
    return pl.pallas_call(
        body, grid=(nb,), in_specs=[rows(IN_W), small] + [rows(w) for w in widths],
        out_specs=[rows(IN_W), small],
        out_shape=[jax.ShapeDtypeStruct((t, IN_W), BF16), jax.ShapeDtypeStruct((16, 128), F32)],
        scratch_shapes=[pltpu.VMEM((8, 128), F32)], compiler_params=_params(1), name=name,
    )(proj, sp, dfq, dfk, dfv, dsq, dsk, dsv, dmq, dc_col, dc_row_t)


def _head_column(values):
    rows = values[0].shape[0]
    lane = lax.broadcasted_iota(jnp.int32, (rows, HEAD), 1)
    out = jnp.zeros((rows, HEAD), F32)
    for h, v in enumerate(values):
        out = jnp.where(lane == h, v, out)
    return out


def _head_row(values, n_rows=8):
    cols = values[0].shape[1]
    sub = lax.broadcasted_iota(jnp.int32, (n_rows, cols), 0)
    out = jnp.zeros((n_rows, cols), F32)
    for h, v in enumerate(values):
        out = jnp.where(sub == h, v, out)
    return out


def _delta(dmixed, o_a, o_b, o_c, name):
    t = dmixed.shape[0]
    tm = min(t, 512)

    def body(d_ref, a_ref, b_ref, c_ref, o_ref):
        cols = []
        for ref, n_heads, base in ((a_ref, FOX_H, 0), (b_ref, SWA_H, FOX_W), (c_ref, MEM_H, FOX_W + SWA_W)):
            for h in range(n_heads):
                cols.append(jnp.sum(d_ref[:, _hs(h, base)] * ref[:, _hs(h)], axis=-1, keepdims=True))
        o_ref[...] = _head_column(cols)

    def rows(w):
        return pl.BlockSpec((tm, w), lambda i: (i, 0))

    return pl.pallas_call(body, grid=(t // tm,), in_specs=[rows(dmixed.shape[1]), rows(FOX_W), rows(SWA_W), rows(MEM_W)],
                          out_specs=rows(HEAD), out_shape=jax.ShapeDtypeStruct((t, HEAD), F32),
                          compiler_params=_params(1), name=name)(dmixed, o_a, o_b, o_c)


def _fox_fwd(fq, fk, fv, c_col, c_row, name, rides=()):
    t = fq.shape[0]
    tb = min(t, 512)
    nb = t // tb

    def body(q_ref, k_ref, v_ref, cc_ref, cr_ref, o_ref, lse_ref, m_s, l_s, acc_s):
        qi, ki = pl.program_id(0), pl.program_id(1)

        @pl.when(ki == 0)
        def _():
            m_s[...] = jnp.full_like(m_s, NEG_INF)
            l_s[...] = jnp.zeros_like(l_s)
            acc_s[...] = jnp.zeros_like(acc_s)

        def step(diagonal):
            if diagonal:
                r = lax.broadcasted_iota(jnp.int32, (tb, tb), 0)
                c = lax.broadcasted_iota(jnp.int32, (tb, tb), 1)
                keep = r >= c
            for h in range(FOX_H):
                s = _dot(q_ref[:, _hs(h)], k_ref[:, _hs(h)], "nt") * SCALE
                s = s + cc_ref[:, h:h + 1] - cr_ref[h:h + 1, :]
                if diagonal:
                    s = jnp.where(keep, s, NEG_INF)
                m_prev = m_s[h]
                m_new = jnp.maximum(m_prev, jnp.max(s, axis=-1, keepdims=True))
                alpha = jnp.exp(m_prev - m_new)
                p = jnp.exp(s - m_new)
                l_s[h] = alpha * l_s[h] + jnp.sum(p, axis=-1, keepdims=True)
                acc_s[:, _hs(h)] = alpha * acc_s[:, _hs(h)] + _dot(p, v_ref[:, _hs(h)])
                m_s[h] = m_new

        @pl.when(ki < qi)
        def _():
            step(False)

        @pl.when(ki == qi)
        def _():
            step(True)
            lses = []
            for h in range(FOX_H):
                o_ref[:, _hs(h)] = acc_s[:, _hs(h)] / l_s[h]
                lses.append(m_s[h] + jnp.log(l_s[h]))
            lse_ref[...] = _head_column(lses)

    qspec = pl.BlockSpec((tb, FOX_W), lambda i, j: (i, 0))
    kspec = pl.BlockSpec((tb, FOX_W), lambda i, j: (jnp.minimum(i, j), 0))
    return _call(
        name, body, (nb, nb),
        [qspec, kspec, kspec, pl.BlockSpec((tb, HEAD), lambda i, j: (i, 0)),
         pl.BlockSpec((8, tb), lambda i, j: (0, jnp.minimum(i, j)))],
        [qspec, pl.BlockSpec((tb, HEAD), lambda i, j: (i, 0))],
        [jax.ShapeDtypeStruct((t, FOX_W), F32), jax.ShapeDtypeStruct((t, HEAD), F32)],
        [fq, fk, fv, c_col, c_row],
        scratch=[pltpu.VMEM((FOX_H, tb, 1), F32), pltpu.VMEM((FOX_H, tb, 1), F32), pltpu.VMEM((tb, FOX_W), F32)],
        rides=rides)


def _fox_bwd(fq, fk, fv, c_col, c_row, dmixed, lse, delta, name, rides=()):
    t = fq.shape[0]
    tb = min(t, 512)
    nb = t // tb

    def body(q_ref, k_ref, v_ref, cc_ref, cr_ref, do_ref, lse_ref, dl_ref,
             dq_ref, dk_ref, dv_ref, dcc_ref, dcr_ref):
        ki, qi = pl.program_id(0), pl.program_id(1)

        @pl.when((ki == 0) & (qi == 0))
        def _():
            dq_ref[...] = jnp.zeros_like(dq_ref)
            dcc_ref[...] = jnp.zeros_like(dcc_ref)

        @pl.when(qi == 0)
        def _():
            dk_ref[...] = jnp.zeros_like(dk_ref)
            dv_ref[...] = jnp.zeros_like(dv_ref)
            dcr_ref[...] = jnp.zeros_like(dcr_ref)

        def step(diagonal):
            rows = pl.ds(pl.multiple_of(qi * tb, tb), tb)
            if diagonal:
                r = lax.broadcasted_iota(jnp.int32, (tb, tb), 0)
                c = lax.broadcasted_iota(jnp.int32, (tb, tb), 1)
                keep = r >= c
            row_sums, col_sums = [], []
            for h in range(FOX_H):
                q, k, v, do = q_ref[:, _hs(h)], k_ref[:, _hs(h)], v_ref[:, _hs(h)], do_ref[:, _hs(h)]
                s = _dot(q, k, "nt") * SCALE + cc_ref[:, h:h + 1] - cr_ref[h:h + 1, :]
                if diagonal:
                    s = jnp.where(keep, s, NEG_INF)
                p = jnp.exp(s - lse_ref[:, h:h + 1])
                dp = _dot(do, v, "nt")
                ds = p * (dp - dl_ref[:, h:h + 1])
                dv_ref[:, _hs(h)] += _dot(p, do, "tn")
                dk_ref[:, _hs(h)] += _dot(ds, q, "tn") * SCALE
                dq_ref[rows, _hs(h)] += _dot(ds, k) * SCALE
                row_sums.append(jnp.sum(ds, axis=1, keepdims=True))
                col_sums.append(jnp.sum(ds, axis=0, keepdims=True))
            dcc_ref[rows, :] += _head_column(row_sums)
            dcr_ref[...] += _head_row(col_sums)

        @pl.when(qi > ki)
        def _():
            step(False)

        @pl.when(qi == ki)
        def _():
            step(True)

    def qmap(j, i):
        return (jnp.maximum(i, j), 0)

    qspec = pl.BlockSpec((tb, FOX_W), qmap)
    kspec = pl.BlockSpec((tb, FOX_W), lambda j, i: (j, 0))
    colspec = pl.BlockSpec((tb, HEAD), qmap)
    rowspec = pl.BlockSpec((8, tb), lambda j, i: (0, j))
    return _call(
        name, body, (nb, nb), [qspec, kspec, kspec, colspec, rowspec, qspec, colspec, colspec],
        [pl.BlockSpec((t, FOX_W), lambda j, i: (0, 0)), kspec, kspec,
         pl.BlockSpec((t, HEAD), lambda j, i: (0, 0)), rowspec],
        [jax.ShapeDtypeStruct((t, FOX_W), F32)] * 3 + [jax.ShapeDtypeStruct((t, HEAD), F32),
                                                       jax.ShapeDtypeStruct((8, t), F32)],
        [fq, fk, fv, c_col, c_row, dmixed, lse, delta], rides=rides)


def _swa_logits(q, k_cur, k_prev, slope, first_block):
    w = SWA_BLOCK
    r = lax.broadcasted_iota(jnp.int32, (w, w), 0)
    j = lax.broadcasted_iota(jnp.int32, (w, w), 1)
    dist_cur = r - j
    dist_prev = w + r - j
    s_cur = _dot(q, k_cur, "nt") * SCALE - slope * dist_cur.astype(F32)
    s_cur = jnp.where(dist_cur >= 0, s_cur, NEG_INF)
    s_prev = _dot(q, k_prev, "nt") * SCALE - slope * dist_prev.astype(F32)
    s_prev = jnp.where((j > r) & jnp.logical_not(first_block), s_prev, NEG_INF)
    return s_cur, s_prev


def _slope(h):
    return float(2.0 ** (-8.0 * (h + 1) / SWA_H))


def _swa_fwd(sq, sk, sv, sp, name):
    t = sq.shape[0]
    w = SWA_BLOCK
    nb = t // w
    group = SWA_H // SWA_KV

    def body(q_ref, kp_ref, kc_ref, vp_ref, vc_ref, sp_ref, o_ref, lse_ref):
        first = pl.program_id(0) == 0
        lses = []
        for h in range(SWA_H):
            kv = h // group
            s_cur, s_prev = _swa_logits(q_ref[:, _hs(h)], kc_ref[:, _hs(kv)], kp_ref[:, _hs(kv)], _slope(h), first)
            sink = sp_ref[R_SINK:R_SINK + 1, h:h + 1]
            m = jnp.maximum(jnp.maximum(jnp.max(s_cur, axis=-1, keepdims=True),
                                        jnp.max(s_prev, axis=-1, keepdims=True)), sink)
            p_cur = jnp.exp(s_cur - m)
            p_prev = jnp.exp(s_prev - m)
            l = jnp.sum(p_cur, axis=-1, keepdims=True) + jnp.sum(p_prev, axis=-1, keepdims=True) + jnp.exp(sink - m)
            o_ref[:, _hs(h)] = (_dot(p_cur, vc_ref[:, _hs(kv)]) + _dot(p_prev, vp_ref[:, _hs(kv)])) / l
            lses.append(m + jnp.log(l))
        lse_ref[...] = _head_column(lses)

    qspec = pl.BlockSpec((w, SWA_W), lambda n: (n, 0))
    cur = pl.BlockSpec((w, SWA_KV_W), lambda n: (n, 0))
    prev = pl.BlockSpec((w, SWA_KV_W), lambda n: (jnp.maximum(n - 1, 0), 0))
    return pl.pallas_call(
        body, grid=(nb,), in_specs=[qspec, prev, cur, prev, cur, pl.BlockSpec((16, 128), lambda n: (0, 0))],
        out_specs=[qspec, pl.BlockSpec((w, HEAD), lambda n: (n, 0))],
        out_shape=[jax.ShapeDtypeStruct((t, SWA_W), F32), jax.ShapeDtypeStruct((t, HEAD), F32)],
        compiler_params=_params(1), name=name)(sq, sk, sk, sv, sv, sp)


def _swa_bwd(sq, sk, sv, sp, dmixed, lse, delta, name):
    t = sq.shape[0]
    w = SWA_BLOCK
    nb = t // w
    group = SWA_H // SWA_KV
    do_block = FOX_W // SWA_W
    assert FOX_W % SWA_W == 0

    def body(q_ref, kp_ref, kc_ref, vp_ref, vc_ref, sp_ref, do_ref, lse_ref, dl_ref,
             dq_ref, dk_ref, dv_ref, dsp_ref, ck, cv):
        step = pl.program_id(0)
        first = step == nb - 1

        @pl.when(step == 0)
        def _():
            ck[...] = jnp.zeros_like(ck)
            cv[...] = jnp.zeros_like(cv)
            dsp_ref[...] = jnp.zeros_like(dsp_ref)

        dk_cur = [None] * SWA_KV
        dk_prev = [None] * SWA_KV
        dv_cur = [None] * SWA_KV
        dv_prev = [None] * SWA_KV
        dsinks = []

        def add(lst, i, v):
            lst[i] = v if lst[i] is None else lst[i] + v

        for h in range(SWA_H):
            kv = h // group
            q, do = q_ref[:, _hs(h)], do_ref[:, _hs(h)]
            kc, kp, vc, vp = kc_ref[:, _hs(kv)], kp_ref[:, _hs(kv)], vc_ref[:, _hs(kv)], vp_ref[:, _hs(kv)]
            s_cur, s_prev = _swa_logits(q, kc, kp, _slope(h), first)
            lse_h = lse_ref[:, h:h + 1]
            dl_h = dl_ref[:, FOX_H + h:FOX_H + h + 1]
            p_cur = jnp.exp(s_cur - lse_h)
            p_prev = jnp.exp(s_prev - lse_h)
            p_sink = jnp.exp(sp_ref[R_SINK:R_SINK + 1, h:h + 1] - lse_h)
            ds_cur = p_cur * (_dot(do, vc, "nt") - dl_h)
            ds_prev = p_prev * (_dot(do, vp, "nt") - dl_h)
            dq_ref[:, _hs(h)] = (_dot(ds_cur, kc) + _dot(ds_prev, kp)) * SCALE
            add(dk_cur, kv, _dot(ds_cur, q, "tn") * SCALE)
            add(dk_prev, kv, _dot(ds_prev, q, "tn") * SCALE)
            add(dv_cur, kv, _dot(p_cur, do, "tn"))
            add(dv_prev, kv, _dot(p_prev, do, "tn"))
            dsinks.append(-jnp.sum(p_sink * dl_h, axis=0, keepdims=True))
        for kv in range(SWA_KV):
            dk_ref[:, _hs(kv)] = dk_cur[kv] + ck[:, _hs(kv)]
            dv_ref[:, _hs(kv)] = dv_cur[kv] + cv[:, _hs(kv)]
            ck[:, _hs(kv)] = dk_prev[kv]
            cv[:, _hs(kv)] = dv_prev[kv]
        lane = lax.broadcasted_iota(jnp.int32, (1, HEAD), 1)
        row = jnp.zeros((1, HEAD), F32)
        for h in range(SWA_H):
            row = jnp.where(lane == h, dsinks[h], row)
        dsp_ref[R_SINK:R_SINK + 1, :] += row

    def rev(n):
        return nb - 1 - n

    qspec = pl.BlockSpec((w, SWA_W), lambda n: (rev(n), 0))
    cur = pl.BlockSpec((w, SWA_KV_W), lambda n: (rev(n), 0))
    prev = pl.BlockSpec((w, SWA_KV_W), lambda n: (jnp.maximum(rev(n) - 1, 0), 0))
    col = pl.BlockSpec((w, HEAD), lambda n: (rev(n), 0))
    small = pl.BlockSpec((16, 128), lambda n: (0, 0))
    return pl.pallas_call(
        body, grid=(nb,),
        in_specs=[qspec, prev, cur, prev, cur, small, pl.BlockSpec((w, SWA_W), lambda n: (rev(n), do_block)), col, col],
        out_specs=[qspec, cur, cur, small],
        out_shape=[jax.ShapeDtypeStruct((t, SWA_W), F32), jax.ShapeDtypeStruct((t, SWA_KV_W), F32),
                   jax.ShapeDtypeStruct((t, SWA_KV_W), F32), jax.ShapeDtypeStruct((16, 128), F32)],
        scratch_shapes=[pltpu.VMEM((w, SWA_KV_W), F32), pltpu.VMEM((w, SWA_KV_W), F32)],
        compiler_params=_params(1), name=name)(sq, sk, sk, sv, sv, sp, dmixed, lse, delta)


def _mem_pre(mkv, sp, name):
    m = mkv.shape[0]

    def body(x_ref, sp_ref, k_ref, v_ref):
        for h in range(MEM_H):
            k_ref[:, _hs(h)] = _head_norm(x_ref[:, _hs(h)], sp_ref[R_MK:R_MK + 1, :]).astype(BF16)
        v_ref[...] = x_ref[:, MEM_W:2 * MEM_W].astype(BF16)

    out = jax.ShapeDtypeStruct((m, MEM_W), BF16)
    return pl.pallas_call(body, out_shape=[out, out], name=name)(mkv, sp)


def _mem_post_bwd(mkv, sp, dmk, dmv, name):
    m = mkv.shape[0]

    def body(x_ref, sp_ref, dk_ref, dv_ref, d_ref, dsp_ref):
        dsp_ref[...] = jnp.zeros_like(dsp_ref)
        total = None
        for h in range(MEM_H):
            dx, dg = _head_norm_bwd(x_ref[:, _hs(h)], sp_ref[R_MK:R_MK + 1, :], dk_ref[:, _hs(h)])
            d_ref[:, _hs(h)] = dx.astype(BF16)
            total = dg if total is None else total + dg
        d_ref[:, MEM_W:2 * MEM_W] = dv_ref[...].astype(BF16)
        dsp_ref[R_MK:R_MK + 1, :] = total

    return pl.pallas_call(body, out_shape=[jax.ShapeDtypeStruct((m, 2 * MEM_W), BF16),
                                           jax.ShapeDtypeStruct((16, 128), F32)], name=name)(mkv, sp, dmk, dmv)


def _mem_fwd(mq, mk, mv, name):
    t = mq.shape[0]
    m = mk.shape[0]
    tq = min(t, 512)

    def body(q_ref, k_ref, v_ref, o_ref, lse_ref):
        lses = []
        for h in range(MEM_H):
            s = _dot(q_ref[:, _hs(h)], k_ref[:, _hs(h)], "nt") * SCALE
            mx = jnp.max(s, axis=-1, keepdims=True)
            p = jnp.exp(s - mx)
            l = jnp.sum(p, axis=-1, keepdims=True)
            o_ref[:, _hs(h)] = _dot(p, v_ref[:, _hs(h)]) / l
            lses.append(mx + jnp.log(l))
        lse_ref[...] = _head_column(lses)

    qspec = pl.BlockSpec((tq, MEM_W), lambda i: (i, 0))
    kspec = pl.BlockSpec((m, MEM_W), lambda i: (0, 0))
    return pl.pallas_call(
        body, grid=(t // tq,), in_specs=[qspec, kspec, kspec],
        out_specs=[qspec, pl.BlockSpec((tq, HEAD), lambda i: (i, 0))],
        out_shape=[jax.ShapeDtypeStruct((t, MEM_W), F32), jax.ShapeDtypeStruct((t, HEAD), F32)],
        compiler_params=_params(1), name=name)(mq, mk, mv)


def _mem_bwd(mq, mk, mv, dmixed, lse, delta, name):
    t = mq.shape[0]
    m = mk.shape[0]
    tq = min(t, 512)
    do_block = (FOX_W + SWA_W) // MEM_W
    assert (FOX_W + SWA_W) % MEM_W == 0

    def body(q_ref, k_ref, v_ref, do_ref, lse_ref, dl_ref, dq_ref, dk_ref, dv_ref):
        @pl.when(pl.program_id(0) == 0)
        def _():
            dk_ref[...] = jnp.zeros_like(dk_ref)
            dv_ref[...] = jnp.zeros_like(dv_ref)

        for h in range(MEM_H):
            q, k, v, do = q_ref[:, _hs(h)], k_ref[:, _hs(h)], v_ref[:, _hs(h)], do_ref[:, _hs(h)]
            s = _dot(q, k, "nt") * SCALE
            p = jnp.exp(s - lse_ref[:, h:h + 1])
            col = FOX_H + SWA_H + h
            ds = p * (_dot(do, v, "nt") - dl_ref[:, col:col + 1])
            dq_ref[:, _hs(h)] = _dot(ds, k) * SCALE
            dk_ref[:, _hs(h)] += _dot(ds, q, "tn") * SCALE
            dv_ref[:, _hs(h)] += _dot(p, do, "tn")

    qspec = pl.BlockSpec((tq, MEM_W), lambda i: (i, 0))
    kspec = pl.BlockSpec((m, MEM_W), lambda i: (0, 0))
    col = pl.BlockSpec((tq, HEAD), lambda i: (i, 0))
    return pl.pallas_call(
        body, grid=(t // tq,),
        in_specs=[qspec, kspec, kspec, pl.BlockSpec((tq, MEM_W), lambda i: (i, do_block)), col, col],
        out_specs=[qspec, kspec, kspec],
        out_shape=[jax.ShapeDtypeStruct((t, MEM_W), F32), jax.ShapeDtypeStruct((m, MEM_W), F32),
                   jax.ShapeDtypeStruct((m, MEM_W), F32)],
        compiler_params=_params(1), name=name)(mq, mk, mv, dmixed, lse, delta)


def _all_gather(xs, name):
    n = len(xs)

    def body(*refs):
        x_refs, o_refs = refs[:n], refs[n:2 * n]
        send_sems, recv_sems, local_sems = refs[2 * n:]
        x, y, c = _me()
        me, sibling = (x, y, c), (x, y, 1 - c)
        chips = [(1 - x, y), (x, 1 - y), (1 - x, 1 - y)]

        def copy(a, k, block, to, src=None):
            slot = o_refs[a].at[_lin(block)]
            return pltpu.make_async_remote_copy(
                src_ref=slot if src is None else src, dst_ref=slot, send_sem=send_sems.at[a, k],
                recv_sem=recv_sems.at[a, k], device_id=to, device_id_type=MESH)

        mine = [pltpu.make_async_copy(x_refs[a], o_refs[a].at[_lin(me)], local_sems.at[a]) for a in range(n)]
        for cp in mine:
            cp.start()
        first = []
        for a in range(n):
            first.append(copy(a, 0, me, sibling, src=x_refs[a]))
            first += [copy(a, 1 + j, me, (*chip, c), src=x_refs[a]) for j, chip in enumerate(chips)]
        for cp in first:
            cp.start()
        passed = []
        for j, chip in enumerate(chips):
            for a in range(n):
                copy(a, 1 + j, (*chip, c), me).wait_recv()
                cp = copy(a, 4 + j, (*chip, c), sibling)
                cp.start()
                passed.append(cp)
        for a in range(n):
            copy(a, 0, sibling, me).wait_recv()
            for j, chip in enumerate(chips):
                copy(a, 4 + j, (*chip, 1 - c), me).wait_recv()
        for cp in first + passed:
            cp.wait_send()
        for cp in mine:
            cp.wait()

    return pl.pallas_call(
        body, in_specs=[ANY] * n, out_specs=[ANY] * n,
        out_shape=[jax.ShapeDtypeStruct((N_DEV,) + x.shape, x.dtype) for x in xs],
        scratch_shapes=[pltpu.SemaphoreType.DMA((n, 7)), pltpu.SemaphoreType.DMA((n, 7)),
                        pltpu.SemaphoreType.DMA((n,))],
        name=name)(*xs)


def _peers():
    x, y, c = _me()
    out = []
    for k in range(1, N_DEV):
        kx, ky, kc = (k >> 2) & 1, (k >> 1) & 1, k & 1
        out.append(((1 - x) if kx else x, (1 - y) if ky else y, (1 - c) if kc else c))
    return out


def _all_reduce_small(xs, name):
    n = len(xs)

    def body(*refs):
        x_refs, o_refs = refs[:n], refs[n:2 * n]
        bufs = refs[2 * n:3 * n]
        send_sems, recv_sems = refs[3 * n:]
        me = _lin(_me())
        peers = _peers()
        for a in range(n):
            bufs[a][me] = x_refs[a][...]
        sends = []
        for a in range(n):
            for k, peer in enumerate(peers):
                sends.append(pltpu.make_async_remote_copy(
                    src_ref=bufs[a].at[me], dst_ref=bufs[a].at[me], send_sem=send_sems.at[a, k],
                    recv_sem=recv_sems.at[a, k], device_id=peer, device_id_type=MESH))
        for cp in sends:
            cp.start()
        for a in range(n):
            for k, peer in enumerate(peers):
                pltpu.make_async_remote_copy(
                    src_ref=bufs[a].at[me], dst_ref=bufs[a].at[_lin(peer)], send_sem=send_sems.at[a, k],
                    recv_sem=recv_sems.at[a, k], device_id=peer, device_id_type=MESH).wait_recv()
        for cp in sends:
            cp.wait_send()
        for a in range(n):
            total = bufs[a][0]
            for q in range(1, N_DEV):
                total = total + bufs[a][q]
            o_refs[a][...] = total

    vmem = pl.BlockSpec(memory_space=pltpu.VMEM)
    return pl.pallas_call(
        body, in_specs=[vmem] * n, out_specs=[vmem] * n,
        out_shape=[jax.ShapeDtypeStruct(x.shape, F32) for x in xs],
        scratch_shapes=[pltpu.VMEM((N_DEV,) + x.shape, F32) for x in xs]
        + [pltpu.SemaphoreType.DMA((n, 7)), pltpu.SemaphoreType.DMA((n, 7))],
        name=name)(*xs)


def _pair_add(part, got, name):
    _, rows, cols = part.shape
    tm = _rows_tile(rows, cols * 2, budget=2 << 20)
    core = jnp.reshape(lax.axis_index("c"), (1,)).astype(jnp.int32)

    def body(c_ref, p_ref, g_ref, o_ref):
        o_ref[...] = (p_ref[...].astype(F32) + g_ref[...].astype(F32)).astype(BF16)

    spec = pl.BlockSpec((None, tm, cols), lambda q, i, c: (q, i, 0))
    grid_spec = pltpu.PrefetchScalarGridSpec(
        num_scalar_prefetch=1, grid=(4, rows // tm),
        in_specs=[pl.BlockSpec((None, tm, cols), lambda q, i, c: (2 * q + c[0], i, 0)), spec], out_specs=spec)
    return pl.pallas_call(body, grid_spec=grid_spec, out_shape=jax.ShapeDtypeStruct((4, rows, cols), BF16),
                          compiler_params=_params(2), name=name)(core, part, got)


def _adam_math(w, g, m, v):
    nm = ADAM_B1 * m + (1.0 - ADAM_B1) * g
    nv = ADAM_B2 * v + (1.0 - ADAM_B2) * (g * g)
    m_hat = nm / (1.0 - ADAM_B1 ** ADAM_STEP)
    v_hat = nv / (1.0 - ADAM_B2 ** ADAM_STEP)
    return -ADAM_LR * (m_hat / (jnp.sqrt(v_hat) + ADAM_EPS) + ADAM_WD * w), nm, nv


def _sum_chips(got, name):
    _, rows, cols = got.shape
    tm = _rows_tile(rows, cols * 2 * 4, budget=2 << 20)

    def body(r_ref, o_ref):
        o_ref[...] = ((r_ref[0].astype(F32) + r_ref[1].astype(F32)) + r_ref[2].astype(F32)) + r_ref[3].astype(F32)

    return pl.pallas_call(
        body, grid=(rows // tm,), in_specs=[pl.BlockSpec((4, tm, cols), lambda i: (0, i, 0))],
        out_specs=pl.BlockSpec((tm, cols), lambda i: (i, 0)), out_shape=jax.ShapeDtypeStruct((rows, cols), F32),
        compiler_params=_params(1), name=name)(got)


def _sum_adamw(got, col_block, w, m, v, name):
    _, rows, cols = w.shape
    tm = _rows_tile(rows, cols * 4, budget=1 << 20)

    def body(r_ref, w_ref, m_ref, v_ref, g_ref, d_ref, nm_ref, nv_ref):
        g = ((r_ref[0].astype(F32) + r_ref[1].astype(F32)) + r_ref[2].astype(F32)) + r_ref[3].astype(F32)
        g_ref[...] = g
        d_ref[...], nm_ref[...], nv_ref[...] = _adam_math(w_ref[...], g, m_ref[...], v_ref[...])

    spec = pl.BlockSpec((None, tm, cols), lambda i: (0, i, 0))
    out = jax.ShapeDtypeStruct(w.shape, F32)
    return pl.pallas_call(
        body, grid=(rows // tm,), in_specs=[pl.BlockSpec((4, tm, cols), lambda i: (0, i, col_block)), spec, spec, spec],
        out_specs=[spec] * 4, out_shape=[out] * 4, compiler_params=_params(1), name=name)(got, w, m, v)


def _adamw(w, g, m, v, name):
    rows, cols = w.shape
    tm = _rows_tile(rows, cols * 4, budget=2 << 20, mult=8)

    def body(w_ref, g_ref, m_ref, v_ref, d_ref, nm_ref, nv_ref):
        d_ref[...], nm_ref[...], nv_ref[...] = _adam_math(w_ref[...], g_ref[...], m_ref[...], v_ref[...])

    spec = pl.BlockSpec((tm, cols), lambda i: (i, 0))
    out = jax.ShapeDtypeStruct(w.shape, F32)
    return pl.pallas_call(body, grid=(rows // tm,), in_specs=[spec] * 4, out_specs=[spec] * 3,
                          out_shape=[out] * 3, compiler_params=_params(1), name=name)(w, g, m, v)


def _permute_in(w):
    logit0 = 3 * FOX_W
    pad = jnp.zeros(w.shape[:-1] + (HEAD - N_LOGIT,), w.dtype)
    return jnp.concatenate([w[..., :logit0], w[..., logit0 + N_LOGIT:], w[..., logit0:logit0 + N_LOGIT], pad], axis=-1)


def _unpermute_in(w):
    logit0 = 3 * FOX_W
    return jnp.concatenate([w[..., :logit0], w[..., C_FL:C_FL + N_LOGIT], w[..., logit0:C_FL]], axis=-1)


def _pad_row(v, width):
    return jnp.pad(v, ((0, 0), (0, width - v.shape[1])))


def _pack_small(fq, fk, sq, sk, mq, mk, fb, sinks):
    rows = [fq, fk, sq, sk, mq, mk, _pad_row(fb, HEAD), _pad_row(sinks, HEAD)]
    return jnp.concatenate(rows + [jnp.zeros((8, HEAD), F32)], axis=0)


def _pack_norms(a, b, c, d):
    return jnp.concatenate([a, b, c, d, jnp.zeros((4, a.shape[1]), F32)], axis=0)


def kernel(x, mem, ffn1_norm, ffn1_gate, ffn1_up, ffn1_down, mix_norm, mem_norm, w_in, forget_bias, w_mem_k, w_mem_v, fox_q_gain, fox_k_gain, swa_q_gain, swa_k_gain, swa_sinks, mem_q_gain, mem_k_gain, w_out, ffn2_norm, ffn2_gate, ffn2_up, ffn2_down, loss_target, m_ffn1_norm, m_ffn1_gate, m_ffn1_up, m_ffn1_down, m_mix_norm, m_mem_norm, m_w_in, m_forget_bias, m_w_mem_k, m_w_mem_v, m_fox_q_gain, m_fox_k_gain, m_swa_q_gain, m_swa_k_gain, m_swa_sinks, m_mem_q_gain, m_mem_k_gain, m_w_out, m_ffn2_norm, m_ffn2_gate, m_ffn2_up, m_ffn2_down, v_ffn1_norm, v_ffn1_gate, v_ffn1_up, v_ffn1_down, v_mix_norm, v_mem_norm, v_w_in, v_forget_bias, v_w_mem_k, v_w_mem_v, v_fox_q_gain, v_fox_k_gain, v_swa_q_gain, v_swa_k_gain, v_swa_sinks, v_mem_q_gain, v_mem_k_gain, v_w_out, v_ffn2_norm, v_ffn2_gate, v_ffn2_up, v_ffn2_down):
    x0 = x[0]
    mem0 = mem[0]
    target = loss_target[0]
    t, d = x0.shape
    d_shard = w_in.shape[1]
    m_len = mem0.shape[0]
    tm = min(t, 512)
    tk = min(t, 512)
    tn = IN_W // 3

    local = {
        "g1": ffn1_gate[0], "u1": ffn1_up[0], "d1": ffn1_down[0],
        "g2": ffn2_gate[0], "u2": ffn2_up[0], "d2": ffn2_down[0],
        "in": _permute_in(w_in[0]), "out": w_out[0],
        "mkv": jnp.concatenate([w_mem_k[0], w_mem_v[0]], axis=1),
    }
    shard = {k: _cast_bf16(v, f"cast_{k}") for k, v in local.items()}
    sp = _pack_small(fox_q_gain, fox_k_gain, swa_q_gain, swa_k_gain, mem_q_gain, mem_k_gain, forget_bias, swa_sinks)
    wt = {}

    wt["g1"], wt["u1"] = _all_gather([shard["g1"], shard["u1"]], "gather_ffn1_in")
    xn1 = _rms_fwd(x0, ffn1_norm, "ffn1_norm")
    (g1, u1, h1), (half,) = _ffn_up(xn1, wt["g1"], wt["u1"], "ffn1",
                                    rides=[_ride_gather_chips([shard["d1"], shard["in"]])])
    (wt["d1"], wt["in"]), = _only_copies("gather_d1_in_sibling", [_ride_gather_sibling(half)])
    x1, (half,) = _ffn_down(x0, h1, wt["d1"], "ffn1",
                            rides=[_ride_gather_chips([shard["out"], shard["mkv"], shard["g2"]])])
    w_in_full = wt["in"].reshape(d, IN_W)

    hn = _rms_fwd(x1, mix_norm, "mix_norm")
    proj, ((wt["out"], wt["mkv"], wt["g2"]), half) = _mm(
        "proj", [(hn, pl.BlockSpec((tm, d), lambda n, i, k: (i, 0)),
                  w_in_full, pl.BlockSpec((d, tn), lambda n, i, k: (0, n)))],
        "nn", (3, t // tm, 1), jax.ShapeDtypeStruct((t, IN_W), F32), pl.BlockSpec((tm, tn), lambda n, i, k: (i, n)),
        rides=[_ride_gather_sibling(half), _ride_gather_chips([shard["u2"]])])
    w_out_full = wt["out"].reshape(d, d)
    w_mkv_full = wt["mkv"].reshape(d, 2 * MEM_W)
    fq, fk, fv, sq, sk, sv, mq, c_col = _attn_pre(proj, sp, "attn_pre")
    c_row = jnp.transpose(c_col[:, :8])

    mn = _rms_fwd(mem0, mem_norm, "mem_norm")
    mkv = _mm("mem_kv", [(mn, pl.BlockSpec((m_len, d), lambda k: (0, 0)),
                          w_mkv_full, pl.BlockSpec((d, 2 * MEM_W), lambda k: (0, 0)))],
              "nn", (1,), jax.ShapeDtypeStruct((m_len, 2 * MEM_W), F32),
              pl.BlockSpec((m_len, 2 * MEM_W), lambda k: (0, 0)))
    mk, mv = _mem_pre(mkv, sp, "mem_pre")

    (o_a, lse_a), ((wt["u2"],), half) = _fox_fwd(
        fq, fk, fv, c_col, c_row, "fox_fwd", rides=[_ride_gather_sibling(half), _ride_gather_chips([shard["d2"]])])
    o_b, lse_b = _swa_fwd(sq, sk, sv, sp, "swa_fwd")
    o_c, lse_c = _mem_fwd(mq, mk, mv, "mem_fwd")

    def rows_spec(width):
        return pl.BlockSpec((tm, width), lambda i, k: (i, 0))

    def wout_rows(first, width):
        assert first % width == 0
        return pl.BlockSpec((width, d), lambda i, k: (first // width, 0))

    xspec = pl.BlockSpec((tm, d), lambda i, k: (i, 0))
    x2, ((wt["d2"],),) = _mm(
        "mix_out",
        [(o_a, rows_spec(FOX_W), w_out_full, wout_rows(0, FOX_W)),
         (o_b, rows_spec(SWA_W), w_out_full, wout_rows(FOX_W, SWA_W)),
         (o_c, rows_spec(MEM_W), w_out_full, wout_rows(FOX_W + SWA_W, MEM_W))],
        "nn", (t // tm, 1), jax.ShapeDtypeStruct((t, d), F32), xspec, res=x1, res_spec=xspec,
        rides=[_ride_gather_sibling(half)])

    xn2 = _rms_fwd(x2, ffn2_norm, "ffn2_norm")
    (g2, u2, h2), _ = _ffn_up(xn2, wt["g2"], wt["u2"], "ffn2")
    x3 = _ffn_down(x2, h2, wt["d2"], "ffn2")

    dy, dyb, sq_err = _loss_head(x3, target, "loss_head")
    loss = lax.psum(0.5 * sq_err[0, 0] / d, ("x", "y", "c"))

    got = {}
    paired = {}
    landed = {}

    def pair(k, part):
        paired[k] = _pair_add(part, got[k], f"pair_{k}")

    (dg2, du2), _ = _ffn_dact(dyb, wt["d2"], g2, u2, "ffn2")
    part_d2 = _ffn_dwd(h2, dyb, "ffn2_dwd")
    part_g2, ((got["d2"],),) = _ffn_dwin(xn2, dg2, "ffn2_dwg", rides=[_ride_scatter_sibling([part_d2])])
    pair("d2", part_d2)
    part_u2, ((landed["d2"],), (got["g2"],)) = _ffn_dwin(
        xn2, du2, "ffn2_dwu", rides=[_ride_scatter_chips([paired["d2"]]), _ride_scatter_sibling([part_g2])])
    pair("g2", part_g2)
    dxn2, ((landed["g2"],), (got["u2"],)) = _ffn_dxn(
        dg2, du2, wt["g2"], wt["u2"], "ffn2_dxn",
        rides=[_ride_scatter_chips([paired["g2"]]), _ride_scatter_sibling([part_u2])])
    pair("u2", part_u2)
    dx2, dx2b, dgain_ffn2 = _rms_bwd(x2, ffn2_norm, dxn2, dy, "ffn2_norm_bwd")

    dmixed = _mm("mix_out_dx", [(dx2b, xspec, w_out_full, pl.BlockSpec((d, d), lambda i, k: (0, 0)))],
                 "nt", (t // tm, 1), jax.ShapeDtypeStruct((t, d), F32), xspec)

    def k_rows(width):
        return pl.BlockSpec((tk, width), lambda j, k: (k, 0))

    part_out = [
        _mm(f"mix_out_dw{i}", [(o, k_rows(width), dx2b, k_rows(d))], "tn", (1, t // tk),
            jax.ShapeDtypeStruct((width, d), BF16), pl.BlockSpec((width, d), lambda j, k: (0, 0)))
        for i, (o, width) in enumerate(((o_a, FOX_W), (o_b, SWA_W), (o_c, MEM_W)))
    ]
    part_out = jnp.concatenate(part_out, axis=0).reshape(N_DEV, d_shard, d)

    delta = _delta(dmixed, o_a, o_b, o_c, "attn_delta")
    (dfq, dfk, dfv, dc_col, dc_row), ((landed["u2"],),) = _fox_bwd(
        fq, fk, fv, c_col, c_row, dmixed, lse_a, delta, "fox_bwd", rides=[_ride_scatter_chips([paired["u2"]])])
    dsq, dsk, dsv, dsp_sink = _swa_bwd(sq, sk, sv, sp, dmixed, lse_b, delta, "swa_bwd")
    dmq, dmk, dmv = _mem_bwd(mq, mk, mv, dmixed, lse_c, delta, "mem_bwd")

    dmkv, dsp_mem = _mem_post_bwd(mkv, sp, dmk, dmv, "mem_post_bwd")
    part_mkv = _mm("mem_kv_dw", [(mn, pl.BlockSpec((m_len, d), lambda k: (0, 0)),
                                  dmkv, pl.BlockSpec((m_len, 2 * MEM_W), lambda k: (0, 0)))],
                   "tn", (1,), jax.ShapeDtypeStruct((d, 2 * MEM_W), BF16),
                   pl.BlockSpec((d, 2 * MEM_W), lambda k: (0, 0))).reshape(N_DEV, d_shard, 2 * MEM_W)
    dmn = _mm("mem_kv_dx", [(dmkv, pl.BlockSpec((m_len, 2 * MEM_W), lambda k: (0, 0)),
                             w_mkv_full, pl.BlockSpec((d, 2 * MEM_W), lambda k: (0, 0)))],
              "nt", (1,), jax.ShapeDtypeStruct((m_len, d), F32), pl.BlockSpec((m_len, d), lambda k: (0, 0)))
    _, _, dgain_mem = _rms_bwd(mem0, mem_norm, dmn, None, "mem_norm_bwd")

    dc_row_t = _pad_row(jnp.transpose(dc_row), HEAD)
    dproj, dsp_attn = _attn_post_bwd(proj, sp, dfq, dfk, dfv, dsq, dsk, dsv, dmq, dc_col, dc_row_t, "attn_post_bwd")
    dhn, ((got["out"], got["mkv"]),) = _mm(
        "proj_dx", [(dproj, pl.BlockSpec((tm, IN_W), lambda i, k: (i, 0)),
                     w_in_full, pl.BlockSpec((d, IN_W), lambda i, k: (0, 0)))],
        "nt", (t // tm, 1), jax.ShapeDtypeStruct((t, d), F32), xspec,
        rides=[_ride_scatter_sibling([part_out, part_mkv])])
    pair("out", part_out)
    pair("mkv", part_mkv)
    part_in, ((landed["out"], landed["mkv"]),) = _mm(
        "proj_dw", [(hn, pl.BlockSpec((tk, d), lambda n, k: (k, 0)),
                     dproj, pl.BlockSpec((tk, tn), lambda n, k: (k, n)))],
        "tn", (3, t // tk), jax.ShapeDtypeStruct((d, IN_W), BF16), pl.BlockSpec((d, tn), lambda n, k: (0, n)),
        rides=[_ride_scatter_chips([paired["out"], paired["mkv"]])])
    part_in = part_in.reshape(N_DEV, d_shard, IN_W)
    dx1, dx1b, dgain_mix = _rms_bwd(x1, mix_norm, dhn, dx2, "mix_norm_bwd")

    (dg1, du1), ((got["in"],),) = _ffn_dact(dx1b, wt["d1"], g1, u1, "ffn1", rides=[_ride_scatter_sibling([part_in])])
    pair("in", part_in)
    part_d1, ((landed["in"],),) = _ffn_dwd(h1, dx1b, "ffn1_dwd", rides=[_ride_scatter_chips([paired["in"]])])
    part_g1, ((got["d1"],),) = _ffn_dwin(xn1, dg1, "ffn1_dwg", rides=[_ride_scatter_sibling([part_d1])])
    pair("d1", part_d1)
    part_u1, ((landed["d1"],), (got["g1"],)) = _ffn_dwin(
        xn1, du1, "ffn1_dwu", rides=[_ride_scatter_chips([paired["d1"]]), _ride_scatter_sibling([part_g1])])
    pair("g1", part_g1)
    dxn1, ((landed["g1"],), (got["u1"],)) = _ffn_dxn(
        dg1, du1, wt["g1"], wt["u1"], "ffn1_dxn",
        rides=[_ride_scatter_chips([paired["g1"]]), _ride_scatter_sibling([part_u1])])
    pair("u1", part_u1)
    grad_x, _, dgain_ffn1 = _rms_bwd(x0, ffn1_norm, dxn1, dx1, "ffn1_norm_bwd")
    (landed["u1"],), = _only_copies("scatter_last", [_ride_scatter_chips([paired["u1"]])])

    norms_sum, small_sum = _all_reduce_small(
        [_pack_norms(dgain_ffn1, dgain_mix, dgain_mem, dgain_ffn2), dsp_attn + dsp_sink + dsp_mem], "reduce_small")

    result = {
        "ffn1_gate": _sum_adamw(landed["g1"], 0, ffn1_gate, m_ffn1_gate, v_ffn1_gate, "adamw_ffn1_gate"),
        "ffn1_up": _sum_adamw(landed["u1"], 0, ffn1_up, m_ffn1_up, v_ffn1_up, "adamw_ffn1_up"),
        "ffn1_down": _sum_adamw(landed["d1"], 0, ffn1_down, m_ffn1_down, v_ffn1_down, "adamw_ffn1_down"),
        "w_mem_k": _sum_adamw(landed["mkv"], 0, w_mem_k, m_w_mem_k, v_w_mem_k, "adamw_w_mem_k"),
        "w_mem_v": _sum_adamw(landed["mkv"], 1, w_mem_v, m_w_mem_v, v_w_mem_v, "adamw_w_mem_v"),
        "w_out": _sum_adamw(landed["out"], 0, w_out, m_w_out, v_w_out, "adamw_w_out"),
        "ffn2_gate": _sum_adamw(landed["g2"], 0, ffn2_gate, m_ffn2_gate, v_ffn2_gate, "adamw_ffn2_gate"),
        "ffn2_up": _sum_adamw(landed["u2"], 0, ffn2_up, m_ffn2_up, v_ffn2_up, "adamw_ffn2_up"),
        "ffn2_down": _sum_adamw(landed["d2"], 0, ffn2_down, m_ffn2_down, v_ffn2_down, "adamw_ffn2_down"),
    }
    grad_in = _unpermute_in(_sum_chips(landed["in"], "sum_w_in"))
    result["w_in"] = (grad_in[None],) + tuple(
        o[None] for o in _adamw(w_in[0], grad_in, m_w_in[0], v_w_in[0], "adamw_w_in"))

    norm_names = ["ffn1_norm", "mix_norm", "mem_norm", "ffn2_norm"]
    norm_w = _pack_norms(ffn1_norm, mix_norm, mem_norm, ffn2_norm)
    norm_m = _pack_norms(m_ffn1_norm, m_mix_norm, m_mem_norm, m_ffn2_norm)
    norm_v = _pack_norms(v_ffn1_norm, v_mix_norm, v_mem_norm, v_ffn2_norm)
    outs = (norms_sum,) + tuple(_adamw(norm_w, norms_sum, norm_m, norm_v, "adamw_norms"))
    for i, k in enumerate(norm_names):
        result[k] = tuple(o[i:i + 1] for o in outs)

    small_names = ["fox_q_gain", "fox_k_gain", "swa_q_gain", "swa_k_gain", "mem_q_gain", "mem_k_gain",
                   "forget_bias", "swa_sinks"]
    small_m = _pack_small(m_fox_q_gain, m_fox_k_gain, m_swa_q_gain, m_swa_k_gain, m_mem_q_gain, m_mem_k_gain,
                          m_forget_bias, m_swa_sinks)
    small_v = _pack_small(v_fox_q_gain, v_fox_k_gain, v_swa_q_gain, v_swa_k_gain, v_mem_q_gain, v_mem_k_gain,
                          v_forget_bias, v_swa_sinks)
    outs = (small_sum,) + tuple(_adamw(sp, small_sum, small_m, small_v, "adamw_small"))
    for i, k in enumerate(small_names):
        width = N_LOGIT if k in ("forget_bias", "swa_sinks") else HEAD
        result[k] = tuple(o[i:i + 1, :width] for o in outs)

    order = ["ffn1_norm", "ffn1_gate", "ffn1_up", "ffn1_down", "mix_norm", "mem_norm", "w_in", "forget_bias",
             "w_mem_k", "w_mem_v", "fox_q_gain", "fox_k_gain", "swa_q_gain", "swa_k_gain", "swa_sinks",
             "mem_q_gain", "mem_k_gain", "w_out", "ffn2_norm", "ffn2_gate", "ffn2_up", "ffn2_down"]
    flat = [loss, grad_x[None]]
    for kind in range(4):
        flat += [result[k][kind] for k in order]
    return tuple(flat)
```

```python
import functools

import jax
import jax.numpy as jnp
from jax import lax
from jax.experimental import pallas as pl
from jax.experimental.pallas import tpu as pltpu

F32 = jnp.float32
BF16 = jnp.bfloat16
MESH = pl.DeviceIdType.MESH
ANY = pl.BlockSpec(memory_space=pl.ANY)

N_DEV = 8
EPS = 1e-6
NEG_INF = -1e30
HEAD = 128
FOX_H, SWA_H, SWA_KV, MEM_H = 6, 6, 2, 4
FOX_W, SWA_W, SWA_KV_W, MEM_W = FOX_H * HEAD, SWA_H * HEAD, SWA_KV * HEAD, MEM_H * HEAD
SCALE = HEAD ** -0.5
SWA_BLOCK = 128
C_FQ, C_FK, C_FV = 0, FOX_W, 2 * FOX_W
C_SQ = 3 * FOX_W
C_SK = C_SQ + SWA_W
C_SV = C_SK + SWA_KV_W
C_MQ = C_SV + SWA_KV_W
C_FL = C_MQ + MEM_W
IN_W = C_FL + HEAD
N_LOGIT = FOX_H
R_FQ, R_FK, R_SQ, R_SK, R_MQ, R_MK, R_FB, R_SINK = range(8)
ADAM_LR, ADAM_B1, ADAM_B2, ADAM_EPS, ADAM_WD, ADAM_STEP = 0.001, 0.9, 0.999, 1e-08, 0.01, 10
VMEM_BYTES = 56 * 1024 * 1024

DN = {
    "nn": (((1,), (0,)), ((), ())),
    "nt": (((1,), (1,)), ((), ())),
    "tn": (((0,), (0,)), ((), ())),
}


def _params(n_axes):
    return pltpu.CompilerParams(dimension_semantics=("arbitrary",) * n_axes, vmem_limit_bytes=VMEM_BYTES)


def _dot(a, b, dims="nn"):
    return lax.dot_general(a.astype(BF16), b.astype(BF16), DN[dims], preferred_element_type=F32)


def _sigmoid(x):
    return 1.0 / (1.0 + jnp.exp(-x))


def _me():
    return lax.axis_index("x"), lax.axis_index("y"), lax.axis_index("c")


def _lin(p):
    return 4 * p[0] + 2 * p[1] + p[2]


def _rows_tile(rows, row_bytes, budget=4 << 20, mult=16):
    best = None
    for k in range(1, rows + 1):
        if rows % k == 0 and (rows // k) % mult == 0 and (rows // k) * row_bytes <= budget:
            best = rows // k
            break
    assert best is not None, (rows, row_bytes)
    return best


class _Ride:
    def __init__(self, inputs, out_shapes, aliases, n_remote, n_local, start, wait):
        self.inputs, self.out_shapes, self.aliases = list(inputs), list(out_shapes), dict(aliases)
        self.n_remote, self.n_local, self.start, self.wait = n_remote, n_local, start, wait


def _remote(src, dst, send, recv, k, to):
    return pltpu.make_async_remote_copy(src_ref=src, dst_ref=dst, send_sem=send.at[k], recv_sem=recv.at[k],
                                        device_id=to, device_id_type=MESH)


def _other_chips(x, y):
    return [(1 - x, y), (x, 1 - y), (1 - x, 1 - y)]


ALL_CHIPS = [(0, 0), (0, 1), (1, 0), (1, 1)]


def _ride_gather_chips(xs):
    n = len(xs)

    def copies(ins, outs, send, recv):
        x, y, c = _me()
        out = []
        for a in range(n):
            for j, chip in enumerate(_other_chips(x, y)):
                peer = (*chip, c)
                out.append((_remote(ins[a], outs[a].at[_lin((x, y, c))], send, recv, 3 * a + j, peer),
                            _remote(ins[a], outs[a].at[_lin(peer)], send, recv, 3 * a + j, peer)))
        return out

    def mine(ins, outs, local):
        me = _lin(_me())
        return [pltpu.make_async_copy(ins[a], outs[a].at[me], local.at[a]) for a in range(n)]

    def start(ins, outs, send, recv, local):
        for cp in mine(ins, outs, local):
            cp.start()
        for sent, _ in copies(ins, outs, send, recv):
            sent.start()

    def wait(ins, outs, send, recv, local):
        for sent, landed in copies(ins, outs, send, recv):
            landed.wait_recv()
            sent.wait_send()
        for cp in mine(ins, outs, local):
            cp.wait()

    shapes = [jax.ShapeDtypeStruct((N_DEV,) + x.shape, x.dtype) for x in xs]
    return _Ride(xs, shapes, {}, 3 * n, n, start, wait)


def _ride_gather_sibling(bufs):
    n = len(bufs)

    def copies(outs, send, recv):
        x, y, c = _me()
        out = []
        for a in range(n):
            for q, (px, py) in enumerate(ALL_CHIPS):
                there = outs[a].at[4 * px + 2 * py + c]
                here = outs[a].at[4 * px + 2 * py + 1 - c]
                out.append((_remote(there, there, send, recv, 4 * a + q, (x, y, 1 - c)),
                            _remote(here, here, send, recv, 4 * a + q, (x, y, 1 - c))))
        return out

    def start(ins, outs, send, recv, local):
        for sent, _ in copies(outs, send, recv):
            sent.start()

    def wait(ins, outs, send, recv, local):
        for sent, landed in copies(outs, send, recv):
            landed.wait_recv()
            sent.wait_send()

    shapes = [jax.ShapeDtypeStruct(b.shape, b.dtype) for b in bufs]
    return _Ride(bufs, shapes, {a: a for a in range(n)}, 4 * n, 0, start, wait)


def _ride_scatter_sibling(parts):
    n = len(parts)

    def copies(ins, outs, send, recv):
        x, y, c = _me()
        out = []
        for a in range(n):
            for q, (px, py) in enumerate(ALL_CHIPS):
                cp = _remote(ins[a].at[4 * px + 2 * py + 1 - c], outs[a].at[q], send, recv, 4 * a + q, (x, y, 1 - c))
                out.append(cp)
        return out

    def start(ins, outs, send, recv, local):
        for cp in copies(ins, outs, send, recv):
            cp.start()

    def wait(ins, outs, send, recv, local):
        for cp in copies(ins, outs, send, recv):
            cp.wait_recv()
            cp.wait_send()

    shapes = [jax.ShapeDtypeStruct((4,) + p.shape[1:], p.dtype) for p in parts]
    return _Ride(parts, shapes, {}, 4 * n, 0, start, wait)


def _ride_scatter_chips(pairs):
    n = len(pairs)

    def copies(ins, outs, send, recv):
        x, y, c = _me()
        out = []
        for a in range(n):
            for j, (px, py) in enumerate(_other_chips(x, y)):
                peer = (px, py, c)
                out.append((_remote(ins[a].at[2 * px + py], outs[a].at[2 * x + y], send, recv, 3 * a + j, peer),
                            _remote(ins[a].at[2 * px + py], outs[a].at[2 * px + py], send, recv, 3 * a + j, peer)))
        return out

    def mine(ins, outs, local):
        x, y, _ = _me()
        return [pltpu.make_async_copy(ins[a].at[2 * x + y], outs[a].at[2 * x + y], local.at[a]) for a in range(n)]

    def start(ins, outs, send, recv, local):
        for cp in mine(ins, outs, local):
            cp.start()
        for sent, _ in copies(ins, outs, send, recv):
            sent.start()

    def wait(ins, outs, send, recv, local):
        for sent, landed in copies(ins, outs, send, recv):
            landed.wait_recv()
            sent.wait_send()
        for cp in mine(ins, outs, local):
            cp.wait()

    shapes = [jax.ShapeDtypeStruct(p.shape, p.dtype) for p in pairs]
    return _Ride(pairs, shapes, {}, 3 * n, n, start, wait)


def _call(name, body, grid, in_specs, out_specs, out_shape, operands, scratch=(), rides=()):
    n_in, n_out, n_scr = len(operands), len(out_shape), len(scratch)
    ride_in, ride_out, ride_scr, aliases, spans = [], [], [], {}, []
    for r in rides:
        for i, o in r.aliases.items():
            aliases[n_in + len(ride_in) + i] = n_out + len(ride_out) + o
        spans.append((len(ride_in), len(r.inputs), len(ride_out), len(r.out_shapes)))
        ride_in += r.inputs
        ride_out += r.out_shapes
        ride_scr += [pltpu.SemaphoreType.DMA((r.n_remote,)), pltpu.SemaphoreType.DMA((r.n_remote,)),
                     pltpu.SemaphoreType.DMA((max(r.n_local, 1),))]

    def wrapped(*refs):
        c_in, r_in = refs[:n_in], refs[n_in:n_in + len(ride_in)]
        p = n_in + len(ride_in)
        c_out, r_out = refs[p:p + n_out], refs[p + n_out:p + n_out + len(ride_out)]
        p += n_out + len(ride_out)
        c_scr, r_scr = refs[p:p + n_scr], refs[p + n_scr:]

        def each(method, at):
            def run():
                for k, (r, (i0, ni, o0, no)) in enumerate(zip(rides, spans)):
                    getattr(r, method)(r_in[i0:i0 + ni], r_out[o0:o0 + no], *r_scr[3 * k:3 * k + 3])
            if not rides:
                return
            if grid:
                cond = functools.reduce(jnp.logical_and, [pl.program_id(ax) == at(n) for ax, n in enumerate(grid)])
                pl.when(cond)(run)
            else:
                run()

        each("start", lambda n: 0)
        body(*c_in, *c_out, *c_scr)
        each("wait", lambda n: n - 1)

    outs = pl.pallas_call(
        wrapped, grid=grid, in_specs=list(in_specs) + [ANY] * len(ride_in),
        out_specs=list(out_specs) + [ANY] * len(ride_out), out_shape=list(out_shape) + ride_out,
        scratch_shapes=list(scratch) + ride_scr, input_output_aliases=aliases,
        compiler_params=_params(len(grid)), name=name)(*operands, *ride_in)
    outs = list(outs)
    ride_results = [outs[n_out + o0:n_out + o0 + no] for (_, _, o0, no) in spans]
    return outs[:n_out], ride_results


def _only_copies(name, rides):
    return _call(name, lambda: None, (), [], [], [], [], rides=rides)[1]


def _mm(name, pairs, dims, grid, out_shape, out_spec, res=None, res_spec=None, alpha=1.0, rides=()):
    n = len(pairs)
    nk = grid[-1]
    kax = len(grid) - 1
    acc_shape = tuple(d for d in out_spec.block_shape if d is not None)

    def body(*refs):
        pos = 2 * n
        r_ref = None
        if res is not None:
            r_ref = refs[pos]
            pos += 1
        o_ref = refs[pos]
        part = None
        for p in range(n):
            d = _dot(refs[2 * p][...], refs[2 * p + 1][...], dims)
            part = d if part is None else part + d

        def finish(acc):
            if alpha != 1.0:
                acc = acc * alpha
            if r_ref is not None:
                acc = r_ref[...] + acc
            o_ref[...] = acc.astype(o_ref.dtype)

        if nk == 1:
            finish(part)
        else:
            acc_ref = refs[pos + 1]
            k = pl.program_id(kax)

            @pl.when(k == 0)
            def _():
                acc_ref[...] = part

            @pl.when(k > 0)
            def _():
                acc_ref[...] += part

            @pl.when(k == nk - 1)
            def _():
                finish(acc_ref[...])

    operands, in_specs = [], []
    for a, a_spec, b, b_spec in pairs:
        operands += [a, b]
        in_specs += [a_spec, b_spec]
    if res is not None:
        operands.append(res)
        in_specs.append(res_spec)
    (out,), ride_results = _call(name, body, grid, in_specs, [out_spec], [out_shape], operands,
                                 scratch=[pltpu.VMEM(acc_shape, F32)] if nk > 1 else [], rides=rides)
    return (out, ride_results) if rides else out


def _cast_bf16(x, name):
    rows, cols = x.shape
    tm = _rows_tile(rows, cols * 4)

    def body(x_ref, o_ref):
        o_ref[...] = x_ref[...].astype(BF16)

    spec = pl.BlockSpec((tm, cols), lambda i: (i, 0))
    return pl.pallas_call(body, grid=(rows // tm,), in_specs=[spec], out_specs=spec,
                          out_shape=jax.ShapeDtypeStruct(x.shape, BF16), compiler_params=_params(1), name=name)(x)


def _rms_fwd(x, gain, name):
    rows, d = x.shape
    tm = min(rows, 512)

    def body(x_ref, g_ref, o_ref):
        xv = x_ref[...]
        r = lax.rsqrt(jnp.mean(xv * xv, axis=-1, keepdims=True) + EPS)
        o_ref[...] = (xv * r * g_ref[...]).astype(BF16)

    spec = pl.BlockSpec((tm, d), lambda i: (i, 0))
    return pl.pallas_call(body, grid=(rows // tm,), in_specs=[spec, pl.BlockSpec((1, d), lambda i: (0, 0))],
                          out_specs=spec, out_shape=jax.ShapeDtypeStruct(x.shape, BF16),
                          compiler_params=_params(1), name=name)(x, gain)


def _rms_bwd(x, gain, dxn, dres, name):
    rows, d = x.shape
    tm = min(rows, 256)
    with_res = dres is not None

    def body(*refs):
        if with_res:
            x_ref, g_ref, dy_ref, r_ref, dx_ref, dxb_ref, dg_ref = refs
        else:
            x_ref, g_ref, dy_ref, dx_ref, dxb_ref, dg_ref = refs
        xv = x_ref[...]
        r = lax.rsqrt(jnp.mean(xv * xv, axis=-1, keepdims=True) + EPS)
        xh = xv * r
        dy = dy_ref[...]
        dxh = dy * g_ref[...]
        dx = r * (dxh - xh * jnp.mean(dxh * xh, axis=-1, keepdims=True))
        if with_res:
            dx = dx + r_ref[...]
        dx_ref[...] = dx
        dxb_ref[...] = dx.astype(BF16)
        part = jnp.sum(dy * xh, axis=0, keepdims=True)

        @pl.when(pl.program_id(0) == 0)
        def _():
            dg_ref[...] = part

        @pl.when(pl.program_id(0) > 0)
        def _():
            dg_ref[...] += part

    spec = pl.BlockSpec((tm, d), lambda i: (i, 0))
    vec = pl.BlockSpec((1, d), lambda i: (0, 0))
    ops = [x, gain, dxn] + ([dres] if with_res else [])
    return pl.pallas_call(
        body, grid=(rows // tm,), in_specs=[spec, vec, spec] + ([spec] if with_res else []),
        out_specs=[spec, spec, vec],
        out_shape=[jax.ShapeDtypeStruct(x.shape, F32), jax.ShapeDtypeStruct(x.shape, BF16),
                   jax.ShapeDtypeStruct((1, d), F32)],
        compiler_params=_params(1), name=name)(*ops)


def _loss_head(y, target, name):
    rows, d = y.shape
    tm = min(rows, 256)

    def body(y_ref, t_ref, dy_ref, dyb_ref, acc_ref):
        err = y_ref[...] - t_ref[...]
        dy = err * (1.0 / d)
        dy_ref[...] = dy
        dyb_ref[...] = dy.astype(BF16)
        part = jnp.zeros((8, 128), F32) + jnp.sum(err * err)

        @pl.when(pl.program_id(0) == 0)
        def _():
            acc_ref[...] = part

        @pl.when(pl.program_id(0) > 0)
        def _():
            acc_ref[...] += part

    spec = pl.BlockSpec((tm, d), lambda i: (i, 0))
    return pl.pallas_call(
        body, grid=(rows // tm,), in_specs=[spec, spec],
        out_specs=[spec, spec, pl.BlockSpec((8, 128), lambda i: (0, 0))],
        out_shape=[jax.ShapeDtypeStruct(y.shape, F32), jax.ShapeDtypeStruct(y.shape, BF16),
                   jax.ShapeDtypeStruct((8, 128), F32)],
        compiler_params=_params(1), name=name)(y, target)


ROW_CHUNK = 256


def _ffn_up(xn, wg, wu, tag, rides=()):
    t, d = xn.shape
    nd, fs, _ = wg.shape
    tm = min(t, 512)
    rc = min(tm, ROW_CHUNK)

    def body(x_ref, wg_ref, wu_ref, g_ref, u_ref, h_ref):
        for r in range(0, tm, rc):
            xv = x_ref[r:r + rc, :]
            g = _dot(xv, wg_ref[...], "nt")
            u = _dot(xv, wu_ref[...], "nt")
            g_ref[r:r + rc, :] = g.astype(BF16)
            u_ref[r:r + rc, :] = u.astype(BF16)
            h_ref[r:r + rc, :] = (g * _sigmoid(g) * u).astype(BF16)

    wspec = pl.BlockSpec((None, fs, d), lambda j, i: (j, 0, 0))
    hspec = pl.BlockSpec((None, tm, fs), lambda j, i: (j, i, 0))
    hid = jax.ShapeDtypeStruct((nd, t, fs), BF16)
    return _call(f"{tag}_up", body, (nd, t // tm), [pl.BlockSpec((tm, d), lambda j, i: (i, 0)), wspec, wspec],
                 [hspec] * 3, [hid] * 3, [xn, wg, wu], rides=rides)


SHARDS_PER_STEP = 2


def _ffn_down(x, h, wd, tag, rides=()):
    nd, t, fs = h.shape
    d = x.shape[1]
    tm = min(t, 512)
    sps = SHARDS_PER_STEP
    xspec = pl.BlockSpec((tm, d), lambda i, k: (i, 0))
    pairs = [(h, pl.BlockSpec((None, tm, fs), lambda i, k, s=s: (sps * k + s, i, 0)),
              wd, pl.BlockSpec((None, fs, d), lambda i, k, s=s: (sps * k + s, 0, 0))) for s in range(sps)]
    return _mm(f"{tag}_down", pairs, "nn", (t // tm, nd // sps), jax.ShapeDtypeStruct((t, d), F32), xspec,
               res=x, res_spec=xspec, alpha=0.5, rides=rides)


def _ffn_dact(dyb, wd, g, u, tag, rides=()):
    nd, t, fs = g.shape
    d = dyb.shape[1]
    tm = min(t, 512)
    rc = min(tm, ROW_CHUNK)

    def body(dy_ref, wd_ref, g_ref, u_ref, dg_ref, du_ref):
        for r in range(0, tm, rc):
            dh = 0.5 * _dot(dy_ref[r:r + rc, :], wd_ref[...], "nt")
            gv = g_ref[r:r + rc, :].astype(F32)
            uv = u_ref[r:r + rc, :].astype(F32)
            sig = _sigmoid(gv)
            du_ref[r:r + rc, :] = (dh * gv * sig).astype(BF16)
            dg_ref[r:r + rc, :] = (dh * uv * sig * (1.0 + gv * (1.0 - sig))).astype(BF16)

    hspec = pl.BlockSpec((None, tm, fs), lambda j, i: (j, i, 0))
    hid = jax.ShapeDtypeStruct((nd, t, fs), BF16)
    return _call(f"{tag}_dact", body, (nd, t // tm),
                 [pl.BlockSpec((tm, d), lambda j, i: (i, 0)), pl.BlockSpec((None, fs, d), lambda j, i: (j, 0, 0)),
                  hspec, hspec], [hspec] * 2, [hid] * 2, [dyb, wd, g, u], rides=rides)


def _ffn_dw(hid, act, alpha, name, rides=()):
    nd, t, fs = hid.shape
    d = act.shape[1]
    tk = min(t, 2048)
    return _mm(name, [(hid, pl.BlockSpec((None, tk, fs), lambda j, k: (j, k, 0)),
                       act, pl.BlockSpec((tk, d), lambda j, k: (k, 0)))], "tn", (nd, t // tk),
               jax.ShapeDtypeStruct((nd, fs, d), BF16), pl.BlockSpec((None, fs, d), lambda j, k: (j, 0, 0)),
               alpha=alpha, rides=rides)


def _ffn_dxn(dg, du, wg, wu, name, rides=()):
    nd, t, fs = dg.shape
    d = wg.shape[2]
    tm = min(t, 512)
    sps = SHARDS_PER_STEP
    pairs = []
    for s in range(sps):
        am = pl.BlockSpec((None, tm, fs), lambda i, k, s=s: (sps * k + s, i, 0))
        wk = pl.BlockSpec((None, fs, d), lambda i, k, s=s: (sps * k + s, 0, 0))
        pairs += [(dg, am, wg, wk), (du, am, wu, wk)]
    return _mm(name, pairs, "nn", (t // tm, nd // sps),
               jax.ShapeDtypeStruct((t, d), F32), pl.BlockSpec((tm, d), lambda i, k: (i, 0)), rides=rides)


def _head_norm(x, gain):
    r = lax.rsqrt(jnp.mean(x * x, axis=-1, keepdims=True) + EPS)
    return x * r * gain


def _head_norm_bwd(x, gain, dy):
    r = lax.rsqrt(jnp.mean(x * x, axis=-1, keepdims=True) + EPS)
    xh = x * r
    dxh = dy * gain
    dx = r * (dxh - xh * jnp.mean(dxh * xh, axis=-1, keepdims=True))
    return dx, jnp.sum(dy * xh, axis=0, keepdims=True)


def _hs(h, base=0):
    return slice(base + h * HEAD, base + (h + 1) * HEAD)


def _tri(n, lower):
    r = lax.broadcasted_iota(jnp.int32, (n, n), 0)
    c = lax.broadcasted_iota(jnp.int32, (n, n), 1)
    return ((r >= c) if lower else (r <= c)).astype(F32)


def _attn_pre(proj, sp, name):
    t = proj.shape[0]
    tm = min(t, 256)

    def body(p_ref, sp_ref, fq, fk, fv, sq, sk, sv, mq, cc, carry):
        @pl.when(pl.program_id(0) == 0)
        def _():
            carry[...] = jnp.zeros_like(carry)

        for h in range(FOX_H):
            fq[:, _hs(h)] = _head_norm(p_ref[:, _hs(h, C_FQ)], sp_ref[R_FQ:R_FQ + 1, :]).astype(BF16)
            fk[:, _hs(h)] = _head_norm(p_ref[:, _hs(h, C_FK)], sp_ref[R_FK:R_FK + 1, :]).astype(BF16)
        fv[...] = p_ref[:, C_FV:C_FV + FOX_W].astype(BF16)
        for h in range(SWA_H):
            sq[:, _hs(h)] = _head_norm(p_ref[:, _hs(h, C_SQ)], sp_ref[R_SQ:R_SQ + 1, :]).astype(BF16)
        for h in range(SWA_KV):
            sk[:, _hs(h)] = _head_norm(p_ref[:, _hs(h, C_SK)], sp_ref[R_SK:R_SK + 1, :]).astype(BF16)
        sv[...] = p_ref[:, C_SV:C_SV + SWA_KV_W].astype(BF16)
        for h in range(MEM_H):
            mq[:, _hs(h)] = _head_norm(p_ref[:, _hs(h, C_MQ)], sp_ref[R_MQ:R_MQ + 1, :]).astype(BF16)
        z = p_ref[:, C_FL:C_FL + HEAD] + sp_ref[R_FB:R_FB + 1, :]
        lane = lax.broadcasted_iota(jnp.int32, z.shape, 1)
        log_f = jnp.minimum(z, 0.0) - jnp.log(1.0 + jnp.exp(-jnp.abs(z)))
        log_f = jnp.where(lane < N_LOGIT, log_f, 0.0)
        c = jnp.dot(_tri(tm, True), log_f, precision=lax.Precision.HIGHEST, preferred_element_type=F32)
        c = c + carry[0:1, :]
        cc[...] = c
        carry[...] = jnp.broadcast_to(c[tm - 1:tm, :], carry.shape)

    def rows(w):
        return pl.BlockSpec((tm, w), lambda i: (i, 0))

    def shape(w, dt):
        return jax.ShapeDtypeStruct((t, w), dt)

    widths = [FOX_W, FOX_W, FOX_W, SWA_W, SWA_KV_W, SWA_KV_W, MEM_W]
    return pl.pallas_call(
        body, grid=(t // tm,), in_specs=[rows(IN_W), pl.BlockSpec((16, 128), lambda i: (0, 0))],
        out_specs=[rows(w) for w in widths] + [rows(HEAD)],
        out_shape=[shape(w, BF16) for w in widths] + [shape(HEAD, F32)],
        scratch_shapes=[pltpu.VMEM((8, 128), F32)], compiler_params=_params(1), name=name)(proj, sp)


def _attn_post_bwd(proj, sp, dfq, dfk, dfv, dsq, dsk, dsv, dmq, dc_col, dc_row_t, name):
    t = proj.shape[0]
    tm = min(t, 256)
    nb = t // tm

    def body(p_ref, sp_ref, dfq_r, dfk_r, dfv_r, dsq_r, dsk_r, dsv_r, dmq_r, dcc_r, dcr_r, dp_ref, dsp_ref, carry):
        @pl.when(pl.program_id(0) == 0)
        def _():
            carry[...] = jnp.zeros_like(carry)
            dsp_ref[...] = jnp.zeros_like(dsp_ref)

        def group(n_heads, col, row, d_ref):
            total = None
            for h in range(n_heads):
                dx, dg = _head_norm_bwd(p_ref[:, _hs(h, col)], sp_ref[row:row + 1, :], d_ref[:, _hs(h)])
                dp_ref[:, _hs(h, col)] = dx.astype(BF16)
                total = dg if total is None else total + dg
            dsp_ref[row:row + 1, :] += total

        group(FOX_H, C_FQ, R_FQ, dfq_r)
        group(FOX_H, C_FK, R_FK, dfk_r)
        dp_ref[:, C_FV:C_FV + FOX_W] = dfv_r[...].astype(BF16)
        group(SWA_H, C_SQ, R_SQ, dsq_r)
        group(SWA_KV, C_SK, R_SK, dsk_r)
        dp_ref[:, C_SV:C_SV + SWA_KV_W] = dsv_r[...].astype(BF16)
        group(MEM_H, C_MQ, R_MQ, dmq_r)
        dc = dcc_r[...] - dcr_r[...]
        rc = jnp.dot(_tri(tm, False), dc, precision=lax.Precision.HIGHEST, preferred_element_type=F32)
        rc = rc + carry[0:1, :]
        carry[...] = jnp.broadcast_to(rc[0:1, :], carry.shape)
        z = p_ref[:, C_FL:C_FL + HEAD] + sp_ref[R_FB:R_FB + 1, :]
        dz = rc * _sigmoid(-z)
        dp_ref[:, C_FL:C_FL + HEAD] = dz.astype(BF16)
        dsp_ref[R_FB:R_FB + 1, :] += jnp.sum(dz, axis=0, keepdims=True)

    def rows(w):
        return pl.BlockSpec((tm, w), lambda i: (nb - 1 - i, 0))

    small = pl.BlockSpec((16, 128), lambda i: (0, 0))
    widths = [FOX_W, FOX_W, FOX_W, SWA_W, SWA_KV_W, SWA_KV_W, MEM_W, HEAD, HEAD]
    return pl.pallas_call(
        body, grid=(nb,), in_specs=[rows(IN_W), small] + [rows(w) for w in widths],
        out_specs=[rows(IN_W), small],
        out_shape=[jax.ShapeDtypeStruct((t, IN_W), BF16), jax.ShapeDtypeStruct((16, 128), F32)],
        scratch_shapes=[pltpu.VMEM((8, 128), F32)], compiler_params=_params(1), name=name,
    )(proj, sp, dfq, dfk, dfv, dsq, dsk, dsv, dmq, dc_col, dc_row_t)


def _head_column(values):
    rows = values[0].shape[0]
    lane = lax.broadcasted_iota(jnp.int32, (rows, HEAD), 1)
    out = jnp.zeros((rows, HEAD), F32)
    for h, v in enumerate(values):
        out = jnp.where(lane == h, v, out)
    return out


def _head_row(values, n_rows=8):
    cols = values[0].shape[1]
    sub = lax.broadcasted_iota(jnp.int32, (n_rows, cols), 0)
    out = jnp.zeros((n_rows, cols), F32)
    for h, v in enumerate(values):
        out = jnp.where(sub == h, v, out)
    return out


def _delta(dmixed, o_a, o_b, o_c, name):
    t = dmixed.shape[0]
    tm = min(t, 512)

    def body(d_ref, a_ref, b_ref, c_ref, o_ref):
        cols = []
        for ref, n_heads, base in ((a_ref, FOX_H, 0), (b_ref, SWA_H, FOX_W), (c_ref, MEM_H, FOX_W + SWA_W)):
            for h in range(n_heads):
                cols.append(jnp.sum(d_ref[:, _hs(h, base)] * ref[:, _hs(h)], axis=-1, keepdims=True))
        o_ref[...] = _head_column(cols)

    def rows(w):
        return pl.BlockSpec((tm, w), lambda i: (i, 0))

    return pl.pallas_call(body, grid=(t // tm,), in_specs=[rows(dmixed.shape[1]), rows(FOX_W), rows(SWA_W), rows(MEM_W)],
                          out_specs=rows(HEAD), out_shape=jax.ShapeDtypeStruct((t, HEAD), F32),
                          compiler_params=_params(1), name=name)(dmixed, o_a, o_b, o_c)


def _fox_fwd(fq, fk, fv, c_col, c_row, name, rides=()):
    t = fq.shape[0]
    tb = min(t, 512)
    nb = t // tb

    def body(q_ref, k_ref, v_ref, cc_ref, cr_ref, o_ref, lse_ref, m_s, l_s, acc_s):
        qi, ki = pl.program_id(0), pl.program_id(1)

        @pl.when(ki == 0)
        def _():
            m_s[...] = jnp.full_like(m_s, NEG_INF)
            l_s[...] = jnp.zeros_like(l_s)
            acc_s[...] = jnp.zeros_like(acc_s)

        def step(diagonal):
            if diagonal:
                r = lax.broadcasted_iota(jnp.int32, (tb, tb), 0)
                c = lax.broadcasted_iota(jnp.int32, (tb, tb), 1)
                keep = r >= c
            for h in range(FOX_H):
                s = _dot(q_ref[:, _hs(h)], k_ref[:, _hs(h)], "nt") * SCALE
                s = s + cc_ref[:, h:h + 1] - cr_ref[h:h + 1, :]
                if diagonal:
                    s = jnp.where(keep, s, NEG_INF)
                m_prev = m_s[h]
                m_new = jnp.maximum(m_prev, jnp.max(s, axis=-1, keepdims=True))
                alpha = jnp.exp(m_prev - m_new)
                p = jnp.exp(s - m_new)
                l_s[h] = alpha * l_s[h] + jnp.sum(p, axis=-1, keepdims=True)
                acc_s[:, _hs(h)] = alpha * acc_s[:, _hs(h)] + _dot(p, v_ref[:, _hs(h)])
                m_s[h] = m_new

        @pl.when(ki < qi)
        def _():
            step(False)

        @pl.when(ki == qi)
        def _():
            step(True)
            lses = []
            for h in range(FOX_H):
                o_ref[:, _hs(h)] = acc_s[:, _hs(h)] / l_s[h]
                lses.append(m_s[h] + jnp.log(l_s[h]))
            lse_ref[...] = _head_column(lses)

    qspec = pl.BlockSpec((tb, FOX_W), lambda i, j: (i, 0))
    kspec = pl.BlockSpec((tb, FOX_W), lambda i, j: (jnp.minimum(i, j), 0))
    return _call(
        name, body, (nb, nb),
        [qspec, kspec, kspec, pl.BlockSpec((tb, HEAD), lambda i, j: (i, 0)),
         pl.BlockSpec((8, tb), lambda i, j: (0, jnp.minimum(i, j)))],
        [qspec, pl.BlockSpec((tb, HEAD), lambda i, j: (i, 0))],
        [jax.ShapeDtypeStruct((t, FOX_W), F32), jax.ShapeDtypeStruct((t, HEAD), F32)],
        [fq, fk, fv, c_col, c_row],
        scratch=[pltpu.VMEM((FOX_H, tb, 1), F32), pltpu.VMEM((FOX_H, tb, 1), F32), pltpu.VMEM((tb, FOX_W), F32)],
        rides=rides)


def _fox_bwd(fq, fk, fv, c_col, c_row, dmixed, lse, delta, name, rides=()):
    t = fq.shape[0]
    tb = min(t, 512)
    nb = t // tb

    def body(q_ref, k_ref, v_ref, cc_ref, cr_ref, do_ref, lse_ref, dl_ref,
             dq_ref, dk_ref, dv_ref, dcc_ref, dcr_ref):
        ki, qi = pl.program_id(0), pl.program_id(1)

        @pl.when((ki == 0) & (qi == 0))
        def _():
            dq_ref[...] = jnp.zeros_like(dq_ref)
            dcc_ref[...] = jnp.zeros_like(dcc_ref)

        @pl.when(qi == 0)
        def _():
            dk_ref[...] = jnp.zeros_like(dk_ref)
            dv_ref[...] = jnp.zeros_like(dv_ref)
            dcr_ref[...] = jnp.zeros_like(dcr_ref)

        def step(diagonal):
            rows = pl.ds(pl.multiple_of(qi * tb, tb), tb)
            if diagonal:
                r = lax.broadcasted_iota(jnp.int32, (tb, tb), 0)
                c = lax.broadcasted_iota(jnp.int32, (tb, tb), 1)
                keep = r >= c
            row_sums, col_sums = [], []
            for h in range(FOX_H):
                q, k, v, do = q_ref[:, _hs(h)], k_ref[:, _hs(h)], v_ref[:, _hs(h)], do_ref[:, _hs(h)]
                s = _dot(q, k, "nt") * SCALE + cc_ref[:, h:h + 1] - cr_ref[h:h + 1, :]
                if diagonal:
                    s = jnp.where(keep, s, NEG_INF)
                p = jnp.exp(s - lse_ref[:, h:h + 1])
                dp = _dot(do, v, "nt")
                ds = p * (dp - dl_ref[:, h:h + 1])
                dv_ref[:, _hs(h)] += _dot(p, do, "tn")
                dk_ref[:, _hs(h)] += _dot(ds, q, "tn") * SCALE
                dq_ref[rows, _hs(h)] += _dot(ds, k) * SCALE
                row_sums.append(jnp.sum(ds, axis=1, keepdims=True))
                col_sums.append(jnp.sum(ds, axis=0, keepdims=True))
            dcc_ref[rows, :] += _head_column(row_sums)
            dcr_ref[...] += _head_row(col_sums)

        @pl.when(qi > ki)
        def _():
            step(False)

        @pl.when(qi == ki)
        def _():
            step(True)

    def qmap(j, i):
        return (jnp.maximum(i, j), 0)

    qspec = pl.BlockSpec((tb, FOX_W), qmap)
    kspec = pl.BlockSpec((tb, FOX_W), lambda j, i: (j, 0))
    colspec = pl.BlockSpec((tb, HEAD), qmap)
    rowspec = pl.BlockSpec((8, tb), lambda j, i: (0, j))
    return _call(
        name, body, (nb, nb), [qspec, kspec, kspec, colspec, rowspec, qspec, colspec, colspec],
        [pl.BlockSpec((t, FOX_W), lambda j, i: (0, 0)), kspec, kspec,
         pl.BlockSpec((t, HEAD), lambda j, i: (0, 0)), rowspec],
        [jax.ShapeDtypeStruct((t, FOX_W), F32)] * 3 + [jax.ShapeDtypeStruct((t, HEAD), F32),
                                                       jax.ShapeDtypeStruct((8, t), F32)],
        [fq, fk, fv, c_col, c_row, dmixed, lse, delta], rides=rides)


def _swa_logits(q, k_cur, k_prev, slope, first_block):
    w = SWA_BLOCK
    r = lax.broadcasted_iota(jnp.int32, (w, w), 0)
    j = lax.broadcasted_iota(jnp.int32, (w, w), 1)
    dist_cur = r - j
    dist_prev = w + r - j
    s_cur = _dot(q, k_cur, "nt") * SCALE - slope * dist_cur.astype(F32)
    s_cur = jnp.where(dist_cur >= 0, s_cur, NEG_INF)
    s_prev = _dot(q, k_prev, "nt") * SCALE - slope * dist_prev.astype(F32)
    s_prev = jnp.where((j > r) & jnp.logical_not(first_block), s_prev, NEG_INF)
    return s_cur, s_prev


def _slope(h):
    return float(2.0 ** (-8.0 * (h + 1) / SWA_H))


def _swa_fwd(sq, sk, sv, sp, name):
    t = sq.shape[0]
    w = SWA_BLOCK
    nb = t // w
    group = SWA_H // SWA_KV

    def body(q_ref, kp_ref, kc_ref, vp_ref, vc_ref, sp_ref, o_ref, lse_ref):
        first = pl.program_id(0) == 0
        lses = []
        for h in range(SWA_H):
            kv = h // group
            s_cur, s_prev = _swa_logits(q_ref[:, _hs(h)], kc_ref[:, _hs(kv)], kp_ref[:, _hs(kv)], _slope(h), first)
            sink = sp_ref[R_SINK:R_SINK + 1, h:h + 1]
            m = jnp.maximum(jnp.maximum(jnp.max(s_cur, axis=-1, keepdims=True),
                                        jnp.max(s_prev, axis=-1, keepdims=True)), sink)
            p_cur = jnp.exp(s_cur - m)
            p_prev = jnp.exp(s_prev - m)
            l = jnp.sum(p_cur, axis=-1, keepdims=True) + jnp.sum(p_prev, axis=-1, keepdims=True) + jnp.exp(sink - m)
            o_ref[:, _hs(h)] = (_dot(p_cur, vc_ref[:, _hs(kv)]) + _dot(p_prev, vp_ref[:, _hs(kv)])) / l
            lses.append(m + jnp.log(l))
        lse_ref[...] = _head_column(lses)

    qspec = pl.BlockSpec((w, SWA_W), lambda n: (n, 0))
    cur = pl.BlockSpec((w, SWA_KV_W), lambda n: (n, 0))
    prev = pl.BlockSpec((w, SWA_KV_W), lambda n: (jnp.maximum(n - 1, 0), 0))
    return pl.pallas_call(
        body, grid=(nb,), in_specs=[qspec, prev, cur, prev, cur, pl.BlockSpec((16, 128), lambda n: (0, 0))],
        out_specs=[qspec, pl.BlockSpec((w, HEAD), lambda n: (n, 0))],
        out_shape=[jax.ShapeDtypeStruct((t, SWA_W), F32), jax.ShapeDtypeStruct((t, HEAD), F32)],
        compiler_params=_params(1), name=name)(sq, sk, sk, sv, sv, sp)


def _swa_bwd(sq, sk, sv, sp, dmixed, lse, delta, name):
    t = sq.shape[0]
    w = SWA_BLOCK
    nb = t // w
    group = SWA_H // SWA_KV
    do_block = FOX_W // SWA_W
    assert FOX_W % SWA_W == 0

    def body(q_ref, kp_ref, kc_ref, vp_ref, vc_ref, sp_ref, do_ref, lse_ref, dl_ref,
             dq_ref, dk_ref, dv_ref, dsp_ref, ck, cv):
        step = pl.program_id(0)
        first = step == nb - 1

        @pl.when(step == 0)
        def _():
            ck[...] = jnp.zeros_like(ck)
            cv[...] = jnp.zeros_like(cv)
            dsp_ref[...] = jnp.zeros_like(dsp_ref)

        dk_cur = [None] * SWA_KV
        dk_prev = [None] * SWA_KV
        dv_cur = [None] * SWA_KV
        dv_prev = [None] * SWA_KV
        dsinks = []

        def add(lst, i, v):
            lst[i] = v if lst[i] is None else lst[i] + v

        for h in range(SWA_H):
            kv = h // group
            q, do = q_ref[:, _hs(h)], do_ref[:, _hs(h)]
            kc, kp, vc, vp = kc_ref[:, _hs(kv)], kp_ref[:, _hs(kv)], vc_ref[:, _hs(kv)], vp_ref[:, _hs(kv)]
            s_cur, s_prev = _swa_logits(q, kc, kp, _slope(h), first)
            lse_h = lse_ref[:, h:h + 1]
            dl_h = dl_ref[:, FOX_H + h:FOX_H + h + 1]
            p_cur = jnp.exp(s_cur - lse_h)
            p_prev = jnp.exp(s_prev - lse_h)
            p_sink = jnp.exp(sp_ref[R_SINK:R_SINK + 1, h:h + 1] - lse_h)
            ds_cur = p_cur * (_dot(do, vc, "nt") - dl_h)
            ds_prev = p_prev * (_dot(do, vp, "nt") - dl_h)
            dq_ref[:, _hs(h)] = (_dot(ds_cur, kc) + _dot(ds_prev, kp)) * SCALE
            add(dk_cur, kv, _dot(ds_cur, q, "tn") * SCALE)
            add(dk_prev, kv, _dot(ds_prev, q, "tn") * SCALE)
            add(dv_cur, kv, _dot(p_cur, do, "tn"))
            add(dv_prev, kv, _dot(p_prev, do, "tn"))
            dsinks.append(-jnp.sum(p_sink * dl_h, axis=0, keepdims=True))
        for kv in range(SWA_KV):
            dk_ref[:, _hs(kv)] = dk_cur[kv] + ck[:, _hs(kv)]
            dv_ref[:, _hs(kv)] = dv_cur[kv] + cv[:, _hs(kv)]
            ck[:, _hs(kv)] = dk_prev[kv]
            cv[:, _hs(kv)] = dv_prev[kv]
        lane = lax.broadcasted_iota(jnp.int32, (1, HEAD), 1)
        row = jnp.zeros((1, HEAD), F32)
        for h in range(SWA_H):
            row = jnp.where(lane == h, dsinks[h], row)
        dsp_ref[R_SINK:R_SINK + 1, :] += row

    def rev(n):
        return nb - 1 - n

    qspec = pl.BlockSpec((w, SWA_W), lambda n: (rev(n), 0))
    cur = pl.BlockSpec((w, SWA_KV_W), lambda n: (rev(n), 0))
    prev = pl.BlockSpec((w, SWA_KV_W), lambda n: (jnp.maximum(rev(n) - 1, 0), 0))
    col = pl.BlockSpec((w, HEAD), lambda n: (rev(n), 0))
    small = pl.BlockSpec((16, 128), lambda n: (0, 0))
    return pl.pallas_call(
        body, grid=(nb,),
        in_specs=[qspec, prev, cur, prev, cur, small, pl.BlockSpec((w, SWA_W), lambda n: (rev(n), do_block)), col, col],
        out_specs=[qspec, cur, cur, small],
        out_shape=[jax.ShapeDtypeStruct((t, SWA_W), F32), jax.ShapeDtypeStruct((t, SWA_KV_W), F32),
                   jax.ShapeDtypeStruct((t, SWA_KV_W), F32), jax.ShapeDtypeStruct((16, 128), F32)],
        scratch_shapes=[pltpu.VMEM((w, SWA_KV_W), F32), pltpu.VMEM((w, SWA_KV_W), F32)],
        compiler_params=_params(1), name=name)(sq, sk, sk, sv, sv, sp, dmixed, lse, delta)


def _mem_pre(mkv, sp, name):
    m = mkv.shape[0]

    def body(x_ref, sp_ref, k_ref, v_ref):
        for h in range(MEM_H):
            k_ref[:, _hs(h)] = _head_norm(x_ref[:, _hs(h)], sp_ref[R_MK:R_MK + 1, :]).astype(BF16)
        v_ref[...] = x_ref[:, MEM_W:2 * MEM_W].astype(BF16)

    out = jax.ShapeDtypeStruct((m, MEM_W), BF16)
    return pl.pallas_call(body, out_shape=[out, out], name=name)(mkv, sp)


def _mem_post_bwd(mkv, sp, dmk, dmv, name):
    m = mkv.shape[0]

    def body(x_ref, sp_ref, dk_ref, dv_ref, d_ref, dsp_ref):
        dsp_ref[...] = jnp.zeros_like(dsp_ref)
        total = None
        for h in range(MEM_H):
            dx, dg = _head_norm_bwd(x_ref[:, _hs(h)], sp_ref[R_MK:R_MK + 1, :], dk_ref[:, _hs(h)])
            d_ref[:, _hs(h)] = dx.astype(BF16)
            total = dg if total is None else total + dg
        d_ref[:, MEM_W:2 * MEM_W] = dv_ref[...].astype(BF16)
        dsp_ref[R_MK:R_MK + 1, :] = total

    return pl.pallas_call(body, out_shape=[jax.ShapeDtypeStruct((m, 2 * MEM_W), BF16),
                                           jax.ShapeDtypeStruct((16, 128), F32)], name=name)(mkv, sp, dmk, dmv)


def _mem_fwd(mq, mk, mv, name):
    t = mq.shape[0]
    m = mk.shape[0]
    tq = min(t, 512)

    def body(q_ref, k_ref, v_ref, o_ref, lse_ref):
        lses = []
        for h in range(MEM_H):
            s = _dot(q_ref[:, _hs(h)], k_ref[:, _hs(h)], "nt") * SCALE
            mx = jnp.max(s, axis=-1, keepdims=True)
            p = jnp.exp(s - mx)
            l = jnp.sum(p, axis=-1, keepdims=True)
            o_ref[:, _hs(h)] = _dot(p, v_ref[:, _hs(h)]) / l
            lses.append(mx + jnp.log(l))
        lse_ref[...] = _head_column(lses)

    qspec = pl.BlockSpec((tq, MEM_W), lambda i: (i, 0))
    kspec = pl.BlockSpec((m, MEM_W), lambda i: (0, 0))
    return pl.pallas_call(
        body, grid=(t // tq,), in_specs=[qspec, kspec, kspec],
        out_specs=[qspec, pl.BlockSpec((tq, HEAD), lambda i: (i, 0))],
        out_shape=[jax.ShapeDtypeStruct((t, MEM_W), F32), jax.ShapeDtypeStruct((t, HEAD), F32)],
        compiler_params=_params(1), name=name)(mq, mk, mv)


def _mem_bwd(mq, mk, mv, dmixed, lse, delta, name):
    t = mq.shape[0]
    m = mk.shape[0]
    tq = min(t, 512)
    do_block = (FOX_W + SWA_W) // MEM_W
    assert (FOX_W + SWA_W) % MEM_W == 0

    def body(q_ref, k_ref, v_ref, do_ref, lse_ref, dl_ref, dq_ref, dk_ref, dv_ref):
        @pl.when(pl.program_id(0) == 0)
        def _():
            dk_ref[...] = jnp.zeros_like(dk_ref)
            dv_ref[...] = jnp.zeros_like(dv_ref)

        for h in range(MEM_H):
            q, k, v, do = q_ref[:, _hs(h)], k_ref[:, _hs(h)], v_ref[:, _hs(h)], do_ref[:, _hs(h)]
            s = _dot(q, k, "nt") * SCALE
            p = jnp.exp(s - lse_ref[:, h:h + 1])
            col = FOX_H + SWA_H + h
            ds = p * (_dot(do, v, "nt") - dl_ref[:, col:col + 1])
            dq_ref[:, _hs(h)] = _dot(ds, k) * SCALE
            dk_ref[:, _hs(h)] += _dot(ds, q, "tn") * SCALE
            dv_ref[:, _hs(h)] += _dot(p, do, "tn")

    qspec = pl.BlockSpec((tq, MEM_W), lambda i: (i, 0))
    kspec = pl.BlockSpec((m, MEM_W), lambda i: (0, 0))
    col = pl.BlockSpec((tq, HEAD), lambda i: (i, 0))
    return pl.pallas_call(
        body, grid=(t // tq,),
        in_specs=[qspec, kspec, kspec, pl.BlockSpec((tq, MEM_W), lambda i: (i, do_block)), col, col],
        out_specs=[qspec, kspec, kspec],
        out_shape=[jax.ShapeDtypeStruct((t, MEM_W), F32), jax.ShapeDtypeStruct((m, MEM_W), F32),
                   jax.ShapeDtypeStruct((m, MEM_W), F32)],
        compiler_params=_params(1), name=name)(mq, mk, mv, dmixed, lse, delta)


def _all_gather(xs, name):
    n = len(xs)

    def body(*refs):
        x_refs, o_refs = refs[:n], refs[n:2 * n]
        send_sems, recv_sems, local_sems = refs[2 * n:]
        x, y, c = _me()
        me, sibling = (x, y, c), (x, y, 1 - c)
        chips = [(1 - x, y), (x, 1 - y), (1 - x, 1 - y)]

        def copy(a, k, block, to, src=None):
            slot = o_refs[a].at[_lin(block)]
            return pltpu.make_async_remote_copy(
                src_ref=slot if src is None else src, dst_ref=slot, send_sem=send_sems.at[a, k],
                recv_sem=recv_sems.at[a, k], device_id=to, device_id_type=MESH)

        mine = [pltpu.make_async_copy(x_refs[a], o_refs[a].at[_lin(me)], local_sems.at[a]) for a in range(n)]
        for cp in mine:
            cp.start()
        first = []
        for a in range(n):
            first.append(copy(a, 0, me, sibling, src=x_refs[a]))
            first += [copy(a, 1 + j, me, (*chip, c), src=x_refs[a]) for j, chip in enumerate(chips)]
        for cp in first:
            cp.start()
        passed = []
        for j, chip in enumerate(chips):
            for a in range(n):
                copy(a, 1 + j, (*chip, c), me).wait_recv()
                cp = copy(a, 4 + j, (*chip, c), sibling)
                cp.start()
                passed.append(cp)
        for a in range(n):
            copy(a, 0, sibling, me).wait_recv()
            for j, chip in enumerate(chips):
                copy(a, 4 + j, (*chip, 1 - c), me).wait_recv()
        for cp in first + passed:
            cp.wait_send()
        for cp in mine:
            cp.wait()

    return pl.pallas_call(
        body, in_specs=[ANY] * n, out_specs=[ANY] * n,
        out_shape=[jax.ShapeDtypeStruct((N_DEV,) + x.shape, x.dtype) for x in xs],
        scratch_shapes=[pltpu.SemaphoreType.DMA((n, 7)), pltpu.SemaphoreType.DMA((n, 7)),
                        pltpu.SemaphoreType.DMA((n,))],
        name=name)(*xs)


def _peers():
    x, y, c = _me()
    out = []
    for k in range(1, N_DEV):
        kx, ky, kc = (k >> 2) & 1, (k >> 1) & 1, k & 1
        out.append(((1 - x) if kx else x, (1 - y) if ky else y, (1 - c) if kc else c))
    return out


def _all_reduce_small(xs, name):
    n = len(xs)

    def body(*refs):
        x_refs, o_refs = refs[:n], refs[n:2 * n]
        bufs = refs[2 * n:3 * n]
        send_sems, recv_sems = refs[3 * n:]
        me = _lin(_me())
        peers = _peers()
        for a in range(n):
            bufs[a][me] = x_refs[a][...]
        sends = []
        for a in range(n):
            for k, peer in enumerate(peers):
                sends.append(pltpu.make_async_remote_copy(
                    src_ref=bufs[a].at[me], dst_ref=bufs[a].at[me], send_sem=send_sems.at[a, k],
                    recv_sem=recv_sems.at[a, k], device_id=peer, device_id_type=MESH))
        for cp in sends:
            cp.start()
        for a in range(n):
            for k, peer in enumerate(peers):
                pltpu.make_async_remote_copy(
                    src_ref=bufs[a].at[me], dst_ref=bufs[a].at[_lin(peer)], send_sem=send_sems.at[a, k],
                    recv_sem=recv_sems.at[a, k], device_id=peer, device_id_type=MESH).wait_recv()
        for cp in sends:
            cp.wait_send()
        for a in range(n):
            total = bufs[a][0]
            for q in range(1, N_DEV):
                total = total + bufs[a][q]
            o_refs[a][...] = total

    vmem = pl.BlockSpec(memory_space=pltpu.VMEM)
    return pl.pallas_call(
        body, in_specs=[vmem] * n, out_specs=[vmem] * n,
        out_shape=[jax.ShapeDtypeStruct(x.shape, F32) for x in xs],
        scratch_shapes=[pltpu.VMEM((N_DEV,) + x.shape, F32) for x in xs]
        + [pltpu.SemaphoreType.DMA((n, 7)), pltpu.SemaphoreType.DMA((n, 7))],
        name=name)(*xs)


def _pair_add(part, got, name):
    _, rows, cols = part.shape
    tm = _rows_tile(rows, cols * 2, budget=2 << 20)
    core = jnp.reshape(lax.axis_index("c"), (1,)).astype(jnp.int32)

    def body(c_ref, p_ref, g_ref, o_ref):
        o_ref[...] = (p_ref[...].astype(F32) + g_ref[...].astype(F32)).astype(BF16)

    spec = pl.BlockSpec((None, tm, cols), lambda q, i, c: (q, i, 0))
    grid_spec = pltpu.PrefetchScalarGridSpec(
        num_scalar_prefetch=1, grid=(4, rows // tm),
        in_specs=[pl.BlockSpec((None, tm, cols), lambda q, i, c: (2 * q + c[0], i, 0)), spec], out_specs=spec)
    return pl.pallas_call(body, grid_spec=grid_spec, out_shape=jax.ShapeDtypeStruct((4, rows, cols), BF16),
                          compiler_params=_params(2), name=name)(core, part, got)


def _adam_math(w, g, m, v):
    nm = ADAM_B1 * m + (1.0 - ADAM_B1) * g
    nv = ADAM_B2 * v + (1.0 - ADAM_B2) * (g * g)
    m_hat = nm / (1.0 - ADAM_B1 ** ADAM_STEP)
    v_hat = nv / (1.0 - ADAM_B2 ** ADAM_STEP)
    return -ADAM_LR * (m_hat / (jnp.sqrt(v_hat) + ADAM_EPS) + ADAM_WD * w), nm, nv


def _sum_chips(got, name):
    _, rows, cols = got.shape
    tm = _rows_tile(rows, cols * 2 * 4, budget=2 << 20)

    def body(r_ref, o_ref):
        o_ref[...] = ((r_ref[0].astype(F32) + r_ref[1].astype(F32)) + r_ref[2].astype(F32)) + r_ref[3].astype(F32)

    return pl.pallas_call(
        body, grid=(rows // tm,), in_specs=[pl.BlockSpec((4, tm, cols), lambda i: (0, i, 0))],
        out_specs=pl.BlockSpec((tm, cols), lambda i: (i, 0)), out_shape=jax.ShapeDtypeStruct((rows, cols), F32),
        compiler_params=_params(1), name=name)(got)


def _sum_adamw(got, col_block, w, m, v, name):
    _, rows, cols = w.shape
    tm = _rows_tile(rows, cols * 4, budget=1 << 20)

    def body(r_ref, w_ref, m_ref, v_ref, g_ref, d_ref, nm_ref, nv_ref):
        g = ((r_ref[0].astype(F32) + r_ref[1].astype(F32)) + r_ref[2].astype(F32)) + r_ref[3].astype(F32)
        g_ref[...] = g
        d_ref[...], nm_ref[...], nv_ref[...] = _adam_math(w_ref[...], g, m_ref[...], v_ref[...])

    spec = pl.BlockSpec((None, tm, cols), lambda i: (0, i, 0))
    out = jax.ShapeDtypeStruct(w.shape, F32)
    return pl.pallas_call(
        body, grid=(rows // tm,), in_specs=[pl.BlockSpec((4, tm, cols), lambda i: (0, i, col_block)), spec, spec, spec],
        out_specs=[spec] * 4, out_shape=[out] * 4, compiler_params=_params(1), name=name)(got, w, m, v)


def _adamw(w, g, m, v, name):
    rows, cols = w.shape
    tm = _rows_tile(rows, cols * 4, budget=2 << 20, mult=8)

    def body(w_ref, g_ref, m_ref, v_ref, d_ref, nm_ref, nv_ref):
        d_ref[...], nm_ref[...], nv_ref[...] = _adam_math(w_ref[...], g_ref[...], m_ref[...], v_ref[...])

    spec = pl.BlockSpec((tm, cols), lambda i: (i, 0))
    out = jax.ShapeDtypeStruct(w.shape, F32)
    return pl.pallas_call(body, grid=(rows // tm,), in_specs=[spec] * 4, out_specs=[spec] * 3,
                          out_shape=[out] * 3, compiler_params=_params(1), name=name)(w, g, m, v)


def _permute_in(w):
    logit0 = 3 * FOX_W
    pad = jnp.zeros(w.shape[:-1] + (HEAD - N_LOGIT,), w.dtype)
    return jnp.concatenate([w[..., :logit0], w[..., logit0 + N_LOGIT:], w[..., logit0:logit0 + N_LOGIT], pad], axis=-1)


def _unpermute_in(w):
    logit0 = 3 * FOX_W
    return jnp.concatenate([w[..., :logit0], w[..., C_FL:C_FL + N_LOGIT], w[..., logit0:C_FL]], axis=-1)


def _pad_row(v, width):
    return jnp.pad(v, ((0, 0), (0, width - v.shape[1])))


def _pack_small(fq, fk, sq, sk, mq, mk, fb, sinks):
    rows = [fq, fk, sq, sk, mq, mk, _pad_row(fb, HEAD), _pad_row(sinks, HEAD)]
    return jnp.concatenate(rows + [jnp.zeros((8, HEAD), F32)], axis=0)


def _pack_norms(a, b, c, d):
    return jnp.concatenate([a, b, c, d, jnp.zeros((4, a.shape[1]), F32)], axis=0)


def kernel(x, mem, ffn1_norm, ffn1_gate, ffn1_up, ffn1_down, mix_norm, mem_norm, w_in, forget_bias, w_mem_k, w_mem_v, fox_q_gain, fox_k_gain, swa_q_gain, swa_k_gain, swa_sinks, mem_q_gain, mem_k_gain, w_out, ffn2_norm, ffn2_gate, ffn2_up, ffn2_down, loss_target, m_ffn1_norm, m_ffn1_gate, m_ffn1_up, m_ffn1_down, m_mix_norm, m_mem_norm, m_w_in, m_forget_bias, m_w_mem_k, m_w_mem_v, m_fox_q_gain, m_fox_k_gain, m_swa_q_gain, m_swa_k_gain, m_swa_sinks, m_mem_q_gain, m_mem_k_gain, m_w_out, m_ffn2_norm, m_ffn2_gate, m_ffn2_up, m_ffn2_down, v_ffn1_norm, v_ffn1_gate, v_ffn1_up, v_ffn1_down, v_mix_norm, v_mem_norm, v_w_in, v_forget_bias, v_w_mem_k, v_w_mem_v, v_fox_q_gain, v_fox_k_gain, v_swa_q_gain, v_swa_k_gain, v_swa_sinks, v_mem_q_gain, v_mem_k_gain, v_w_out, v_ffn2_norm, v_ffn2_gate, v_ffn2_up, v_ffn2_down):
    x0 = x[0]
    mem0 = mem[0]
    target = loss_target[0]
    t, d = x0.shape
    d_shard = w_in.shape[1]
    m_len = mem0.shape[0]
    tm = min(t, 512)
    tk = min(t, 512)
    tn = IN_W // 3
    tkw, tnw = min(t, 1024), IN_W // 3

    def swap(a):
        return jnp.swapaxes(a, 1, 2)

    gate1, up1, gate2, up2 = swap(ffn1_gate), swap(ffn1_up), swap(ffn2_gate), swap(ffn2_up)

    local = {
        "g1": gate1[0], "u1": up1[0], "d1": ffn1_down[0],
        "g2": gate2[0], "u2": up2[0], "d2": ffn2_down[0],
        "in": _permute_in(w_in[0]), "out": w_out[0],
        "mkv": jnp.concatenate([w_mem_k[0], w_mem_v[0]], axis=1),
    }
    shard = {k: _cast_bf16(v, f"cast_{k}") for k, v in local.items()}
    sp = _pack_small(fox_q_gain, fox_k_gain, swa_q_gain, swa_k_gain, mem_q_gain, mem_k_gain, forget_bias, swa_sinks)
    wt = {}

    wt["g1"], wt["u1"] = _all_gather([shard["g1"], shard["u1"]], "gather_ffn1_in")
    xn1 = _rms_fwd(x0, ffn1_norm, "ffn1_norm")
    (g1, u1, h1), (half,) = _ffn_up(xn1, wt["g1"], wt["u1"], "ffn1",
                                    rides=[_ride_gather_chips([shard["d1"], shard["in"]])])
    (wt["d1"], wt["in"]), = _only_copies("gather_d1_in_sibling", [_ride_gather_sibling(half)])
    x1, (half,) = _ffn_down(x0, h1, wt["d1"], "ffn1",
                            rides=[_ride_gather_chips([shard["out"], shard["mkv"], shard["g2"]])])
    w_in_full = wt["in"].reshape(d, IN_W)

    hn = _rms_fwd(x1, mix_norm, "mix_norm")
    proj, ((wt["out"], wt["mkv"], wt["g2"]), half) = _mm(
        "proj", [(hn, pl.BlockSpec((tm, d), lambda n, i, k: (i, 0)),
                  w_in_full, pl.BlockSpec((d, tn), lambda n, i, k: (0, n)))],
        "nn", (3, t // tm, 1), jax.ShapeDtypeStruct((t, IN_W), F32), pl.BlockSpec((tm, tn), lambda n, i, k: (i, n)),
        rides=[_ride_gather_sibling(half), _ride_gather_chips([shard["u2"]])])
    w_out_full = wt["out"].reshape(d, d)
    w_mkv_full = wt["mkv"].reshape(d, 2 * MEM_W)
    fq, fk, fv, sq, sk, sv, mq, c_col = _attn_pre(proj, sp, "attn_pre")
    c_row = jnp.transpose(c_col[:, :8])

    mn = _rms_fwd(mem0, mem_norm, "mem_norm")
    mkv = _mm("mem_kv", [(mn, pl.BlockSpec((m_len, d), lambda k: (0, 0)),
                          w_mkv_full, pl.BlockSpec((d, 2 * MEM_W), lambda k: (0, 0)))],
              "nn", (1,), jax.ShapeDtypeStruct((m_len, 2 * MEM_W), F32),
              pl.BlockSpec((m_len, 2 * MEM_W), lambda k: (0, 0)))
    mk, mv = _mem_pre(mkv, sp, "mem_pre")

    (o_a, lse_a), ((wt["u2"],), half) = _fox_fwd(
        fq, fk, fv, c_col, c_row, "fox_fwd", rides=[_ride_gather_sibling(half), _ride_gather_chips([shard["d2"]])])
    o_b, lse_b = _swa_fwd(sq, sk, sv, sp, "swa_fwd")
    o_c, lse_c = _mem_fwd(mq, mk, mv, "mem_fwd")

    def rows_spec(width):
        return pl.BlockSpec((tm, width), lambda i, k: (i, 0))

    def wout_rows(first, width):
        assert first % width == 0
        return pl.BlockSpec((width, d), lambda i, k: (first // width, 0))

    xspec = pl.BlockSpec((tm, d), lambda i, k: (i, 0))
    x2, ((wt["d2"],),) = _mm(
        "mix_out",
        [(o_a, rows_spec(FOX_W), w_out_full, wout_rows(0, FOX_W)),
         (o_b, rows_spec(SWA_W), w_out_full, wout_rows(FOX_W, SWA_W)),
         (o_c, rows_spec(MEM_W), w_out_full, wout_rows(FOX_W + SWA_W, MEM_W))],
        "nn", (t // tm, 1), jax.ShapeDtypeStruct((t, d), F32), xspec, res=x1, res_spec=xspec,
        rides=[_ride_gather_sibling(half)])

    xn2 = _rms_fwd(x2, ffn2_norm, "ffn2_norm")
    (g2, u2, h2), _ = _ffn_up(xn2, wt["g2"], wt["u2"], "ffn2")
    x3 = _ffn_down(x2, h2, wt["d2"], "ffn2")

    dy, dyb, sq_err = _loss_head(x3, target, "loss_head")
    loss = lax.psum(0.5 * sq_err[0, 0] / d, ("x", "y", "c"))

    got = {}
    paired = {}
    landed = {}

    def pair(k, part):
        paired[k] = _pair_add(part, got[k], f"pair_{k}")

    (dg2, du2), _ = _ffn_dact(dyb, wt["d2"], g2, u2, "ffn2")
    part_d2 = _ffn_dw(h2, dyb, 0.5, "ffn2_dwd")
    part_g2, ((got["d2"],),) = _ffn_dw(dg2, xn2, 1.0, "ffn2_dwg", rides=[_ride_scatter_sibling([part_d2])])
    pair("d2", part_d2)
    part_u2, ((landed["d2"],), (got["g2"],)) = _ffn_dw(
        du2, xn2, 1.0, "ffn2_dwu", rides=[_ride_scatter_chips([paired["d2"]]), _ride_scatter_sibling([part_g2])])
    pair("g2", part_g2)
    dxn2, ((landed["g2"],), (got["u2"],)) = _ffn_dxn(
        dg2, du2, wt["g2"], wt["u2"], "ffn2_dxn",
        rides=[_ride_scatter_chips([paired["g2"]]), _ride_scatter_sibling([part_u2])])
    pair("u2", part_u2)
    dx2, dx2b, dgain_ffn2 = _rms_bwd(x2, ffn2_norm, dxn2, dy, "ffn2_norm_bwd")

    dmixed = _mm("mix_out_dx", [(dx2b, xspec, w_out_full, pl.BlockSpec((d, d), lambda i, k: (0, 0)))],
                 "nt", (t // tm, 1), jax.ShapeDtypeStruct((t, d), F32), xspec)

    def k_rows(width):
        return pl.BlockSpec((tk, width), lambda j, k: (k, 0))

    part_out = [
        _mm(f"mix_out_dw{i}", [(o, k_rows(width), dx2b, k_rows(d))], "tn", (1, t // tk),
            jax.ShapeDtypeStruct((width, d), BF16), pl.BlockSpec((width, d), lambda j, k: (0, 0)))
        for i, (o, width) in enumerate(((o_a, FOX_W), (o_b, SWA_W), (o_c, MEM_W)))
    ]
    part_out = jnp.concatenate(part_out, axis=0).reshape(N_DEV, d_shard, d)

    delta = _delta(dmixed, o_a, o_b, o_c, "attn_delta")
    (dfq, dfk, dfv, dc_col, dc_row), ((landed["u2"],),) = _fox_bwd(
        fq, fk, fv, c_col, c_row, dmixed, lse_a, delta, "fox_bwd", rides=[_ride_scatter_chips([paired["u2"]])])
    dsq, dsk, dsv, dsp_sink = _swa_bwd(sq, sk, sv, sp, dmixed, lse_b, delta, "swa_bwd")
    dmq, dmk, dmv = _mem_bwd(mq, mk, mv, dmixed, lse_c, delta, "mem_bwd")

    dmkv, dsp_mem = _mem_post_bwd(mkv, sp, dmk, dmv, "mem_post_bwd")
    part_mkv = _mm("mem_kv_dw", [(mn, pl.BlockSpec((m_len, d), lambda k: (0, 0)),
                                  dmkv, pl.BlockSpec((m_len, 2 * MEM_W), lambda k: (0, 0)))],
                   "tn", (1,), jax.ShapeDtypeStruct((d, 2 * MEM_W), BF16),
                   pl.BlockSpec((d, 2 * MEM_W), lambda k: (0, 0))).reshape(N_DEV, d_shard, 2 * MEM_W)
    dmn = _mm("mem_kv_dx", [(dmkv, pl.BlockSpec((m_len, 2 * MEM_W), lambda k: (0, 0)),
                             w_mkv_full, pl.BlockSpec((d, 2 * MEM_W), lambda k: (0, 0)))],
              "nt", (1,), jax.ShapeDtypeStruct((m_len, d), F32), pl.BlockSpec((m_len, d), lambda k: (0, 0)))
    _, _, dgain_mem = _rms_bwd(mem0, mem_norm, dmn, None, "mem_norm_bwd")

    dc_row_t = _pad_row(jnp.transpose(dc_row), HEAD)
    dproj, dsp_attn = _attn_post_bwd(proj, sp, dfq, dfk, dfv, dsq, dsk, dsv, dmq, dc_col, dc_row_t, "attn_post_bwd")
    dhn, ((got["out"], got["mkv"]),) = _mm(
        "proj_dx", [(dproj, pl.BlockSpec((tm, IN_W), lambda i, k: (i, 0)),
                     w_in_full, pl.BlockSpec((d, IN_W), lambda i, k: (0, 0)))],
        "nt", (t // tm, 1), jax.ShapeDtypeStruct((t, d), F32), xspec,
        rides=[_ride_scatter_sibling([part_out, part_mkv])])
    pair("out", part_out)
    pair("mkv", part_mkv)
    part_in, ((landed["out"], landed["mkv"]),) = _mm(
        "proj_dw", [(hn, pl.BlockSpec((tkw, d), lambda n, k: (k, 0)),
                     dproj, pl.BlockSpec((tkw, tnw), lambda n, k: (k, n)))],
        "tn", (IN_W // tnw, t // tkw), jax.ShapeDtypeStruct((d, IN_W), BF16), pl.BlockSpec((d, tnw), lambda n, k: (0, n)),
        rides=[_ride_scatter_chips([paired["out"], paired["mkv"]])])
    part_in = part_in.reshape(N_DEV, d_shard, IN_W)
    dx1, dx1b, dgain_mix = _rms_bwd(x1, mix_norm, dhn, dx2, "mix_norm_bwd")

    (dg1, du1), ((got["in"],),) = _ffn_dact(dx1b, wt["d1"], g1, u1, "ffn1", rides=[_ride_scatter_sibling([part_in])])
    pair("in", part_in)
    part_d1, ((landed["in"],),) = _ffn_dw(h1, dx1b, 0.5, "ffn1_dwd", rides=[_ride_scatter_chips([paired["in"]])])
    part_g1, ((got["d1"],),) = _ffn_dw(dg1, xn1, 1.0, "ffn1_dwg", rides=[_ride_scatter_sibling([part_d1])])
    pair("d1", part_d1)
    part_u1, ((landed["d1"],), (got["g1"],)) = _ffn_dw(
        du1, xn1, 1.0, "ffn1_dwu", rides=[_ride_scatter_chips([paired["d1"]]), _ride_scatter_sibling([part_g1])])
    pair("g1", part_g1)
    dxn1, ((landed["g1"],), (got["u1"],)) = _ffn_dxn(
        dg1, du1, wt["g1"], wt["u1"], "ffn1_dxn",
        rides=[_ride_scatter_chips([paired["g1"]]), _ride_scatter_sibling([part_u1])])
    pair("u1", part_u1)
    grad_x, _, dgain_ffn1 = _rms_bwd(x0, ffn1_norm, dxn1, dx1, "ffn1_norm_bwd")
    (landed["u1"],), = _only_copies("scatter_last", [_ride_scatter_chips([paired["u1"]])])

    norms_sum, small_sum = _all_reduce_small(
        [_pack_norms(dgain_ffn1, dgain_mix, dgain_mem, dgain_ffn2), dsp_attn + dsp_sink + dsp_mem], "reduce_small")

    result = {
        "ffn1_gate": map(swap, _sum_adamw(landed["g1"], 0, gate1, swap(m_ffn1_gate), swap(v_ffn1_gate), "adamw_ffn1_gate")),
        "ffn1_up": map(swap, _sum_adamw(landed["u1"], 0, up1, swap(m_ffn1_up), swap(v_ffn1_up), "adamw_ffn1_up")),
        "ffn1_down": _sum_adamw(landed["d1"], 0, ffn1_down, m_ffn1_down, v_ffn1_down, "adamw_ffn1_down"),
        "w_mem_k": _sum_adamw(landed["mkv"], 0, w_mem_k, m_w_mem_k, v_w_mem_k, "adamw_w_mem_k"),
        "w_mem_v": _sum_adamw(landed["mkv"], 1, w_mem_v, m_w_mem_v, v_w_mem_v, "adamw_w_mem_v"),
        "w_out": _sum_adamw(landed["out"], 0, w_out, m_w_out, v_w_out, "adamw_w_out"),
        "ffn2_gate": map(swap, _sum_adamw(landed["g2"], 0, gate2, swap(m_ffn2_gate), swap(v_ffn2_gate), "adamw_ffn2_gate")),
        "ffn2_up": map(swap, _sum_adamw(landed["u2"], 0, up2, swap(m_ffn2_up), swap(v_ffn2_up), "adamw_ffn2_up")),
        "ffn2_down": _sum_adamw(landed["d2"], 0, ffn2_down, m_ffn2_down, v_ffn2_down, "adamw_ffn2_down"),
    }
    grad_in = _unpermute_in(_sum_chips(landed["in"], "sum_w_in"))
    result["w_in"] = (grad_in[None],) + tuple(
        o[None] for o in _adamw(w_in[0], grad_in, m_w_in[0], v_w_in[0], "adamw_w_in"))

    norm_names = ["ffn1_norm", "mix_norm", "mem_norm", "ffn2_norm"]
    norm_w = _pack_norms(ffn1_norm, mix_norm, mem_norm, ffn2_norm)
    norm_m = _pack_norms(m_ffn1_norm, m_mix_norm, m_mem_norm, m_ffn2_norm)
    norm_v = _pack_norms(v_ffn1_norm, v_mix_norm, v_mem_norm, v_ffn2_norm)
    outs = (norms_sum,) + tuple(_adamw(norm_w, norms_sum, norm_m, norm_v, "adamw_norms"))
    for i, k in enumerate(norm_names):
        result[k] = tuple(o[i:i + 1] for o in outs)

    small_names = ["fox_q_gain", "fox_k_gain", "swa_q_gain", "swa_k_gain", "mem_q_gain", "mem_k_gain",
                   "forget_bias", "swa_sinks"]
    small_m = _pack_small(m_fox_q_gain, m_fox_k_gain, m_swa_q_gain, m_swa_k_gain, m_mem_q_gain, m_mem_k_gain,
                          m_forget_bias, m_swa_sinks)
    small_v = _pack_small(v_fox_q_gain, v_fox_k_gain, v_swa_q_gain, v_swa_k_gain, v_mem_q_gain, v_mem_k_gain,
                          v_forget_bias, v_swa_sinks)
    outs = (small_sum,) + tuple(_adamw(sp, small_sum, small_m, small_v, "adamw_small"))
    for i, k in enumerate(small_names):
        width = N_LOGIT if k in ("forget_bias", "swa_sinks") else HEAD
        result[k] = tuple(o[i:i + 1, :width] for o in outs)

    order = ["ffn1_norm", "ffn1_gate", "ffn1_up", "ffn1_down", "mix_norm", "mem_norm", "w_in", "forget_bias",
             "w_mem_k", "w_mem_v", "fox_q_gain", "fox_k_gain", "swa_q_gain", "swa_k_gain", "swa_sinks",
             "mem_q_gain", "mem_k_gain", "w_out", "ffn2_norm", "ffn2_gate", "ffn2_up", "ffn2_down"]
    result = {k: tuple(v) for k, v in result.items()}
    flat = [loss, grad_x[None]]
    for kind in range(4):
        flat += [result[k][kind] for k in order]
    return tuple(flat)
```

```python
import functools

import jax
import jax.numpy as jnp
from jax import lax
from jax.experimental import pallas as pl
from jax.experimental.pallas import tpu as pltpu

F32 = jnp.float32
BF16 = jnp.bfloat16
MESH = pl.DeviceIdType.MESH
ANY = pl.BlockSpec(memory_space=pl.ANY)

N_DEV = 8
EPS = 1e-6
NEG_INF = -1e30
HEAD = 128
FOX_H, SWA_H, SWA_KV, MEM_H = 6, 6, 2, 4
FOX_W, SWA_W, SWA_KV_W, MEM_W = FOX_H * HEAD, SWA_H * HEAD, SWA_KV * HEAD, MEM_H * HEAD
SCALE = HEAD ** -0.5
SWA_BLOCK = 128
C_FQ, C_FK, C_FV = 0, FOX_W, 2 * FOX_W
C_SQ = 3 * FOX_W
C_SK = C_SQ + SWA_W
C_SV = C_SK + SWA_KV_W
C_MQ = C_SV + SWA_KV_W
C_FL = C_MQ + MEM_W
IN_W = C_FL + HEAD
N_LOGIT = FOX_H
R_FQ, R_FK, R_SQ, R_SK, R_MQ, R_MK, R_FB, R_SINK = range(8)
ADAM_LR, ADAM_B1, ADAM_B2, ADAM_EPS, ADAM_WD, ADAM_STEP = 0.001, 0.9, 0.999, 1e-08, 0.01, 10
VMEM_BYTES = 56 * 1024 * 1024

DN = {
    "nn": (((1,), (0,)), ((), ())),
    "nt": (((1,), (1,)), ((), ())),
    "tn": (((0,), (0,)), ((), ())),
}


def _params(n_axes):
    return pltpu.CompilerParams(dimension_semantics=("arbitrary",) * n_axes, vmem_limit_bytes=VMEM_BYTES)


def _dot(a, b, dims="nn"):
    return lax.dot_general(a.astype(BF16), b.astype(BF16), DN[dims], preferred_element_type=F32)


def _sigmoid(x):
    return 0.5 * jnp.tanh(0.5 * x) + 0.5


def _me():
    return lax.axis_index("x"), lax.axis_index("y"), lax.axis_index("c")


def _lin(p):
    return 4 * p[0] + 2 * p[1] + p[2]


def _rows_tile(rows, row_bytes, budget=4 << 20, mult=16):
    best = None
    for k in range(1, rows + 1):
        if rows % k == 0 and (rows // k) % mult == 0 and (rows // k) * row_bytes <= budget:
            best = rows // k
            break
    assert best is not None, (rows, row_bytes)
    return best


class _Ride:
    def __init__(self, inputs, out_shapes, aliases, n_remote, n_local, start, wait):
        self.inputs, self.out_shapes, self.aliases = list(inputs), list(out_shapes), dict(aliases)
        self.n_remote, self.n_local, self.start, self.wait = n_remote, n_local, start, wait


def _remote(src, dst, send, recv, k, to):
    return pltpu.make_async_remote_copy(src_ref=src, dst_ref=dst, send_sem=send.at[k], recv_sem=recv.at[k],
                                        device_id=to, device_id_type=MESH)


def _other_chips(x, y):
    return [(1 - x, y), (x, 1 - y), (1 - x, 1 - y)]


ALL_CHIPS = [(0, 0), (0, 1), (1, 0), (1, 1)]


def _ride_gather_chips(xs):
    n = len(xs)

    def copies(ins, outs, send, recv):
        x, y, c = _me()
        out = []
        for a in range(n):
            for j, chip in enumerate(_other_chips(x, y)):
                peer = (*chip, c)
                out.append((_remote(ins[a], outs[a].at[_lin((x, y, c))], send, recv, 3 * a + j, peer),
                            _remote(ins[a], outs[a].at[_lin(peer)], send, recv, 3 * a + j, peer)))
        return out

    def mine(ins, outs, local):
        me = _lin(_me())
        return [pltpu.make_async_copy(ins[a], outs[a].at[me], local.at[a]) for a in range(n)]

    def start(ins, outs, send, recv, local):
        for cp in mine(ins, outs, local):
            cp.start()
        for sent, _ in copies(ins, outs, send, recv):
            sent.start()

    def wait(ins, outs, send, recv, local):
        for sent, landed in copies(ins, outs, send, recv):
            landed.wait_recv()
            sent.wait_send()
        for cp in mine(ins, outs, local):
            cp.wait()

    shapes = [jax.ShapeDtypeStruct((N_DEV,) + x.shape, x.dtype) for x in xs]
    return _Ride(xs, shapes, {}, 3 * n, n, start, wait)


def _ride_gather_sibling(bufs):
    n = len(bufs)

    def copies(outs, send, recv):
        x, y, c = _me()
        out = []
        for a in range(n):
            for q, (px, py) in enumerate(ALL_CHIPS):
                there = outs[a].at[4 * px + 2 * py + c]
                here = outs[a].at[4 * px + 2 * py + 1 - c]
                out.append((_remote(there, there, send, recv, 4 * a + q, (x, y, 1 - c)),
                            _remote(here, here, send, recv, 4 * a + q, (x, y, 1 - c))))
        return out

    def start(ins, outs, send, recv, local):
        for sent, _ in copies(outs, send, recv):
            sent.start()

    def wait(ins, outs, send, recv, local):
        for sent, landed in copies(outs, send, recv):
            landed.wait_recv()
            sent.wait_send()

    shapes = [jax.ShapeDtypeStruct(b.shape, b.dtype) for b in bufs]
    return _Ride(bufs, shapes, {a: a for a in range(n)}, 4 * n, 0, start, wait)


def _ride_scatter_sibling(parts):
    n = len(parts)

    def copies(ins, outs, send, recv):
        x, y, c = _me()
        out = []
        for a in range(n):
            for q, (px, py) in enumerate(ALL_CHIPS):
                cp = _remote(ins[a].at[4 * px + 2 * py + 1 - c], outs[a].at[q], send, recv, 4 * a + q, (x, y, 1 - c))
                out.append(cp)
        return out

    def start(ins, outs, send, recv, local):
        for cp in copies(ins, outs, send, recv):
            cp.start()

    def wait(ins, outs, send, recv, local):
        for cp in copies(ins, outs, send, recv):
            cp.wait_recv()
            cp.wait_send()

    shapes = [jax.ShapeDtypeStruct((4,) + p.shape[1:], p.dtype) for p in parts]
    return _Ride(parts, shapes, {}, 4 * n, 0, start, wait)


def _ride_scatter_chips(pairs):
    n = len(pairs)

    def copies(ins, outs, send, recv):
        x, y, c = _me()
        out = []
        for a in range(n):
            for j, (px, py) in enumerate(_other_chips(x, y)):
                peer = (px, py, c)
                out.append((_remote(ins[a].at[2 * px + py], outs[a].at[2 * x + y], send, recv, 3 * a + j, peer),
                            _remote(ins[a].at[2 * px + py], outs[a].at[2 * px + py], send, recv, 3 * a + j, peer)))
        return out

    def mine(ins, outs, local):
        x, y, _ = _me()
        return [pltpu.make_async_copy(ins[a].at[2 * x + y], outs[a].at[2 * x + y], local.at[a]) for a in range(n)]

    def start(ins, outs, send, recv, local):
        for cp in mine(ins, outs, local):
            cp.start()
        for sent, _ in copies(ins, outs, send, recv):
            sent.start()

    def wait(ins, outs, send, recv, local):
        for sent, landed in copies(ins, outs, send, recv):
            landed.wait_recv()
            sent.wait_send()
        for cp in mine(ins, outs, local):
            cp.wait()

    shapes = [jax.ShapeDtypeStruct(p.shape, p.dtype) for p in pairs]
    return _Ride(pairs, shapes, {}, 3 * n, n, start, wait)


def _call(name, body, grid, in_specs, out_specs, out_shape, operands, scratch=(), rides=()):
    n_in, n_out, n_scr = len(operands), len(out_shape), len(scratch)
    ride_in, ride_out, ride_scr, aliases, spans = [], [], [], {}, []
    for r in rides:
        for i, o in r.aliases.items():
            aliases[n_in + len(ride_in) + i] = n_out + len(ride_out) + o
        spans.append((len(ride_in), len(r.inputs), len(ride_out), len(r.out_shapes)))
        ride_in += r.inputs
        ride_out += r.out_shapes
        ride_scr += [pltpu.SemaphoreType.DMA((r.n_remote,)), pltpu.SemaphoreType.DMA((r.n_remote,)),
                     pltpu.SemaphoreType.DMA((max(r.n_local, 1),))]

    def wrapped(*refs):
        c_in, r_in = refs[:n_in], refs[n_in:n_in + len(ride_in)]
        p = n_in + len(ride_in)
        c_out, r_out = refs[p:p + n_out], refs[p + n_out:p + n_out + len(ride_out)]
        p += n_out + len(ride_out)
        c_scr, r_scr = refs[p:p + n_scr], refs[p + n_scr:]

        def each(method, at):
            def run():
                for k, (r, (i0, ni, o0, no)) in enumerate(zip(rides, spans)):
                    getattr(r, method)(r_in[i0:i0 + ni], r_out[o0:o0 + no], *r_scr[3 * k:3 * k + 3])
            if not rides:
                return
            if grid:
                cond = functools.reduce(jnp.logical_and, [pl.program_id(ax) == at(n) for ax, n in enumerate(grid)])
                pl.when(cond)(run)
            else:
                run()

        each("start", lambda n: 0)
        body(*c_in, *c_out, *c_scr)
        each("wait", lambda n: n - 1)

    outs = pl.pallas_call(
        wrapped, grid=grid, in_specs=list(in_specs) + [ANY] * len(ride_in),
        out_specs=list(out_specs) + [ANY] * len(ride_out), out_shape=list(out_shape) + ride_out,
        scratch_shapes=list(scratch) + ride_scr, input_output_aliases=aliases,
        compiler_params=_params(len(grid)), name=name)(*operands, *ride_in)
    outs = list(outs)
    ride_results = [outs[n_out + o0:n_out + o0 + no] for (_, _, o0, no) in spans]
    return outs[:n_out], ride_results


def _only_copies(name, rides):
    return _call(name, lambda: None, (), [], [], [], [], rides=rides)[1]


def _mm(name, pairs, dims, grid, out_shape, out_spec, res=None, res_spec=None, alpha=1.0, rides=()):
    n = len(pairs)
    nk = grid[-1]
    kax = len(grid) - 1
    acc_shape = tuple(d for d in out_spec.block_shape if d is not None)

    def body(*refs):
        pos = 2 * n
        r_ref = None
        if res is not None:
            r_ref = refs[pos]
            pos += 1
        o_ref = refs[pos]
        part = None
        for p in range(n):
            d = _dot(refs[2 * p][...], refs[2 * p + 1][...], dims)
            part = d if part is None else part + d

        def finish(acc):
            if alpha != 1.0:
                acc = acc * alpha
            if r_ref is not None:
                acc = r_ref[...] + acc
            o_ref[...] = acc.astype(o_ref.dtype)

        if nk == 1:
            finish(part)
        else:
            acc_ref = refs[pos + 1]
            k = pl.program_id(kax)

            @pl.when(k == 0)
            def _():
                acc_ref[...] = part

            @pl.when(k > 0)
            def _():
                acc_ref[...] += part

            @pl.when(k == nk - 1)
            def _():
                finish(acc_ref[...])

    operands, in_specs = [], []
    for a, a_spec, b, b_spec in pairs:
        operands += [a, b]
        in_specs += [a_spec, b_spec]
    if res is not None:
        operands.append(res)
        in_specs.append(res_spec)
    (out,), ride_results = _call(name, body, grid, in_specs, [out_spec], [out_shape], operands,
                                 scratch=[pltpu.VMEM(acc_shape, F32)] if nk > 1 else [], rides=rides)
    return (out, ride_results) if rides else out


def _cast_bf16(x, name):
    rows, cols = x.shape
    tm = _rows_tile(rows, cols * 4)

    def body(x_ref, o_ref):
        o_ref[...] = x_ref[...].astype(BF16)

    spec = pl.BlockSpec((tm, cols), lambda i: (i, 0))
    return pl.pallas_call(body, grid=(rows // tm,), in_specs=[spec], out_specs=spec,
                          out_shape=jax.ShapeDtypeStruct(x.shape, BF16), compiler_params=_params(1), name=name)(x)


def _rms_fwd(x, gain, name):
    rows, d = x.shape
    tm = min(rows, 512)

    def body(x_ref, g_ref, o_ref):
        xv = x_ref[...]
        r = lax.rsqrt(jnp.mean(xv * xv, axis=-1, keepdims=True) + EPS)
        o_ref[...] = (xv * r * g_ref[...]).astype(BF16)

    spec = pl.BlockSpec((tm, d), lambda i: (i, 0))
    return pl.pallas_call(body, grid=(rows // tm,), in_specs=[spec, pl.BlockSpec((1, d), lambda i: (0, 0))],
                          out_specs=spec, out_shape=jax.ShapeDtypeStruct(x.shape, BF16),
                          compiler_params=_params(1), name=name)(x, gain)


def _rms_bwd(x, gain, dxn, dres, name):
    rows, d = x.shape
    tm = min(rows, 256)
    with_res = dres is not None

    def body(*refs):
        if with_res:
            x_ref, g_ref, dy_ref, r_ref, dx_ref, dxb_ref, dg_ref = refs
        else:
            x_ref, g_ref, dy_ref, dx_ref, dxb_ref, dg_ref = refs
        xv = x_ref[...]
        r = lax.rsqrt(jnp.mean(xv * xv, axis=-1, keepdims=True) + EPS)
        xh = xv * r
        dy = dy_ref[...]
        dxh = dy * g_ref[...]
        dx = r * (dxh - xh * jnp.mean(dxh * xh, axis=-1, keepdims=True))
        if with_res:
            dx = dx + r_ref[...]
        dx_ref[...] = dx
        dxb_ref[...] = dx.astype(BF16)
        part = jnp.sum(dy * xh, axis=0, keepdims=True)

        @pl.when(pl.program_id(0) == 0)
        def _():
            dg_ref[...] = part

        @pl.when(pl.program_id(0) > 0)
        def _():
            dg_ref[...] += part

    spec = pl.BlockSpec((tm, d), lambda i: (i, 0))
    vec = pl.BlockSpec((1, d), lambda i: (0, 0))
    ops = [x, gain, dxn] + ([dres] if with_res else [])
    return pl.pallas_call(
        body, grid=(rows // tm,), in_specs=[spec, vec, spec] + ([spec] if with_res else []),
        out_specs=[spec, spec, vec],
        out_shape=[jax.ShapeDtypeStruct(x.shape, F32), jax.ShapeDtypeStruct(x.shape, BF16),
                   jax.ShapeDtypeStruct((1, d), F32)],
        compiler_params=_params(1), name=name)(*ops)


def _loss_head(y, target, name):
    rows, d = y.shape
    tm = min(rows, 256)

    def body(y_ref, t_ref, dy_ref, dyb_ref, acc_ref):
        err = y_ref[...] - t_ref[...]
        dy = err * (1.0 / d)
        dy_ref[...] = dy
        dyb_ref[...] = dy.astype(BF16)
        part = jnp.zeros((8, 128), F32) + jnp.sum(err * err)

        @pl.when(pl.program_id(0) == 0)
        def _():
            acc_ref[...] = part

        @pl.when(pl.program_id(0) > 0)
        def _():
            acc_ref[...] += part

    spec = pl.BlockSpec((tm, d), lambda i: (i, 0))
    return pl.pallas_call(
        body, grid=(rows // tm,), in_specs=[spec, spec],
        out_specs=[spec, spec, pl.BlockSpec((8, 128), lambda i: (0, 0))],
        out_shape=[jax.ShapeDtypeStruct(y.shape, F32), jax.ShapeDtypeStruct(y.shape, BF16),
                   jax.ShapeDtypeStruct((8, 128), F32)],
        compiler_params=_params(1), name=name)(y, target)


ROW_CHUNK = 256


def _ffn_up(xn, wg, wu, tag, rides=()):
    t, d = xn.shape
    nd, fs, _ = wg.shape
    tm = min(t, 512)
    rc = min(tm, ROW_CHUNK)

    def body(x_ref, wg_ref, wu_ref, g_ref, u_ref, h_ref):
        for r in range(0, tm, rc):
            xv = x_ref[r:r + rc, :]
            g = _dot(xv, wg_ref[...], "nt")
            u = _dot(xv, wu_ref[...], "nt")
            g_ref[r:r + rc, :] = g.astype(BF16)
            u_ref[r:r + rc, :] = u.astype(BF16)
            h_ref[r:r + rc, :] = (g * _sigmoid(g) * u).astype(BF16)

    wspec = pl.BlockSpec((None, fs, d), lambda j, i: (j, 0, 0))
    hspec = pl.BlockSpec((None, tm, fs), lambda j, i: (j, i, 0))
    hid = jax.ShapeDtypeStruct((nd, t, fs), BF16)
    return _call(f"{tag}_up", body, (nd, t // tm), [pl.BlockSpec((tm, d), lambda j, i: (i, 0)), wspec, wspec],
                 [hspec] * 3, [hid] * 3, [xn, wg, wu], rides=rides)


SHARDS_PER_STEP = 2


def _ffn_down(x, h, wd, tag, rides=()):
    nd, t, fs = h.shape
    d = x.shape[1]
    tm = min(t, 512)
    sps = SHARDS_PER_STEP
    xspec = pl.BlockSpec((tm, d), lambda i, k: (i, 0))
    pairs = [(h, pl.BlockSpec((None, tm, fs), lambda i, k, s=s: (sps * k + s, i, 0)),
              wd, pl.BlockSpec((None, fs, d), lambda i, k, s=s: (sps * k + s, 0, 0))) for s in range(sps)]
    return _mm(f"{tag}_down", pairs, "nn", (t // tm, nd // sps), jax.ShapeDtypeStruct((t, d), F32), xspec,
               res=x, res_spec=xspec, alpha=0.5, rides=rides)


def _ffn_dact(dyb, wd, g, u, tag, rides=()):
    nd, t, fs = g.shape
    d = dyb.shape[1]
    tm = min(t, 512)
    rc = min(tm, ROW_CHUNK)

    def body(dy_ref, wd_ref, g_ref, u_ref, dg_ref, du_ref):
        for r in range(0, tm, rc):
            dh = 0.5 * _dot(dy_ref[r:r + rc, :], wd_ref[...], "nt")
            gv = g_ref[r:r + rc, :].astype(F32)
            uv = u_ref[r:r + rc, :].astype(F32)
            sig = _sigmoid(gv)
            du_ref[r:r + rc, :] = (dh * gv * sig).astype(BF16)
            dg_ref[r:r + rc, :] = (dh * uv * sig * (1.0 + gv * (1.0 - sig))).astype(BF16)

    hspec = pl.BlockSpec((None, tm, fs), lambda j, i: (j, i, 0))
    hid = jax.ShapeDtypeStruct((nd, t, fs), BF16)
    return _call(f"{tag}_dact", body, (nd, t // tm),
                 [pl.BlockSpec((tm, d), lambda j, i: (i, 0)), pl.BlockSpec((None, fs, d), lambda j, i: (j, 0, 0)),
                  hspec, hspec], [hspec] * 2, [hid] * 2, [dyb, wd, g, u], rides=rides)


def _ffn_dw(hid, act, alpha, name, rides=()):
    nd, t, fs = hid.shape
    d = act.shape[1]
    tk = min(t, 2048)
    return _mm(name, [(hid, pl.BlockSpec((None, tk, fs), lambda j, k: (j, k, 0)),
                       act, pl.BlockSpec((tk, d), lambda j, k: (k, 0)))], "tn", (nd, t // tk),
               jax.ShapeDtypeStruct((nd, fs, d), BF16), pl.BlockSpec((None, fs, d), lambda j, k: (j, 0, 0)),
               alpha=alpha, rides=rides)


def _ffn_dxn(dg, du, wg, wu, name, rides=(), shards=None, res=None):
    nd, t, fs = dg.shape
    d = wg.shape[2]
    tm = min(t, 512)
    sps = SHARDS_PER_STEP
    lo, hi = (0, nd) if shards is None else shards
    pairs = []
    for s in range(sps):
        am = pl.BlockSpec((None, tm, fs), lambda i, k, s=s: (lo + sps * k + s, i, 0))
        wk = pl.BlockSpec((None, fs, d), lambda i, k, s=s: (lo + sps * k + s, 0, 0))
        pairs += [(dg, am, wg, wk), (du, am, wu, wk)]
    xspec = pl.BlockSpec((tm, d), lambda i, k: (i, 0))
    return _mm(name, pairs, "nn", (t // tm, (hi - lo) // sps), jax.ShapeDtypeStruct((t, d), F32), xspec,
               res=res, res_spec=xspec if res is not None else None, rides=rides)


def _head_norm(x, gain):
    r = lax.rsqrt(jnp.mean(x * x, axis=-1, keepdims=True) + EPS)
    return x * r * gain


def _head_norm_bwd(x, gain, dy):
    r = lax.rsqrt(jnp.mean(x * x, axis=-1, keepdims=True) + EPS)
    xh = x * r
    dxh = dy * gain
    dx = r * (dxh - xh * jnp.mean(dxh * xh, axis=-1, keepdims=True))
    return dx, jnp.sum(dy * xh, axis=0, keepdims=True)


def _hs(h, base=0):
    return slice(base + h * HEAD, base + (h + 1) * HEAD)


def _tri(n, lower):
    r = lax.broadcasted_iota(jnp.int32, (n, n), 0)
    c = lax.broadcasted_iota(jnp.int32, (n, n), 1)
    return ((r >= c) if lower else (r <= c)).astype(F32)


def _attn_pre(proj, sp, name):
    t = proj.shape[0]
    tm = min(t, 256)

    def body(p_ref, sp_ref, fq, fk, fv, sq, sk, sv, mq, cc, carry):
        @pl.when(pl.program_id(0) == 0)
        def _():
            carry[...] = jnp.zeros_like(carry)

        for h in range(FOX_H):
            fq[:, _hs(h)] = _head_norm(p_ref[:, _hs(h, C_FQ)], sp_ref[R_FQ:R_FQ + 1, :]).astype(BF16)
            fk[:, _hs(h)] = _head_norm(p_ref[:, _hs(h, C_FK)], sp_ref[R_FK:R_FK + 1, :]).astype(BF16)
        fv[...] = p_ref[:, C_FV:C_FV + FOX_W].astype(BF16)
        for h in range(SWA_H):
            sq[:, _hs(h)] = _head_norm(p_ref[:, _hs(h, C_SQ)], sp_ref[R_SQ:R_SQ + 1, :]).astype(BF16)
        for h in range(SWA_KV):
            sk[:, _hs(h)] = _head_norm(p_ref[:, _hs(h, C_SK)], sp_ref[R_SK:R_SK + 1, :]).astype(BF16)
        sv[...] = p_ref[:, C_SV:C_SV + SWA_KV_W].astype(BF16)
        for h in range(MEM_H):
            mq[:, _hs(h)] = _head_norm(p_ref[:, _hs(h, C_MQ)], sp_ref[R_MQ:R_MQ + 1, :]).astype(BF16)
        z = p_ref[:, C_FL:C_FL + HEAD] + sp_ref[R_FB:R_FB + 1, :]
        lane = lax.broadcasted_iota(jnp.int32, z.shape, 1)
        log_f = jnp.minimum(z, 0.0) - jnp.log(1.0 + jnp.exp(-jnp.abs(z)))
        log_f = jnp.where(lane < N_LOGIT, log_f, 0.0)
        c = jnp.dot(_tri(tm, True), log_f, precision=lax.Precision.HIGHEST, preferred_element_type=F32)
        c = c + carry[0:1, :]
        cc[...] = c
        carry[...] = jnp.broadcast_to(c[tm - 1:tm, :], carry.shape)

    def rows(w):
        return pl.BlockSpec((tm, w), lambda i: (i, 0))

    def shape(w, dt):
        return jax.ShapeDtypeStruct((t, w), dt)

    widths = [FOX_W, FOX_W, FOX_W, SWA_W, SWA_KV_W, SWA_KV_W, MEM_W]
    return pl.pallas_call(
        body, grid=(t // tm,), in_specs=[rows(IN_W), pl.BlockSpec((16, 128), lambda i: (0, 0))],
        out_specs=[rows(w) for w in widths] + [rows(HEAD)],
        out_shape=[shape(w, BF16) for w in widths] + [shape(HEAD, F32)],
        scratch_shapes=[pltpu.VMEM((8, 128), F32)], compiler_params=_params(1), name=name)(proj, sp)


def _attn_post_bwd(proj, sp, dfq, dfk, dfv, dsq, dsk, dsv, dmq, dc_col, dc_row_t, name):
    t = proj.shape[0]
    tm = min(t, 256)
    nb = t // tm

    def body(p_ref, sp_ref, dfq_r, dfk_r, dfv_r, dsq_r, dsk_r, dsv_r, dmq_r, dcc_r, dcr_r, dp_ref, dsp_ref, carry):
        @pl.when(pl.program_id(0) == 0)
        def _():
            carry[...] = jnp.zeros_like(carry)
            dsp_ref[...] = jnp.zeros_like(dsp_ref)

        def group(n_heads, col, row, d_ref):
            total = None
            for h in range(n_heads):
                dx, dg = _head_norm_bwd(p_ref[:, _hs(h, col)], sp_ref[row:row + 1, :], d_ref[:, _hs(h)])
                dp_ref[:, _hs(h, col)] = dx.astype(BF16)
                total = dg if total is None else total + dg
            dsp_ref[row:row + 1, :] += total

        group(FOX_H, C_FQ, R_FQ, dfq_r)
        group(FOX_H, C_FK, R_FK, dfk_r)
        dp_ref[:, C_FV:C_FV + FOX_W] = dfv_r[...].astype(BF16)
        group(SWA_H, C_SQ, R_SQ, dsq_r)
        group(SWA_KV, C_SK, R_SK, dsk_r)
        dp_ref[:, C_SV:C_SV + SWA_KV_W] = dsv_r[...].astype(BF16)
        group(MEM_H, C_MQ, R_MQ, dmq_r)
        dc = dcc_r[...] - dcr_r[...]
        rc = jnp.dot(_tri(tm, False), dc, precision=lax.Precision.HIGHEST, preferred_element_type=F32)
        rc = rc + carry[0:1, :]
        carry[...] = jnp.broadcast_to(rc[0:1, :], carry.shape)
        z = p_ref[:, C_FL:C_FL + HEAD] + sp_ref[R_FB:R_FB + 1, :]
        dz = rc * _sigmoid(-z)
        dp_ref[:, C_FL:C_FL + HEAD] = dz.astype(BF16)
        dsp_ref[R_FB:R_FB + 1, :] += jnp.sum(dz, axis=0, keepdims=True)

    def rows(w):
        return pl.BlockSpec((tm, w), lambda i: (nb - 1 - i, 0))

    small = pl.BlockSpec((16, 128), lambda i: (0, 0))
    widths = [FOX_W, FOX_W, FOX_W, SWA_W, SWA_KV_W, SWA_KV_W, MEM_W, HEAD, HEAD]
    return pl.pallas_call(
        body, grid=(nb,), in_specs=[rows(IN_W), small] + [rows(w) for w in widths],
        out_specs=[rows(IN_W), small],
        out_shape=[jax.ShapeDtypeStruct((t, IN_W), BF16), jax.ShapeDtypeStruct((16, 128), F32)],
        scratch_shapes=[pltpu.VMEM((8, 128), F32)], compiler_params=_params(1), name=name,
    )(proj, sp, dfq, dfk, dfv, dsq, dsk, dsv, dmq, dc_col, dc_row_t)


def _head_column(values):
    rows = values[0].shape[0]
    lane = lax.broadcasted_iota(jnp.int32, (rows, HEAD), 1)
    out = jnp.zeros((rows, HEAD), F32)
    for h, v in enumerate(values):
        out = jnp.where(lane == h, v, out)
    return out


def _head_row(values, n_rows=8):
    cols = values[0].shape[1]
    sub = lax.broadcasted_iota(jnp.int32, (n_rows, cols), 0)
    out = jnp.zeros((n_rows, cols), F32)
    for h, v in enumerate(values):
        out = jnp.where(sub == h, v, out)
    return out


def _delta(dmixed, o_a, o_b, o_c, name):
    t = dmixed.shape[0]
    tm = min(t, 512)

    def body(d_ref, a_ref, b_ref, c_ref, o_ref, rep_ref):
        cols = []
        for ref, n_heads, base in ((a_ref, FOX_H, 0), (b_ref, SWA_H, FOX_W), (c_ref, MEM_H, FOX_W + SWA_W)):
            for h in range(n_heads):
                cols.append(jnp.sum(d_ref[:, _hs(h, base)] * ref[:, _hs(h)], axis=-1, keepdims=True))
        o_ref[...] = _head_column(cols)
        for h in range(FOX_H):
            rep_ref[h] = jnp.broadcast_to(cols[h], (tm, HEAD))

    def rows(w):
        return pl.BlockSpec((tm, w), lambda i: (i, 0))

    return pl.pallas_call(body, grid=(t // tm,), in_specs=[rows(dmixed.shape[1]), rows(FOX_W), rows(SWA_W), rows(MEM_W)],
                          out_specs=[rows(HEAD), pl.BlockSpec((FOX_H, tm, HEAD), lambda i: (0, i, 0))],
                          out_shape=[jax.ShapeDtypeStruct((t, HEAD), F32), jax.ShapeDtypeStruct((FOX_H, t, HEAD), F32)],
                          compiler_params=_params(1), name=name)(dmixed, o_a, o_b, o_c)


def _fox_fwd(fq, fk, fv, c_rep, c_row, name, rides=()):
    t = fq.shape[0]
    tb = min(t, 512)
    nb = t // tb
    n_tiles = tb // HEAD

    def body(q_ref, k_ref, v_ref, cc_ref, cr_ref, o_ref, lse_ref, m_s, l_s, acc_s):
        qi, ki = pl.program_id(0), pl.program_id(1)

        @pl.when(ki == 0)
        def _():
            m_s[...] = jnp.full_like(m_s, NEG_INF)
            l_s[...] = jnp.zeros_like(l_s)
            acc_s[...] = jnp.zeros_like(acc_s)

        def step(diagonal):
            if diagonal:
                r = lax.broadcasted_iota(jnp.int32, (tb, HEAD), 0)
                c = lax.broadcasted_iota(jnp.int32, (tb, HEAD), 1)
            for h in range(FOX_H):
                s = _dot(q_ref[:, _hs(h)], k_ref[:, _hs(h)], "nt")
                cc = cc_ref[h]
                tiles, m_cur = [], None
                for j in range(n_tiles):
                    st = s[:, _hs(j)] * SCALE + cc - cr_ref[h:h + 1, _hs(j)]
                    if diagonal:
                        st = jnp.where(r >= c + j * HEAD, st, NEG_INF)
                    tiles.append(st)
                    m_cur = st if m_cur is None else jnp.maximum(m_cur, st)
                m_prev = m_s[h]
                m_new = jnp.maximum(m_prev, jnp.max(m_cur, axis=-1, keepdims=True))
                alpha = jnp.exp(m_prev - m_new)
                ps = [jnp.exp(st - m_new) for st in tiles]
                l_cur = ps[0]
                for p in ps[1:]:
                    l_cur = l_cur + p
                l_s[h] = alpha * l_s[h] + jnp.sum(l_cur, axis=-1, keepdims=True)
                p = jnp.concatenate([p.astype(BF16) for p in ps], axis=1)
                acc_s[:, _hs(h)] = alpha * acc_s[:, _hs(h)] + _dot(p, v_ref[:, _hs(h)])
                m_s[h] = m_new

        @pl.when(ki < qi)
        def _():
            step(False)

        @pl.when(ki == qi)
        def _():
            step(True)
            for h in range(FOX_H):
                o_ref[:, _hs(h)] = acc_s[:, _hs(h)] / l_s[h]
                lse_ref[h] = m_s[h] + jnp.log(l_s[h])

    qspec = pl.BlockSpec((tb, FOX_W), lambda i, j: (i, 0))
    kspec = pl.BlockSpec((tb, FOX_W), lambda i, j: (jnp.minimum(i, j), 0))
    rep = pl.BlockSpec((FOX_H, tb, HEAD), lambda i, j: (0, i, 0))
    return _call(
        name, body, (nb, nb),
        [qspec, kspec, kspec, rep, pl.BlockSpec((8, tb), lambda i, j: (0, jnp.minimum(i, j)))],
        [qspec, rep],
        [jax.ShapeDtypeStruct((t, FOX_W), F32), jax.ShapeDtypeStruct((FOX_H, t, HEAD), F32)],
        [fq, fk, fv, c_rep, c_row],
        scratch=[pltpu.VMEM((FOX_H, tb, HEAD), F32), pltpu.VMEM((FOX_H, tb, HEAD), F32), pltpu.VMEM((tb, FOX_W), F32)],
        rides=rides)


def _fox_bwd(fq, fk, fv, c_rep, c_row, dmixed, lse, delta, name, rides=()):
    t = fq.shape[0]
    tb = min(t, 512)
    nb = t // tb
    n_tiles = tb // HEAD

    def body(q_ref, k_ref, v_ref, cc_ref, cr_ref, do_ref, lse_ref, dl_ref,
             dq_ref, dk_ref, dv_ref, dcc_ref, dcr_ref):
        ki, qi = pl.program_id(0), pl.program_id(1)

        @pl.when((ki == 0) & (qi == 0))
        def _():
            dq_ref[...] = jnp.zeros_like(dq_ref)
            dcc_ref[...] = jnp.zeros_like(dcc_ref)

        @pl.when(qi == 0)
        def _():
            dk_ref[...] = jnp.zeros_like(dk_ref)
            dv_ref[...] = jnp.zeros_like(dv_ref)
            dcr_ref[...] = jnp.zeros_like(dcr_ref)

        def step(diagonal):
            rows = pl.ds(pl.multiple_of(qi * tb, tb), tb)
            if diagonal:
                r = lax.broadcasted_iota(jnp.int32, (tb, HEAD), 0)
                c = lax.broadcasted_iota(jnp.int32, (tb, HEAD), 1)
            row_sums, col_sums = [], []
            for h in range(FOX_H):
                q, k, v, do = q_ref[:, _hs(h)], k_ref[:, _hs(h)], v_ref[:, _hs(h)], do_ref[:, _hs(h)]
                s = _dot(q, k, "nt")
                dp = _dot(do, v, "nt")
                cc, lse_h, dl_h = cc_ref[h], lse_ref[h], dl_ref[h]
                ps, dss, row = [], [], None
                for j in range(n_tiles):
                    st = s[:, _hs(j)] * SCALE + cc - cr_ref[h:h + 1, _hs(j)]
                    if diagonal:
                        st = jnp.where(r >= c + j * HEAD, st, NEG_INF)
                    pt = jnp.exp(st - lse_h)
                    dst = pt * (dp[:, _hs(j)] - dl_h)
                    ps.append(pt.astype(BF16))
                    dss.append(dst)
                    row = dst if row is None else row + dst
                p = jnp.concatenate(ps, axis=1)
                ds = jnp.concatenate(dss, axis=1)
                dsb = ds.astype(BF16)
                dv_ref[:, _hs(h)] += _dot(p, do, "tn")
                dk_ref[:, _hs(h)] += _dot(dsb, q, "tn") * SCALE
                dq_ref[rows, _hs(h)] += _dot(dsb, k) * SCALE
                row_sums.append(jnp.sum(row, axis=1, keepdims=True))
                col_sums.append(jnp.sum(ds, axis=0, keepdims=True))
            dcc_ref[rows, :] += _head_column(row_sums)
            dcr_ref[...] += _head_row(col_sums)

        @pl.when(qi > ki)
        def _():
            step(False)

        @pl.when(qi == ki)
        def _():
            step(True)

    def qmap(j, i):
        return (jnp.maximum(i, j), 0)

    qspec = pl.BlockSpec((tb, FOX_W), qmap)
    kspec = pl.BlockSpec((tb, FOX_W), lambda j, i: (j, 0))
    rep = pl.BlockSpec((FOX_H, tb, HEAD), lambda j, i: (0, jnp.maximum(i, j), 0))
    rowspec = pl.BlockSpec((8, tb), lambda j, i: (0, j))
    return _call(
        name, body, (nb, nb), [qspec, kspec, kspec, rep, rowspec, qspec, rep, rep],
        [pl.BlockSpec((t, FOX_W), lambda j, i: (0, 0)), kspec, kspec,
         pl.BlockSpec((t, HEAD), lambda j, i: (0, 0)), rowspec],
        [jax.ShapeDtypeStruct((t, FOX_W), F32)] * 3 + [jax.ShapeDtypeStruct((t, HEAD), F32),
                                                       jax.ShapeDtypeStruct((8, t), F32)],
        [fq, fk, fv, c_rep, c_row, dmixed, lse, delta], rides=rides)


def _swa_logits(q, k_cur, k_prev, slope, first_block):
    w = SWA_BLOCK
    r = lax.broadcasted_iota(jnp.int32, (w, w), 0)
    j = lax.broadcasted_iota(jnp.int32, (w, w), 1)
    dist_cur = r - j
    dist_prev = w + r - j
    s_cur = _dot(q, k_cur, "nt") * SCALE - slope * dist_cur.astype(F32)
    s_cur = jnp.where(dist_cur >= 0, s_cur, NEG_INF)
    s_prev = _dot(q, k_prev, "nt") * SCALE - slope * dist_prev.astype(F32)
    s_prev = jnp.where((j > r) & jnp.logical_not(first_block), s_prev, NEG_INF)
    return s_cur, s_prev


def _slope(h):
    return float(2.0 ** (-8.0 * (h + 1) / SWA_H))


def _swa_fwd(sq, sk, sv, sp, name, rides=()):
    t = sq.shape[0]
    w = SWA_BLOCK
    nb = t // w
    group = SWA_H // SWA_KV

    def body(q_ref, kp_ref, kc_ref, vp_ref, vc_ref, sp_ref, o_ref, lse_ref):
        first = pl.program_id(0) == 0
        lses = []
        for h in range(SWA_H):
            kv = h // group
            s_cur, s_prev = _swa_logits(q_ref[:, _hs(h)], kc_ref[:, _hs(kv)], kp_ref[:, _hs(kv)], _slope(h), first)
            sink = sp_ref[R_SINK:R_SINK + 1, h:h + 1]
            m = jnp.maximum(jnp.maximum(jnp.max(s_cur, axis=-1, keepdims=True),
                                        jnp.max(s_prev, axis=-1, keepdims=True)), sink)
            p_cur = jnp.exp(s_cur - m)
            p_prev = jnp.exp(s_prev - m)
            l = jnp.sum(p_cur, axis=-1, keepdims=True) + jnp.sum(p_prev, axis=-1, keepdims=True) + jnp.exp(sink - m)
            o_ref[:, _hs(h)] = (_dot(p_cur, vc_ref[:, _hs(kv)]) + _dot(p_prev, vp_ref[:, _hs(kv)])) / l
            lses.append(m + jnp.log(l))
        lse_ref[...] = _head_column(lses)

    qspec = pl.BlockSpec((w, SWA_W), lambda n: (n, 0))
    cur = pl.BlockSpec((w, SWA_KV_W), lambda n: (n, 0))
    prev = pl.BlockSpec((w, SWA_KV_W), lambda n: (jnp.maximum(n - 1, 0), 0))
    return _call(
        name, body, (nb,), [qspec, prev, cur, prev, cur, pl.BlockSpec((16, 128), lambda n: (0, 0))],
        [qspec, pl.BlockSpec((w, HEAD), lambda n: (n, 0))],
        [jax.ShapeDtypeStruct((t, SWA_W), F32), jax.ShapeDtypeStruct((t, HEAD), F32)],
        [sq, sk, sk, sv, sv, sp], rides=rides)


def _swa_bwd(sq, sk, sv, sp, dmixed, lse, delta, name):
    t = sq.shape[0]
    w = SWA_BLOCK
    nb = t // w
    group = SWA_H // SWA_KV
    do_block = FOX_W // SWA_W
    assert FOX_W % SWA_W == 0

    def body(q_ref, kp_ref, kc_ref, vp_ref, vc_ref, sp_ref, do_ref, lse_ref, dl_ref,
             dq_ref, dk_ref, dv_ref, dsp_ref, ck, cv):
        step = pl.program_id(0)
        first = step == nb - 1

        @pl.when(step == 0)
        def _():
            ck[...] = jnp.zeros_like(ck)
            cv[...] = jnp.zeros_like(cv)
            dsp_ref[...] = jnp.zeros_like(dsp_ref)

        dk_cur = [None] * SWA_KV
        dk_prev = [None] * SWA_KV
        dv_cur = [None] * SWA_KV
        dv_prev = [None] * SWA_KV
        dsinks = []

        def add(lst, i, v):
            lst[i] = v if lst[i] is None else lst[i] + v

        for h in range(SWA_H):
            kv = h // group
            q, do = q_ref[:, _hs(h)], do_ref[:, _hs(h)]
            kc, kp, vc, vp = kc_ref[:, _hs(kv)], kp_ref[:, _hs(kv)], vc_ref[:, _hs(kv)], vp_ref[:, _hs(kv)]
            s_cur, s_prev = _swa_logits(q, kc, kp, _slope(h), first)
            lse_h = lse_ref[:, h:h + 1]
            dl_h = dl_ref[:, FOX_H + h:FOX_H + h + 1]
            p_cur = jnp.exp(s_cur - lse_h)
            p_prev = jnp.exp(s_prev - lse_h)
            p_sink = jnp.exp(sp_ref[R_SINK:R_SINK + 1, h:h + 1] - lse_h)
            ds_cur = p_cur * (_dot(do, vc, "nt") - dl_h)
            ds_prev = p_prev * (_dot(do, vp, "nt") - dl_h)
            dq_ref[:, _hs(h)] = (_dot(ds_cur, kc) + _dot(ds_prev, kp)) * SCALE
            add(dk_cur, kv, _dot(ds_cur, q, "tn") * SCALE)
            add(dk_prev, kv, _dot(ds_prev, q, "tn") * SCALE)
            add(dv_cur, kv, _dot(p_cur, do, "tn"))
            add(dv_prev, kv, _dot(p_prev, do, "tn"))
            dsinks.append(-jnp.sum(p_sink * dl_h, axis=0, keepdims=True))
        for kv in range(SWA_KV):
            dk_ref[:, _hs(kv)] = dk_cur[kv] + ck[:, _hs(kv)]
            dv_ref[:, _hs(kv)] = dv_cur[kv] + cv[:, _hs(kv)]
            ck[:, _hs(kv)] = dk_prev[kv]
            cv[:, _hs(kv)] = dv_prev[kv]
        lane = lax.broadcasted_iota(jnp.int32, (1, HEAD), 1)
        row = jnp.zeros((1, HEAD), F32)
        for h in range(SWA_H):
            row = jnp.where(lane == h, dsinks[h], row)
        dsp_ref[R_SINK:R_SINK + 1, :] += row

    def rev(n):
        return nb - 1 - n

    qspec = pl.BlockSpec((w, SWA_W), lambda n: (rev(n), 0))
    cur = pl.BlockSpec((w, SWA_KV_W), lambda n: (rev(n), 0))
    prev = pl.BlockSpec((w, SWA_KV_W), lambda n: (jnp.maximum(rev(n) - 1, 0), 0))
    col = pl.BlockSpec((w, HEAD), lambda n: (rev(n), 0))
    small = pl.BlockSpec((16, 128), lambda n: (0, 0))
    return pl.pallas_call(
        body, grid=(nb,),
        in_specs=[qspec, prev, cur, prev, cur, small, pl.BlockSpec((w, SWA_W), lambda n: (rev(n), do_block)), col, col],
        out_specs=[qspec, cur, cur, small],
        out_shape=[jax.ShapeDtypeStruct((t, SWA_W), F32), jax.ShapeDtypeStruct((t, SWA_KV_W), F32),
                   jax.ShapeDtypeStruct((t, SWA_KV_W), F32), jax.ShapeDtypeStruct((16, 128), F32)],
        scratch_shapes=[pltpu.VMEM((w, SWA_KV_W), F32), pltpu.VMEM((w, SWA_KV_W), F32)],
        compiler_params=_params(1), name=name)(sq, sk, sk, sv, sv, sp, dmixed, lse, delta)


def _mem_pre(mkv, sp, name):
    m = mkv.shape[0]

    def body(x_ref, sp_ref, k_ref, v_ref):
        for h in range(MEM_H):
            k_ref[:, _hs(h)] = _head_norm(x_ref[:, _hs(h)], sp_ref[R_MK:R_MK + 1, :]).astype(BF16)
        v_ref[...] = x_ref[:, MEM_W:2 * MEM_W].astype(BF16)

    out = jax.ShapeDtypeStruct((m, MEM_W), BF16)
    return pl.pallas_call(body, out_shape=[out, out], name=name)(mkv, sp)


def _mem_post_bwd(mkv, sp, dmk, dmv, name):
    m = mkv.shape[0]

    def body(x_ref, sp_ref, dk_ref, dv_ref, d_ref, dsp_ref):
        dsp_ref[...] = jnp.zeros_like(dsp_ref)
        total = None
        for h in range(MEM_H):
            dx, dg = _head_norm_bwd(x_ref[:, _hs(h)], sp_ref[R_MK:R_MK + 1, :], dk_ref[:, _hs(h)])
            d_ref[:, _hs(h)] = dx.astype(BF16)
            total = dg if total is None else total + dg
        d_ref[:, MEM_W:2 * MEM_W] = dv_ref[...].astype(BF16)
        dsp_ref[R_MK:R_MK + 1, :] = total

    return pl.pallas_call(body, out_shape=[jax.ShapeDtypeStruct((m, 2 * MEM_W), BF16),
                                           jax.ShapeDtypeStruct((16, 128), F32)], name=name)(mkv, sp, dmk, dmv)


def _mem_fwd(mq, mk, mv, name):
    t = mq.shape[0]
    m = mk.shape[0]
    tq = min(t, 512)

    def body(q_ref, k_ref, v_ref, o_ref, lse_ref):
        lses = []
        for h in range(MEM_H):
            s = _dot(q_ref[:, _hs(h)], k_ref[:, _hs(h)], "nt") * SCALE
            mx = jnp.max(s, axis=-1, keepdims=True)
            p = jnp.exp(s - mx)
            l = jnp.sum(p, axis=-1, keepdims=True)
            o_ref[:, _hs(h)] = _dot(p, v_ref[:, _hs(h)]) / l
            lses.append(mx + jnp.log(l))
        lse_ref[...] = _head_column(lses)

    qspec = pl.BlockSpec((tq, MEM_W), lambda i: (i, 0))
    kspec = pl.BlockSpec((m, MEM_W), lambda i: (0, 0))
    return pl.pallas_call(
        body, grid=(t // tq,), in_specs=[qspec, kspec, kspec],
        out_specs=[qspec, pl.BlockSpec((tq, HEAD), lambda i: (i, 0))],
        out_shape=[jax.ShapeDtypeStruct((t, MEM_W), F32), jax.ShapeDtypeStruct((t, HEAD), F32)],
        compiler_params=_params(1), name=name)(mq, mk, mv)


def _mem_bwd(mq, mk, mv, dmixed, lse, delta, name):
    t = mq.shape[0]
    m = mk.shape[0]
    tq = min(t, 512)
    do_block = (FOX_W + SWA_W) // MEM_W
    assert (FOX_W + SWA_W) % MEM_W == 0

    def body(q_ref, k_ref, v_ref, do_ref, lse_ref, dl_ref, dq_ref, dk_ref, dv_ref):
        @pl.when(pl.program_id(0) == 0)
        def _():
            dk_ref[...] = jnp.zeros_like(dk_ref)
            dv_ref[...] = jnp.zeros_like(dv_ref)

        for h in range(MEM_H):
            q, k, v, do = q_ref[:, _hs(h)], k_ref[:, _hs(h)], v_ref[:, _hs(h)], do_ref[:, _hs(h)]
            s = _dot(q, k, "nt") * SCALE
            p = jnp.exp(s - lse_ref[:, h:h + 1])
            col = FOX_H + SWA_H + h
            ds = p * (_dot(do, v, "nt") - dl_ref[:, col:col + 1])
            dq_ref[:, _hs(h)] = _dot(ds, k) * SCALE
            dk_ref[:, _hs(h)] += _dot(ds, q, "tn") * SCALE
            dv_ref[:, _hs(h)] += _dot(p, do, "tn")

    qspec = pl.BlockSpec((tq, MEM_W), lambda i: (i, 0))
    kspec = pl.BlockSpec((m, MEM_W), lambda i: (0, 0))
    col = pl.BlockSpec((tq, HEAD), lambda i: (i, 0))
    return pl.pallas_call(
        body, grid=(t // tq,),
        in_specs=[qspec, kspec, kspec, pl.BlockSpec((tq, MEM_W), lambda i: (i, do_block)), col, col],
        out_specs=[qspec, kspec, kspec],
        out_shape=[jax.ShapeDtypeStruct((t, MEM_W), F32), jax.ShapeDtypeStruct((m, MEM_W), F32),
                   jax.ShapeDtypeStruct((m, MEM_W), F32)],
        compiler_params=_params(1), name=name)(mq, mk, mv, dmixed, lse, delta)


def _all_gather(xs, name):
    n = len(xs)

    def body(*refs):
        x_refs, o_refs = refs[:n], refs[n:2 * n]
        send_sems, recv_sems, local_sems = refs[2 * n:]
        x, y, c = _me()
        me, sibling = (x, y, c), (x, y, 1 - c)
        chips = [(1 - x, y), (x, 1 - y), (1 - x, 1 - y)]

        def copy(a, k, block, to, src=None):
            slot = o_refs[a].at[_lin(block)]
            return pltpu.make_async_remote_copy(
                src_ref=slot if src is None else src, dst_ref=slot, send_sem=send_sems.at[a, k],
                recv_sem=recv_sems.at[a, k], device_id=to, device_id_type=MESH)

        mine = [pltpu.make_async_copy(x_refs[a], o_refs[a].at[_lin(me)], local_sems.at[a]) for a in range(n)]
        for cp in mine:
            cp.start()
        first = []
        for a in range(n):
            first.append(copy(a, 0, me, sibling, src=x_refs[a]))
            first += [copy(a, 1 + j, me, (*chip, c), src=x_refs[a]) for j, chip in enumerate(chips)]
        for cp in first:
            cp.start()
        passed = []
        for j, chip in enumerate(chips):
            for a in range(n):
                copy(a, 1 + j, (*chip, c), me).wait_recv()
                cp = copy(a, 4 + j, (*chip, c), sibling)
                cp.start()
                passed.append(cp)
        for a in range(n):
            copy(a, 0, sibling, me).wait_recv()
            for j, chip in enumerate(chips):
                copy(a, 4 + j, (*chip, 1 - c), me).wait_recv()
        for cp in first + passed:
            cp.wait_send()
        for cp in mine:
            cp.wait()

    return pl.pallas_call(
        body, in_specs=[ANY] * n, out_specs=[ANY] * n,
        out_shape=[jax.ShapeDtypeStruct((N_DEV,) + x.shape, x.dtype) for x in xs],
        scratch_shapes=[pltpu.SemaphoreType.DMA((n, 7)), pltpu.SemaphoreType.DMA((n, 7)),
                        pltpu.SemaphoreType.DMA((n,))],
        name=name)(*xs)


def _peers():
    x, y, c = _me()
    out = []
    for k in range(1, N_DEV):
        kx, ky, kc = (k >> 2) & 1, (k >> 1) & 1, k & 1
        out.append(((1 - x) if kx else x, (1 - y) if ky else y, (1 - c) if kc else c))
    return out


def _all_reduce_small(xs, name):
    n = len(xs)

    def body(*refs):
        x_refs, o_refs = refs[:n], refs[n:2 * n]
        bufs = refs[2 * n:3 * n]
        send_sems, recv_sems = refs[3 * n:]
        me = _lin(_me())
        peers = _peers()
        for a in range(n):
            bufs[a][me] = x_refs[a][...]
        sends = []
        for a in range(n):
            for k, peer in enumerate(peers):
                sends.append(pltpu.make_async_remote_copy(
                    src_ref=bufs[a].at[me], dst_ref=bufs[a].at[me], send_sem=send_sems.at[a, k],
                    recv_sem=recv_sems.at[a, k], device_id=peer, device_id_type=MESH))
        for cp in sends:
            cp.start()
        for a in range(n):
            for k, peer in enumerate(peers):
                pltpu.make_async_remote_copy(
                    src_ref=bufs[a].at[me], dst_ref=bufs[a].at[_lin(peer)], send_sem=send_sems.at[a, k],
                    recv_sem=recv_sems.at[a, k], device_id=peer, device_id_type=MESH).wait_recv()
        for cp in sends:
            cp.wait_send()
        for a in range(n):
            total = bufs[a][0]
            for q in range(1, N_DEV):
                total = total + bufs[a][q]
            o_refs[a][...] = total

    vmem = pl.BlockSpec(memory_space=pltpu.VMEM)
    return pl.pallas_call(
        body, in_specs=[vmem] * n, out_specs=[vmem] * n,
        out_shape=[jax.ShapeDtypeStruct(x.shape, F32) for x in xs],
        scratch_shapes=[pltpu.VMEM((N_DEV,) + x.shape, F32) for x in xs]
        + [pltpu.SemaphoreType.DMA((n, 7)), pltpu.SemaphoreType.DMA((n, 7))],
        name=name)(*xs)


def _pair_add(part, got, name):
    _, rows, cols = part.shape
    tm = _rows_tile(rows, cols * 2, budget=2 << 20)
    core = jnp.reshape(lax.axis_index("c"), (1,)).astype(jnp.int32)

    def body(c_ref, p_ref, g_ref, o_ref):
        o_ref[...] = (p_ref[...].astype(F32) + g_ref[...].astype(F32)).astype(BF16)

    spec = pl.BlockSpec((None, tm, cols), lambda q, i, c: (q, i, 0))
    grid_spec = pltpu.PrefetchScalarGridSpec(
        num_scalar_prefetch=1, grid=(4, rows // tm),
        in_specs=[pl.BlockSpec((None, tm, cols), lambda q, i, c: (2 * q + c[0], i, 0)), spec], out_specs=spec)
    return pl.pallas_call(body, grid_spec=grid_spec, out_shape=jax.ShapeDtypeStruct((4, rows, cols), BF16),
                          compiler_params=_params(2), name=name)(core, part, got)


def _adam_math(w, g, m, v):
    nm = ADAM_B1 * m + (1.0 - ADAM_B1) * g
    nv = ADAM_B2 * v + (1.0 - ADAM_B2) * (g * g)
    m_hat = nm / (1.0 - ADAM_B1 ** ADAM_STEP)
    v_hat = nv / (1.0 - ADAM_B2 ** ADAM_STEP)
    return -ADAM_LR * (m_hat / (jnp.sqrt(v_hat) + ADAM_EPS) + ADAM_WD * w), nm, nv


def _sum_chips(got, name):
    _, rows, cols = got.shape
    tm = _rows_tile(rows, cols * 2 * 4, budget=2 << 20)

    def body(r_ref, o_ref):
        o_ref[...] = ((r_ref[0].astype(F32) + r_ref[1].astype(F32)) + r_ref[2].astype(F32)) + r_ref[3].astype(F32)

    return pl.pallas_call(
        body, grid=(rows // tm,), in_specs=[pl.BlockSpec((4, tm, cols), lambda i: (0, i, 0))],
        out_specs=pl.BlockSpec((tm, cols), lambda i: (i, 0)), out_shape=jax.ShapeDtypeStruct((rows, cols), F32),
        compiler_params=_params(1), name=name)(got)


def _sum_adamw(got, col_block, w, m, v, name):
    _, rows, cols = w.shape
    tm = _rows_tile(rows, cols * 4, budget=1 << 20)

    def body(r_ref, w_ref, m_ref, v_ref, g_ref, d_ref, nm_ref, nv_ref):
        g = ((r_ref[0].astype(F32) + r_ref[1].astype(F32)) + r_ref[2].astype(F32)) + r_ref[3].astype(F32)
        g_ref[...] = g
        d_ref[...], nm_ref[...], nv_ref[...] = _adam_math(w_ref[...], g, m_ref[...], v_ref[...])

    spec = pl.BlockSpec((None, tm, cols), lambda i: (0, i, 0))
    out = jax.ShapeDtypeStruct(w.shape, F32)
    return pl.pallas_call(
        body, grid=(rows // tm,), in_specs=[pl.BlockSpec((4, tm, cols), lambda i: (0, i, col_block)), spec, spec, spec],
        out_specs=[spec] * 4, out_shape=[out] * 4, compiler_params=_params(1), name=name)(got, w, m, v)


def _adamw(w, g, m, v, name):
    rows, cols = w.shape
    tm = _rows_tile(rows, cols * 4, budget=2 << 20, mult=8)

    def body(w_ref, g_ref, m_ref, v_ref, d_ref, nm_ref, nv_ref):
        d_ref[...], nm_ref[...], nv_ref[...] = _adam_math(w_ref[...], g_ref[...], m_ref[...], v_ref[...])

    spec = pl.BlockSpec((tm, cols), lambda i: (i, 0))
    out = jax.ShapeDtypeStruct(w.shape, F32)
    return pl.pallas_call(body, grid=(rows // tm,), in_specs=[spec] * 4, out_specs=[spec] * 3,
                          out_shape=[out] * 3, compiler_params=_params(1), name=name)(w, g, m, v)


def _permute_in(w):
    logit0 = 3 * FOX_W
    pad = jnp.zeros(w.shape[:-1] + (HEAD - N_LOGIT,), w.dtype)
    return jnp.concatenate([w[..., :logit0], w[..., logit0 + N_LOGIT:], w[..., logit0:logit0 + N_LOGIT], pad], axis=-1)


def _unpermute_in(w):
    logit0 = 3 * FOX_W
    return jnp.concatenate([w[..., :logit0], w[..., C_FL:C_FL + N_LOGIT], w[..., logit0:C_FL]], axis=-1)


def _pad_row(v, width):
    return jnp.pad(v, ((0, 0), (0, width - v.shape[1])))


def _pack_small(fq, fk, sq, sk, mq, mk, fb, sinks):
    rows = [fq, fk, sq, sk, mq, mk, _pad_row(fb, HEAD), _pad_row(sinks, HEAD)]
    return jnp.concatenate(rows + [jnp.zeros((8, HEAD), F32)], axis=0)


def _pack_norms(a, b, c, d):
    return jnp.concatenate([a, b, c, d, jnp.zeros((4, a.shape[1]), F32)], axis=0)


def kernel(x, mem, ffn1_norm, ffn1_gate, ffn1_up, ffn1_down, mix_norm, mem_norm, w_in, forget_bias, w_mem_k, w_mem_v, fox_q_gain, fox_k_gain, swa_q_gain, swa_k_gain, swa_sinks, mem_q_gain, mem_k_gain, w_out, ffn2_norm, ffn2_gate, ffn2_up, ffn2_down, loss_target, m_ffn1_norm, m_ffn1_gate, m_ffn1_up, m_ffn1_down, m_mix_norm, m_mem_norm, m_w_in, m_forget_bias, m_w_mem_k, m_w_mem_v, m_fox_q_gain, m_fox_k_gain, m_swa_q_gain, m_swa_k_gain, m_swa_sinks, m_mem_q_gain, m_mem_k_gain, m_w_out, m_ffn2_norm, m_ffn2_gate, m_ffn2_up, m_ffn2_down, v_ffn1_norm, v_ffn1_gate, v_ffn1_up, v_ffn1_down, v_mix_norm, v_mem_norm, v_w_in, v_forget_bias, v_w_mem_k, v_w_mem_v, v_fox_q_gain, v_fox_k_gain, v_swa_q_gain, v_swa_k_gain, v_swa_sinks, v_mem_q_gain, v_mem_k_gain, v_w_out, v_ffn2_norm, v_ffn2_gate, v_ffn2_up, v_ffn2_down):
    x0 = x[0]
    mem0 = mem[0]
    target = loss_target[0]
    t, d = x0.shape
    d_shard = w_in.shape[1]
    m_len = mem0.shape[0]
    tm = min(t, 512)
    tk = min(t, 512)
    tn = IN_W // 3
    tkw, tnw = min(t, 1024), IN_W // 3

    def swap(a):
        return jnp.swapaxes(a, 1, 2)

    gate1, up1, gate2, up2 = swap(ffn1_gate), swap(ffn1_up), swap(ffn2_gate), swap(ffn2_up)

    local = {
        "g1": gate1[0], "u1": up1[0], "d1": ffn1_down[0],
        "g2": gate2[0], "u2": up2[0], "d2": ffn2_down[0],
        "in": _permute_in(w_in[0]), "out": w_out[0],
        "mkv": jnp.concatenate([w_mem_k[0], w_mem_v[0]], axis=1),
    }
    shard = {k: _cast_bf16(v, f"cast_{k}") for k, v in local.items()}
    sp = _pack_small(fox_q_gain, fox_k_gain, swa_q_gain, swa_k_gain, mem_q_gain, mem_k_gain, forget_bias, swa_sinks)
    wt = {}

    wt["g1"], wt["u1"] = _all_gather([shard["g1"], shard["u1"]], "gather_ffn1_in")
    xn1 = _rms_fwd(x0, ffn1_norm, "ffn1_norm")
    (g1, u1, h1), (half,) = _ffn_up(xn1, wt["g1"], wt["u1"], "ffn1",
                                    rides=[_ride_gather_chips([shard["d1"], shard["in"], shard["out"]])])
    (wt["d1"], wt["in"]), = _only_copies("gather_d1_in_sibling", [_ride_gather_sibling(half[:2])])
    x1, ((wt["out"],), half) = _ffn_down(
        x0, h1, wt["d1"], "ffn1",
        rides=[_ride_gather_sibling(half[2:]), _ride_gather_chips([shard["mkv"], shard["g2"]])])
    w_in_full = wt["in"].reshape(d, IN_W)

    hn = _rms_fwd(x1, mix_norm, "mix_norm")
    proj, ((wt["mkv"], wt["g2"]),) = _mm(
        "proj", [(hn, pl.BlockSpec((tm, d), lambda n, i, k: (i, 0)),
                  w_in_full, pl.BlockSpec((d, tn), lambda n, i, k: (0, n)))],
        "nn", (3, t // tm, 1), jax.ShapeDtypeStruct((t, IN_W), F32), pl.BlockSpec((tm, tn), lambda n, i, k: (i, n)),
        rides=[_ride_gather_sibling(half)])
    w_out_full = wt["out"].reshape(d, d)
    w_mkv_full = wt["mkv"].reshape(d, 2 * MEM_W)
    fq, fk, fv, sq, sk, sv, mq, c_col = _attn_pre(proj, sp, "attn_pre")
    c_row = jnp.transpose(c_col[:, :8])
    c_rep = jnp.broadcast_to(c_row[:FOX_H, :, None], (FOX_H, t, HEAD))

    mn = _rms_fwd(mem0, mem_norm, "mem_norm")
    mkv = _mm("mem_kv", [(mn, pl.BlockSpec((m_len, d), lambda k: (0, 0)),
                          w_mkv_full, pl.BlockSpec((d, 2 * MEM_W), lambda k: (0, 0)))],
              "nn", (1,), jax.ShapeDtypeStruct((m_len, 2 * MEM_W), F32),
              pl.BlockSpec((m_len, 2 * MEM_W), lambda k: (0, 0)))
    mk, mv = _mem_pre(mkv, sp, "mem_pre")

    (o_a, lse_a), (half,) = _fox_fwd(
        fq, fk, fv, c_rep, c_row, "fox_fwd", rides=[_ride_gather_chips([shard["u2"], shard["d2"]])])
    (o_b, lse_b), ((wt["u2"], wt["d2"]),) = _swa_fwd(sq, sk, sv, sp, "swa_fwd", rides=[_ride_gather_sibling(half)])
    o_c, lse_c = _mem_fwd(mq, mk, mv, "mem_fwd")

    def rows_spec(width):
        return pl.BlockSpec((tm, width), lambda i, k: (i, 0))

    def wout_rows(first, width):
        assert first % width == 0
        return pl.BlockSpec((width, d), lambda i, k: (first // width, 0))

    xspec = pl.BlockSpec((tm, d), lambda i, k: (i, 0))
    x2 = _mm(
        "mix_out",
        [(o_a, rows_spec(FOX_W), w_out_full, wout_rows(0, FOX_W)),
         (o_b, rows_spec(SWA_W), w_out_full, wout_rows(FOX_W, SWA_W)),
         (o_c, rows_spec(MEM_W), w_out_full, wout_rows(FOX_W + SWA_W, MEM_W))],
        "nn", (t // tm, 1), jax.ShapeDtypeStruct((t, d), F32), xspec, res=x1, res_spec=xspec)

    xn2 = _rms_fwd(x2, ffn2_norm, "ffn2_norm")
    (g2, u2, h2), _ = _ffn_up(xn2, wt["g2"], wt["u2"], "ffn2")
    x3 = _ffn_down(x2, h2, wt["d2"], "ffn2")

    dy, dyb, sq_err = _loss_head(x3, target, "loss_head")
    loss = lax.psum(0.5 * sq_err[0, 0] / d, ("x", "y", "c"))

    got = {}
    paired = {}
    landed = {}

    def pair(k, part):
        paired[k] = _pair_add(part, got[k], f"pair_{k}")

    (dg2, du2), _ = _ffn_dact(dyb, wt["d2"], g2, u2, "ffn2")
    part_d2 = _ffn_dw(h2, dyb, 0.5, "ffn2_dwd")
    part_g2, ((got["d2"],),) = _ffn_dw(dg2, xn2, 1.0, "ffn2_dwg", rides=[_ride_scatter_sibling([part_d2])])
    pair("d2", part_d2)
    part_u2, ((landed["d2"],), (got["g2"],)) = _ffn_dw(
        du2, xn2, 1.0, "ffn2_dwu", rides=[_ride_scatter_chips([paired["d2"]]), _ride_scatter_sibling([part_g2])])
    pair("g2", part_g2)
    dxn2, ((landed["g2"],), (got["u2"],)) = _ffn_dxn(
        dg2, du2, wt["g2"], wt["u2"], "ffn2_dxn",
        rides=[_ride_scatter_chips([paired["g2"]]), _ride_scatter_sibling([part_u2])])
    pair("u2", part_u2)
    dx2, dx2b, dgain_ffn2 = _rms_bwd(x2, ffn2_norm, dxn2, dy, "ffn2_norm_bwd")

    dmixed = _mm("mix_out_dx", [(dx2b, xspec, w_out_full, pl.BlockSpec((d, d), lambda i, k: (0, 0)))],
                 "nt", (t // tm, 1), jax.ShapeDtypeStruct((t, d), F32), xspec)

    def k_rows(width):
        return pl.BlockSpec((tk, width), lambda j, k: (k, 0))

    part_out = [
        _mm(f"mix_out_dw{i}", [(o, k_rows(width), dx2b, k_rows(d))], "tn", (1, t // tk),
            jax.ShapeDtypeStruct((width, d), BF16), pl.BlockSpec((width, d), lambda j, k: (0, 0)))
        for i, (o, width) in enumerate(((o_a, FOX_W), (o_b, SWA_W), (o_c, MEM_W)))
    ]
    part_out = jnp.concatenate(part_out, axis=0).reshape(N_DEV, d_shard, d)

    delta, delta_rep = _delta(dmixed, o_a, o_b, o_c, "attn_delta")
    (dfq, dfk, dfv, dc_col, dc_row), ((landed["u2"],),) = _fox_bwd(
        fq, fk, fv, c_rep, c_row, dmixed, lse_a, delta_rep, "fox_bwd", rides=[_ride_scatter_chips([paired["u2"]])])
    dsq, dsk, dsv, dsp_sink = _swa_bwd(sq, sk, sv, sp, dmixed, lse_b, delta, "swa_bwd")
    dmq, dmk, dmv = _mem_bwd(mq, mk, mv, dmixed, lse_c, delta, "mem_bwd")

    dmkv, dsp_mem = _mem_post_bwd(mkv, sp, dmk, dmv, "mem_post_bwd")
    part_mkv = _mm("mem_kv_dw", [(mn, pl.BlockSpec((m_len, d), lambda k: (0, 0)),
                                  dmkv, pl.BlockSpec((m_len, 2 * MEM_W), lambda k: (0, 0)))],
                   "tn", (1,), jax.ShapeDtypeStruct((d, 2 * MEM_W), BF16),
                   pl.BlockSpec((d, 2 * MEM_W), lambda k: (0, 0))).reshape(N_DEV, d_shard, 2 * MEM_W)
    dmn = _mm("mem_kv_dx", [(dmkv, pl.BlockSpec((m_len, 2 * MEM_W), lambda k: (0, 0)),
                             w_mkv_full, pl.BlockSpec((d, 2 * MEM_W), lambda k: (0, 0)))],
              "nt", (1,), jax.ShapeDtypeStruct((m_len, d), F32), pl.BlockSpec((m_len, d), lambda k: (0, 0)))
    _, _, dgain_mem = _rms_bwd(mem0, mem_norm, dmn, None, "mem_norm_bwd")

    dc_row_t = _pad_row(jnp.transpose(dc_row), HEAD)
    dproj, dsp_attn = _attn_post_bwd(proj, sp, dfq, dfk, dfv, dsq, dsk, dsv, dmq, dc_col, dc_row_t, "attn_post_bwd")
    dhn, ((got["out"], got["mkv"]),) = _mm(
        "proj_dx", [(dproj, pl.BlockSpec((tm, IN_W), lambda i, k: (i, 0)),
                     w_in_full, pl.BlockSpec((d, IN_W), lambda i, k: (0, 0)))],
        "nt", (t // tm, 1), jax.ShapeDtypeStruct((t, d), F32), xspec,
        rides=[_ride_scatter_sibling([part_out, part_mkv])])
    pair("out", part_out)
    pair("mkv", part_mkv)
    part_in, ((landed["out"], landed["mkv"]),) = _mm(
        "proj_dw", [(hn, pl.BlockSpec((tkw, d), lambda n, k: (k, 0)),
                     dproj, pl.BlockSpec((tkw, tnw), lambda n, k: (k, n)))],
        "tn", (IN_W // tnw, t // tkw), jax.ShapeDtypeStruct((d, IN_W), BF16), pl.BlockSpec((d, tnw), lambda n, k: (0, n)),
        rides=[_ride_scatter_chips([paired["out"], paired["mkv"]])])
    part_in = part_in.reshape(N_DEV, d_shard, IN_W)
    dx1, dx1b, dgain_mix = _rms_bwd(x1, mix_norm, dhn, dx2, "mix_norm_bwd")

    (dg1, du1), ((got["in"],),) = _ffn_dact(dx1b, wt["d1"], g1, u1, "ffn1", rides=[_ride_scatter_sibling([part_in])])
    pair("in", part_in)
    part_d1, ((landed["in"],),) = _ffn_dw(h1, dx1b, 0.5, "ffn1_dwd", rides=[_ride_scatter_chips([paired["in"]])])
    part_g1, ((got["d1"],),) = _ffn_dw(dg1, xn1, 1.0, "ffn1_dwg", rides=[_ride_scatter_sibling([part_d1])])
    pair("d1", part_d1)
    part_u1, ((landed["d1"],), (got["g1"],)) = _ffn_dw(
        du1, xn1, 1.0, "ffn1_dwu", rides=[_ride_scatter_chips([paired["d1"]]), _ride_scatter_sibling([part_g1])])
    pair("g1", part_g1)
    n_half = N_DEV // 2
    dxn1, ((landed["g1"],), (got["u1"],)) = _ffn_dxn(
        dg1, du1, wt["g1"], wt["u1"], "ffn1_dxn_a", shards=(0, n_half),
        rides=[_ride_scatter_chips([paired["g1"]]), _ride_scatter_sibling([part_u1])])
    pair("u1", part_u1)
    dxn1, ((landed["u1"],),) = _ffn_dxn(
        dg1, du1, wt["g1"], wt["u1"], "ffn1_dxn_b", shards=(n_half, N_DEV), res=dxn1,
        rides=[_ride_scatter_chips([paired["u1"]])])
    grad_x, _, dgain_ffn1 = _rms_bwd(x0, ffn1_norm, dxn1, dx1, "ffn1_norm_bwd")

    norms_sum, small_sum = _all_reduce_small(
        [_pack_norms(dgain_ffn1, dgain_mix, dgain_mem, dgain_ffn2), dsp_attn + dsp_sink + dsp_mem], "reduce_small")

    result = {
        "ffn1_gate": map(swap, _sum_adamw(landed["g1"], 0, gate1, swap(m_ffn1_gate), swap(v_ffn1_gate), "adamw_ffn1_gate")),
        "ffn1_up": map(swap, _sum_adamw(landed["u1"], 0, up1, swap(m_ffn1_up), swap(v_ffn1_up), "adamw_ffn1_up")),
        "ffn1_down": _sum_adamw(landed["d1"], 0, ffn1_down, m_ffn1_down, v_ffn1_down, "adamw_ffn1_down"),
        "w_mem_k": _sum_adamw(landed["mkv"], 0, w_mem_k, m_w_mem_k, v_w_mem_k, "adamw_w_mem_k"),
        "w_mem_v": _sum_adamw(landed["mkv"], 1, w_mem_v, m_w_mem_v, v_w_mem_v, "adamw_w_mem_v"),
        "w_out": _sum_adamw(landed["out"], 0, w_out, m_w_out, v_w_out, "adamw_w_out"),
        "ffn2_gate": map(swap, _sum_adamw(landed["g2"], 0, gate2, swap(m_ffn2_gate), swap(v_ffn2_gate), "adamw_ffn2_gate")),
        "ffn2_up": map(swap, _sum_adamw(landed["u2"], 0, up2, swap(m_ffn2_up), swap(v_ffn2_up), "adamw_ffn2_up")),
        "ffn2_down": _sum_adamw(landed["d2"], 0, ffn2_down, m_ffn2_down, v_ffn2_down, "adamw_ffn2_down"),
    }
    grad_in = _unpermute_in(_sum_chips(landed["in"], "sum_w_in"))
    result["w_in"] = (grad_in[None],) + tuple(
        o[None] for o in _adamw(w_in[0], grad_in, m_w_in[0], v_w_in[0], "adamw_w_in"))

    norm_names = ["ffn1_norm", "mix_norm", "mem_norm", "ffn2_norm"]
    norm_w = _pack_norms(ffn1_norm, mix_norm, mem_norm, ffn2_norm)
    norm_m = _pack_norms(m_ffn1_norm, m_mix_norm, m_mem_norm, m_ffn2_norm)
    norm_v = _pack_norms(v_ffn1_norm, v_mix_norm, v_mem_norm, v_ffn2_norm)
    outs = (norms_sum,) + tuple(_adamw(norm_w, norms_sum, norm_m, norm_v, "adamw_norms"))
    for i, k in enumerate(norm_names):
        result[k] = tuple(o[i:i + 1] for o in outs)

    small_names = ["fox_q_gain", "fox_k_gain", "swa_q_gain", "swa_k_gain", "mem_q_gain", "mem_k_gain",
                   "forget_bias", "swa_sinks"]
    small_m = _pack_small(m_fox_q_gain, m_fox_k_gain, m_swa_q_gain, m_swa_k_gain, m_mem_q_gain, m_mem_k_gain,
                          m_forget_bias, m_swa_sinks)
    small_v = _pack_small(v_fox_q_gain, v_fox_k_gain, v_swa_q_gain, v_swa_k_gain, v_mem_q_gain, v_mem_k_gain,
                          v_forget_bias, v_swa_sinks)
    outs = (small_sum,) + tuple(_adamw(sp, small_sum, small_m, small_v, "adamw_small"))
    for i, k in enumerate(small_names):
        width = N_LOGIT if k in ("forget_bias", "swa_sinks") else HEAD
        result[k] = tuple(o[i:i + 1, :width] for o in outs)

    order = ["ffn1_norm", "ffn1_gate", "ffn1_up", "ffn1_down", "mix_norm", "mem_norm", "w_in", "forget_bias",
             "w_mem_k", "w_mem_v", "fox_q_gain", "fox_k_gain", "swa_q_gain", "swa_k_gain", "swa_sinks",
             "mem_q_gain", "mem_k_gain", "w_out", "ffn2_norm", "ffn2_gate", "ffn2_up", "ffn2_down"]
    result = {k: tuple(v) for k, v in result.items()}
    flat = [loss, grad_x[None]]
    for kind in range(4):
        flat += [result[k][kind] for k in order]
    return tuple(flat)
```

```python
import functools

import jax
import jax.numpy as jnp
from jax import lax
from jax.experimental import pallas as pl
from jax.experimental.pallas import tpu as pltpu

F32 = jnp.float32
BF16 = jnp.bfloat16
MESH = pl.DeviceIdType.MESH
ANY = pl.BlockSpec(memory_space=pl.ANY)

N_DEV = 8
EPS = 1e-6
NEG_INF = -1e30
HEAD = 128
FOX_H, SWA_H, SWA_KV, MEM_H = 6, 6, 2, 4
FOX_W, SWA_W, SWA_KV_W, MEM_W = FOX_H * HEAD, SWA_H * HEAD, SWA_KV * HEAD, MEM_H * HEAD
SCALE = HEAD ** -0.5
SWA_BLOCK = 128
C_FQ, C_FK, C_FV = 0, FOX_W, 2 * FOX_W
C_SQ = 3 * FOX_W
C_SK = C_SQ + SWA_W
C_SV = C_SK + SWA_KV_W
C_MQ = C_SV + SWA_KV_W
C_FL = C_MQ + MEM_W
IN_W = C_FL + HEAD
N_LOGIT = FOX_H
R_FQ, R_FK, R_SQ, R_SK, R_MQ, R_MK, R_FB, R_SINK = range(8)
ADAM_LR, ADAM_B1, ADAM_B2, ADAM_EPS, ADAM_WD, ADAM_STEP = 0.001, 0.9, 0.999, 1e-08, 0.01, 10
VMEM_BYTES = 56 * 1024 * 1024

DN = {
    "nn": (((1,), (0,)), ((), ())),
    "nt": (((1,), (1,)), ((), ())),
    "tn": (((0,), (0,)), ((), ())),
}


def _params(n_axes):
    return pltpu.CompilerParams(dimension_semantics=("arbitrary",) * n_axes, vmem_limit_bytes=VMEM_BYTES)


def _dot(a, b, dims="nn"):
    return lax.dot_general(a.astype(BF16), b.astype(BF16), DN[dims], preferred_element_type=F32)


def _sigmoid(x):
    return 0.5 * jnp.tanh(0.5 * x) + 0.5


def _me():
    return lax.axis_index("x"), lax.axis_index("y"), lax.axis_index("c")


def _lin(p):
    return 4 * p[0] + 2 * p[1] + p[2]


def _rows_tile(rows, row_bytes, budget=4 << 20, mult=16):
    best = None
    for k in range(1, rows + 1):
        if rows % k == 0 and (rows // k) % mult == 0 and (rows // k) * row_bytes <= budget:
            best = rows // k
            break
    assert best is not None, (rows, row_bytes)
    return best


class _Ride:
    def __init__(self, inputs, out_shapes, aliases, n_remote, n_local, start, wait):
        self.inputs, self.out_shapes, self.aliases = list(inputs), list(out_shapes), dict(aliases)
        self.n_remote, self.n_local, self.start, self.wait = n_remote, n_local, start, wait


def _remote(src, dst, send, recv, k, to):
    return pltpu.make_async_remote_copy(src_ref=src, dst_ref=dst, send_sem=send.at[k], recv_sem=recv.at[k],
                                        device_id=to, device_id_type=MESH)


def _other_chips(x, y):
    return [(1 - x, y), (x, 1 - y), (1 - x, 1 - y)]


ALL_CHIPS = [(0, 0), (0, 1), (1, 0), (1, 1)]


def _ride_gather_chips(xs):
    n = len(xs)

    def copies(ins, outs, send, recv):
        x, y, c = _me()
        out = []
        for a in range(n):
            for j, chip in enumerate(_other_chips(x, y)):
                peer = (*chip, c)
                out.append((_remote(ins[a], outs[a].at[_lin((x, y, c))], send, recv, 3 * a + j, peer),
                            _remote(ins[a], outs[a].at[_lin(peer)], send, recv, 3 * a + j, peer)))
        return out

    def mine(ins, outs, local):
        me = _lin(_me())
        return [pltpu.make_async_copy(ins[a], outs[a].at[me], local.at[a]) for a in range(n)]

    def start(ins, outs, send, recv, local):
        for cp in mine(ins, outs, local):
            cp.start()
        for sent, _ in copies(ins, outs, send, recv):
            sent.start()

    def wait(ins, outs, send, recv, local):
        for sent, landed in copies(ins, outs, send, recv):
            landed.wait_recv()
            sent.wait_send()
        for cp in mine(ins, outs, local):
            cp.wait()

    shapes = [jax.ShapeDtypeStruct((N_DEV,) + x.shape, x.dtype) for x in xs]
    return _Ride(xs, shapes, {}, 3 * n, n, start, wait)


def _ride_gather_sibling(bufs):
    n = len(bufs)

    def copies(outs, send, recv):
        x, y, c = _me()
        out = []
        for a in range(n):
            for q, (px, py) in enumerate(ALL_CHIPS):
                there = outs[a].at[4 * px + 2 * py + c]
                here = outs[a].at[4 * px + 2 * py + 1 - c]
                out.append((_remote(there, there, send, recv, 4 * a + q, (x, y, 1 - c)),
                            _remote(here, here, send, recv, 4 * a + q, (x, y, 1 - c))))
        return out

    def start(ins, outs, send, recv, local):
        for sent, _ in copies(outs, send, recv):
            sent.start()

    def wait(ins, outs, send, recv, local):
        for sent, landed in copies(outs, send, recv):
            landed.wait_recv()
            sent.wait_send()

    shapes = [jax.ShapeDtypeStruct(b.shape, b.dtype) for b in bufs]
    return _Ride(bufs, shapes, {a: a for a in range(n)}, 4 * n, 0, start, wait)


def _ride_gather(xs, mid_frac):
    n = len(xs)
    chips = _ride_gather_chips(xs)

    def sibling_copies(outs, send, recv):
        x, y, c = _me()
        out = []
        for a in range(n):
            for q, (px, py) in enumerate(ALL_CHIPS):
                there = outs[a].at[4 * px + 2 * py + c]
                here = outs[a].at[4 * px + 2 * py + 1 - c]
                k = 3 * n + 4 * a + q
                out.append((_remote(there, there, send, recv, k, (x, y, 1 - c)),
                            _remote(here, here, send, recv, k, (x, y, 1 - c))))
        return out

    def mid(ins, outs, send, recv, local):
        chips.wait(ins, outs, send, recv, local)
        for sent, _ in sibling_copies(outs, send, recv):
            sent.start()

    def wait(ins, outs, send, recv, local):
        for sent, landed in sibling_copies(outs, send, recv):
            landed.wait_recv()
            sent.wait_send()

    ride = _Ride(xs, chips.out_shapes, {}, 7 * n, n, chips.start, wait)
    ride.mid, ride.mid_frac = mid, mid_frac
    return ride


def _ride_scatter_sibling(parts):
    n = len(parts)

    def copies(ins, outs, send, recv):
        x, y, c = _me()
        out = []
        for a in range(n):
            for q, (px, py) in enumerate(ALL_CHIPS):
                cp = _remote(ins[a].at[4 * px + 2 * py + 1 - c], outs[a].at[q], send, recv, 4 * a + q, (x, y, 1 - c))
                out.append(cp)
        return out

    def start(ins, outs, send, recv, local):
        for cp in copies(ins, outs, send, recv):
            cp.start()

    def wait(ins, outs, send, recv, local):
        for cp in copies(ins, outs, send, recv):
            cp.wait_recv()
            cp.wait_send()

    shapes = [jax.ShapeDtypeStruct((4,) + p.shape[1:], p.dtype) for p in parts]
    return _Ride(parts, shapes, {}, 4 * n, 0, start, wait)


def _ride_scatter_chips(pairs):
    n = len(pairs)

    def copies(ins, outs, send, recv):
        x, y, c = _me()
        out = []
        for a in range(n):
            for j, (px, py) in enumerate(_other_chips(x, y)):
                peer = (px, py, c)
                out.append((_remote(ins[a].at[2 * px + py], outs[a].at[2 * x + y], send, recv, 3 * a + j, peer),
                            _remote(ins[a].at[2 * px + py], outs[a].at[2 * px + py], send, recv, 3 * a + j, peer)))
        return out

    def mine(ins, outs, local):
        x, y, _ = _me()
        return [pltpu.make_async_copy(ins[a].at[2 * x + y], outs[a].at[2 * x + y], local.at[a]) for a in range(n)]

    def start(ins, outs, send, recv, local):
        for cp in mine(ins, outs, local):
            cp.start()
        for sent, _ in copies(ins, outs, send, recv):
            sent.start()

    def wait(ins, outs, send, recv, local):
        for sent, landed in copies(ins, outs, send, recv):
            landed.wait_recv()
            sent.wait_send()
        for cp in mine(ins, outs, local):
            cp.wait()

    shapes = [jax.ShapeDtypeStruct(p.shape, p.dtype) for p in pairs]
    return _Ride(pairs, shapes, {}, 3 * n, n, start, wait)


def _call(name, body, grid, in_specs, out_specs, out_shape, operands, scratch=(), rides=()):
    n_in, n_out, n_scr = len(operands), len(out_shape), len(scratch)
    ride_in, ride_out, ride_scr, aliases, spans = [], [], [], {}, []
    for r in rides:
        for i, o in r.aliases.items():
            aliases[n_in + len(ride_in) + i] = n_out + len(ride_out) + o
        spans.append((len(ride_in), len(r.inputs), len(ride_out), len(r.out_shapes)))
        ride_in += r.inputs
        ride_out += r.out_shapes
        ride_scr += [pltpu.SemaphoreType.DMA((r.n_remote,)), pltpu.SemaphoreType.DMA((r.n_remote,)),
                     pltpu.SemaphoreType.DMA((max(r.n_local, 1),))]

    def wrapped(*refs):
        c_in, r_in = refs[:n_in], refs[n_in:n_in + len(ride_in)]
        p = n_in + len(ride_in)
        c_out, r_out = refs[p:p + n_out], refs[p + n_out:p + n_out + len(ride_out)]
        p += n_out + len(ride_out)
        c_scr, r_scr = refs[p:p + n_scr], refs[p + n_scr:]

        n_steps = functools.reduce(lambda a, b: a * b, grid, 1)
        step = functools.reduce(lambda acc, ax: acc * grid[ax] + pl.program_id(ax), range(len(grid)), 0)

        def each(method, at):
            for k, (r, (i0, ni, o0, no)) in enumerate(zip(rides, spans)):
                fn = getattr(r, method, None)
                if fn is None:
                    continue
                run = functools.partial(fn, r_in[i0:i0 + ni], r_out[o0:o0 + no], *r_scr[3 * k:3 * k + 3])
                if grid:
                    pl.when(step == at(r))(run)
                else:
                    run()

        each("start", lambda r: 0)
        each("mid", lambda r: int(r.mid_frac * (n_steps - 1)))
        body(*c_in, *c_out, *c_scr)
        each("wait", lambda r: n_steps - 1)

    outs = pl.pallas_call(
        wrapped, grid=grid, in_specs=list(in_specs) + [ANY] * len(ride_in),
        out_specs=list(out_specs) + [ANY] * len(ride_out), out_shape=list(out_shape) + ride_out,
        scratch_shapes=list(scratch) + ride_scr, input_output_aliases=aliases,
        compiler_params=_params(len(grid)), name=name)(*operands, *ride_in)
    outs = list(outs)
    ride_results = [outs[n_out + o0:n_out + o0 + no] for (_, _, o0, no) in spans]
    return outs[:n_out], ride_results


def _only_copies(name, rides):
    return _call(name, lambda: None, (), [], [], [], [], rides=rides)[1]


def _mm(name, pairs, dims, grid, out_shape, out_spec, res=None, res_spec=None, alpha=1.0, rides=()):
    n = len(pairs)
    nk = grid[-1]
    kax = len(grid) - 1
    acc_shape = tuple(d for d in out_spec.block_shape if d is not None)

    def body(*refs):
        pos = 2 * n
        r_ref = None
        if res is not None:
            r_ref = refs[pos]
            pos += 1
        o_ref = refs[pos]
        part = None
        for p in range(n):
            d = _dot(refs[2 * p][...], refs[2 * p + 1][...], dims)
            part = d if part is None else part + d

        def finish(acc):
            if alpha != 1.0:
                acc = acc * alpha
            if r_ref is not None:
                acc = r_ref[...] + acc
            o_ref[...] = acc.astype(o_ref.dtype)

        if nk == 1:
            finish(part)
        else:
            acc_ref = refs[pos + 1]
            k = pl.program_id(kax)

            @pl.when(k == 0)
            def _():
                acc_ref[...] = part

            @pl.when(k > 0)
            def _():
                acc_ref[...] += part

            @pl.when(k == nk - 1)
            def _():
                finish(acc_ref[...])

    operands, in_specs = [], []
    for a, a_spec, b, b_spec in pairs:
        operands += [a, b]
        in_specs += [a_spec, b_spec]
    if res is not None:
        operands.append(res)
        in_specs.append(res_spec)
    (out,), ride_results = _call(name, body, grid, in_specs, [out_spec], [out_shape], operands,
                                 scratch=[pltpu.VMEM(acc_shape, F32)] if nk > 1 else [], rides=rides)
    return (out, ride_results) if rides else out


def _cast_bf16(x, name):
    rows, cols = x.shape
    tm = _rows_tile(rows, cols * 4)

    def body(x_ref, o_ref):
        o_ref[...] = x_ref[...].astype(BF16)

    spec = pl.BlockSpec((tm, cols), lambda i: (i, 0))
    return pl.pallas_call(body, grid=(rows // tm,), in_specs=[spec], out_specs=spec,
                          out_shape=jax.ShapeDtypeStruct(x.shape, BF16), compiler_params=_params(1), name=name)(x)


def _rms_fwd(x, gain, name):
    rows, d = x.shape
    tm = min(rows, 512)

    def body(x_ref, g_ref, o_ref):
        xv = x_ref[...]
        r = lax.rsqrt(jnp.mean(xv * xv, axis=-1, keepdims=True) + EPS)
        o_ref[...] = (xv * r * g_ref[...]).astype(BF16)

    spec = pl.BlockSpec((tm, d), lambda i: (i, 0))
    return pl.pallas_call(body, grid=(rows // tm,), in_specs=[spec, pl.BlockSpec((1, d), lambda i: (0, 0))],
                          out_specs=spec, out_shape=jax.ShapeDtypeStruct(x.shape, BF16),
                          compiler_params=_params(1), name=name)(x, gain)


def _rms_bwd(x, gain, dxn, dres, name):
    rows, d = x.shape
    tm = min(rows, 256)
    with_res = dres is not None

    def body(*refs):
        if with_res:
            x_ref, g_ref, dy_ref, r_ref, dx_ref, dxb_ref, dg_ref = refs
        else:
            x_ref, g_ref, dy_ref, dx_ref, dxb_ref, dg_ref = refs
        xv = x_ref[...]
        r = lax.rsqrt(jnp.mean(xv * xv, axis=-1, keepdims=True) + EPS)
        xh = xv * r
        dy = dy_ref[...]
        dxh = dy * g_ref[...]
        dx = r * (dxh - xh * jnp.mean(dxh * xh, axis=-1, keepdims=True))
        if with_res:
            dx = dx + r_ref[...]
        dx_ref[...] = dx
        dxb_ref[...] = dx.astype(BF16)
        part = jnp.sum(dy * xh, axis=0, keepdims=True)

        @pl.when(pl.program_id(0) == 0)
        def _():
            dg_ref[...] = part

        @pl.when(pl.program_id(0) > 0)
        def _():
            dg_ref[...] += part

    spec = pl.BlockSpec((tm, d), lambda i: (i, 0))
    vec = pl.BlockSpec((1, d), lambda i: (0, 0))
    ops = [x, gain, dxn] + ([dres] if with_res else [])
    return pl.pallas_call(
        body, grid=(rows // tm,), in_specs=[spec, vec, spec] + ([spec] if with_res else []),
        out_specs=[spec, spec, vec],
        out_shape=[jax.ShapeDtypeStruct(x.shape, F32), jax.ShapeDtypeStruct(x.shape, BF16),
                   jax.ShapeDtypeStruct((1, d), F32)],
        compiler_params=_params(1), name=name)(*ops)


def _loss_head(y, target, name):
    rows, d = y.shape
    tm = min(rows, 256)

    def body(y_ref, t_ref, dy_ref, dyb_ref, acc_ref):
        err = y_ref[...] - t_ref[...]
        dy = err * (1.0 / d)
        dy_ref[...] = dy
        dyb_ref[...] = dy.astype(BF16)
        part = jnp.zeros((8, 128), F32) + jnp.sum(err * err)

        @pl.when(pl.program_id(0) == 0)
        def _():
            acc_ref[...] = part

        @pl.when(pl.program_id(0) > 0)
        def _():
            acc_ref[...] += part

    spec = pl.BlockSpec((tm, d), lambda i: (i, 0))
    return pl.pallas_call(
        body, grid=(rows // tm,), in_specs=[spec, spec],
        out_specs=[spec, spec, pl.BlockSpec((8, 128), lambda i: (0, 0))],
        out_shape=[jax.ShapeDtypeStruct(y.shape, F32), jax.ShapeDtypeStruct(y.shape, BF16),
                   jax.ShapeDtypeStruct((8, 128), F32)],
        compiler_params=_params(1), name=name)(y, target)


ROW_CHUNK = 256


def _ffn_up(xn, wg, wu, tag, rides=()):
    t, d = xn.shape
    nd, fs, _ = wg.shape
    tm = min(t, 512)
    rc = min(tm, ROW_CHUNK)

    def body(x_ref, wg_ref, wu_ref, g_ref, u_ref, h_ref):
        for r in range(0, tm, rc):
            xv = x_ref[r:r + rc, :]
            g = _dot(xv, wg_ref[...], "nt")
            u = _dot(xv, wu_ref[...], "nt")
            g_ref[r:r + rc, :] = g.astype(BF16)
            u_ref[r:r + rc, :] = u.astype(BF16)
            h_ref[r:r + rc, :] = (g * _sigmoid(g) * u).astype(BF16)

    wspec = pl.BlockSpec((None, fs, d), lambda j, i: (j, 0, 0))
    hspec = pl.BlockSpec((None, tm, fs), lambda j, i: (j, i, 0))
    hid = jax.ShapeDtypeStruct((nd, t, fs), BF16)
    return _call(f"{tag}_up", body, (nd, t // tm), [pl.BlockSpec((tm, d), lambda j, i: (i, 0)), wspec, wspec],
                 [hspec] * 3, [hid] * 3, [xn, wg, wu], rides=rides)


SHARDS_PER_STEP = 2


def _ffn_down(x, h, wd, tag, rides=()):
    nd, t, fs = h.shape
    d = x.shape[1]
    tm = min(t, 512)
    sps = SHARDS_PER_STEP
    xspec = pl.BlockSpec((tm, d), lambda i, k: (i, 0))
    pairs = [(h, pl.BlockSpec((None, tm, fs), lambda i, k, s=s: (sps * k + s, i, 0)),
              wd, pl.BlockSpec((None, fs, d), lambda i, k, s=s: (sps * k + s, 0, 0))) for s in range(sps)]
    return _mm(f"{tag}_down", pairs, "nn", (t // tm, nd // sps), jax.ShapeDtypeStruct((t, d), F32), xspec,
               res=x, res_spec=xspec, alpha=0.5, rides=rides)


def _ffn_dact(dyb, wd, g, u, tag, rides=()):
    nd, t, fs = g.shape
    d = dyb.shape[1]
    tm = min(t, 512)
    rc = min(tm, ROW_CHUNK)

    def body(dy_ref, wd_ref, g_ref, u_ref, dg_ref, du_ref):
        for r in range(0, tm, rc):
            dh = 0.5 * _dot(dy_ref[r:r + rc, :], wd_ref[...], "nt")
            gv = g_ref[r:r + rc, :].astype(F32)
            uv = u_ref[r:r + rc, :].astype(F32)
            sig = _sigmoid(gv)
            du_ref[r:r + rc, :] = (dh * gv * sig).astype(BF16)
            dg_ref[r:r + rc, :] = (dh * uv * sig * (1.0 + gv * (1.0 - sig))).astype(BF16)

    hspec = pl.BlockSpec((None, tm, fs), lambda j, i: (j, i, 0))
    hid = jax.ShapeDtypeStruct((nd, t, fs), BF16)
    return _call(f"{tag}_dact", body, (nd, t // tm),
                 [pl.BlockSpec((tm, d), lambda j, i: (i, 0)), pl.BlockSpec((None, fs, d), lambda j, i: (j, 0, 0)),
                  hspec, hspec], [hspec] * 2, [hid] * 2, [dyb, wd, g, u], rides=rides)


def _ffn_dw(hid, act, alpha, name, rides=()):
    nd, t, fs = hid.shape
    d = act.shape[1]
    tk = min(t, 2048)
    return _mm(name, [(hid, pl.BlockSpec((None, tk, fs), lambda j, k: (j, k, 0)),
                       act, pl.BlockSpec((tk, d), lambda j, k: (k, 0)))], "tn", (nd, t // tk),
               jax.ShapeDtypeStruct((nd, fs, d), BF16), pl.BlockSpec((None, fs, d), lambda j, k: (j, 0, 0)),
               alpha=alpha, rides=rides)


def _ffn_dxn(dg, du, wg, wu, name, rides=(), shards=None, res=None):
    nd, t, fs = dg.shape
    d = wg.shape[2]
    tm = min(t, 512)
    sps = SHARDS_PER_STEP
    lo, hi = (0, nd) if shards is None else shards
    pairs = []
    for s in range(sps):
        am = pl.BlockSpec((None, tm, fs), lambda i, k, s=s: (lo + sps * k + s, i, 0))
        wk = pl.BlockSpec((None, fs, d), lambda i, k, s=s: (lo + sps * k + s, 0, 0))
        pairs += [(dg, am, wg, wk), (du, am, wu, wk)]
    xspec = pl.BlockSpec((tm, d), lambda i, k: (i, 0))
    return _mm(name, pairs, "nn", (t // tm, (hi - lo) // sps), jax.ShapeDtypeStruct((t, d), F32), xspec,
               res=res, res_spec=xspec if res is not None else None, rides=rides)


def _head_norm(x, gain):
    r = lax.rsqrt(jnp.mean(x * x, axis=-1, keepdims=True) + EPS)
    return x * r * gain


def _head_norm_bwd(x, gain, dy):
    r = lax.rsqrt(jnp.mean(x * x, axis=-1, keepdims=True) + EPS)
    xh = x * r
    dxh = dy * gain
    dx = r * (dxh - xh * jnp.mean(dxh * xh, axis=-1, keepdims=True))
    return dx, jnp.sum(dy * xh, axis=0, keepdims=True)


def _hs(h, base=0):
    return slice(base + h * HEAD, base + (h + 1) * HEAD)


def _tri(n, lower):
    r = lax.broadcasted_iota(jnp.int32, (n, n), 0)
    c = lax.broadcasted_iota(jnp.int32, (n, n), 1)
    return ((r >= c) if lower else (r <= c)).astype(F32)


def _attn_pre(proj, sp, name):
    t = proj.shape[0]
    tm = min(t, 256)

    def body(p_ref, sp_ref, fq, fk, fv, sq, sk, sv, mq, cc, carry):
        @pl.when(pl.program_id(0) == 0)
        def _():
            carry[...] = jnp.zeros_like(carry)

        for h in range(FOX_H):
            fq[:, _hs(h)] = _head_norm(p_ref[:, _hs(h, C_FQ)], sp_ref[R_FQ:R_FQ + 1, :]).astype(BF16)
            fk[:, _hs(h)] = _head_norm(p_ref[:, _hs(h, C_FK)], sp_ref[R_FK:R_FK + 1, :]).astype(BF16)
        fv[...] = p_ref[:, C_FV:C_FV + FOX_W].astype(BF16)
        for h in range(SWA_H):
            sq[:, _hs(h)] = _head_norm(p_ref[:, _hs(h, C_SQ)], sp_ref[R_SQ:R_SQ + 1, :]).astype(BF16)
        for h in range(SWA_KV):
            sk[:, _hs(h)] = _head_norm(p_ref[:, _hs(h, C_SK)], sp_ref[R_SK:R_SK + 1, :]).astype(BF16)
        sv[...] = p_ref[:, C_SV:C_SV + SWA_KV_W].astype(BF16)
        for h in range(MEM_H):
            mq[:, _hs(h)] = _head_norm(p_ref[:, _hs(h, C_MQ)], sp_ref[R_MQ:R_MQ + 1, :]).astype(BF16)
        z = p_ref[:, C_FL:C_FL + HEAD] + sp_ref[R_FB:R_FB + 1, :]
        lane = lax.broadcasted_iota(jnp.int32, z.shape, 1)
        log_f = jnp.minimum(z, 0.0) - jnp.log(1.0 + jnp.exp(-jnp.abs(z)))
        log_f = jnp.where(lane < N_LOGIT, log_f, 0.0)
        c = jnp.dot(_tri(tm, True), log_f, precision=lax.Precision.HIGHEST, preferred_element_type=F32)
        c = c + carry[0:1, :]
        cc[...] = c
        carry[...] = jnp.broadcast_to(c[tm - 1:tm, :], carry.shape)

    def rows(w):
        return pl.BlockSpec((tm, w), lambda i: (i, 0))

    def shape(w, dt):
        return jax.ShapeDtypeStruct((t, w), dt)

    widths = [FOX_W, FOX_W, FOX_W, SWA_W, SWA_KV_W, SWA_KV_W, MEM_W]
    return pl.pallas_call(
        body, grid=(t // tm,), in_specs=[rows(IN_W), pl.BlockSpec((16, 128), lambda i: (0, 0))],
        out_specs=[rows(w) for w in widths] + [rows(HEAD)],
        out_shape=[shape(w, BF16) for w in widths] + [shape(HEAD, F32)],
        scratch_shapes=[pltpu.VMEM((8, 128), F32)], compiler_params=_params(1), name=name)(proj, sp)


def _attn_post_bwd(proj, sp, dfq, dfk, dfv, dsq, dsk, dsv, dmq, dc_col, dc_row_t, name):
    t = proj.shape[0]
    tm = min(t, 256)
    nb = t // tm

    def body(p_ref, sp_ref, dfq_r, dfk_r, dfv_r, dsq_r, dsk_r, dsv_r, dmq_r, dcc_r, dcr_r, dp_ref, dsp_ref, carry):
        @pl.when(pl.program_id(0) == 0)
        def _():
            carry[...] = jnp.zeros_like(carry)
            dsp_ref[...] = jnp.zeros_like(dsp_ref)

        def group(n_heads, col, row, d_ref):
            total = None
            for h in range(n_heads):
                dx, dg = _head_norm_bwd(p_ref[:, _hs(h, col)], sp_ref[row:row + 1, :], d_ref[:, _hs(h)])
                dp_ref[:, _hs(h, col)] = dx.astype(BF16)
                total = dg if total is None else total + dg
            dsp_ref[row:row + 1, :] += total

        group(FOX_H, C_FQ, R_FQ, dfq_r)
        group(FOX_H, C_FK, R_FK, dfk_r)
        dp_ref[:, C_FV:C_FV + FOX_W] = dfv_r[...].astype(BF16)
        group(SWA_H, C_SQ, R_SQ, dsq_r)
        group(SWA_KV, C_SK, R_SK, dsk_r)
        dp_ref[:, C_SV:C_SV + SWA_KV_W] = dsv_r[...].astype(BF16)
        group(MEM_H, C_MQ, R_MQ, dmq_r)
        dc = dcc_r[...] - dcr_r[...]
        rc = jnp.dot(_tri(tm, False), dc, precision=lax.Precision.HIGHEST, preferred_element_type=F32)
        rc = rc + carry[0:1, :]
        carry[...] = jnp.broadcast_to(rc[0:1, :], carry.shape)
        z = p_ref[:, C_FL:C_FL + HEAD] + sp_ref[R_FB:R_FB + 1, :]
        dz = rc * _sigmoid(-z)
        dp_ref[:, C_FL:C_FL + HEAD] = dz.astype(BF16)
        dsp_ref[R_FB:R_FB + 1, :] += jnp.sum(dz, axis=0, keepdims=True)

    def rows(w):
        return pl.BlockSpec((tm, w), lambda i: (nb - 1 - i, 0))

    small = pl.BlockSpec((16, 128), lambda i: (0, 0))
    widths = [FOX_W, FOX_W, FOX_W, SWA_W, SWA_KV_W, SWA_KV_W, MEM_W, HEAD, HEAD]
    return pl.pallas_call(
        body, grid=(nb,), in_specs=[rows(IN_W), small] + [rows(w) for w in widths],
        out_specs=[rows(IN_W), small],
        out_shape=[jax.ShapeDtypeStruct((t, IN_W), BF16), jax.ShapeDtypeStruct((16, 128), F32)],
        scratch_shapes=[pltpu.VMEM((8, 128), F32)], compiler_params=_params(1), name=name,
    )(proj, sp, dfq, dfk, dfv, dsq, dsk, dsv, dmq, dc_col, dc_row_t)


def _head_column(values):
    rows = values[0].shape[0]
    lane = lax.broadcasted_iota(jnp.int32, (rows, HEAD), 1)
    out = jnp.zeros((rows, HEAD), F32)
    for h, v in enumerate(values):
        out = jnp.where(lane == h, v, out)
    return out


def _head_row(values, n_rows=8):
    cols = values[0].shape[1]
    sub = lax.broadcasted_iota(jnp.int32, (n_rows, cols), 0)
    out = jnp.zeros((n_rows, cols), F32)
    for h, v in enumerate(values):
        out = jnp.where(sub == h, v, out)
    return out


def _delta(dmixed, o_a, o_b, o_c, name):
    t = dmixed.shape[0]
    tm = min(t, 512)

    def body(d_ref, a_ref, b_ref, c_ref, o_ref, rep_ref):
        cols = []
        for ref, n_heads, base in ((a_ref, FOX_H, 0), (b_ref, SWA_H, FOX_W), (c_ref, MEM_H, FOX_W + SWA_W)):
            for h in range(n_heads):
                cols.append(jnp.sum(d_ref[:, _hs(h, base)] * ref[:, _hs(h)], axis=-1, keepdims=True))
        o_ref[...] = _head_column(cols)
        for h in range(FOX_H):
            rep_ref[h] = jnp.broadcast_to(cols[h], (tm, HEAD))

    def rows(w):
        return pl.BlockSpec((tm, w), lambda i: (i, 0))

    return pl.pallas_call(body, grid=(t // tm,), in_specs=[rows(dmixed.shape[1]), rows(FOX_W), rows(SWA_W), rows(MEM_W)],
                          out_specs=[rows(HEAD), pl.BlockSpec((FOX_H, tm, HEAD), lambda i: (0, i, 0))],
                          out_shape=[jax.ShapeDtypeStruct((t, HEAD), F32), jax.ShapeDtypeStruct((FOX_H, t, HEAD), F32)],
                          compiler_params=_params(1), name=name)(dmixed, o_a, o_b, o_c)


def _fox_fwd(fq, fk, fv, c_rep, c_row, name, rides=()):
    t = fq.shape[0]
    tb = min(t, 512)
    nb = t // tb
    n_tiles = tb // HEAD

    def body(q_ref, k_ref, v_ref, cc_ref, cr_ref, o_ref, lse_ref, m_s, l_s, acc_s):
        qi, ki = pl.program_id(0), pl.program_id(1)

        @pl.when(ki == 0)
        def _():
            m_s[...] = jnp.full_like(m_s, NEG_INF)
            l_s[...] = jnp.zeros_like(l_s)
            acc_s[...] = jnp.zeros_like(acc_s)

        def step(diagonal):
            if diagonal:
                r = lax.broadcasted_iota(jnp.int32, (tb, HEAD), 0)
                c = lax.broadcasted_iota(jnp.int32, (tb, HEAD), 1)
            for h in range(FOX_H):
                s = _dot(q_ref[:, _hs(h)], k_ref[:, _hs(h)], "nt")
                cc = cc_ref[h]
                tiles, m_cur = [], None
                for j in range(n_tiles):
                    st = s[:, _hs(j)] * SCALE + cc - cr_ref[h:h + 1, _hs(j)]
                    if diagonal:
                        st = jnp.where(r >= c + j * HEAD, st, NEG_INF)
                    tiles.append(st)
                    m_cur = st if m_cur is None else jnp.maximum(m_cur, st)
                m_prev = m_s[h]
                m_new = jnp.maximum(m_prev, jnp.max(m_cur, axis=-1, keepdims=True))
                alpha = jnp.exp(m_prev - m_new)
                ps = [jnp.exp(st - m_new) for st in tiles]
                l_cur = ps[0]
                for p in ps[1:]:
                    l_cur = l_cur + p
                l_s[h] = alpha * l_s[h] + jnp.sum(l_cur, axis=-1, keepdims=True)
                p = jnp.concatenate([p.astype(BF16) for p in ps], axis=1)
                acc_s[:, _hs(h)] = alpha * acc_s[:, _hs(h)] + _dot(p, v_ref[:, _hs(h)])
                m_s[h] = m_new

        @pl.when(ki < qi)
        def _():
            step(False)

        @pl.when(ki == qi)
        def _():
            step(True)
            for h in range(FOX_H):
                o_ref[:, _hs(h)] = acc_s[:, _hs(h)] / l_s[h]
                lse_ref[h] = m_s[h] + jnp.log(l_s[h])

    qspec = pl.BlockSpec((tb, FOX_W), lambda i, j: (i, 0))
    kspec = pl.BlockSpec((tb, FOX_W), lambda i, j: (jnp.minimum(i, j), 0))
    rep = pl.BlockSpec((FOX_H, tb, HEAD), lambda i, j: (0, i, 0))
    return _call(
        name, body, (nb, nb),
        [qspec, kspec, kspec, rep, pl.BlockSpec((8, tb), lambda i, j: (0, jnp.minimum(i, j)))],
        [qspec, rep],
        [jax.ShapeDtypeStruct((t, FOX_W), F32), jax.ShapeDtypeStruct((FOX_H, t, HEAD), F32)],
        [fq, fk, fv, c_rep, c_row],
        scratch=[pltpu.VMEM((FOX_H, tb, HEAD), F32), pltpu.VMEM((FOX_H, tb, HEAD), F32), pltpu.VMEM((tb, FOX_W), F32)],
        rides=rides)


def _fox_bwd(fq, fk, fv, c_rep, c_row, dmixed, lse, delta, name, rides=()):
    t = fq.shape[0]
    tb = min(t, 512)
    nb = t // tb
    n_tiles = tb // HEAD

    def body(q_ref, k_ref, v_ref, cc_ref, cr_ref, do_ref, lse_ref, dl_ref,
             dq_ref, dk_ref, dv_ref, dcc_ref, dcr_ref):
        ki, qi = pl.program_id(0), pl.program_id(1)

        @pl.when((ki == 0) & (qi == 0))
        def _():
            dq_ref[...] = jnp.zeros_like(dq_ref)
            dcc_ref[...] = jnp.zeros_like(dcc_ref)

        @pl.when(qi == 0)
        def _():
            dk_ref[...] = jnp.zeros_like(dk_ref)
            dv_ref[...] = jnp.zeros_like(dv_ref)
            dcr_ref[...] = jnp.zeros_like(dcr_ref)

        def step(diagonal):
            rows = pl.ds(pl.multiple_of(qi * tb, tb), tb)
            if diagonal:
                r = lax.broadcasted_iota(jnp.int32, (tb, HEAD), 0)
                c = lax.broadcasted_iota(jnp.int32, (tb, HEAD), 1)
            row_sums, col_sums = [], []
            for h in range(FOX_H):
                q, k, v, do = q_ref[:, _hs(h)], k_ref[:, _hs(h)], v_ref[:, _hs(h)], do_ref[:, _hs(h)]
                s = _dot(q, k, "nt")
                dp = _dot(do, v, "nt")
                cc, lse_h, dl_h = cc_ref[h], lse_ref[h], dl_ref[h]
                ps, dss, row = [], [], None
                for j in range(n_tiles):
                    st = s[:, _hs(j)] * SCALE + cc - cr_ref[h:h + 1, _hs(j)]
                    if diagonal:
                        st = jnp.where(r >= c + j * HEAD, st, NEG_INF)
                    pt = jnp.exp(st - lse_h)
                    dst = pt * (dp[:, _hs(j)] - dl_h)
                    ps.append(pt.astype(BF16))
                    dss.append(dst)
                    row = dst if row is None else row + dst
                p = jnp.concatenate(ps, axis=1)
                ds = jnp.concatenate(dss, axis=1)
                dsb = ds.astype(BF16)
                dv_ref[:, _hs(h)] += _dot(p, do, "tn")
                dk_ref[:, _hs(h)] += _dot(dsb, q, "tn") * SCALE
                dq_ref[rows, _hs(h)] += _dot(dsb, k) * SCALE
                row_sums.append(jnp.sum(row, axis=1, keepdims=True))
                col_sums.append(jnp.sum(ds, axis=0, keepdims=True))
            dcc_ref[rows, :] += _head_column(row_sums)
            dcr_ref[...] += _head_row(col_sums)

        @pl.when(qi > ki)
        def _():
            step(False)

        @pl.when(qi == ki)
        def _():
            step(True)

    def qmap(j, i):
        return (jnp.maximum(i, j), 0)

    qspec = pl.BlockSpec((tb, FOX_W), qmap)
    kspec = pl.BlockSpec((tb, FOX_W), lambda j, i: (j, 0))
    rep = pl.BlockSpec((FOX_H, tb, HEAD), lambda j, i: (0, jnp.maximum(i, j), 0))
    rowspec = pl.BlockSpec((8, tb), lambda j, i: (0, j))
    return _call(
        name, body, (nb, nb), [qspec, kspec, kspec, rep, rowspec, qspec, rep, rep],
        [pl.BlockSpec((t, FOX_W), lambda j, i: (0, 0)), kspec, kspec,
         pl.BlockSpec((t, HEAD), lambda j, i: (0, 0)), rowspec],
        [jax.ShapeDtypeStruct((t, FOX_W), F32)] * 3 + [jax.ShapeDtypeStruct((t, HEAD), F32),
                                                       jax.ShapeDtypeStruct((8, t), F32)],
        [fq, fk, fv, c_rep, c_row, dmixed, lse, delta], rides=rides)


def _swa_logits(q, k_cur, k_prev, slope, first_block):
    w = SWA_BLOCK
    r = lax.broadcasted_iota(jnp.int32, (w, w), 0)
    j = lax.broadcasted_iota(jnp.int32, (w, w), 1)
    dist_cur = r - j
    dist_prev = w + r - j
    s_cur = _dot(q, k_cur, "nt") * SCALE - slope * dist_cur.astype(F32)
    s_cur = jnp.where(dist_cur >= 0, s_cur, NEG_INF)
    s_prev = _dot(q, k_prev, "nt") * SCALE - slope * dist_prev.astype(F32)
    s_prev = jnp.where((j > r) & jnp.logical_not(first_block), s_prev, NEG_INF)
    return s_cur, s_prev


def _slope(h):
    return float(2.0 ** (-8.0 * (h + 1) / SWA_H))


def _swa_fwd(sq, sk, sv, sp, name, rides=()):
    t = sq.shape[0]
    w = SWA_BLOCK
    nb = t // w
    group = SWA_H // SWA_KV

    def body(q_ref, kp_ref, kc_ref, vp_ref, vc_ref, sp_ref, o_ref, lse_ref):
        first = pl.program_id(0) == 0
        lses = []
        for h in range(SWA_H):
            kv = h // group
            s_cur, s_prev = _swa_logits(q_ref[:, _hs(h)], kc_ref[:, _hs(kv)], kp_ref[:, _hs(kv)], _slope(h), first)
            sink = sp_ref[R_SINK:R_SINK + 1, h:h + 1]
            m = jnp.maximum(jnp.maximum(jnp.max(s_cur, axis=-1, keepdims=True),
                                        jnp.max(s_prev, axis=-1, keepdims=True)), sink)
            p_cur = jnp.exp(s_cur - m)
            p_prev = jnp.exp(s_prev - m)
            l = jnp.sum(p_cur, axis=-1, keepdims=True) + jnp.sum(p_prev, axis=-1, keepdims=True) + jnp.exp(sink - m)
            o_ref[:, _hs(h)] = (_dot(p_cur, vc_ref[:, _hs(kv)]) + _dot(p_prev, vp_ref[:, _hs(kv)])) / l
            lses.append(m + jnp.log(l))
        lse_ref[...] = _head_column(lses)

    qspec = pl.BlockSpec((w, SWA_W), lambda n: (n, 0))
    cur = pl.BlockSpec((w, SWA_KV_W), lambda n: (n, 0))
    prev = pl.BlockSpec((w, SWA_KV_W), lambda n: (jnp.maximum(n - 1, 0), 0))
    return _call(
        name, body, (nb,), [qspec, prev, cur, prev, cur, pl.BlockSpec((16, 128), lambda n: (0, 0))],
        [qspec, pl.BlockSpec((w, HEAD), lambda n: (n, 0))],
        [jax.ShapeDtypeStruct((t, SWA_W), F32), jax.ShapeDtypeStruct((t, HEAD), F32)],
        [sq, sk, sk, sv, sv, sp], rides=rides)


def _swa_bwd(sq, sk, sv, sp, dmixed, lse, delta, name):
    t = sq.shape[0]
    w = SWA_BLOCK
    nb = t // w
    group = SWA_H // SWA_KV
    do_block = FOX_W // SWA_W
    assert FOX_W % SWA_W == 0

    def body(q_ref, kp_ref, kc_ref, vp_ref, vc_ref, sp_ref, do_ref, lse_ref, dl_ref,
             dq_ref, dk_ref, dv_ref, dsp_ref, ck, cv):
        step = pl.program_id(0)
        first = step == nb - 1

        @pl.when(step == 0)
        def _():
            ck[...] = jnp.zeros_like(ck)
            cv[...] = jnp.zeros_like(cv)
            dsp_ref[...] = jnp.zeros_like(dsp_ref)

        dk_cur = [None] * SWA_KV
        dk_prev = [None] * SWA_KV
        dv_cur = [None] * SWA_KV
        dv_prev = [None] * SWA_KV
        dsinks = []

        def add(lst, i, v):
            lst[i] = v if lst[i] is None else lst[i] + v

        for h in range(SWA_H):
            kv = h // group
            q, do = q_ref[:, _hs(h)], do_ref[:, _hs(h)]
            kc, kp, vc, vp = kc_ref[:, _hs(kv)], kp_ref[:, _hs(kv)], vc_ref[:, _hs(kv)], vp_ref[:, _hs(kv)]
            s_cur, s_prev = _swa_logits(q, kc, kp, _slope(h), first)
            lse_h = lse_ref[:, h:h + 1]
            dl_h = dl_ref[:, FOX_H + h:FOX_H + h + 1]
            p_cur = jnp.exp(s_cur - lse_h)
            p_prev = jnp.exp(s_prev - lse_h)
            p_sink = jnp.exp(sp_ref[R_SINK:R_SINK + 1, h:h + 1] - lse_h)
            ds_cur = p_cur * (_dot(do, vc, "nt") - dl_h)
            ds_prev = p_prev * (_dot(do, vp, "nt") - dl_h)
            dq_ref[:, _hs(h)] = (_dot(ds_cur, kc) + _dot(ds_prev, kp)) * SCALE
            add(dk_cur, kv, _dot(ds_cur, q, "tn") * SCALE)
            add(dk_prev, kv, _dot(ds_prev, q, "tn") * SCALE)
            add(dv_cur, kv, _dot(p_cur, do, "tn"))
            add(dv_prev, kv, _dot(p_prev, do, "tn"))
            dsinks.append(-jnp.sum(p_sink * dl_h, axis=0, keepdims=True))
        for kv in range(SWA_KV):
            dk_ref[:, _hs(kv)] = dk_cur[kv] + ck[:, _hs(kv)]
            dv_ref[:, _hs(kv)] = dv_cur[kv] + cv[:, _hs(kv)]
            ck[:, _hs(kv)] = dk_prev[kv]
            cv[:, _hs(kv)] = dv_prev[kv]
        lane = lax.broadcasted_iota(jnp.int32, (1, HEAD), 1)
        row = jnp.zeros((1, HEAD), F32)
        for h in range(SWA_H):
            row = jnp.where(lane == h, dsinks[h], row)
        dsp_ref[R_SINK:R_SINK + 1, :] += row

    def rev(n):
        return nb - 1 - n

    qspec = pl.BlockSpec((w, SWA_W), lambda n: (rev(n), 0))
    cur = pl.BlockSpec((w, SWA_KV_W), lambda n: (rev(n), 0))
    prev = pl.BlockSpec((w, SWA_KV_W), lambda n: (jnp.maximum(rev(n) - 1, 0), 0))
    col = pl.BlockSpec((w, HEAD), lambda n: (rev(n), 0))
    small = pl.BlockSpec((16, 128), lambda n: (0, 0))
    return pl.pallas_call(
        body, grid=(nb,),
        in_specs=[qspec, prev, cur, prev, cur, small, pl.BlockSpec((w, SWA_W), lambda n: (rev(n), do_block)), col, col],
        out_specs=[qspec, cur, cur, small],
        out_shape=[jax.ShapeDtypeStruct((t, SWA_W), F32), jax.ShapeDtypeStruct((t, SWA_KV_W), F32),
                   jax.ShapeDtypeStruct((t, SWA_KV_W), F32), jax.ShapeDtypeStruct((16, 128), F32)],
        scratch_shapes=[pltpu.VMEM((w, SWA_KV_W), F32), pltpu.VMEM((w, SWA_KV_W), F32)],
        compiler_params=_params(1), name=name)(sq, sk, sk, sv, sv, sp, dmixed, lse, delta)


def _mem_pre(mkv, sp, name):
    m = mkv.shape[0]

    def body(x_ref, sp_ref, k_ref, v_ref):
        for h in range(MEM_H):
            k_ref[:, _hs(h)] = _head_norm(x_ref[:, _hs(h)], sp_ref[R_MK:R_MK + 1, :]).astype(BF16)
        v_ref[...] = x_ref[:, MEM_W:2 * MEM_W].astype(BF16)

    out = jax.ShapeDtypeStruct((m, MEM_W), BF16)
    return pl.pallas_call(body, out_shape=[out, out], name=name)(mkv, sp)


def _mem_post_bwd(mkv, sp, dmk, dmv, name):
    m = mkv.shape[0]

    def body(x_ref, sp_ref, dk_ref, dv_ref, d_ref, dsp_ref):
        dsp_ref[...] = jnp.zeros_like(dsp_ref)
        total = None
        for h in range(MEM_H):
            dx, dg = _head_norm_bwd(x_ref[:, _hs(h)], sp_ref[R_MK:R_MK + 1, :], dk_ref[:, _hs(h)])
            d_ref[:, _hs(h)] = dx.astype(BF16)
            total = dg if total is None else total + dg
        d_ref[:, MEM_W:2 * MEM_W] = dv_ref[...].astype(BF16)
        dsp_ref[R_MK:R_MK + 1, :] = total

    return pl.pallas_call(body, out_shape=[jax.ShapeDtypeStruct((m, 2 * MEM_W), BF16),
                                           jax.ShapeDtypeStruct((16, 128), F32)], name=name)(mkv, sp, dmk, dmv)


def _mem_fwd(mq, mk, mv, name):
    t = mq.shape[0]
    m = mk.shape[0]
    tq = min(t, 512)

    def body(q_ref, k_ref, v_ref, o_ref, lse_ref):
        lses = []
        for h in range(MEM_H):
            s = _dot(q_ref[:, _hs(h)], k_ref[:, _hs(h)], "nt") * SCALE
            mx = jnp.max(s, axis=-1, keepdims=True)
            p = jnp.exp(s - mx)
            l = jnp.sum(p, axis=-1, keepdims=True)
            o_ref[:, _hs(h)] = _dot(p, v_ref[:, _hs(h)]) / l
            lses.append(mx + jnp.log(l))
        lse_ref[...] = _head_column(lses)

    qspec = pl.BlockSpec((tq, MEM_W), lambda i: (i, 0))
    kspec = pl.BlockSpec((m, MEM_W), lambda i: (0, 0))
    return pl.pallas_call(
        body, grid=(t // tq,), in_specs=[qspec, kspec, kspec],
        out_specs=[qspec, pl.BlockSpec((tq, HEAD), lambda i: (i, 0))],
        out_shape=[jax.ShapeDtypeStruct((t, MEM_W), F32), jax.ShapeDtypeStruct((t, HEAD), F32)],
        compiler_params=_params(1), name=name)(mq, mk, mv)


def _mem_bwd(mq, mk, mv, dmixed, lse, delta, name):
    t = mq.shape[0]
    m = mk.shape[0]
    tq = min(t, 512)
    do_block = (FOX_W + SWA_W) // MEM_W
    assert (FOX_W + SWA_W) % MEM_W == 0

    def body(q_ref, k_ref, v_ref, do_ref, lse_ref, dl_ref, dq_ref, dk_ref, dv_ref):
        @pl.when(pl.program_id(0) == 0)
        def _():
            dk_ref[...] = jnp.zeros_like(dk_ref)
            dv_ref[...] = jnp.zeros_like(dv_ref)

        for h in range(MEM_H):
            q, k, v, do = q_ref[:, _hs(h)], k_ref[:, _hs(h)], v_ref[:, _hs(h)], do_ref[:, _hs(h)]
            s = _dot(q, k, "nt") * SCALE
            p = jnp.exp(s - lse_ref[:, h:h + 1])
            col = FOX_H + SWA_H + h
            ds = p * (_dot(do, v, "nt") - dl_ref[:, col:col + 1])
            dq_ref[:, _hs(h)] = _dot(ds, k) * SCALE
            dk_ref[:, _hs(h)] += _dot(ds, q, "tn") * SCALE
            dv_ref[:, _hs(h)] += _dot(p, do, "tn")

    qspec = pl.BlockSpec((tq, MEM_W), lambda i: (i, 0))
    kspec = pl.BlockSpec((m, MEM_W), lambda i: (0, 0))
    col = pl.BlockSpec((tq, HEAD), lambda i: (i, 0))
    return pl.pallas_call(
        body, grid=(t // tq,),
        in_specs=[qspec, kspec, kspec, pl.BlockSpec((tq, MEM_W), lambda i: (i, do_block)), col, col],
        out_specs=[qspec, kspec, kspec],
        out_shape=[jax.ShapeDtypeStruct((t, MEM_W), F32), jax.ShapeDtypeStruct((m, MEM_W), F32),
                   jax.ShapeDtypeStruct((m, MEM_W), F32)],
        compiler_params=_params(1), name=name)(mq, mk, mv, dmixed, lse, delta)


def _all_gather(xs, name):
    n = len(xs)

    def body(*refs):
        x_refs, o_refs = refs[:n], refs[n:2 * n]
        send_sems, recv_sems, local_sems = refs[2 * n:]
        x, y, c = _me()
        me, sibling = (x, y, c), (x, y, 1 - c)
        chips = [(1 - x, y), (x, 1 - y), (1 - x, 1 - y)]

        def copy(a, k, block, to, src=None):
            slot = o_refs[a].at[_lin(block)]
            return pltpu.make_async_remote_copy(
                src_ref=slot if src is None else src, dst_ref=slot, send_sem=send_sems.at[a, k],
                recv_sem=recv_sems.at[a, k], device_id=to, device_id_type=MESH)

        mine = [pltpu.make_async_copy(x_refs[a], o_refs[a].at[_lin(me)], local_sems.at[a]) for a in range(n)]
        for cp in mine:
            cp.start()
        first = []
        for a in range(n):
            first.append(copy(a, 0, me, sibling, src=x_refs[a]))
            first += [copy(a, 1 + j, me, (*chip, c), src=x_refs[a]) for j, chip in enumerate(chips)]
        for cp in first:
            cp.start()
        passed = []
        for j, chip in enumerate(chips):
            for a in range(n):
                copy(a, 1 + j, (*chip, c), me).wait_recv()
                cp = copy(a, 4 + j, (*chip, c), sibling)
                cp.start()
                passed.append(cp)
        for a in range(n):
            copy(a, 0, sibling, me).wait_recv()
            for j, chip in enumerate(chips):
                copy(a, 4 + j, (*chip, 1 - c), me).wait_recv()
        for cp in first + passed:
            cp.wait_send()
        for cp in mine:
            cp.wait()

    return pl.pallas_call(
        body, in_specs=[ANY] * n, out_specs=[ANY] * n,
        out_shape=[jax.ShapeDtypeStruct((N_DEV,) + x.shape, x.dtype) for x in xs],
        scratch_shapes=[pltpu.SemaphoreType.DMA((n, 7)), pltpu.SemaphoreType.DMA((n, 7)),
                        pltpu.SemaphoreType.DMA((n,))],
        name=name)(*xs)


def _peers():
    x, y, c = _me()
    out = []
    for k in range(1, N_DEV):
        kx, ky, kc = (k >> 2) & 1, (k >> 1) & 1, k & 1
        out.append(((1 - x) if kx else x, (1 - y) if ky else y, (1 - c) if kc else c))
    return out


def _all_reduce_small(xs, name):
    n = len(xs)

    def body(*refs):
        x_refs, o_refs = refs[:n], refs[n:2 * n]
        bufs = refs[2 * n:3 * n]
        send_sems, recv_sems = refs[3 * n:]
        me = _lin(_me())
        peers = _peers()
        for a in range(n):
            bufs[a][me] = x_refs[a][...]
        sends = []
        for a in range(n):
            for k, peer in enumerate(peers):
                sends.append(pltpu.make_async_remote_copy(
                    src_ref=bufs[a].at[me], dst_ref=bufs[a].at[me], send_sem=send_sems.at[a, k],
                    recv_sem=recv_sems.at[a, k], device_id=peer, device_id_type=MESH))
        for cp in sends:
            cp.start()
        for a in range(n):
            for k, peer in enumerate(peers):
                pltpu.make_async_remote_copy(
                    src_ref=bufs[a].at[me], dst_ref=bufs[a].at[_lin(peer)], send_sem=send_sems.at[a, k],
                    recv_sem=recv_sems.at[a, k], device_id=peer, device_id_type=MESH).wait_recv()
        for cp in sends:
            cp.wait_send()
        for a in range(n):
            total = bufs[a][0]
            for q in range(1, N_DEV):
                total = total + bufs[a][q]
            o_refs[a][...] = total

    vmem = pl.BlockSpec(memory_space=pltpu.VMEM)
    return pl.pallas_call(
        body, in_specs=[vmem] * n, out_specs=[vmem] * n,
        out_shape=[jax.ShapeDtypeStruct(x.shape, F32) for x in xs],
        scratch_shapes=[pltpu.VMEM((N_DEV,) + x.shape, F32) for x in xs]
        + [pltpu.SemaphoreType.DMA((n, 7)), pltpu.SemaphoreType.DMA((n, 7))],
        name=name)(*xs)


def _pair_add(part, got, name):
    _, rows, cols = part.shape
    tm = _rows_tile(rows, cols * 2, budget=2 << 20)
    core = jnp.reshape(lax.axis_index("c"), (1,)).astype(jnp.int32)

    def body(c_ref, p_ref, g_ref, o_ref):
        o_ref[...] = (p_ref[...].astype(F32) + g_ref[...].astype(F32)).astype(BF16)

    spec = pl.BlockSpec((None, tm, cols), lambda q, i, c: (q, i, 0))
    grid_spec = pltpu.PrefetchScalarGridSpec(
        num_scalar_prefetch=1, grid=(4, rows // tm),
        in_specs=[pl.BlockSpec((None, tm, cols), lambda q, i, c: (2 * q + c[0], i, 0)), spec], out_specs=spec)
    return pl.pallas_call(body, grid_spec=grid_spec, out_shape=jax.ShapeDtypeStruct((4, rows, cols), BF16),
                          compiler_params=_params(2), name=name)(core, part, got)


def _adam_math(w, g, m, v):
    nm = ADAM_B1 * m + (1.0 - ADAM_B1) * g
    nv = ADAM_B2 * v + (1.0 - ADAM_B2) * (g * g)
    m_hat = nm / (1.0 - ADAM_B1 ** ADAM_STEP)
    v_hat = nv / (1.0 - ADAM_B2 ** ADAM_STEP)
    return -ADAM_LR * (m_hat / (jnp.sqrt(v_hat) + ADAM_EPS) + ADAM_WD * w), nm, nv


def _sum_chips(got, name):
    _, rows, cols = got.shape
    tm = _rows_tile(rows, cols * 2 * 4, budget=2 << 20)

    def body(r_ref, o_ref):
        o_ref[...] = ((r_ref[0].astype(F32) + r_ref[1].astype(F32)) + r_ref[2].astype(F32)) + r_ref[3].astype(F32)

    return pl.pallas_call(
        body, grid=(rows // tm,), in_specs=[pl.BlockSpec((4, tm, cols), lambda i: (0, i, 0))],
        out_specs=pl.BlockSpec((tm, cols), lambda i: (i, 0)), out_shape=jax.ShapeDtypeStruct((rows, cols), F32),
        compiler_params=_params(1), name=name)(got)


def _sum_adamw(got, col_block, w, m, v, name):
    _, rows, cols = w.shape
    tm = _rows_tile(rows, cols * 4, budget=1 << 20)

    def body(r_ref, w_ref, m_ref, v_ref, g_ref, d_ref, nm_ref, nv_ref):
        g = ((r_ref[0].astype(F32) + r_ref[1].astype(F32)) + r_ref[2].astype(F32)) + r_ref[3].astype(F32)
        g_ref[...] = g
        d_ref[...], nm_ref[...], nv_ref[...] = _adam_math(w_ref[...], g, m_ref[...], v_ref[...])

    spec = pl.BlockSpec((None, tm, cols), lambda i: (0, i, 0))
    out = jax.ShapeDtypeStruct(w.shape, F32)
    return pl.pallas_call(
        body, grid=(rows // tm,), in_specs=[pl.BlockSpec((4, tm, cols), lambda i: (0, i, col_block)), spec, spec, spec],
        out_specs=[spec] * 4, out_shape=[out] * 4, compiler_params=_params(1), name=name)(got, w, m, v)


def _adamw(w, g, m, v, name):
    rows, cols = w.shape
    tm = _rows_tile(rows, cols * 4, budget=2 << 20, mult=8)

    def body(w_ref, g_ref, m_ref, v_ref, d_ref, nm_ref, nv_ref):
        d_ref[...], nm_ref[...], nv_ref[...] = _adam_math(w_ref[...], g_ref[...], m_ref[...], v_ref[...])

    spec = pl.BlockSpec((tm, cols), lambda i: (i, 0))
    out = jax.ShapeDtypeStruct(w.shape, F32)
    return pl.pallas_call(body, grid=(rows // tm,), in_specs=[spec] * 4, out_specs=[spec] * 3,
                          out_shape=[out] * 3, compiler_params=_params(1), name=name)(w, g, m, v)


def _permute_in(w):
    logit0 = 3 * FOX_W
    pad = jnp.zeros(w.shape[:-1] + (HEAD - N_LOGIT,), w.dtype)
    return jnp.concatenate([w[..., :logit0], w[..., logit0 + N_LOGIT:], w[..., logit0:logit0 + N_LOGIT], pad], axis=-1)


def _unpermute_in(w):
    logit0 = 3 * FOX_W
    return jnp.concatenate([w[..., :logit0], w[..., C_FL:C_FL + N_LOGIT], w[..., logit0:C_FL]], axis=-1)


def _pad_row(v, width):
    return jnp.pad(v, ((0, 0), (0, width - v.shape[1])))


def _pack_small(fq, fk, sq, sk, mq, mk, fb, sinks):
    rows = [fq, fk, sq, sk, mq, mk, _pad_row(fb, HEAD), _pad_row(sinks, HEAD)]
    return jnp.concatenate(rows + [jnp.zeros((8, HEAD), F32)], axis=0)


def _pack_norms(a, b, c, d):
    return jnp.concatenate([a, b, c, d, jnp.zeros((4, a.shape[1]), F32)], axis=0)


def kernel(x, mem, ffn1_norm, ffn1_gate, ffn1_up, ffn1_down, mix_norm, mem_norm, w_in, forget_bias, w_mem_k, w_mem_v, fox_q_gain, fox_k_gain, swa_q_gain, swa_k_gain, swa_sinks, mem_q_gain, mem_k_gain, w_out, ffn2_norm, ffn2_gate, ffn2_up, ffn2_down, loss_target, m_ffn1_norm, m_ffn1_gate, m_ffn1_up, m_ffn1_down, m_mix_norm, m_mem_norm, m_w_in, m_forget_bias, m_w_mem_k, m_w_mem_v, m_fox_q_gain, m_fox_k_gain, m_swa_q_gain, m_swa_k_gain, m_swa_sinks, m_mem_q_gain, m_mem_k_gain, m_w_out, m_ffn2_norm, m_ffn2_gate, m_ffn2_up, m_ffn2_down, v_ffn1_norm, v_ffn1_gate, v_ffn1_up, v_ffn1_down, v_mix_norm, v_mem_norm, v_w_in, v_forget_bias, v_w_mem_k, v_w_mem_v, v_fox_q_gain, v_fox_k_gain, v_swa_q_gain, v_swa_k_gain, v_swa_sinks, v_mem_q_gain, v_mem_k_gain, v_w_out, v_ffn2_norm, v_ffn2_gate, v_ffn2_up, v_ffn2_down):
    x0 = x[0]
    mem0 = mem[0]
    target = loss_target[0]
    t, d = x0.shape
    d_shard = w_in.shape[1]
    m_len = mem0.shape[0]
    tm = min(t, 512)
    tk = min(t, 512)
    tn = IN_W // 3
    tkw, tnw = min(t, 1024), IN_W // 3

    def swap(a):
        return jnp.swapaxes(a, 1, 2)

    gate1, up1, gate2, up2 = swap(ffn1_gate), swap(ffn1_up), swap(ffn2_gate), swap(ffn2_up)

    local = {
        "g1": gate1[0], "u1": up1[0], "d1": ffn1_down[0],
        "g2": gate2[0], "u2": up2[0], "d2": ffn2_down[0],
        "in": _permute_in(w_in[0]), "out": w_out[0],
        "mkv": jnp.concatenate([w_mem_k[0], w_mem_v[0]], axis=1),
    }
    shard = {k: _cast_bf16(v, f"cast_{k}") for k, v in local.items()}
    sp = _pack_small(fox_q_gain, fox_k_gain, swa_q_gain, swa_k_gain, mem_q_gain, mem_k_gain, forget_bias, swa_sinks)
    wt = {}

    wt["g1"], wt["u1"] = _all_gather([shard["g1"], shard["u1"]], "gather_ffn1_in")
    xn1 = _rms_fwd(x0, ffn1_norm, "ffn1_norm")
    (g1, u1, h1), ((wt["d1"], wt["in"]),) = _ffn_up(
        xn1, wt["g1"], wt["u1"], "ffn1", rides=[_ride_gather([shard["d1"], shard["in"]], 0.87)])
    x1, ((wt["out"], wt["mkv"]),) = _ffn_down(
        x0, h1, wt["d1"], "ffn1", rides=[_ride_gather([shard["out"], shard["mkv"]], 0.6)])
    w_in_full = wt["in"].reshape(d, IN_W)

    hn = _rms_fwd(x1, mix_norm, "mix_norm")
    proj, (half,) = _mm(
        "proj", [(hn, pl.BlockSpec((tm, d), lambda n, i, k: (i, 0)),
                  w_in_full, pl.BlockSpec((d, tn), lambda n, i, k: (0, n)))],
        "nn", (3, t // tm, 1), jax.ShapeDtypeStruct((t, IN_W), F32), pl.BlockSpec((tm, tn), lambda n, i, k: (i, n)),
        rides=[_ride_gather_chips([shard["g2"]])])
    w_out_full = wt["out"].reshape(d, d)
    w_mkv_full = wt["mkv"].reshape(d, 2 * MEM_W)
    fq, fk, fv, sq, sk, sv, mq, c_col = _attn_pre(proj, sp, "attn_pre")
    c_row = jnp.transpose(c_col[:, :8])
    c_rep = jnp.broadcast_to(c_row[:FOX_H, :, None], (FOX_H, t, HEAD))

    mn = _rms_fwd(mem0, mem_norm, "mem_norm")
    mkv = _mm("mem_kv", [(mn, pl.BlockSpec((m_len, d), lambda k: (0, 0)),
                          w_mkv_full, pl.BlockSpec((d, 2 * MEM_W), lambda k: (0, 0)))],
              "nn", (1,), jax.ShapeDtypeStruct((m_len, 2 * MEM_W), F32),
              pl.BlockSpec((m_len, 2 * MEM_W), lambda k: (0, 0)))
    mk, mv = _mem_pre(mkv, sp, "mem_pre")

    (o_a, lse_a), ((wt["g2"],), half) = _fox_fwd(
        fq, fk, fv, c_rep, c_row, "fox_fwd", rides=[_ride_gather_sibling(half), _ride_gather_chips([shard["u2"]])])
    (o_b, lse_b), ((wt["u2"],),) = _swa_fwd(sq, sk, sv, sp, "swa_fwd", rides=[_ride_gather_sibling(half)])
    o_c, lse_c = _mem_fwd(mq, mk, mv, "mem_fwd")

    def rows_spec(width):
        return pl.BlockSpec((tm, width), lambda i, k: (i, 0))

    def wout_rows(first, width):
        assert first % width == 0
        return pl.BlockSpec((width, d), lambda i, k: (first // width, 0))

    xspec = pl.BlockSpec((tm, d), lambda i, k: (i, 0))
    x2 = _mm(
        "mix_out",
        [(o_a, rows_spec(FOX_W), w_out_full, wout_rows(0, FOX_W)),
         (o_b, rows_spec(SWA_W), w_out_full, wout_rows(FOX_W, SWA_W)),
         (o_c, rows_spec(MEM_W), w_out_full, wout_rows(FOX_W + SWA_W, MEM_W))],
        "nn", (t // tm, 1), jax.ShapeDtypeStruct((t, d), F32), xspec, res=x1, res_spec=xspec)

    xn2 = _rms_fwd(x2, ffn2_norm, "ffn2_norm")
    (g2, u2, h2), ((wt["d2"],),) = _ffn_up(xn2, wt["g2"], wt["u2"], "ffn2", rides=[_ride_gather([shard["d2"]], 0.75)])
    x3 = _ffn_down(x2, h2, wt["d2"], "ffn2")

    dy, dyb, sq_err = _loss_head(x3, target, "loss_head")
    loss = lax.psum(0.5 * sq_err[0, 0] / d, ("x", "y", "c"))

    got = {}
    paired = {}
    landed = {}

    def pair(k, part):
        paired[k] = _pair_add(part, got[k], f"pair_{k}")

    (dg2, du2), _ = _ffn_dact(dyb, wt["d2"], g2, u2, "ffn2")
    part_d2 = _ffn_dw(h2, dyb, 0.5, "ffn2_dwd")
    part_g2, ((got["d2"],),) = _ffn_dw(dg2, xn2, 1.0, "ffn2_dwg", rides=[_ride_scatter_sibling([part_d2])])
    pair("d2", part_d2)
    part_u2, ((landed["d2"],), (got["g2"],)) = _ffn_dw(
        du2, xn2, 1.0, "ffn2_dwu", rides=[_ride_scatter_chips([paired["d2"]]), _ride_scatter_sibling([part_g2])])
    pair("g2", part_g2)
    dxn2, ((landed["g2"],), (got["u2"],)) = _ffn_dxn(
        dg2, du2, wt["g2"], wt["u2"], "ffn2_dxn",
        rides=[_ride_scatter_chips([paired["g2"]]), _ride_scatter_sibling([part_u2])])
    pair("u2", part_u2)
    dx2, dx2b, dgain_ffn2 = _rms_bwd(x2, ffn2_norm, dxn2, dy, "ffn2_norm_bwd")

    dmixed = _mm("mix_out_dx", [(dx2b, xspec, w_out_full, pl.BlockSpec((d, d), lambda i, k: (0, 0)))],
                 "nt", (t // tm, 1), jax.ShapeDtypeStruct((t, d), F32), xspec)

    def k_rows(width):
        return pl.BlockSpec((tk, width), lambda j, k: (k, 0))

    part_out = [
        _mm(f"mix_out_dw{i}", [(o, k_rows(width), dx2b, k_rows(d))], "tn", (1, t // tk),
            jax.ShapeDtypeStruct((width, d), BF16), pl.BlockSpec((width, d), lambda j, k: (0, 0)))
        for i, (o, width) in enumerate(((o_a, FOX_W), (o_b, SWA_W), (o_c, MEM_W)))
    ]
    part_out = jnp.concatenate(part_out, axis=0).reshape(N_DEV, d_shard, d)

    delta, delta_rep = _delta(dmixed, o_a, o_b, o_c, "attn_delta")
    (dfq, dfk, dfv, dc_col, dc_row), ((landed["u2"],),) = _fox_bwd(
        fq, fk, fv, c_rep, c_row, dmixed, lse_a, delta_rep, "fox_bwd", rides=[_ride_scatter_chips([paired["u2"]])])
    dsq, dsk, dsv, dsp_sink = _swa_bwd(sq, sk, sv, sp, dmixed, lse_b, delta, "swa_bwd")
    dmq, dmk, dmv = _mem_bwd(mq, mk, mv, dmixed, lse_c, delta, "mem_bwd")

    dmkv, dsp_mem = _mem_post_bwd(mkv, sp, dmk, dmv, "mem_post_bwd")
    part_mkv = _mm("mem_kv_dw", [(mn, pl.BlockSpec((m_len, d), lambda k: (0, 0)),
                                  dmkv, pl.BlockSpec((m_len, 2 * MEM_W), lambda k: (0, 0)))],
                   "tn", (1,), jax.ShapeDtypeStruct((d, 2 * MEM_W), BF16),
                   pl.BlockSpec((d, 2 * MEM_W), lambda k: (0, 0))).reshape(N_DEV, d_shard, 2 * MEM_W)
    dmn = _mm("mem_kv_dx", [(dmkv, pl.BlockSpec((m_len, 2 * MEM_W), lambda k: (0, 0)),
                             w_mkv_full, pl.BlockSpec((d, 2 * MEM_W), lambda k: (0, 0)))],
              "nt", (1,), jax.ShapeDtypeStruct((m_len, d), F32), pl.BlockSpec((m_len, d), lambda k: (0, 0)))
    _, _, dgain_mem = _rms_bwd(mem0, mem_norm, dmn, None, "mem_norm_bwd")

    dc_row_t = _pad_row(jnp.transpose(dc_row), HEAD)
    dproj, dsp_attn = _attn_post_bwd(proj, sp, dfq, dfk, dfv, dsq, dsk, dsv, dmq, dc_col, dc_row_t, "attn_post_bwd")
    dhn, ((got["out"], got["mkv"]),) = _mm(
        "proj_dx", [(dproj, pl.BlockSpec((tm, IN_W), lambda i, k: (i, 0)),
                     w_in_full, pl.BlockSpec((d, IN_W), lambda i, k: (0, 0)))],
        "nt", (t // tm, 1), jax.ShapeDtypeStruct((t, d), F32), xspec,
        rides=[_ride_scatter_sibling([part_out, part_mkv])])
    pair("out", part_out)
    pair("mkv", part_mkv)
    part_in, ((landed["out"], landed["mkv"]),) = _mm(
        "proj_dw", [(hn, pl.BlockSpec((tkw, d), lambda n, k: (k, 0)),
                     dproj, pl.BlockSpec((tkw, tnw), lambda n, k: (k, n)))],
        "tn", (IN_W // tnw, t // tkw), jax.ShapeDtypeStruct((d, IN_W), BF16), pl.BlockSpec((d, tnw), lambda n, k: (0, n)),
        rides=[_ride_scatter_chips([paired["out"], paired["mkv"]])])
    part_in = part_in.reshape(N_DEV, d_shard, IN_W)
    dx1, dx1b, dgain_mix = _rms_bwd(x1, mix_norm, dhn, dx2, "mix_norm_bwd")

    (dg1, du1), ((got["in"],),) = _ffn_dact(dx1b, wt["d1"], g1, u1, "ffn1", rides=[_ride_scatter_sibling([part_in])])
    pair("in", part_in)
    part_d1, ((landed["in"],),) = _ffn_dw(h1, dx1b, 0.5, "ffn1_dwd", rides=[_ride_scatter_chips([paired["in"]])])
    part_g1, ((got["d1"],),) = _ffn_dw(dg1, xn1, 1.0, "ffn1_dwg", rides=[_ride_scatter_sibling([part_d1])])
    pair("d1", part_d1)
    part_u1, ((landed["d1"],), (got["g1"],)) = _ffn_dw(
        du1, xn1, 1.0, "ffn1_dwu", rides=[_ride_scatter_chips([paired["d1"]]), _ride_scatter_sibling([part_g1])])
    pair("g1", part_g1)
    n_half = N_DEV // 2
    dxn1, ((landed["g1"],), (got["u1"],)) = _ffn_dxn(
        dg1, du1, wt["g1"], wt["u1"], "ffn1_dxn_a", shards=(0, n_half),
        rides=[_ride_scatter_chips([paired["g1"]]), _ride_scatter_sibling([part_u1])])
    pair("u1", part_u1)
    dxn1, ((landed["u1"],),) = _ffn_dxn(
        dg1, du1, wt["g1"], wt["u1"], "ffn1_dxn_b", shards=(n_half, N_DEV), res=dxn1,
        rides=[_ride_scatter_chips([paired["u1"]])])
    grad_x, _, dgain_ffn1 = _rms_bwd(x0, ffn1_norm, dxn1, dx1, "ffn1_norm_bwd")

    norms_sum, small_sum = _all_reduce_small(
        [_pack_norms(dgain_ffn1, dgain_mix, dgain_mem, dgain_ffn2), dsp_attn + dsp_sink + dsp_mem], "reduce_small")

    result = {
        "ffn1_gate": map(swap, _sum_adamw(landed["g1"], 0, gate1, swap(m_ffn1_gate), swap(v_ffn1_gate), "adamw_ffn1_gate")),
        "ffn1_up": map(swap, _sum_adamw(landed["u1"], 0, up1, swap(m_ffn1_up), swap(v_ffn1_up), "adamw_ffn1_up")),
        "ffn1_down": _sum_adamw(landed["d1"], 0, ffn1_down, m_ffn1_down, v_ffn1_down, "adamw_ffn1_down"),
        "w_mem_k": _sum_adamw(landed["mkv"], 0, w_mem_k, m_w_mem_k, v_w_mem_k, "adamw_w_mem_k"),
        "w_mem_v": _sum_adamw(landed["mkv"], 1, w_mem_v, m_w_mem_v, v_w_mem_v, "adamw_w_mem_v"),
        "w_out": _sum_adamw(landed["out"], 0, w_out, m_w_out, v_w_out, "adamw_w_out"),
        "ffn2_gate": map(swap, _sum_adamw(landed["g2"], 0, gate2, swap(m_ffn2_gate), swap(v_ffn2_gate), "adamw_ffn2_gate")),
        "ffn2_up": map(swap, _sum_adamw(landed["u2"], 0, up2, swap(m_ffn2_up), swap(v_ffn2_up), "adamw_ffn2_up")),
        "ffn2_down": _sum_adamw(landed["d2"], 0, ffn2_down, m_ffn2_down, v_ffn2_down, "adamw_ffn2_down"),
    }
    grad_in = _unpermute_in(_sum_chips(landed["in"], "sum_w_in"))
    result["w_in"] = (grad_in[None],) + tuple(
        o[None] for o in _adamw(w_in[0], grad_in, m_w_in[0], v_w_in[0], "adamw_w_in"))

    norm_names = ["ffn1_norm", "mix_norm", "mem_norm", "ffn2_norm"]
    norm_w = _pack_norms(ffn1_norm, mix_norm, mem_norm, ffn2_norm)
    norm_m = _pack_norms(m_ffn1_norm, m_mix_norm, m_mem_norm, m_ffn2_norm)
    norm_v = _pack_norms(v_ffn1_norm, v_mix_norm, v_mem_norm, v_ffn2_norm)
    outs = (norms_sum,) + tuple(_adamw(norm_w, norms_sum, norm_m, norm_v, "adamw_norms"))
    for i, k in enumerate(norm_names):
        result[k] = tuple(o[i:i + 1] for o in outs)

    small_names = ["fox_q_gain", "fox_k_gain", "swa_q_gain", "swa_k_gain", "mem_q_gain", "mem_k_gain",
                   "forget_bias", "swa_sinks"]
    small_m = _pack_small(m_fox_q_gain, m_fox_k_gain, m_swa_q_gain, m_swa_k_gain, m_mem_q_gain, m_mem_k_gain,
                          m_forget_bias, m_swa_sinks)
    small_v = _pack_small(v_fox_q_gain, v_fox_k_gain, v_swa_q_gain, v_swa_k_gain, v_mem_q_gain, v_mem_k_gain,
                          v_forget_bias, v_swa_sinks)
    outs = (small_sum,) + tuple(_adamw(sp, small_sum, small_m, small_v, "adamw_small"))
    for i, k in enumerate(small_names):
        width = N_LOGIT if k in ("forget_bias", "swa_sinks") else HEAD
        result[k] = tuple(o[i:i + 1, :width] for o in outs)

    order = ["ffn1_norm", "ffn1_gate", "ffn1_up", "ffn1_down", "mix_norm", "mem_norm", "w_in", "forget_bias",
             "w_mem_k", "w_mem_v", "fox_q_gain", "fox_k_gain", "swa_q_gain", "swa_k_gain", "swa_sinks",
             "mem_q_gain", "mem_k_gain", "w_out", "ffn2_norm", "ffn2_gate", "ffn2_up", "ffn2_down"]
    result = {k: tuple(v) for k, v in result.items()}
    flat = [loss, grad_x[None]]
    for kind in range(4):
        flat += [result[k][kind] for k in order]
    return tuple(flat)
```

```python
import functools

import jax
import jax.numpy as jnp
from jax import lax
from jax.experimental import pallas as pl
from jax.experimental.pallas import tpu as pltpu

F32 = jnp.float32
BF16 = jnp.bfloat16
MESH = pl.DeviceIdType.MESH
ANY = pl.BlockSpec(memory_space=pl.ANY)

N_DEV = 8
EPS = 1e-6
NEG_INF = -1e30
HEAD = 128
FOX_H, SWA_H, SWA_KV, MEM_H = 6, 6, 2, 4
FOX_W, SWA_W, SWA_KV_W, MEM_W = FOX_H * HEAD, SWA_H * HEAD, SWA_KV * HEAD, MEM_H * HEAD
SCALE = HEAD ** -0.5
SWA_BLOCK = 128
C_FQ, C_FK, C_FV = 0, FOX_W, 2 * FOX_W
C_SQ = 3 * FOX_W
C_SK = C_SQ + SWA_W
C_SV = C_SK + SWA_KV_W
C_MQ = C_SV + SWA_KV_W
C_FL = C_MQ + MEM_W
IN_W = C_FL + HEAD
N_LOGIT = FOX_H
R_FQ, R_FK, R_SQ, R_SK, R_MQ, R_MK, R_FB, R_SINK = range(8)
ADAM_LR, ADAM_B1, ADAM_B2, ADAM_EPS, ADAM_WD, ADAM_STEP = 0.001, 0.9, 0.999, 1e-08, 0.01, 10
VMEM_BYTES = 56 * 1024 * 1024

DN = {
    "nn": (((1,), (0,)), ((), ())),
    "nt": (((1,), (1,)), ((), ())),
    "tn": (((0,), (0,)), ((), ())),
}


def _params(n_axes):
    return pltpu.CompilerParams(dimension_semantics=("arbitrary",) * n_axes, vmem_limit_bytes=VMEM_BYTES)


def _dot(a, b, dims="nn"):
    return lax.dot_general(a.astype(BF16), b.astype(BF16), DN[dims], preferred_element_type=F32)


def _sigmoid(x):
    return 0.5 * jnp.tanh(0.5 * x) + 0.5


def _me():
    return lax.axis_index("x"), lax.axis_index("y"), lax.axis_index("c")


def _lin(p):
    return 4 * p[0] + 2 * p[1] + p[2]


def _rows_tile(rows, row_bytes, budget=4 << 20, mult=16):
    best = None
    for k in range(1, rows + 1):
        if rows % k == 0 and (rows // k) % mult == 0 and (rows // k) * row_bytes <= budget:
            best = rows // k
            break
    assert best is not None, (rows, row_bytes)
    return best


class _Ride:
    def __init__(self, inputs, out_shapes, aliases, n_remote, n_local, start, wait):
        self.inputs, self.out_shapes, self.aliases = list(inputs), list(out_shapes), dict(aliases)
        self.n_remote, self.n_local, self.start, self.wait = n_remote, n_local, start, wait


def _remote(src, dst, send, recv, k, to):
    return pltpu.make_async_remote_copy(src_ref=src, dst_ref=dst, send_sem=send.at[k], recv_sem=recv.at[k],
                                        device_id=to, device_id_type=MESH)


def _other_chips(x, y):
    return [(1 - x, y), (x, 1 - y), (1 - x, 1 - y)]


ALL_CHIPS = [(0, 0), (0, 1), (1, 0), (1, 1)]


def _ride_gather_chips(xs):
    n = len(xs)

    def copies(ins, outs, send, recv):
        x, y, c = _me()
        out = []
        for a in range(n):
            for j, chip in enumerate(_other_chips(x, y)):
                peer = (*chip, c)
                out.append((_remote(ins[a], outs[a].at[_lin((x, y, c))], send, recv, 3 * a + j, peer),
                            _remote(ins[a], outs[a].at[_lin(peer)], send, recv, 3 * a + j, peer)))
        return out

    def mine(ins, outs, local):
        me = _lin(_me())
        return [pltpu.make_async_copy(ins[a], outs[a].at[me], local.at[a]) for a in range(n)]

    def start(ins, outs, send, recv, local):
        for cp in mine(ins, outs, local):
            cp.start()
        for sent, _ in copies(ins, outs, send, recv):
            sent.start()

    def wait(ins, outs, send, recv, local):
        for sent, landed in copies(ins, outs, send, recv):
            landed.wait_recv()
            sent.wait_send()
        for cp in mine(ins, outs, local):
            cp.wait()

    shapes = [jax.ShapeDtypeStruct((N_DEV,) + x.shape, x.dtype) for x in xs]
    return _Ride(xs, shapes, {}, 3 * n, n, start, wait)


def _ride_gather_sibling(bufs):
    n = len(bufs)

    def copies(outs, send, recv):
        x, y, c = _me()
        out = []
        for a in range(n):
            for q, (px, py) in enumerate(ALL_CHIPS):
                there = outs[a].at[4 * px + 2 * py + c]
                here = outs[a].at[4 * px + 2 * py + 1 - c]
                out.append((_remote(there, there, send, recv, 4 * a + q, (x, y, 1 - c)),
                            _remote(here, here, send, recv, 4 * a + q, (x, y, 1 - c))))
        return out

    def start(ins, outs, send, recv, local):
        for sent, _ in copies(outs, send, recv):
            sent.start()

    def wait(ins, outs, send, recv, local):
        for sent, landed in copies(outs, send, recv):
            landed.wait_recv()
            sent.wait_send()

    shapes = [jax.ShapeDtypeStruct(b.shape, b.dtype) for b in bufs]
    return _Ride(bufs, shapes, {a: a for a in range(n)}, 4 * n, 0, start, wait)


def _ride_gather(xs, mid_frac):
    n = len(xs)
    chips = _ride_gather_chips(xs)

    def sibling_copies(outs, send, recv):
        x, y, c = _me()
        out = []
        for a in range(n):
            for q, (px, py) in enumerate(ALL_CHIPS):
                there = outs[a].at[4 * px + 2 * py + c]
                here = outs[a].at[4 * px + 2 * py + 1 - c]
                k = 3 * n + 4 * a + q
                out.append((_remote(there, there, send, recv, k, (x, y, 1 - c)),
                            _remote(here, here, send, recv, k, (x, y, 1 - c))))
        return out

    def mid(ins, outs, send, recv, local):
        chips.wait(ins, outs, send, recv, local)
        for sent, _ in sibling_copies(outs, send, recv):
            sent.start()

    def wait(ins, outs, send, recv, local):
        for sent, landed in sibling_copies(outs, send, recv):
            landed.wait_recv()
            sent.wait_send()

    ride = _Ride(xs, chips.out_shapes, {}, 7 * n, n, chips.start, wait)
    ride.mid, ride.mid_frac = mid, mid_frac
    return ride


def _ride_scatter_sibling(parts):
    n = len(parts)

    def copies(ins, outs, send, recv):
        x, y, c = _me()
        out = []
        for a in range(n):
            for q, (px, py) in enumerate(ALL_CHIPS):
                cp = _remote(ins[a].at[4 * px + 2 * py + 1 - c], outs[a].at[q], send, recv, 4 * a + q, (x, y, 1 - c))
                out.append(cp)
        return out

    def start(ins, outs, send, recv, local):
        for cp in copies(ins, outs, send, recv):
            cp.start()

    def wait(ins, outs, send, recv, local):
        for cp in copies(ins, outs, send, recv):
            cp.wait_recv()
            cp.wait_send()

    shapes = [jax.ShapeDtypeStruct((4,) + p.shape[1:], p.dtype) for p in parts]
    return _Ride(parts, shapes, {}, 4 * n, 0, start, wait)


def _ride_scatter_chips(pairs):
    n = len(pairs)

    def copies(ins, outs, send, recv):
        x, y, c = _me()
        out = []
        for a in range(n):
            for j, (px, py) in enumerate(_other_chips(x, y)):
                peer = (px, py, c)
                out.append((_remote(ins[a].at[2 * px + py], outs[a].at[2 * x + y], send, recv, 3 * a + j, peer),
                            _remote(ins[a].at[2 * px + py], outs[a].at[2 * px + py], send, recv, 3 * a + j, peer)))
        return out

    def mine(ins, outs, local):
        x, y, _ = _me()
        return [pltpu.make_async_copy(ins[a].at[2 * x + y], outs[a].at[2 * x + y], local.at[a]) for a in range(n)]

    def start(ins, outs, send, recv, local):
        for cp in mine(ins, outs, local):
            cp.start()
        for sent, _ in copies(ins, outs, send, recv):
            sent.start()

    def wait(ins, outs, send, recv, local):
        for sent, landed in copies(ins, outs, send, recv):
            landed.wait_recv()
            sent.wait_send()
        for cp in mine(ins, outs, local):
            cp.wait()

    shapes = [jax.ShapeDtypeStruct(p.shape, p.dtype) for p in pairs]
    return _Ride(pairs, shapes, {}, 3 * n, n, start, wait)


def _call(name, body, grid, in_specs, out_specs, out_shape, operands, scratch=(), rides=()):
    n_in, n_out, n_scr = len(operands), len(out_shape), len(scratch)
    ride_in, ride_out, ride_scr, aliases, spans = [], [], [], {}, []
    for r in rides:
        for i, o in r.aliases.items():
            aliases[n_in + len(ride_in) + i] = n_out + len(ride_out) + o
        spans.append((len(ride_in), len(r.inputs), len(ride_out), len(r.out_shapes)))
        ride_in += r.inputs
        ride_out += r.out_shapes
        ride_scr += [pltpu.SemaphoreType.DMA((r.n_remote,)), pltpu.SemaphoreType.DMA((r.n_remote,)),
                     pltpu.SemaphoreType.DMA((max(r.n_local, 1),))]

    def wrapped(*refs):
        c_in, r_in = refs[:n_in], refs[n_in:n_in + len(ride_in)]
        p = n_in + len(ride_in)
        c_out, r_out = refs[p:p + n_out], refs[p + n_out:p + n_out + len(ride_out)]
        p += n_out + len(ride_out)
        c_scr, r_scr = refs[p:p + n_scr], refs[p + n_scr:]

        n_steps = functools.reduce(lambda a, b: a * b, grid, 1)
        step = functools.reduce(lambda acc, ax: acc * grid[ax] + pl.program_id(ax), range(len(grid)), 0)

        def each(method, at):
            for k, (r, (i0, ni, o0, no)) in enumerate(zip(rides, spans)):
                fn = getattr(r, method, None)
                if fn is None:
                    continue
                run = functools.partial(fn, r_in[i0:i0 + ni], r_out[o0:o0 + no], *r_scr[3 * k:3 * k + 3])
                if grid:
                    pl.when(step == at(r))(run)
                else:
                    run()

        each("start", lambda r: 0)
        each("mid", lambda r: int(r.mid_frac * (n_steps - 1)))
        body(*c_in, *c_out, *c_scr)
        each("wait", lambda r: n_steps - 1)

    outs = pl.pallas_call(
        wrapped, grid=grid, in_specs=list(in_specs) + [ANY] * len(ride_in),
        out_specs=list(out_specs) + [ANY] * len(ride_out), out_shape=list(out_shape) + ride_out,
        scratch_shapes=list(scratch) + ride_scr, input_output_aliases=aliases,
        compiler_params=_params(len(grid)), name=name)(*operands, *ride_in)
    outs = list(outs)
    ride_results = [outs[n_out + o0:n_out + o0 + no] for (_, _, o0, no) in spans]
    return outs[:n_out], ride_results


def _only_copies(name, rides):
    return _call(name, lambda: None, (), [], [], [], [], rides=rides)[1]


def _mm(name, pairs, dims, grid, out_shape, out_spec, res=None, res_spec=None, alpha=1.0, rides=()):
    n = len(pairs)
    nk = grid[-1]
    kax = len(grid) - 1
    acc_shape = tuple(d for d in out_spec.block_shape if d is not None)

    def body(*refs):
        pos = 2 * n
        r_ref = None
        if res is not None:
            r_ref = refs[pos]
            pos += 1
        o_ref = refs[pos]
        part = None
        for p in range(n):
            d = _dot(refs[2 * p][...], refs[2 * p + 1][...], dims)
            part = d if part is None else part + d

        def finish(acc):
            if alpha != 1.0:
                acc = acc * alpha
            if r_ref is not None:
                acc = r_ref[...] + acc
            o_ref[...] = acc.astype(o_ref.dtype)

        if nk == 1:
            finish(part)
        else:
            acc_ref = refs[pos + 1]
            k = pl.program_id(kax)

            @pl.when(k == 0)
            def _():
                acc_ref[...] = part

            @pl.when(k > 0)
            def _():
                acc_ref[...] += part

            @pl.when(k == nk - 1)
            def _():
                finish(acc_ref[...])

    operands, in_specs = [], []
    for a, a_spec, b, b_spec in pairs:
        operands += [a, b]
        in_specs += [a_spec, b_spec]
    if res is not None:
        operands.append(res)
        in_specs.append(res_spec)
    (out,), ride_results = _call(name, body, grid, in_specs, [out_spec], [out_shape], operands,
                                 scratch=[pltpu.VMEM(acc_shape, F32)] if nk > 1 else [], rides=rides)
    return (out, ride_results) if rides else out


def _cast_bf16(x, name):
    rows, cols = x.shape
    tm = _rows_tile(rows, cols * 4)

    def body(x_ref, o_ref):
        o_ref[...] = x_ref[...].astype(BF16)

    spec = pl.BlockSpec((tm, cols), lambda i: (i, 0))
    return pl.pallas_call(body, grid=(rows // tm,), in_specs=[spec], out_specs=spec,
                          out_shape=jax.ShapeDtypeStruct(x.shape, BF16), compiler_params=_params(1), name=name)(x)


def _rms_fwd(x, gain, name):
    rows, d = x.shape
    tm = min(rows, 512)

    def body(x_ref, g_ref, o_ref):
        xv = x_ref[...]
        r = lax.rsqrt(jnp.mean(xv * xv, axis=-1, keepdims=True) + EPS)
        o_ref[...] = (xv * r * g_ref[...]).astype(BF16)

    spec = pl.BlockSpec((tm, d), lambda i: (i, 0))
    return pl.pallas_call(body, grid=(rows // tm,), in_specs=[spec, pl.BlockSpec((1, d), lambda i: (0, 0))],
                          out_specs=spec, out_shape=jax.ShapeDtypeStruct(x.shape, BF16),
                          compiler_params=_params(1), name=name)(x, gain)


def _rms_bwd(x, gain, dxn, dres, name):
    rows, d = x.shape
    tm = min(rows, 256)
    with_res = dres is not None

    def body(*refs):
        if with_res:
            x_ref, g_ref, dy_ref, r_ref, dx_ref, dxb_ref, dg_ref = refs
        else:
            x_ref, g_ref, dy_ref, dx_ref, dxb_ref, dg_ref = refs
        xv = x_ref[...]
        r = lax.rsqrt(jnp.mean(xv * xv, axis=-1, keepdims=True) + EPS)
        xh = xv * r
        dy = dy_ref[...]
        dxh = dy * g_ref[...]
        dx = r * (dxh - xh * jnp.mean(dxh * xh, axis=-1, keepdims=True))
        if with_res:
            dx = dx + r_ref[...]
        dx_ref[...] = dx
        dxb_ref[...] = dx.astype(BF16)
        part = jnp.sum(dy * xh, axis=0, keepdims=True)

        @pl.when(pl.program_id(0) == 0)
        def _():
            dg_ref[...] = part

        @pl.when(pl.program_id(0) > 0)
        def _():
            dg_ref[...] += part

    spec = pl.BlockSpec((tm, d), lambda i: (i, 0))
    vec = pl.BlockSpec((1, d), lambda i: (0, 0))
    ops = [x, gain, dxn] + ([dres] if with_res else [])
    return pl.pallas_call(
        body, grid=(rows // tm,), in_specs=[spec, vec, spec] + ([spec] if with_res else []),
        out_specs=[spec, spec, vec],
        out_shape=[jax.ShapeDtypeStruct(x.shape, F32), jax.ShapeDtypeStruct(x.shape, BF16),
                   jax.ShapeDtypeStruct((1, d), F32)],
        compiler_params=_params(1), name=name)(*ops)


def _loss_head(y, target, name):
    rows, d = y.shape
    tm = min(rows, 256)

    def body(y_ref, t_ref, dy_ref, dyb_ref, acc_ref):
        err = y_ref[...] - t_ref[...]
        dy = err * (1.0 / d)
        dy_ref[...] = dy
        dyb_ref[...] = dy.astype(BF16)
        part = jnp.zeros((8, 128), F32) + jnp.sum(err * err)

        @pl.when(pl.program_id(0) == 0)
        def _():
            acc_ref[...] = part

        @pl.when(pl.program_id(0) > 0)
        def _():
            acc_ref[...] += part

    spec = pl.BlockSpec((tm, d), lambda i: (i, 0))
    return pl.pallas_call(
        body, grid=(rows // tm,), in_specs=[spec, spec],
        out_specs=[spec, spec, pl.BlockSpec((8, 128), lambda i: (0, 0))],
        out_shape=[jax.ShapeDtypeStruct(y.shape, F32), jax.ShapeDtypeStruct(y.shape, BF16),
                   jax.ShapeDtypeStruct((8, 128), F32)],
        compiler_params=_params(1), name=name)(y, target)


ROW_CHUNK = 256


def _ffn_up(xn, wg, wu, tag, rides=()):
    t, d = xn.shape
    nd, fs, _ = wg.shape
    tm = min(t, 512)
    rc = min(tm, ROW_CHUNK)

    def body(x_ref, wg_ref, wu_ref, g_ref, u_ref, h_ref):
        for r in range(0, tm, rc):
            xv = x_ref[r:r + rc, :]
            g = _dot(xv, wg_ref[...], "nt")
            u = _dot(xv, wu_ref[...], "nt")
            g_ref[r:r + rc, :] = g.astype(BF16)
            u_ref[r:r + rc, :] = u.astype(BF16)
            h_ref[r:r + rc, :] = (g * _sigmoid(g) * u).astype(BF16)

    wspec = pl.BlockSpec((None, fs, d), lambda j, i: (j, 0, 0))
    hspec = pl.BlockSpec((None, tm, fs), lambda j, i: (j, i, 0))
    hid = jax.ShapeDtypeStruct((nd, t, fs), BF16)
    return _call(f"{tag}_up", body, (nd, t // tm), [pl.BlockSpec((tm, d), lambda j, i: (i, 0)), wspec, wspec],
                 [hspec] * 3, [hid] * 3, [xn, wg, wu], rides=rides)


def _ffn_contract(hid, w, name, res=None, alpha=1.0, rides=()):
    nd, t, fs = hid.shape
    d = w.shape[2]
    tm = min(t, 256)
    xspec = pl.BlockSpec((tm, d), lambda i, k: (i, 0))
    pairs = [(hid, pl.BlockSpec((None, tm, fs), lambda i, k, s=s: (s, i, 0)),
              w, pl.BlockSpec((None, fs, d), lambda i, k, s=s: (s, 0, 0), pipeline_mode=pl.Buffered(1)))
             for s in range(nd)]
    return _mm(name, pairs, "nn", (t // tm, 1), jax.ShapeDtypeStruct((t, d), F32), xspec,
               res=res, res_spec=xspec if res is not None else None, alpha=alpha, rides=rides)


def _ffn_down(x, h, wd, tag, rides=()):
    return _ffn_contract(h, wd, f"{tag}_down", res=x, alpha=0.5, rides=rides)


def _ffn_dact(dyb, wd, g, u, tag, rides=()):
    nd, t, fs = g.shape
    d = dyb.shape[1]
    tm = min(t, 512)
    rc = min(tm, ROW_CHUNK)

    def body(dy_ref, wd_ref, g_ref, u_ref, dg_ref, du_ref):
        for r in range(0, tm, rc):
            dh = 0.5 * _dot(dy_ref[r:r + rc, :], wd_ref[...], "nt")
            gv = g_ref[r:r + rc, :].astype(F32)
            uv = u_ref[r:r + rc, :].astype(F32)
            sig = _sigmoid(gv)
            du_ref[r:r + rc, :] = (dh * gv * sig).astype(BF16)
            dg_ref[r:r + rc, :] = (dh * uv * sig * (1.0 + gv * (1.0 - sig))).astype(BF16)

    hspec = pl.BlockSpec((None, tm, fs), lambda j, i: (j, i, 0))
    hid = jax.ShapeDtypeStruct((nd, t, fs), BF16)
    return _call(f"{tag}_dact", body, (nd, t // tm),
                 [pl.BlockSpec((tm, d), lambda j, i: (i, 0)), pl.BlockSpec((None, fs, d), lambda j, i: (j, 0, 0)),
                  hspec, hspec], [hspec] * 2, [hid] * 2, [dyb, wd, g, u], rides=rides)


def _ffn_dw(hid, act, alpha, name, rides=()):
    nd, t, fs = hid.shape
    d = act.shape[1]
    tk = t
    return _mm(name, [(hid, pl.BlockSpec((None, tk, fs), lambda j, k: (j, k, 0)),
                       act, pl.BlockSpec((tk, d), lambda j, k: (k, 0), pipeline_mode=pl.Buffered(1)))],
               "tn", (nd, t // tk),
               jax.ShapeDtypeStruct((nd, fs, d), BF16), pl.BlockSpec((None, fs, d), lambda j, k: (j, 0, 0)),
               alpha=alpha, rides=rides)


def _head_norm(x, gain):
    r = lax.rsqrt(jnp.mean(x * x, axis=-1, keepdims=True) + EPS)
    return x * r * gain


def _head_norm_bwd(x, gain, dy):
    r = lax.rsqrt(jnp.mean(x * x, axis=-1, keepdims=True) + EPS)
    xh = x * r
    dxh = dy * gain
    dx = r * (dxh - xh * jnp.mean(dxh * xh, axis=-1, keepdims=True))
    return dx, jnp.sum(dy * xh, axis=0, keepdims=True)


def _hs(h, base=0):
    return slice(base + h * HEAD, base + (h + 1) * HEAD)


def _tri(n, lower):
    r = lax.broadcasted_iota(jnp.int32, (n, n), 0)
    c = lax.broadcasted_iota(jnp.int32, (n, n), 1)
    return ((r >= c) if lower else (r <= c)).astype(F32)


def _attn_pre(proj, sp, name):
    t = proj.shape[0]
    tm = min(t, 256)

    def body(p_ref, sp_ref, fq, fk, fv, sq, sk, sv, mq, cc, carry):
        @pl.when(pl.program_id(0) == 0)
        def _():
            carry[...] = jnp.zeros_like(carry)

        for h in range(FOX_H):
            fq[:, _hs(h)] = _head_norm(p_ref[:, _hs(h, C_FQ)], sp_ref[R_FQ:R_FQ + 1, :]).astype(BF16)
            fk[:, _hs(h)] = _head_norm(p_ref[:, _hs(h, C_FK)], sp_ref[R_FK:R_FK + 1, :]).astype(BF16)
        fv[...] = p_ref[:, C_FV:C_FV + FOX_W].astype(BF16)
        for h in range(SWA_H):
            sq[:, _hs(h)] = _head_norm(p_ref[:, _hs(h, C_SQ)], sp_ref[R_SQ:R_SQ + 1, :]).astype(BF16)
        for h in range(SWA_KV):
            sk[:, _hs(h)] = _head_norm(p_ref[:, _hs(h, C_SK)], sp_ref[R_SK:R_SK + 1, :]).astype(BF16)
        sv[...] = p_ref[:, C_SV:C_SV + SWA_KV_W].astype(BF16)
        for h in range(MEM_H):
            mq[:, _hs(h)] = _head_norm(p_ref[:, _hs(h, C_MQ)], sp_ref[R_MQ:R_MQ + 1, :]).astype(BF16)
        z = p_ref[:, C_FL:C_FL + HEAD] + sp_ref[R_FB:R_FB + 1, :]
        lane = lax.broadcasted_iota(jnp.int32, z.shape, 1)
        log_f = jnp.minimum(z, 0.0) - jnp.log(1.0 + jnp.exp(-jnp.abs(z)))
        log_f = jnp.where(lane < N_LOGIT, log_f, 0.0)
        c = jnp.dot(_tri(tm, True), log_f, precision=lax.Precision.HIGHEST, preferred_element_type=F32)
        c = c + carry[0:1, :]
        cc[...] = c
        carry[...] = jnp.broadcast_to(c[tm - 1:tm, :], carry.shape)

    def rows(w):
        return pl.BlockSpec((tm, w), lambda i: (i, 0))

    def shape(w, dt):
        return jax.ShapeDtypeStruct((t, w), dt)

    widths = [FOX_W, FOX_W, FOX_W, SWA_W, SWA_KV_W, SWA_KV_W, MEM_W]
    return pl.pallas_call(
        body, grid=(t // tm,), in_specs=[rows(IN_W), pl.BlockSpec((16, 128), lambda i: (0, 0))],
        out_specs=[rows(w) for w in widths] + [rows(HEAD)],
        out_shape=[shape(w, BF16) for w in widths] + [shape(HEAD, F32)],
        scratch_shapes=[pltpu.VMEM((8, 128), F32)], compiler_params=_params(1), name=name)(proj, sp)


def _attn_post_bwd(proj, sp, dfq, dfk, dfv, dsq, dsk, dsv, dmq, dc_col, dc_row_t, name):
    t = proj.shape[0]
    tm = min(t, 256)
    nb = t // tm

    def body(p_ref, sp_ref, dfq_r, dfk_r, dfv_r, dsq_r, dsk_r, dsv_r, dmq_r, dcc_r, dcr_r, dp_ref, dsp_ref, carry):
        @pl.when(pl.program_id(0) == 0)
        def _():
            carry[...] = jnp.zeros_like(carry)
            dsp_ref[...] = jnp.zeros_like(dsp_ref)

        def group(n_heads, col, row, d_ref):
            total = None
            for h in range(n_heads):
                dx, dg = _head_norm_bwd(p_ref[:, _hs(h, col)], sp_ref[row:row + 1, :], d_ref[:, _hs(h)])
                dp_ref[:, _hs(h, col)] = dx.astype(BF16)
                total = dg if total is None else total + dg
            dsp_ref[row:row + 1, :] += total

        group(FOX_H, C_FQ, R_FQ, dfq_r)
        group(FOX_H, C_FK, R_FK, dfk_r)
        dp_ref[:, C_FV:C_FV + FOX_W] = dfv_r[...].astype(BF16)
        group(SWA_H, C_SQ, R_SQ, dsq_r)
        group(SWA_KV, C_SK, R_SK, dsk_r)
        dp_ref[:, C_SV:C_SV + SWA_KV_W] = dsv_r[...].astype(BF16)
        group(MEM_H, C_MQ, R_MQ, dmq_r)
        dc = dcc_r[...] - dcr_r[...]
        rc = jnp.dot(_tri(tm, False), dc, precision=lax.Precision.HIGHEST, preferred_element_type=F32)
        rc = rc + carry[0:1, :]
        carry[...] = jnp.broadcast_to(rc[0:1, :], carry.shape)
        z = p_ref[:, C_FL:C_FL + HEAD] + sp_ref[R_FB:R_FB + 1, :]
        dz = rc * _sigmoid(-z)
        dp_ref[:, C_FL:C_FL + HEAD] = dz.astype(BF16)
        dsp_ref[R_FB:R_FB + 1, :] += jnp.sum(dz, axis=0, keepdims=True)

    def rows(w):
        return pl.BlockSpec((tm, w), lambda i: (nb - 1 - i, 0))

    small = pl.BlockSpec((16, 128), lambda i: (0, 0))
    widths = [FOX_W, FOX_W, FOX_W, SWA_W, SWA_KV_W, SWA_KV_W, MEM_W, HEAD, HEAD]
    return pl.pallas_call(
        body, grid=(nb,), in_specs=[rows(IN_W), small] + [rows(w) for w in widths],
        out_specs=[rows(IN_W), small],
        out_shape=[jax.ShapeDtypeStruct((t, IN_W), BF16), jax.ShapeDtypeStruct((16, 128), F32)],
        scratch_shapes=[pltpu.VMEM((8, 128), F32)], compiler_params=_params(1), name=name,
    )(proj, sp, dfq, dfk, dfv, dsq, dsk, dsv, dmq, dc_col, dc_row_t)


def _head_column(values):
    rows = values[0].shape[0]
    lane = lax.broadcasted_iota(jnp.int32, (rows, HEAD), 1)
    out = jnp.zeros((rows, HEAD), F32)
    for h, v in enumerate(values):
        out = jnp.where(lane == h, v, out)
    return out


def _head_row(values, n_rows=8):
    cols = values[0].shape[1]
    sub = lax.broadcasted_iota(jnp.int32, (n_rows, cols), 0)
    out = jnp.zeros((n_rows, cols), F32)
    for h, v in enumerate(values):
        out = jnp.where(sub == h, v, out)
    return out


def _delta(dmixed, o_a, o_b, o_c, name):
    t = dmixed.shape[0]
    tm = min(t, 512)

    def body(d_ref, a_ref, b_ref, c_ref, o_ref, rep_ref):
        cols = []
        for ref, n_heads, base in ((a_ref, FOX_H, 0), (b_ref, SWA_H, FOX_W), (c_ref, MEM_H, FOX_W + SWA_W)):
            for h in range(n_heads):
                cols.append(jnp.sum(d_ref[:, _hs(h, base)] * ref[:, _hs(h)], axis=-1, keepdims=True))
        o_ref[...] = _head_column(cols)
        for h in range(FOX_H):
            rep_ref[h] = jnp.broadcast_to(cols[h], (tm, HEAD))

    def rows(w):
        return pl.BlockSpec((tm, w), lambda i: (i, 0))

    return pl.pallas_call(body, grid=(t // tm,), in_specs=[rows(dmixed.shape[1]), rows(FOX_W), rows(SWA_W), rows(MEM_W)],
                          out_specs=[rows(HEAD), pl.BlockSpec((FOX_H, tm, HEAD), lambda i: (0, i, 0))],
                          out_shape=[jax.ShapeDtypeStruct((t, HEAD), F32), jax.ShapeDtypeStruct((FOX_H, t, HEAD), F32)],
                          compiler_params=_params(1), name=name)(dmixed, o_a, o_b, o_c)


def _fox_fwd(fq, fk, fv, c_rep, c_row, name, rides=()):
    t = fq.shape[0]
    tb = min(t, 512)
    nb = t // tb
    n_tiles = tb // HEAD

    def body(q_ref, k_ref, v_ref, cc_ref, cr_ref, o_ref, lse_ref, m_s, l_s, acc_s):
        qi, ki = pl.program_id(0), pl.program_id(1)

        @pl.when(ki == 0)
        def _():
            m_s[...] = jnp.full_like(m_s, NEG_INF)
            l_s[...] = jnp.zeros_like(l_s)
            acc_s[...] = jnp.zeros_like(acc_s)

        def step(diagonal):
            if diagonal:
                r = lax.broadcasted_iota(jnp.int32, (tb, HEAD), 0)
                c = lax.broadcasted_iota(jnp.int32, (tb, HEAD), 1)
            for h in range(FOX_H):
                s = _dot(q_ref[:, _hs(h)], k_ref[:, _hs(h)], "nt")
                cc = cc_ref[h]
                tiles, m_cur = [], None
                for j in range(n_tiles):
                    st = s[:, _hs(j)] * SCALE + cc - cr_ref[h:h + 1, _hs(j)]
                    if diagonal:
                        st = jnp.where(r >= c + j * HEAD, st, NEG_INF)
                    tiles.append(st)
                    m_cur = st if m_cur is None else jnp.maximum(m_cur, st)
                m_prev = m_s[h]
                m_new = jnp.maximum(m_prev, jnp.max(m_cur, axis=-1, keepdims=True))
                alpha = jnp.exp(m_prev - m_new)
                ps = [jnp.exp(st - m_new) for st in tiles]
                l_cur = ps[0]
                for p in ps[1:]:
                    l_cur = l_cur + p
                l_s[h] = alpha * l_s[h] + jnp.sum(l_cur, axis=-1, keepdims=True)
                p = jnp.concatenate([p.astype(BF16) for p in ps], axis=1)
                acc_s[:, _hs(h)] = alpha * acc_s[:, _hs(h)] + _dot(p, v_ref[:, _hs(h)])
                m_s[h] = m_new

        @pl.when(ki < qi)
        def _():
            step(False)

        @pl.when(ki == qi)
        def _():
            step(True)
            for h in range(FOX_H):
                o_ref[:, _hs(h)] = acc_s[:, _hs(h)] / l_s[h]
                lse_ref[h] = m_s[h] + jnp.log(l_s[h])

    qspec = pl.BlockSpec((tb, FOX_W), lambda i, j: (i, 0))
    kspec = pl.BlockSpec((tb, FOX_W), lambda i, j: (jnp.minimum(i, j), 0))
    rep = pl.BlockSpec((FOX_H, tb, HEAD), lambda i, j: (0, i, 0))
    return _call(
        name, body, (nb, nb),
        [qspec, kspec, kspec, rep, pl.BlockSpec((8, tb), lambda i, j: (0, jnp.minimum(i, j)))],
        [qspec, rep],
        [jax.ShapeDtypeStruct((t, FOX_W), F32), jax.ShapeDtypeStruct((FOX_H, t, HEAD), F32)],
        [fq, fk, fv, c_rep, c_row],
        scratch=[pltpu.VMEM((FOX_H, tb, HEAD), F32), pltpu.VMEM((FOX_H, tb, HEAD), F32), pltpu.VMEM((tb, FOX_W), F32)],
        rides=rides)


def _fox_bwd(fq, fk, fv, c_rep, c_row, dmixed, lse, delta, name, rides=()):
    t = fq.shape[0]
    tb = min(t, 512)
    nb = t // tb
    n_tiles = tb // HEAD

    def body(q_ref, k_ref, v_ref, cc_ref, cr_ref, do_ref, lse_ref, dl_ref,
             dq_ref, dk_ref, dv_ref, dcc_ref, dcr_ref):
        ki, qi = pl.program_id(0), pl.program_id(1)

        @pl.when((ki == 0) & (qi == 0))
        def _():
            dq_ref[...] = jnp.zeros_like(dq_ref)
            dcc_ref[...] = jnp.zeros_like(dcc_ref)

        @pl.when(qi == 0)
        def _():
            dk_ref[...] = jnp.zeros_like(dk_ref)
            dv_ref[...] = jnp.zeros_like(dv_ref)
            dcr_ref[...] = jnp.zeros_like(dcr_ref)

        def step(diagonal):
            rows = pl.ds(pl.multiple_of(qi * tb, tb), tb)
            if diagonal:
                r = lax.broadcasted_iota(jnp.int32, (tb, HEAD), 0)
                c = lax.broadcasted_iota(jnp.int32, (tb, HEAD), 1)
            row_sums, col_sums = [], []
            for h in range(FOX_H):
                q, k, v, do = q_ref[:, _hs(h)], k_ref[:, _hs(h)], v_ref[:, _hs(h)], do_ref[:, _hs(h)]
                s = _dot(q, k, "nt")
                dp = _dot(do, v, "nt")
                cc, lse_h, dl_h = cc_ref[h], lse_ref[h], dl_ref[h]
                ps, dss, row = [], [], None
                for j in range(n_tiles):
                    st = s[:, _hs(j)] * SCALE + cc - cr_ref[h:h + 1, _hs(j)]
                    if diagonal:
                        st = jnp.where(r >= c + j * HEAD, st, NEG_INF)
                    pt = jnp.exp(st - lse_h)
                    dst = pt * (dp[:, _hs(j)] - dl_h)
                    ps.append(pt.astype(BF16))
                    dss.append(dst)
                    row = dst if row is None else row + dst
                p = jnp.concatenate(ps, axis=1)
                ds = jnp.concatenate(dss, axis=1)
                dsb = ds.astype(BF16)
                dv_ref[:, _hs(h)] += _dot(p, do, "tn")
                dk_ref[:, _hs(h)] += _dot(dsb, q, "tn") * SCALE
                dq_ref[rows, _hs(h)] += _dot(dsb, k) * SCALE
                row_sums.append(jnp.sum(row, axis=1, keepdims=True))
                col_sums.append(jnp.sum(ds, axis=0, keepdims=True))
            dcc_ref[rows, :] += _head_column(row_sums)
            dcr_ref[...] += _head_row(col_sums)

        @pl.when(qi > ki)
        def _():
            step(False)

        @pl.when(qi == ki)
        def _():
            step(True)

    def qmap(j, i):
        return (jnp.maximum(i, j), 0)

    qspec = pl.BlockSpec((tb, FOX_W), qmap)
    kspec = pl.BlockSpec((tb, FOX_W), lambda j, i: (j, 0))
    rep = pl.BlockSpec((FOX_H, tb, HEAD), lambda j, i: (0, jnp.maximum(i, j), 0))
    rowspec = pl.BlockSpec((8, tb), lambda j, i: (0, j))
    return _call(
        name, body, (nb, nb), [qspec, kspec, kspec, rep, rowspec, qspec, rep, rep],
        [pl.BlockSpec((t, FOX_W), lambda j, i: (0, 0)), kspec, kspec,
         pl.BlockSpec((t, HEAD), lambda j, i: (0, 0)), rowspec],
        [jax.ShapeDtypeStruct((t, FOX_W), F32)] * 3 + [jax.ShapeDtypeStruct((t, HEAD), F32),
                                                       jax.ShapeDtypeStruct((8, t), F32)],
        [fq, fk, fv, c_rep, c_row, dmixed, lse, delta], rides=rides)


def _swa_logits(q, k_cur, k_prev, slope, first_block):
    w = SWA_BLOCK
    r = lax.broadcasted_iota(jnp.int32, (w, w), 0)
    j = lax.broadcasted_iota(jnp.int32, (w, w), 1)
    dist_cur = r - j
    dist_prev = w + r - j
    s_cur = _dot(q, k_cur, "nt") * SCALE - slope * dist_cur.astype(F32)
    s_cur = jnp.where(dist_cur >= 0, s_cur, NEG_INF)
    s_prev = _dot(q, k_prev, "nt") * SCALE - slope * dist_prev.astype(F32)
    s_prev = jnp.where((j > r) & jnp.logical_not(first_block), s_prev, NEG_INF)
    return s_cur, s_prev


def _slope(h):
    return float(2.0 ** (-8.0 * (h + 1) / SWA_H))


def _swa_fwd(sq, sk, sv, sp, name, rides=()):
    t = sq.shape[0]
    w = SWA_BLOCK
    nb = t // w
    group = SWA_H // SWA_KV

    def body(q_ref, kp_ref, kc_ref, vp_ref, vc_ref, sp_ref, o_ref, lse_ref):
        first = pl.program_id(0) == 0
        lses = []
        for h in range(SWA_H):
            kv = h // group
            s_cur, s_prev = _swa_logits(q_ref[:, _hs(h)], kc_ref[:, _hs(kv)], kp_ref[:, _hs(kv)], _slope(h), first)
            sink = sp_ref[R_SINK:R_SINK + 1, h:h + 1]
            m = jnp.maximum(jnp.maximum(jnp.max(s_cur, axis=-1, keepdims=True),
                                        jnp.max(s_prev, axis=-1, keepdims=True)), sink)
            p_cur = jnp.exp(s_cur - m)
            p_prev = jnp.exp(s_prev - m)
            l = jnp.sum(p_cur, axis=-1, keepdims=True) + jnp.sum(p_prev, axis=-1, keepdims=True) + jnp.exp(sink - m)
            o_ref[:, _hs(h)] = (_dot(p_cur, vc_ref[:, _hs(kv)]) + _dot(p_prev, vp_ref[:, _hs(kv)])) / l
            lses.append(m + jnp.log(l))
        lse_ref[...] = _head_column(lses)

    qspec = pl.BlockSpec((w, SWA_W), lambda n: (n, 0))
    cur = pl.BlockSpec((w, SWA_KV_W), lambda n: (n, 0))
    prev = pl.BlockSpec((w, SWA_KV_W), lambda n: (jnp.maximum(n - 1, 0), 0))
    return _call(
        name, body, (nb,), [qspec, prev, cur, prev, cur, pl.BlockSpec((16, 128), lambda n: (0, 0))],
        [qspec, pl.BlockSpec((w, HEAD), lambda n: (n, 0))],
        [jax.ShapeDtypeStruct((t, SWA_W), F32), jax.ShapeDtypeStruct((t, HEAD), F32)],
        [sq, sk, sk, sv, sv, sp], rides=rides)


def _swa_bwd(sq, sk, sv, sp, dmixed, lse, delta, name):
    t = sq.shape[0]
    w = SWA_BLOCK
    nb = t // w
    group = SWA_H // SWA_KV
    do_block = FOX_W // SWA_W
    assert FOX_W % SWA_W == 0

    def body(q_ref, kp_ref, kc_ref, vp_ref, vc_ref, sp_ref, do_ref, lse_ref, dl_ref,
             dq_ref, dk_ref, dv_ref, dsp_ref, ck, cv):
        step = pl.program_id(0)
        first = step == nb - 1

        @pl.when(step == 0)
        def _():
            ck[...] = jnp.zeros_like(ck)
            cv[...] = jnp.zeros_like(cv)
            dsp_ref[...] = jnp.zeros_like(dsp_ref)

        dk_cur = [None] * SWA_KV
        dk_prev = [None] * SWA_KV
        dv_cur = [None] * SWA_KV
        dv_prev = [None] * SWA_KV
        dsinks = []

        def add(lst, i, v):
            lst[i] = v if lst[i] is None else lst[i] + v

        for h in range(SWA_H):
            kv = h // group
            q, do = q_ref[:, _hs(h)], do_ref[:, _hs(h)]
            kc, kp, vc, vp = kc_ref[:, _hs(kv)], kp_ref[:, _hs(kv)], vc_ref[:, _hs(kv)], vp_ref[:, _hs(kv)]
            s_cur, s_prev = _swa_logits(q, kc, kp, _slope(h), first)
            lse_h = lse_ref[:, h:h + 1]
            dl_h = dl_ref[:, FOX_H + h:FOX_H + h + 1]
            p_cur = jnp.exp(s_cur - lse_h)
            p_prev = jnp.exp(s_prev - lse_h)
            p_sink = jnp.exp(sp_ref[R_SINK:R_SINK + 1, h:h + 1] - lse_h)
            ds_cur = p_cur * (_dot(do, vc, "nt") - dl_h)
            ds_prev = p_prev * (_dot(do, vp, "nt") - dl_h)
            dq_ref[:, _hs(h)] = (_dot(ds_cur, kc) + _dot(ds_prev, kp)) * SCALE
            add(dk_cur, kv, _dot(ds_cur, q, "tn") * SCALE)
            add(dk_prev, kv, _dot(ds_prev, q, "tn") * SCALE)
            add(dv_cur, kv, _dot(p_cur, do, "tn"))
            add(dv_prev, kv, _dot(p_prev, do, "tn"))
            dsinks.append(-jnp.sum(p_sink * dl_h, axis=0, keepdims=True))
        for kv in range(SWA_KV):
            dk_ref[:, _hs(kv)] = dk_cur[kv] + ck[:, _hs(kv)]
            dv_ref[:, _hs(kv)] = dv_cur[kv] + cv[:, _hs(kv)]
            ck[:, _hs(kv)] = dk_prev[kv]
            cv[:, _hs(kv)] = dv_prev[kv]
        lane = lax.broadcasted_iota(jnp.int32, (1, HEAD), 1)
        row = jnp.zeros((1, HEAD), F32)
        for h in range(SWA_H):
            row = jnp.where(lane == h, dsinks[h], row)
        dsp_ref[R_SINK:R_SINK + 1, :] += row

    def rev(n):
        return nb - 1 - n

    qspec = pl.BlockSpec((w, SWA_W), lambda n: (rev(n), 0))
    cur = pl.BlockSpec((w, SWA_KV_W), lambda n: (rev(n), 0))
    prev = pl.BlockSpec((w, SWA_KV_W), lambda n: (jnp.maximum(rev(n) - 1, 0), 0))
    col = pl.BlockSpec((w, HEAD), lambda n: (rev(n), 0))
    small = pl.BlockSpec((16, 128), lambda n: (0, 0))
    return pl.pallas_call(
        body, grid=(nb,),
        in_specs=[qspec, prev, cur, prev, cur, small, pl.BlockSpec((w, SWA_W), lambda n: (rev(n), do_block)), col, col],
        out_specs=[qspec, cur, cur, small],
        out_shape=[jax.ShapeDtypeStruct((t, SWA_W), F32), jax.ShapeDtypeStruct((t, SWA_KV_W), F32),
                   jax.ShapeDtypeStruct((t, SWA_KV_W), F32), jax.ShapeDtypeStruct((16, 128), F32)],
        scratch_shapes=[pltpu.VMEM((w, SWA_KV_W), F32), pltpu.VMEM((w, SWA_KV_W), F32)],
        compiler_params=_params(1), name=name)(sq, sk, sk, sv, sv, sp, dmixed, lse, delta)


def _mem_pre(mkv, sp, name):
    m = mkv.shape[0]

    def body(x_ref, sp_ref, k_ref, v_ref):
        for h in range(MEM_H):
            k_ref[:, _hs(h)] = _head_norm(x_ref[:, _hs(h)], sp_ref[R_MK:R_MK + 1, :]).astype(BF16)
        v_ref[...] = x_ref[:, MEM_W:2 * MEM_W].astype(BF16)

    out = jax.ShapeDtypeStruct((m, MEM_W), BF16)
    return pl.pallas_call(body, out_shape=[out, out], name=name)(mkv, sp)


def _mem_post_bwd(mkv, sp, dmk, dmv, name):
    m = mkv.shape[0]

    def body(x_ref, sp_ref, dk_ref, dv_ref, d_ref, dsp_ref):
        dsp_ref[...] = jnp.zeros_like(dsp_ref)
        total = None
        for h in range(MEM_H):
            dx, dg = _head_norm_bwd(x_ref[:, _hs(h)], sp_ref[R_MK:R_MK + 1, :], dk_ref[:, _hs(h)])
            d_ref[:, _hs(h)] = dx.astype(BF16)
            total = dg if total is None else total + dg
        d_ref[:, MEM_W:2 * MEM_W] = dv_ref[...].astype(BF16)
        dsp_ref[R_MK:R_MK + 1, :] = total

    return pl.pallas_call(body, out_shape=[jax.ShapeDtypeStruct((m, 2 * MEM_W), BF16),
                                           jax.ShapeDtypeStruct((16, 128), F32)], name=name)(mkv, sp, dmk, dmv)


def _mem_fwd(mq, mk, mv, name):
    t = mq.shape[0]
    m = mk.shape[0]
    tq = min(t, 512)

    def body(q_ref, k_ref, v_ref, o_ref, lse_ref):
        lses = []
        for h in range(MEM_H):
            s = _dot(q_ref[:, _hs(h)], k_ref[:, _hs(h)], "nt") * SCALE
            mx = jnp.max(s, axis=-1, keepdims=True)
            p = jnp.exp(s - mx)
            l = jnp.sum(p, axis=-1, keepdims=True)
            o_ref[:, _hs(h)] = _dot(p, v_ref[:, _hs(h)]) / l
            lses.append(mx + jnp.log(l))
        lse_ref[...] = _head_column(lses)

    qspec = pl.BlockSpec((tq, MEM_W), lambda i: (i, 0))
    kspec = pl.BlockSpec((m, MEM_W), lambda i: (0, 0))
    return pl.pallas_call(
        body, grid=(t // tq,), in_specs=[qspec, kspec, kspec],
        out_specs=[qspec, pl.BlockSpec((tq, HEAD), lambda i: (i, 0))],
        out_shape=[jax.ShapeDtypeStruct((t, MEM_W), F32), jax.ShapeDtypeStruct((t, HEAD), F32)],
        compiler_params=_params(1), name=name)(mq, mk, mv)


def _mem_bwd(mq, mk, mv, dmixed, lse, delta, name):
    t = mq.shape[0]
    m = mk.shape[0]
    tq = min(t, 512)
    do_block = (FOX_W + SWA_W) // MEM_W
    assert (FOX_W + SWA_W) % MEM_W == 0

    def body(q_ref, k_ref, v_ref, do_ref, lse_ref, dl_ref, dq_ref, dk_ref, dv_ref):
        @pl.when(pl.program_id(0) == 0)
        def _():
            dk_ref[...] = jnp.zeros_like(dk_ref)
            dv_ref[...] = jnp.zeros_like(dv_ref)

        for h in range(MEM_H):
            q, k, v, do = q_ref[:, _hs(h)], k_ref[:, _hs(h)], v_ref[:, _hs(h)], do_ref[:, _hs(h)]
            s = _dot(q, k, "nt") * SCALE
            p = jnp.exp(s - lse_ref[:, h:h + 1])
            col = FOX_H + SWA_H + h
            ds = p * (_dot(do, v, "nt") - dl_ref[:, col:col + 1])
            dq_ref[:, _hs(h)] = _dot(ds, k) * SCALE
            dk_ref[:, _hs(h)] += _dot(ds, q, "tn") * SCALE
            dv_ref[:, _hs(h)] += _dot(p, do, "tn")

    qspec = pl.BlockSpec((tq, MEM_W), lambda i: (i, 0))
    kspec = pl.BlockSpec((m, MEM_W), lambda i: (0, 0))
    col = pl.BlockSpec((tq, HEAD), lambda i: (i, 0))
    return pl.pallas_call(
        body, grid=(t // tq,),
        in_specs=[qspec, kspec, kspec, pl.BlockSpec((tq, MEM_W), lambda i: (i, do_block)), col, col],
        out_specs=[qspec, kspec, kspec],
        out_shape=[jax.ShapeDtypeStruct((t, MEM_W), F32), jax.ShapeDtypeStruct((m, MEM_W), F32),
                   jax.ShapeDtypeStruct((m, MEM_W), F32)],
        compiler_params=_params(1), name=name)(mq, mk, mv, dmixed, lse, delta)


def _all_gather(xs, name):
    n = len(xs)

    def body(*refs):
        x_refs, o_refs = refs[:n], refs[n:2 * n]
        send_sems, recv_sems, local_sems = refs[2 * n:]
        x, y, c = _me()
        me, sibling = (x, y, c), (x, y, 1 - c)
        chips = [(1 - x, y), (x, 1 - y), (1 - x, 1 - y)]

        def copy(a, k, block, to, src=None):
            slot = o_refs[a].at[_lin(block)]
            return pltpu.make_async_remote_copy(
                src_ref=slot if src is None else src, dst_ref=slot, send_sem=send_sems.at[a, k],
                recv_sem=recv_sems.at[a, k], device_id=to, device_id_type=MESH)

        mine = [pltpu.make_async_copy(x_refs[a], o_refs[a].at[_lin(me)], local_sems.at[a]) for a in range(n)]
        for cp in mine:
            cp.start()
        first = []
        for a in range(n):
            first.append(copy(a, 0, me, sibling, src=x_refs[a]))
            first += [copy(a, 1 + j, me, (*chip, c), src=x_refs[a]) for j, chip in enumerate(chips)]
        for cp in first:
            cp.start()
        passed = []
        for j, chip in enumerate(chips):
            for a in range(n):
                copy(a, 1 + j, (*chip, c), me).wait_recv()
                cp = copy(a, 4 + j, (*chip, c), sibling)
                cp.start()
                passed.append(cp)
        for a in range(n):
            copy(a, 0, sibling, me).wait_recv()
            for j, chip in enumerate(chips):
                copy(a, 4 + j, (*chip, 1 - c), me).wait_recv()
        for cp in first + passed:
            cp.wait_send()
        for cp in mine:
            cp.wait()

    return pl.pallas_call(
        body, in_specs=[ANY] * n, out_specs=[ANY] * n,
        out_shape=[jax.ShapeDtypeStruct((N_DEV,) + x.shape, x.dtype) for x in xs],
        scratch_shapes=[pltpu.SemaphoreType.DMA((n, 7)), pltpu.SemaphoreType.DMA((n, 7)),
                        pltpu.SemaphoreType.DMA((n,))],
        name=name)(*xs)


def _peers():
    x, y, c = _me()
    out = []
    for k in range(1, N_DEV):
        kx, ky, kc = (k >> 2) & 1, (k >> 1) & 1, k & 1
        out.append(((1 - x) if kx else x, (1 - y) if ky else y, (1 - c) if kc else c))
    return out


def _all_reduce_small(xs, name):
    n = len(xs)

    def body(*refs):
        x_refs, o_refs = refs[:n], refs[n:2 * n]
        bufs = refs[2 * n:3 * n]
        send_sems, recv_sems = refs[3 * n:]
        me = _lin(_me())
        peers = _peers()
        for a in range(n):
            bufs[a][me] = x_refs[a][...]
        sends = []
        for a in range(n):
            for k, peer in enumerate(peers):
                sends.append(pltpu.make_async_remote_copy(
                    src_ref=bufs[a].at[me], dst_ref=bufs[a].at[me], send_sem=send_sems.at[a, k],
                    recv_sem=recv_sems.at[a, k], device_id=peer, device_id_type=MESH))
        for cp in sends:
            cp.start()
        for a in range(n):
            for k, peer in enumerate(peers):
                pltpu.make_async_remote_copy(
                    src_ref=bufs[a].at[me], dst_ref=bufs[a].at[_lin(peer)], send_sem=send_sems.at[a, k],
                    recv_sem=recv_sems.at[a, k], device_id=peer, device_id_type=MESH).wait_recv()
        for cp in sends:
            cp.wait_send()
        for a in range(n):
            total = bufs[a][0]
            for q in range(1, N_DEV):
                total = total + bufs[a][q]
            o_refs[a][...] = total

    vmem = pl.BlockSpec(memory_space=pltpu.VMEM)
    return pl.pallas_call(
        body, in_specs=[vmem] * n, out_specs=[vmem] * n,
        out_shape=[jax.ShapeDtypeStruct(x.shape, F32) for x in xs],
        scratch_shapes=[pltpu.VMEM((N_DEV,) + x.shape, F32) for x in xs]
        + [pltpu.SemaphoreType.DMA((n, 7)), pltpu.SemaphoreType.DMA((n, 7))],
        name=name)(*xs)


def _pair_add(part, got, name):
    _, rows, cols = part.shape
    tm = _rows_tile(rows, cols * 2, budget=2 << 20)
    core = jnp.reshape(lax.axis_index("c"), (1,)).astype(jnp.int32)

    def body(c_ref, p_ref, g_ref, o_ref):
        o_ref[...] = (p_ref[...].astype(F32) + g_ref[...].astype(F32)).astype(BF16)

    spec = pl.BlockSpec((None, tm, cols), lambda q, i, c: (q, i, 0))
    grid_spec = pltpu.PrefetchScalarGridSpec(
        num_scalar_prefetch=1, grid=(4, rows // tm),
        in_specs=[pl.BlockSpec((None, tm, cols), lambda q, i, c: (2 * q + c[0], i, 0)), spec], out_specs=spec)
    return pl.pallas_call(body, grid_spec=grid_spec, out_shape=jax.ShapeDtypeStruct((4, rows, cols), BF16),
                          compiler_params=_params(2), name=name)(core, part, got)


def _adam_math(w, g, m, v):
    nm = ADAM_B1 * m + (1.0 - ADAM_B1) * g
    nv = ADAM_B2 * v + (1.0 - ADAM_B2) * (g * g)
    m_hat = nm / (1.0 - ADAM_B1 ** ADAM_STEP)
    v_hat = nv / (1.0 - ADAM_B2 ** ADAM_STEP)
    return -ADAM_LR * (m_hat / (jnp.sqrt(v_hat) + ADAM_EPS) + ADAM_WD * w), nm, nv


def _sum_chips(got, name):
    _, rows, cols = got.shape
    tm = _rows_tile(rows, cols * 2 * 4, budget=2 << 20)

    def body(r_ref, o_ref):
        o_ref[...] = ((r_ref[0].astype(F32) + r_ref[1].astype(F32)) + r_ref[2].astype(F32)) + r_ref[3].astype(F32)

    return pl.pallas_call(
        body, grid=(rows // tm,), in_specs=[pl.BlockSpec((4, tm, cols), lambda i: (0, i, 0))],
        out_specs=pl.BlockSpec((tm, cols), lambda i: (i, 0)), out_shape=jax.ShapeDtypeStruct((rows, cols), F32),
        compiler_params=_params(1), name=name)(got)


def _sum_adamw(got, col_block, w, m, v, name):
    _, rows, cols = w.shape
    tm = _rows_tile(rows, cols * 4, budget=1 << 20)

    def body(r_ref, w_ref, m_ref, v_ref, g_ref, d_ref, nm_ref, nv_ref):
        g = ((r_ref[0].astype(F32) + r_ref[1].astype(F32)) + r_ref[2].astype(F32)) + r_ref[3].astype(F32)
        g_ref[...] = g
        d_ref[...], nm_ref[...], nv_ref[...] = _adam_math(w_ref[...], g, m_ref[...], v_ref[...])

    spec = pl.BlockSpec((None, tm, cols), lambda i: (0, i, 0))
    out = jax.ShapeDtypeStruct(w.shape, F32)
    return pl.pallas_call(
        body, grid=(rows // tm,), in_specs=[pl.BlockSpec((4, tm, cols), lambda i: (0, i, col_block)), spec, spec, spec],
        out_specs=[spec] * 4, out_shape=[out] * 4, compiler_params=_params(1), name=name)(got, w, m, v)


def _adamw(w, g, m, v, name):
    rows, cols = w.shape
    tm = _rows_tile(rows, cols * 4, budget=2 << 20, mult=8)

    def body(w_ref, g_ref, m_ref, v_ref, d_ref, nm_ref, nv_ref):
        d_ref[...], nm_ref[...], nv_ref[...] = _adam_math(w_ref[...], g_ref[...], m_ref[...], v_ref[...])

    spec = pl.BlockSpec((tm, cols), lambda i: (i, 0))
    out = jax.ShapeDtypeStruct(w.shape, F32)
    return pl.pallas_call(body, grid=(rows // tm,), in_specs=[spec] * 4, out_specs=[spec] * 3,
                          out_shape=[out] * 3, compiler_params=_params(1), name=name)(w, g, m, v)


def _permute_in(w):
    logit0 = 3 * FOX_W
    pad = jnp.zeros(w.shape[:-1] + (HEAD - N_LOGIT,), w.dtype)
    return jnp.concatenate([w[..., :logit0], w[..., logit0 + N_LOGIT:], w[..., logit0:logit0 + N_LOGIT], pad], axis=-1)


def _unpermute_in(w):
    logit0 = 3 * FOX_W
    return jnp.concatenate([w[..., :logit0], w[..., C_FL:C_FL + N_LOGIT], w[..., logit0:C_FL]], axis=-1)


def _pad_row(v, width):
    return jnp.pad(v, ((0, 0), (0, width - v.shape[1])))


def _pack_small(fq, fk, sq, sk, mq, mk, fb, sinks):
    rows = [fq, fk, sq, sk, mq, mk, _pad_row(fb, HEAD), _pad_row(sinks, HEAD)]
    return jnp.concatenate(rows + [jnp.zeros((8, HEAD), F32)], axis=0)


def _pack_norms(a, b, c, d):
    return jnp.concatenate([a, b, c, d, jnp.zeros((4, a.shape[1]), F32)], axis=0)


def kernel(x, mem, ffn1_norm, ffn1_gate, ffn1_up, ffn1_down, mix_norm, mem_norm, w_in, forget_bias, w_mem_k, w_mem_v, fox_q_gain, fox_k_gain, swa_q_gain, swa_k_gain, swa_sinks, mem_q_gain, mem_k_gain, w_out, ffn2_norm, ffn2_gate, ffn2_up, ffn2_down, loss_target, m_ffn1_norm, m_ffn1_gate, m_ffn1_up, m_ffn1_down, m_mix_norm, m_mem_norm, m_w_in, m_forget_bias, m_w_mem_k, m_w_mem_v, m_fox_q_gain, m_fox_k_gain, m_swa_q_gain, m_swa_k_gain, m_swa_sinks, m_mem_q_gain, m_mem_k_gain, m_w_out, m_ffn2_norm, m_ffn2_gate, m_ffn2_up, m_ffn2_down, v_ffn1_norm, v_ffn1_gate, v_ffn1_up, v_ffn1_down, v_mix_norm, v_mem_norm, v_w_in, v_forget_bias, v_w_mem_k, v_w_mem_v, v_fox_q_gain, v_fox_k_gain, v_swa_q_gain, v_swa_k_gain, v_swa_sinks, v_mem_q_gain, v_mem_k_gain, v_w_out, v_ffn2_norm, v_ffn2_gate, v_ffn2_up, v_ffn2_down):
    x0 = x[0]
    mem0 = mem[0]
    target = loss_target[0]
    t, d = x0.shape
    d_shard = w_in.shape[1]
    m_len = mem0.shape[0]
    tm = min(t, 512)
    tk = min(t, 512)
    tn = IN_W // 3
    tkw, tnw = min(t, 1024), IN_W // 3

    def swap(a):
        return jnp.swapaxes(a, 1, 2)

    gate1, up1, gate2, up2 = swap(ffn1_gate), swap(ffn1_up), swap(ffn2_gate), swap(ffn2_up)

    local = {
        "g1": gate1[0], "u1": up1[0], "d1": ffn1_down[0],
        "g2": gate2[0], "u2": up2[0], "d2": ffn2_down[0],
        "in": _permute_in(w_in[0]), "out": w_out[0],
        "mkv": jnp.concatenate([w_mem_k[0], w_mem_v[0]], axis=1),
    }
    shard = {k: _cast_bf16(v, f"cast_{k}") for k, v in local.items()}
    sp = _pack_small(fox_q_gain, fox_k_gain, swa_q_gain, swa_k_gain, mem_q_gain, mem_k_gain, forget_bias, swa_sinks)
    wt = {}

    wt["g1"], wt["u1"] = _all_gather([shard["g1"], shard["u1"]], "gather_ffn1_in")
    xn1 = _rms_fwd(x0, ffn1_norm, "ffn1_norm")
    (g1, u1, h1), ((wt["d1"], wt["in"]),) = _ffn_up(
        xn1, wt["g1"], wt["u1"], "ffn1", rides=[_ride_gather([shard["d1"], shard["in"]], 0.87)])
    x1, ((wt["out"], wt["mkv"]),) = _ffn_down(
        x0, h1, wt["d1"], "ffn1", rides=[_ride_gather([shard["out"], shard["mkv"]], 0.6)])
    w_in_full = wt["in"].reshape(d, IN_W)

    hn = _rms_fwd(x1, mix_norm, "mix_norm")
    proj, (half,) = _mm(
        "proj", [(hn, pl.BlockSpec((tm, d), lambda n, i, k: (i, 0)),
                  w_in_full, pl.BlockSpec((d, tn), lambda n, i, k: (0, n)))],
        "nn", (3, t // tm, 1), jax.ShapeDtypeStruct((t, IN_W), F32), pl.BlockSpec((tm, tn), lambda n, i, k: (i, n)),
        rides=[_ride_gather_chips([shard["g2"]])])
    w_out_full = wt["out"].reshape(d, d)
    w_mkv_full = wt["mkv"].reshape(d, 2 * MEM_W)
    fq, fk, fv, sq, sk, sv, mq, c_col = _attn_pre(proj, sp, "attn_pre")
    c_row = jnp.transpose(c_col[:, :8])
    c_rep = jnp.broadcast_to(c_row[:FOX_H, :, None], (FOX_H, t, HEAD))

    mn = _rms_fwd(mem0, mem_norm, "mem_norm")
    mkv = _mm("mem_kv", [(mn, pl.BlockSpec((m_len, d), lambda k: (0, 0)),
                          w_mkv_full, pl.BlockSpec((d, 2 * MEM_W), lambda k: (0, 0)))],
              "nn", (1,), jax.ShapeDtypeStruct((m_len, 2 * MEM_W), F32),
              pl.BlockSpec((m_len, 2 * MEM_W), lambda k: (0, 0)))
    mk, mv = _mem_pre(mkv, sp, "mem_pre")

    (o_a, lse_a), ((wt["g2"],), half) = _fox_fwd(
        fq, fk, fv, c_rep, c_row, "fox_fwd", rides=[_ride_gather_sibling(half), _ride_gather_chips([shard["u2"]])])
    (o_b, lse_b), ((wt["u2"],),) = _swa_fwd(sq, sk, sv, sp, "swa_fwd", rides=[_ride_gather_sibling(half)])
    o_c, lse_c = _mem_fwd(mq, mk, mv, "mem_fwd")

    def rows_spec(width):
        return pl.BlockSpec((tm, width), lambda i, k: (i, 0))

    def wout_rows(first, width):
        assert first % width == 0
        return pl.BlockSpec((width, d), lambda i, k: (first // width, 0))

    xspec = pl.BlockSpec((tm, d), lambda i, k: (i, 0))
    x2 = _mm(
        "mix_out",
        [(o_a, rows_spec(FOX_W), w_out_full, wout_rows(0, FOX_W)),
         (o_b, rows_spec(SWA_W), w_out_full, wout_rows(FOX_W, SWA_W)),
         (o_c, rows_spec(MEM_W), w_out_full, wout_rows(FOX_W + SWA_W, MEM_W))],
        "nn", (t // tm, 1), jax.ShapeDtypeStruct((t, d), F32), xspec, res=x1, res_spec=xspec)

    xn2 = _rms_fwd(x2, ffn2_norm, "ffn2_norm")
    (g2, u2, h2), ((wt["d2"],),) = _ffn_up(xn2, wt["g2"], wt["u2"], "ffn2", rides=[_ride_gather([shard["d2"]], 0.75)])
    x3 = _ffn_down(x2, h2, wt["d2"], "ffn2")

    dy, dyb, sq_err = _loss_head(x3, target, "loss_head")
    loss = lax.psum(0.5 * sq_err[0, 0] / d, ("x", "y", "c"))

    got = {}
    paired = {}
    landed = {}

    def pair(k, part):
        paired[k] = _pair_add(part, got[k], f"pair_{k}")

    (dg2, du2), _ = _ffn_dact(dyb, wt["d2"], g2, u2, "ffn2")
    part_d2 = _ffn_dw(h2, dyb, 0.5, "ffn2_dwd")
    part_g2, ((got["d2"],),) = _ffn_dw(dg2, xn2, 1.0, "ffn2_dwg", rides=[_ride_scatter_sibling([part_d2])])
    pair("d2", part_d2)
    part_u2, ((landed["d2"],), (got["g2"],)) = _ffn_dw(
        du2, xn2, 1.0, "ffn2_dwu", rides=[_ride_scatter_chips([paired["d2"]]), _ride_scatter_sibling([part_g2])])
    pair("g2", part_g2)
    dxn2, ((landed["g2"],),) = _ffn_contract(dg2, wt["g2"], "ffn2_dxn_g", rides=[_ride_scatter_chips([paired["g2"]])])
    dxn2, ((got["u2"],),) = _ffn_contract(du2, wt["u2"], "ffn2_dxn_u", res=dxn2,
                                          rides=[_ride_scatter_sibling([part_u2])])
    pair("u2", part_u2)
    dx2, dx2b, dgain_ffn2 = _rms_bwd(x2, ffn2_norm, dxn2, dy, "ffn2_norm_bwd")

    dmixed = _mm("mix_out_dx", [(dx2b, xspec, w_out_full, pl.BlockSpec((d, d), lambda i, k: (0, 0)))],
                 "nt", (t // tm, 1), jax.ShapeDtypeStruct((t, d), F32), xspec)

    def k_rows(width):
        return pl.BlockSpec((tk, width), lambda j, k: (k, 0))

    part_out = [
        _mm(f"mix_out_dw{i}", [(o, k_rows(width), dx2b, k_rows(d))], "tn", (1, t // tk),
            jax.ShapeDtypeStruct((width, d), BF16), pl.BlockSpec((width, d), lambda j, k: (0, 0)))
        for i, (o, width) in enumerate(((o_a, FOX_W), (o_b, SWA_W), (o_c, MEM_W)))
    ]
    part_out = jnp.concatenate(part_out, axis=0).reshape(N_DEV, d_shard, d)

    delta, delta_rep = _delta(dmixed, o_a, o_b, o_c, "attn_delta")
    (dfq, dfk, dfv, dc_col, dc_row), ((landed["u2"],),) = _fox_bwd(
        fq, fk, fv, c_rep, c_row, dmixed, lse_a, delta_rep, "fox_bwd", rides=[_ride_scatter_chips([paired["u2"]])])
    dsq, dsk, dsv, dsp_sink = _swa_bwd(sq, sk, sv, sp, dmixed, lse_b, delta, "swa_bwd")
    dmq, dmk, dmv = _mem_bwd(mq, mk, mv, dmixed, lse_c, delta, "mem_bwd")

    dmkv, dsp_mem = _mem_post_bwd(mkv, sp, dmk, dmv, "mem_post_bwd")
    part_mkv = _mm("mem_kv_dw", [(mn, pl.BlockSpec((m_len, d), lambda k: (0, 0)),
                                  dmkv, pl.BlockSpec((m_len, 2 * MEM_W), lambda k: (0, 0)))],
                   "tn", (1,), jax.ShapeDtypeStruct((d, 2 * MEM_W), BF16),
                   pl.BlockSpec((d, 2 * MEM_W), lambda k: (0, 0))).reshape(N_DEV, d_shard, 2 * MEM_W)
    dmn = _mm("mem_kv_dx", [(dmkv, pl.BlockSpec((m_len, 2 * MEM_W), lambda k: (0, 0)),
                             w_mkv_full, pl.BlockSpec((d, 2 * MEM_W), lambda k: (0, 0)))],
              "nt", (1,), jax.ShapeDtypeStruct((m_len, d), F32), pl.BlockSpec((m_len, d), lambda k: (0, 0)))
    _, _, dgain_mem = _rms_bwd(mem0, mem_norm, dmn, None, "mem_norm_bwd")

    dc_row_t = _pad_row(jnp.transpose(dc_row), HEAD)
    dproj, dsp_attn = _attn_post_bwd(proj, sp, dfq, dfk, dfv, dsq, dsk, dsv, dmq, dc_col, dc_row_t, "attn_post_bwd")
    dhn, ((got["out"], got["mkv"]),) = _mm(
        "proj_dx", [(dproj, pl.BlockSpec((tm, IN_W), lambda i, k: (i, 0)),
                     w_in_full, pl.BlockSpec((d, IN_W), lambda i, k: (0, 0)))],
        "nt", (t // tm, 1), jax.ShapeDtypeStruct((t, d), F32), xspec,
        rides=[_ride_scatter_sibling([part_out, part_mkv])])
    pair("out", part_out)
    pair("mkv", part_mkv)
    part_in, ((landed["out"], landed["mkv"]),) = _mm(
        "proj_dw", [(hn, pl.BlockSpec((tkw, d), lambda n, k: (k, 0)),
                     dproj, pl.BlockSpec((tkw, tnw), lambda n, k: (k, n)))],
        "tn", (IN_W // tnw, t // tkw), jax.ShapeDtypeStruct((d, IN_W), BF16), pl.BlockSpec((d, tnw), lambda n, k: (0, n)),
        rides=[_ride_scatter_chips([paired["out"], paired["mkv"]])])
    part_in = part_in.reshape(N_DEV, d_shard, IN_W)
    dx1, dx1b, dgain_mix = _rms_bwd(x1, mix_norm, dhn, dx2, "mix_norm_bwd")

    (dg1, du1), ((got["in"],),) = _ffn_dact(dx1b, wt["d1"], g1, u1, "ffn1", rides=[_ride_scatter_sibling([part_in])])
    pair("in", part_in)
    part_d1, ((landed["in"],),) = _ffn_dw(h1, dx1b, 0.5, "ffn1_dwd", rides=[_ride_scatter_chips([paired["in"]])])
    part_g1, ((got["d1"],),) = _ffn_dw(dg1, xn1, 1.0, "ffn1_dwg", rides=[_ride_scatter_sibling([part_d1])])
    pair("d1", part_d1)
    part_u1, ((landed["d1"],), (got["g1"],)) = _ffn_dw(
        du1, xn1, 1.0, "ffn1_dwu", rides=[_ride_scatter_chips([paired["d1"]]), _ride_scatter_sibling([part_g1])])
    pair("g1", part_g1)
    dxn1, ((landed["g1"],), (got["u1"],)) = _ffn_contract(
        dg1, wt["g1"], "ffn1_dxn_g", rides=[_ride_scatter_chips([paired["g1"]]), _ride_scatter_sibling([part_u1])])
    pair("u1", part_u1)
    dxn1, ((landed["u1"],),) = _ffn_contract(du1, wt["u1"], "ffn1_dxn_u", res=dxn1,
                                             rides=[_ride_scatter_chips([paired["u1"]])])
    grad_x, _, dgain_ffn1 = _rms_bwd(x0, ffn1_norm, dxn1, dx1, "ffn1_norm_bwd")

    norms_sum, small_sum = _all_reduce_small(
        [_pack_norms(dgain_ffn1, dgain_mix, dgain_mem, dgain_ffn2), dsp_attn + dsp_sink + dsp_mem], "reduce_small")

    result = {
        "ffn1_gate": map(swap, _sum_adamw(landed["g1"], 0, gate1, swap(m_ffn1_gate), swap(v_ffn1_gate), "adamw_ffn1_gate")),
        "ffn1_up": map(swap, _sum_adamw(landed["u1"], 0, up1, swap(m_ffn1_up), swap(v_ffn1_up), "adamw_ffn1_up")),
        "ffn1_down": _sum_adamw(landed["d1"], 0, ffn1_down, m_ffn1_down, v_ffn1_down, "adamw_ffn1_down"),
        "w_mem_k": _sum_adamw(landed["mkv"], 0, w_mem_k, m_w_mem_k, v_w_mem_k, "adamw_w_mem_k"),
        "w_mem_v": _sum_adamw(landed["mkv"], 1, w_mem_v, m_w_mem_v, v_w_mem_v, "adamw_w_mem_v"),
        "w_out": _sum_adamw(landed["out"], 0, w_out, m_w_out, v_w_out, "adamw_w_out"),
        "ffn2_gate": map(swap, _sum_adamw(landed["g2"], 0, gate2, swap(m_ffn2_gate), swap(v_ffn2_gate), "adamw_ffn2_gate")),
        "ffn2_up": map(swap, _sum_adamw(landed["u2"], 0, up2, swap(m_ffn2_up), swap(v_ffn2_up), "adamw_ffn2_up")),
        "ffn2_down": _sum_adamw(landed["d2"], 0, ffn2_down, m_ffn2_down, v_ffn2_down, "adamw_ffn2_down"),
    }
    grad_in = _unpermute_in(_sum_chips(landed["in"], "sum_w_in"))
    result["w_in"] = (grad_in[None],) + tuple(
        o[None] for o in _adamw(w_in[0], grad_in, m_w_in[0], v_w_in[0], "adamw_w_in"))

    norm_names = ["ffn1_norm", "mix_norm", "mem_norm", "ffn2_norm"]
    norm_w = _pack_norms(ffn1_norm, mix_norm, mem_norm, ffn2_norm)
    norm_m = _pack_norms(m_ffn1_norm, m_mix_norm, m_mem_norm, m_ffn2_norm)
    norm_v = _pack_norms(v_ffn1_norm, v_mix_norm, v_mem_norm, v_ffn2_norm)
    outs = (norms_sum,) + tuple(_adamw(norm_w, norms_sum, norm_m, norm_v, "adamw_norms"))
    for i, k in enumerate(norm_names):
        result[k] = tuple(o[i:i + 1] for o in outs)

    small_names = ["fox_q_gain", "fox_k_gain", "swa_q_gain", "swa_k_gain", "mem_q_gain", "mem_k_gain",
                   "forget_bias", "swa_sinks"]
    small_m = _pack_small(m_fox_q_gain, m_fox_k_gain, m_swa_q_gain, m_swa_k_gain, m_mem_q_gain, m_mem_k_gain,
                          m_forget_bias, m_swa_sinks)
    small_v = _pack_small(v_fox_q_gain, v_fox_k_gain, v_swa_q_gain, v_swa_k_gain, v_mem_q_gain, v_mem_k_gain,
                          v_forget_bias, v_swa_sinks)
    outs = (small_sum,) + tuple(_adamw(sp, small_sum, small_m, small_v, "adamw_small"))
    for i, k in enumerate(small_names):
        width = N_LOGIT if k in ("forget_bias", "swa_sinks") else HEAD
        result[k] = tuple(o[i:i + 1, :width] for o in outs)

    order = ["ffn1_norm", "ffn1_gate", "ffn1_up", "ffn1_down", "mix_norm", "mem_norm", "w_in", "forget_bias",
             "w_mem_k", "w_mem_v", "fox_q_gain", "fox_k_gain", "swa_q_gain", "swa_k_gain", "swa_sinks",
             "mem_q_gain", "mem_k_gain", "w_out", "ffn2_norm", "ffn2_gate", "ffn2_up", "ffn2_down"]
    result = {k: tuple(v) for k, v in result.items()}
    flat = [loss, grad_x[None]]
    for kind in range(4):
        flat += [result[k][kind] for k in order]
    return tuple(flat)
```

```python
import functools

import jax
import jax.numpy as jnp
from jax import lax
from jax.experimental import pallas as pl
from jax.experimental.pallas import tpu as pltpu

F32 = jnp.float32
BF16 = jnp.bfloat16
MESH = pl.DeviceIdType.MESH
ANY = pl.BlockSpec(memory_space=pl.ANY)

N_DEV = 8
EPS = 1e-6
NEG_INF = -1e30
HEAD = 128
FOX_H, SWA_H, SWA_KV, MEM_H = 6, 6, 2, 4
FOX_W, SWA_W, SWA_KV_W, MEM_W = FOX_H * HEAD, SWA_H * HEAD, SWA_KV * HEAD, MEM_H * HEAD
SCALE = HEAD ** -0.5
SWA_BLOCK = 128
C_FQ, C_FK, C_FV = 0, FOX_W, 2 * FOX_W
C_SQ = 3 * FOX_W
C_SK = C_SQ + SWA_W
C_SV = C_SK + SWA_KV_W
C_MQ = C_SV + SWA_KV_W
C_FL = C_MQ + MEM_W
IN_W = C_FL + HEAD
N_LOGIT = FOX_H
R_FQ, R_FK, R_SQ, R_SK, R_MQ, R_MK, R_FB, R_SINK = range(8)
ADAM_LR, ADAM_B1, ADAM_B2, ADAM_EPS, ADAM_WD, ADAM_STEP = 0.001, 0.9, 0.999, 1e-08, 0.01, 10
VMEM_BYTES = 56 * 1024 * 1024

DN = {
    "nn": (((1,), (0,)), ((), ())),
    "nt": (((1,), (1,)), ((), ())),
    "tn": (((0,), (0,)), ((), ())),
}


def _params(n_axes):
    return pltpu.CompilerParams(dimension_semantics=("arbitrary",) * n_axes, vmem_limit_bytes=VMEM_BYTES)


def _dot(a, b, dims="nn"):
    return lax.dot_general(a.astype(BF16), b.astype(BF16), DN[dims], preferred_element_type=F32)


def _sigmoid(x):
    return 0.5 * jnp.tanh(0.5 * x) + 0.5


def _me():
    return lax.axis_index("x"), lax.axis_index("y"), lax.axis_index("c")


def _lin(p):
    return 4 * p[0] + 2 * p[1] + p[2]


def _rows_tile(rows, row_bytes, budget=4 << 20, mult=16):
    best = None
    for k in range(1, rows + 1):
        if rows % k == 0 and (rows // k) % mult == 0 and (rows // k) * row_bytes <= budget:
            best = rows // k
            break
    assert best is not None, (rows, row_bytes)
    return best


class _Ride:
    def __init__(self, inputs, out_shapes, aliases, n_remote, n_local, start, wait):
        self.inputs, self.out_shapes, self.aliases = list(inputs), list(out_shapes), dict(aliases)
        self.n_remote, self.n_local, self.start, self.wait = n_remote, n_local, start, wait


def _remote(src, dst, send, recv, k, to):
    return pltpu.make_async_remote_copy(src_ref=src, dst_ref=dst, send_sem=send.at[k], recv_sem=recv.at[k],
                                        device_id=to, device_id_type=MESH)


def _other_chips(x, y):
    return [(1 - x, y), (x, 1 - y), (1 - x, 1 - y)]


ALL_CHIPS = [(0, 0), (0, 1), (1, 0), (1, 1)]


def _ride_gather_chips(xs):
    n = len(xs)

    def copies(ins, outs, send, recv):
        x, y, c = _me()
        out = []
        for a in range(n):
            for j, chip in enumerate(_other_chips(x, y)):
                peer = (*chip, c)
                out.append((_remote(ins[a], outs[a].at[_lin((x, y, c))], send, recv, 3 * a + j, peer),
                            _remote(ins[a], outs[a].at[_lin(peer)], send, recv, 3 * a + j, peer)))
        return out

    def mine(ins, outs, local):
        me = _lin(_me())
        return [pltpu.make_async_copy(ins[a], outs[a].at[me], local.at[a]) for a in range(n)]

    def start(ins, outs, send, recv, local):
        for cp in mine(ins, outs, local):
            cp.start()
        for sent, _ in copies(ins, outs, send, recv):
            sent.start()

    def wait(ins, outs, send, recv, local):
        for sent, landed in copies(ins, outs, send, recv):
            landed.wait_recv()
            sent.wait_send()
        for cp in mine(ins, outs, local):
            cp.wait()

    shapes = [jax.ShapeDtypeStruct((N_DEV,) + x.shape, x.dtype) for x in xs]
    return _Ride(xs, shapes, {}, 3 * n, n, start, wait)


def _ride_gather_sibling(bufs):
    n = len(bufs)

    def copies(outs, send, recv):
        x, y, c = _me()
        out = []
        for a in range(n):
            for q, (px, py) in enumerate(ALL_CHIPS):
                there = outs[a].at[4 * px + 2 * py + c]
                here = outs[a].at[4 * px + 2 * py + 1 - c]
                out.append((_remote(there, there, send, recv, 4 * a + q, (x, y, 1 - c)),
                            _remote(here, here, send, recv, 4 * a + q, (x, y, 1 - c))))
        return out

    def start(ins, outs, send, recv, local):
        for sent, _ in copies(outs, send, recv):
            sent.start()

    def wait(ins, outs, send, recv, local):
        for sent, landed in copies(outs, send, recv):
            landed.wait_recv()
            sent.wait_send()

    shapes = [jax.ShapeDtypeStruct(b.shape, b.dtype) for b in bufs]
    return _Ride(bufs, shapes, {a: a for a in range(n)}, 4 * n, 0, start, wait)


def _ride_gather(xs, mid_frac):
    n = len(xs)
    chips = _ride_gather_chips(xs)

    def sibling_copies(outs, send, recv):
        x, y, c = _me()
        out = []
        for a in range(n):
            for q, (px, py) in enumerate(ALL_CHIPS):
                there = outs[a].at[4 * px + 2 * py + c]
                here = outs[a].at[4 * px + 2 * py + 1 - c]
                k = 3 * n + 4 * a + q
                out.append((_remote(there, there, send, recv, k, (x, y, 1 - c)),
                            _remote(here, here, send, recv, k, (x, y, 1 - c))))
        return out

    def mid(ins, outs, send, recv, local):
        chips.wait(ins, outs, send, recv, local)
        for sent, _ in sibling_copies(outs, send, recv):
            sent.start()

    def wait(ins, outs, send, recv, local):
        for sent, landed in sibling_copies(outs, send, recv):
            landed.wait_recv()
            sent.wait_send()

    ride = _Ride(xs, chips.out_shapes, {}, 7 * n, n, chips.start, wait)
    ride.mid, ride.mid_frac = mid, mid_frac
    return ride


def _ride_scatter_sibling(parts):
    n = len(parts)

    def copies(ins, outs, send, recv):
        x, y, c = _me()
        out = []
        for a in range(n):
            for q, (px, py) in enumerate(ALL_CHIPS):
                cp = _remote(ins[a].at[4 * px + 2 * py + 1 - c], outs[a].at[q], send, recv, 4 * a + q, (x, y, 1 - c))
                out.append(cp)
        return out

    def start(ins, outs, send, recv, local):
        for cp in copies(ins, outs, send, recv):
            cp.start()

    def wait(ins, outs, send, recv, local):
        for cp in copies(ins, outs, send, recv):
            cp.wait_recv()
            cp.wait_send()

    shapes = [jax.ShapeDtypeStruct((4,) + p.shape[1:], p.dtype) for p in parts]
    return _Ride(parts, shapes, {}, 4 * n, 0, start, wait)


def _ride_scatter_chips(pairs, rows=None, into=None):
    n = len(pairs)

    def part(ref, slot):
        return ref.at[slot] if rows is None else ref.at[slot, pl.ds(rows[0], rows[1])]

    def copies(ins, outs, send, recv):
        x, y, c = _me()
        out = []
        for a in range(n):
            for j, (px, py) in enumerate(_other_chips(x, y)):
                peer = (px, py, c)
                src = part(ins[a], 2 * px + py)
                out.append((_remote(src, part(outs[a], 2 * x + y), send, recv, 3 * a + j, peer),
                            _remote(src, part(outs[a], 2 * px + py), send, recv, 3 * a + j, peer)))
        return out

    def mine(ins, outs, local):
        x, y, _ = _me()
        return [pltpu.make_async_copy(part(ins[a], 2 * x + y), part(outs[a], 2 * x + y), local.at[a])
                for a in range(n)]

    def start(ins, outs, send, recv, local):
        for cp in mine(ins, outs, local):
            cp.start()
        for sent, _ in copies(ins, outs, send, recv):
            sent.start()

    def wait(ins, outs, send, recv, local):
        for sent, landed in copies(ins, outs, send, recv):
            landed.wait_recv()
            sent.wait_send()
        for cp in mine(ins, outs, local):
            cp.wait()

    shapes = [jax.ShapeDtypeStruct(p.shape, p.dtype) for p in pairs]
    if into is None:
        return _Ride(pairs, shapes, {}, 3 * n, n, start, wait)
    return _Ride(list(pairs) + list(into), shapes, {n + a: a for a in range(n)}, 3 * n, n, start, wait)


def _call(name, body, grid, in_specs, out_specs, out_shape, operands, scratch=(), rides=()):
    n_in, n_out, n_scr = len(operands), len(out_shape), len(scratch)
    ride_in, ride_out, ride_scr, aliases, spans = [], [], [], {}, []
    for r in rides:
        for i, o in r.aliases.items():
            aliases[n_in + len(ride_in) + i] = n_out + len(ride_out) + o
        spans.append((len(ride_in), len(r.inputs), len(ride_out), len(r.out_shapes)))
        ride_in += r.inputs
        ride_out += r.out_shapes
        ride_scr += [pltpu.SemaphoreType.DMA((r.n_remote,)), pltpu.SemaphoreType.DMA((r.n_remote,)),
                     pltpu.SemaphoreType.DMA((max(r.n_local, 1),))]

    def wrapped(*refs):
        c_in, r_in = refs[:n_in], refs[n_in:n_in + len(ride_in)]
        p = n_in + len(ride_in)
        c_out, r_out = refs[p:p + n_out], refs[p + n_out:p + n_out + len(ride_out)]
        p += n_out + len(ride_out)
        c_scr, r_scr = refs[p:p + n_scr], refs[p + n_scr:]

        n_steps = functools.reduce(lambda a, b: a * b, grid, 1)
        step = functools.reduce(lambda acc, ax: acc * grid[ax] + pl.program_id(ax), range(len(grid)), 0)

        def each(method, at):
            for k, (r, (i0, ni, o0, no)) in enumerate(zip(rides, spans)):
                fn = getattr(r, method, None)
                if fn is None:
                    continue
                run = functools.partial(fn, r_in[i0:i0 + ni], r_out[o0:o0 + no], *r_scr[3 * k:3 * k + 3])
                if grid:
                    pl.when(step == at(r))(run)
                else:
                    run()

        each("start", lambda r: 0)
        each("mid", lambda r: int(r.mid_frac * (n_steps - 1)))
        body(*c_in, *c_out, *c_scr)
        each("wait", lambda r: n_steps - 1)

    outs = pl.pallas_call(
        wrapped, grid=grid, in_specs=list(in_specs) + [ANY] * len(ride_in),
        out_specs=list(out_specs) + [ANY] * len(ride_out), out_shape=list(out_shape) + ride_out,
        scratch_shapes=list(scratch) + ride_scr, input_output_aliases=aliases,
        compiler_params=_params(len(grid)), name=name)(*operands, *ride_in)
    outs = list(outs)
    ride_results = [outs[n_out + o0:n_out + o0 + no] for (_, _, o0, no) in spans]
    return outs[:n_out], ride_results


def _only_copies(name, rides):
    return _call(name, lambda: None, (), [], [], [], [], rides=rides)[1]


def _mm(name, pairs, dims, grid, out_shape, out_spec, res=None, res_spec=None, alpha=1.0, rides=()):
    n = len(pairs)
    nk = grid[-1]
    kax = len(grid) - 1
    acc_shape = tuple(d for d in out_spec.block_shape if d is not None)

    def body(*refs):
        pos = 2 * n
        r_ref = None
        if res is not None:
            r_ref = refs[pos]
            pos += 1
        o_ref = refs[pos]
        part = None
        for p in range(n):
            d = _dot(refs[2 * p][...], refs[2 * p + 1][...], dims)
            part = d if part is None else part + d

        def finish(acc):
            if alpha != 1.0:
                acc = acc * alpha
            if r_ref is not None:
                acc = r_ref[...] + acc
            o_ref[...] = acc.astype(o_ref.dtype)

        if nk == 1:
            finish(part)
        else:
            acc_ref = refs[pos + 1]
            k = pl.program_id(kax)

            @pl.when(k == 0)
            def _():
                acc_ref[...] = part

            @pl.when(k > 0)
            def _():
                acc_ref[...] += part

            @pl.when(k == nk - 1)
            def _():
                finish(acc_ref[...])

    operands, in_specs = [], []
    for a, a_spec, b, b_spec in pairs:
        operands += [a, b]
        in_specs += [a_spec, b_spec]
    if res is not None:
        operands.append(res)
        in_specs.append(res_spec)
    (out,), ride_results = _call(name, body, grid, in_specs, [out_spec], [out_shape], operands,
                                 scratch=[pltpu.VMEM(acc_shape, F32)] if nk > 1 else [], rides=rides)
    return (out, ride_results) if rides else out


def _cast_bf16(x, name):
    rows, cols = x.shape
    tm = _rows_tile(rows, cols * 4)

    def body(x_ref, o_ref):
        o_ref[...] = x_ref[...].astype(BF16)

    spec = pl.BlockSpec((tm, cols), lambda i: (i, 0))
    return pl.pallas_call(body, grid=(rows // tm,), in_specs=[spec], out_specs=spec,
                          out_shape=jax.ShapeDtypeStruct(x.shape, BF16), compiler_params=_params(1), name=name)(x)


def _rms_fwd(x, gain, name):
    rows, d = x.shape
    tm = min(rows, 512)

    def body(x_ref, g_ref, o_ref):
        xv = x_ref[...]
        r = lax.rsqrt(jnp.mean(xv * xv, axis=-1, keepdims=True) + EPS)
        o_ref[...] = (xv * r * g_ref[...]).astype(BF16)

    spec = pl.BlockSpec((tm, d), lambda i: (i, 0))
    return pl.pallas_call(body, grid=(rows // tm,), in_specs=[spec, pl.BlockSpec((1, d), lambda i: (0, 0))],
                          out_specs=spec, out_shape=jax.ShapeDtypeStruct(x.shape, BF16),
                          compiler_params=_params(1), name=name)(x, gain)


def _rms_bwd(x, gain, dxn, dres, name, rides=()):
    rows, d = x.shape
    tm = min(rows, 256)
    with_res = dres is not None

    def body(*refs):
        if with_res:
            x_ref, g_ref, dy_ref, r_ref, dx_ref, dxb_ref, dg_ref = refs
        else:
            x_ref, g_ref, dy_ref, dx_ref, dxb_ref, dg_ref = refs
        xv = x_ref[...]
        r = lax.rsqrt(jnp.mean(xv * xv, axis=-1, keepdims=True) + EPS)
        xh = xv * r
        dy = dy_ref[...]
        dxh = dy * g_ref[...]
        dx = r * (dxh - xh * jnp.mean(dxh * xh, axis=-1, keepdims=True))
        if with_res:
            dx = dx + r_ref[...]
        dx_ref[...] = dx
        dxb_ref[...] = dx.astype(BF16)
        part = jnp.sum(dy * xh, axis=0, keepdims=True)

        @pl.when(pl.program_id(0) == 0)
        def _():
            dg_ref[...] = part

        @pl.when(pl.program_id(0) > 0)
        def _():
            dg_ref[...] += part

    spec = pl.BlockSpec((tm, d), lambda i: (i, 0))
    vec = pl.BlockSpec((1, d), lambda i: (0, 0))
    ops = [x, gain, dxn] + ([dres] if with_res else [])
    outs, ride_results = _call(
        name, body, (rows // tm,), [spec, vec, spec] + ([spec] if with_res else []), [spec, spec, vec],
        [jax.ShapeDtypeStruct(x.shape, F32), jax.ShapeDtypeStruct(x.shape, BF16), jax.ShapeDtypeStruct((1, d), F32)],
        ops, rides=rides)
    return (outs, ride_results) if rides else outs


def _loss_head(y, target, name):
    rows, d = y.shape
    tm = min(rows, 256)

    def body(y_ref, t_ref, dy_ref, dyb_ref, acc_ref):
        err = y_ref[...] - t_ref[...]
        dy = err * (1.0 / d)
        dy_ref[...] = dy
        dyb_ref[...] = dy.astype(BF16)
        part = jnp.zeros((8, 128), F32) + jnp.sum(err * err)

        @pl.when(pl.program_id(0) == 0)
        def _():
            acc_ref[...] = part

        @pl.when(pl.program_id(0) > 0)
        def _():
            acc_ref[...] += part

    spec = pl.BlockSpec((tm, d), lambda i: (i, 0))
    return pl.pallas_call(
        body, grid=(rows // tm,), in_specs=[spec, spec],
        out_specs=[spec, spec, pl.BlockSpec((8, 128), lambda i: (0, 0))],
        out_shape=[jax.ShapeDtypeStruct(y.shape, F32), jax.ShapeDtypeStruct(y.shape, BF16),
                   jax.ShapeDtypeStruct((8, 128), F32)],
        compiler_params=_params(1), name=name)(y, target)


ROW_CHUNK = 256


def _ffn_up(xn, wg, wu, tag, rides=()):
    t, d = xn.shape
    nd, fs, _ = wg.shape
    tm = min(t, 512)
    rc = min(tm, ROW_CHUNK)

    def body(x_ref, wg_ref, wu_ref, g_ref, u_ref, h_ref):
        for r in range(0, tm, rc):
            xv = x_ref[r:r + rc, :]
            g = _dot(xv, wg_ref[...], "nt")
            u = _dot(xv, wu_ref[...], "nt")
            g_ref[r:r + rc, :] = g.astype(BF16)
            u_ref[r:r + rc, :] = u.astype(BF16)
            h_ref[r:r + rc, :] = (g * _sigmoid(g) * u).astype(BF16)

    wspec = pl.BlockSpec((None, fs, d), lambda j, i: (j, 0, 0))
    hspec = pl.BlockSpec((None, tm, fs), lambda j, i: (j, i, 0))
    hid = jax.ShapeDtypeStruct((nd, t, fs), BF16)
    return _call(f"{tag}_up", body, (nd, t // tm), [pl.BlockSpec((tm, d), lambda j, i: (i, 0)), wspec, wspec],
                 [hspec] * 3, [hid] * 3, [xn, wg, wu], rides=rides)


def _ffn_contract(hid, w, name, res=None, alpha=1.0, rides=()):
    nd, t, fs = hid.shape
    d = w.shape[2]
    tm = min(t, 256)
    xspec = pl.BlockSpec((tm, d), lambda i, k: (i, 0))
    pairs = [(hid, pl.BlockSpec((None, tm, fs), lambda i, k, s=s: (s, i, 0)),
              w, pl.BlockSpec((None, fs, d), lambda i, k, s=s: (s, 0, 0), pipeline_mode=pl.Buffered(1)))
             for s in range(nd)]
    return _mm(name, pairs, "nn", (t // tm, 1), jax.ShapeDtypeStruct((t, d), F32), xspec,
               res=res, res_spec=xspec if res is not None else None, alpha=alpha, rides=rides)


def _ffn_down(x, h, wd, tag, rides=()):
    return _ffn_contract(h, wd, f"{tag}_down", res=x, alpha=0.5, rides=rides)


def _ffn_dact(dyb, wd, g, u, tag, rides=()):
    nd, t, fs = g.shape
    d = dyb.shape[1]
    tm = min(t, 512)
    rc = min(tm, ROW_CHUNK)

    def body(dy_ref, wd_ref, g_ref, u_ref, dg_ref, du_ref):
        for r in range(0, tm, rc):
            dh = 0.5 * _dot(dy_ref[r:r + rc, :], wd_ref[...], "nt")
            gv = g_ref[r:r + rc, :].astype(F32)
            uv = u_ref[r:r + rc, :].astype(F32)
            sig = _sigmoid(gv)
            du_ref[r:r + rc, :] = (dh * gv * sig).astype(BF16)
            dg_ref[r:r + rc, :] = (dh * uv * sig * (1.0 + gv * (1.0 - sig))).astype(BF16)

    hspec = pl.BlockSpec((None, tm, fs), lambda j, i: (j, i, 0))
    hid = jax.ShapeDtypeStruct((nd, t, fs), BF16)
    return _call(f"{tag}_dact", body, (nd, t // tm),
                 [pl.BlockSpec((tm, d), lambda j, i: (i, 0)), pl.BlockSpec((None, fs, d), lambda j, i: (j, 0, 0)),
                  hspec, hspec], [hspec] * 2, [hid] * 2, [dyb, wd, g, u], rides=rides)


def _ffn_dw(hid, act, alpha, name, rides=()):
    nd, t, fs = hid.shape
    d = act.shape[1]
    tk = t
    return _mm(name, [(hid, pl.BlockSpec((None, tk, fs), lambda j, k: (j, k, 0)),
                       act, pl.BlockSpec((tk, d), lambda j, k: (k, 0), pipeline_mode=pl.Buffered(1)))],
               "tn", (nd, t // tk),
               jax.ShapeDtypeStruct((nd, fs, d), BF16), pl.BlockSpec((None, fs, d), lambda j, k: (j, 0, 0)),
               alpha=alpha, rides=rides)


def _head_norm(x, gain):
    r = lax.rsqrt(jnp.mean(x * x, axis=-1, keepdims=True) + EPS)
    return x * r * gain


def _head_norm_bwd(x, gain, dy):
    r = lax.rsqrt(jnp.mean(x * x, axis=-1, keepdims=True) + EPS)
    xh = x * r
    dxh = dy * gain
    dx = r * (dxh - xh * jnp.mean(dxh * xh, axis=-1, keepdims=True))
    return dx, jnp.sum(dy * xh, axis=0, keepdims=True)


def _hs(h, base=0):
    return slice(base + h * HEAD, base + (h + 1) * HEAD)


def _tri(n, lower):
    r = lax.broadcasted_iota(jnp.int32, (n, n), 0)
    c = lax.broadcasted_iota(jnp.int32, (n, n), 1)
    return ((r >= c) if lower else (r <= c)).astype(F32)


def _attn_pre(proj, sp, name):
    t = proj.shape[0]
    tm = min(t, 256)

    def body(p_ref, sp_ref, fq, fk, fv, sq, sk, sv, mq, cc, carry):
        @pl.when(pl.program_id(0) == 0)
        def _():
            carry[...] = jnp.zeros_like(carry)

        for h in range(FOX_H):
            fq[:, _hs(h)] = _head_norm(p_ref[:, _hs(h, C_FQ)], sp_ref[R_FQ:R_FQ + 1, :]).astype(BF16)
            fk[:, _hs(h)] = _head_norm(p_ref[:, _hs(h, C_FK)], sp_ref[R_FK:R_FK + 1, :]).astype(BF16)
        fv[...] = p_ref[:, C_FV:C_FV + FOX_W].astype(BF16)
        for h in range(SWA_H):
            sq[:, _hs(h)] = _head_norm(p_ref[:, _hs(h, C_SQ)], sp_ref[R_SQ:R_SQ + 1, :]).astype(BF16)
        for h in range(SWA_KV):
            sk[:, _hs(h)] = _head_norm(p_ref[:, _hs(h, C_SK)], sp_ref[R_SK:R_SK + 1, :]).astype(BF16)
        sv[...] = p_ref[:, C_SV:C_SV + SWA_KV_W].astype(BF16)
        for h in range(MEM_H):
            mq[:, _hs(h)] = _head_norm(p_ref[:, _hs(h, C_MQ)], sp_ref[R_MQ:R_MQ + 1, :]).astype(BF16)
        z = p_ref[:, C_FL:C_FL + HEAD] + sp_ref[R_FB:R_FB + 1, :]
        lane = lax.broadcasted_iota(jnp.int32, z.shape, 1)
        log_f = jnp.minimum(z, 0.0) - jnp.log(1.0 + jnp.exp(-jnp.abs(z)))
        log_f = jnp.where(lane < N_LOGIT, log_f, 0.0)
        c = jnp.dot(_tri(tm, True), log_f, precision=lax.Precision.HIGHEST, preferred_element_type=F32)
        c = c + carry[0:1, :]
        cc[...] = c
        carry[...] = jnp.broadcast_to(c[tm - 1:tm, :], carry.shape)

    def rows(w):
        return pl.BlockSpec((tm, w), lambda i: (i, 0))

    def shape(w, dt):
        return jax.ShapeDtypeStruct((t, w), dt)

    widths = [FOX_W, FOX_W, FOX_W, SWA_W, SWA_KV_W, SWA_KV_W, MEM_W]
    return pl.pallas_call(
        body, grid=(t // tm,), in_specs=[rows(IN_W), pl.BlockSpec((16, 128), lambda i: (0, 0))],
        out_specs=[rows(w) for w in widths] + [rows(HEAD)],
        out_shape=[shape(w, BF16) for w in widths] + [shape(HEAD, F32)],
        scratch_shapes=[pltpu.VMEM((8, 128), F32)], compiler_params=_params(1), name=name)(proj, sp)


def _attn_post_bwd(proj, sp, dfq, dfk, dfv, dsq, dsk, dsv, dmq, dc_col, dc_row_t, name):
    t = proj.shape[0]
    tm = min(t, 256)
    nb = t // tm

    def body(p_ref, sp_ref, dfq_r, dfk_r, dfv_r, dsq_r, dsk_r, dsv_r, dmq_r, dcc_r, dcr_r, dp_ref, dsp_ref, carry):
        @pl.when(pl.program_id(0) == 0)
        def _():
            carry[...] = jnp.zeros_like(carry)
            dsp_ref[...] = jnp.zeros_like(dsp_ref)

        def group(n_heads, col, row, d_ref):
            total = None
            for h in range(n_heads):
                dx, dg = _head_norm_bwd(p_ref[:, _hs(h, col)], sp_ref[row:row + 1, :], d_ref[:, _hs(h)])
                dp_ref[:, _hs(h, col)] = dx.astype(BF16)
                total = dg if total is None else total + dg
            dsp_ref[row:row + 1, :] += total

        group(FOX_H, C_FQ, R_FQ, dfq_r)
        group(FOX_H, C_FK, R_FK, dfk_r)
        dp_ref[:, C_FV:C_FV + FOX_W] = dfv_r[...].astype(BF16)
        group(SWA_H, C_SQ, R_SQ, dsq_r)
        group(SWA_KV, C_SK, R_SK, dsk_r)
        dp_ref[:, C_SV:C_SV + SWA_KV_W] = dsv_r[...].astype(BF16)
        group(MEM_H, C_MQ, R_MQ, dmq_r)
        dc = dcc_r[...] - dcr_r[...]
        rc = jnp.dot(_tri(tm, False), dc, precision=lax.Precision.HIGHEST, preferred_element_type=F32)
        rc = rc + carry[0:1, :]
        carry[...] = jnp.broadcast_to(rc[0:1, :], carry.shape)
        z = p_ref[:, C_FL:C_FL + HEAD] + sp_ref[R_FB:R_FB + 1, :]
        dz = rc * _sigmoid(-z)
        dp_ref[:, C_FL:C_FL + HEAD] = dz.astype(BF16)
        dsp_ref[R_FB:R_FB + 1, :] += jnp.sum(dz, axis=0, keepdims=True)

    def rows(w):
        return pl.BlockSpec((tm, w), lambda i: (nb - 1 - i, 0))

    small = pl.BlockSpec((16, 128), lambda i: (0, 0))
    widths = [FOX_W, FOX_W, FOX_W, SWA_W, SWA_KV_W, SWA_KV_W, MEM_W, HEAD, HEAD]
    return pl.pallas_call(
        body, grid=(nb,), in_specs=[rows(IN_W), small] + [rows(w) for w in widths],
        out_specs=[rows(IN_W), small],
        out_shape=[jax.ShapeDtypeStruct((t, IN_W), BF16), jax.ShapeDtypeStruct((16, 128), F32)],
        scratch_shapes=[pltpu.VMEM((8, 128), F32)], compiler_params=_params(1), name=name,
    )(proj, sp, dfq, dfk, dfv, dsq, dsk, dsv, dmq, dc_col, dc_row_t)


def _head_column(values):
    rows = values[0].shape[0]
    lane = lax.broadcasted_iota(jnp.int32, (rows, HEAD), 1)
    out = jnp.zeros((rows, HEAD), F32)
    for h, v in enumerate(values):
        out = jnp.where(lane == h, v, out)
    return out


def _head_row(values, n_rows=8):
    cols = values[0].shape[1]
    sub = lax.broadcasted_iota(jnp.int32, (n_rows, cols), 0)
    out = jnp.zeros((n_rows, cols), F32)
    for h, v in enumerate(values):
        out = jnp.where(sub == h, v, out)
    return out


def _delta(dmixed, o_a, o_b, o_c, name):
    t = dmixed.shape[0]
    tm = min(t, 512)

    def body(d_ref, a_ref, b_ref, c_ref, o_ref, rep_ref):
        cols = []
        for ref, n_heads, base in ((a_ref, FOX_H, 0), (b_ref, SWA_H, FOX_W), (c_ref, MEM_H, FOX_W + SWA_W)):
            for h in range(n_heads):
                cols.append(jnp.sum(d_ref[:, _hs(h, base)] * ref[:, _hs(h)], axis=-1, keepdims=True))
        o_ref[...] = _head_column(cols)
        for h in range(FOX_H):
            rep_ref[h] = jnp.broadcast_to(cols[h], (tm, HEAD))

    def rows(w):
        return pl.BlockSpec((tm, w), lambda i: (i, 0))

    return pl.pallas_call(body, grid=(t // tm,), in_specs=[rows(dmixed.shape[1]), rows(FOX_W), rows(SWA_W), rows(MEM_W)],
                          out_specs=[rows(HEAD), pl.BlockSpec((FOX_H, tm, HEAD), lambda i: (0, i, 0))],
                          out_shape=[jax.ShapeDtypeStruct((t, HEAD), F32), jax.ShapeDtypeStruct((FOX_H, t, HEAD), F32)],
                          compiler_params=_params(1), name=name)(dmixed, o_a, o_b, o_c)


def _fox_fwd(fq, fk, fv, c_rep, c_row, name, rides=()):
    t = fq.shape[0]
    tb = min(t, 512)
    nb = t // tb
    n_tiles = tb // HEAD

    def body(q_ref, k_ref, v_ref, cc_ref, cr_ref, o_ref, lse_ref, m_s, l_s, acc_s):
        qi, ki = pl.program_id(0), pl.program_id(1)

        @pl.when(ki == 0)
        def _():
            m_s[...] = jnp.full_like(m_s, NEG_INF)
            l_s[...] = jnp.zeros_like(l_s)
            acc_s[...] = jnp.zeros_like(acc_s)

        def step(diagonal):
            if diagonal:
                r = lax.broadcasted_iota(jnp.int32, (tb, HEAD), 0)
                c = lax.broadcasted_iota(jnp.int32, (tb, HEAD), 1)
            for h in range(FOX_H):
                s = _dot(q_ref[:, _hs(h)], k_ref[:, _hs(h)], "nt")
                cc = cc_ref[h]
                tiles, m_cur = [], None
                for j in range(n_tiles):
                    st = s[:, _hs(j)] * SCALE + cc - cr_ref[h:h + 1, _hs(j)]
                    if diagonal:
                        st = jnp.where(r >= c + j * HEAD, st, NEG_INF)
                    tiles.append(st)
                    m_cur = st if m_cur is None else jnp.maximum(m_cur, st)
                m_prev = m_s[h]
                m_new = jnp.maximum(m_prev, jnp.max(m_cur, axis=-1, keepdims=True))
                alpha = jnp.exp(m_prev - m_new)
                ps = [jnp.exp(st - m_new) for st in tiles]
                l_cur = ps[0]
                for p in ps[1:]:
                    l_cur = l_cur + p
                l_s[h] = alpha * l_s[h] + jnp.sum(l_cur, axis=-1, keepdims=True)
                p = jnp.concatenate([p.astype(BF16) for p in ps], axis=1)
                acc_s[:, _hs(h)] = alpha * acc_s[:, _hs(h)] + _dot(p, v_ref[:, _hs(h)])
                m_s[h] = m_new

        @pl.when(ki < qi)
        def _():
            step(False)

        @pl.when(ki == qi)
        def _():
            step(True)
            for h in range(FOX_H):
                o_ref[:, _hs(h)] = acc_s[:, _hs(h)] / l_s[h]
                lse_ref[h] = m_s[h] + jnp.log(l_s[h])

    qspec = pl.BlockSpec((tb, FOX_W), lambda i, j: (i, 0))
    kspec = pl.BlockSpec((tb, FOX_W), lambda i, j: (jnp.minimum(i, j), 0))
    rep = pl.BlockSpec((FOX_H, tb, HEAD), lambda i, j: (0, i, 0))
    return _call(
        name, body, (nb, nb),
        [qspec, kspec, kspec, rep, pl.BlockSpec((8, tb), lambda i, j: (0, jnp.minimum(i, j)))],
        [qspec, rep],
        [jax.ShapeDtypeStruct((t, FOX_W), F32), jax.ShapeDtypeStruct((FOX_H, t, HEAD), F32)],
        [fq, fk, fv, c_rep, c_row],
        scratch=[pltpu.VMEM((FOX_H, tb, HEAD), F32), pltpu.VMEM((FOX_H, tb, HEAD), F32), pltpu.VMEM((tb, FOX_W), F32)],
        rides=rides)


def _fox_bwd(fq, fk, fv, c_rep, c_row, dmixed, lse, delta, name, rides=()):
    t = fq.shape[0]
    tb = min(t, 512)
    nb = t // tb
    n_tiles = tb // HEAD

    def body(q_ref, k_ref, v_ref, cc_ref, cr_ref, do_ref, lse_ref, dl_ref,
             dq_ref, dk_ref, dv_ref, dcc_ref, dcr_ref):
        ki, qi = pl.program_id(0), pl.program_id(1)

        @pl.when((ki == 0) & (qi == 0))
        def _():
            dq_ref[...] = jnp.zeros_like(dq_ref)
            dcc_ref[...] = jnp.zeros_like(dcc_ref)

        @pl.when(qi == 0)
        def _():
            dk_ref[...] = jnp.zeros_like(dk_ref)
            dv_ref[...] = jnp.zeros_like(dv_ref)
            dcr_ref[...] = jnp.zeros_like(dcr_ref)

        def step(diagonal):
            rows = pl.ds(pl.multiple_of(qi * tb, tb), tb)
            if diagonal:
                r = lax.broadcasted_iota(jnp.int32, (tb, HEAD), 0)
                c = lax.broadcasted_iota(jnp.int32, (tb, HEAD), 1)
            row_sums, col_sums = [], []
            for h in range(FOX_H):
                q, k, v, do = q_ref[:, _hs(h)], k_ref[:, _hs(h)], v_ref[:, _hs(h)], do_ref[:, _hs(h)]
                s = _dot(q, k, "nt")
                dp = _dot(do, v, "nt")
                cc, lse_h, dl_h = cc_ref[h], lse_ref[h], dl_ref[h]
                ps, dss, row = [], [], None
                for j in range(n_tiles):
                    st = s[:, _hs(j)] * SCALE + cc - cr_ref[h:h + 1, _hs(j)]
                    if diagonal:
                        st = jnp.where(r >= c + j * HEAD, st, NEG_INF)
                    pt = jnp.exp(st - lse_h)
                    dst = pt * (dp[:, _hs(j)] - dl_h)
                    ps.append(pt.astype(BF16))
                    dss.append(dst)
                    row = dst if row is None else row + dst
                p = jnp.concatenate(ps, axis=1)
                ds = jnp.concatenate(dss, axis=1)
                dsb = ds.astype(BF16)
                dv_ref[:, _hs(h)] += _dot(p, do, "tn")
                dk_ref[:, _hs(h)] += _dot(dsb, q, "tn") * SCALE
                dq_ref[rows, _hs(h)] += _dot(dsb, k) * SCALE
                row_sums.append(jnp.sum(row, axis=1, keepdims=True))
                col_sums.append(jnp.sum(ds, axis=0, keepdims=True))
            dcc_ref[rows, :] += _head_column(row_sums)
            dcr_ref[...] += _head_row(col_sums)

        @pl.when(qi > ki)
        def _():
            step(False)

        @pl.when(qi == ki)
        def _():
            step(True)

    def qmap(j, i):
        return (jnp.maximum(i, j), 0)

    qspec = pl.BlockSpec((tb, FOX_W), qmap)
    kspec = pl.BlockSpec((tb, FOX_W), lambda j, i: (j, 0))
    rep = pl.BlockSpec((FOX_H, tb, HEAD), lambda j, i: (0, jnp.maximum(i, j), 0))
    rowspec = pl.BlockSpec((8, tb), lambda j, i: (0, j))
    return _call(
        name, body, (nb, nb), [qspec, kspec, kspec, rep, rowspec, qspec, rep, rep],
        [pl.BlockSpec((t, FOX_W), lambda j, i: (0, 0)), kspec, kspec,
         pl.BlockSpec((t, HEAD), lambda j, i: (0, 0)), rowspec],
        [jax.ShapeDtypeStruct((t, FOX_W), F32)] * 3 + [jax.ShapeDtypeStruct((t, HEAD), F32),
                                                       jax.ShapeDtypeStruct((8, t), F32)],
        [fq, fk, fv, c_rep, c_row, dmixed, lse, delta], rides=rides)


def _swa_logits(q, k_cur, k_prev, slope, first_block):
    w = SWA_BLOCK
    r = lax.broadcasted_iota(jnp.int32, (w, w), 0)
    j = lax.broadcasted_iota(jnp.int32, (w, w), 1)
    dist_cur = r - j
    dist_prev = w + r - j
    s_cur = _dot(q, k_cur, "nt") * SCALE - slope * dist_cur.astype(F32)
    s_cur = jnp.where(dist_cur >= 0, s_cur, NEG_INF)
    s_prev = _dot(q, k_prev, "nt") * SCALE - slope * dist_prev.astype(F32)
    s_prev = jnp.where((j > r) & jnp.logical_not(first_block), s_prev, NEG_INF)
    return s_cur, s_prev


def _slope(h):
    return float(2.0 ** (-8.0 * (h + 1) / SWA_H))


def _swa_fwd(sq, sk, sv, sp, name, rides=()):
    t = sq.shape[0]
    w = SWA_BLOCK
    nb = t // w
    group = SWA_H // SWA_KV

    def body(q_ref, kp_ref, kc_ref, vp_ref, vc_ref, sp_ref, o_ref, lse_ref):
        first = pl.program_id(0) == 0
        lses = []
        for h in range(SWA_H):
            kv = h // group
            s_cur, s_prev = _swa_logits(q_ref[:, _hs(h)], kc_ref[:, _hs(kv)], kp_ref[:, _hs(kv)], _slope(h), first)
            sink = sp_ref[R_SINK:R_SINK + 1, h:h + 1]
            m = jnp.maximum(jnp.maximum(jnp.max(s_cur, axis=-1, keepdims=True),
                                        jnp.max(s_prev, axis=-1, keepdims=True)), sink)
            p_cur = jnp.exp(s_cur - m)
            p_prev = jnp.exp(s_prev - m)
            l = jnp.sum(p_cur, axis=-1, keepdims=True) + jnp.sum(p_prev, axis=-1, keepdims=True) + jnp.exp(sink - m)
            o_ref[:, _hs(h)] = (_dot(p_cur, vc_ref[:, _hs(kv)]) + _dot(p_prev, vp_ref[:, _hs(kv)])) / l
            lses.append(m + jnp.log(l))
        lse_ref[...] = _head_column(lses)

    qspec = pl.BlockSpec((w, SWA_W), lambda n: (n, 0))
    cur = pl.BlockSpec((w, SWA_KV_W), lambda n: (n, 0))
    prev = pl.BlockSpec((w, SWA_KV_W), lambda n: (jnp.maximum(n - 1, 0), 0))
    return _call(
        name, body, (nb,), [qspec, prev, cur, prev, cur, pl.BlockSpec((16, 128), lambda n: (0, 0))],
        [qspec, pl.BlockSpec((w, HEAD), lambda n: (n, 0))],
        [jax.ShapeDtypeStruct((t, SWA_W), F32), jax.ShapeDtypeStruct((t, HEAD), F32)],
        [sq, sk, sk, sv, sv, sp], rides=rides)


def _swa_bwd(sq, sk, sv, sp, dmixed, lse, delta, name):
    t = sq.shape[0]
    w = SWA_BLOCK
    nb = t // w
    group = SWA_H // SWA_KV
    do_block = FOX_W // SWA_W
    assert FOX_W % SWA_W == 0

    def body(q_ref, kp_ref, kc_ref, vp_ref, vc_ref, sp_ref, do_ref, lse_ref, dl_ref,
             dq_ref, dk_ref, dv_ref, dsp_ref, ck, cv):
        step = pl.program_id(0)
        first = step == nb - 1

        @pl.when(step == 0)
        def _():
            ck[...] = jnp.zeros_like(ck)
            cv[...] = jnp.zeros_like(cv)
            dsp_ref[...] = jnp.zeros_like(dsp_ref)

        dk_cur = [None] * SWA_KV
        dk_prev = [None] * SWA_KV
        dv_cur = [None] * SWA_KV
        dv_prev = [None] * SWA_KV
        dsinks = []

        def add(lst, i, v):
            lst[i] = v if lst[i] is None else lst[i] + v

        for h in range(SWA_H):
            kv = h // group
            q, do = q_ref[:, _hs(h)], do_ref[:, _hs(h)]
            kc, kp, vc, vp = kc_ref[:, _hs(kv)], kp_ref[:, _hs(kv)], vc_ref[:, _hs(kv)], vp_ref[:, _hs(kv)]
            s_cur, s_prev = _swa_logits(q, kc, kp, _slope(h), first)
            lse_h = lse_ref[:, h:h + 1]
            dl_h = dl_ref[:, FOX_H + h:FOX_H + h + 1]
            p_cur = jnp.exp(s_cur - lse_h)
            p_prev = jnp.exp(s_prev - lse_h)
            p_sink = jnp.exp(sp_ref[R_SINK:R_SINK + 1, h:h + 1] - lse_h)
            ds_cur = p_cur * (_dot(do, vc, "nt") - dl_h)
            ds_prev = p_prev * (_dot(do, vp, "nt") - dl_h)
            dq_ref[:, _hs(h)] = (_dot(ds_cur, kc) + _dot(ds_prev, kp)) * SCALE
            add(dk_cur, kv, _dot(ds_cur, q, "tn") * SCALE)
            add(dk_prev, kv, _dot(ds_prev, q, "tn") * SCALE)
            add(dv_cur, kv, _dot(p_cur, do, "tn"))
            add(dv_prev, kv, _dot(p_prev, do, "tn"))
            dsinks.append(-jnp.sum(p_sink * dl_h, axis=0, keepdims=True))
        for kv in range(SWA_KV):
            dk_ref[:, _hs(kv)] = dk_cur[kv] + ck[:, _hs(kv)]
            dv_ref[:, _hs(kv)] = dv_cur[kv] + cv[:, _hs(kv)]
            ck[:, _hs(kv)] = dk_prev[kv]
            cv[:, _hs(kv)] = dv_prev[kv]
        lane = lax.broadcasted_iota(jnp.int32, (1, HEAD), 1)
        row = jnp.zeros((1, HEAD), F32)
        for h in range(SWA_H):
            row = jnp.where(lane == h, dsinks[h], row)
        dsp_ref[R_SINK:R_SINK + 1, :] += row

    def rev(n):
        return nb - 1 - n

    qspec = pl.BlockSpec((w, SWA_W), lambda n: (rev(n), 0))
    cur = pl.BlockSpec((w, SWA_KV_W), lambda n: (rev(n), 0))
    prev = pl.BlockSpec((w, SWA_KV_W), lambda n: (jnp.maximum(rev(n) - 1, 0), 0))
    col = pl.BlockSpec((w, HEAD), lambda n: (rev(n), 0))
    small = pl.BlockSpec((16, 128), lambda n: (0, 0))
    return pl.pallas_call(
        body, grid=(nb,),
        in_specs=[qspec, prev, cur, prev, cur, small, pl.BlockSpec((w, SWA_W), lambda n: (rev(n), do_block)), col, col],
        out_specs=[qspec, cur, cur, small],
        out_shape=[jax.ShapeDtypeStruct((t, SWA_W), F32), jax.ShapeDtypeStruct((t, SWA_KV_W), F32),
                   jax.ShapeDtypeStruct((t, SWA_KV_W), F32), jax.ShapeDtypeStruct((16, 128), F32)],
        scratch_shapes=[pltpu.VMEM((w, SWA_KV_W), F32), pltpu.VMEM((w, SWA_KV_W), F32)],
        compiler_params=_params(1), name=name)(sq, sk, sk, sv, sv, sp, dmixed, lse, delta)


def _mem_pre(mkv, sp, name):
    m = mkv.shape[0]

    def body(x_ref, sp_ref, k_ref, v_ref):
        for h in range(MEM_H):
            k_ref[:, _hs(h)] = _head_norm(x_ref[:, _hs(h)], sp_ref[R_MK:R_MK + 1, :]).astype(BF16)
        v_ref[...] = x_ref[:, MEM_W:2 * MEM_W].astype(BF16)

    out = jax.ShapeDtypeStruct((m, MEM_W), BF16)
    return pl.pallas_call(body, out_shape=[out, out], name=name)(mkv, sp)


def _mem_post_bwd(mkv, sp, dmk, dmv, name):
    m = mkv.shape[0]

    def body(x_ref, sp_ref, dk_ref, dv_ref, d_ref, dsp_ref):
        dsp_ref[...] = jnp.zeros_like(dsp_ref)
        total = None
        for h in range(MEM_H):
            dx, dg = _head_norm_bwd(x_ref[:, _hs(h)], sp_ref[R_MK:R_MK + 1, :], dk_ref[:, _hs(h)])
            d_ref[:, _hs(h)] = dx.astype(BF16)
            total = dg if total is None else total + dg
        d_ref[:, MEM_W:2 * MEM_W] = dv_ref[...].astype(BF16)
        dsp_ref[R_MK:R_MK + 1, :] = total

    return pl.pallas_call(body, out_shape=[jax.ShapeDtypeStruct((m, 2 * MEM_W), BF16),
                                           jax.ShapeDtypeStruct((16, 128), F32)], name=name)(mkv, sp, dmk, dmv)


def _mem_fwd(mq, mk, mv, name):
    t = mq.shape[0]
    m = mk.shape[0]
    tq = min(t, 512)

    def body(q_ref, k_ref, v_ref, o_ref, lse_ref):
        lses = []
        for h in range(MEM_H):
            s = _dot(q_ref[:, _hs(h)], k_ref[:, _hs(h)], "nt") * SCALE
            mx = jnp.max(s, axis=-1, keepdims=True)
            p = jnp.exp(s - mx)
            l = jnp.sum(p, axis=-1, keepdims=True)
            o_ref[:, _hs(h)] = _dot(p, v_ref[:, _hs(h)]) / l
            lses.append(mx + jnp.log(l))
        lse_ref[...] = _head_column(lses)

    qspec = pl.BlockSpec((tq, MEM_W), lambda i: (i, 0))
    kspec = pl.BlockSpec((m, MEM_W), lambda i: (0, 0))
    return pl.pallas_call(
        body, grid=(t // tq,), in_specs=[qspec, kspec, kspec],
        out_specs=[qspec, pl.BlockSpec((tq, HEAD), lambda i: (i, 0))],
        out_shape=[jax.ShapeDtypeStruct((t, MEM_W), F32), jax.ShapeDtypeStruct((t, HEAD), F32)],
        compiler_params=_params(1), name=name)(mq, mk, mv)


def _mem_bwd(mq, mk, mv, dmixed, lse, delta, name):
    t = mq.shape[0]
    m = mk.shape[0]
    tq = min(t, 512)
    do_block = (FOX_W + SWA_W) // MEM_W
    assert (FOX_W + SWA_W) % MEM_W == 0

    def body(q_ref, k_ref, v_ref, do_ref, lse_ref, dl_ref, dq_ref, dk_ref, dv_ref):
        @pl.when(pl.program_id(0) == 0)
        def _():
            dk_ref[...] = jnp.zeros_like(dk_ref)
            dv_ref[...] = jnp.zeros_like(dv_ref)

        for h in range(MEM_H):
            q, k, v, do = q_ref[:, _hs(h)], k_ref[:, _hs(h)], v_ref[:, _hs(h)], do_ref[:, _hs(h)]
            s = _dot(q, k, "nt") * SCALE
            p = jnp.exp(s - lse_ref[:, h:h + 1])
            col = FOX_H + SWA_H + h
            ds = p * (_dot(do, v, "nt") - dl_ref[:, col:col + 1])
            dq_ref[:, _hs(h)] = _dot(ds, k) * SCALE
            dk_ref[:, _hs(h)] += _dot(ds, q, "tn") * SCALE
            dv_ref[:, _hs(h)] += _dot(p, do, "tn")

    qspec = pl.BlockSpec((tq, MEM_W), lambda i: (i, 0))
    kspec = pl.BlockSpec((m, MEM_W), lambda i: (0, 0))
    col = pl.BlockSpec((tq, HEAD), lambda i: (i, 0))
    return pl.pallas_call(
        body, grid=(t // tq,),
        in_specs=[qspec, kspec, kspec, pl.BlockSpec((tq, MEM_W), lambda i: (i, do_block)), col, col],
        out_specs=[qspec, kspec, kspec],
        out_shape=[jax.ShapeDtypeStruct((t, MEM_W), F32), jax.ShapeDtypeStruct((m, MEM_W), F32),
                   jax.ShapeDtypeStruct((m, MEM_W), F32)],
        compiler_params=_params(1), name=name)(mq, mk, mv, dmixed, lse, delta)


def _all_gather(xs, name):
    n = len(xs)

    def body(*refs):
        x_refs, o_refs = refs[:n], refs[n:2 * n]
        send_sems, recv_sems, local_sems = refs[2 * n:]
        x, y, c = _me()
        me, sibling = (x, y, c), (x, y, 1 - c)
        chips = [(1 - x, y), (x, 1 - y), (1 - x, 1 - y)]

        def copy(a, k, block, to, src=None):
            slot = o_refs[a].at[_lin(block)]
            return pltpu.make_async_remote_copy(
                src_ref=slot if src is None else src, dst_ref=slot, send_sem=send_sems.at[a, k],
                recv_sem=recv_sems.at[a, k], device_id=to, device_id_type=MESH)

        mine = [pltpu.make_async_copy(x_refs[a], o_refs[a].at[_lin(me)], local_sems.at[a]) for a in range(n)]
        for cp in mine:
            cp.start()
        first = []
        for a in range(n):
            first.append(copy(a, 0, me, sibling, src=x_refs[a]))
            first += [copy(a, 1 + j, me, (*chip, c), src=x_refs[a]) for j, chip in enumerate(chips)]
        for cp in first:
            cp.start()
        passed = []
        for j, chip in enumerate(chips):
            for a in range(n):
                copy(a, 1 + j, (*chip, c), me).wait_recv()
                cp = copy(a, 4 + j, (*chip, c), sibling)
                cp.start()
                passed.append(cp)
        for a in range(n):
            copy(a, 0, sibling, me).wait_recv()
            for j, chip in enumerate(chips):
                copy(a, 4 + j, (*chip, 1 - c), me).wait_recv()
        for cp in first + passed:
            cp.wait_send()
        for cp in mine:
            cp.wait()

    return pl.pallas_call(
        body, in_specs=[ANY] * n, out_specs=[ANY] * n,
        out_shape=[jax.ShapeDtypeStruct((N_DEV,) + x.shape, x.dtype) for x in xs],
        scratch_shapes=[pltpu.SemaphoreType.DMA((n, 7)), pltpu.SemaphoreType.DMA((n, 7)),
                        pltpu.SemaphoreType.DMA((n,))],
        name=name)(*xs)


def _peers():
    x, y, c = _me()
    out = []
    for k in range(1, N_DEV):
        kx, ky, kc = (k >> 2) & 1, (k >> 1) & 1, k & 1
        out.append(((1 - x) if kx else x, (1 - y) if ky else y, (1 - c) if kc else c))
    return out


def _all_reduce_small(xs, name):
    n = len(xs)

    def body(*refs):
        x_refs, o_refs = refs[:n], refs[n:2 * n]
        bufs = refs[2 * n:3 * n]
        send_sems, recv_sems = refs[3 * n:]
        me = _lin(_me())
        peers = _peers()
        for a in range(n):
            bufs[a][me] = x_refs[a][...]
        sends = []
        for a in range(n):
            for k, peer in enumerate(peers):
                sends.append(pltpu.make_async_remote_copy(
                    src_ref=bufs[a].at[me], dst_ref=bufs[a].at[me], send_sem=send_sems.at[a, k],
                    recv_sem=recv_sems.at[a, k], device_id=peer, device_id_type=MESH))
        for cp in sends:
            cp.start()
        for a in range(n):
            for k, peer in enumerate(peers):
                pltpu.make_async_remote_copy(
                    src_ref=bufs[a].at[me], dst_ref=bufs[a].at[_lin(peer)], send_sem=send_sems.at[a, k],
                    recv_sem=recv_sems.at[a, k], device_id=peer, device_id_type=MESH).wait_recv()
        for cp in sends:
            cp.wait_send()
        for a in range(n):
            total = bufs[a][0]
            for q in range(1, N_DEV):
                total = total + bufs[a][q]
            o_refs[a][...] = total

    vmem = pl.BlockSpec(memory_space=pltpu.VMEM)
    return pl.pallas_call(
        body, in_specs=[vmem] * n, out_specs=[vmem] * n,
        out_shape=[jax.ShapeDtypeStruct(x.shape, F32) for x in xs],
        scratch_shapes=[pltpu.VMEM((N_DEV,) + x.shape, F32) for x in xs]
        + [pltpu.SemaphoreType.DMA((n, 7)), pltpu.SemaphoreType.DMA((n, 7))],
        name=name)(*xs)


def _pair_add(part, got, name):
    _, rows, cols = part.shape
    tm = _rows_tile(rows, cols * 2, budget=2 << 20)
    core = jnp.reshape(lax.axis_index("c"), (1,)).astype(jnp.int32)

    def body(c_ref, p_ref, g_ref, o_ref):
        o_ref[...] = (p_ref[...].astype(F32) + g_ref[...].astype(F32)).astype(BF16)

    spec = pl.BlockSpec((None, tm, cols), lambda q, i, c: (q, i, 0))
    grid_spec = pltpu.PrefetchScalarGridSpec(
        num_scalar_prefetch=1, grid=(4, rows // tm),
        in_specs=[pl.BlockSpec((None, tm, cols), lambda q, i, c: (2 * q + c[0], i, 0)), spec], out_specs=spec)
    return pl.pallas_call(body, grid_spec=grid_spec, out_shape=jax.ShapeDtypeStruct((4, rows, cols), BF16),
                          compiler_params=_params(2), name=name)(core, part, got)


def _adam_math(w, g, m, v):
    nm = ADAM_B1 * m + (1.0 - ADAM_B1) * g
    nv = ADAM_B2 * v + (1.0 - ADAM_B2) * (g * g)
    m_hat = nm / (1.0 - ADAM_B1 ** ADAM_STEP)
    v_hat = nv / (1.0 - ADAM_B2 ** ADAM_STEP)
    return -ADAM_LR * (m_hat / (jnp.sqrt(v_hat) + ADAM_EPS) + ADAM_WD * w), nm, nv


def _sum_chips(got, name):
    _, rows, cols = got.shape
    tm = _rows_tile(rows, cols * 2 * 4, budget=2 << 20)

    def body(r_ref, o_ref):
        o_ref[...] = ((r_ref[0].astype(F32) + r_ref[1].astype(F32)) + r_ref[2].astype(F32)) + r_ref[3].astype(F32)

    return pl.pallas_call(
        body, grid=(rows // tm,), in_specs=[pl.BlockSpec((4, tm, cols), lambda i: (0, i, 0))],
        out_specs=pl.BlockSpec((tm, cols), lambda i: (i, 0)), out_shape=jax.ShapeDtypeStruct((rows, cols), F32),
        compiler_params=_params(1), name=name)(got)


def _sum_adamw(got, col_block, w, m, v, name):
    _, rows, cols = w.shape
    tm = _rows_tile(rows, cols * 4, budget=1 << 20)

    def body(r_ref, w_ref, m_ref, v_ref, g_ref, d_ref, nm_ref, nv_ref):
        g = ((r_ref[0].astype(F32) + r_ref[1].astype(F32)) + r_ref[2].astype(F32)) + r_ref[3].astype(F32)
        g_ref[...] = g
        d_ref[...], nm_ref[...], nv_ref[...] = _adam_math(w_ref[...], g, m_ref[...], v_ref[...])

    spec = pl.BlockSpec((None, tm, cols), lambda i: (0, i, 0))
    out = jax.ShapeDtypeStruct(w.shape, F32)
    return pl.pallas_call(
        body, grid=(rows // tm,), in_specs=[pl.BlockSpec((4, tm, cols), lambda i: (0, i, col_block)), spec, spec, spec],
        out_specs=[spec] * 4, out_shape=[out] * 4, compiler_params=_params(1), name=name)(got, w, m, v)


def _adamw(w, g, m, v, name):
    rows, cols = w.shape
    tm = _rows_tile(rows, cols * 4, budget=2 << 20, mult=8)

    def body(w_ref, g_ref, m_ref, v_ref, d_ref, nm_ref, nv_ref):
        d_ref[...], nm_ref[...], nv_ref[...] = _adam_math(w_ref[...], g_ref[...], m_ref[...], v_ref[...])

    spec = pl.BlockSpec((tm, cols), lambda i: (i, 0))
    out = jax.ShapeDtypeStruct(w.shape, F32)
    return pl.pallas_call(body, grid=(rows // tm,), in_specs=[spec] * 4, out_specs=[spec] * 3,
                          out_shape=[out] * 3, compiler_params=_params(1), name=name)(w, g, m, v)


def _permute_in(w):
    logit0 = 3 * FOX_W
    pad = jnp.zeros(w.shape[:-1] + (HEAD - N_LOGIT,), w.dtype)
    return jnp.concatenate([w[..., :logit0], w[..., logit0 + N_LOGIT:], w[..., logit0:logit0 + N_LOGIT], pad], axis=-1)


def _unpermute_in(w):
    logit0 = 3 * FOX_W
    return jnp.concatenate([w[..., :logit0], w[..., C_FL:C_FL + N_LOGIT], w[..., logit0:C_FL]], axis=-1)


def _pad_row(v, width):
    return jnp.pad(v, ((0, 0), (0, width - v.shape[1])))


def _pack_small(fq, fk, sq, sk, mq, mk, fb, sinks):
    rows = [fq, fk, sq, sk, mq, mk, _pad_row(fb, HEAD), _pad_row(sinks, HEAD)]
    return jnp.concatenate(rows + [jnp.zeros((8, HEAD), F32)], axis=0)


def _pack_norms(a, b, c, d):
    return jnp.concatenate([a, b, c, d, jnp.zeros((4, a.shape[1]), F32)], axis=0)


def kernel(x, mem, ffn1_norm, ffn1_gate, ffn1_up, ffn1_down, mix_norm, mem_norm, w_in, forget_bias, w_mem_k, w_mem_v, fox_q_gain, fox_k_gain, swa_q_gain, swa_k_gain, swa_sinks, mem_q_gain, mem_k_gain, w_out, ffn2_norm, ffn2_gate, ffn2_up, ffn2_down, loss_target, m_ffn1_norm, m_ffn1_gate, m_ffn1_up, m_ffn1_down, m_mix_norm, m_mem_norm, m_w_in, m_forget_bias, m_w_mem_k, m_w_mem_v, m_fox_q_gain, m_fox_k_gain, m_swa_q_gain, m_swa_k_gain, m_swa_sinks, m_mem_q_gain, m_mem_k_gain, m_w_out, m_ffn2_norm, m_ffn2_gate, m_ffn2_up, m_ffn2_down, v_ffn1_norm, v_ffn1_gate, v_ffn1_up, v_ffn1_down, v_mix_norm, v_mem_norm, v_w_in, v_forget_bias, v_w_mem_k, v_w_mem_v, v_fox_q_gain, v_fox_k_gain, v_swa_q_gain, v_swa_k_gain, v_swa_sinks, v_mem_q_gain, v_mem_k_gain, v_w_out, v_ffn2_norm, v_ffn2_gate, v_ffn2_up, v_ffn2_down):
    x0 = x[0]
    mem0 = mem[0]
    target = loss_target[0]
    t, d = x0.shape
    d_shard = w_in.shape[1]
    m_len = mem0.shape[0]
    tm = min(t, 512)
    tk = min(t, 512)
    tn = IN_W // 3
    tkw, tnw = min(t, 1024), IN_W // 3

    def swap(a):
        return jnp.swapaxes(a, 1, 2)

    gate1, up1, gate2, up2 = swap(ffn1_gate), swap(ffn1_up), swap(ffn2_gate), swap(ffn2_up)

    local = {
        "g1": gate1[0], "u1": up1[0], "d1": ffn1_down[0],
        "g2": gate2[0], "u2": up2[0], "d2": ffn2_down[0],
        "in": _permute_in(w_in[0]), "out": w_out[0],
        "mkv": jnp.concatenate([w_mem_k[0], w_mem_v[0]], axis=1),
    }
    shard = {k: _cast_bf16(v, f"cast_{k}") for k, v in local.items()}
    sp = _pack_small(fox_q_gain, fox_k_gain, swa_q_gain, swa_k_gain, mem_q_gain, mem_k_gain, forget_bias, swa_sinks)
    wt = {}

    wt["g1"], wt["u1"] = _all_gather([shard["g1"], shard["u1"]], "gather_ffn1_in")
    xn1 = _rms_fwd(x0, ffn1_norm, "ffn1_norm")
    (g1, u1, h1), ((wt["d1"], wt["in"]),) = _ffn_up(
        xn1, wt["g1"], wt["u1"], "ffn1", rides=[_ride_gather([shard["d1"], shard["in"]], 0.87)])
    x1, ((wt["out"], wt["mkv"]),) = _ffn_down(
        x0, h1, wt["d1"], "ffn1", rides=[_ride_gather([shard["out"], shard["mkv"]], 0.6)])
    w_in_full = wt["in"].reshape(d, IN_W)

    hn = _rms_fwd(x1, mix_norm, "mix_norm")
    proj, (half,) = _mm(
        "proj", [(hn, pl.BlockSpec((tm, d), lambda n, i, k: (i, 0)),
                  w_in_full, pl.BlockSpec((d, tn), lambda n, i, k: (0, n)))],
        "nn", (3, t // tm, 1), jax.ShapeDtypeStruct((t, IN_W), F32), pl.BlockSpec((tm, tn), lambda n, i, k: (i, n)),
        rides=[_ride_gather_chips([shard["g2"]])])
    w_out_full = wt["out"].reshape(d, d)
    w_mkv_full = wt["mkv"].reshape(d, 2 * MEM_W)
    fq, fk, fv, sq, sk, sv, mq, c_col = _attn_pre(proj, sp, "attn_pre")
    c_row = jnp.transpose(c_col[:, :8])
    c_rep = jnp.broadcast_to(c_row[:FOX_H, :, None], (FOX_H, t, HEAD))

    mn = _rms_fwd(mem0, mem_norm, "mem_norm")
    mkv = _mm("mem_kv", [(mn, pl.BlockSpec((m_len, d), lambda k: (0, 0)),
                          w_mkv_full, pl.BlockSpec((d, 2 * MEM_W), lambda k: (0, 0)))],
              "nn", (1,), jax.ShapeDtypeStruct((m_len, 2 * MEM_W), F32),
              pl.BlockSpec((m_len, 2 * MEM_W), lambda k: (0, 0)))
    mk, mv = _mem_pre(mkv, sp, "mem_pre")

    (o_a, lse_a), ((wt["g2"],), half) = _fox_fwd(
        fq, fk, fv, c_rep, c_row, "fox_fwd", rides=[_ride_gather_sibling(half), _ride_gather_chips([shard["u2"]])])
    (o_b, lse_b), ((wt["u2"],),) = _swa_fwd(sq, sk, sv, sp, "swa_fwd", rides=[_ride_gather_sibling(half)])
    o_c, lse_c = _mem_fwd(mq, mk, mv, "mem_fwd")

    def rows_spec(width):
        return pl.BlockSpec((tm, width), lambda i, k: (i, 0))

    def wout_rows(first, width):
        assert first % width == 0
        return pl.BlockSpec((width, d), lambda i, k: (first // width, 0))

    xspec = pl.BlockSpec((tm, d), lambda i, k: (i, 0))
    x2 = _mm(
        "mix_out",
        [(o_a, rows_spec(FOX_W), w_out_full, wout_rows(0, FOX_W)),
         (o_b, rows_spec(SWA_W), w_out_full, wout_rows(FOX_W, SWA_W)),
         (o_c, rows_spec(MEM_W), w_out_full, wout_rows(FOX_W + SWA_W, MEM_W))],
        "nn", (t // tm, 1), jax.ShapeDtypeStruct((t, d), F32), xspec, res=x1, res_spec=xspec)

    xn2 = _rms_fwd(x2, ffn2_norm, "ffn2_norm")
    (g2, u2, h2), ((wt["d2"],),) = _ffn_up(xn2, wt["g2"], wt["u2"], "ffn2", rides=[_ride_gather([shard["d2"]], 0.75)])
    x3 = _ffn_down(x2, h2, wt["d2"], "ffn2")

    dy, dyb, sq_err = _loss_head(x3, target, "loss_head")
    loss = lax.psum(0.5 * sq_err[0, 0] / d, ("x", "y", "c"))

    got = {}
    paired = {}
    landed = {}

    def pair(k, part):
        paired[k] = _pair_add(part, got[k], f"pair_{k}")

    (dg2, du2), _ = _ffn_dact(dyb, wt["d2"], g2, u2, "ffn2")
    part_d2 = _ffn_dw(h2, dyb, 0.5, "ffn2_dwd")
    part_g2, ((got["d2"],),) = _ffn_dw(dg2, xn2, 1.0, "ffn2_dwg", rides=[_ride_scatter_sibling([part_d2])])
    pair("d2", part_d2)
    half_rows = part_d2.shape[1] // 2
    first, second = (0, half_rows), (half_rows, half_rows)
    part_u2, (half, (got["g2"],)) = _ffn_dw(
        du2, xn2, 1.0, "ffn2_dwu",
        rides=[_ride_scatter_chips([paired["d2"]], first), _ride_scatter_sibling([part_g2])])
    pair("g2", part_g2)
    dxn2, ((landed["d2"],),) = _ffn_contract(
        dg2, wt["g2"], "ffn2_dxn_g", rides=[_ride_scatter_chips([paired["d2"]], second, into=half)])
    dxn2, (half_g2, (got["u2"],)) = _ffn_contract(
        du2, wt["u2"], "ffn2_dxn_u", res=dxn2,
        rides=[_ride_scatter_chips([paired["g2"]], first), _ride_scatter_sibling([part_u2])])
    pair("u2", part_u2)
    dx2, dx2b, dgain_ffn2 = _rms_bwd(x2, ffn2_norm, dxn2, dy, "ffn2_norm_bwd")

    dmixed = _mm("mix_out_dx", [(dx2b, xspec, w_out_full, pl.BlockSpec((d, d), lambda i, k: (0, 0)))],
                 "nt", (t // tm, 1), jax.ShapeDtypeStruct((t, d), F32), xspec)

    def k_rows(width):
        return pl.BlockSpec((tk, width), lambda j, k: (k, 0))

    part_out = [
        _mm(f"mix_out_dw{i}", [(o, k_rows(width), dx2b, k_rows(d))], "tn", (1, t // tk),
            jax.ShapeDtypeStruct((width, d), BF16), pl.BlockSpec((width, d), lambda j, k: (0, 0)))
        for i, (o, width) in enumerate(((o_a, FOX_W), (o_b, SWA_W), (o_c, MEM_W)))
    ]
    part_out = jnp.concatenate(part_out, axis=0).reshape(N_DEV, d_shard, d)

    delta, delta_rep = _delta(dmixed, o_a, o_b, o_c, "attn_delta")
    (dfq, dfk, dfv, dc_col, dc_row), ((landed["g2"],), (landed["u2"],)) = _fox_bwd(
        fq, fk, fv, c_rep, c_row, dmixed, lse_a, delta_rep, "fox_bwd",
        rides=[_ride_scatter_chips([paired["g2"]], second, into=half_g2), _ride_scatter_chips([paired["u2"]])])
    dsq, dsk, dsv, dsp_sink = _swa_bwd(sq, sk, sv, sp, dmixed, lse_b, delta, "swa_bwd")
    dmq, dmk, dmv = _mem_bwd(mq, mk, mv, dmixed, lse_c, delta, "mem_bwd")

    dmkv, dsp_mem = _mem_post_bwd(mkv, sp, dmk, dmv, "mem_post_bwd")
    part_mkv = _mm("mem_kv_dw", [(mn, pl.BlockSpec((m_len, d), lambda k: (0, 0)),
                                  dmkv, pl.BlockSpec((m_len, 2 * MEM_W), lambda k: (0, 0)))],
                   "tn", (1,), jax.ShapeDtypeStruct((d, 2 * MEM_W), BF16),
                   pl.BlockSpec((d, 2 * MEM_W), lambda k: (0, 0))).reshape(N_DEV, d_shard, 2 * MEM_W)
    dmn = _mm("mem_kv_dx", [(dmkv, pl.BlockSpec((m_len, 2 * MEM_W), lambda k: (0, 0)),
                             w_mkv_full, pl.BlockSpec((d, 2 * MEM_W), lambda k: (0, 0)))],
              "nt", (1,), jax.ShapeDtypeStruct((m_len, d), F32), pl.BlockSpec((m_len, d), lambda k: (0, 0)))
    _, _, dgain_mem = _rms_bwd(mem0, mem_norm, dmn, None, "mem_norm_bwd")

    dc_row_t = _pad_row(jnp.transpose(dc_row), HEAD)
    dproj, dsp_attn = _attn_post_bwd(proj, sp, dfq, dfk, dfv, dsq, dsk, dsv, dmq, dc_col, dc_row_t, "attn_post_bwd")
    dhn, ((got["out"], got["mkv"]),) = _mm(
        "proj_dx", [(dproj, pl.BlockSpec((tm, IN_W), lambda i, k: (i, 0)),
                     w_in_full, pl.BlockSpec((d, IN_W), lambda i, k: (0, 0)))],
        "nt", (t // tm, 1), jax.ShapeDtypeStruct((t, d), F32), xspec,
        rides=[_ride_scatter_sibling([part_out, part_mkv])])
    pair("out", part_out)
    pair("mkv", part_mkv)
    part_in, ((landed["out"], landed["mkv"]),) = _mm(
        "proj_dw", [(hn, pl.BlockSpec((tkw, d), lambda n, k: (k, 0)),
                     dproj, pl.BlockSpec((tkw, tnw), lambda n, k: (k, n)))],
        "tn", (IN_W // tnw, t // tkw), jax.ShapeDtypeStruct((d, IN_W), BF16), pl.BlockSpec((d, tnw), lambda n, k: (0, n)),
        rides=[_ride_scatter_chips([paired["out"], paired["mkv"]])])
    part_in = part_in.reshape(N_DEV, d_shard, IN_W)
    dx1, dx1b, dgain_mix = _rms_bwd(x1, mix_norm, dhn, dx2, "mix_norm_bwd")

    (dg1, du1), ((got["in"],),) = _ffn_dact(dx1b, wt["d1"], g1, u1, "ffn1", rides=[_ride_scatter_sibling([part_in])])
    pair("in", part_in)
    part_d1, ((landed["in"],),) = _ffn_dw(h1, dx1b, 0.5, "ffn1_dwd", rides=[_ride_scatter_chips([paired["in"]])])
    part_g1, ((got["d1"],),) = _ffn_dw(dg1, xn1, 1.0, "ffn1_dwg", rides=[_ride_scatter_sibling([part_d1])])
    pair("d1", part_d1)
    part_u1, (half_d1, (got["g1"],)) = _ffn_dw(
        du1, xn1, 1.0, "ffn1_dwu",
        rides=[_ride_scatter_chips([paired["d1"]], first), _ride_scatter_sibling([part_g1])])
    pair("g1", part_g1)
    dxn1, ((landed["d1"],), half_g1, (got["u1"],)) = _ffn_contract(
        dg1, wt["g1"], "ffn1_dxn_g",
        rides=[_ride_scatter_chips([paired["d1"]], second, into=half_d1), _ride_scatter_chips([paired["g1"]], first),
               _ride_scatter_sibling([part_u1])])
    pair("u1", part_u1)
    dxn1, ((landed["g1"],), half_u1) = _ffn_contract(
        du1, wt["u1"], "ffn1_dxn_u", res=dxn1,
        rides=[_ride_scatter_chips([paired["g1"]], second, into=half_g1), _ride_scatter_chips([paired["u1"]], first)])
    (grad_x, _, dgain_ffn1), ((landed["u1"],),) = _rms_bwd(
        x0, ffn1_norm, dxn1, dx1, "ffn1_norm_bwd", rides=[_ride_scatter_chips([paired["u1"]], second, into=half_u1)])

    norms_sum, small_sum = _all_reduce_small(
        [_pack_norms(dgain_ffn1, dgain_mix, dgain_mem, dgain_ffn2), dsp_attn + dsp_sink + dsp_mem], "reduce_small")

    result = {
        "ffn1_gate": map(swap, _sum_adamw(landed["g1"], 0, gate1, swap(m_ffn1_gate), swap(v_ffn1_gate), "adamw_ffn1_gate")),
        "ffn1_up": map(swap, _sum_adamw(landed["u1"], 0, up1, swap(m_ffn1_up), swap(v_ffn1_up), "adamw_ffn1_up")),
        "ffn1_down": _sum_adamw(landed["d1"], 0, ffn1_down, m_ffn1_down, v_ffn1_down, "adamw_ffn1_down"),
        "w_mem_k": _sum_adamw(landed["mkv"], 0, w_mem_k, m_w_mem_k, v_w_mem_k, "adamw_w_mem_k"),
        "w_mem_v": _sum_adamw(landed["mkv"], 1, w_mem_v, m_w_mem_v, v_w_mem_v, "adamw_w_mem_v"),
        "w_out": _sum_adamw(landed["out"], 0, w_out, m_w_out, v_w_out, "adamw_w_out"),
        "ffn2_gate": map(swap, _sum_adamw(landed["g2"], 0, gate2, swap(m_ffn2_gate), swap(v_ffn2_gate), "adamw_ffn2_gate")),
        "ffn2_up": map(swap, _sum_adamw(landed["u2"], 0, up2, swap(m_ffn2_up), swap(v_ffn2_up), "adamw_ffn2_up")),
        "ffn2_down": _sum_adamw(landed["d2"], 0, ffn2_down, m_ffn2_down, v_ffn2_down, "adamw_ffn2_down"),
    }
    grad_in = _unpermute_in(_sum_chips(landed["in"], "sum_w_in"))
    result["w_in"] = (grad_in[None],) + tuple(
        o[None] for o in _adamw(w_in[0], grad_in, m_w_in[0], v_w_in[0], "adamw_w_in"))

    norm_names = ["ffn1_norm", "mix_norm", "mem_norm", "ffn2_norm"]
    norm_w = _pack_norms(ffn1_norm, mix_norm, mem_norm, ffn2_norm)
    norm_m = _pack_norms(m_ffn1_norm, m_mix_norm, m_mem_norm, m_ffn2_norm)
    norm_v = _pack_norms(v_ffn1_norm, v_mix_norm, v_mem_norm, v_ffn2_norm)
    outs = (norms_sum,) + tuple(_adamw(norm_w, norms_sum, norm_m, norm_v, "adamw_norms"))
    for i, k in enumerate(norm_names):
        result[k] = tuple(o[i:i + 1] for o in outs)

    small_names = ["fox_q_gain", "fox_k_gain", "swa_q_gain", "swa_k_gain", "mem_q_gain", "mem_k_gain",
                   "forget_bias", "swa_sinks"]
    small_m = _pack_small(m_fox_q_gain, m_fox_k_gain, m_swa_q_gain, m_swa_k_gain, m_mem_q_gain, m_mem_k_gain,
                          m_forget_bias, m_swa_sinks)
    small_v = _pack_small(v_fox_q_gain, v_fox_k_gain, v_swa_q_gain, v_swa_k_gain, v_mem_q_gain, v_mem_k_gain,
                          v_forget_bias, v_swa_sinks)
    outs = (small_sum,) + tuple(_adamw(sp, small_sum, small_m, small_v, "adamw_small"))
    for i, k in enumerate(small_names):
        width = N_LOGIT if k in ("forget_bias", "swa_sinks") else HEAD
        result[k] = tuple(o[i:i + 1, :width] for o in outs)

    order = ["ffn1_norm", "ffn1_gate", "ffn1_up", "ffn1_down", "mix_norm", "mem_norm", "w_in", "forget_bias",
             "w_mem_k", "w_mem_v", "fox_q_gain", "fox_k_gain", "swa_q_gain", "swa_k_gain", "swa_sinks",
             "mem_q_gain", "mem_k_gain", "w_out", "ffn2_norm", "ffn2_gate", "ffn2_up", "ffn2_down"]
    result = {k: tuple(v) for k, v in result.items()}
    flat = [loss, grad_x[None]]
    for kind in range(4):
        flat += [result[k][kind] for k in order]
    return tuple(flat)
```

```python
import functools

import jax
import jax.numpy as jnp
from jax import lax
from jax.experimental import pallas as pl
from jax.experimental.pallas import tpu as pltpu

F32 = jnp.float32
BF16 = jnp.bfloat16
MESH = pl.DeviceIdType.MESH
ANY = pl.BlockSpec(memory_space=pl.ANY)

N_DEV = 8
EPS = 1e-6
NEG_INF = -1e30
HEAD = 128
FOX_H, SWA_H, SWA_KV, MEM_H = 6, 6, 2, 4
FOX_W, SWA_W, SWA_KV_W, MEM_W = FOX_H * HEAD, SWA_H * HEAD, SWA_KV * HEAD, MEM_H * HEAD
SCALE = HEAD ** -0.5
SWA_BLOCK = 128
C_FQ, C_FK, C_FV = 0, FOX_W, 2 * FOX_W
C_SQ = 3 * FOX_W
C_SK = C_SQ + SWA_W
C_SV = C_SK + SWA_KV_W
C_MQ = C_SV + SWA_KV_W
C_FL = C_MQ + MEM_W
IN_W = C_FL + HEAD
N_LOGIT = FOX_H
R_FQ, R_FK, R_SQ, R_SK, R_MQ, R_MK, R_FB, R_SINK = range(8)
ADAM_LR, ADAM_B1, ADAM_B2, ADAM_EPS, ADAM_WD, ADAM_STEP = 0.001, 0.9, 0.999, 1e-08, 0.01, 10
VMEM_BYTES = 56 * 1024 * 1024

DN = {
    "nn": (((1,), (0,)), ((), ())),
    "nt": (((1,), (1,)), ((), ())),
    "tn": (((0,), (0,)), ((), ())),
}


def _params(n_axes):
    return pltpu.CompilerParams(dimension_semantics=("arbitrary",) * n_axes, vmem_limit_bytes=VMEM_BYTES)


def _dot(a, b, dims="nn"):
    return lax.dot_general(a.astype(BF16), b.astype(BF16), DN[dims], preferred_element_type=F32)


def _sigmoid(x):
    return 0.5 * jnp.tanh(0.5 * x) + 0.5


def _me():
    return lax.axis_index("x"), lax.axis_index("y"), lax.axis_index("c")


def _lin(p):
    return 4 * p[0] + 2 * p[1] + p[2]


def _rows_tile(rows, row_bytes, budget=4 << 20, mult=16):
    best = None
    for k in range(1, rows + 1):
        if rows % k == 0 and (rows // k) % mult == 0 and (rows // k) * row_bytes <= budget:
            best = rows // k
            break
    assert best is not None, (rows, row_bytes)
    return best


class _Ride:
    def __init__(self, inputs, out_shapes, aliases, n_remote, n_local, start, wait):
        self.inputs, self.out_shapes, self.aliases = list(inputs), list(out_shapes), dict(aliases)
        self.n_remote, self.n_local, self.start, self.wait = n_remote, n_local, start, wait


def _remote(src, dst, send, recv, k, to):
    return pltpu.make_async_remote_copy(src_ref=src, dst_ref=dst, send_sem=send.at[k], recv_sem=recv.at[k],
                                        device_id=to, device_id_type=MESH)


def _other_chips(x, y):
    return [(1 - x, y), (x, 1 - y), (1 - x, 1 - y)]


ALL_CHIPS = [(0, 0), (0, 1), (1, 0), (1, 1)]


def _ride_gather_chips(xs):
    n = len(xs)

    def copies(ins, outs, send, recv):
        x, y, c = _me()
        out = []
        for a in range(n):
            for j, chip in enumerate(_other_chips(x, y)):
                peer = (*chip, c)
                out.append((_remote(ins[a], outs[a].at[_lin((x, y, c))], send, recv, 3 * a + j, peer),
                            _remote(ins[a], outs[a].at[_lin(peer)], send, recv, 3 * a + j, peer)))
        return out

    def mine(ins, outs, local):
        me = _lin(_me())
        return [pltpu.make_async_copy(ins[a], outs[a].at[me], local.at[a]) for a in range(n)]

    def start(ins, outs, send, recv, local):
        for cp in mine(ins, outs, local):
            cp.start()
        for sent, _ in copies(ins, outs, send, recv):
            sent.start()

    def wait(ins, outs, send, recv, local):
        for sent, landed in copies(ins, outs, send, recv):
            landed.wait_recv()
            sent.wait_send()
        for cp in mine(ins, outs, local):
            cp.wait()

    shapes = [jax.ShapeDtypeStruct((N_DEV,) + x.shape, x.dtype) for x in xs]
    return _Ride(xs, shapes, {}, 3 * n, n, start, wait)


def _ride_gather_sibling(bufs):
    n = len(bufs)

    def copies(outs, send, recv):
        x, y, c = _me()
        out = []
        for a in range(n):
            for q, (px, py) in enumerate(ALL_CHIPS):
                there = outs[a].at[4 * px + 2 * py + c]
                here = outs[a].at[4 * px + 2 * py + 1 - c]
                out.append((_remote(there, there, send, recv, 4 * a + q, (x, y, 1 - c)),
                            _remote(here, here, send, recv, 4 * a + q, (x, y, 1 - c))))
        return out

    def start(ins, outs, send, recv, local):
        for sent, _ in copies(outs, send, recv):
            sent.start()

    def wait(ins, outs, send, recv, local):
        for sent, landed in copies(outs, send, recv):
            landed.wait_recv()
            sent.wait_send()

    shapes = [jax.ShapeDtypeStruct(b.shape, b.dtype) for b in bufs]
    return _Ride(bufs, shapes, {a: a for a in range(n)}, 4 * n, 0, start, wait)


def _ride_gather(xs, mid_frac):
    n = len(xs)
    chips = _ride_gather_chips(xs)

    def sibling_copies(outs, send, recv):
        x, y, c = _me()
        out = []
        for a in range(n):
            for q, (px, py) in enumerate(ALL_CHIPS):
                there = outs[a].at[4 * px + 2 * py + c]
                here = outs[a].at[4 * px + 2 * py + 1 - c]
                k = 3 * n + 4 * a + q
                out.append((_remote(there, there, send, recv, k, (x, y, 1 - c)),
                            _remote(here, here, send, recv, k, (x, y, 1 - c))))
        return out

    def mid(ins, outs, send, recv, local):
        chips.wait(ins, outs, send, recv, local)
        for sent, _ in sibling_copies(outs, send, recv):
            sent.start()

    def wait(ins, outs, send, recv, local):
        for sent, landed in sibling_copies(outs, send, recv):
            landed.wait_recv()
            sent.wait_send()

    ride = _Ride(xs, chips.out_shapes, {}, 7 * n, n, chips.start, wait)
    ride.mid, ride.mid_frac = mid, mid_frac
    return ride


def _ride_scatter_sibling(parts):
    n = len(parts)

    def copies(ins, outs, send, recv):
        x, y, c = _me()
        out = []
        for a in range(n):
            for q, (px, py) in enumerate(ALL_CHIPS):
                cp = _remote(ins[a].at[4 * px + 2 * py + 1 - c], outs[a].at[q], send, recv, 4 * a + q, (x, y, 1 - c))
                out.append(cp)
        return out

    def start(ins, outs, send, recv, local):
        for cp in copies(ins, outs, send, recv):
            cp.start()

    def wait(ins, outs, send, recv, local):
        for cp in copies(ins, outs, send, recv):
            cp.wait_recv()
            cp.wait_send()

    shapes = [jax.ShapeDtypeStruct((4,) + p.shape[1:], p.dtype) for p in parts]
    return _Ride(parts, shapes, {}, 4 * n, 0, start, wait)


def _ride_scatter_chips(pairs, rows=None, into=None):
    n = len(pairs)

    def part(ref, slot):
        return ref.at[slot] if rows is None else ref.at[slot, pl.ds(rows[0], rows[1])]

    def copies(ins, outs, send, recv):
        x, y, c = _me()
        out = []
        for a in range(n):
            for j, (px, py) in enumerate(_other_chips(x, y)):
                peer = (px, py, c)
                src = part(ins[a], 2 * px + py)
                out.append((_remote(src, part(outs[a], 2 * x + y), send, recv, 3 * a + j, peer),
                            _remote(src, part(outs[a], 2 * px + py), send, recv, 3 * a + j, peer)))
        return out

    def mine(ins, outs, local):
        x, y, _ = _me()
        return [pltpu.make_async_copy(part(ins[a], 2 * x + y), part(outs[a], 2 * x + y), local.at[a])
                for a in range(n)]

    def start(ins, outs, send, recv, local):
        for cp in mine(ins, outs, local):
            cp.start()
        for sent, _ in copies(ins, outs, send, recv):
            sent.start()

    def wait(ins, outs, send, recv, local):
        for sent, landed in copies(ins, outs, send, recv):
            landed.wait_recv()
            sent.wait_send()
        for cp in mine(ins, outs, local):
            cp.wait()

    shapes = [jax.ShapeDtypeStruct(p.shape, p.dtype) for p in pairs]
    if into is None:
        return _Ride(pairs, shapes, {}, 3 * n, n, start, wait)
    return _Ride(list(pairs) + list(into), shapes, {n + a: a for a in range(n)}, 3 * n, n, start, wait)


def _call(name, body, grid, in_specs, out_specs, out_shape, operands, scratch=(), rides=()):
    n_in, n_out, n_scr = len(operands), len(out_shape), len(scratch)
    ride_in, ride_out, ride_scr, aliases, spans = [], [], [], {}, []
    for r in rides:
        for i, o in r.aliases.items():
            aliases[n_in + len(ride_in) + i] = n_out + len(ride_out) + o
        spans.append((len(ride_in), len(r.inputs), len(ride_out), len(r.out_shapes)))
        ride_in += r.inputs
        ride_out += r.out_shapes
        ride_scr += [pltpu.SemaphoreType.DMA((r.n_remote,)), pltpu.SemaphoreType.DMA((r.n_remote,)),
                     pltpu.SemaphoreType.DMA((max(r.n_local, 1),))]

    def wrapped(*refs):
        c_in, r_in = refs[:n_in], refs[n_in:n_in + len(ride_in)]
        p = n_in + len(ride_in)
        c_out, r_out = refs[p:p + n_out], refs[p + n_out:p + n_out + len(ride_out)]
        p += n_out + len(ride_out)
        c_scr, r_scr = refs[p:p + n_scr], refs[p + n_scr:]

        n_steps = functools.reduce(lambda a, b: a * b, grid, 1)
        step = functools.reduce(lambda acc, ax: acc * grid[ax] + pl.program_id(ax), range(len(grid)), 0)

        def each(method, at):
            for k, (r, (i0, ni, o0, no)) in enumerate(zip(rides, spans)):
                fn = getattr(r, method, None)
                if fn is None:
                    continue
                run = functools.partial(fn, r_in[i0:i0 + ni], r_out[o0:o0 + no], *r_scr[3 * k:3 * k + 3])
                if grid:
                    pl.when(step == at(r))(run)
                else:
                    run()

        each("start", lambda r: 0)
        each("mid", lambda r: int(r.mid_frac * (n_steps - 1)))
        body(*c_in, *c_out, *c_scr)
        each("wait", lambda r: n_steps - 1)

    outs = pl.pallas_call(
        wrapped, grid=grid, in_specs=list(in_specs) + [ANY] * len(ride_in),
        out_specs=list(out_specs) + [ANY] * len(ride_out), out_shape=list(out_shape) + ride_out,
        scratch_shapes=list(scratch) + ride_scr, input_output_aliases=aliases,
        compiler_params=_params(len(grid)), name=name)(*operands, *ride_in)
    outs = list(outs)
    ride_results = [outs[n_out + o0:n_out + o0 + no] for (_, _, o0, no) in spans]
    return outs[:n_out], ride_results


def _only_copies(name, rides):
    return _call(name, lambda: None, (), [], [], [], [], rides=rides)[1]


class _Tail:
    def __init__(self, extra, out_shapes, out_specs, fn):
        self.extra, self.out_shapes, self.out_specs, self.fn = list(extra), list(out_shapes), list(out_specs), fn


def _mm(name, pairs, dims, grid, out_shape, out_spec, res=None, res_spec=None, alpha=1.0, rides=(), tail=None):
    n = len(pairs)
    nk = grid[-1]
    kax = len(grid) - 1
    acc_shape = tuple(d for d in out_spec.block_shape if d is not None)
    n_extra = len(tail.extra) if tail else 0
    n_outs = len(tail.out_shapes) if tail else 1

    def body(*refs):
        pos = 2 * n
        r_ref = None
        if res is not None:
            r_ref = refs[pos]
            pos += 1
        x_refs = refs[pos:pos + n_extra]
        o_refs = refs[pos + n_extra:pos + n_extra + n_outs]
        pos += n_extra + n_outs
        part = None
        for p in range(n):
            d = _dot(refs[2 * p][...], refs[2 * p + 1][...], dims)
            part = d if part is None else part + d

        def finish(acc):
            if alpha != 1.0:
                acc = acc * alpha
            if r_ref is not None:
                acc = r_ref[...] + acc
            if tail:
                tail.fn(acc, x_refs, o_refs)
            else:
                o_refs[0][...] = acc.astype(o_refs[0].dtype)

        if nk == 1:
            finish(part)
        else:
            acc_ref = refs[pos]
            k = pl.program_id(kax)

            @pl.when(k == 0)
            def _():
                acc_ref[...] = part

            @pl.when(k > 0)
            def _():
                acc_ref[...] += part

            @pl.when(k == nk - 1)
            def _():
                finish(acc_ref[...])

    operands, in_specs = [], []
    for a, a_spec, b, b_spec in pairs:
        operands += [a, b]
        in_specs += [a_spec, b_spec]
    if res is not None:
        operands.append(res)
        in_specs.append(res_spec)
    for a, a_spec in (tail.extra if tail else []):
        operands.append(a)
        in_specs.append(a_spec)
    outs, ride_results = _call(name, body, grid, in_specs, tail.out_specs if tail else [out_spec],
                               tail.out_shapes if tail else [out_shape], operands,
                               scratch=[pltpu.VMEM(acc_shape, F32)] if nk > 1 else [], rides=rides)
    outs = outs if tail else outs[0]
    return (outs, ride_results) if rides else outs


def _accumulate(ref, part):
    @pl.when(pl.program_id(0) == 0)
    def _():
        ref[...] = part

    @pl.when(pl.program_id(0) > 0)
    def _():
        ref[...] += part


def _tail_norm(gain, t, d, tm):
    def fn(v, x_refs, o_refs):
        o_refs[0][...] = v
        r = lax.rsqrt(jnp.mean(v * v, axis=-1, keepdims=True) + EPS)
        o_refs[1][...] = (v * r * x_refs[0][...]).astype(BF16)

    rows = pl.BlockSpec((tm, d), lambda i, k: (i, 0))
    return _Tail([(gain, pl.BlockSpec((1, d), lambda i, k: (0, 0)))],
                 [jax.ShapeDtypeStruct((t, d), F32), jax.ShapeDtypeStruct((t, d), BF16)], [rows, rows], fn)


def _tail_loss(target, t, d, tm):
    def fn(v, x_refs, o_refs):
        err = v - x_refs[0][...]
        dy = err * (1.0 / d)
        o_refs[0][...] = dy
        o_refs[1][...] = dy.astype(BF16)
        _accumulate(o_refs[2], jnp.zeros((8, 128), F32) + jnp.sum(err * err))

    rows = pl.BlockSpec((tm, d), lambda i, k: (i, 0))
    return _Tail([(target, rows)],
                 [jax.ShapeDtypeStruct((t, d), F32), jax.ShapeDtypeStruct((t, d), BF16),
                  jax.ShapeDtypeStruct((8, 128), F32)],
                 [rows, rows, pl.BlockSpec((8, 128), lambda i, k: (0, 0))], fn)


def _tail_norm_bwd(x, gain, dres, t, d, tm):
    def fn(dy, x_refs, o_refs):
        xv = x_refs[0][...]
        r = lax.rsqrt(jnp.mean(xv * xv, axis=-1, keepdims=True) + EPS)
        xh = xv * r
        dxh = dy * x_refs[1][...]
        dx = r * (dxh - xh * jnp.mean(dxh * xh, axis=-1, keepdims=True)) + x_refs[2][...]
        o_refs[0][...] = dx
        o_refs[1][...] = dx.astype(BF16)
        _accumulate(o_refs[2], jnp.sum(dy * xh, axis=0, keepdims=True))

    rows = pl.BlockSpec((tm, d), lambda i, k: (i, 0))
    vec = pl.BlockSpec((1, d), lambda i, k: (0, 0))
    return _Tail([(x, rows), (gain, vec), (dres, rows)],
                 [jax.ShapeDtypeStruct((t, d), F32), jax.ShapeDtypeStruct((t, d), BF16),
                  jax.ShapeDtypeStruct((1, d), F32)], [rows, rows, vec], fn)


def _cast_bf16(x, name):
    rows, cols = x.shape
    tm = _rows_tile(rows, cols * 4)

    def body(x_ref, o_ref):
        o_ref[...] = x_ref[...].astype(BF16)

    spec = pl.BlockSpec((tm, cols), lambda i: (i, 0))
    return pl.pallas_call(body, grid=(rows // tm,), in_specs=[spec], out_specs=spec,
                          out_shape=jax.ShapeDtypeStruct(x.shape, BF16), compiler_params=_params(1), name=name)(x)


def _rms_fwd(x, gain, name):
    rows, d = x.shape
    tm = min(rows, 512)

    def body(x_ref, g_ref, o_ref):
        xv = x_ref[...]
        r = lax.rsqrt(jnp.mean(xv * xv, axis=-1, keepdims=True) + EPS)
        o_ref[...] = (xv * r * g_ref[...]).astype(BF16)

    spec = pl.BlockSpec((tm, d), lambda i: (i, 0))
    return pl.pallas_call(body, grid=(rows // tm,), in_specs=[spec, pl.BlockSpec((1, d), lambda i: (0, 0))],
                          out_specs=spec, out_shape=jax.ShapeDtypeStruct(x.shape, BF16),
                          compiler_params=_params(1), name=name)(x, gain)


def _rms_bwd(x, gain, dxn, dres, name, rides=()):
    rows, d = x.shape
    tm = min(rows, 256)
    with_res = dres is not None

    def body(*refs):
        if with_res:
            x_ref, g_ref, dy_ref, r_ref, dx_ref, dxb_ref, dg_ref = refs
        else:
            x_ref, g_ref, dy_ref, dx_ref, dxb_ref, dg_ref = refs
        xv = x_ref[...]
        r = lax.rsqrt(jnp.mean(xv * xv, axis=-1, keepdims=True) + EPS)
        xh = xv * r
        dy = dy_ref[...]
        dxh = dy * g_ref[...]
        dx = r * (dxh - xh * jnp.mean(dxh * xh, axis=-1, keepdims=True))
        if with_res:
            dx = dx + r_ref[...]
        dx_ref[...] = dx
        dxb_ref[...] = dx.astype(BF16)
        part = jnp.sum(dy * xh, axis=0, keepdims=True)

        @pl.when(pl.program_id(0) == 0)
        def _():
            dg_ref[...] = part

        @pl.when(pl.program_id(0) > 0)
        def _():
            dg_ref[...] += part

    spec = pl.BlockSpec((tm, d), lambda i: (i, 0))
    vec = pl.BlockSpec((1, d), lambda i: (0, 0))
    ops = [x, gain, dxn] + ([dres] if with_res else [])
    outs, ride_results = _call(
        name, body, (rows // tm,), [spec, vec, spec] + ([spec] if with_res else []), [spec, spec, vec],
        [jax.ShapeDtypeStruct(x.shape, F32), jax.ShapeDtypeStruct(x.shape, BF16), jax.ShapeDtypeStruct((1, d), F32)],
        ops, rides=rides)
    return (outs, ride_results) if rides else outs


ROW_CHUNK = 256


def _ffn_up(xn, wg, wu, tag, rides=()):
    t, d = xn.shape
    nd, fs, _ = wg.shape
    tm = min(t, 512)
    rc = min(tm, ROW_CHUNK)

    def body(x_ref, wg_ref, wu_ref, g_ref, u_ref, h_ref):
        for r in range(0, tm, rc):
            xv = x_ref[r:r + rc, :]
            g = _dot(xv, wg_ref[...], "nt")
            u = _dot(xv, wu_ref[...], "nt")
            g_ref[r:r + rc, :] = g.astype(BF16)
            u_ref[r:r + rc, :] = u.astype(BF16)
            h_ref[r:r + rc, :] = (g * _sigmoid(g) * u).astype(BF16)

    wspec = pl.BlockSpec((None, fs, d), lambda j, i: (j, 0, 0))
    hspec = pl.BlockSpec((None, tm, fs), lambda j, i: (j, i, 0))
    hid = jax.ShapeDtypeStruct((nd, t, fs), BF16)
    return _call(f"{tag}_up", body, (nd, t // tm), [pl.BlockSpec((tm, d), lambda j, i: (i, 0)), wspec, wspec],
                 [hspec] * 3, [hid] * 3, [xn, wg, wu], rides=rides)


CONTRACT_ROWS = 256


def _ffn_contract(hid, w, name, res=None, alpha=1.0, rides=(), tail=None):
    nd, t, fs = hid.shape
    d = w.shape[2]
    tm = min(t, CONTRACT_ROWS)
    xspec = pl.BlockSpec((tm, d), lambda i, k: (i, 0))
    pairs = [(hid, pl.BlockSpec((None, tm, fs), lambda i, k, s=s: (s, i, 0)),
              w, pl.BlockSpec((None, fs, d), lambda i, k, s=s: (s, 0, 0), pipeline_mode=pl.Buffered(1)))
             for s in range(nd)]
    return _mm(name, pairs, "nn", (t // tm, 1), jax.ShapeDtypeStruct((t, d), F32), xspec,
               res=res, res_spec=xspec if res is not None else None, alpha=alpha, rides=rides, tail=tail)


def _ffn_down(x, h, wd, tag, rides=(), tail=None):
    return _ffn_contract(h, wd, f"{tag}_down", res=x, alpha=0.5, rides=rides, tail=tail)


def _ffn_dact(dyb, wd, g, u, tag, rides=()):
    nd, t, fs = g.shape
    d = dyb.shape[1]
    tm = min(t, 512)
    rc = min(tm, ROW_CHUNK)

    def body(dy_ref, wd_ref, g_ref, u_ref, dg_ref, du_ref):
        for r in range(0, tm, rc):
            dh = 0.5 * _dot(dy_ref[r:r + rc, :], wd_ref[...], "nt")
            gv = g_ref[r:r + rc, :].astype(F32)
            uv = u_ref[r:r + rc, :].astype(F32)
            sig = _sigmoid(gv)
            du_ref[r:r + rc, :] = (dh * gv * sig).astype(BF16)
            dg_ref[r:r + rc, :] = (dh * uv * sig * (1.0 + gv * (1.0 - sig))).astype(BF16)

    hspec = pl.BlockSpec((None, tm, fs), lambda j, i: (j, i, 0))
    hid = jax.ShapeDtypeStruct((nd, t, fs), BF16)
    return _call(f"{tag}_dact", body, (nd, t // tm),
                 [pl.BlockSpec((tm, d), lambda j, i: (i, 0)), pl.BlockSpec((None, fs, d), lambda j, i: (j, 0, 0)),
                  hspec, hspec], [hspec] * 2, [hid] * 2, [dyb, wd, g, u], rides=rides)


def _ffn_dw(hid, act, alpha, name, rides=()):
    nd, t, fs = hid.shape
    d = act.shape[1]
    tk = t
    return _mm(name, [(hid, pl.BlockSpec((None, tk, fs), lambda j, k: (j, k, 0)),
                       act, pl.BlockSpec((tk, d), lambda j, k: (k, 0), pipeline_mode=pl.Buffered(1)))],
               "tn", (nd, t // tk),
               jax.ShapeDtypeStruct((nd, fs, d), BF16), pl.BlockSpec((None, fs, d), lambda j, k: (j, 0, 0)),
               alpha=alpha, rides=rides)


def _head_norm(x, gain):
    r = lax.rsqrt(jnp.mean(x * x, axis=-1, keepdims=True) + EPS)
    return x * r * gain


def _head_norm_bwd(x, gain, dy):
    r = lax.rsqrt(jnp.mean(x * x, axis=-1, keepdims=True) + EPS)
    xh = x * r
    dxh = dy * gain
    dx = r * (dxh - xh * jnp.mean(dxh * xh, axis=-1, keepdims=True))
    return dx, jnp.sum(dy * xh, axis=0, keepdims=True)


def _hs(h, base=0):
    return slice(base + h * HEAD, base + (h + 1) * HEAD)


def _tri(n, lower):
    r = lax.broadcasted_iota(jnp.int32, (n, n), 0)
    c = lax.broadcasted_iota(jnp.int32, (n, n), 1)
    return ((r >= c) if lower else (r <= c)).astype(F32)


def _attn_pre(proj, sp, name):
    t = proj.shape[0]
    tm = min(t, 256)

    def body(p_ref, sp_ref, fq, fk, fv, sq, sk, sv, mq, cc, carry):
        @pl.when(pl.program_id(0) == 0)
        def _():
            carry[...] = jnp.zeros_like(carry)

        for h in range(FOX_H):
            fq[:, _hs(h)] = _head_norm(p_ref[:, _hs(h, C_FQ)], sp_ref[R_FQ:R_FQ + 1, :]).astype(BF16)
            fk[:, _hs(h)] = _head_norm(p_ref[:, _hs(h, C_FK)], sp_ref[R_FK:R_FK + 1, :]).astype(BF16)
        fv[...] = p_ref[:, C_FV:C_FV + FOX_W].astype(BF16)
        for h in range(SWA_H):
            sq[:, _hs(h)] = _head_norm(p_ref[:, _hs(h, C_SQ)], sp_ref[R_SQ:R_SQ + 1, :]).astype(BF16)
        for h in range(SWA_KV):
            sk[:, _hs(h)] = _head_norm(p_ref[:, _hs(h, C_SK)], sp_ref[R_SK:R_SK + 1, :]).astype(BF16)
        sv[...] = p_ref[:, C_SV:C_SV + SWA_KV_W].astype(BF16)
        for h in range(MEM_H):
            mq[:, _hs(h)] = _head_norm(p_ref[:, _hs(h, C_MQ)], sp_ref[R_MQ:R_MQ + 1, :]).astype(BF16)
        z = p_ref[:, C_FL:C_FL + HEAD] + sp_ref[R_FB:R_FB + 1, :]
        lane = lax.broadcasted_iota(jnp.int32, z.shape, 1)
        log_f = jnp.minimum(z, 0.0) - jnp.log(1.0 + jnp.exp(-jnp.abs(z)))
        log_f = jnp.where(lane < N_LOGIT, log_f, 0.0)
        c = jnp.dot(_tri(tm, True), log_f, precision=lax.Precision.HIGHEST, preferred_element_type=F32)
        c = c + carry[0:1, :]
        cc[...] = c
        carry[...] = jnp.broadcast_to(c[tm - 1:tm, :], carry.shape)

    def rows(w):
        return pl.BlockSpec((tm, w), lambda i: (i, 0))

    def shape(w, dt):
        return jax.ShapeDtypeStruct((t, w), dt)

    widths = [FOX_W, FOX_W, FOX_W, SWA_W, SWA_KV_W, SWA_KV_W, MEM_W]
    return pl.pallas_call(
        body, grid=(t // tm,), in_specs=[rows(IN_W), pl.BlockSpec((16, 128), lambda i: (0, 0))],
        out_specs=[rows(w) for w in widths] + [rows(HEAD)],
        out_shape=[shape(w, BF16) for w in widths] + [shape(HEAD, F32)],
        scratch_shapes=[pltpu.VMEM((8, 128), F32)], compiler_params=_params(1), name=name)(proj, sp)


def _attn_post_bwd(proj, sp, dfq, dfk, dfv, dsq, dsk, dsv, dmq, dc_col, dc_row_t, name):
    t = proj.shape[0]
    tm = min(t, 256)
    nb = t // tm

    def body(p_ref, sp_ref, dfq_r, dfk_r, dfv_r, dsq_r, dsk_r, dsv_r, dmq_r, dcc_r, dcr_r, dp_ref, dsp_ref, carry):
        @pl.when(pl.program_id(0) == 0)
        def _():
            carry[...] = jnp.zeros_like(carry)
            dsp_ref[...] = jnp.zeros_like(dsp_ref)

        def group(n_heads, col, row, d_ref):
            total = None
            for h in range(n_heads):
                dx, dg = _head_norm_bwd(p_ref[:, _hs(h, col)], sp_ref[row:row + 1, :], d_ref[:, _hs(h)])
                dp_ref[:, _hs(h, col)] = dx.astype(BF16)
                total = dg if total is None else total + dg
            dsp_ref[row:row + 1, :] += total

        group(FOX_H, C_FQ, R_FQ, dfq_r)
        group(FOX_H, C_FK, R_FK, dfk_r)
        dp_ref[:, C_FV:C_FV + FOX_W] = dfv_r[...].astype(BF16)
        group(SWA_H, C_SQ, R_SQ, dsq_r)
        group(SWA_KV, C_SK, R_SK, dsk_r)
        dp_ref[:, C_SV:C_SV + SWA_KV_W] = dsv_r[...].astype(BF16)
        group(MEM_H, C_MQ, R_MQ, dmq_r)
        dc = dcc_r[...] - dcr_r[...]
        rc = jnp.dot(_tri(tm, False), dc, precision=lax.Precision.HIGHEST, preferred_element_type=F32)
        rc = rc + carry[0:1, :]
        carry[...] = jnp.broadcast_to(rc[0:1, :], carry.shape)
        z = p_ref[:, C_FL:C_FL + HEAD] + sp_ref[R_FB:R_FB + 1, :]
        dz = rc * _sigmoid(-z)
        dp_ref[:, C_FL:C_FL + HEAD] = dz.astype(BF16)
        dsp_ref[R_FB:R_FB + 1, :] += jnp.sum(dz, axis=0, keepdims=True)

    def rows(w):
        return pl.BlockSpec((tm, w), lambda i: (nb - 1 - i, 0))

    small = pl.BlockSpec((16, 128), lambda i: (0, 0))
    widths = [FOX_W, FOX_W, FOX_W, SWA_W, SWA_KV_W, SWA_KV_W, MEM_W, HEAD, HEAD]
    return pl.pallas_call(
        body, grid=(nb,), in_specs=[rows(IN_W), small] + [rows(w) for w in widths],
        out_specs=[rows(IN_W), small],
        out_shape=[jax.ShapeDtypeStruct((t, IN_W), BF16), jax.ShapeDtypeStruct((16, 128), F32)],
        scratch_shapes=[pltpu.VMEM((8, 128), F32)], compiler_params=_params(1), name=name,
    )(proj, sp, dfq, dfk, dfv, dsq, dsk, dsv, dmq, dc_col, dc_row_t)


def _head_column(values):
    rows = values[0].shape[0]
    lane = lax.broadcasted_iota(jnp.int32, (rows, HEAD), 1)
    out = jnp.zeros((rows, HEAD), F32)
    for h, v in enumerate(values):
        out = jnp.where(lane == h, v, out)
    return out


def _head_row(values, n_rows=8):
    cols = values[0].shape[1]
    sub = lax.broadcasted_iota(jnp.int32, (n_rows, cols), 0)
    out = jnp.zeros((n_rows, cols), F32)
    for h, v in enumerate(values):
        out = jnp.where(sub == h, v, out)
    return out


def _delta(dmixed, o_a, o_b, o_c, name):
    t = dmixed.shape[0]
    tm = min(t, 512)

    def body(d_ref, a_ref, b_ref, c_ref, o_ref, rep_ref):
        cols = []
        for ref, n_heads, base in ((a_ref, FOX_H, 0), (b_ref, SWA_H, FOX_W), (c_ref, MEM_H, FOX_W + SWA_W)):
            for h in range(n_heads):
                cols.append(jnp.sum(d_ref[:, _hs(h, base)] * ref[:, _hs(h)], axis=-1, keepdims=True))
        o_ref[...] = _head_column(cols)
        for h in range(FOX_H):
            rep_ref[h] = jnp.broadcast_to(cols[h], (tm, HEAD))

    def rows(w):
        return pl.BlockSpec((tm, w), lambda i: (i, 0))

    return pl.pallas_call(body, grid=(t // tm,), in_specs=[rows(dmixed.shape[1]), rows(FOX_W), rows(SWA_W), rows(MEM_W)],
                          out_specs=[rows(HEAD), pl.BlockSpec((FOX_H, tm, HEAD), lambda i: (0, i, 0))],
                          out_shape=[jax.ShapeDtypeStruct((t, HEAD), F32), jax.ShapeDtypeStruct((FOX_H, t, HEAD), F32)],
                          compiler_params=_params(1), name=name)(dmixed, o_a, o_b, o_c)


def _fox_fwd(fq, fk, fv, c_rep, c_row, name, rides=()):
    t = fq.shape[0]
    tb = min(t, 512)
    nb = t // tb
    n_tiles = tb // HEAD

    def body(q_ref, k_ref, v_ref, cc_ref, cr_ref, o_ref, lse_ref, m_s, l_s, acc_s):
        qi, ki = pl.program_id(0), pl.program_id(1)

        @pl.when(ki == 0)
        def _():
            m_s[...] = jnp.full_like(m_s, NEG_INF)
            l_s[...] = jnp.zeros_like(l_s)
            acc_s[...] = jnp.zeros_like(acc_s)

        def step(diagonal):
            if diagonal:
                r = lax.broadcasted_iota(jnp.int32, (tb, HEAD), 0)
                c = lax.broadcasted_iota(jnp.int32, (tb, HEAD), 1)
            for h in range(FOX_H):
                s = _dot(q_ref[:, _hs(h)], k_ref[:, _hs(h)], "nt")
                cc = cc_ref[h]
                tiles, m_cur = [], None
                for j in range(n_tiles):
                    st = s[:, _hs(j)] * SCALE + cc - cr_ref[h:h + 1, _hs(j)]
                    if diagonal:
                        st = jnp.where(r >= c + j * HEAD, st, NEG_INF)
                    tiles.append(st)
                    m_cur = st if m_cur is None else jnp.maximum(m_cur, st)
                m_prev = m_s[h]
                m_new = jnp.maximum(m_prev, jnp.max(m_cur, axis=-1, keepdims=True))
                alpha = jnp.exp(m_prev - m_new)
                ps = [jnp.exp(st - m_new) for st in tiles]
                l_cur = ps[0]
                for p in ps[1:]:
                    l_cur = l_cur + p
                l_s[h] = alpha * l_s[h] + jnp.sum(l_cur, axis=-1, keepdims=True)
                p = jnp.concatenate([p.astype(BF16) for p in ps], axis=1)
                acc_s[:, _hs(h)] = alpha * acc_s[:, _hs(h)] + _dot(p, v_ref[:, _hs(h)])
                m_s[h] = m_new

        @pl.when(ki < qi)
        def _():
            step(False)

        @pl.when(ki == qi)
        def _():
            step(True)
            for h in range(FOX_H):
                o_ref[:, _hs(h)] = acc_s[:, _hs(h)] / l_s[h]
                lse_ref[h] = m_s[h] + jnp.log(l_s[h])

    qspec = pl.BlockSpec((tb, FOX_W), lambda i, j: (i, 0))
    kspec = pl.BlockSpec((tb, FOX_W), lambda i, j: (jnp.minimum(i, j), 0))
    rep = pl.BlockSpec((FOX_H, tb, HEAD), lambda i, j: (0, i, 0))
    return _call(
        name, body, (nb, nb),
        [qspec, kspec, kspec, rep, pl.BlockSpec((8, tb), lambda i, j: (0, jnp.minimum(i, j)))],
        [qspec, rep],
        [jax.ShapeDtypeStruct((t, FOX_W), F32), jax.ShapeDtypeStruct((FOX_H, t, HEAD), F32)],
        [fq, fk, fv, c_rep, c_row],
        scratch=[pltpu.VMEM((FOX_H, tb, HEAD), F32), pltpu.VMEM((FOX_H, tb, HEAD), F32), pltpu.VMEM((tb, FOX_W), F32)],
        rides=rides)


def _fox_bwd(fq, fk, fv, c_rep, c_row, dmixed, lse, delta, name, rides=()):
    t = fq.shape[0]
    tb = min(t, 512)
    nb = t // tb
    n_tiles = tb // HEAD

    def body(q_ref, k_ref, v_ref, cc_ref, cr_ref, do_ref, lse_ref, dl_ref,
             dq_ref, dk_ref, dv_ref, dcc_ref, dcr_ref):
        ki, qi = pl.program_id(0), pl.program_id(1)

        @pl.when((ki == 0) & (qi == 0))
        def _():
            dq_ref[...] = jnp.zeros_like(dq_ref)
            dcc_ref[...] = jnp.zeros_like(dcc_ref)

        @pl.when(qi == 0)
        def _():
            dk_ref[...] = jnp.zeros_like(dk_ref)
            dv_ref[...] = jnp.zeros_like(dv_ref)
            dcr_ref[...] = jnp.zeros_like(dcr_ref)

        def step(diagonal):
            rows = pl.ds(pl.multiple_of(qi * tb, tb), tb)
            if diagonal:
                r = lax.broadcasted_iota(jnp.int32, (tb, HEAD), 0)
                c = lax.broadcasted_iota(jnp.int32, (tb, HEAD), 1)
            row_sums, col_sums = [], []
            for h in range(FOX_H):
                q, k, v, do = q_ref[:, _hs(h)], k_ref[:, _hs(h)], v_ref[:, _hs(h)], do_ref[:, _hs(h)]
                s = _dot(q, k, "nt")
                dp = _dot(do, v, "nt")
                cc, lse_h, dl_h = cc_ref[h], lse_ref[h], dl_ref[h]
                ps, dss, row = [], [], None
                for j in range(n_tiles):
                    st = s[:, _hs(j)] * SCALE + cc - cr_ref[h:h + 1, _hs(j)]
                    if diagonal:
                        st = jnp.where(r >= c + j * HEAD, st, NEG_INF)
                    pt = jnp.exp(st - lse_h)
                    dst = pt * (dp[:, _hs(j)] - dl_h)
                    ps.append(pt.astype(BF16))
                    dss.append(dst)
                    row = dst if row is None else row + dst
                p = jnp.concatenate(ps, axis=1)
                ds = jnp.concatenate(dss, axis=1)
                dsb = ds.astype(BF16)
                dv_ref[:, _hs(h)] += _dot(p, do, "tn")
                dk_ref[:, _hs(h)] += _dot(dsb, q, "tn") * SCALE
                dq_ref[rows, _hs(h)] += _dot(dsb, k) * SCALE
                row_sums.append(jnp.sum(row, axis=1, keepdims=True))
                col_sums.append(jnp.sum(ds, axis=0, keepdims=True))
            dcc_ref[rows, :] += _head_column(row_sums)
            dcr_ref[...] += _head_row(col_sums)

        @pl.when(qi > ki)
        def _():
            step(False)

        @pl.when(qi == ki)
        def _():
            step(True)

    def qmap(j, i):
        return (jnp.maximum(i, j), 0)

    qspec = pl.BlockSpec((tb, FOX_W), qmap)
    kspec = pl.BlockSpec((tb, FOX_W), lambda j, i: (j, 0))
    rep = pl.BlockSpec((FOX_H, tb, HEAD), lambda j, i: (0, jnp.maximum(i, j), 0))
    rowspec = pl.BlockSpec((8, tb), lambda j, i: (0, j))
    return _call(
        name, body, (nb, nb), [qspec, kspec, kspec, rep, rowspec, qspec, rep, rep],
        [pl.BlockSpec((t, FOX_W), lambda j, i: (0, 0)), kspec, kspec,
         pl.BlockSpec((t, HEAD), lambda j, i: (0, 0)), rowspec],
        [jax.ShapeDtypeStruct((t, FOX_W), F32)] * 3 + [jax.ShapeDtypeStruct((t, HEAD), F32),
                                                       jax.ShapeDtypeStruct((8, t), F32)],
        [fq, fk, fv, c_rep, c_row, dmixed, lse, delta], rides=rides)


def _swa_logits(q, k_cur, k_prev, slope, first_block):
    w = SWA_BLOCK
    r = lax.broadcasted_iota(jnp.int32, (w, w), 0)
    j = lax.broadcasted_iota(jnp.int32, (w, w), 1)
    dist_cur = r - j
    dist_prev = w + r - j
    s_cur = _dot(q, k_cur, "nt") * SCALE - slope * dist_cur.astype(F32)
    s_cur = jnp.where(dist_cur >= 0, s_cur, NEG_INF)
    s_prev = _dot(q, k_prev, "nt") * SCALE - slope * dist_prev.astype(F32)
    s_prev = jnp.where((j > r) & jnp.logical_not(first_block), s_prev, NEG_INF)
    return s_cur, s_prev


def _slope(h):
    return float(2.0 ** (-8.0 * (h + 1) / SWA_H))


def _swa_fwd(sq, sk, sv, sp, name, rides=()):
    t = sq.shape[0]
    w = SWA_BLOCK
    nb = t // w
    group = SWA_H // SWA_KV

    def body(q_ref, kp_ref, kc_ref, vp_ref, vc_ref, sp_ref, o_ref, lse_ref):
        first = pl.program_id(0) == 0
        lses = []
        for h in range(SWA_H):
            kv = h // group
            s_cur, s_prev = _swa_logits(q_ref[:, _hs(h)], kc_ref[:, _hs(kv)], kp_ref[:, _hs(kv)], _slope(h), first)
            sink = sp_ref[R_SINK:R_SINK + 1, h:h + 1]
            m = jnp.maximum(jnp.maximum(jnp.max(s_cur, axis=-1, keepdims=True),
                                        jnp.max(s_prev, axis=-1, keepdims=True)), sink)
            p_cur = jnp.exp(s_cur - m)
            p_prev = jnp.exp(s_prev - m)
            l = jnp.sum(p_cur, axis=-1, keepdims=True) + jnp.sum(p_prev, axis=-1, keepdims=True) + jnp.exp(sink - m)
            o_ref[:, _hs(h)] = (_dot(p_cur, vc_ref[:, _hs(kv)]) + _dot(p_prev, vp_ref[:, _hs(kv)])) / l
            lses.append(m + jnp.log(l))
        lse_ref[...] = _head_column(lses)

    qspec = pl.BlockSpec((w, SWA_W), lambda n: (n, 0))
    cur = pl.BlockSpec((w, SWA_KV_W), lambda n: (n, 0))
    prev = pl.BlockSpec((w, SWA_KV_W), lambda n: (jnp.maximum(n - 1, 0), 0))
    return _call(
        name, body, (nb,), [qspec, prev, cur, prev, cur, pl.BlockSpec((16, 128), lambda n: (0, 0))],
        [qspec, pl.BlockSpec((w, HEAD), lambda n: (n, 0))],
        [jax.ShapeDtypeStruct((t, SWA_W), F32), jax.ShapeDtypeStruct((t, HEAD), F32)],
        [sq, sk, sk, sv, sv, sp], rides=rides)


def _swa_bwd(sq, sk, sv, sp, dmixed, lse, delta, name):
    t = sq.shape[0]
    w = SWA_BLOCK
    nb = t // w
    group = SWA_H // SWA_KV
    do_block = FOX_W // SWA_W
    assert FOX_W % SWA_W == 0

    def body(q_ref, kp_ref, kc_ref, vp_ref, vc_ref, sp_ref, do_ref, lse_ref, dl_ref,
             dq_ref, dk_ref, dv_ref, dsp_ref, ck, cv):
        step = pl.program_id(0)
        first = step == nb - 1

        @pl.when(step == 0)
        def _():
            ck[...] = jnp.zeros_like(ck)
            cv[...] = jnp.zeros_like(cv)
            dsp_ref[...] = jnp.zeros_like(dsp_ref)

        dk_cur = [None] * SWA_KV
        dk_prev = [None] * SWA_KV
        dv_cur = [None] * SWA_KV
        dv_prev = [None] * SWA_KV
        dsinks = []

        def add(lst, i, v):
            lst[i] = v if lst[i] is None else lst[i] + v

        for h in range(SWA_H):
            kv = h // group
            q, do = q_ref[:, _hs(h)], do_ref[:, _hs(h)]
            kc, kp, vc, vp = kc_ref[:, _hs(kv)], kp_ref[:, _hs(kv)], vc_ref[:, _hs(kv)], vp_ref[:, _hs(kv)]
            s_cur, s_prev = _swa_logits(q, kc, kp, _slope(h), first)
            lse_h = lse_ref[:, h:h + 1]
            dl_h = dl_ref[:, FOX_H + h:FOX_H + h + 1]
            p_cur = jnp.exp(s_cur - lse_h)
            p_prev = jnp.exp(s_prev - lse_h)
            p_sink = jnp.exp(sp_ref[R_SINK:R_SINK + 1, h:h + 1] - lse_h)
            ds_cur = p_cur * (_dot(do, vc, "nt") - dl_h)
            ds_prev = p_prev * (_dot(do, vp, "nt") - dl_h)
            dq_ref[:, _hs(h)] = (_dot(ds_cur, kc) + _dot(ds_prev, kp)) * SCALE
            add(dk_cur, kv, _dot(ds_cur, q, "tn") * SCALE)
            add(dk_prev, kv, _dot(ds_prev, q, "tn") * SCALE)
            add(dv_cur, kv, _dot(p_cur, do, "tn"))
            add(dv_prev, kv, _dot(p_prev, do, "tn"))
            dsinks.append(-jnp.sum(p_sink * dl_h, axis=0, keepdims=True))
        for kv in range(SWA_KV):
            dk_ref[:, _hs(kv)] = dk_cur[kv] + ck[:, _hs(kv)]
            dv_ref[:, _hs(kv)] = dv_cur[kv] + cv[:, _hs(kv)]
            ck[:, _hs(kv)] = dk_prev[kv]
            cv[:, _hs(kv)] = dv_prev[kv]
        lane = lax.broadcasted_iota(jnp.int32, (1, HEAD), 1)
        row = jnp.zeros((1, HEAD), F32)
        for h in range(SWA_H):
            row = jnp.where(lane == h, dsinks[h], row)
        dsp_ref[R_SINK:R_SINK + 1, :] += row

    def rev(n):
        return nb - 1 - n

    qspec = pl.BlockSpec((w, SWA_W), lambda n: (rev(n), 0))
    cur = pl.BlockSpec((w, SWA_KV_W), lambda n: (rev(n), 0))
    prev = pl.BlockSpec((w, SWA_KV_W), lambda n: (jnp.maximum(rev(n) - 1, 0), 0))
    col = pl.BlockSpec((w, HEAD), lambda n: (rev(n), 0))
    small = pl.BlockSpec((16, 128), lambda n: (0, 0))
    return pl.pallas_call(
        body, grid=(nb,),
        in_specs=[qspec, prev, cur, prev, cur, small, pl.BlockSpec((w, SWA_W), lambda n: (rev(n), do_block)), col, col],
        out_specs=[qspec, cur, cur, small],
        out_shape=[jax.ShapeDtypeStruct((t, SWA_W), F32), jax.ShapeDtypeStruct((t, SWA_KV_W), F32),
                   jax.ShapeDtypeStruct((t, SWA_KV_W), F32), jax.ShapeDtypeStruct((16, 128), F32)],
        scratch_shapes=[pltpu.VMEM((w, SWA_KV_W), F32), pltpu.VMEM((w, SWA_KV_W), F32)],
        compiler_params=_params(1), name=name)(sq, sk, sk, sv, sv, sp, dmixed, lse, delta)


def _mem_pre(mkv, sp, name):
    m = mkv.shape[0]

    def body(x_ref, sp_ref, k_ref, v_ref):
        for h in range(MEM_H):
            k_ref[:, _hs(h)] = _head_norm(x_ref[:, _hs(h)], sp_ref[R_MK:R_MK + 1, :]).astype(BF16)
        v_ref[...] = x_ref[:, MEM_W:2 * MEM_W].astype(BF16)

    out = jax.ShapeDtypeStruct((m, MEM_W), BF16)
    return pl.pallas_call(body, out_shape=[out, out], name=name)(mkv, sp)


def _mem_post_bwd(mkv, sp, dmk, dmv, name):
    m = mkv.shape[0]

    def body(x_ref, sp_ref, dk_ref, dv_ref, d_ref, dsp_ref):
        dsp_ref[...] = jnp.zeros_like(dsp_ref)
        total = None
        for h in range(MEM_H):
            dx, dg = _head_norm_bwd(x_ref[:, _hs(h)], sp_ref[R_MK:R_MK + 1, :], dk_ref[:, _hs(h)])
            d_ref[:, _hs(h)] = dx.astype(BF16)
            total = dg if total is None else total + dg
        d_ref[:, MEM_W:2 * MEM_W] = dv_ref[...].astype(BF16)
        dsp_ref[R_MK:R_MK + 1, :] = total

    return pl.pallas_call(body, out_shape=[jax.ShapeDtypeStruct((m, 2 * MEM_W), BF16),
                                           jax.ShapeDtypeStruct((16, 128), F32)], name=name)(mkv, sp, dmk, dmv)


def _mem_fwd(mq, mk, mv, name):
    t = mq.shape[0]
    m = mk.shape[0]
    tq = min(t, 512)

    def body(q_ref, k_ref, v_ref, o_ref, lse_ref):
        lses = []
        for h in range(MEM_H):
            s = _dot(q_ref[:, _hs(h)], k_ref[:, _hs(h)], "nt") * SCALE
            mx = jnp.max(s, axis=-1, keepdims=True)
            p = jnp.exp(s - mx)
            l = jnp.sum(p, axis=-1, keepdims=True)
            o_ref[:, _hs(h)] = _dot(p, v_ref[:, _hs(h)]) / l
            lses.append(mx + jnp.log(l))
        lse_ref[...] = _head_column(lses)

    qspec = pl.BlockSpec((tq, MEM_W), lambda i: (i, 0))
    kspec = pl.BlockSpec((m, MEM_W), lambda i: (0, 0))
    return pl.pallas_call(
        body, grid=(t // tq,), in_specs=[qspec, kspec, kspec],
        out_specs=[qspec, pl.BlockSpec((tq, HEAD), lambda i: (i, 0))],
        out_shape=[jax.ShapeDtypeStruct((t, MEM_W), F32), jax.ShapeDtypeStruct((t, HEAD), F32)],
        compiler_params=_params(1), name=name)(mq, mk, mv)


def _mem_bwd(mq, mk, mv, dmixed, lse, delta, name):
    t = mq.shape[0]
    m = mk.shape[0]
    tq = min(t, 512)
    do_block = (FOX_W + SWA_W) // MEM_W
    assert (FOX_W + SWA_W) % MEM_W == 0

    def body(q_ref, k_ref, v_ref, do_ref, lse_ref, dl_ref, dq_ref, dk_ref, dv_ref):
        @pl.when(pl.program_id(0) == 0)
        def _():
            dk_ref[...] = jnp.zeros_like(dk_ref)
            dv_ref[...] = jnp.zeros_like(dv_ref)

        for h in range(MEM_H):
            q, k, v, do = q_ref[:, _hs(h)], k_ref[:, _hs(h)], v_ref[:, _hs(h)], do_ref[:, _hs(h)]
            s = _dot(q, k, "nt") * SCALE
            p = jnp.exp(s - lse_ref[:, h:h + 1])
            col = FOX_H + SWA_H + h
            ds = p * (_dot(do, v, "nt") - dl_ref[:, col:col + 1])
            dq_ref[:, _hs(h)] = _dot(ds, k) * SCALE
            dk_ref[:, _hs(h)] += _dot(ds, q, "tn") * SCALE
            dv_ref[:, _hs(h)] += _dot(p, do, "tn")

    qspec = pl.BlockSpec((tq, MEM_W), lambda i: (i, 0))
    kspec = pl.BlockSpec((m, MEM_W), lambda i: (0, 0))
    col = pl.BlockSpec((tq, HEAD), lambda i: (i, 0))
    return pl.pallas_call(
        body, grid=(t // tq,),
        in_specs=[qspec, kspec, kspec, pl.BlockSpec((tq, MEM_W), lambda i: (i, do_block)), col, col],
        out_specs=[qspec, kspec, kspec],
        out_shape=[jax.ShapeDtypeStruct((t, MEM_W), F32), jax.ShapeDtypeStruct((m, MEM_W), F32),
                   jax.ShapeDtypeStruct((m, MEM_W), F32)],
        compiler_params=_params(1), name=name)(mq, mk, mv, dmixed, lse, delta)


def _all_gather(xs, name):
    n = len(xs)

    def body(*refs):
        x_refs, o_refs = refs[:n], refs[n:2 * n]
        send_sems, recv_sems, local_sems = refs[2 * n:]
        x, y, c = _me()
        me, sibling = (x, y, c), (x, y, 1 - c)
        chips = [(1 - x, y), (x, 1 - y), (1 - x, 1 - y)]

        def copy(a, k, block, to, src=None):
            slot = o_refs[a].at[_lin(block)]
            return pltpu.make_async_remote_copy(
                src_ref=slot if src is None else src, dst_ref=slot, send_sem=send_sems.at[a, k],
                recv_sem=recv_sems.at[a, k], device_id=to, device_id_type=MESH)

        mine = [pltpu.make_async_copy(x_refs[a], o_refs[a].at[_lin(me)], local_sems.at[a]) for a in range(n)]
        for cp in mine:
            cp.start()
        first = []
        for a in range(n):
            first.append(copy(a, 0, me, sibling, src=x_refs[a]))
            first += [copy(a, 1 + j, me, (*chip, c), src=x_refs[a]) for j, chip in enumerate(chips)]
        for cp in first:
            cp.start()
        passed = []
        for j, chip in enumerate(chips):
            for a in range(n):
                copy(a, 1 + j, (*chip, c), me).wait_recv()
                cp = copy(a, 4 + j, (*chip, c), sibling)
                cp.start()
                passed.append(cp)
        for a in range(n):
            copy(a, 0, sibling, me).wait_recv()
            for j, chip in enumerate(chips):
                copy(a, 4 + j, (*chip, 1 - c), me).wait_recv()
        for cp in first + passed:
            cp.wait_send()
        for cp in mine:
            cp.wait()

    return pl.pallas_call(
        body, in_specs=[ANY] * n, out_specs=[ANY] * n,
        out_shape=[jax.ShapeDtypeStruct((N_DEV,) + x.shape, x.dtype) for x in xs],
        scratch_shapes=[pltpu.SemaphoreType.DMA((n, 7)), pltpu.SemaphoreType.DMA((n, 7)),
                        pltpu.SemaphoreType.DMA((n,))],
        name=name)(*xs)


def _peers():
    x, y, c = _me()
    out = []
    for k in range(1, N_DEV):
        kx, ky, kc = (k >> 2) & 1, (k >> 1) & 1, k & 1
        out.append(((1 - x) if kx else x, (1 - y) if ky else y, (1 - c) if kc else c))
    return out


def _all_reduce_small(xs, name):
    n = len(xs)

    def body(*refs):
        x_refs, o_refs = refs[:n], refs[n:2 * n]
        bufs = refs[2 * n:3 * n]
        send_sems, recv_sems = refs[3 * n:]
        me = _lin(_me())
        peers = _peers()
        for a in range(n):
            bufs[a][me] = x_refs[a][...]
        sends = []
        for a in range(n):
            for k, peer in enumerate(peers):
                sends.append(pltpu.make_async_remote_copy(
                    src_ref=bufs[a].at[me], dst_ref=bufs[a].at[me], send_sem=send_sems.at[a, k],
                    recv_sem=recv_sems.at[a, k], device_id=peer, device_id_type=MESH))
        for cp in sends:
            cp.start()
        for a in range(n):
            for k, peer in enumerate(peers):
                pltpu.make_async_remote_copy(
                    src_ref=bufs[a].at[me], dst_ref=bufs[a].at[_lin(peer)], send_sem=send_sems.at[a, k],
                    recv_sem=recv_sems.at[a, k], device_id=peer, device_id_type=MESH).wait_recv()
        for cp in sends:
            cp.wait_send()
        for a in range(n):
            total = bufs[a][0]
            for q in range(1, N_DEV):
                total = total + bufs[a][q]
            o_refs[a][...] = total

    vmem = pl.BlockSpec(memory_space=pltpu.VMEM)
    return pl.pallas_call(
        body, in_specs=[vmem] * n, out_specs=[vmem] * n,
        out_shape=[jax.ShapeDtypeStruct(x.shape, F32) for x in xs],
        scratch_shapes=[pltpu.VMEM((N_DEV,) + x.shape, F32) for x in xs]
        + [pltpu.SemaphoreType.DMA((n, 7)), pltpu.SemaphoreType.DMA((n, 7))],
        name=name)(*xs)


def _pair_add(part, got, name):
    _, rows, cols = part.shape
    tm = _rows_tile(rows, cols * 2, budget=2 << 20)
    core = jnp.reshape(lax.axis_index("c"), (1,)).astype(jnp.int32)

    def body(c_ref, p_ref, g_ref, o_ref):
        o_ref[...] = (p_ref[...].astype(F32) + g_ref[...].astype(F32)).astype(BF16)

    spec = pl.BlockSpec((None, tm, cols), lambda q, i, c: (q, i, 0))
    grid_spec = pltpu.PrefetchScalarGridSpec(
        num_scalar_prefetch=1, grid=(4, rows // tm),
        in_specs=[pl.BlockSpec((None, tm, cols), lambda q, i, c: (2 * q + c[0], i, 0)), spec], out_specs=spec)
    return pl.pallas_call(body, grid_spec=grid_spec, out_shape=jax.ShapeDtypeStruct((4, rows, cols), BF16),
                          compiler_params=_params(2), name=name)(core, part, got)


def _adam_math(w, g, m, v):
    nm = ADAM_B1 * m + (1.0 - ADAM_B1) * g
    nv = ADAM_B2 * v + (1.0 - ADAM_B2) * (g * g)
    m_hat = nm / (1.0 - ADAM_B1 ** ADAM_STEP)
    v_hat = nv / (1.0 - ADAM_B2 ** ADAM_STEP)
    return -ADAM_LR * (m_hat / (jnp.sqrt(v_hat) + ADAM_EPS) + ADAM_WD * w), nm, nv


def _sum_chips(got, name):
    _, rows, cols = got.shape
    tm = _rows_tile(rows, cols * 2 * 4, budget=2 << 20)

    def body(r_ref, o_ref):
        o_ref[...] = ((r_ref[0].astype(F32) + r_ref[1].astype(F32)) + r_ref[2].astype(F32)) + r_ref[3].astype(F32)

    return pl.pallas_call(
        body, grid=(rows // tm,), in_specs=[pl.BlockSpec((4, tm, cols), lambda i: (0, i, 0))],
        out_specs=pl.BlockSpec((tm, cols), lambda i: (i, 0)), out_shape=jax.ShapeDtypeStruct((rows, cols), F32),
        compiler_params=_params(1), name=name)(got)


def _sum_adamw(got, col_block, w, m, v, name):
    _, rows, cols = w.shape
    tm = _rows_tile(rows, cols * 4, budget=1 << 20)

    def body(r_ref, w_ref, m_ref, v_ref, g_ref, d_ref, nm_ref, nv_ref):
        g = ((r_ref[0].astype(F32) + r_ref[1].astype(F32)) + r_ref[2].astype(F32)) + r_ref[3].astype(F32)
        g_ref[...] = g
        d_ref[...], nm_ref[...], nv_ref[...] = _adam_math(w_ref[...], g, m_ref[...], v_ref[...])

    spec = pl.BlockSpec((None, tm, cols), lambda i: (0, i, 0))
    out = jax.ShapeDtypeStruct(w.shape, F32)
    return pl.pallas_call(
        body, grid=(rows // tm,), in_specs=[pl.BlockSpec((4, tm, cols), lambda i: (0, i, col_block)), spec, spec, spec],
        out_specs=[spec] * 4, out_shape=[out] * 4, compiler_params=_params(1), name=name)(got, w, m, v)


def _adamw(w, g, m, v, name):
    rows, cols = w.shape
    tm = _rows_tile(rows, cols * 4, budget=2 << 20, mult=8)

    def body(w_ref, g_ref, m_ref, v_ref, d_ref, nm_ref, nv_ref):
        d_ref[...], nm_ref[...], nv_ref[...] = _adam_math(w_ref[...], g_ref[...], m_ref[...], v_ref[...])

    spec = pl.BlockSpec((tm, cols), lambda i: (i, 0))
    out = jax.ShapeDtypeStruct(w.shape, F32)
    return pl.pallas_call(body, grid=(rows // tm,), in_specs=[spec] * 4, out_specs=[spec] * 3,
                          out_shape=[out] * 3, compiler_params=_params(1), name=name)(w, g, m, v)


def _permute_in(w):
    logit0 = 3 * FOX_W
    pad = jnp.zeros(w.shape[:-1] + (HEAD - N_LOGIT,), w.dtype)
    return jnp.concatenate([w[..., :logit0], w[..., logit0 + N_LOGIT:], w[..., logit0:logit0 + N_LOGIT], pad], axis=-1)


def _unpermute_in(w):
    logit0 = 3 * FOX_W
    return jnp.concatenate([w[..., :logit0], w[..., C_FL:C_FL + N_LOGIT], w[..., logit0:C_FL]], axis=-1)


def _pad_row(v, width):
    return jnp.pad(v, ((0, 0), (0, width - v.shape[1])))


def _pack_small(fq, fk, sq, sk, mq, mk, fb, sinks):
    rows = [fq, fk, sq, sk, mq, mk, _pad_row(fb, HEAD), _pad_row(sinks, HEAD)]
    return jnp.concatenate(rows + [jnp.zeros((8, HEAD), F32)], axis=0)


def _pack_norms(a, b, c, d):
    return jnp.concatenate([a, b, c, d, jnp.zeros((4, a.shape[1]), F32)], axis=0)


def kernel(x, mem, ffn1_norm, ffn1_gate, ffn1_up, ffn1_down, mix_norm, mem_norm, w_in, forget_bias, w_mem_k, w_mem_v, fox_q_gain, fox_k_gain, swa_q_gain, swa_k_gain, swa_sinks, mem_q_gain, mem_k_gain, w_out, ffn2_norm, ffn2_gate, ffn2_up, ffn2_down, loss_target, m_ffn1_norm, m_ffn1_gate, m_ffn1_up, m_ffn1_down, m_mix_norm, m_mem_norm, m_w_in, m_forget_bias, m_w_mem_k, m_w_mem_v, m_fox_q_gain, m_fox_k_gain, m_swa_q_gain, m_swa_k_gain, m_swa_sinks, m_mem_q_gain, m_mem_k_gain, m_w_out, m_ffn2_norm, m_ffn2_gate, m_ffn2_up, m_ffn2_down, v_ffn1_norm, v_ffn1_gate, v_ffn1_up, v_ffn1_down, v_mix_norm, v_mem_norm, v_w_in, v_forget_bias, v_w_mem_k, v_w_mem_v, v_fox_q_gain, v_fox_k_gain, v_swa_q_gain, v_swa_k_gain, v_swa_sinks, v_mem_q_gain, v_mem_k_gain, v_w_out, v_ffn2_norm, v_ffn2_gate, v_ffn2_up, v_ffn2_down):
    x0 = x[0]
    mem0 = mem[0]
    target = loss_target[0]
    t, d = x0.shape
    d_shard = w_in.shape[1]
    m_len = mem0.shape[0]
    tm = min(t, 512)
    tk = min(t, 512)
    tn = IN_W // 3
    tkw, tnw = min(t, 1024), IN_W // 3

    def swap(a):
        return jnp.swapaxes(a, 1, 2)

    gate1, up1, gate2, up2 = swap(ffn1_gate), swap(ffn1_up), swap(ffn2_gate), swap(ffn2_up)

    local = {
        "g1": gate1[0], "u1": up1[0], "d1": ffn1_down[0],
        "g2": gate2[0], "u2": up2[0], "d2": ffn2_down[0],
        "in": _permute_in(w_in[0]), "out": w_out[0],
        "mkv": jnp.concatenate([w_mem_k[0], w_mem_v[0]], axis=1),
    }
    shard = {k: _cast_bf16(v, f"cast_{k}") for k, v in local.items()}
    sp = _pack_small(fox_q_gain, fox_k_gain, swa_q_gain, swa_k_gain, mem_q_gain, mem_k_gain, forget_bias, swa_sinks)
    wt = {}

    wt["g1"], wt["u1"] = _all_gather([shard["g1"], shard["u1"]], "gather_ffn1_in")
    xn1 = _rms_fwd(x0, ffn1_norm, "ffn1_norm")
    (g1, u1, h1), ((wt["d1"], wt["in"]),) = _ffn_up(
        xn1, wt["g1"], wt["u1"], "ffn1", rides=[_ride_gather([shard["d1"], shard["in"]], 0.87)])
    tc = min(t, CONTRACT_ROWS)
    (x1, hn), ((wt["out"], wt["mkv"]),) = _ffn_down(
        x0, h1, wt["d1"], "ffn1", rides=[_ride_gather([shard["out"], shard["mkv"]], 0.6)],
        tail=_tail_norm(mix_norm, t, d, tc))
    w_in_full = wt["in"].reshape(d, IN_W)

    proj, (half,) = _mm(
        "proj", [(hn, pl.BlockSpec((tm, d), lambda n, i, k: (i, 0)),
                  w_in_full, pl.BlockSpec((d, tn), lambda n, i, k: (0, n)))],
        "nn", (3, t // tm, 1), jax.ShapeDtypeStruct((t, IN_W), F32), pl.BlockSpec((tm, tn), lambda n, i, k: (i, n)),
        rides=[_ride_gather_chips([shard["g2"]])])
    w_out_full = wt["out"].reshape(d, d)
    w_mkv_full = wt["mkv"].reshape(d, 2 * MEM_W)
    fq, fk, fv, sq, sk, sv, mq, c_col = _attn_pre(proj, sp, "attn_pre")
    c_row = jnp.transpose(c_col[:, :8])
    c_rep = jnp.broadcast_to(c_row[:FOX_H, :, None], (FOX_H, t, HEAD))

    mn = _rms_fwd(mem0, mem_norm, "mem_norm")
    mkv = _mm("mem_kv", [(mn, pl.BlockSpec((m_len, d), lambda k: (0, 0)),
                          w_mkv_full, pl.BlockSpec((d, 2 * MEM_W), lambda k: (0, 0)))],
              "nn", (1,), jax.ShapeDtypeStruct((m_len, 2 * MEM_W), F32),
              pl.BlockSpec((m_len, 2 * MEM_W), lambda k: (0, 0)))
    mk, mv = _mem_pre(mkv, sp, "mem_pre")

    (o_a, lse_a), ((wt["g2"],), half) = _fox_fwd(
        fq, fk, fv, c_rep, c_row, "fox_fwd", rides=[_ride_gather_sibling(half), _ride_gather_chips([shard["u2"]])])
    (o_b, lse_b), ((wt["u2"],),) = _swa_fwd(sq, sk, sv, sp, "swa_fwd", rides=[_ride_gather_sibling(half)])
    o_c, lse_c = _mem_fwd(mq, mk, mv, "mem_fwd")

    def rows_spec(width):
        return pl.BlockSpec((tm, width), lambda i, k: (i, 0))

    def wout_rows(first, width):
        assert first % width == 0
        return pl.BlockSpec((width, d), lambda i, k: (first // width, 0), pipeline_mode=pl.Buffered(1))

    xspec = pl.BlockSpec((tm, d), lambda i, k: (i, 0))
    x2, xn2 = _mm(
        "mix_out",
        [(o_a, rows_spec(FOX_W), w_out_full, wout_rows(0, FOX_W)),
         (o_b, rows_spec(SWA_W), w_out_full, wout_rows(FOX_W, SWA_W)),
         (o_c, rows_spec(MEM_W), w_out_full, wout_rows(FOX_W + SWA_W, MEM_W))],
        "nn", (t // tm, 1), jax.ShapeDtypeStruct((t, d), F32), xspec, res=x1, res_spec=xspec,
        tail=_tail_norm(ffn2_norm, t, d, tm))

    (g2, u2, h2), ((wt["d2"],),) = _ffn_up(xn2, wt["g2"], wt["u2"], "ffn2", rides=[_ride_gather([shard["d2"]], 0.75)])
    dy, dyb, sq_err = _ffn_down(x2, h2, wt["d2"], "ffn2", tail=_tail_loss(target, t, d, tc))
    loss = lax.psum(0.5 * sq_err[0, 0] / d, ("x", "y", "c"))

    got = {}
    paired = {}
    landed = {}

    def pair(k, part):
        paired[k] = _pair_add(part, got[k], f"pair_{k}")

    (dg2, du2), _ = _ffn_dact(dyb, wt["d2"], g2, u2, "ffn2")
    part_d2 = _ffn_dw(h2, dyb, 0.5, "ffn2_dwd")
    part_g2, ((got["d2"],),) = _ffn_dw(dg2, xn2, 1.0, "ffn2_dwg", rides=[_ride_scatter_sibling([part_d2])])
    pair("d2", part_d2)
    half_rows = part_d2.shape[1] // 2
    first, second = (0, half_rows), (half_rows, half_rows)
    part_u2, (half, (got["g2"],)) = _ffn_dw(
        du2, xn2, 1.0, "ffn2_dwu",
        rides=[_ride_scatter_chips([paired["d2"]], first), _ride_scatter_sibling([part_g2])])
    pair("g2", part_g2)
    dxn2, ((landed["d2"],),) = _ffn_contract(
        dg2, wt["g2"], "ffn2_dxn_g", rides=[_ride_scatter_chips([paired["d2"]], second, into=half)])
    (dx2, dx2b, dgain_ffn2), (half_g2, (got["u2"],)) = _ffn_contract(
        du2, wt["u2"], "ffn2_dxn_u", res=dxn2,
        rides=[_ride_scatter_chips([paired["g2"]], first), _ride_scatter_sibling([part_u2])],
        tail=_tail_norm_bwd(x2, ffn2_norm, dy, t, d, tc))
    pair("u2", part_u2)

    dmixed = _mm("mix_out_dx", [(dx2b, xspec, w_out_full, pl.BlockSpec((d, d), lambda i, k: (0, 0)))],
                 "nt", (t // tm, 1), jax.ShapeDtypeStruct((t, d), F32), xspec)

    def k_rows(width):
        return pl.BlockSpec((tk, width), lambda j, k: (k, 0))

    part_out = [
        _mm(f"mix_out_dw{i}", [(o, k_rows(width), dx2b, k_rows(d))], "tn", (1, t // tk),
            jax.ShapeDtypeStruct((width, d), BF16), pl.BlockSpec((width, d), lambda j, k: (0, 0)))
        for i, (o, width) in enumerate(((o_a, FOX_W), (o_b, SWA_W), (o_c, MEM_W)))
    ]
    part_out = jnp.concatenate(part_out, axis=0).reshape(N_DEV, d_shard, d)

    delta, delta_rep = _delta(dmixed, o_a, o_b, o_c, "attn_delta")
    (dfq, dfk, dfv, dc_col, dc_row), ((landed["g2"],), (landed["u2"],)) = _fox_bwd(
        fq, fk, fv, c_rep, c_row, dmixed, lse_a, delta_rep, "fox_bwd",
        rides=[_ride_scatter_chips([paired["g2"]], second, into=half_g2), _ride_scatter_chips([paired["u2"]])])
    dsq, dsk, dsv, dsp_sink = _swa_bwd(sq, sk, sv, sp, dmixed, lse_b, delta, "swa_bwd")
    dmq, dmk, dmv = _mem_bwd(mq, mk, mv, dmixed, lse_c, delta, "mem_bwd")

    dmkv, dsp_mem = _mem_post_bwd(mkv, sp, dmk, dmv, "mem_post_bwd")
    part_mkv = _mm("mem_kv_dw", [(mn, pl.BlockSpec((m_len, d), lambda k: (0, 0)),
                                  dmkv, pl.BlockSpec((m_len, 2 * MEM_W), lambda k: (0, 0)))],
                   "tn", (1,), jax.ShapeDtypeStruct((d, 2 * MEM_W), BF16),
                   pl.BlockSpec((d, 2 * MEM_W), lambda k: (0, 0))).reshape(N_DEV, d_shard, 2 * MEM_W)
    dmn = _mm("mem_kv_dx", [(dmkv, pl.BlockSpec((m_len, 2 * MEM_W), lambda k: (0, 0)),
                             w_mkv_full, pl.BlockSpec((d, 2 * MEM_W), lambda k: (0, 0)))],
              "nt", (1,), jax.ShapeDtypeStruct((m_len, d), F32), pl.BlockSpec((m_len, d), lambda k: (0, 0)))
    _, _, dgain_mem = _rms_bwd(mem0, mem_norm, dmn, None, "mem_norm_bwd")

    dc_row_t = _pad_row(jnp.transpose(dc_row), HEAD)
    dproj, dsp_attn = _attn_post_bwd(proj, sp, dfq, dfk, dfv, dsq, dsk, dsv, dmq, dc_col, dc_row_t, "attn_post_bwd")
    (dx1, dx1b, dgain_mix), ((got["out"], got["mkv"]),) = _mm(
        "proj_dx", [(dproj, pl.BlockSpec((tc, IN_W), lambda i, k: (i, 0)),
                     w_in_full, pl.BlockSpec((d, IN_W), lambda i, k: (0, 0), pipeline_mode=pl.Buffered(1)))],
        "nt", (t // tc, 1), jax.ShapeDtypeStruct((t, d), F32), pl.BlockSpec((tc, d), lambda i, k: (i, 0)),
        rides=[_ride_scatter_sibling([part_out, part_mkv])], tail=_tail_norm_bwd(x1, mix_norm, dx2, t, d, tc))
    pair("out", part_out)
    pair("mkv", part_mkv)
    part_in, ((landed["out"], landed["mkv"]),) = _mm(
        "proj_dw", [(hn, pl.BlockSpec((tkw, d), lambda n, k: (k, 0)),
                     dproj, pl.BlockSpec((tkw, tnw), lambda n, k: (k, n)))],
        "tn", (IN_W // tnw, t // tkw), jax.ShapeDtypeStruct((d, IN_W), BF16), pl.BlockSpec((d, tnw), lambda n, k: (0, n)),
        rides=[_ride_scatter_chips([paired["out"], paired["mkv"]])])
    part_in = part_in.reshape(N_DEV, d_shard, IN_W)

    (dg1, du1), ((got["in"],),) = _ffn_dact(dx1b, wt["d1"], g1, u1, "ffn1", rides=[_ride_scatter_sibling([part_in])])
    pair("in", part_in)
    part_d1, ((landed["in"],),) = _ffn_dw(h1, dx1b, 0.5, "ffn1_dwd", rides=[_ride_scatter_chips([paired["in"]])])
    part_g1, ((got["d1"],),) = _ffn_dw(dg1, xn1, 1.0, "ffn1_dwg", rides=[_ride_scatter_sibling([part_d1])])
    pair("d1", part_d1)
    part_u1, (half_d1, (got["g1"],)) = _ffn_dw(
        du1, xn1, 1.0, "ffn1_dwu",
        rides=[_ride_scatter_chips([paired["d1"]], first), _ride_scatter_sibling([part_g1])])
    pair("g1", part_g1)
    dxn1, ((landed["d1"],), half_g1, (got["u1"],)) = _ffn_contract(
        dg1, wt["g1"], "ffn1_dxn_g",
        rides=[_ride_scatter_chips([paired["d1"]], second, into=half_d1), _ride_scatter_chips([paired["g1"]], first),
               _ride_scatter_sibling([part_u1])])
    pair("u1", part_u1)
    (grad_x, _, dgain_ffn1), ((landed["g1"],), (landed["u1"],)) = _ffn_contract(
        du1, wt["u1"], "ffn1_dxn_u", res=dxn1,
        rides=[_ride_scatter_chips([paired["g1"]], second, into=half_g1), _ride_scatter_chips([paired["u1"]])],
        tail=_tail_norm_bwd(x0, ffn1_norm, dx1, t, d, tc))

    norms_sum, small_sum = _all_reduce_small(
        [_pack_norms(dgain_ffn1, dgain_mix, dgain_mem, dgain_ffn2), dsp_attn + dsp_sink + dsp_mem], "reduce_small")

    result = {
        "ffn1_gate": map(swap, _sum_adamw(landed["g1"], 0, gate1, swap(m_ffn1_gate), swap(v_ffn1_gate), "adamw_ffn1_gate")),
        "ffn1_up": map(swap, _sum_adamw(landed["u1"], 0, up1, swap(m_ffn1_up), swap(v_ffn1_up), "adamw_ffn1_up")),
        "ffn1_down": _sum_adamw(landed["d1"], 0, ffn1_down, m_ffn1_down, v_ffn1_down, "adamw_ffn1_down"),
        "w_mem_k": _sum_adamw(landed["mkv"], 0, w_mem_k, m_w_mem_k, v_w_mem_k, "adamw_w_mem_k"),
        "w_mem_v": _sum_adamw(landed["mkv"], 1, w_mem_v, m_w_mem_v, v_w_mem_v, "adamw_w_mem_v"),
        "w_out": _sum_adamw(landed["out"], 0, w_out, m_w_out, v_w_out, "adamw_w_out"),
        "ffn2_gate": map(swap, _sum_adamw(landed["g2"], 0, gate2, swap(m_ffn2_gate), swap(v_ffn2_gate), "adamw_ffn2_gate")),
        "ffn2_up": map(swap, _sum_adamw(landed["u2"], 0, up2, swap(m_ffn2_up), swap(v_ffn2_up), "adamw_ffn2_up")),
        "ffn2_down": _sum_adamw(landed["d2"], 0, ffn2_down, m_ffn2_down, v_ffn2_down, "adamw_ffn2_down"),
    }
    grad_in = _unpermute_in(_sum_chips(landed["in"], "sum_w_in"))
    result["w_in"] = (grad_in[None],) + tuple(
        o[None] for o in _adamw(w_in[0], grad_in, m_w_in[0], v_w_in[0], "adamw_w_in"))

    norm_names = ["ffn1_norm", "mix_norm", "mem_norm", "ffn2_norm"]
    norm_w = _pack_norms(ffn1_norm, mix_norm, mem_norm, ffn2_norm)
    norm_m = _pack_norms(m_ffn1_norm, m_mix_norm, m_mem_norm, m_ffn2_norm)
    norm_v = _pack_norms(v_ffn1_norm, v_mix_norm, v_mem_norm, v_ffn2_norm)
    outs = (norms_sum,) + tuple(_adamw(norm_w, norms_sum, norm_m, norm_v, "adamw_norms"))
    for i, k in enumerate(norm_names):
        result[k] = tuple(o[i:i + 1] for o in outs)

    small_names = ["fox_q_gain", "fox_k_gain", "swa_q_gain", "swa_k_gain", "mem_q_gain", "mem_k_gain",
                   "forget_bias", "swa_sinks"]
    small_m = _pack_small(m_fox_q_gain, m_fox_k_gain, m_swa_q_gain, m_swa_k_gain, m_mem_q_gain, m_mem_k_gain,
                          m_forget_bias, m_swa_sinks)
    small_v = _pack_small(v_fox_q_gain, v_fox_k_gain, v_swa_q_gain, v_swa_k_gain, v_mem_q_gain, v_mem_k_gain,
                          v_forget_bias, v_swa_sinks)
    outs = (small_sum,) + tuple(_adamw(sp, small_sum, small_m, small_v, "adamw_small"))
    for i, k in enumerate(small_names):
        width = N_LOGIT if k in ("forget_bias", "swa_sinks") else HEAD
        result[k] = tuple(o[i:i + 1, :width] for o in outs)

    order = ["ffn1_norm", "ffn1_gate", "ffn1_up", "ffn1_down", "mix_norm", "mem_norm", "w_in", "forget_bias",
             "w_mem_k", "w_mem_v", "fox_q_gain", "fox_k_gain", "swa_q_gain", "swa_k_gain", "swa_sinks",
             "mem_q_gain", "mem_k_gain", "w_out", "ffn2_norm", "ffn2_gate", "ffn2_up", "ffn2_down"]
    result = {k: tuple(v) for k, v in result.items()}
    flat = [loss, grad_x[None]]
    for kind in range(4):
        flat += [result[k][kind] for k in order]
    return tuple(flat)
```

```python
import functools

import jax
import jax.numpy as jnp
from jax import lax
from jax.experimental import pallas as pl
from jax.experimental.pallas import tpu as pltpu

F32 = jnp.float32
BF16 = jnp.bfloat16
MESH = pl.DeviceIdType.MESH
ANY = pl.BlockSpec(memory_space=pl.ANY)

N_DEV = 8
EPS = 1e-6
NEG_INF = -1e30
HEAD = 128
FOX_H, SWA_H, SWA_KV, MEM_H = 6, 6, 2, 4
FOX_W, SWA_W, SWA_KV_W, MEM_W = FOX_H * HEAD, SWA_H * HEAD, SWA_KV * HEAD, MEM_H * HEAD
SCALE = HEAD ** -0.5
SWA_BLOCK = 128
C_FQ, C_FK, C_FV = 0, FOX_W, 2 * FOX_W
C_SQ = 3 * FOX_W
C_SK = C_SQ + SWA_W
C_SV = C_SK + SWA_KV_W
C_MQ = C_SV + SWA_KV_W
C_FL = C_MQ + MEM_W
IN_W = C_FL + HEAD
N_LOGIT = FOX_H
R_FQ, R_FK, R_SQ, R_SK, R_MQ, R_MK, R_FB, R_SINK = range(8)
ADAM_LR, ADAM_B1, ADAM_B2, ADAM_EPS, ADAM_WD, ADAM_STEP = 0.001, 0.9, 0.999, 1e-08, 0.01, 10
VMEM_BYTES = 56 * 1024 * 1024

DN = {
    "nn": (((1,), (0,)), ((), ())),
    "nt": (((1,), (1,)), ((), ())),
    "tn": (((0,), (0,)), ((), ())),
}


def _params(n_axes):
    return pltpu.CompilerParams(dimension_semantics=("arbitrary",) * n_axes, vmem_limit_bytes=VMEM_BYTES)


def _dot(a, b, dims="nn"):
    return lax.dot_general(a.astype(BF16), b.astype(BF16), DN[dims], preferred_element_type=F32)


def _sigmoid(x):
    return 0.5 * jnp.tanh(0.5 * x) + 0.5


def _me():
    return lax.axis_index("x"), lax.axis_index("y"), lax.axis_index("c")


def _lin(p):
    return 4 * p[0] + 2 * p[1] + p[2]


def _rows_tile(rows, row_bytes, budget=4 << 20, mult=16):
    best = None
    for k in range(1, rows + 1):
        if rows % k == 0 and (rows // k) % mult == 0 and (rows // k) * row_bytes <= budget:
            best = rows // k
            break
    assert best is not None, (rows, row_bytes)
    return best


class _Ride:
    def __init__(self, inputs, out_shapes, aliases, n_remote, n_local, start, wait):
        self.inputs, self.out_shapes, self.aliases = list(inputs), list(out_shapes), dict(aliases)
        self.n_remote, self.n_local, self.start, self.wait = n_remote, n_local, start, wait


def _remote(src, dst, send, recv, k, to):
    return pltpu.make_async_remote_copy(src_ref=src, dst_ref=dst, send_sem=send.at[k], recv_sem=recv.at[k],
                                        device_id=to, device_id_type=MESH)


def _other_chips(x, y):
    return [(1 - x, y), (x, 1 - y), (1 - x, 1 - y)]


ALL_CHIPS = [(0, 0), (0, 1), (1, 0), (1, 1)]


def _rows_of(ref, rows, slot=None):
    if slot is None:
        return ref if rows is None else ref.at[pl.ds(rows[0], rows[1])]
    return ref.at[slot] if rows is None else ref.at[slot, pl.ds(rows[0], rows[1])]


def _ride_gather_chips(xs, rows=None, into=None):
    n = len(xs)

    def copies(ins, outs, send, recv):
        x, y, c = _me()
        out = []
        for a in range(n):
            for j, chip in enumerate(_other_chips(x, y)):
                peer = (*chip, c)
                src = _rows_of(ins[a], rows)
                out.append((_remote(src, _rows_of(outs[a], rows, _lin((x, y, c))), send, recv, 3 * a + j, peer),
                            _remote(src, _rows_of(outs[a], rows, _lin(peer)), send, recv, 3 * a + j, peer)))
        return out

    def mine(ins, outs, local):
        me = _lin(_me())
        return [pltpu.make_async_copy(_rows_of(ins[a], rows), _rows_of(outs[a], rows, me), local.at[a])
                for a in range(n)]

    def start(ins, outs, send, recv, local):
        for cp in mine(ins, outs, local):
            cp.start()
        for sent, _ in copies(ins, outs, send, recv):
            sent.start()

    def wait(ins, outs, send, recv, local):
        for sent, landed in copies(ins, outs, send, recv):
            landed.wait_recv()
            sent.wait_send()
        for cp in mine(ins, outs, local):
            cp.wait()

    shapes = [jax.ShapeDtypeStruct((N_DEV,) + x.shape, x.dtype) for x in xs]
    if into is None:
        return _Ride(xs, shapes, {}, 3 * n, n, start, wait)
    return _Ride(list(xs) + list(into), shapes, {n + a: a for a in range(n)}, 3 * n, n, start, wait)


def _ride_gather_sibling(bufs):
    n = len(bufs)

    def copies(outs, send, recv):
        x, y, c = _me()
        out = []
        for a in range(n):
            for q, (px, py) in enumerate(ALL_CHIPS):
                there = outs[a].at[4 * px + 2 * py + c]
                here = outs[a].at[4 * px + 2 * py + 1 - c]
                out.append((_remote(there, there, send, recv, 4 * a + q, (x, y, 1 - c)),
                            _remote(here, here, send, recv, 4 * a + q, (x, y, 1 - c))))
        return out

    def start(ins, outs, send, recv, local):
        for sent, _ in copies(outs, send, recv):
            sent.start()

    def wait(ins, outs, send, recv, local):
        for sent, landed in copies(outs, send, recv):
            landed.wait_recv()
            sent.wait_send()

    shapes = [jax.ShapeDtypeStruct(b.shape, b.dtype) for b in bufs]
    return _Ride(bufs, shapes, {a: a for a in range(n)}, 4 * n, 0, start, wait)


def _ride_gather(xs, mid_frac, rows=None, into=None):
    n = len(xs)
    chips = _ride_gather_chips(xs, rows, into)

    def sibling_copies(outs, send, recv):
        x, y, c = _me()
        out = []
        for a in range(n):
            for q, (px, py) in enumerate(ALL_CHIPS):
                there = _rows_of(outs[a], rows, 4 * px + 2 * py + c)
                here = _rows_of(outs[a], rows, 4 * px + 2 * py + 1 - c)
                k = 3 * n + 4 * a + q
                out.append((_remote(there, there, send, recv, k, (x, y, 1 - c)),
                            _remote(here, here, send, recv, k, (x, y, 1 - c))))
        return out

    def mid(ins, outs, send, recv, local):
        chips.wait(ins, outs, send, recv, local)
        for sent, _ in sibling_copies(outs, send, recv):
            sent.start()

    def wait(ins, outs, send, recv, local):
        for sent, landed in sibling_copies(outs, send, recv):
            landed.wait_recv()
            sent.wait_send()

    ride = _Ride(chips.inputs, chips.out_shapes, chips.aliases, 7 * n, n, chips.start, wait)
    ride.mid, ride.mid_frac = mid, mid_frac
    return ride


def _ride_scatter_sibling(parts):
    n = len(parts)

    def copies(ins, outs, send, recv):
        x, y, c = _me()
        out = []
        for a in range(n):
            for q, (px, py) in enumerate(ALL_CHIPS):
                cp = _remote(ins[a].at[4 * px + 2 * py + 1 - c], outs[a].at[q], send, recv, 4 * a + q, (x, y, 1 - c))
                out.append(cp)
        return out

    def start(ins, outs, send, recv, local):
        for cp in copies(ins, outs, send, recv):
            cp.start()

    def wait(ins, outs, send, recv, local):
        for cp in copies(ins, outs, send, recv):
            cp.wait_recv()
            cp.wait_send()

    shapes = [jax.ShapeDtypeStruct((4,) + p.shape[1:], p.dtype) for p in parts]
    return _Ride(parts, shapes, {}, 4 * n, 0, start, wait)


def _ride_scatter_chips(pairs, rows=None, into=None):
    n = len(pairs)

    def part(ref, slot):
        return _rows_of(ref, rows, slot)

    def copies(ins, outs, send, recv):
        x, y, c = _me()
        out = []
        for a in range(n):
            for j, (px, py) in enumerate(_other_chips(x, y)):
                peer = (px, py, c)
                src = part(ins[a], 2 * px + py)
                out.append((_remote(src, part(outs[a], 2 * x + y), send, recv, 3 * a + j, peer),
                            _remote(src, part(outs[a], 2 * px + py), send, recv, 3 * a + j, peer)))
        return out

    def mine(ins, outs, local):
        x, y, _ = _me()
        return [pltpu.make_async_copy(part(ins[a], 2 * x + y), part(outs[a], 2 * x + y), local.at[a])
                for a in range(n)]

    def start(ins, outs, send, recv, local):
        for cp in mine(ins, outs, local):
            cp.start()
        for sent, _ in copies(ins, outs, send, recv):
            sent.start()

    def wait(ins, outs, send, recv, local):
        for sent, landed in copies(ins, outs, send, recv):
            landed.wait_recv()
            sent.wait_send()
        for cp in mine(ins, outs, local):
            cp.wait()

    shapes = [jax.ShapeDtypeStruct(p.shape, p.dtype) for p in pairs]
    if into is None:
        return _Ride(pairs, shapes, {}, 3 * n, n, start, wait)
    return _Ride(list(pairs) + list(into), shapes, {n + a: a for a in range(n)}, 3 * n, n, start, wait)


def _call(name, body, grid, in_specs, out_specs, out_shape, operands, scratch=(), rides=()):
    n_in, n_out, n_scr = len(operands), len(out_shape), len(scratch)
    ride_in, ride_out, ride_scr, aliases, spans = [], [], [], {}, []
    for r in rides:
        for i, o in r.aliases.items():
            aliases[n_in + len(ride_in) + i] = n_out + len(ride_out) + o
        spans.append((len(ride_in), len(r.inputs), len(ride_out), len(r.out_shapes)))
        ride_in += r.inputs
        ride_out += r.out_shapes
        ride_scr += [pltpu.SemaphoreType.DMA((r.n_remote,)), pltpu.SemaphoreType.DMA((r.n_remote,)),
                     pltpu.SemaphoreType.DMA((max(r.n_local, 1),))]

    def wrapped(*refs):
        c_in, r_in = refs[:n_in], refs[n_in:n_in + len(ride_in)]
        p = n_in + len(ride_in)
        c_out, r_out = refs[p:p + n_out], refs[p + n_out:p + n_out + len(ride_out)]
        p += n_out + len(ride_out)
        c_scr, r_scr = refs[p:p + n_scr], refs[p + n_scr:]

        n_steps = functools.reduce(lambda a, b: a * b, grid, 1)
        step = functools.reduce(lambda acc, ax: acc * grid[ax] + pl.program_id(ax), range(len(grid)), 0)

        def each(method, at):
            for k, (r, (i0, ni, o0, no)) in enumerate(zip(rides, spans)):
                fn = getattr(r, method, None)
                if fn is None:
                    continue
                run = functools.partial(fn, r_in[i0:i0 + ni], r_out[o0:o0 + no], *r_scr[3 * k:3 * k + 3])
                if grid:
                    pl.when(step == at(r))(run)
                else:
                    run()

        each("start", lambda r: 0)
        each("mid", lambda r: int(r.mid_frac * (n_steps - 1)))
        body(*c_in, *c_out, *c_scr)
        each("wait", lambda r: n_steps - 1)

    outs = pl.pallas_call(
        wrapped, grid=grid, in_specs=list(in_specs) + [ANY] * len(ride_in),
        out_specs=list(out_specs) + [ANY] * len(ride_out), out_shape=list(out_shape) + ride_out,
        scratch_shapes=list(scratch) + ride_scr, input_output_aliases=aliases,
        compiler_params=_params(len(grid)), name=name)(*operands, *ride_in)
    outs = list(outs)
    ride_results = [outs[n_out + o0:n_out + o0 + no] for (_, _, o0, no) in spans]
    return outs[:n_out], ride_results


def _only_copies(name, rides):
    return _call(name, lambda: None, (), [], [], [], [], rides=rides)[1]


class _Tail:
    def __init__(self, extra, out_shapes, out_specs, fn):
        self.extra, self.out_shapes, self.out_specs, self.fn = list(extra), list(out_shapes), list(out_specs), fn


def _mm(name, pairs, dims, grid, out_shape, out_spec, res=None, res_spec=None, alpha=1.0, rides=(), tail=None):
    n = len(pairs)
    nk = grid[-1]
    kax = len(grid) - 1
    acc_shape = tuple(d for d in out_spec.block_shape if d is not None)
    n_extra = len(tail.extra) if tail else 0
    n_outs = len(tail.out_shapes) if tail else 1

    def body(*refs):
        pos = 2 * n
        r_ref = None
        if res is not None:
            r_ref = refs[pos]
            pos += 1
        x_refs = refs[pos:pos + n_extra]
        o_refs = refs[pos + n_extra:pos + n_extra + n_outs]
        pos += n_extra + n_outs
        part = None
        for p in range(n):
            d = _dot(refs[2 * p][...], refs[2 * p + 1][...], dims)
            part = d if part is None else part + d

        def finish(acc):
            if alpha != 1.0:
                acc = acc * alpha
            if r_ref is not None:
                acc = r_ref[...] + acc
            if tail:
                tail.fn(acc, x_refs, o_refs)
            else:
                o_refs[0][...] = acc.astype(o_refs[0].dtype)

        if nk == 1:
            finish(part)
        else:
            acc_ref = refs[pos]
            k = pl.program_id(kax)

            @pl.when(k == 0)
            def _():
                acc_ref[...] = part

            @pl.when(k > 0)
            def _():
                acc_ref[...] += part

            @pl.when(k == nk - 1)
            def _():
                finish(acc_ref[...])

    operands, in_specs = [], []
    for a, a_spec, b, b_spec in pairs:
        operands += [a, b]
        in_specs += [a_spec, b_spec]
    if res is not None:
        operands.append(res)
        in_specs.append(res_spec)
    for a, a_spec in (tail.extra if tail else []):
        operands.append(a)
        in_specs.append(a_spec)
    outs, ride_results = _call(name, body, grid, in_specs, tail.out_specs if tail else [out_spec],
                               tail.out_shapes if tail else [out_shape], operands,
                               scratch=[pltpu.VMEM(acc_shape, F32)] if nk > 1 else [], rides=rides)
    outs = outs if tail else outs[0]
    return (outs, ride_results) if rides else outs


def _accumulate(ref, part):
    @pl.when(pl.program_id(0) == 0)
    def _():
        ref[...] = part

    @pl.when(pl.program_id(0) > 0)
    def _():
        ref[...] += part


def _tail_norm(gain, t, d, tm):
    def fn(v, x_refs, o_refs):
        o_refs[0][...] = v
        r = lax.rsqrt(jnp.mean(v * v, axis=-1, keepdims=True) + EPS)
        o_refs[1][...] = (v * r * x_refs[0][...]).astype(BF16)

    rows = pl.BlockSpec((tm, d), lambda i, k: (i, 0))
    return _Tail([(gain, pl.BlockSpec((1, d), lambda i, k: (0, 0)))],
                 [jax.ShapeDtypeStruct((t, d), F32), jax.ShapeDtypeStruct((t, d), BF16)], [rows, rows], fn)


def _tail_loss(target, t, d, tm):
    def fn(v, x_refs, o_refs):
        err = v - x_refs[0][...]
        dy = err * (1.0 / d)
        o_refs[0][...] = dy
        o_refs[1][...] = dy.astype(BF16)
        _accumulate(o_refs[2], jnp.zeros((8, 128), F32) + jnp.sum(err * err))

    rows = pl.BlockSpec((tm, d), lambda i, k: (i, 0))
    return _Tail([(target, rows)],
                 [jax.ShapeDtypeStruct((t, d), F32), jax.ShapeDtypeStruct((t, d), BF16),
                  jax.ShapeDtypeStruct((8, 128), F32)],
                 [rows, rows, pl.BlockSpec((8, 128), lambda i, k: (0, 0))], fn)


def _tail_norm_bwd(x, gain, dres, t, d, tm):
    def fn(dy, x_refs, o_refs):
        xv = x_refs[0][...]
        r = lax.rsqrt(jnp.mean(xv * xv, axis=-1, keepdims=True) + EPS)
        xh = xv * r
        dxh = dy * x_refs[1][...]
        dx = r * (dxh - xh * jnp.mean(dxh * xh, axis=-1, keepdims=True)) + x_refs[2][...]
        o_refs[0][...] = dx
        o_refs[1][...] = dx.astype(BF16)
        _accumulate(o_refs[2], jnp.sum(dy * xh, axis=0, keepdims=True))

    rows = pl.BlockSpec((tm, d), lambda i, k: (i, 0))
    vec = pl.BlockSpec((1, d), lambda i, k: (0, 0))
    return _Tail([(x, rows), (gain, vec), (dres, rows)],
                 [jax.ShapeDtypeStruct((t, d), F32), jax.ShapeDtypeStruct((t, d), BF16),
                  jax.ShapeDtypeStruct((1, d), F32)], [rows, rows, vec], fn)


def _cast_bf16(x, name):
    rows, cols = x.shape
    tm = _rows_tile(rows, cols * 4)

    def body(x_ref, o_ref):
        o_ref[...] = x_ref[...].astype(BF16)

    spec = pl.BlockSpec((tm, cols), lambda i: (i, 0))
    return pl.pallas_call(body, grid=(rows // tm,), in_specs=[spec], out_specs=spec,
                          out_shape=jax.ShapeDtypeStruct(x.shape, BF16), compiler_params=_params(1), name=name)(x)


def _rms_fwd(x, gain, name):
    rows, d = x.shape
    tm = min(rows, 512)

    def body(x_ref, g_ref, o_ref):
        xv = x_ref[...]
        r = lax.rsqrt(jnp.mean(xv * xv, axis=-1, keepdims=True) + EPS)
        o_ref[...] = (xv * r * g_ref[...]).astype(BF16)

    spec = pl.BlockSpec((tm, d), lambda i: (i, 0))
    return pl.pallas_call(body, grid=(rows // tm,), in_specs=[spec, pl.BlockSpec((1, d), lambda i: (0, 0))],
                          out_specs=spec, out_shape=jax.ShapeDtypeStruct(x.shape, BF16),
                          compiler_params=_params(1), name=name)(x, gain)


def _rms_bwd(x, gain, dxn, dres, name, rides=()):
    rows, d = x.shape
    tm = min(rows, 256)
    with_res = dres is not None

    def body(*refs):
        if with_res:
            x_ref, g_ref, dy_ref, r_ref, dx_ref, dxb_ref, dg_ref = refs
        else:
            x_ref, g_ref, dy_ref, dx_ref, dxb_ref, dg_ref = refs
        xv = x_ref[...]
        r = lax.rsqrt(jnp.mean(xv * xv, axis=-1, keepdims=True) + EPS)
        xh = xv * r
        dy = dy_ref[...]
        dxh = dy * g_ref[...]
        dx = r * (dxh - xh * jnp.mean(dxh * xh, axis=-1, keepdims=True))
        if with_res:
            dx = dx + r_ref[...]
        dx_ref[...] = dx
        dxb_ref[...] = dx.astype(BF16)
        part = jnp.sum(dy * xh, axis=0, keepdims=True)

        @pl.when(pl.program_id(0) == 0)
        def _():
            dg_ref[...] = part

        @pl.when(pl.program_id(0) > 0)
        def _():
            dg_ref[...] += part

    spec = pl.BlockSpec((tm, d), lambda i: (i, 0))
    vec = pl.BlockSpec((1, d), lambda i: (0, 0))
    ops = [x, gain, dxn] + ([dres] if with_res else [])
    outs, ride_results = _call(
        name, body, (rows // tm,), [spec, vec, spec] + ([spec] if with_res else []), [spec, spec, vec],
        [jax.ShapeDtypeStruct(x.shape, F32), jax.ShapeDtypeStruct(x.shape, BF16), jax.ShapeDtypeStruct((1, d), F32)],
        ops, rides=rides)
    return (outs, ride_results) if rides else outs


ROW_CHUNK = 256


def _ffn_up(xn, wg, wu, tag, rides=()):
    t, d = xn.shape
    nd, fs, _ = wg.shape
    tm = min(t, 512)
    rc = min(tm, ROW_CHUNK)

    def body(x_ref, wg_ref, wu_ref, a_ref, b_ref, h_ref):
        for r in range(0, tm, rc):
            xv = x_ref[r:r + rc, :]
            g = _dot(xv, wg_ref[...], "nt")
            u = _dot(xv, wu_ref[...], "nt")
            sig = _sigmoid(g)
            silu = g * sig
            a_ref[r:r + rc, :] = (0.5 * u * (sig + silu * (1.0 - sig))).astype(BF16)
            b_ref[r:r + rc, :] = (0.5 * silu).astype(BF16)
            h_ref[r:r + rc, :] = (silu * u).astype(BF16)

    wspec = pl.BlockSpec((None, fs, d), lambda j, i: (j, 0, 0))
    hspec = pl.BlockSpec((None, tm, fs), lambda j, i: (j, i, 0))
    hid = jax.ShapeDtypeStruct((nd, t, fs), BF16)
    return _call(f"{tag}_up", body, (nd, t // tm), [pl.BlockSpec((tm, d), lambda j, i: (i, 0)), wspec, wspec],
                 [hspec] * 3, [hid] * 3, [xn, wg, wu], rides=rides)


CONTRACT_ROWS = 256


def _ffn_contract(hid, w, name, res=None, alpha=1.0, rides=(), tail=None):
    nd, t, fs = hid.shape
    d = w.shape[2]
    tm = min(t, CONTRACT_ROWS)
    xspec = pl.BlockSpec((tm, d), lambda i, k: (i, 0))
    pairs = [(hid, pl.BlockSpec((None, tm, fs), lambda i, k, s=s: (s, i, 0)),
              w, pl.BlockSpec((None, fs, d), lambda i, k, s=s: (s, 0, 0), pipeline_mode=pl.Buffered(1)))
             for s in range(nd)]
    return _mm(name, pairs, "nn", (t // tm, 1), jax.ShapeDtypeStruct((t, d), F32), xspec,
               res=res, res_spec=xspec if res is not None else None, alpha=alpha, rides=rides, tail=tail)


def _ffn_down(x, h, wd, tag, rides=(), tail=None):
    return _ffn_contract(h, wd, f"{tag}_down", res=x, alpha=0.5, rides=rides, tail=tail)


def _ffn_dact(dyb, wd, a, b, tag, rides=()):
    nd, t, fs = a.shape
    d = dyb.shape[1]
    tm = min(t, 512)
    rc = min(tm, ROW_CHUNK)

    def body(dy_ref, wd_ref, a_ref, b_ref, dg_ref, du_ref):
        for r in range(0, tm, rc):
            dh = _dot(dy_ref[r:r + rc, :], wd_ref[...], "nt")
            dg_ref[r:r + rc, :] = (dh * a_ref[r:r + rc, :].astype(F32)).astype(BF16)
            du_ref[r:r + rc, :] = (dh * b_ref[r:r + rc, :].astype(F32)).astype(BF16)

    hspec = pl.BlockSpec((None, tm, fs), lambda j, i: (j, i, 0))
    hid = jax.ShapeDtypeStruct((nd, t, fs), BF16)
    return _call(f"{tag}_dact", body, (nd, t // tm),
                 [pl.BlockSpec((tm, d), lambda j, i: (i, 0)), pl.BlockSpec((None, fs, d), lambda j, i: (j, 0, 0)),
                  hspec, hspec], [hspec] * 2, [hid] * 2, [dyb, wd, a, b], rides=rides)


def _ffn_dw(hid, act, alpha, name, rides=()):
    nd, t, fs = hid.shape
    d = act.shape[1]
    tk = t
    return _mm(name, [(hid, pl.BlockSpec((None, tk, fs), lambda j, k: (j, k, 0)),
                       act, pl.BlockSpec((tk, d), lambda j, k: (k, 0), pipeline_mode=pl.Buffered(1)))],
               "tn", (nd, t // tk),
               jax.ShapeDtypeStruct((nd, fs, d), BF16), pl.BlockSpec((None, fs, d), lambda j, k: (j, 0, 0)),
               alpha=alpha, rides=rides)


def _head_norm(x, gain):
    r = lax.rsqrt(jnp.mean(x * x, axis=-1, keepdims=True) + EPS)
    return x * r * gain


def _head_norm_bwd(x, gain, dy):
    r = lax.rsqrt(jnp.mean(x * x, axis=-1, keepdims=True) + EPS)
    xh = x * r
    dxh = dy * gain
    dx = r * (dxh - xh * jnp.mean(dxh * xh, axis=-1, keepdims=True))
    return dx, jnp.sum(dy * xh, axis=0, keepdims=True)


def _hs(h, base=0):
    return slice(base + h * HEAD, base + (h + 1) * HEAD)


def _tri(n, lower):
    r = lax.broadcasted_iota(jnp.int32, (n, n), 0)
    c = lax.broadcasted_iota(jnp.int32, (n, n), 1)
    return ((r >= c) if lower else (r <= c)).astype(F32)


def _attn_pre(proj, sp, name):
    t = proj.shape[0]
    tm = min(t, 256)

    def body(p_ref, sp_ref, fq, fk, fv, sq, sk, sv, mq, cc, carry):
        @pl.when(pl.program_id(0) == 0)
        def _():
            carry[...] = jnp.zeros_like(carry)

        for h in range(FOX_H):
            fq[:, _hs(h)] = _head_norm(p_ref[:, _hs(h, C_FQ)], sp_ref[R_FQ:R_FQ + 1, :]).astype(BF16)
            fk[:, _hs(h)] = _head_norm(p_ref[:, _hs(h, C_FK)], sp_ref[R_FK:R_FK + 1, :]).astype(BF16)
        fv[...] = p_ref[:, C_FV:C_FV + FOX_W].astype(BF16)
        for h in range(SWA_H):
            sq[:, _hs(h)] = _head_norm(p_ref[:, _hs(h, C_SQ)], sp_ref[R_SQ:R_SQ + 1, :]).astype(BF16)
        for h in range(SWA_KV):
            sk[:, _hs(h)] = _head_norm(p_ref[:, _hs(h, C_SK)], sp_ref[R_SK:R_SK + 1, :]).astype(BF16)
        sv[...] = p_ref[:, C_SV:C_SV + SWA_KV_W].astype(BF16)
        for h in range(MEM_H):
            mq[:, _hs(h)] = _head_norm(p_ref[:, _hs(h, C_MQ)], sp_ref[R_MQ:R_MQ + 1, :]).astype(BF16)
        z = p_ref[:, C_FL:C_FL + HEAD] + sp_ref[R_FB:R_FB + 1, :]
        lane = lax.broadcasted_iota(jnp.int32, z.shape, 1)
        log_f = jnp.minimum(z, 0.0) - jnp.log(1.0 + jnp.exp(-jnp.abs(z)))
        log_f = jnp.where(lane < N_LOGIT, log_f, 0.0)
        c = jnp.dot(_tri(tm, True), log_f, precision=lax.Precision.HIGHEST, preferred_element_type=F32)
        c = c + carry[0:1, :]
        cc[...] = c
        carry[...] = jnp.broadcast_to(c[tm - 1:tm, :], carry.shape)

    def rows(w):
        return pl.BlockSpec((tm, w), lambda i: (i, 0))

    def shape(w, dt):
        return jax.ShapeDtypeStruct((t, w), dt)

    widths = [FOX_W, FOX_W, FOX_W, SWA_W, SWA_KV_W, SWA_KV_W, MEM_W]
    return pl.pallas_call(
        body, grid=(t // tm,), in_specs=[rows(IN_W), pl.BlockSpec((16, 128), lambda i: (0, 0))],
        out_specs=[rows(w) for w in widths] + [rows(HEAD)],
        out_shape=[shape(w, BF16) for w in widths] + [shape(HEAD, F32)],
        scratch_shapes=[pltpu.VMEM((8, 128), F32)], compiler_params=_params(1), name=name)(proj, sp)


def _attn_post_bwd(proj, sp, dfq, dfk, dfv, dsq, dsk, dsv, dmq, dc_col, dc_row_t, name):
    t = proj.shape[0]
    tm = min(t, 256)
    nb = t // tm

    def body(p_ref, sp_ref, dfq_r, dfk_r, dfv_r, dsq_r, dsk_r, dsv_r, dmq_r, dcc_r, dcr_r, dp_ref, dsp_ref, carry):
        @pl.when(pl.program_id(0) == 0)
        def _():
            carry[...] = jnp.zeros_like(carry)
            dsp_ref[...] = jnp.zeros_like(dsp_ref)

        def group(n_heads, col, row, d_ref):
            total = None
            for h in range(n_heads):
                dx, dg = _head_norm_bwd(p_ref[:, _hs(h, col)], sp_ref[row:row + 1, :], d_ref[:, _hs(h)])
                dp_ref[:, _hs(h, col)] = dx.astype(BF16)
                total = dg if total is None else total + dg
            dsp_ref[row:row + 1, :] += total

        group(FOX_H, C_FQ, R_FQ, dfq_r)
        group(FOX_H, C_FK, R_FK, dfk_r)
        dp_ref[:, C_FV:C_FV + FOX_W] = dfv_r[...].astype(BF16)
        group(SWA_H, C_SQ, R_SQ, dsq_r)
        group(SWA_KV, C_SK, R_SK, dsk_r)
        dp_ref[:, C_SV:C_SV + SWA_KV_W] = dsv_r[...].astype(BF16)
        group(MEM_H, C_MQ, R_MQ, dmq_r)
        dc = dcc_r[...] - dcr_r[...]
        rc = jnp.dot(_tri(tm, False), dc, precision=lax.Precision.HIGHEST, preferred_element_type=F32)
        rc = rc + carry[0:1, :]
        carry[...] = jnp.broadcast_to(rc[0:1, :], carry.shape)
        z = p_ref[:, C_FL:C_FL + HEAD] + sp_ref[R_FB:R_FB + 1, :]
        dz = rc * _sigmoid(-z)
        dp_ref[:, C_FL:C_FL + HEAD] = dz.astype(BF16)
        dsp_ref[R_FB:R_FB + 1, :] += jnp.sum(dz, axis=0, keepdims=True)

    def rows(w):
        return pl.BlockSpec((tm, w), lambda i: (nb - 1 - i, 0))

    small = pl.BlockSpec((16, 128), lambda i: (0, 0))
    widths = [FOX_W, FOX_W, FOX_W, SWA_W, SWA_KV_W, SWA_KV_W, MEM_W, HEAD, HEAD]
    return pl.pallas_call(
        body, grid=(nb,), in_specs=[rows(IN_W), small] + [rows(w) for w in widths],
        out_specs=[rows(IN_W), small],
        out_shape=[jax.ShapeDtypeStruct((t, IN_W), BF16), jax.ShapeDtypeStruct((16, 128), F32)],
        scratch_shapes=[pltpu.VMEM((8, 128), F32)], compiler_params=_params(1), name=name,
    )(proj, sp, dfq, dfk, dfv, dsq, dsk, dsv, dmq, dc_col, dc_row_t)


def _head_column(values):
    rows = values[0].shape[0]
    lane = lax.broadcasted_iota(jnp.int32, (rows, HEAD), 1)
    out = jnp.zeros((rows, HEAD), F32)
    for h, v in enumerate(values):
        out = jnp.where(lane == h, v, out)
    return out


def _head_row(values, n_rows=8):
    cols = values[0].shape[1]
    sub = lax.broadcasted_iota(jnp.int32, (n_rows, cols), 0)
    out = jnp.zeros((n_rows, cols), F32)
    for h, v in enumerate(values):
        out = jnp.where(sub == h, v, out)
    return out


def _delta(dmixed, o_a, o_b, o_c, name):
    t = dmixed.shape[0]
    tm = min(t, 512)

    def body(d_ref, a_ref, b_ref, c_ref, o_ref, rep_ref):
        cols = []
        for ref, n_heads, base in ((a_ref, FOX_H, 0), (b_ref, SWA_H, FOX_W), (c_ref, MEM_H, FOX_W + SWA_W)):
            for h in range(n_heads):
                cols.append(jnp.sum(d_ref[:, _hs(h, base)] * ref[:, _hs(h)], axis=-1, keepdims=True))
        o_ref[...] = _head_column(cols)
        for h in range(FOX_H):
            rep_ref[h] = jnp.broadcast_to(cols[h], (tm, HEAD))

    def rows(w):
        return pl.BlockSpec((tm, w), lambda i: (i, 0))

    return pl.pallas_call(body, grid=(t // tm,), in_specs=[rows(dmixed.shape[1]), rows(FOX_W), rows(SWA_W), rows(MEM_W)],
                          out_specs=[rows(HEAD), pl.BlockSpec((FOX_H, tm, HEAD), lambda i: (0, i, 0))],
                          out_shape=[jax.ShapeDtypeStruct((t, HEAD), F32), jax.ShapeDtypeStruct((FOX_H, t, HEAD), F32)],
                          compiler_params=_params(1), name=name)(dmixed, o_a, o_b, o_c)


def _fox_fwd(fq, fk, fv, c_rep, c_row, name, rides=()):
    t = fq.shape[0]
    tb = min(t, 512)
    nb = t // tb
    n_tiles = tb // HEAD

    def body(q_ref, k_ref, v_ref, cc_ref, cr_ref, o_ref, lse_ref, m_s, l_s, acc_s):
        qi, ki = pl.program_id(0), pl.program_id(1)

        @pl.when(ki == 0)
        def _():
            m_s[...] = jnp.full_like(m_s, NEG_INF)
            l_s[...] = jnp.zeros_like(l_s)
            acc_s[...] = jnp.zeros_like(acc_s)

        def step(diagonal):
            if diagonal:
                r = lax.broadcasted_iota(jnp.int32, (tb, HEAD), 0)
                c = lax.broadcasted_iota(jnp.int32, (tb, HEAD), 1)
            for h in range(FOX_H):
                s = _dot(q_ref[:, _hs(h)], k_ref[:, _hs(h)], "nt")
                cc = cc_ref[h]
                tiles, m_cur = [], None
                for j in range(n_tiles):
                    st = s[:, _hs(j)] * SCALE + cc - cr_ref[h:h + 1, _hs(j)]
                    if diagonal:
                        st = jnp.where(r >= c + j * HEAD, st, NEG_INF)
                    tiles.append(st)
                    m_cur = st if m_cur is None else jnp.maximum(m_cur, st)
                m_prev = m_s[h]
                m_new = jnp.maximum(m_prev, jnp.max(m_cur, axis=-1, keepdims=True))
                alpha = jnp.exp(m_prev - m_new)
                ps = [jnp.exp(st - m_new) for st in tiles]
                l_cur = ps[0]
                for p in ps[1:]:
                    l_cur = l_cur + p
                l_s[h] = alpha * l_s[h] + jnp.sum(l_cur, axis=-1, keepdims=True)
                p = jnp.concatenate([p.astype(BF16) for p in ps], axis=1)
                acc_s[:, _hs(h)] = alpha * acc_s[:, _hs(h)] + _dot(p, v_ref[:, _hs(h)])
                m_s[h] = m_new

        @pl.when(ki < qi)
        def _():
            step(False)

        @pl.when(ki == qi)
        def _():
            step(True)
            for h in range(FOX_H):
                o_ref[:, _hs(h)] = acc_s[:, _hs(h)] / l_s[h]
                lse_ref[h] = m_s[h] + jnp.log(l_s[h])

    qspec = pl.BlockSpec((tb, FOX_W), lambda i, j: (i, 0))
    kspec = pl.BlockSpec((tb, FOX_W), lambda i, j: (jnp.minimum(i, j), 0))
    rep = pl.BlockSpec((FOX_H, tb, HEAD), lambda i, j: (0, i, 0))
    return _call(
        name, body, (nb, nb),
        [qspec, kspec, kspec, rep, pl.BlockSpec((8, tb), lambda i, j: (0, jnp.minimum(i, j)))],
        [qspec, rep],
        [jax.ShapeDtypeStruct((t, FOX_W), F32), jax.ShapeDtypeStruct((FOX_H, t, HEAD), F32)],
        [fq, fk, fv, c_rep, c_row],
        scratch=[pltpu.VMEM((FOX_H, tb, HEAD), F32), pltpu.VMEM((FOX_H, tb, HEAD), F32), pltpu.VMEM((tb, FOX_W), F32)],
        rides=rides)


def _fox_bwd(fq, fk, fv, c_rep, c_row, dmixed, lse, delta, name, rides=()):
    t = fq.shape[0]
    tb = min(t, 512)
    nb = t // tb
    n_tiles = tb // HEAD

    def body(q_ref, k_ref, v_ref, cc_ref, cr_ref, do_ref, lse_ref, dl_ref,
             dq_ref, dk_ref, dv_ref, dcc_ref, dcr_ref):
        ki, qi = pl.program_id(0), pl.program_id(1)

        @pl.when((ki == 0) & (qi == 0))
        def _():
            dq_ref[...] = jnp.zeros_like(dq_ref)
            dcc_ref[...] = jnp.zeros_like(dcc_ref)

        @pl.when(qi == 0)
        def _():
            dk_ref[...] = jnp.zeros_like(dk_ref)
            dv_ref[...] = jnp.zeros_like(dv_ref)
            dcr_ref[...] = jnp.zeros_like(dcr_ref)

        def step(diagonal):
            rows = pl.ds(pl.multiple_of(qi * tb, tb), tb)
            if diagonal:
                r = lax.broadcasted_iota(jnp.int32, (tb, HEAD), 0)
                c = lax.broadcasted_iota(jnp.int32, (tb, HEAD), 1)
            row_sums, col_sums = [], []
            for h in range(FOX_H):
                q, k, v, do = q_ref[:, _hs(h)], k_ref[:, _hs(h)], v_ref[:, _hs(h)], do_ref[:, _hs(h)]
                s = _dot(q, k, "nt")
                dp = _dot(do, v, "nt")
                cc, lse_h, dl_h = cc_ref[h], lse_ref[h], dl_ref[h]
                ps, dss, row = [], [], None
                for j in range(n_tiles):
                    st = s[:, _hs(j)] * SCALE + cc - cr_ref[h:h + 1, _hs(j)]
                    if diagonal:
                        st = jnp.where(r >= c + j * HEAD, st, NEG_INF)
                    pt = jnp.exp(st - lse_h)
                    dst = pt * (dp[:, _hs(j)] - dl_h)
                    ps.append(pt.astype(BF16))
                    dss.append(dst)
                    row = dst if row is None else row + dst
                p = jnp.concatenate(ps, axis=1)
                ds = jnp.concatenate(dss, axis=1)
                dsb = ds.astype(BF16)
                dv_ref[:, _hs(h)] += _dot(p, do, "tn")
                dk_ref[:, _hs(h)] += _dot(dsb, q, "tn") * SCALE
                dq_ref[rows, _hs(h)] += _dot(dsb, k) * SCALE
                row_sums.append(jnp.sum(row, axis=1, keepdims=True))
                col_sums.append(jnp.sum(ds, axis=0, keepdims=True))
            dcc_ref[rows, :] += _head_column(row_sums)
            dcr_ref[...] += _head_row(col_sums)

        @pl.when(qi > ki)
        def _():
            step(False)

        @pl.when(qi == ki)
        def _():
            step(True)

    def qmap(j, i):
        return (jnp.maximum(i, j), 0)

    qspec = pl.BlockSpec((tb, FOX_W), qmap)
    kspec = pl.BlockSpec((tb, FOX_W), lambda j, i: (j, 0))
    rep = pl.BlockSpec((FOX_H, tb, HEAD), lambda j, i: (0, jnp.maximum(i, j), 0))
    rowspec = pl.BlockSpec((8, tb), lambda j, i: (0, j))
    return _call(
        name, body, (nb, nb), [qspec, kspec, kspec, rep, rowspec, qspec, rep, rep],
        [pl.BlockSpec((t, FOX_W), lambda j, i: (0, 0)), kspec, kspec,
         pl.BlockSpec((t, HEAD), lambda j, i: (0, 0)), rowspec],
        [jax.ShapeDtypeStruct((t, FOX_W), F32)] * 3 + [jax.ShapeDtypeStruct((t, HEAD), F32),
                                                       jax.ShapeDtypeStruct((8, t), F32)],
        [fq, fk, fv, c_rep, c_row, dmixed, lse, delta], rides=rides)


def _swa_logits(q, k_cur, k_prev, slope, first_block):
    w = SWA_BLOCK
    r = lax.broadcasted_iota(jnp.int32, (w, w), 0)
    j = lax.broadcasted_iota(jnp.int32, (w, w), 1)
    dist_cur = r - j
    dist_prev = w + r - j
    s_cur = _dot(q, k_cur, "nt") * SCALE - slope * dist_cur.astype(F32)
    s_cur = jnp.where(dist_cur >= 0, s_cur, NEG_INF)
    s_prev = _dot(q, k_prev, "nt") * SCALE - slope * dist_prev.astype(F32)
    s_prev = jnp.where((j > r) & jnp.logical_not(first_block), s_prev, NEG_INF)
    return s_cur, s_prev


def _slope(h):
    return float(2.0 ** (-8.0 * (h + 1) / SWA_H))


def _swa_fwd(sq, sk, sv, sp, name, rides=()):
    t = sq.shape[0]
    w = SWA_BLOCK
    nb = t // w
    group = SWA_H // SWA_KV

    def body(q_ref, kp_ref, kc_ref, vp_ref, vc_ref, sp_ref, o_ref, lse_ref):
        first = pl.program_id(0) == 0
        lses = []
        for h in range(SWA_H):
            kv = h // group
            s_cur, s_prev = _swa_logits(q_ref[:, _hs(h)], kc_ref[:, _hs(kv)], kp_ref[:, _hs(kv)], _slope(h), first)
            sink = sp_ref[R_SINK:R_SINK + 1, h:h + 1]
            m = jnp.maximum(jnp.maximum(jnp.max(s_cur, axis=-1, keepdims=True),
                                        jnp.max(s_prev, axis=-1, keepdims=True)), sink)
            p_cur = jnp.exp(s_cur - m)
            p_prev = jnp.exp(s_prev - m)
            l = jnp.sum(p_cur, axis=-1, keepdims=True) + jnp.sum(p_prev, axis=-1, keepdims=True) + jnp.exp(sink - m)
            o_ref[:, _hs(h)] = (_dot(p_cur, vc_ref[:, _hs(kv)]) + _dot(p_prev, vp_ref[:, _hs(kv)])) / l
            lses.append(m + jnp.log(l))
        lse_ref[...] = _head_column(lses)

    qspec = pl.BlockSpec((w, SWA_W), lambda n: (n, 0))
    cur = pl.BlockSpec((w, SWA_KV_W), lambda n: (n, 0))
    prev = pl.BlockSpec((w, SWA_KV_W), lambda n: (jnp.maximum(n - 1, 0), 0))
    return _call(
        name, body, (nb,), [qspec, prev, cur, prev, cur, pl.BlockSpec((16, 128), lambda n: (0, 0))],
        [qspec, pl.BlockSpec((w, HEAD), lambda n: (n, 0))],
        [jax.ShapeDtypeStruct((t, SWA_W), F32), jax.ShapeDtypeStruct((t, HEAD), F32)],
        [sq, sk, sk, sv, sv, sp], rides=rides)


def _swa_bwd(sq, sk, sv, sp, dmixed, lse, delta, name):
    t = sq.shape[0]
    w = SWA_BLOCK
    nb = t // w
    group = SWA_H // SWA_KV
    do_block = FOX_W // SWA_W
    assert FOX_W % SWA_W == 0

    def body(q_ref, kp_ref, kc_ref, vp_ref, vc_ref, sp_ref, do_ref, lse_ref, dl_ref,
             dq_ref, dk_ref, dv_ref, dsp_ref, ck, cv):
        step = pl.program_id(0)
        first = step == nb - 1

        @pl.when(step == 0)
        def _():
            ck[...] = jnp.zeros_like(ck)
            cv[...] = jnp.zeros_like(cv)
            dsp_ref[...] = jnp.zeros_like(dsp_ref)

        dk_cur = [None] * SWA_KV
        dk_prev = [None] * SWA_KV
        dv_cur = [None] * SWA_KV
        dv_prev = [None] * SWA_KV
        dsinks = []

        def add(lst, i, v):
            lst[i] = v if lst[i] is None else lst[i] + v

        for h in range(SWA_H):
            kv = h // group
            q, do = q_ref[:, _hs(h)], do_ref[:, _hs(h)]
            kc, kp, vc, vp = kc_ref[:, _hs(kv)], kp_ref[:, _hs(kv)], vc_ref[:, _hs(kv)], vp_ref[:, _hs(kv)]
            s_cur, s_prev = _swa_logits(q, kc, kp, _slope(h), first)
            lse_h = lse_ref[:, h:h + 1]
            dl_h = dl_ref[:, FOX_H + h:FOX_H + h + 1]
            p_cur = jnp.exp(s_cur - lse_h)
            p_prev = jnp.exp(s_prev - lse_h)
            p_sink = jnp.exp(sp_ref[R_SINK:R_SINK + 1, h:h + 1] - lse_h)
            ds_cur = p_cur * (_dot(do, vc, "nt") - dl_h)
            ds_prev = p_prev * (_dot(do, vp, "nt") - dl_h)
            dq_ref[:, _hs(h)] = (_dot(ds_cur, kc) + _dot(ds_prev, kp)) * SCALE
            add(dk_cur, kv, _dot(ds_cur, q, "tn") * SCALE)
            add(dk_prev, kv, _dot(ds_prev, q, "tn") * SCALE)
            add(dv_cur, kv, _dot(p_cur, do, "tn"))
            add(dv_prev, kv, _dot(p_prev, do, "tn"))
            dsinks.append(-jnp.sum(p_sink * dl_h, axis=0, keepdims=True))
        for kv in range(SWA_KV):
            dk_ref[:, _hs(kv)] = dk_cur[kv] + ck[:, _hs(kv)]
            dv_ref[:, _hs(kv)] = dv_cur[kv] + cv[:, _hs(kv)]
            ck[:, _hs(kv)] = dk_prev[kv]
            cv[:, _hs(kv)] = dv_prev[kv]
        lane = lax.broadcasted_iota(jnp.int32, (1, HEAD), 1)
        row = jnp.zeros((1, HEAD), F32)
        for h in range(SWA_H):
            row = jnp.where(lane == h, dsinks[h], row)
        dsp_ref[R_SINK:R_SINK + 1, :] += row

    def rev(n):
        return nb - 1 - n

    qspec = pl.BlockSpec((w, SWA_W), lambda n: (rev(n), 0))
    cur = pl.BlockSpec((w, SWA_KV_W), lambda n: (rev(n), 0))
    prev = pl.BlockSpec((w, SWA_KV_W), lambda n: (jnp.maximum(rev(n) - 1, 0), 0))
    col = pl.BlockSpec((w, HEAD), lambda n: (rev(n), 0))
    small = pl.BlockSpec((16, 128), lambda n: (0, 0))
    return pl.pallas_call(
        body, grid=(nb,),
        in_specs=[qspec, prev, cur, prev, cur, small, pl.BlockSpec((w, SWA_W), lambda n: (rev(n), do_block)), col, col],
        out_specs=[qspec, cur, cur, small],
        out_shape=[jax.ShapeDtypeStruct((t, SWA_W), F32), jax.ShapeDtypeStruct((t, SWA_KV_W), F32),
                   jax.ShapeDtypeStruct((t, SWA_KV_W), F32), jax.ShapeDtypeStruct((16, 128), F32)],
        scratch_shapes=[pltpu.VMEM((w, SWA_KV_W), F32), pltpu.VMEM((w, SWA_KV_W), F32)],
        compiler_params=_params(1), name=name)(sq, sk, sk, sv, sv, sp, dmixed, lse, delta)


def _mem_pre(mkv, sp, name):
    m = mkv.shape[0]

    def body(x_ref, sp_ref, k_ref, v_ref):
        for h in range(MEM_H):
            k_ref[:, _hs(h)] = _head_norm(x_ref[:, _hs(h)], sp_ref[R_MK:R_MK + 1, :]).astype(BF16)
        v_ref[...] = x_ref[:, MEM_W:2 * MEM_W].astype(BF16)

    out = jax.ShapeDtypeStruct((m, MEM_W), BF16)
    return pl.pallas_call(body, out_shape=[out, out], name=name)(mkv, sp)


def _mem_post_bwd(mkv, sp, dmk, dmv, name):
    m = mkv.shape[0]

    def body(x_ref, sp_ref, dk_ref, dv_ref, d_ref, dsp_ref):
        dsp_ref[...] = jnp.zeros_like(dsp_ref)
        total = None
        for h in range(MEM_H):
            dx, dg = _head_norm_bwd(x_ref[:, _hs(h)], sp_ref[R_MK:R_MK + 1, :], dk_ref[:, _hs(h)])
            d_ref[:, _hs(h)] = dx.astype(BF16)
            total = dg if total is None else total + dg
        d_ref[:, MEM_W:2 * MEM_W] = dv_ref[...].astype(BF16)
        dsp_ref[R_MK:R_MK + 1, :] = total

    return pl.pallas_call(body, out_shape=[jax.ShapeDtypeStruct((m, 2 * MEM_W), BF16),
                                           jax.ShapeDtypeStruct((16, 128), F32)], name=name)(mkv, sp, dmk, dmv)


def _mem_fwd(mq, mk, mv, name):
    t = mq.shape[0]
    m = mk.shape[0]
    tq = min(t, 512)

    def body(q_ref, k_ref, v_ref, o_ref, lse_ref):
        lses = []
        for h in range(MEM_H):
            s = _dot(q_ref[:, _hs(h)], k_ref[:, _hs(h)], "nt") * SCALE
            mx = jnp.max(s, axis=-1, keepdims=True)
            p = jnp.exp(s - mx)
            l = jnp.sum(p, axis=-1, keepdims=True)
            o_ref[:, _hs(h)] = _dot(p, v_ref[:, _hs(h)]) / l
            lses.append(mx + jnp.log(l))
        lse_ref[...] = _head_column(lses)

    qspec = pl.BlockSpec((tq, MEM_W), lambda i: (i, 0))
    kspec = pl.BlockSpec((m, MEM_W), lambda i: (0, 0))
    return pl.pallas_call(
        body, grid=(t // tq,), in_specs=[qspec, kspec, kspec],
        out_specs=[qspec, pl.BlockSpec((tq, HEAD), lambda i: (i, 0))],
        out_shape=[jax.ShapeDtypeStruct((t, MEM_W), F32), jax.ShapeDtypeStruct((t, HEAD), F32)],
        compiler_params=_params(1), name=name)(mq, mk, mv)


def _mem_bwd(mq, mk, mv, dmixed, lse, delta, name):
    t = mq.shape[0]
    m = mk.shape[0]
    tq = min(t, 512)
    do_block = (FOX_W + SWA_W) // MEM_W
    assert (FOX_W + SWA_W) % MEM_W == 0

    def body(q_ref, k_ref, v_ref, do_ref, lse_ref, dl_ref, dq_ref, dk_ref, dv_ref):
        @pl.when(pl.program_id(0) == 0)
        def _():
            dk_ref[...] = jnp.zeros_like(dk_ref)
            dv_ref[...] = jnp.zeros_like(dv_ref)

        for h in range(MEM_H):
            q, k, v, do = q_ref[:, _hs(h)], k_ref[:, _hs(h)], v_ref[:, _hs(h)], do_ref[:, _hs(h)]
            s = _dot(q, k, "nt") * SCALE
            p = jnp.exp(s - lse_ref[:, h:h + 1])
            col = FOX_H + SWA_H + h
            ds = p * (_dot(do, v, "nt") - dl_ref[:, col:col + 1])
            dq_ref[:, _hs(h)] = _dot(ds, k) * SCALE
            dk_ref[:, _hs(h)] += _dot(ds, q, "tn") * SCALE
            dv_ref[:, _hs(h)] += _dot(p, do, "tn")

    qspec = pl.BlockSpec((tq, MEM_W), lambda i: (i, 0))
    kspec = pl.BlockSpec((m, MEM_W), lambda i: (0, 0))
    col = pl.BlockSpec((tq, HEAD), lambda i: (i, 0))
    return pl.pallas_call(
        body, grid=(t // tq,),
        in_specs=[qspec, kspec, kspec, pl.BlockSpec((tq, MEM_W), lambda i: (i, do_block)), col, col],
        out_specs=[qspec, kspec, kspec],
        out_shape=[jax.ShapeDtypeStruct((t, MEM_W), F32), jax.ShapeDtypeStruct((m, MEM_W), F32),
                   jax.ShapeDtypeStruct((m, MEM_W), F32)],
        compiler_params=_params(1), name=name)(mq, mk, mv, dmixed, lse, delta)


def _all_gather(xs, name):
    n = len(xs)

    def body(*refs):
        x_refs, o_refs = refs[:n], refs[n:2 * n]
        send_sems, recv_sems, local_sems = refs[2 * n:]
        x, y, c = _me()
        me, sibling = (x, y, c), (x, y, 1 - c)
        chips = [(1 - x, y), (x, 1 - y), (1 - x, 1 - y)]

        def copy(a, k, block, to, src=None):
            slot = o_refs[a].at[_lin(block)]
            return pltpu.make_async_remote_copy(
                src_ref=slot if src is None else src, dst_ref=slot, send_sem=send_sems.at[a, k],
                recv_sem=recv_sems.at[a, k], device_id=to, device_id_type=MESH)

        mine = [pltpu.make_async_copy(x_refs[a], o_refs[a].at[_lin(me)], local_sems.at[a]) for a in range(n)]
        for cp in mine:
            cp.start()
        first = []
        for a in range(n):
            first.append(copy(a, 0, me, sibling, src=x_refs[a]))
            first += [copy(a, 1 + j, me, (*chip, c), src=x_refs[a]) for j, chip in enumerate(chips)]
        for cp in first:
            cp.start()
        passed = []
        for j, chip in enumerate(chips):
            for a in range(n):
                copy(a, 1 + j, (*chip, c), me).wait_recv()
                cp = copy(a, 4 + j, (*chip, c), sibling)
                cp.start()
                passed.append(cp)
        for a in range(n):
            copy(a, 0, sibling, me).wait_recv()
            for j, chip in enumerate(chips):
                copy(a, 4 + j, (*chip, 1 - c), me).wait_recv()
        for cp in first + passed:
            cp.wait_send()
        for cp in mine:
            cp.wait()

    return pl.pallas_call(
        body, in_specs=[ANY] * n, out_specs=[ANY] * n,
        out_shape=[jax.ShapeDtypeStruct((N_DEV,) + x.shape, x.dtype) for x in xs],
        scratch_shapes=[pltpu.SemaphoreType.DMA((n, 7)), pltpu.SemaphoreType.DMA((n, 7)),
                        pltpu.SemaphoreType.DMA((n,))],
        name=name)(*xs)


def _peers():
    x, y, c = _me()
    out = []
    for k in range(1, N_DEV):
        kx, ky, kc = (k >> 2) & 1, (k >> 1) & 1, k & 1
        out.append(((1 - x) if kx else x, (1 - y) if ky else y, (1 - c) if kc else c))
    return out


def _all_reduce_small(xs, name):
    n = len(xs)

    def body(*refs):
        x_refs, o_refs = refs[:n], refs[n:2 * n]
        bufs = refs[2 * n:3 * n]
        send_sems, recv_sems = refs[3 * n:]
        me = _lin(_me())
        peers = _peers()
        for a in range(n):
            bufs[a][me] = x_refs[a][...]
        sends = []
        for a in range(n):
            for k, peer in enumerate(peers):
                sends.append(pltpu.make_async_remote_copy(
                    src_ref=bufs[a].at[me], dst_ref=bufs[a].at[me], send_sem=send_sems.at[a, k],
                    recv_sem=recv_sems.at[a, k], device_id=peer, device_id_type=MESH))
        for cp in sends:
            cp.start()
        for a in range(n):
            for k, peer in enumerate(peers):
                pltpu.make_async_remote_copy(
                    src_ref=bufs[a].at[me], dst_ref=bufs[a].at[_lin(peer)], send_sem=send_sems.at[a, k],
                    recv_sem=recv_sems.at[a, k], device_id=peer, device_id_type=MESH).wait_recv()
        for cp in sends:
            cp.wait_send()
        for a in range(n):
            total = bufs[a][0]
            for q in range(1, N_DEV):
                total = total + bufs[a][q]
            o_refs[a][...] = total

    vmem = pl.BlockSpec(memory_space=pltpu.VMEM)
    return pl.pallas_call(
        body, in_specs=[vmem] * n, out_specs=[vmem] * n,
        out_shape=[jax.ShapeDtypeStruct(x.shape, F32) for x in xs],
        scratch_shapes=[pltpu.VMEM((N_DEV,) + x.shape, F32) for x in xs]
        + [pltpu.SemaphoreType.DMA((n, 7)), pltpu.SemaphoreType.DMA((n, 7))],
        name=name)(*xs)


def _pair_add(part, got, name):
    _, rows, cols = part.shape
    tm = _rows_tile(rows, cols * 2, budget=2 << 20)
    core = jnp.reshape(lax.axis_index("c"), (1,)).astype(jnp.int32)

    def body(c_ref, p_ref, g_ref, o_ref):
        o_ref[...] = (p_ref[...].astype(F32) + g_ref[...].astype(F32)).astype(BF16)

    spec = pl.BlockSpec((None, tm, cols), lambda q, i, c: (q, i, 0))
    grid_spec = pltpu.PrefetchScalarGridSpec(
        num_scalar_prefetch=1, grid=(4, rows // tm),
        in_specs=[pl.BlockSpec((None, tm, cols), lambda q, i, c: (2 * q + c[0], i, 0)), spec], out_specs=spec)
    return pl.pallas_call(body, grid_spec=grid_spec, out_shape=jax.ShapeDtypeStruct((4, rows, cols), BF16),
                          compiler_params=_params(2), name=name)(core, part, got)


def _adam_math(w, g, m, v):
    nm = ADAM_B1 * m + (1.0 - ADAM_B1) * g
    nv = ADAM_B2 * v + (1.0 - ADAM_B2) * (g * g)
    m_hat = nm / (1.0 - ADAM_B1 ** ADAM_STEP)
    v_hat = nv / (1.0 - ADAM_B2 ** ADAM_STEP)
    return -ADAM_LR * (m_hat / (jnp.sqrt(v_hat) + ADAM_EPS) + ADAM_WD * w), nm, nv


def _sum_chips(got, name):
    _, rows, cols = got.shape
    tm = _rows_tile(rows, cols * 2 * 4, budget=2 << 20)

    def body(r_ref, o_ref):
        o_ref[...] = ((r_ref[0].astype(F32) + r_ref[1].astype(F32)) + r_ref[2].astype(F32)) + r_ref[3].astype(F32)

    return pl.pallas_call(
        body, grid=(rows // tm,), in_specs=[pl.BlockSpec((4, tm, cols), lambda i: (0, i, 0))],
        out_specs=pl.BlockSpec((tm, cols), lambda i: (i, 0)), out_shape=jax.ShapeDtypeStruct((rows, cols), F32),
        compiler_params=_params(1), name=name)(got)


def _sum_adamw(got, col_block, w, m, v, name):
    _, rows, cols = w.shape
    tm = _rows_tile(rows, cols * 4, budget=1 << 20)

    def body(r_ref, w_ref, m_ref, v_ref, g_ref, d_ref, nm_ref, nv_ref):
        g = ((r_ref[0].astype(F32) + r_ref[1].astype(F32)) + r_ref[2].astype(F32)) + r_ref[3].astype(F32)
        g_ref[...] = g
        d_ref[...], nm_ref[...], nv_ref[...] = _adam_math(w_ref[...], g, m_ref[...], v_ref[...])

    spec = pl.BlockSpec((None, tm, cols), lambda i: (0, i, 0))
    out = jax.ShapeDtypeStruct(w.shape, F32)
    return pl.pallas_call(
        body, grid=(rows // tm,), in_specs=[pl.BlockSpec((4, tm, cols), lambda i: (0, i, col_block)), spec, spec, spec],
        out_specs=[spec] * 4, out_shape=[out] * 4, compiler_params=_params(1), name=name)(got, w, m, v)


def _adamw(w, g, m, v, name):
    rows, cols = w.shape
    tm = _rows_tile(rows, cols * 4, budget=2 << 20, mult=8)

    def body(w_ref, g_ref, m_ref, v_ref, d_ref, nm_ref, nv_ref):
        d_ref[...], nm_ref[...], nv_ref[...] = _adam_math(w_ref[...], g_ref[...], m_ref[...], v_ref[...])

    spec = pl.BlockSpec((tm, cols), lambda i: (i, 0))
    out = jax.ShapeDtypeStruct(w.shape, F32)
    return pl.pallas_call(body, grid=(rows // tm,), in_specs=[spec] * 4, out_specs=[spec] * 3,
                          out_shape=[out] * 3, compiler_params=_params(1), name=name)(w, g, m, v)


def _permute_in(w):
    logit0 = 3 * FOX_W
    pad = jnp.zeros(w.shape[:-1] + (HEAD - N_LOGIT,), w.dtype)
    return jnp.concatenate([w[..., :logit0], w[..., logit0 + N_LOGIT:], w[..., logit0:logit0 + N_LOGIT], pad], axis=-1)


def _unpermute_in(w):
    logit0 = 3 * FOX_W
    return jnp.concatenate([w[..., :logit0], w[..., C_FL:C_FL + N_LOGIT], w[..., logit0:C_FL]], axis=-1)


def _pad_row(v, width):
    return jnp.pad(v, ((0, 0), (0, width - v.shape[1])))


def _pack_small(fq, fk, sq, sk, mq, mk, fb, sinks):
    rows = [fq, fk, sq, sk, mq, mk, _pad_row(fb, HEAD), _pad_row(sinks, HEAD)]
    return jnp.concatenate(rows + [jnp.zeros((8, HEAD), F32)], axis=0)


def _pack_norms(a, b, c, d):
    return jnp.concatenate([a, b, c, d, jnp.zeros((4, a.shape[1]), F32)], axis=0)


def kernel(x, mem, ffn1_norm, ffn1_gate, ffn1_up, ffn1_down, mix_norm, mem_norm, w_in, forget_bias, w_mem_k, w_mem_v, fox_q_gain, fox_k_gain, swa_q_gain, swa_k_gain, swa_sinks, mem_q_gain, mem_k_gain, w_out, ffn2_norm, ffn2_gate, ffn2_up, ffn2_down, loss_target, m_ffn1_norm, m_ffn1_gate, m_ffn1_up, m_ffn1_down, m_mix_norm, m_mem_norm, m_w_in, m_forget_bias, m_w_mem_k, m_w_mem_v, m_fox_q_gain, m_fox_k_gain, m_swa_q_gain, m_swa_k_gain, m_swa_sinks, m_mem_q_gain, m_mem_k_gain, m_w_out, m_ffn2_norm, m_ffn2_gate, m_ffn2_up, m_ffn2_down, v_ffn1_norm, v_ffn1_gate, v_ffn1_up, v_ffn1_down, v_mix_norm, v_mem_norm, v_w_in, v_forget_bias, v_w_mem_k, v_w_mem_v, v_fox_q_gain, v_fox_k_gain, v_swa_q_gain, v_swa_k_gain, v_swa_sinks, v_mem_q_gain, v_mem_k_gain, v_w_out, v_ffn2_norm, v_ffn2_gate, v_ffn2_up, v_ffn2_down):
    x0 = x[0]
    mem0 = mem[0]
    target = loss_target[0]
    t, d = x0.shape
    d_shard = w_in.shape[1]
    m_len = mem0.shape[0]
    tm = min(t, 512)
    tk = min(t, 512)
    tn = IN_W // 3
    tkw, tnw = min(t, 1024), IN_W // 3

    def swap(a):
        return jnp.swapaxes(a, 1, 2)

    gate1, up1, gate2, up2 = swap(ffn1_gate), swap(ffn1_up), swap(ffn2_gate), swap(ffn2_up)

    local = {
        "g1": gate1[0], "u1": up1[0], "d1": ffn1_down[0],
        "g2": gate2[0], "u2": up2[0], "d2": ffn2_down[0],
        "in": _permute_in(w_in[0]), "out": w_out[0],
        "mkv": jnp.concatenate([w_mem_k[0], w_mem_v[0]], axis=1),
    }
    shard = {k: _cast_bf16(v, f"cast_{k}") for k, v in local.items()}
    sp = _pack_small(fox_q_gain, fox_k_gain, swa_q_gain, swa_k_gain, mem_q_gain, mem_k_gain, forget_bias, swa_sinks)
    wt = {}

    wt["g1"], wt["u1"] = _all_gather([shard["g1"], shard["u1"]], "gather_ffn1_in")
    xn1 = _rms_fwd(x0, ffn1_norm, "ffn1_norm")
    half_in = d_shard // 2
    half_fs = shard["g2"].shape[0] // 2
    (a1, b1, h1), ((wt["d1"],), half) = _ffn_up(
        xn1, wt["g1"], wt["u1"], "ffn1",
        rides=[_ride_gather([shard["d1"]], 0.85), _ride_gather([shard["in"]], 0.85, rows=(0, half_in))])
    tc = min(t, CONTRACT_ROWS)
    (x1, hn), ((wt["out"], wt["mkv"]), (wt["in"],)) = _ffn_down(
        x0, h1, wt["d1"], "ffn1",
        rides=[_ride_gather([shard["out"], shard["mkv"]], 0.6),
               _ride_gather([shard["in"]], 0.6, rows=(half_in, half_in), into=half)],
        tail=_tail_norm(mix_norm, t, d, tc))
    w_in_full = wt["in"].reshape(d, IN_W)

    proj, (half,) = _mm(
        "proj", [(hn, pl.BlockSpec((tm, d), lambda n, i, k: (i, 0)),
                  w_in_full, pl.BlockSpec((d, tn), lambda n, i, k: (0, n)))],
        "nn", (3, t // tm, 1), jax.ShapeDtypeStruct((t, IN_W), F32), pl.BlockSpec((tm, tn), lambda n, i, k: (i, n)),
        rides=[_ride_gather_chips([shard["g2"]], rows=(0, half_fs))])
    w_out_full = wt["out"].reshape(d, d)
    w_mkv_full = wt["mkv"].reshape(d, 2 * MEM_W)
    fq, fk, fv, sq, sk, sv, mq, c_col = _attn_pre(proj, sp, "attn_pre")
    c_row = jnp.transpose(c_col[:, :8])
    c_rep = jnp.broadcast_to(c_row[:FOX_H, :, None], (FOX_H, t, HEAD))

    mn = _rms_fwd(mem0, mem_norm, "mem_norm")
    mkv = _mm("mem_kv", [(mn, pl.BlockSpec((m_len, d), lambda k: (0, 0)),
                          w_mkv_full, pl.BlockSpec((d, 2 * MEM_W), lambda k: (0, 0)))],
              "nn", (1,), jax.ShapeDtypeStruct((m_len, 2 * MEM_W), F32),
              pl.BlockSpec((m_len, 2 * MEM_W), lambda k: (0, 0)))
    mk, mv = _mem_pre(mkv, sp, "mem_pre")

    (o_a, lse_a), (half, half_u2) = _fox_fwd(
        fq, fk, fv, c_rep, c_row, "fox_fwd",
        rides=[_ride_gather_chips([shard["g2"]], rows=(half_fs, half_fs), into=half),
               _ride_gather_chips([shard["u2"]], rows=(0, half_fs))])
    (o_b, lse_b), ((wt["g2"],), half_u2) = _swa_fwd(
        sq, sk, sv, sp, "swa_fwd",
        rides=[_ride_gather_sibling(half), _ride_gather_chips([shard["u2"]], rows=(half_fs, half_fs), into=half_u2)])
    o_c, lse_c = _mem_fwd(mq, mk, mv, "mem_fwd")

    def rows_spec(width):
        return pl.BlockSpec((tm, width), lambda i, k: (i, 0))

    def wout_rows(first, width):
        assert first % width == 0
        return pl.BlockSpec((width, d), lambda i, k: (first // width, 0), pipeline_mode=pl.Buffered(1))

    xspec = pl.BlockSpec((tm, d), lambda i, k: (i, 0))
    (x2, xn2), ((wt["u2"],),) = _mm(
        "mix_out",
        [(o_a, rows_spec(FOX_W), w_out_full, wout_rows(0, FOX_W)),
         (o_b, rows_spec(SWA_W), w_out_full, wout_rows(FOX_W, SWA_W)),
         (o_c, rows_spec(MEM_W), w_out_full, wout_rows(FOX_W + SWA_W, MEM_W))],
        "nn", (t // tm, 1), jax.ShapeDtypeStruct((t, d), F32), xspec, res=x1, res_spec=xspec,
        rides=[_ride_gather_sibling(half_u2)], tail=_tail_norm(ffn2_norm, t, d, tm))

    (a2, b2, h2), ((wt["d2"],),) = _ffn_up(xn2, wt["g2"], wt["u2"], "ffn2", rides=[_ride_gather([shard["d2"]], 0.75)])
    dy, dyb, sq_err = _ffn_down(x2, h2, wt["d2"], "ffn2", tail=_tail_loss(target, t, d, tc))
    loss = lax.psum(0.5 * sq_err[0, 0] / d, ("x", "y", "c"))

    got = {}
    paired = {}
    landed = {}

    def pair(k, part):
        paired[k] = _pair_add(part, got[k], f"pair_{k}")

    (dg2, du2), _ = _ffn_dact(dyb, wt["d2"], a2, b2, "ffn2")
    part_d2 = _ffn_dw(h2, dyb, 0.5, "ffn2_dwd")
    part_g2, ((got["d2"],),) = _ffn_dw(dg2, xn2, 1.0, "ffn2_dwg", rides=[_ride_scatter_sibling([part_d2])])
    pair("d2", part_d2)
    half_rows = part_d2.shape[1] // 2
    first, second = (0, half_rows), (half_rows, half_rows)
    part_u2, (half, (got["g2"],)) = _ffn_dw(
        du2, xn2, 1.0, "ffn2_dwu",
        rides=[_ride_scatter_chips([paired["d2"]], first), _ride_scatter_sibling([part_g2])])
    pair("g2", part_g2)
    dxn2, ((landed["d2"],),) = _ffn_contract(
        dg2, wt["g2"], "ffn2_dxn_g", rides=[_ride_scatter_chips([paired["d2"]], second, into=half)])
    (dx2, dx2b, dgain_ffn2), (half_g2, (got["u2"],)) = _ffn_contract(
        du2, wt["u2"], "ffn2_dxn_u", res=dxn2,
        rides=[_ride_scatter_chips([paired["g2"]], first), _ride_scatter_sibling([part_u2])],
        tail=_tail_norm_bwd(x2, ffn2_norm, dy, t, d, tc))
    pair("u2", part_u2)

    dmixed = _mm("mix_out_dx", [(dx2b, xspec, w_out_full, pl.BlockSpec((d, d), lambda i, k: (0, 0)))],
                 "nt", (t // tm, 1), jax.ShapeDtypeStruct((t, d), F32), xspec)

    def k_rows(width):
        return pl.BlockSpec((tk, width), lambda j, k: (k, 0))

    part_out = [
        _mm(f"mix_out_dw{i}", [(o, k_rows(width), dx2b, k_rows(d))], "tn", (1, t // tk),
            jax.ShapeDtypeStruct((width, d), BF16), pl.BlockSpec((width, d), lambda j, k: (0, 0)))
        for i, (o, width) in enumerate(((o_a, FOX_W), (o_b, SWA_W), (o_c, MEM_W)))
    ]
    part_out = jnp.concatenate(part_out, axis=0).reshape(N_DEV, d_shard, d)

    delta, delta_rep = _delta(dmixed, o_a, o_b, o_c, "attn_delta")
    (dfq, dfk, dfv, dc_col, dc_row), ((landed["g2"],), (landed["u2"],)) = _fox_bwd(
        fq, fk, fv, c_rep, c_row, dmixed, lse_a, delta_rep, "fox_bwd",
        rides=[_ride_scatter_chips([paired["g2"]], second, into=half_g2), _ride_scatter_chips([paired["u2"]])])
    dsq, dsk, dsv, dsp_sink = _swa_bwd(sq, sk, sv, sp, dmixed, lse_b, delta, "swa_bwd")
    dmq, dmk, dmv = _mem_bwd(mq, mk, mv, dmixed, lse_c, delta, "mem_bwd")

    dmkv, dsp_mem = _mem_post_bwd(mkv, sp, dmk, dmv, "mem_post_bwd")
    part_mkv = _mm("mem_kv_dw", [(mn, pl.BlockSpec((m_len, d), lambda k: (0, 0)),
                                  dmkv, pl.BlockSpec((m_len, 2 * MEM_W), lambda k: (0, 0)))],
                   "tn", (1,), jax.ShapeDtypeStruct((d, 2 * MEM_W), BF16),
                   pl.BlockSpec((d, 2 * MEM_W), lambda k: (0, 0))).reshape(N_DEV, d_shard, 2 * MEM_W)
    dmn = _mm("mem_kv_dx", [(dmkv, pl.BlockSpec((m_len, 2 * MEM_W), lambda k: (0, 0)),
                             w_mkv_full, pl.BlockSpec((d, 2 * MEM_W), lambda k: (0, 0)))],
              "nt", (1,), jax.ShapeDtypeStruct((m_len, d), F32), pl.BlockSpec((m_len, d), lambda k: (0, 0)))
    _, _, dgain_mem = _rms_bwd(mem0, mem_norm, dmn, None, "mem_norm_bwd")

    dc_row_t = _pad_row(jnp.transpose(dc_row), HEAD)
    dproj, dsp_attn = _attn_post_bwd(proj, sp, dfq, dfk, dfv, dsq, dsk, dsv, dmq, dc_col, dc_row_t, "attn_post_bwd")
    (dx1, dx1b, dgain_mix), ((got["out"], got["mkv"]),) = _mm(
        "proj_dx", [(dproj, pl.BlockSpec((tc, IN_W), lambda i, k: (i, 0)),
                     w_in_full, pl.BlockSpec((d, IN_W), lambda i, k: (0, 0), pipeline_mode=pl.Buffered(1)))],
        "nt", (t // tc, 1), jax.ShapeDtypeStruct((t, d), F32), pl.BlockSpec((tc, d), lambda i, k: (i, 0)),
        rides=[_ride_scatter_sibling([part_out, part_mkv])], tail=_tail_norm_bwd(x1, mix_norm, dx2, t, d, tc))
    pair("out", part_out)
    pair("mkv", part_mkv)
    part_in, ((landed["out"], landed["mkv"]),) = _mm(
        "proj_dw", [(hn, pl.BlockSpec((tkw, d), lambda n, k: (k, 0)),
                     dproj, pl.BlockSpec((tkw, tnw), lambda n, k: (k, n)))],
        "tn", (IN_W // tnw, t // tkw), jax.ShapeDtypeStruct((d, IN_W), BF16), pl.BlockSpec((d, tnw), lambda n, k: (0, n)),
        rides=[_ride_scatter_chips([paired["out"], paired["mkv"]])])
    part_in = part_in.reshape(N_DEV, d_shard, IN_W)

    (dg1, du1), ((got["in"],),) = _ffn_dact(dx1b, wt["d1"], a1, b1, "ffn1", rides=[_ride_scatter_sibling([part_in])])
    pair("in", part_in)
    part_d1, ((landed["in"],),) = _ffn_dw(h1, dx1b, 0.5, "ffn1_dwd", rides=[_ride_scatter_chips([paired["in"]])])
    part_g1, ((got["d1"],),) = _ffn_dw(dg1, xn1, 1.0, "ffn1_dwg", rides=[_ride_scatter_sibling([part_d1])])
    pair("d1", part_d1)
    part_u1, (half_d1, (got["g1"],)) = _ffn_dw(
        du1, xn1, 1.0, "ffn1_dwu",
        rides=[_ride_scatter_chips([paired["d1"]], first), _ride_scatter_sibling([part_g1])])
    pair("g1", part_g1)
    dxn1, ((landed["d1"],), half_g1, (got["u1"],)) = _ffn_contract(
        dg1, wt["g1"], "ffn1_dxn_g",
        rides=[_ride_scatter_chips([paired["d1"]], second, into=half_d1), _ride_scatter_chips([paired["g1"]], first),
               _ride_scatter_sibling([part_u1])])
    pair("u1", part_u1)
    (grad_x, _, dgain_ffn1), ((landed["g1"],), (landed["u1"],)) = _ffn_contract(
        du1, wt["u1"], "ffn1_dxn_u", res=dxn1,
        rides=[_ride_scatter_chips([paired["g1"]], second, into=half_g1), _ride_scatter_chips([paired["u1"]])],
        tail=_tail_norm_bwd(x0, ffn1_norm, dx1, t, d, tc))

    norms_sum, small_sum = _all_reduce_small(
        [_pack_norms(dgain_ffn1, dgain_mix, dgain_mem, dgain_ffn2), dsp_attn + dsp_sink + dsp_mem], "reduce_small")

    result = {
        "ffn1_gate": map(swap, _sum_adamw(landed["g1"], 0, gate1, swap(m_ffn1_gate), swap(v_ffn1_gate), "adamw_ffn1_gate")),
        "ffn1_up": map(swap, _sum_adamw(landed["u1"], 0, up1, swap(m_ffn1_up), swap(v_ffn1_up), "adamw_ffn1_up")),
        "ffn1_down": _sum_adamw(landed["d1"], 0, ffn1_down, m_ffn1_down, v_ffn1_down, "adamw_ffn1_down"),
        "w_mem_k": _sum_adamw(landed["mkv"], 0, w_mem_k, m_w_mem_k, v_w_mem_k, "adamw_w_mem_k"),
        "w_mem_v": _sum_adamw(landed["mkv"], 1, w_mem_v, m_w_mem_v, v_w_mem_v, "adamw_w_mem_v"),
        "w_out": _sum_adamw(landed["out"], 0, w_out, m_w_out, v_w_out, "adamw_w_out"),
        "ffn2_gate": map(swap, _sum_adamw(landed["g2"], 0, gate2, swap(m_ffn2_gate), swap(v_ffn2_gate), "adamw_ffn2_gate")),
        "ffn2_up": map(swap, _sum_adamw(landed["u2"], 0, up2, swap(m_ffn2_up), swap(v_ffn2_up), "adamw_ffn2_up")),
        "ffn2_down": _sum_adamw(landed["d2"], 0, ffn2_down, m_ffn2_down, v_ffn2_down, "adamw_ffn2_down"),
    }
    grad_in = _unpermute_in(_sum_chips(landed["in"], "sum_w_in"))
    result["w_in"] = (grad_in[None],) + tuple(
        o[None] for o in _adamw(w_in[0], grad_in, m_w_in[0], v_w_in[0], "adamw_w_in"))

    norm_names = ["ffn1_norm", "mix_norm", "mem_norm", "ffn2_norm"]
    norm_w = _pack_norms(ffn1_norm, mix_norm, mem_norm, ffn2_norm)
    norm_m = _pack_norms(m_ffn1_norm, m_mix_norm, m_mem_norm, m_ffn2_norm)
    norm_v = _pack_norms(v_ffn1_norm, v_mix_norm, v_mem_norm, v_ffn2_norm)
    outs = (norms_sum,) + tuple(_adamw(norm_w, norms_sum, norm_m, norm_v, "adamw_norms"))
    for i, k in enumerate(norm_names):
        result[k] = tuple(o[i:i + 1] for o in outs)

    small_names = ["fox_q_gain", "fox_k_gain", "swa_q_gain", "swa_k_gain", "mem_q_gain", "mem_k_gain",
                   "forget_bias", "swa_sinks"]
    small_m = _pack_small(m_fox_q_gain, m_fox_k_gain, m_swa_q_gain, m_swa_k_gain, m_mem_q_gain, m_mem_k_gain,
                          m_forget_bias, m_swa_sinks)
    small_v = _pack_small(v_fox_q_gain, v_fox_k_gain, v_swa_q_gain, v_swa_k_gain, v_mem_q_gain, v_mem_k_gain,
                          v_forget_bias, v_swa_sinks)
    outs = (small_sum,) + tuple(_adamw(sp, small_sum, small_m, small_v, "adamw_small"))
    for i, k in enumerate(small_names):
        width = N_LOGIT if k in ("forget_bias", "swa_sinks") else HEAD
        result[k] = tuple(o[i:i + 1, :width] for o in outs)

    order = ["ffn1_norm", "ffn1_gate", "ffn1_up", "ffn1_down", "mix_norm", "mem_norm", "w_in", "forget_bias",
             "w_mem_k", "w_mem_v", "fox_q_gain", "fox_k_gain", "swa_q_gain", "swa_k_gain", "swa_sinks",
             "mem_q_gain", "mem_k_gain", "w_out", "ffn2_norm", "ffn2_gate", "ffn2_up", "ffn2_down"]
    result = {k: tuple(v) for k, v in result.items()}
    flat = [loss, grad_x[None]]
    for kind in range(4):
        flat += [result[k][kind] for k in order]
    return tuple(flat)
```

```python
import functools

import jax
import jax.numpy as jnp
from jax import lax
from jax.experimental import pallas as pl
from jax.experimental.pallas import tpu as pltpu

F32 = jnp.float32
BF16 = jnp.bfloat16
MESH = pl.DeviceIdType.MESH
ANY = pl.BlockSpec(memory_space=pl.ANY)

N_DEV = 8
EPS = 1e-6
NEG_INF = -1e30
HEAD = 128
FOX_H, SWA_H, SWA_KV, MEM_H = 6, 6, 2, 4
FOX_W, SWA_W, SWA_KV_W, MEM_W = FOX_H * HEAD, SWA_H * HEAD, SWA_KV * HEAD, MEM_H * HEAD
SCALE = HEAD ** -0.5
SWA_BLOCK = 128
C_FQ, C_FK, C_FV = 0, FOX_W, 2 * FOX_W
C_SQ = 3 * FOX_W
C_SK = C_SQ + SWA_W
C_SV = C_SK + SWA_KV_W
C_MQ = C_SV + SWA_KV_W
C_FL = C_MQ + MEM_W
IN_W = C_FL + HEAD
N_LOGIT = FOX_H
R_FQ, R_FK, R_SQ, R_SK, R_MQ, R_MK, R_FB, R_SINK = range(8)
ADAM_LR, ADAM_B1, ADAM_B2, ADAM_EPS, ADAM_WD, ADAM_STEP = 0.001, 0.9, 0.999, 1e-08, 0.01, 10
VMEM_BYTES = 56 * 1024 * 1024

DN = {
    "nn": (((1,), (0,)), ((), ())),
    "nt": (((1,), (1,)), ((), ())),
    "tn": (((0,), (0,)), ((), ())),
}


def _params(n_axes):
    return pltpu.CompilerParams(dimension_semantics=("arbitrary",) * n_axes, vmem_limit_bytes=VMEM_BYTES)


def _dot(a, b, dims="nn"):
    return lax.dot_general(a.astype(BF16), b.astype(BF16), DN[dims], preferred_element_type=F32)


def _sigmoid(x):
    return 0.5 * jnp.tanh(0.5 * x) + 0.5


def _me():
    return lax.axis_index("x"), lax.axis_index("y"), lax.axis_index("c")


def _lin(p):
    return 4 * p[0] + 2 * p[1] + p[2]


def _rows_tile(rows, row_bytes, budget=4 << 20, mult=16):
    best = None
    for k in range(1, rows + 1):
        if rows % k == 0 and (rows // k) % mult == 0 and (rows // k) * row_bytes <= budget:
            best = rows // k
            break
    assert best is not None, (rows, row_bytes)
    return best


class _Ride:
    def __init__(self, inputs, out_shapes, aliases, n_remote, n_local, start, wait):
        self.inputs, self.out_shapes, self.aliases = list(inputs), list(out_shapes), dict(aliases)
        self.n_remote, self.n_local, self.start, self.wait = n_remote, n_local, start, wait


def _remote(src, dst, send, recv, k, to):
    return pltpu.make_async_remote_copy(src_ref=src, dst_ref=dst, send_sem=send.at[k], recv_sem=recv.at[k],
                                        device_id=to, device_id_type=MESH)


def _other_chips(x, y):
    return [(1 - x, y), (x, 1 - y), (1 - x, 1 - y)]


ALL_CHIPS = [(0, 0), (0, 1), (1, 0), (1, 1)]


def _rows_of(ref, rows, slot=None):
    if slot is None:
        return ref if rows is None else ref.at[pl.ds(rows[0], rows[1])]
    return ref.at[slot] if rows is None else ref.at[slot, pl.ds(rows[0], rows[1])]


def _ride_gather_chips(xs, rows=None, into=None):
    n = len(xs)

    def copies(ins, outs, send, recv):
        x, y, c = _me()
        out = []
        for a in range(n):
            for j, chip in enumerate(_other_chips(x, y)):
                peer = (*chip, c)
                src = _rows_of(ins[a], rows)
                out.append((_remote(src, _rows_of(outs[a], rows, _lin((x, y, c))), send, recv, 3 * a + j, peer),
                            _remote(src, _rows_of(outs[a], rows, _lin(peer)), send, recv, 3 * a + j, peer)))
        return out

    def mine(ins, outs, local):
        me = _lin(_me())
        return [pltpu.make_async_copy(_rows_of(ins[a], rows), _rows_of(outs[a], rows, me), local.at[a])
                for a in range(n)]

    def start(ins, outs, send, recv, local):
        for cp in mine(ins, outs, local):
            cp.start()
        for sent, _ in copies(ins, outs, send, recv):
            sent.start()

    def wait(ins, outs, send, recv, local):
        for sent, landed in copies(ins, outs, send, recv):
            landed.wait_recv()
            sent.wait_send()
        for cp in mine(ins, outs, local):
            cp.wait()

    shapes = [jax.ShapeDtypeStruct((N_DEV,) + x.shape, x.dtype) for x in xs]
    if into is None:
        return _Ride(xs, shapes, {}, 3 * n, n, start, wait)
    return _Ride(list(xs) + list(into), shapes, {n + a: a for a in range(n)}, 3 * n, n, start, wait)


def _ride_gather_sibling(bufs):
    n = len(bufs)

    def copies(outs, send, recv):
        x, y, c = _me()
        out = []
        for a in range(n):
            for q, (px, py) in enumerate(ALL_CHIPS):
                there = outs[a].at[4 * px + 2 * py + c]
                here = outs[a].at[4 * px + 2 * py + 1 - c]
                out.append((_remote(there, there, send, recv, 4 * a + q, (x, y, 1 - c)),
                            _remote(here, here, send, recv, 4 * a + q, (x, y, 1 - c))))
        return out

    def start(ins, outs, send, recv, local):
        for sent, _ in copies(outs, send, recv):
            sent.start()

    def wait(ins, outs, send, recv, local):
        for sent, landed in copies(outs, send, recv):
            landed.wait_recv()
            sent.wait_send()

    shapes = [jax.ShapeDtypeStruct(b.shape, b.dtype) for b in bufs]
    return _Ride(bufs, shapes, {a: a for a in range(n)}, 4 * n, 0, start, wait)


def _ride_gather(xs, mid_frac, rows=None, into=None):
    n = len(xs)
    chips = _ride_gather_chips(xs, rows, into)

    def sibling_copies(outs, send, recv):
        x, y, c = _me()
        out = []
        for a in range(n):
            for q, (px, py) in enumerate(ALL_CHIPS):
                there = _rows_of(outs[a], rows, 4 * px + 2 * py + c)
                here = _rows_of(outs[a], rows, 4 * px + 2 * py + 1 - c)
                k = 3 * n + 4 * a + q
                out.append((_remote(there, there, send, recv, k, (x, y, 1 - c)),
                            _remote(here, here, send, recv, k, (x, y, 1 - c))))
        return out

    def mid(ins, outs, send, recv, local):
        chips.wait(ins, outs, send, recv, local)
        for sent, _ in sibling_copies(outs, send, recv):
            sent.start()

    def wait(ins, outs, send, recv, local):
        for sent, landed in sibling_copies(outs, send, recv):
            landed.wait_recv()
            sent.wait_send()

    ride = _Ride(chips.inputs, chips.out_shapes, chips.aliases, 7 * n, n, chips.start, wait)
    ride.mid, ride.mid_frac = mid, mid_frac
    return ride


def _ride_scatter_sibling(parts):
    n = len(parts)

    def copies(ins, outs, send, recv):
        x, y, c = _me()
        out = []
        for a in range(n):
            for q, (px, py) in enumerate(ALL_CHIPS):
                cp = _remote(ins[a].at[4 * px + 2 * py + 1 - c], outs[a].at[q], send, recv, 4 * a + q, (x, y, 1 - c))
                out.append(cp)
        return out

    def start(ins, outs, send, recv, local):
        for cp in copies(ins, outs, send, recv):
            cp.start()

    def wait(ins, outs, send, recv, local):
        for cp in copies(ins, outs, send, recv):
            cp.wait_recv()
            cp.wait_send()

    shapes = [jax.ShapeDtypeStruct((4,) + p.shape[1:], p.dtype) for p in parts]
    return _Ride(parts, shapes, {}, 4 * n, 0, start, wait)


def _ride_scatter_chips(pairs, rows=None, into=None):
    n = len(pairs)

    def part(ref, slot):
        return _rows_of(ref, rows, slot)

    def copies(ins, outs, send, recv):
        x, y, c = _me()
        out = []
        for a in range(n):
            for j, (px, py) in enumerate(_other_chips(x, y)):
                peer = (px, py, c)
                src = part(ins[a], 2 * px + py)
                out.append((_remote(src, part(outs[a], 2 * x + y), send, recv, 3 * a + j, peer),
                            _remote(src, part(outs[a], 2 * px + py), send, recv, 3 * a + j, peer)))
        return out

    def mine(ins, outs, local):
        x, y, _ = _me()
        return [pltpu.make_async_copy(part(ins[a], 2 * x + y), part(outs[a], 2 * x + y), local.at[a])
                for a in range(n)]

    def start(ins, outs, send, recv, local):
        for cp in mine(ins, outs, local):
            cp.start()
        for sent, _ in copies(ins, outs, send, recv):
            sent.start()

    def wait(ins, outs, send, recv, local):
        for sent, landed in copies(ins, outs, send, recv):
            landed.wait_recv()
            sent.wait_send()
        for cp in mine(ins, outs, local):
            cp.wait()

    shapes = [jax.ShapeDtypeStruct(p.shape, p.dtype) for p in pairs]
    if into is None:
        return _Ride(pairs, shapes, {}, 3 * n, n, start, wait)
    return _Ride(list(pairs) + list(into), shapes, {n + a: a for a in range(n)}, 3 * n, n, start, wait)


def _call(name, body, grid, in_specs, out_specs, out_shape, operands, scratch=(), rides=()):
    n_in, n_out, n_scr = len(operands), len(out_shape), len(scratch)
    ride_in, ride_out, ride_scr, aliases, spans = [], [], [], {}, []
    for r in rides:
        for i, o in r.aliases.items():
            aliases[n_in + len(ride_in) + i] = n_out + len(ride_out) + o
        spans.append((len(ride_in), len(r.inputs), len(ride_out), len(r.out_shapes)))
        ride_in += r.inputs
        ride_out += r.out_shapes
        ride_scr += [pltpu.SemaphoreType.DMA((r.n_remote,)), pltpu.SemaphoreType.DMA((r.n_remote,)),
                     pltpu.SemaphoreType.DMA((max(r.n_local, 1),))]

    def wrapped(*refs):
        c_in, r_in = refs[:n_in], refs[n_in:n_in + len(ride_in)]
        p = n_in + len(ride_in)
        c_out, r_out = refs[p:p + n_out], refs[p + n_out:p + n_out + len(ride_out)]
        p += n_out + len(ride_out)
        c_scr, r_scr = refs[p:p + n_scr], refs[p + n_scr:]

        n_steps = functools.reduce(lambda a, b: a * b, grid, 1)
        step = functools.reduce(lambda acc, ax: acc * grid[ax] + pl.program_id(ax), range(len(grid)), 0)

        def each(method, at):
            for k, (r, (i0, ni, o0, no)) in enumerate(zip(rides, spans)):
                fn = getattr(r, method, None)
                if fn is None:
                    continue
                run = functools.partial(fn, r_in[i0:i0 + ni], r_out[o0:o0 + no], *r_scr[3 * k:3 * k + 3])
                if grid:
                    pl.when(step == at(r))(run)
                else:
                    run()

        each("start", lambda r: 0)
        each("mid", lambda r: int(r.mid_frac * (n_steps - 1)))
        body(*c_in, *c_out, *c_scr)
        each("wait", lambda r: n_steps - 1)

    outs = pl.pallas_call(
        wrapped, grid=grid, in_specs=list(in_specs) + [ANY] * len(ride_in),
        out_specs=list(out_specs) + [ANY] * len(ride_out), out_shape=list(out_shape) + ride_out,
        scratch_shapes=list(scratch) + ride_scr, input_output_aliases=aliases,
        compiler_params=_params(len(grid)), name=name)(*operands, *ride_in)
    outs = list(outs)
    ride_results = [outs[n_out + o0:n_out + o0 + no] for (_, _, o0, no) in spans]
    return outs[:n_out], ride_results


def _only_copies(name, rides):
    return _call(name, lambda: None, (), [], [], [], [], rides=rides)[1]


class _Tail:
    def __init__(self, extra, out_shapes, out_specs, fn):
        self.extra, self.out_shapes, self.out_specs, self.fn = list(extra), list(out_shapes), list(out_specs), fn


def _mm(name, pairs, dims, grid, out_shape, out_spec, res=None, res_spec=None, alpha=1.0, rides=(), tail=None):
    n = len(pairs)
    nk = grid[-1]
    kax = len(grid) - 1
    acc_shape = tuple(d for d in out_spec.block_shape if d is not None)
    n_extra = len(tail.extra) if tail else 0
    n_outs = len(tail.out_shapes) if tail else 1

    def body(*refs):
        pos = 2 * n
        r_ref = None
        if res is not None:
            r_ref = refs[pos]
            pos += 1
        x_refs = refs[pos:pos + n_extra]
        o_refs = refs[pos + n_extra:pos + n_extra + n_outs]
        pos += n_extra + n_outs
        part = None
        for p in range(n):
            d = _dot(refs[2 * p][...], refs[2 * p + 1][...], dims)
            part = d if part is None else part + d

        def finish(acc):
            if alpha != 1.0:
                acc = acc * alpha
            if r_ref is not None:
                acc = r_ref[...] + acc
            if tail:
                tail.fn(acc, x_refs, o_refs)
            else:
                o_refs[0][...] = acc.astype(o_refs[0].dtype)

        if nk == 1:
            finish(part)
        else:
            acc_ref = refs[pos]
            k = pl.program_id(kax)

            @pl.when(k == 0)
            def _():
                acc_ref[...] = part

            @pl.when(k > 0)
            def _():
                acc_ref[...] += part

            @pl.when(k == nk - 1)
            def _():
                finish(acc_ref[...])

    operands, in_specs = [], []
    for a, a_spec, b, b_spec in pairs:
        operands += [a, b]
        in_specs += [a_spec, b_spec]
    if res is not None:
        operands.append(res)
        in_specs.append(res_spec)
    for a, a_spec in (tail.extra if tail else []):
        operands.append(a)
        in_specs.append(a_spec)
    outs, ride_results = _call(name, body, grid, in_specs, tail.out_specs if tail else [out_spec],
                               tail.out_shapes if tail else [out_shape], operands,
                               scratch=[pltpu.VMEM(acc_shape, F32)] if nk > 1 else [], rides=rides)
    outs = outs if tail else outs[0]
    return (outs, ride_results) if rides else outs


def _accumulate(ref, part):
    @pl.when(pl.program_id(0) == 0)
    def _():
        ref[...] = part

    @pl.when(pl.program_id(0) > 0)
    def _():
        ref[...] += part


def _tail_norm(gain, t, d, tm):
    def fn(v, x_refs, o_refs):
        o_refs[0][...] = v
        r = lax.rsqrt(jnp.mean(v * v, axis=-1, keepdims=True) + EPS)
        o_refs[1][...] = (v * r * x_refs[0][...]).astype(BF16)

    rows = pl.BlockSpec((tm, d), lambda i, k: (i, 0))
    return _Tail([(gain, pl.BlockSpec((1, d), lambda i, k: (0, 0)))],
                 [jax.ShapeDtypeStruct((t, d), F32), jax.ShapeDtypeStruct((t, d), BF16)], [rows, rows], fn)


def _tail_loss(target, t, d, tm):
    def fn(v, x_refs, o_refs):
        err = v - x_refs[0][...]
        dy = err * (1.0 / d)
        o_refs[0][...] = dy
        o_refs[1][...] = dy.astype(BF16)
        _accumulate(o_refs[2], jnp.zeros((8, 128), F32) + jnp.sum(err * err))

    rows = pl.BlockSpec((tm, d), lambda i, k: (i, 0))
    return _Tail([(target, rows)],
                 [jax.ShapeDtypeStruct((t, d), F32), jax.ShapeDtypeStruct((t, d), BF16),
                  jax.ShapeDtypeStruct((8, 128), F32)],
                 [rows, rows, pl.BlockSpec((8, 128), lambda i, k: (0, 0))], fn)


def _tail_norm_bwd(x, gain, dres, t, d, tm):
    def fn(dy, x_refs, o_refs):
        xv = x_refs[0][...]
        r = lax.rsqrt(jnp.mean(xv * xv, axis=-1, keepdims=True) + EPS)
        xh = xv * r
        dxh = dy * x_refs[1][...]
        dx = r * (dxh - xh * jnp.mean(dxh * xh, axis=-1, keepdims=True)) + x_refs[2][...]
        o_refs[0][...] = dx
        o_refs[1][...] = dx.astype(BF16)
        _accumulate(o_refs[2], jnp.sum(dy * xh, axis=0, keepdims=True))

    rows = pl.BlockSpec((tm, d), lambda i, k: (i, 0))
    vec = pl.BlockSpec((1, d), lambda i, k: (0, 0))
    return _Tail([(x, rows), (gain, vec), (dres, rows)],
                 [jax.ShapeDtypeStruct((t, d), F32), jax.ShapeDtypeStruct((t, d), BF16),
                  jax.ShapeDtypeStruct((1, d), F32)], [rows, rows, vec], fn)


def _cast_bf16(x, name):
    rows, cols = x.shape
    tm = _rows_tile(rows, cols * 4)

    def body(x_ref, o_ref):
        o_ref[...] = x_ref[...].astype(BF16)

    spec = pl.BlockSpec((tm, cols), lambda i: (i, 0))
    return pl.pallas_call(body, grid=(rows // tm,), in_specs=[spec], out_specs=spec,
                          out_shape=jax.ShapeDtypeStruct(x.shape, BF16), compiler_params=_params(1), name=name)(x)


def _rms_fwd(x, gain, name):
    rows, d = x.shape
    tm = min(rows, 512)

    def body(x_ref, g_ref, o_ref):
        xv = x_ref[...]
        r = lax.rsqrt(jnp.mean(xv * xv, axis=-1, keepdims=True) + EPS)
        o_ref[...] = (xv * r * g_ref[...]).astype(BF16)

    spec = pl.BlockSpec((tm, d), lambda i: (i, 0))
    return pl.pallas_call(body, grid=(rows // tm,), in_specs=[spec, pl.BlockSpec((1, d), lambda i: (0, 0))],
                          out_specs=spec, out_shape=jax.ShapeDtypeStruct(x.shape, BF16),
                          compiler_params=_params(1), name=name)(x, gain)


def _rms_bwd(x, gain, dxn, dres, name, rides=()):
    rows, d = x.shape
    tm = min(rows, 256)
    with_res = dres is not None

    def body(*refs):
        if with_res:
            x_ref, g_ref, dy_ref, r_ref, dx_ref, dxb_ref, dg_ref = refs
        else:
            x_ref, g_ref, dy_ref, dx_ref, dxb_ref, dg_ref = refs
        xv = x_ref[...]
        r = lax.rsqrt(jnp.mean(xv * xv, axis=-1, keepdims=True) + EPS)
        xh = xv * r
        dy = dy_ref[...]
        dxh = dy * g_ref[...]
        dx = r * (dxh - xh * jnp.mean(dxh * xh, axis=-1, keepdims=True))
        if with_res:
            dx = dx + r_ref[...]
        dx_ref[...] = dx
        dxb_ref[...] = dx.astype(BF16)
        part = jnp.sum(dy * xh, axis=0, keepdims=True)

        @pl.when(pl.program_id(0) == 0)
        def _():
            dg_ref[...] = part

        @pl.when(pl.program_id(0) > 0)
        def _():
            dg_ref[...] += part

    spec = pl.BlockSpec((tm, d), lambda i: (i, 0))
    vec = pl.BlockSpec((1, d), lambda i: (0, 0))
    ops = [x, gain, dxn] + ([dres] if with_res else [])
    outs, ride_results = _call(
        name, body, (rows // tm,), [spec, vec, spec] + ([spec] if with_res else []), [spec, spec, vec],
        [jax.ShapeDtypeStruct(x.shape, F32), jax.ShapeDtypeStruct(x.shape, BF16), jax.ShapeDtypeStruct((1, d), F32)],
        ops, rides=rides)
    return (outs, ride_results) if rides else outs


ROW_CHUNK = 256


def _ffn_up(xn, wg, wu, tag, rides=()):
    t, d = xn.shape
    nd, fs, _ = wg.shape
    tm = min(t, 512)
    rc = min(tm, ROW_CHUNK)

    def body(x_ref, wg_ref, wu_ref, a_ref, b_ref, h_ref):
        for r in range(0, tm, rc):
            xv = x_ref[r:r + rc, :]
            g = _dot(xv, wg_ref[...], "nt")
            u = _dot(xv, wu_ref[...], "nt")
            sig = _sigmoid(g)
            silu = g * sig
            a_ref[r:r + rc, :] = (0.5 * u * (sig + silu * (1.0 - sig))).astype(BF16)
            b_ref[r:r + rc, :] = (0.5 * silu).astype(BF16)
            h_ref[r:r + rc, :] = (silu * u).astype(BF16)

    wspec = pl.BlockSpec((None, fs, d), lambda j, i: (j, 0, 0))
    hspec = pl.BlockSpec((None, tm, fs), lambda j, i: (j, i, 0))
    hid = jax.ShapeDtypeStruct((nd, t, fs), BF16)
    return _call(f"{tag}_up", body, (nd, t // tm), [pl.BlockSpec((tm, d), lambda j, i: (i, 0)), wspec, wspec],
                 [hspec] * 3, [hid] * 3, [xn, wg, wu], rides=rides)


CONTRACT_ROWS = 256


def _ffn_contract(hid, w, name, res=None, alpha=1.0, rides=(), tail=None):
    nd, t, fs = hid.shape
    d = w.shape[2]
    tm = min(t, CONTRACT_ROWS)
    xspec = pl.BlockSpec((tm, d), lambda i, k: (i, 0))
    pairs = [(hid, pl.BlockSpec((None, tm, fs), lambda i, k, s=s: (s, i, 0)),
              w, pl.BlockSpec((None, fs, d), lambda i, k, s=s: (s, 0, 0), pipeline_mode=pl.Buffered(1)))
             for s in range(nd)]
    return _mm(name, pairs, "nn", (t // tm, 1), jax.ShapeDtypeStruct((t, d), F32), xspec,
               res=res, res_spec=xspec if res is not None else None, alpha=alpha, rides=rides, tail=tail)


def _ffn_down(x, h, wd, tag, rides=(), tail=None):
    return _ffn_contract(h, wd, f"{tag}_down", res=x, alpha=0.5, rides=rides, tail=tail)


def _ffn_dact(dyb, wd, a, b, tag, rides=()):
    nd, t, fs = a.shape
    d = dyb.shape[1]
    tm = min(t, 512)
    rc = min(tm, ROW_CHUNK)

    def body(dy_ref, wd_ref, a_ref, b_ref, dg_ref, du_ref):
        for r in range(0, tm, rc):
            dh = _dot(dy_ref[r:r + rc, :], wd_ref[...], "nt")
            dg_ref[r:r + rc, :] = (dh * a_ref[r:r + rc, :].astype(F32)).astype(BF16)
            du_ref[r:r + rc, :] = (dh * b_ref[r:r + rc, :].astype(F32)).astype(BF16)

    hspec = pl.BlockSpec((None, tm, fs), lambda j, i: (j, i, 0))
    hid = jax.ShapeDtypeStruct((nd, t, fs), BF16)
    return _call(f"{tag}_dact", body, (nd, t // tm),
                 [pl.BlockSpec((tm, d), lambda j, i: (i, 0)), pl.BlockSpec((None, fs, d), lambda j, i: (j, 0, 0)),
                  hspec, hspec], [hspec] * 2, [hid] * 2, [dyb, wd, a, b], rides=rides)


def _ffn_dw(hid, act, alpha, name, rides=()):
    nd, t, fs = hid.shape
    d = act.shape[1]
    tk = t
    return _mm(name, [(hid, pl.BlockSpec((None, tk, fs), lambda j, k: (j, k, 0)),
                       act, pl.BlockSpec((tk, d), lambda j, k: (k, 0), pipeline_mode=pl.Buffered(1)))],
               "tn", (nd, t // tk),
               jax.ShapeDtypeStruct((nd, fs, d), BF16), pl.BlockSpec((None, fs, d), lambda j, k: (j, 0, 0)),
               alpha=alpha, rides=rides)


def _head_norm(x, gain):
    r = lax.rsqrt(jnp.mean(x * x, axis=-1, keepdims=True) + EPS)
    return x * r * gain


def _head_norm_bwd(x, gain, dy):
    r = lax.rsqrt(jnp.mean(x * x, axis=-1, keepdims=True) + EPS)
    xh = x * r
    dxh = dy * gain
    dx = r * (dxh - xh * jnp.mean(dxh * xh, axis=-1, keepdims=True))
    return dx, jnp.sum(dy * xh, axis=0, keepdims=True)


def _hs(h, base=0):
    return slice(base + h * HEAD, base + (h + 1) * HEAD)


def _tri(n, lower):
    r = lax.broadcasted_iota(jnp.int32, (n, n), 0)
    c = lax.broadcasted_iota(jnp.int32, (n, n), 1)
    return ((r >= c) if lower else (r <= c)).astype(F32)


def _attn_pre(proj, sp, name):
    t = proj.shape[0]
    tm = min(t, 256)

    def body(p_ref, sp_ref, fq, fk, fv, sq, sk, sv, mq, cc, carry):
        @pl.when(pl.program_id(0) == 0)
        def _():
            carry[...] = jnp.zeros_like(carry)

        for h in range(FOX_H):
            fq[:, _hs(h)] = _head_norm(p_ref[:, _hs(h, C_FQ)], sp_ref[R_FQ:R_FQ + 1, :]).astype(BF16)
            fk[:, _hs(h)] = _head_norm(p_ref[:, _hs(h, C_FK)], sp_ref[R_FK:R_FK + 1, :]).astype(BF16)
        fv[...] = p_ref[:, C_FV:C_FV + FOX_W].astype(BF16)
        for h in range(SWA_H):
            sq[:, _hs(h)] = _head_norm(p_ref[:, _hs(h, C_SQ)], sp_ref[R_SQ:R_SQ + 1, :]).astype(BF16)
        for h in range(SWA_KV):
            sk[:, _hs(h)] = _head_norm(p_ref[:, _hs(h, C_SK)], sp_ref[R_SK:R_SK + 1, :]).astype(BF16)
        sv[...] = p_ref[:, C_SV:C_SV + SWA_KV_W].astype(BF16)
        for h in range(MEM_H):
            mq[:, _hs(h)] = _head_norm(p_ref[:, _hs(h, C_MQ)], sp_ref[R_MQ:R_MQ + 1, :]).astype(BF16)
        z = p_ref[:, C_FL:C_FL + HEAD] + sp_ref[R_FB:R_FB + 1, :]
        lane = lax.broadcasted_iota(jnp.int32, z.shape, 1)
        log_f = jnp.minimum(z, 0.0) - jnp.log(1.0 + jnp.exp(-jnp.abs(z)))
        log_f = jnp.where(lane < N_LOGIT, log_f, 0.0)
        c = jnp.dot(_tri(tm, True), log_f, precision=lax.Precision.HIGHEST, preferred_element_type=F32)
        c = c + carry[0:1, :]
        cc[...] = c
        carry[...] = jnp.broadcast_to(c[tm - 1:tm, :], carry.shape)

    def rows(w):
        return pl.BlockSpec((tm, w), lambda i: (i, 0))

    def shape(w, dt):
        return jax.ShapeDtypeStruct((t, w), dt)

    widths = [FOX_W, FOX_W, FOX_W, SWA_W, SWA_KV_W, SWA_KV_W, MEM_W]
    return pl.pallas_call(
        body, grid=(t // tm,), in_specs=[rows(IN_W), pl.BlockSpec((16, 128), lambda i: (0, 0))],
        out_specs=[rows(w) for w in widths] + [rows(HEAD)],
        out_shape=[shape(w, BF16) for w in widths] + [shape(HEAD, F32)],
        scratch_shapes=[pltpu.VMEM((8, 128), F32)], compiler_params=_params(1), name=name)(proj, sp)


def _attn_post_bwd(proj, sp, dfq, dfk, dfv, dsq, dsk, dsv, dmq, dc_col, dc_row_t, name):
    t = proj.shape[0]
    tm = min(t, 256)
    nb = t // tm

    def body(p_ref, sp_ref, dfq_r, dfk_r, dfv_r, dsq_r, dsk_r, dsv_r, dmq_r, dcc_r, dcr_r, dp_ref, dsp_ref, carry):
        @pl.when(pl.program_id(0) == 0)
        def _():
            carry[...] = jnp.zeros_like(carry)
            dsp_ref[...] = jnp.zeros_like(dsp_ref)

        def group(n_heads, col, row, d_ref):
            total = None
            for h in range(n_heads):
                dx, dg = _head_norm_bwd(p_ref[:, _hs(h, col)], sp_ref[row:row + 1, :], d_ref[:, _hs(h)])
                dp_ref[:, _hs(h, col)] = dx.astype(BF16)
                total = dg if total is None else total + dg
            dsp_ref[row:row + 1, :] += total

        group(FOX_H, C_FQ, R_FQ, dfq_r)
        group(FOX_H, C_FK, R_FK, dfk_r)
        dp_ref[:, C_FV:C_FV + FOX_W] = dfv_r[...].astype(BF16)
        group(SWA_H, C_SQ, R_SQ, dsq_r)
        group(SWA_KV, C_SK, R_SK, dsk_r)
        dp_ref[:, C_SV:C_SV + SWA_KV_W] = dsv_r[...].astype(BF16)
        group(MEM_H, C_MQ, R_MQ, dmq_r)
        dc = dcc_r[...] - dcr_r[...]
        rc = jnp.dot(_tri(tm, False), dc, precision=lax.Precision.HIGHEST, preferred_element_type=F32)
        rc = rc + carry[0:1, :]
        carry[...] = jnp.broadcast_to(rc[0:1, :], carry.shape)
        z = p_ref[:, C_FL:C_FL + HEAD] + sp_ref[R_FB:R_FB + 1, :]
        dz = rc * _sigmoid(-z)
        dp_ref[:, C_FL:C_FL + HEAD] = dz.astype(BF16)
        dsp_ref[R_FB:R_FB + 1, :] += jnp.sum(dz, axis=0, keepdims=True)

    def rows(w):
        return pl.BlockSpec((tm, w), lambda i: (nb - 1 - i, 0))

    small = pl.BlockSpec((16, 128), lambda i: (0, 0))
    widths = [FOX_W, FOX_W, FOX_W, SWA_W, SWA_KV_W, SWA_KV_W, MEM_W, HEAD, HEAD]
    return pl.pallas_call(
        body, grid=(nb,), in_specs=[rows(IN_W), small] + [rows(w) for w in widths],
        out_specs=[rows(IN_W), small],
        out_shape=[jax.ShapeDtypeStruct((t, IN_W), BF16), jax.ShapeDtypeStruct((16, 128), F32)],
        scratch_shapes=[pltpu.VMEM((8, 128), F32)], compiler_params=_params(1), name=name,
    )(proj, sp, dfq, dfk, dfv, dsq, dsk, dsv, dmq, dc_col, dc_row_t)


def _head_column(values):
    rows = values[0].shape[0]
    lane = lax.broadcasted_iota(jnp.int32, (rows, HEAD), 1)
    out = jnp.zeros((rows, HEAD), F32)
    for h, v in enumerate(values):
        out = jnp.where(lane == h, v, out)
    return out


def _head_row(values, n_rows=8):
    cols = values[0].shape[1]
    sub = lax.broadcasted_iota(jnp.int32, (n_rows, cols), 0)
    out = jnp.zeros((n_rows, cols), F32)
    for h, v in enumerate(values):
        out = jnp.where(sub == h, v, out)
    return out


def _delta(dmixed, o_a, o_b, o_c, name):
    t = dmixed.shape[0]
    tm = min(t, 512)

    def body(d_ref, a_ref, b_ref, c_ref, o_ref, rep_ref):
        cols = []
        for ref, n_heads, base in ((a_ref, FOX_H, 0), (b_ref, SWA_H, FOX_W), (c_ref, MEM_H, FOX_W + SWA_W)):
            for h in range(n_heads):
                cols.append(jnp.sum(d_ref[:, _hs(h, base)] * ref[:, _hs(h)], axis=-1, keepdims=True))
        o_ref[...] = _head_column(cols)
        for h in range(FOX_H):
            rep_ref[h] = jnp.broadcast_to(cols[h], (tm, HEAD))

    def rows(w):
        return pl.BlockSpec((tm, w), lambda i: (i, 0))

    return pl.pallas_call(body, grid=(t // tm,), in_specs=[rows(dmixed.shape[1]), rows(FOX_W), rows(SWA_W), rows(MEM_W)],
                          out_specs=[rows(HEAD), pl.BlockSpec((FOX_H, tm, HEAD), lambda i: (0, i, 0))],
                          out_shape=[jax.ShapeDtypeStruct((t, HEAD), F32), jax.ShapeDtypeStruct((FOX_H, t, HEAD), F32)],
                          compiler_params=_params(1), name=name)(dmixed, o_a, o_b, o_c)


def _fox_fwd(fq, fk, fv, c_rep, c_row, name, rides=()):
    t = fq.shape[0]
    tb = min(t, 512)
    nb = t // tb
    n_tiles = tb // HEAD

    def body(q_ref, k_ref, v_ref, cc_ref, cr_ref, o_ref, lse_ref, m_s, l_s, acc_s):
        qi, ki = pl.program_id(0), pl.program_id(1)

        @pl.when(ki == 0)
        def _():
            m_s[...] = jnp.full_like(m_s, NEG_INF)
            l_s[...] = jnp.zeros_like(l_s)
            acc_s[...] = jnp.zeros_like(acc_s)

        def step(diagonal):
            if diagonal:
                r = lax.broadcasted_iota(jnp.int32, (tb, HEAD), 0)
                c = lax.broadcasted_iota(jnp.int32, (tb, HEAD), 1)
            for h in range(FOX_H):
                s = _dot(q_ref[:, _hs(h)], k_ref[:, _hs(h)], "nt")
                cc = cc_ref[h]
                tiles, m_cur = [], None
                for j in range(n_tiles):
                    st = s[:, _hs(j)] * SCALE + cc - cr_ref[h:h + 1, _hs(j)]
                    if diagonal:
                        st = jnp.where(r >= c + j * HEAD, st, NEG_INF)
                    tiles.append(st)
                    m_cur = st if m_cur is None else jnp.maximum(m_cur, st)
                m_prev = m_s[h]
                m_new = jnp.maximum(m_prev, jnp.max(m_cur, axis=-1, keepdims=True))
                alpha = jnp.exp(m_prev - m_new)
                ps = [jnp.exp(st - m_new) for st in tiles]
                l_cur = ps[0]
                for p in ps[1:]:
                    l_cur = l_cur + p
                l_s[h] = alpha * l_s[h] + jnp.sum(l_cur, axis=-1, keepdims=True)
                p = jnp.concatenate([p.astype(BF16) for p in ps], axis=1)
                acc_s[:, _hs(h)] = alpha * acc_s[:, _hs(h)] + _dot(p, v_ref[:, _hs(h)])
                m_s[h] = m_new

        @pl.when(ki < qi)
        def _():
            step(False)

        @pl.when(ki == qi)
        def _():
            step(True)
            for h in range(FOX_H):
                o_ref[:, _hs(h)] = acc_s[:, _hs(h)] / l_s[h]
                lse_ref[h] = m_s[h] + jnp.log(l_s[h])

    qspec = pl.BlockSpec((tb, FOX_W), lambda i, j: (i, 0))
    kspec = pl.BlockSpec((tb, FOX_W), lambda i, j: (jnp.minimum(i, j), 0))
    rep = pl.BlockSpec((FOX_H, tb, HEAD), lambda i, j: (0, i, 0))
    return _call(
        name, body, (nb, nb),
        [qspec, kspec, kspec, rep, pl.BlockSpec((8, tb), lambda i, j: (0, jnp.minimum(i, j)))],
        [qspec, rep],
        [jax.ShapeDtypeStruct((t, FOX_W), F32), jax.ShapeDtypeStruct((FOX_H, t, HEAD), F32)],
        [fq, fk, fv, c_rep, c_row],
        scratch=[pltpu.VMEM((FOX_H, tb, HEAD), F32), pltpu.VMEM((FOX_H, tb, HEAD), F32), pltpu.VMEM((tb, FOX_W), F32)],
        rides=rides)


def _fox_bwd(fq, fk, fv, c_rep, c_row, dmixed, lse, delta, name, rides=()):
    t = fq.shape[0]
    tb = min(t, 512)
    nb = t // tb
    n_tiles = tb // HEAD

    def body(q_ref, k_ref, v_ref, cc_ref, cr_ref, do_ref, lse_ref, dl_ref,
             dq_ref, dk_ref, dv_ref, dcc_ref, dcr_ref):
        ki, qi = pl.program_id(0), pl.program_id(1)

        @pl.when((ki == 0) & (qi == 0))
        def _():
            dq_ref[...] = jnp.zeros_like(dq_ref)
            dcc_ref[...] = jnp.zeros_like(dcc_ref)

        @pl.when(qi == 0)
        def _():
            dk_ref[...] = jnp.zeros_like(dk_ref)
            dv_ref[...] = jnp.zeros_like(dv_ref)
            dcr_ref[...] = jnp.zeros_like(dcr_ref)

        def step(diagonal):
            rows = pl.ds(pl.multiple_of(qi * tb, tb), tb)
            if diagonal:
                r = lax.broadcasted_iota(jnp.int32, (tb, HEAD), 0)
                c = lax.broadcasted_iota(jnp.int32, (tb, HEAD), 1)
            row_sums, col_sums = [], []
            for h in range(FOX_H):
                q, k, v, do = q_ref[:, _hs(h)], k_ref[:, _hs(h)], v_ref[:, _hs(h)], do_ref[:, _hs(h)]
                s = _dot(q, k, "nt")
                dp = _dot(do, v, "nt")
                cc, lse_h, dl_h = cc_ref[h], lse_ref[h], dl_ref[h]
                ps, dss, row = [], [], None
                for j in range(n_tiles):
                    st = s[:, _hs(j)] * SCALE + cc - cr_ref[h:h + 1, _hs(j)]
                    if diagonal:
                        st = jnp.where(r >= c + j * HEAD, st, NEG_INF)
                    pt = jnp.exp(st - lse_h)
                    dst = pt * (dp[:, _hs(j)] - dl_h)
                    ps.append(pt.astype(BF16))
                    dss.append(dst)
                    row = dst if row is None else row + dst
                p = jnp.concatenate(ps, axis=1)
                ds = jnp.concatenate(dss, axis=1)
                dsb = ds.astype(BF16)
                dv_ref[:, _hs(h)] += _dot(p, do, "tn")
                dk_ref[:, _hs(h)] += _dot(dsb, q, "tn") * SCALE
                dq_ref[rows, _hs(h)] += _dot(dsb, k) * SCALE
                row_sums.append(jnp.sum(row, axis=1, keepdims=True))
                col_sums.append(jnp.sum(ds, axis=0, keepdims=True))
            dcc_ref[rows, :] += _head_column(row_sums)
            dcr_ref[...] += _head_row(col_sums)

        @pl.when(qi > ki)
        def _():
            step(False)

        @pl.when(qi == ki)
        def _():
            step(True)

    def qmap(j, i):
        return (jnp.maximum(i, j), 0)

    qspec = pl.BlockSpec((tb, FOX_W), qmap)
    kspec = pl.BlockSpec((tb, FOX_W), lambda j, i: (j, 0))
    rep = pl.BlockSpec((FOX_H, tb, HEAD), lambda j, i: (0, jnp.maximum(i, j), 0))
    rowspec = pl.BlockSpec((8, tb), lambda j, i: (0, j))
    return _call(
        name, body, (nb, nb), [qspec, kspec, kspec, rep, rowspec, qspec, rep, rep],
        [pl.BlockSpec((t, FOX_W), lambda j, i: (0, 0)), kspec, kspec,
         pl.BlockSpec((t, HEAD), lambda j, i: (0, 0)), rowspec],
        [jax.ShapeDtypeStruct((t, FOX_W), F32)] * 3 + [jax.ShapeDtypeStruct((t, HEAD), F32),
                                                       jax.ShapeDtypeStruct((8, t), F32)],
        [fq, fk, fv, c_rep, c_row, dmixed, lse, delta], rides=rides)


def _swa_logits(q, k_cur, k_prev, slope, first_block):
    w = SWA_BLOCK
    r = lax.broadcasted_iota(jnp.int32, (w, w), 0)
    j = lax.broadcasted_iota(jnp.int32, (w, w), 1)
    dist_cur = r - j
    dist_prev = w + r - j
    s_cur = _dot(q, k_cur, "nt") * SCALE - slope * dist_cur.astype(F32)
    s_cur = jnp.where(dist_cur >= 0, s_cur, NEG_INF)
    s_prev = _dot(q, k_prev, "nt") * SCALE - slope * dist_prev.astype(F32)
    s_prev = jnp.where((j > r) & jnp.logical_not(first_block), s_prev, NEG_INF)
    return s_cur, s_prev


def _slope(h):
    return float(2.0 ** (-8.0 * (h + 1) / SWA_H))


def _swa_fwd(sq, sk, sv, sp, name, rides=()):
    t = sq.shape[0]
    w = SWA_BLOCK
    nb = t // w
    group = SWA_H // SWA_KV

    def body(q_ref, kp_ref, kc_ref, vp_ref, vc_ref, sp_ref, o_ref, lse_ref):
        first = pl.program_id(0) == 0
        lses = []
        for h in range(SWA_H):
            kv = h // group
            s_cur, s_prev = _swa_logits(q_ref[:, _hs(h)], kc_ref[:, _hs(kv)], kp_ref[:, _hs(kv)], _slope(h), first)
            sink = sp_ref[R_SINK:R_SINK + 1, h:h + 1]
            m = jnp.maximum(jnp.maximum(jnp.max(s_cur, axis=-1, keepdims=True),
                                        jnp.max(s_prev, axis=-1, keepdims=True)), sink)
            p_cur = jnp.exp(s_cur - m)
            p_prev = jnp.exp(s_prev - m)
            l = jnp.sum(p_cur, axis=-1, keepdims=True) + jnp.sum(p_prev, axis=-1, keepdims=True) + jnp.exp(sink - m)
            o_ref[:, _hs(h)] = (_dot(p_cur, vc_ref[:, _hs(kv)]) + _dot(p_prev, vp_ref[:, _hs(kv)])) / l
            lses.append(m + jnp.log(l))
        lse_ref[...] = _head_column(lses)

    qspec = pl.BlockSpec((w, SWA_W), lambda n: (n, 0))
    cur = pl.BlockSpec((w, SWA_KV_W), lambda n: (n, 0))
    prev = pl.BlockSpec((w, SWA_KV_W), lambda n: (jnp.maximum(n - 1, 0), 0))
    return _call(
        name, body, (nb,), [qspec, prev, cur, prev, cur, pl.BlockSpec((16, 128), lambda n: (0, 0))],
        [qspec, pl.BlockSpec((w, HEAD), lambda n: (n, 0))],
        [jax.ShapeDtypeStruct((t, SWA_W), F32), jax.ShapeDtypeStruct((t, HEAD), F32)],
        [sq, sk, sk, sv, sv, sp], rides=rides)


def _swa_bwd(sq, sk, sv, sp, dmixed, lse, delta, name):
    t = sq.shape[0]
    w = SWA_BLOCK
    nb = t // w
    group = SWA_H // SWA_KV
    do_block = FOX_W // SWA_W
    assert FOX_W % SWA_W == 0

    def body(q_ref, kp_ref, kc_ref, vp_ref, vc_ref, sp_ref, do_ref, lse_ref, dl_ref,
             dq_ref, dk_ref, dv_ref, dsp_ref, ck, cv):
        step = pl.program_id(0)
        first = step == nb - 1

        @pl.when(step == 0)
        def _():
            ck[...] = jnp.zeros_like(ck)
            cv[...] = jnp.zeros_like(cv)
            dsp_ref[...] = jnp.zeros_like(dsp_ref)

        dk_cur = [None] * SWA_KV
        dk_prev = [None] * SWA_KV
        dv_cur = [None] * SWA_KV
        dv_prev = [None] * SWA_KV
        dsinks = []

        def add(lst, i, v):
            lst[i] = v if lst[i] is None else lst[i] + v

        for h in range(SWA_H):
            kv = h // group
            q, do = q_ref[:, _hs(h)], do_ref[:, _hs(h)]
            kc, kp, vc, vp = kc_ref[:, _hs(kv)], kp_ref[:, _hs(kv)], vc_ref[:, _hs(kv)], vp_ref[:, _hs(kv)]
            s_cur, s_prev = _swa_logits(q, kc, kp, _slope(h), first)
            lse_h = lse_ref[:, h:h + 1]
            dl_h = dl_ref[:, FOX_H + h:FOX_H + h + 1]
            p_cur = jnp.exp(s_cur - lse_h)
            p_prev = jnp.exp(s_prev - lse_h)
            p_sink = jnp.exp(sp_ref[R_SINK:R_SINK + 1, h:h + 1] - lse_h)
            ds_cur = p_cur * (_dot(do, vc, "nt") - dl_h)
            ds_prev = p_prev * (_dot(do, vp, "nt") - dl_h)
            dq_ref[:, _hs(h)] = (_dot(ds_cur, kc) + _dot(ds_prev, kp)) * SCALE
            add(dk_cur, kv, _dot(ds_cur, q, "tn") * SCALE)
            add(dk_prev, kv, _dot(ds_prev, q, "tn") * SCALE)
            add(dv_cur, kv, _dot(p_cur, do, "tn"))
            add(dv_prev, kv, _dot(p_prev, do, "tn"))
            dsinks.append(-jnp.sum(p_sink * dl_h, axis=0, keepdims=True))
        for kv in range(SWA_KV):
            dk_ref[:, _hs(kv)] = dk_cur[kv] + ck[:, _hs(kv)]
            dv_ref[:, _hs(kv)] = dv_cur[kv] + cv[:, _hs(kv)]
            ck[:, _hs(kv)] = dk_prev[kv]
            cv[:, _hs(kv)] = dv_prev[kv]
        lane = lax.broadcasted_iota(jnp.int32, (1, HEAD), 1)
        row = jnp.zeros((1, HEAD), F32)
        for h in range(SWA_H):
            row = jnp.where(lane == h, dsinks[h], row)
        dsp_ref[R_SINK:R_SINK + 1, :] += row

    def rev(n):
        return nb - 1 - n

    qspec = pl.BlockSpec((w, SWA_W), lambda n: (rev(n), 0))
    cur = pl.BlockSpec((w, SWA_KV_W), lambda n: (rev(n), 0))
    prev = pl.BlockSpec((w, SWA_KV_W), lambda n: (jnp.maximum(rev(n) - 1, 0), 0))
    col = pl.BlockSpec((w, HEAD), lambda n: (rev(n), 0))
    small = pl.BlockSpec((16, 128), lambda n: (0, 0))
    return pl.pallas_call(
        body, grid=(nb,),
        in_specs=[qspec, prev, cur, prev, cur, small, pl.BlockSpec((w, SWA_W), lambda n: (rev(n), do_block)), col, col],
        out_specs=[qspec, cur, cur, small],
        out_shape=[jax.ShapeDtypeStruct((t, SWA_W), F32), jax.ShapeDtypeStruct((t, SWA_KV_W), F32),
                   jax.ShapeDtypeStruct((t, SWA_KV_W), F32), jax.ShapeDtypeStruct((16, 128), F32)],
        scratch_shapes=[pltpu.VMEM((w, SWA_KV_W), F32), pltpu.VMEM((w, SWA_KV_W), F32)],
        compiler_params=_params(1), name=name)(sq, sk, sk, sv, sv, sp, dmixed, lse, delta)


def _mem_pre(mkv, sp, name):
    m = mkv.shape[0]

    def body(x_ref, sp_ref, k_ref, v_ref):
        for h in range(MEM_H):
            k_ref[:, _hs(h)] = _head_norm(x_ref[:, _hs(h)], sp_ref[R_MK:R_MK + 1, :]).astype(BF16)
        v_ref[...] = x_ref[:, MEM_W:2 * MEM_W].astype(BF16)

    out = jax.ShapeDtypeStruct((m, MEM_W), BF16)
    return pl.pallas_call(body, out_shape=[out, out], name=name)(mkv, sp)


def _mem_post_bwd(mkv, sp, dmk, dmv, name):
    m = mkv.shape[0]

    def body(x_ref, sp_ref, dk_ref, dv_ref, d_ref, dsp_ref):
        dsp_ref[...] = jnp.zeros_like(dsp_ref)
        total = None
        for h in range(MEM_H):
            dx, dg = _head_norm_bwd(x_ref[:, _hs(h)], sp_ref[R_MK:R_MK + 1, :], dk_ref[:, _hs(h)])
            d_ref[:, _hs(h)] = dx.astype(BF16)
            total = dg if total is None else total + dg
        d_ref[:, MEM_W:2 * MEM_W] = dv_ref[...].astype(BF16)
        dsp_ref[R_MK:R_MK + 1, :] = total

    return pl.pallas_call(body, out_shape=[jax.ShapeDtypeStruct((m, 2 * MEM_W), BF16),
                                           jax.ShapeDtypeStruct((16, 128), F32)], name=name)(mkv, sp, dmk, dmv)


def _mem_fwd(mq, mk, mv, name):
    t = mq.shape[0]
    m = mk.shape[0]
    tq = min(t, 512)

    def body(q_ref, k_ref, v_ref, o_ref, lse_ref):
        lses = []
        for h in range(MEM_H):
            s = _dot(q_ref[:, _hs(h)], k_ref[:, _hs(h)], "nt") * SCALE
            mx = jnp.max(s, axis=-1, keepdims=True)
            p = jnp.exp(s - mx)
            l = jnp.sum(p, axis=-1, keepdims=True)
            o_ref[:, _hs(h)] = _dot(p, v_ref[:, _hs(h)]) / l
            lses.append(mx + jnp.log(l))
        lse_ref[...] = _head_column(lses)

    qspec = pl.BlockSpec((tq, MEM_W), lambda i: (i, 0))
    kspec = pl.BlockSpec((m, MEM_W), lambda i: (0, 0))
    return pl.pallas_call(
        body, grid=(t // tq,), in_specs=[qspec, kspec, kspec],
        out_specs=[qspec, pl.BlockSpec((tq, HEAD), lambda i: (i, 0))],
        out_shape=[jax.ShapeDtypeStruct((t, MEM_W), F32), jax.ShapeDtypeStruct((t, HEAD), F32)],
        compiler_params=_params(1), name=name)(mq, mk, mv)


def _mem_bwd(mq, mk, mv, dmixed, lse, delta, name):
    t = mq.shape[0]
    m = mk.shape[0]
    tq = min(t, 512)
    do_block = (FOX_W + SWA_W) // MEM_W
    assert (FOX_W + SWA_W) % MEM_W == 0

    def body(q_ref, k_ref, v_ref, do_ref, lse_ref, dl_ref, dq_ref, dk_ref, dv_ref):
        @pl.when(pl.program_id(0) == 0)
        def _():
            dk_ref[...] = jnp.zeros_like(dk_ref)
            dv_ref[...] = jnp.zeros_like(dv_ref)

        for h in range(MEM_H):
            q, k, v, do = q_ref[:, _hs(h)], k_ref[:, _hs(h)], v_ref[:, _hs(h)], do_ref[:, _hs(h)]
            s = _dot(q, k, "nt") * SCALE
            p = jnp.exp(s - lse_ref[:, h:h + 1])
            col = FOX_H + SWA_H + h
            ds = p * (_dot(do, v, "nt") - dl_ref[:, col:col + 1])
            dq_ref[:, _hs(h)] = _dot(ds, k) * SCALE
            dk_ref[:, _hs(h)] += _dot(ds, q, "tn") * SCALE
            dv_ref[:, _hs(h)] += _dot(p, do, "tn")

    qspec = pl.BlockSpec((tq, MEM_W), lambda i: (i, 0))
    kspec = pl.BlockSpec((m, MEM_W), lambda i: (0, 0))
    col = pl.BlockSpec((tq, HEAD), lambda i: (i, 0))
    return pl.pallas_call(
        body, grid=(t // tq,),
        in_specs=[qspec, kspec, kspec, pl.BlockSpec((tq, MEM_W), lambda i: (i, do_block)), col, col],
        out_specs=[qspec, kspec, kspec],
        out_shape=[jax.ShapeDtypeStruct((t, MEM_W), F32), jax.ShapeDtypeStruct((m, MEM_W), F32),
                   jax.ShapeDtypeStruct((m, MEM_W), F32)],
        compiler_params=_params(1), name=name)(mq, mk, mv, dmixed, lse, delta)


def _all_gather(xs, name):
    n = len(xs)

    def body(*refs):
        x_refs, o_refs = refs[:n], refs[n:2 * n]
        send_sems, recv_sems, local_sems = refs[2 * n:]
        x, y, c = _me()
        me, sibling = (x, y, c), (x, y, 1 - c)
        chips = [(1 - x, y), (x, 1 - y), (1 - x, 1 - y)]

        def copy(a, k, block, to, src=None):
            slot = o_refs[a].at[_lin(block)]
            return pltpu.make_async_remote_copy(
                src_ref=slot if src is None else src, dst_ref=slot, send_sem=send_sems.at[a, k],
                recv_sem=recv_sems.at[a, k], device_id=to, device_id_type=MESH)

        mine = [pltpu.make_async_copy(x_refs[a], o_refs[a].at[_lin(me)], local_sems.at[a]) for a in range(n)]
        for cp in mine:
            cp.start()
        first = []
        for a in range(n):
            first.append(copy(a, 0, me, sibling, src=x_refs[a]))
            first += [copy(a, 1 + j, me, (*chip, c), src=x_refs[a]) for j, chip in enumerate(chips)]
        for cp in first:
            cp.start()
        passed = []
        for j, chip in enumerate(chips):
            for a in range(n):
                copy(a, 1 + j, (*chip, c), me).wait_recv()
                cp = copy(a, 4 + j, (*chip, c), sibling)
                cp.start()
                passed.append(cp)
        for a in range(n):
            copy(a, 0, sibling, me).wait_recv()
            for j, chip in enumerate(chips):
                copy(a, 4 + j, (*chip, 1 - c), me).wait_recv()
        for cp in first + passed:
            cp.wait_send()
        for cp in mine:
            cp.wait()

    return pl.pallas_call(
        body, in_specs=[ANY] * n, out_specs=[ANY] * n,
        out_shape=[jax.ShapeDtypeStruct((N_DEV,) + x.shape, x.dtype) for x in xs],
        scratch_shapes=[pltpu.SemaphoreType.DMA((n, 7)), pltpu.SemaphoreType.DMA((n, 7)),
                        pltpu.SemaphoreType.DMA((n,))],
        name=name)(*xs)


def _peers():
    x, y, c = _me()
    out = []
    for k in range(1, N_DEV):
        kx, ky, kc = (k >> 2) & 1, (k >> 1) & 1, k & 1
        out.append(((1 - x) if kx else x, (1 - y) if ky else y, (1 - c) if kc else c))
    return out


def _all_reduce_small(xs, name):
    n = len(xs)

    def body(*refs):
        x_refs, o_refs = refs[:n], refs[n:2 * n]
        bufs = refs[2 * n:3 * n]
        send_sems, recv_sems = refs[3 * n:]
        me = _lin(_me())
        peers = _peers()
        for a in range(n):
            bufs[a][me] = x_refs[a][...]
        sends = []
        for a in range(n):
            for k, peer in enumerate(peers):
                sends.append(pltpu.make_async_remote_copy(
                    src_ref=bufs[a].at[me], dst_ref=bufs[a].at[me], send_sem=send_sems.at[a, k],
                    recv_sem=recv_sems.at[a, k], device_id=peer, device_id_type=MESH))
        for cp in sends:
            cp.start()
        for a in range(n):
            for k, peer in enumerate(peers):
                pltpu.make_async_remote_copy(
                    src_ref=bufs[a].at[me], dst_ref=bufs[a].at[_lin(peer)], send_sem=send_sems.at[a, k],
                    recv_sem=recv_sems.at[a, k], device_id=peer, device_id_type=MESH).wait_recv()
        for cp in sends:
            cp.wait_send()
        for a in range(n):
            total = bufs[a][0]
            for q in range(1, N_DEV):
                total = total + bufs[a][q]
            o_refs[a][...] = total

    vmem = pl.BlockSpec(memory_space=pltpu.VMEM)
    return pl.pallas_call(
        body, in_specs=[vmem] * n, out_specs=[vmem] * n,
        out_shape=[jax.ShapeDtypeStruct(x.shape, F32) for x in xs],
        scratch_shapes=[pltpu.VMEM((N_DEV,) + x.shape, F32) for x in xs]
        + [pltpu.SemaphoreType.DMA((n, 7)), pltpu.SemaphoreType.DMA((n, 7))],
        name=name)(*xs)


def _pair_add(part, got, name):
    _, rows, cols = part.shape
    tm = _rows_tile(rows, cols * 2, budget=2 << 20)
    core = jnp.reshape(lax.axis_index("c"), (1,)).astype(jnp.int32)

    def body(c_ref, p_ref, g_ref, o_ref):
        o_ref[...] = (p_ref[...].astype(F32) + g_ref[...].astype(F32)).astype(BF16)

    spec = pl.BlockSpec((None, tm, cols), lambda q, i, c: (q, i, 0))
    grid_spec = pltpu.PrefetchScalarGridSpec(
        num_scalar_prefetch=1, grid=(4, rows // tm),
        in_specs=[pl.BlockSpec((None, tm, cols), lambda q, i, c: (2 * q + c[0], i, 0)), spec], out_specs=spec)
    return pl.pallas_call(body, grid_spec=grid_spec, out_shape=jax.ShapeDtypeStruct((4, rows, cols), BF16),
                          compiler_params=_params(2), name=name)(core, part, got)


def _adam_math(w, g, m, v):
    nm = ADAM_B1 * m + (1.0 - ADAM_B1) * g
    nv = ADAM_B2 * v + (1.0 - ADAM_B2) * (g * g)
    m_hat = nm / (1.0 - ADAM_B1 ** ADAM_STEP)
    v_hat = nv / (1.0 - ADAM_B2 ** ADAM_STEP)
    return -ADAM_LR * (m_hat / (jnp.sqrt(v_hat) + ADAM_EPS) + ADAM_WD * w), nm, nv


def _sum_chips(got, name):
    _, rows, cols = got.shape
    tm = _rows_tile(rows, cols * 2 * 4, budget=2 << 20)

    def body(r_ref, o_ref):
        o_ref[...] = ((r_ref[0].astype(F32) + r_ref[1].astype(F32)) + r_ref[2].astype(F32)) + r_ref[3].astype(F32)

    return pl.pallas_call(
        body, grid=(rows // tm,), in_specs=[pl.BlockSpec((4, tm, cols), lambda i: (0, i, 0))],
        out_specs=pl.BlockSpec((tm, cols), lambda i: (i, 0)), out_shape=jax.ShapeDtypeStruct((rows, cols), F32),
        compiler_params=_params(1), name=name)(got)


def _sum_adamw(got, col_block, w, m, v, name):
    _, rows, cols = w.shape
    tm = _rows_tile(rows, cols * 4, budget=1 << 20)

    def body(r_ref, w_ref, m_ref, v_ref, g_ref, d_ref, nm_ref, nv_ref):
        g = ((r_ref[0].astype(F32) + r_ref[1].astype(F32)) + r_ref[2].astype(F32)) + r_ref[3].astype(F32)
        g_ref[...] = g
        d_ref[...], nm_ref[...], nv_ref[...] = _adam_math(w_ref[...], g, m_ref[...], v_ref[...])

    spec = pl.BlockSpec((None, tm, cols), lambda i: (0, i, 0))
    out = jax.ShapeDtypeStruct(w.shape, F32)
    return pl.pallas_call(
        body, grid=(rows // tm,), in_specs=[pl.BlockSpec((4, tm, cols), lambda i: (0, i, col_block)), spec, spec, spec],
        out_specs=[spec] * 4, out_shape=[out] * 4, compiler_params=_params(1), name=name)(got, w, m, v)


def _adamw(w, g, m, v, name):
    rows, cols = w.shape

    def body(w_ref, g_ref, m_ref, v_ref, d_ref, nm_ref, nv_ref):
        d_ref[...], nm_ref[...], nv_ref[...] = _adam_math(w_ref[...], g_ref[...], m_ref[...], v_ref[...])

    out = jax.ShapeDtypeStruct(w.shape, F32)
    if rows % 8:
        return pl.pallas_call(body, out_shape=[out] * 3, compiler_params=_params(0), name=name)(w, g, m, v)
    tm = _rows_tile(rows, cols * 4, budget=2 << 20, mult=8)
    spec = pl.BlockSpec((tm, cols), lambda i: (i, 0))
    return pl.pallas_call(body, grid=(rows // tm,), in_specs=[spec] * 4, out_specs=[spec] * 3,
                          out_shape=[out] * 3, compiler_params=_params(1), name=name)(w, g, m, v)


def _permute_in(w):
    logit0 = 3 * FOX_W
    pad = jnp.zeros(w.shape[:-1] + (HEAD - N_LOGIT,), w.dtype)
    return jnp.concatenate([w[..., :logit0], w[..., logit0 + N_LOGIT:], w[..., logit0:logit0 + N_LOGIT], pad], axis=-1)


def _unpermute_in_rows(w):
    logit0 = 3 * FOX_W
    return jnp.concatenate([w[:logit0], w[C_FL:C_FL + N_LOGIT], w[logit0:C_FL]], axis=0)


def _pad_row(v, width):
    return jnp.pad(v, ((0, 0), (0, width - v.shape[1])))


def _pack_small(fq, fk, sq, sk, mq, mk, fb, sinks):
    rows = [fq, fk, sq, sk, mq, mk, _pad_row(fb, HEAD), _pad_row(sinks, HEAD)]
    return jnp.concatenate(rows + [jnp.zeros((8, HEAD), F32)], axis=0)


def _pack_norms(a, b, c, d):
    return jnp.concatenate([a, b, c, d, jnp.zeros((4, a.shape[1]), F32)], axis=0)


def kernel(x, mem, ffn1_norm, ffn1_gate, ffn1_up, ffn1_down, mix_norm, mem_norm, w_in, forget_bias, w_mem_k, w_mem_v, fox_q_gain, fox_k_gain, swa_q_gain, swa_k_gain, swa_sinks, mem_q_gain, mem_k_gain, w_out, ffn2_norm, ffn2_gate, ffn2_up, ffn2_down, loss_target, m_ffn1_norm, m_ffn1_gate, m_ffn1_up, m_ffn1_down, m_mix_norm, m_mem_norm, m_w_in, m_forget_bias, m_w_mem_k, m_w_mem_v, m_fox_q_gain, m_fox_k_gain, m_swa_q_gain, m_swa_k_gain, m_swa_sinks, m_mem_q_gain, m_mem_k_gain, m_w_out, m_ffn2_norm, m_ffn2_gate, m_ffn2_up, m_ffn2_down, v_ffn1_norm, v_ffn1_gate, v_ffn1_up, v_ffn1_down, v_mix_norm, v_mem_norm, v_w_in, v_forget_bias, v_w_mem_k, v_w_mem_v, v_fox_q_gain, v_fox_k_gain, v_swa_q_gain, v_swa_k_gain, v_swa_sinks, v_mem_q_gain, v_mem_k_gain, v_w_out, v_ffn2_norm, v_ffn2_gate, v_ffn2_up, v_ffn2_down):
    x0 = x[0]
    mem0 = mem[0]
    target = loss_target[0]
    t, d = x0.shape
    d_shard = w_in.shape[1]
    m_len = mem0.shape[0]
    tm = min(t, 512)
    tk = min(t, 512)
    tn = IN_W // 3
    tkw, tnw = min(t, 1024), IN_W // 3

    def swap(a):
        return jnp.swapaxes(a, 1, 2)

    gate1, up1, gate2, up2 = swap(ffn1_gate), swap(ffn1_up), swap(ffn2_gate), swap(ffn2_up)

    local = {
        "g1": gate1[0], "u1": up1[0], "d1": ffn1_down[0],
        "g2": gate2[0], "u2": up2[0], "d2": ffn2_down[0],
        "in": _permute_in(w_in[0]), "out": w_out[0],
        "mkv": jnp.concatenate([w_mem_k[0], w_mem_v[0]], axis=1),
    }
    shard = {k: _cast_bf16(v, f"cast_{k}") for k, v in local.items()}
    sp = _pack_small(fox_q_gain, fox_k_gain, swa_q_gain, swa_k_gain, mem_q_gain, mem_k_gain, forget_bias, swa_sinks)
    wt = {}

    wt["g1"], wt["u1"] = _all_gather([shard["g1"], shard["u1"]], "gather_ffn1_in")
    xn1 = _rms_fwd(x0, ffn1_norm, "ffn1_norm")
    half_fs = shard["g2"].shape[0] // 2
    (a1, b1, h1), ((wt["d1"], wt["in"]),) = _ffn_up(
        xn1, wt["g1"], wt["u1"], "ffn1", rides=[_ride_gather([shard["d1"], shard["in"]], 0.87)])
    tc = min(t, CONTRACT_ROWS)
    (x1, hn), ((wt["out"], wt["mkv"]),) = _ffn_down(
        x0, h1, wt["d1"], "ffn1", rides=[_ride_gather([shard["out"], shard["mkv"]], 0.6)],
        tail=_tail_norm(mix_norm, t, d, tc))
    w_in_full = wt["in"].reshape(d, IN_W)

    proj, (half,) = _mm(
        "proj", [(hn, pl.BlockSpec((tm, d), lambda n, i, k: (i, 0)),
                  w_in_full, pl.BlockSpec((d, tn), lambda n, i, k: (0, n)))],
        "nn", (3, t // tm, 1), jax.ShapeDtypeStruct((t, IN_W), F32), pl.BlockSpec((tm, tn), lambda n, i, k: (i, n)),
        rides=[_ride_gather_chips([shard["g2"]], rows=(0, half_fs))])
    w_out_full = wt["out"].reshape(d, d)
    w_mkv_full = wt["mkv"].reshape(d, 2 * MEM_W)
    fq, fk, fv, sq, sk, sv, mq, c_col = _attn_pre(proj, sp, "attn_pre")
    c_row = jnp.transpose(c_col[:, :8])
    c_rep = jnp.broadcast_to(c_row[:FOX_H, :, None], (FOX_H, t, HEAD))

    mn = _rms_fwd(mem0, mem_norm, "mem_norm")
    mkv = _mm("mem_kv", [(mn, pl.BlockSpec((m_len, d), lambda k: (0, 0)),
                          w_mkv_full, pl.BlockSpec((d, 2 * MEM_W), lambda k: (0, 0)))],
              "nn", (1,), jax.ShapeDtypeStruct((m_len, 2 * MEM_W), F32),
              pl.BlockSpec((m_len, 2 * MEM_W), lambda k: (0, 0)))
    mk, mv = _mem_pre(mkv, sp, "mem_pre")

    (o_a, lse_a), (half, half_u2) = _fox_fwd(
        fq, fk, fv, c_rep, c_row, "fox_fwd",
        rides=[_ride_gather_chips([shard["g2"]], rows=(half_fs, half_fs), into=half),
               _ride_gather_chips([shard["u2"]], rows=(0, half_fs))])
    (o_b, lse_b), ((wt["g2"],), half_u2) = _swa_fwd(
        sq, sk, sv, sp, "swa_fwd",
        rides=[_ride_gather_sibling(half), _ride_gather_chips([shard["u2"]], rows=(half_fs, half_fs), into=half_u2)])
    o_c, lse_c = _mem_fwd(mq, mk, mv, "mem_fwd")

    def rows_spec(width):
        return pl.BlockSpec((tm, width), lambda i, k: (i, 0))

    def wout_rows(first, width):
        assert first % width == 0
        return pl.BlockSpec((width, d), lambda i, k: (first // width, 0), pipeline_mode=pl.Buffered(1))

    xspec = pl.BlockSpec((tm, d), lambda i, k: (i, 0))
    (x2, xn2), ((wt["u2"],),) = _mm(
        "mix_out",
        [(o_a, rows_spec(FOX_W), w_out_full, wout_rows(0, FOX_W)),
         (o_b, rows_spec(SWA_W), w_out_full, wout_rows(FOX_W, SWA_W)),
         (o_c, rows_spec(MEM_W), w_out_full, wout_rows(FOX_W + SWA_W, MEM_W))],
        "nn", (t // tm, 1), jax.ShapeDtypeStruct((t, d), F32), xspec, res=x1, res_spec=xspec,
        rides=[_ride_gather_sibling(half_u2)], tail=_tail_norm(ffn2_norm, t, d, tm))

    (a2, b2, h2), ((wt["d2"],),) = _ffn_up(xn2, wt["g2"], wt["u2"], "ffn2", rides=[_ride_gather([shard["d2"]], 0.75)])
    dy, dyb, sq_err = _ffn_down(x2, h2, wt["d2"], "ffn2", tail=_tail_loss(target, t, d, tc))
    loss = lax.psum(0.5 * sq_err[0, 0] / d, ("x", "y", "c"))

    got = {}
    paired = {}
    landed = {}

    def pair(k, part):
        paired[k] = _pair_add(part, got[k], f"pair_{k}")

    (dg2, du2), _ = _ffn_dact(dyb, wt["d2"], a2, b2, "ffn2")
    part_d2 = _ffn_dw(h2, dyb, 0.5, "ffn2_dwd")
    part_g2, ((got["d2"],),) = _ffn_dw(dg2, xn2, 1.0, "ffn2_dwg", rides=[_ride_scatter_sibling([part_d2])])
    pair("d2", part_d2)
    half_rows = part_d2.shape[1] // 2
    first, second = (0, half_rows), (half_rows, half_rows)
    part_u2, (half, (got["g2"],)) = _ffn_dw(
        du2, xn2, 1.0, "ffn2_dwu",
        rides=[_ride_scatter_chips([paired["d2"]], first), _ride_scatter_sibling([part_g2])])
    pair("g2", part_g2)
    dxn2, ((landed["d2"],),) = _ffn_contract(
        dg2, wt["g2"], "ffn2_dxn_g", rides=[_ride_scatter_chips([paired["d2"]], second, into=half)])
    (dx2, dx2b, dgain_ffn2), (half_g2, (got["u2"],)) = _ffn_contract(
        du2, wt["u2"], "ffn2_dxn_u", res=dxn2,
        rides=[_ride_scatter_chips([paired["g2"]], first), _ride_scatter_sibling([part_u2])],
        tail=_tail_norm_bwd(x2, ffn2_norm, dy, t, d, tc))
    pair("u2", part_u2)

    dmixed = _mm("mix_out_dx", [(dx2b, xspec, w_out_full, pl.BlockSpec((d, d), lambda i, k: (0, 0)))],
                 "nt", (t // tm, 1), jax.ShapeDtypeStruct((t, d), F32), xspec)

    def k_rows(width):
        return pl.BlockSpec((tk, width), lambda j, k: (k, 0))

    part_out = [
        _mm(f"mix_out_dw{i}", [(o, k_rows(width), dx2b, k_rows(d))], "tn", (1, t // tk),
            jax.ShapeDtypeStruct((width, d), BF16), pl.BlockSpec((width, d), lambda j, k: (0, 0)))
        for i, (o, width) in enumerate(((o_a, FOX_W), (o_b, SWA_W), (o_c, MEM_W)))
    ]
    part_out = jnp.concatenate(part_out, axis=0).reshape(N_DEV, d_shard, d)

    delta, delta_rep = _delta(dmixed, o_a, o_b, o_c, "attn_delta")
    (dfq, dfk, dfv, dc_col, dc_row), ((landed["g2"],), (landed["u2"],)) = _fox_bwd(
        fq, fk, fv, c_rep, c_row, dmixed, lse_a, delta_rep, "fox_bwd",
        rides=[_ride_scatter_chips([paired["g2"]], second, into=half_g2), _ride_scatter_chips([paired["u2"]])])
    dsq, dsk, dsv, dsp_sink = _swa_bwd(sq, sk, sv, sp, dmixed, lse_b, delta, "swa_bwd")
    dmq, dmk, dmv = _mem_bwd(mq, mk, mv, dmixed, lse_c, delta, "mem_bwd")

    dmkv, dsp_mem = _mem_post_bwd(mkv, sp, dmk, dmv, "mem_post_bwd")
    part_mkv = _mm("mem_kv_dw", [(mn, pl.BlockSpec((m_len, d), lambda k: (0, 0)),
                                  dmkv, pl.BlockSpec((m_len, 2 * MEM_W), lambda k: (0, 0)))],
                   "tn", (1,), jax.ShapeDtypeStruct((d, 2 * MEM_W), BF16),
                   pl.BlockSpec((d, 2 * MEM_W), lambda k: (0, 0))).reshape(N_DEV, d_shard, 2 * MEM_W)
    dmn = _mm("mem_kv_dx", [(dmkv, pl.BlockSpec((m_len, 2 * MEM_W), lambda k: (0, 0)),
                             w_mkv_full, pl.BlockSpec((d, 2 * MEM_W), lambda k: (0, 0)))],
              "nt", (1,), jax.ShapeDtypeStruct((m_len, d), F32), pl.BlockSpec((m_len, d), lambda k: (0, 0)))
    _, _, dgain_mem = _rms_bwd(mem0, mem_norm, dmn, None, "mem_norm_bwd")

    dc_row_t = _pad_row(jnp.transpose(dc_row), HEAD)
    dproj, dsp_attn = _attn_post_bwd(proj, sp, dfq, dfk, dfv, dsq, dsk, dsv, dmq, dc_col, dc_row_t, "attn_post_bwd")
    (dx1, dx1b, dgain_mix), ((got["out"], got["mkv"]),) = _mm(
        "proj_dx", [(dproj, pl.BlockSpec((tc, IN_W), lambda i, k: (i, 0)),
                     w_in_full, pl.BlockSpec((d, IN_W), lambda i, k: (0, 0), pipeline_mode=pl.Buffered(1)))],
        "nt", (t // tc, 1), jax.ShapeDtypeStruct((t, d), F32), pl.BlockSpec((tc, d), lambda i, k: (i, 0)),
        rides=[_ride_scatter_sibling([part_out, part_mkv])], tail=_tail_norm_bwd(x1, mix_norm, dx2, t, d, tc))
    pair("out", part_out)
    pair("mkv", part_mkv)
    part_in, ((landed["out"], landed["mkv"]),) = _mm(
        "proj_dw", [(hn, pl.BlockSpec((tkw, d), lambda n, k: (k, 0)),
                     dproj, pl.BlockSpec((tkw, tnw), lambda n, k: (k, n)))],
        "tn", (IN_W // tnw, t // tkw), jax.ShapeDtypeStruct((d, IN_W), BF16), pl.BlockSpec((d, tnw), lambda n, k: (0, n)),
        rides=[_ride_scatter_chips([paired["out"], paired["mkv"]])])
    part_in = part_in.reshape(N_DEV, d_shard, IN_W)

    (dg1, du1), ((got["in"],),) = _ffn_dact(dx1b, wt["d1"], a1, b1, "ffn1", rides=[_ride_scatter_sibling([part_in])])
    pair("in", part_in)
    part_d1, ((landed["in"],),) = _ffn_dw(h1, dx1b, 0.5, "ffn1_dwd", rides=[_ride_scatter_chips([paired["in"]])])
    part_g1, ((got["d1"],),) = _ffn_dw(dg1, xn1, 1.0, "ffn1_dwg", rides=[_ride_scatter_sibling([part_d1])])
    pair("d1", part_d1)
    part_u1, (half_d1, (got["g1"],)) = _ffn_dw(
        du1, xn1, 1.0, "ffn1_dwu",
        rides=[_ride_scatter_chips([paired["d1"]], first), _ride_scatter_sibling([part_g1])])
    pair("g1", part_g1)
    dxn1, ((landed["d1"],), half_g1, (got["u1"],)) = _ffn_contract(
        dg1, wt["g1"], "ffn1_dxn_g",
        rides=[_ride_scatter_chips([paired["d1"]], second, into=half_d1), _ride_scatter_chips([paired["g1"]], first),
               _ride_scatter_sibling([part_u1])])
    pair("u1", part_u1)
    (grad_x, _, dgain_ffn1), ((landed["g1"],), (landed["u1"],)) = _ffn_contract(
        du1, wt["u1"], "ffn1_dxn_u", res=dxn1,
        rides=[_ride_scatter_chips([paired["g1"]], second, into=half_g1), _ride_scatter_chips([paired["u1"]])],
        tail=_tail_norm_bwd(x0, ffn1_norm, dx1, t, d, tc))

    norms_sum, small_sum = _all_reduce_small(
        [_pack_norms(dgain_ffn1, dgain_mix, dgain_mem, dgain_ffn2), dsp_attn + dsp_sink + dsp_mem], "reduce_small")

    result = {
        "ffn1_gate": map(swap, _sum_adamw(landed["g1"], 0, gate1, swap(m_ffn1_gate), swap(v_ffn1_gate), "adamw_ffn1_gate")),
        "ffn1_up": map(swap, _sum_adamw(landed["u1"], 0, up1, swap(m_ffn1_up), swap(v_ffn1_up), "adamw_ffn1_up")),
        "ffn1_down": _sum_adamw(landed["d1"], 0, ffn1_down, m_ffn1_down, v_ffn1_down, "adamw_ffn1_down"),
        "w_mem_k": _sum_adamw(landed["mkv"], 0, w_mem_k, m_w_mem_k, v_w_mem_k, "adamw_w_mem_k"),
        "w_mem_v": _sum_adamw(landed["mkv"], 1, w_mem_v, m_w_mem_v, v_w_mem_v, "adamw_w_mem_v"),
        "w_out": _sum_adamw(landed["out"], 0, w_out, m_w_out, v_w_out, "adamw_w_out"),
        "ffn2_gate": map(swap, _sum_adamw(landed["g2"], 0, gate2, swap(m_ffn2_gate), swap(v_ffn2_gate), "adamw_ffn2_gate")),
        "ffn2_up": map(swap, _sum_adamw(landed["u2"], 0, up2, swap(m_ffn2_up), swap(v_ffn2_up), "adamw_ffn2_up")),
        "ffn2_down": _sum_adamw(landed["d2"], 0, ffn2_down, m_ffn2_down, v_ffn2_down, "adamw_ffn2_down"),
    }
    grad_in = _unpermute_in_rows(jnp.transpose(_sum_chips(landed["in"], "sum_w_in")))
    result["w_in"] = tuple(swap(o[None]) for o in (grad_in,) + tuple(
        _adamw(swap(w_in)[0], grad_in, swap(m_w_in)[0], swap(v_w_in)[0], "adamw_w_in")))

    norm_names = ["ffn1_norm", "mix_norm", "mem_norm", "ffn2_norm"]
    norm_w = _pack_norms(ffn1_norm, mix_norm, mem_norm, ffn2_norm)
    norm_m = _pack_norms(m_ffn1_norm, m_mix_norm, m_mem_norm, m_ffn2_norm)
    norm_v = _pack_norms(v_ffn1_norm, v_mix_norm, v_mem_norm, v_ffn2_norm)
    outs = (norms_sum,) + tuple(_adamw(norm_w, norms_sum, norm_m, norm_v, "adamw_norms"))
    for i, k in enumerate(norm_names):
        result[k] = tuple(o[i:i + 1] for o in outs)

    small_names = ["fox_q_gain", "fox_k_gain", "swa_q_gain", "swa_k_gain", "mem_q_gain", "mem_k_gain",
                   "forget_bias", "swa_sinks"]
    small_m = _pack_small(m_fox_q_gain, m_fox_k_gain, m_swa_q_gain, m_swa_k_gain, m_mem_q_gain, m_mem_k_gain,
                          m_forget_bias, m_swa_sinks)
    small_v = _pack_small(v_fox_q_gain, v_fox_k_gain, v_swa_q_gain, v_swa_k_gain, v_mem_q_gain, v_mem_k_gain,
                          v_forget_bias, v_swa_sinks)
    outs = (small_sum,) + tuple(_adamw(sp, small_sum, small_m, small_v, "adamw_small"))
    for i, k in enumerate(small_names):
        width = N_LOGIT if k in ("forget_bias", "swa_sinks") else HEAD
        result[k] = tuple(o[i:i + 1, :width] for o in outs)

    order = ["ffn1_norm", "ffn1_gate", "ffn1_up", "ffn1_down", "mix_norm", "mem_norm", "w_in", "forget_bias",
             "w_mem_k", "w_mem_v", "fox_q_gain", "fox_k_gain", "swa_q_gain", "swa_k_gain", "swa_sinks",
             "mem_q_gain", "mem_k_gain", "w_out", "ffn2_norm", "ffn2_gate", "ffn2_up", "ffn2_down"]
    result = {k: tuple(v) for k, v in result.items()}
    flat = [loss, grad_x[None]]
    for kind in range(4):
        flat += [result[k][kind] for k in order]
    return tuple(flat)
```

```python
import functools

import jax
import jax.numpy as jnp
from jax import lax
from jax.experimental import pallas as pl
from jax.experimental.pallas import tpu as pltpu

F32 = jnp.float32
BF16 = jnp.bfloat16
MESH = pl.DeviceIdType.MESH
ANY = pl.BlockSpec(memory_space=pl.ANY)

N_DEV = 8
EPS = 1e-6
NEG_INF = -1e30
HEAD = 128
FOX_H, SWA_H, SWA_KV, MEM_H = 6, 6, 2, 4
FOX_W, SWA_W, SWA_KV_W, MEM_W = FOX_H * HEAD, SWA_H * HEAD, SWA_KV * HEAD, MEM_H * HEAD
SCALE = HEAD ** -0.5
SWA_BLOCK = 128
C_FQ, C_FK, C_FV = 0, FOX_W, 2 * FOX_W
C_SQ = 3 * FOX_W
C_SK = C_SQ + SWA_W
C_SV = C_SK + SWA_KV_W
C_MQ = C_SV + SWA_KV_W
C_FL = C_MQ + MEM_W
IN_W = C_FL + HEAD
N_LOGIT = FOX_H
R_FQ, R_FK, R_SQ, R_SK, R_MQ, R_MK, R_FB, R_SINK = range(8)
ADAM_LR, ADAM_B1, ADAM_B2, ADAM_EPS, ADAM_WD, ADAM_STEP = 0.001, 0.9, 0.999, 1e-08, 0.01, 10
VMEM_BYTES = 56 * 1024 * 1024

DN = {
    "nn": (((1,), (0,)), ((), ())),
    "nt": (((1,), (1,)), ((), ())),
    "tn": (((0,), (0,)), ((), ())),
}


def _params(n_axes):
    return pltpu.CompilerParams(dimension_semantics=("arbitrary",) * n_axes, vmem_limit_bytes=VMEM_BYTES)


def _dot(a, b, dims="nn"):
    return lax.dot_general(a.astype(BF16), b.astype(BF16), DN[dims], preferred_element_type=F32)


def _sigmoid(x):
    return 0.5 * jnp.tanh(0.5 * x) + 0.5


def _me():
    return lax.axis_index("x"), lax.axis_index("y"), lax.axis_index("c")


def _lin(p):
    return 4 * p[0] + 2 * p[1] + p[2]


def _rows_tile(rows, row_bytes, budget=4 << 20, mult=16):
    best = None
    for k in range(1, rows + 1):
        if rows % k == 0 and (rows // k) % mult == 0 and (rows // k) * row_bytes <= budget:
            best = rows // k
            break
    assert best is not None, (rows, row_bytes)
    return best


class _Ride:
    def __init__(self, inputs, out_shapes, aliases, n_remote, n_local, start, wait):
        self.inputs, self.out_shapes, self.aliases = list(inputs), list(out_shapes), dict(aliases)
        self.n_remote, self.n_local, self.start, self.wait = n_remote, n_local, start, wait


def _remote(src, dst, send, recv, k, to):
    return pltpu.make_async_remote_copy(src_ref=src, dst_ref=dst, send_sem=send.at[k], recv_sem=recv.at[k],
                                        device_id=to, device_id_type=MESH)


def _other_chips(x, y):
    return [(1 - x, y), (x, 1 - y), (1 - x, 1 - y)]


ALL_CHIPS = [(0, 0), (0, 1), (1, 0), (1, 1)]


def _rows_of(ref, rows, slot=None):
    if slot is None:
        return ref if rows is None else ref.at[pl.ds(rows[0], rows[1])]
    return ref.at[slot] if rows is None else ref.at[slot, pl.ds(rows[0], rows[1])]


def _ride_gather_chips(xs, rows=None, into=None):
    n = len(xs)

    def copies(ins, outs, send, recv):
        x, y, c = _me()
        out = []
        for a in range(n):
            for j, chip in enumerate(_other_chips(x, y)):
                peer = (*chip, c)
                src = _rows_of(ins[a], rows)
                out.append((_remote(src, _rows_of(outs[a], rows, _lin((x, y, c))), send, recv, 3 * a + j, peer),
                            _remote(src, _rows_of(outs[a], rows, _lin(peer)), send, recv, 3 * a + j, peer)))
        return out

    def mine(ins, outs, local):
        me = _lin(_me())
        return [pltpu.make_async_copy(_rows_of(ins[a], rows), _rows_of(outs[a], rows, me), local.at[a])
                for a in range(n)]

    def start(ins, outs, send, recv, local):
        for cp in mine(ins, outs, local):
            cp.start()
        for sent, _ in copies(ins, outs, send, recv):
            sent.start()

    def wait(ins, outs, send, recv, local):
        for sent, landed in copies(ins, outs, send, recv):
            landed.wait_recv()
            sent.wait_send()
        for cp in mine(ins, outs, local):
            cp.wait()

    shapes = [jax.ShapeDtypeStruct((N_DEV,) + x.shape, x.dtype) for x in xs]
    if into is None:
        return _Ride(xs, shapes, {}, 3 * n, n, start, wait)
    return _Ride(list(xs) + list(into), shapes, {n + a: a for a in range(n)}, 3 * n, n, start, wait)


def _ride_gather_sibling(bufs):
    n = len(bufs)

    def copies(outs, send, recv):
        x, y, c = _me()
        out = []
        for a in range(n):
            for q, (px, py) in enumerate(ALL_CHIPS):
                there = outs[a].at[4 * px + 2 * py + c]
                here = outs[a].at[4 * px + 2 * py + 1 - c]
                out.append((_remote(there, there, send, recv, 4 * a + q, (x, y, 1 - c)),
                            _remote(here, here, send, recv, 4 * a + q, (x, y, 1 - c))))
        return out

    def start(ins, outs, send, recv, local):
        for sent, _ in copies(outs, send, recv):
            sent.start()

    def wait(ins, outs, send, recv, local):
        for sent, landed in copies(outs, send, recv):
            landed.wait_recv()
            sent.wait_send()

    shapes = [jax.ShapeDtypeStruct(b.shape, b.dtype) for b in bufs]
    return _Ride(bufs, shapes, {a: a for a in range(n)}, 4 * n, 0, start, wait)


def _ride_gather(xs, mid_frac, rows=None, into=None):
    n = len(xs)
    chips = _ride_gather_chips(xs, rows, into)

    def sibling_copies(outs, send, recv):
        x, y, c = _me()
        out = []
        for a in range(n):
            for q, (px, py) in enumerate(ALL_CHIPS):
                there = _rows_of(outs[a], rows, 4 * px + 2 * py + c)
                here = _rows_of(outs[a], rows, 4 * px + 2 * py + 1 - c)
                k = 3 * n + 4 * a + q
                out.append((_remote(there, there, send, recv, k, (x, y, 1 - c)),
                            _remote(here, here, send, recv, k, (x, y, 1 - c))))
        return out

    def mid(ins, outs, send, recv, local):
        chips.wait(ins, outs, send, recv, local)
        for sent, _ in sibling_copies(outs, send, recv):
            sent.start()

    def wait(ins, outs, send, recv, local):
        for sent, landed in sibling_copies(outs, send, recv):
            landed.wait_recv()
            sent.wait_send()

    ride = _Ride(chips.inputs, chips.out_shapes, chips.aliases, 7 * n, n, chips.start, wait)
    ride.mid, ride.mid_frac = mid, mid_frac
    return ride


def _ride_scatter_sibling(parts):
    n = len(parts)

    def copies(ins, outs, send, recv):
        x, y, c = _me()
        out = []
        for a in range(n):
            for q, (px, py) in enumerate(ALL_CHIPS):
                cp = _remote(ins[a].at[4 * px + 2 * py + 1 - c], outs[a].at[q], send, recv, 4 * a + q, (x, y, 1 - c))
                out.append(cp)
        return out

    def start(ins, outs, send, recv, local):
        for cp in copies(ins, outs, send, recv):
            cp.start()

    def wait(ins, outs, send, recv, local):
        for cp in copies(ins, outs, send, recv):
            cp.wait_recv()
            cp.wait_send()

    shapes = [jax.ShapeDtypeStruct((4,) + p.shape[1:], p.dtype) for p in parts]
    return _Ride(parts, shapes, {}, 4 * n, 0, start, wait)


def _ride_scatter_chips(pairs, rows=None, into=None):
    n = len(pairs)

    def part(ref, slot):
        return _rows_of(ref, rows, slot)

    def copies(ins, outs, send, recv):
        x, y, c = _me()
        out = []
        for a in range(n):
            for j, (px, py) in enumerate(_other_chips(x, y)):
                peer = (px, py, c)
                src = part(ins[a], 2 * px + py)
                out.append((_remote(src, part(outs[a], 2 * x + y), send, recv, 3 * a + j, peer),
                            _remote(src, part(outs[a], 2 * px + py), send, recv, 3 * a + j, peer)))
        return out

    def mine(ins, outs, local):
        x, y, _ = _me()
        return [pltpu.make_async_copy(part(ins[a], 2 * x + y), part(outs[a], 2 * x + y), local.at[a])
                for a in range(n)]

    def start(ins, outs, send, recv, local):
        for cp in mine(ins, outs, local):
            cp.start()
        for sent, _ in copies(ins, outs, send, recv):
            sent.start()

    def wait(ins, outs, send, recv, local):
        for sent, landed in copies(ins, outs, send, recv):
            landed.wait_recv()
            sent.wait_send()
        for cp in mine(ins, outs, local):
            cp.wait()

    shapes = [jax.ShapeDtypeStruct(p.shape, p.dtype) for p in pairs]
    if into is None:
        return _Ride(pairs, shapes, {}, 3 * n, n, start, wait)
    return _Ride(list(pairs) + list(into), shapes, {n + a: a for a in range(n)}, 3 * n, n, start, wait)


def _call(name, body, grid, in_specs, out_specs, out_shape, operands, scratch=(), rides=()):
    n_in, n_out, n_scr = len(operands), len(out_shape), len(scratch)
    ride_in, ride_out, ride_scr, aliases, spans = [], [], [], {}, []
    for r in rides:
        for i, o in r.aliases.items():
            aliases[n_in + len(ride_in) + i] = n_out + len(ride_out) + o
        spans.append((len(ride_in), len(r.inputs), len(ride_out), len(r.out_shapes)))
        ride_in += r.inputs
        ride_out += r.out_shapes
        ride_scr += [pltpu.SemaphoreType.DMA((r.n_remote,)), pltpu.SemaphoreType.DMA((r.n_remote,)),
                     pltpu.SemaphoreType.DMA((max(r.n_local, 1),))]

    def wrapped(*refs):
        c_in, r_in = refs[:n_in], refs[n_in:n_in + len(ride_in)]
        p = n_in + len(ride_in)
        c_out, r_out = refs[p:p + n_out], refs[p + n_out:p + n_out + len(ride_out)]
        p += n_out + len(ride_out)
        c_scr, r_scr = refs[p:p + n_scr], refs[p + n_scr:]

        n_steps = functools.reduce(lambda a, b: a * b, grid, 1)
        step = functools.reduce(lambda acc, ax: acc * grid[ax] + pl.program_id(ax), range(len(grid)), 0)

        def each(method, at):
            for k, (r, (i0, ni, o0, no)) in enumerate(zip(rides, spans)):
                fn = getattr(r, method, None)
                if fn is None:
                    continue
                run = functools.partial(fn, r_in[i0:i0 + ni], r_out[o0:o0 + no], *r_scr[3 * k:3 * k + 3])
                if grid:
                    pl.when(step == at(r))(run)
                else:
                    run()

        each("start", lambda r: 0)
        each("mid", lambda r: int(r.mid_frac * (n_steps - 1)))
        body(*c_in, *c_out, *c_scr)
        each("wait", lambda r: n_steps - 1)

    outs = pl.pallas_call(
        wrapped, grid=grid, in_specs=list(in_specs) + [ANY] * len(ride_in),
        out_specs=list(out_specs) + [ANY] * len(ride_out), out_shape=list(out_shape) + ride_out,
        scratch_shapes=list(scratch) + ride_scr, input_output_aliases=aliases,
        compiler_params=_params(len(grid)), name=name)(*operands, *ride_in)
    outs = list(outs)
    ride_results = [outs[n_out + o0:n_out + o0 + no] for (_, _, o0, no) in spans]
    return outs[:n_out], ride_results


def _only_copies(name, rides):
    return _call(name, lambda: None, (), [], [], [], [], rides=rides)[1]


class _Tail:
    def __init__(self, extra, out_shapes, out_specs, fn):
        self.extra, self.out_shapes, self.out_specs, self.fn = list(extra), list(out_shapes), list(out_specs), fn


def _mm(name, pairs, dims, grid, out_shape, out_spec, res=None, res_spec=None, alpha=1.0, rides=(), tail=None):
    n = len(pairs)
    nk = grid[-1]
    kax = len(grid) - 1
    acc_shape = tuple(d for d in out_spec.block_shape if d is not None)
    n_extra = len(tail.extra) if tail else 0
    n_outs = len(tail.out_shapes) if tail else 1

    def body(*refs):
        pos = 2 * n
        r_ref = None
        if res is not None:
            r_ref = refs[pos]
            pos += 1
        x_refs = refs[pos:pos + n_extra]
        o_refs = refs[pos + n_extra:pos + n_extra + n_outs]
        pos += n_extra + n_outs
        part = None
        for p in range(n):
            d = _dot(refs[2 * p][...], refs[2 * p + 1][...], dims)
            part = d if part is None else part + d

        def finish(acc):
            if alpha != 1.0:
                acc = acc * alpha
            if r_ref is not None:
                acc = r_ref[...] + acc
            if tail:
                tail.fn(acc, x_refs, o_refs)
            else:
                o_refs[0][...] = acc.astype(o_refs[0].dtype)

        if nk == 1:
            finish(part)
        else:
            acc_ref = refs[pos]
            k = pl.program_id(kax)

            @pl.when(k == 0)
            def _():
                acc_ref[...] = part

            @pl.when(k > 0)
            def _():
                acc_ref[...] += part

            @pl.when(k == nk - 1)
            def _():
                finish(acc_ref[...])

    operands, in_specs = [], []
    for a, a_spec, b, b_spec in pairs:
        operands += [a, b]
        in_specs += [a_spec, b_spec]
    if res is not None:
        operands.append(res)
        in_specs.append(res_spec)
    for a, a_spec in (tail.extra if tail else []):
        operands.append(a)
        in_specs.append(a_spec)
    outs, ride_results = _call(name, body, grid, in_specs, tail.out_specs if tail else [out_spec],
                               tail.out_shapes if tail else [out_shape], operands,
                               scratch=[pltpu.VMEM(acc_shape, F32)] if nk > 1 else [], rides=rides)
    outs = outs if tail else outs[0]
    return (outs, ride_results) if rides else outs


def _accumulate(ref, part):
    @pl.when(pl.program_id(0) == 0)
    def _():
        ref[...] = part

    @pl.when(pl.program_id(0) > 0)
    def _():
        ref[...] += part


def _tail_norm(gain, t, d, tm):
    def fn(v, x_refs, o_refs):
        o_refs[0][...] = v
        r = lax.rsqrt(jnp.mean(v * v, axis=-1, keepdims=True) + EPS)
        o_refs[1][...] = (v * r * x_refs[0][...]).astype(BF16)

    rows = pl.BlockSpec((tm, d), lambda i, k: (i, 0))
    return _Tail([(gain, pl.BlockSpec((1, d), lambda i, k: (0, 0)))],
                 [jax.ShapeDtypeStruct((t, d), F32), jax.ShapeDtypeStruct((t, d), BF16)], [rows, rows], fn)


def _tail_loss(target, t, d, tm):
    def fn(v, x_refs, o_refs):
        err = v - x_refs[0][...]
        dy = err * (1.0 / d)
        o_refs[0][...] = dy
        o_refs[1][...] = dy.astype(BF16)
        _accumulate(o_refs[2], jnp.zeros((8, 128), F32) + jnp.sum(err * err))

    rows = pl.BlockSpec((tm, d), lambda i, k: (i, 0))
    return _Tail([(target, rows)],
                 [jax.ShapeDtypeStruct((t, d), F32), jax.ShapeDtypeStruct((t, d), BF16),
                  jax.ShapeDtypeStruct((8, 128), F32)],
                 [rows, rows, pl.BlockSpec((8, 128), lambda i, k: (0, 0))], fn)


def _tail_norm_bwd(x, gain, dres, t, d, tm):
    def fn(dy, x_refs, o_refs):
        xv = x_refs[0][...]
        r = lax.rsqrt(jnp.mean(xv * xv, axis=-1, keepdims=True) + EPS)
        xh = xv * r
        dxh = dy * x_refs[1][...]
        dx = r * (dxh - xh * jnp.mean(dxh * xh, axis=-1, keepdims=True)) + x_refs[2][...]
        o_refs[0][...] = dx
        o_refs[1][...] = dx.astype(BF16)
        _accumulate(o_refs[2], jnp.sum(dy * xh, axis=0, keepdims=True))

    rows = pl.BlockSpec((tm, d), lambda i, k: (i, 0))
    vec = pl.BlockSpec((1, d), lambda i, k: (0, 0))
    return _Tail([(x, rows), (gain, vec), (dres, rows)],
                 [jax.ShapeDtypeStruct((t, d), F32), jax.ShapeDtypeStruct((t, d), BF16),
                  jax.ShapeDtypeStruct((1, d), F32)], [rows, rows, vec], fn)


def _cast_bf16(x, name):
    rows, cols = x.shape
    tm = _rows_tile(rows, cols * 4)

    def body(x_ref, o_ref):
        o_ref[...] = x_ref[...].astype(BF16)

    spec = pl.BlockSpec((tm, cols), lambda i: (i, 0))
    return pl.pallas_call(body, grid=(rows // tm,), in_specs=[spec], out_specs=spec,
                          out_shape=jax.ShapeDtypeStruct(x.shape, BF16), compiler_params=_params(1), name=name)(x)


def _rms_fwd(x, gain, name):
    rows, d = x.shape
    tm = min(rows, 512)

    def body(x_ref, g_ref, o_ref):
        xv = x_ref[...]
        r = lax.rsqrt(jnp.mean(xv * xv, axis=-1, keepdims=True) + EPS)
        o_ref[...] = (xv * r * g_ref[...]).astype(BF16)

    spec = pl.BlockSpec((tm, d), lambda i: (i, 0))
    return pl.pallas_call(body, grid=(rows // tm,), in_specs=[spec, pl.BlockSpec((1, d), lambda i: (0, 0))],
                          out_specs=spec, out_shape=jax.ShapeDtypeStruct(x.shape, BF16),
                          compiler_params=_params(1), name=name)(x, gain)


def _rms_bwd(x, gain, dxn, dres, name, rides=()):
    rows, d = x.shape
    tm = min(rows, 256)
    with_res = dres is not None

    def body(*refs):
        if with_res:
            x_ref, g_ref, dy_ref, r_ref, dx_ref, dxb_ref, dg_ref = refs
        else:
            x_ref, g_ref, dy_ref, dx_ref, dxb_ref, dg_ref = refs
        xv = x_ref[...]
        r = lax.rsqrt(jnp.mean(xv * xv, axis=-1, keepdims=True) + EPS)
        xh = xv * r
        dy = dy_ref[...]
        dxh = dy * g_ref[...]
        dx = r * (dxh - xh * jnp.mean(dxh * xh, axis=-1, keepdims=True))
        if with_res:
            dx = dx + r_ref[...]
        dx_ref[...] = dx
        dxb_ref[...] = dx.astype(BF16)
        part = jnp.sum(dy * xh, axis=0, keepdims=True)

        @pl.when(pl.program_id(0) == 0)
        def _():
            dg_ref[...] = part

        @pl.when(pl.program_id(0) > 0)
        def _():
            dg_ref[...] += part

    spec = pl.BlockSpec((tm, d), lambda i: (i, 0))
    vec = pl.BlockSpec((1, d), lambda i: (0, 0))
    ops = [x, gain, dxn] + ([dres] if with_res else [])
    outs, ride_results = _call(
        name, body, (rows // tm,), [spec, vec, spec] + ([spec] if with_res else []), [spec, spec, vec],
        [jax.ShapeDtypeStruct(x.shape, F32), jax.ShapeDtypeStruct(x.shape, BF16), jax.ShapeDtypeStruct((1, d), F32)],
        ops, rides=rides)
    return (outs, ride_results) if rides else outs


ROW_CHUNK = 256


def _ffn_up(xn, wg, wu, tag, rides=()):
    t, d = xn.shape
    nd, fs, _ = wg.shape
    tm = min(t, 512)
    rc = min(tm, ROW_CHUNK)

    def body(x_ref, wg_ref, wu_ref, a_ref, b_ref, h_ref):
        for r in range(0, tm, rc):
            xv = x_ref[r:r + rc, :]
            g = _dot(xv, wg_ref[...], "nt")
            u = _dot(xv, wu_ref[...], "nt")
            sig = _sigmoid(g)
            silu = g * sig
            a_ref[r:r + rc, :] = (0.5 * u * (sig + silu * (1.0 - sig))).astype(BF16)
            b_ref[r:r + rc, :] = (0.5 * silu).astype(BF16)
            h_ref[r:r + rc, :] = (silu * u).astype(BF16)

    wspec = pl.BlockSpec((None, fs, d), lambda j, i: (j, 0, 0))
    hspec = pl.BlockSpec((None, tm, fs), lambda j, i: (j, i, 0))
    hid = jax.ShapeDtypeStruct((nd, t, fs), BF16)
    return _call(f"{tag}_up", body, (nd, t // tm), [pl.BlockSpec((tm, d), lambda j, i: (i, 0)), wspec, wspec],
                 [hspec] * 3, [hid] * 3, [xn, wg, wu], rides=rides)


CONTRACT_ROWS = 256


def _ffn_contract(hid, w, name, res=None, alpha=1.0, rides=(), tail=None):
    nd, t, fs = hid.shape
    d = w.shape[2]
    tm = min(t, CONTRACT_ROWS)
    xspec = pl.BlockSpec((tm, d), lambda i, k: (i, 0))
    pairs = [(hid, pl.BlockSpec((None, tm, fs), lambda i, k, s=s: (s, i, 0)),
              w, pl.BlockSpec((None, fs, d), lambda i, k, s=s: (s, 0, 0), pipeline_mode=pl.Buffered(1)))
             for s in range(nd)]
    return _mm(name, pairs, "nn", (t // tm, 1), jax.ShapeDtypeStruct((t, d), F32), xspec,
               res=res, res_spec=xspec if res is not None else None, alpha=alpha, rides=rides, tail=tail)


def _ffn_down(x, h, wd, tag, rides=(), tail=None):
    return _ffn_contract(h, wd, f"{tag}_down", res=x, alpha=0.5, rides=rides, tail=tail)


def _ffn_dact(dyb, wd, a, b, tag, rides=()):
    nd, t, fs = a.shape
    d = dyb.shape[1]
    tm = min(t, 512)
    rc = min(tm, ROW_CHUNK)

    def body(dy_ref, wd_ref, a_ref, b_ref, dg_ref, du_ref):
        for r in range(0, tm, rc):
            dh = _dot(dy_ref[r:r + rc, :], wd_ref[...], "nt")
            dg_ref[r:r + rc, :] = (dh * a_ref[r:r + rc, :].astype(F32)).astype(BF16)
            du_ref[r:r + rc, :] = (dh * b_ref[r:r + rc, :].astype(F32)).astype(BF16)

    hspec = pl.BlockSpec((None, tm, fs), lambda j, i: (j, i, 0))
    hid = jax.ShapeDtypeStruct((nd, t, fs), BF16)
    return _call(f"{tag}_dact", body, (nd, t // tm),
                 [pl.BlockSpec((tm, d), lambda j, i: (i, 0)), pl.BlockSpec((None, fs, d), lambda j, i: (j, 0, 0)),
                  hspec, hspec], [hspec] * 2, [hid] * 2, [dyb, wd, a, b], rides=rides)


def _ffn_dw(hid, act, alpha, name, rides=()):
    nd, t, fs = hid.shape
    d = act.shape[1]
    tk = t
    return _mm(name, [(hid, pl.BlockSpec((None, tk, fs), lambda j, k: (j, k, 0)),
                       act, pl.BlockSpec((tk, d), lambda j, k: (k, 0), pipeline_mode=pl.Buffered(1)))],
               "tn", (nd, t // tk),
               jax.ShapeDtypeStruct((nd, fs, d), BF16), pl.BlockSpec((None, fs, d), lambda j, k: (j, 0, 0)),
               alpha=alpha, rides=rides)


def _head_norm(x, gain):
    r = lax.rsqrt(jnp.mean(x * x, axis=-1, keepdims=True) + EPS)
    return x * r * gain


def _head_norm_bwd(x, gain, dy):
    r = lax.rsqrt(jnp.mean(x * x, axis=-1, keepdims=True) + EPS)
    xh = x * r
    dxh = dy * gain
    dx = r * (dxh - xh * jnp.mean(dxh * xh, axis=-1, keepdims=True))
    return dx, jnp.sum(dy * xh, axis=0, keepdims=True)


def _hs(h, base=0):
    return slice(base + h * HEAD, base + (h + 1) * HEAD)


def _tri(n, lower):
    r = lax.broadcasted_iota(jnp.int32, (n, n), 0)
    c = lax.broadcasted_iota(jnp.int32, (n, n), 1)
    return ((r >= c) if lower else (r <= c)).astype(F32)


def _attn_pre(proj, sp, name):
    t = proj.shape[0]
    tm = min(t, 256)

    def body(p_ref, sp_ref, fq, fk, fv, sq, sk, sv, mq, cc, carry):
        @pl.when(pl.program_id(0) == 0)
        def _():
            carry[...] = jnp.zeros_like(carry)

        for h in range(FOX_H):
            fq[:, _hs(h)] = _head_norm(p_ref[:, _hs(h, C_FQ)], sp_ref[R_FQ:R_FQ + 1, :]).astype(BF16)
            fk[:, _hs(h)] = _head_norm(p_ref[:, _hs(h, C_FK)], sp_ref[R_FK:R_FK + 1, :]).astype(BF16)
        fv[...] = p_ref[:, C_FV:C_FV + FOX_W].astype(BF16)
        for h in range(SWA_H):
            sq[:, _hs(h)] = _head_norm(p_ref[:, _hs(h, C_SQ)], sp_ref[R_SQ:R_SQ + 1, :]).astype(BF16)
        for h in range(SWA_KV):
            sk[:, _hs(h)] = _head_norm(p_ref[:, _hs(h, C_SK)], sp_ref[R_SK:R_SK + 1, :]).astype(BF16)
        sv[...] = p_ref[:, C_SV:C_SV + SWA_KV_W].astype(BF16)
        for h in range(MEM_H):
            mq[:, _hs(h)] = _head_norm(p_ref[:, _hs(h, C_MQ)], sp_ref[R_MQ:R_MQ + 1, :]).astype(BF16)
        z = p_ref[:, C_FL:C_FL + HEAD] + sp_ref[R_FB:R_FB + 1, :]
        lane = lax.broadcasted_iota(jnp.int32, z.shape, 1)
        log_f = jnp.minimum(z, 0.0) - jnp.log(1.0 + jnp.exp(-jnp.abs(z)))
        log_f = jnp.where(lane < N_LOGIT, log_f, 0.0)
        c = jnp.dot(_tri(tm, True), log_f, precision=lax.Precision.HIGHEST, preferred_element_type=F32)
        c = c + carry[0:1, :]
        cc[...] = c
        carry[...] = jnp.broadcast_to(c[tm - 1:tm, :], carry.shape)

    def rows(w):
        return pl.BlockSpec((tm, w), lambda i: (i, 0))

    def shape(w, dt):
        return jax.ShapeDtypeStruct((t, w), dt)

    widths = [FOX_W, FOX_W, FOX_W, SWA_W, SWA_KV_W, SWA_KV_W, MEM_W]
    return pl.pallas_call(
        body, grid=(t // tm,), in_specs=[rows(IN_W), pl.BlockSpec((16, 128), lambda i: (0, 0))],
        out_specs=[rows(w) for w in widths] + [rows(HEAD)],
        out_shape=[shape(w, BF16) for w in widths] + [shape(HEAD, F32)],
        scratch_shapes=[pltpu.VMEM((8, 128), F32)], compiler_params=_params(1), name=name)(proj, sp)


def _attn_post_bwd(proj, sp, dfq, dfk, dfv, dsq, dsk, dsv, dmq, dc_col, dc_row_t, name):
    t = proj.shape[0]
    tm = min(t, 256)
    nb = t // tm

    def body(p_ref, sp_ref, dfq_r, dfk_r, dfv_r, dsq_r, dsk_r, dsv_r, dmq_r, dcc_r, dcr_r, dp_ref, dsp_ref, carry):
        @pl.when(pl.program_id(0) == 0)
        def _():
            carry[...] = jnp.zeros_like(carry)
            dsp_ref[...] = jnp.zeros_like(dsp_ref)

        def group(n_heads, col, row, d_ref):
            total = None
            for h in range(n_heads):
                dx, dg = _head_norm_bwd(p_ref[:, _hs(h, col)], sp_ref[row:row + 1, :], d_ref[:, _hs(h)])
                dp_ref[:, _hs(h, col)] = dx.astype(BF16)
                total = dg if total is None else total + dg
            dsp_ref[row:row + 1, :] += total

        group(FOX_H, C_FQ, R_FQ, dfq_r)
        group(FOX_H, C_FK, R_FK, dfk_r)
        dp_ref[:, C_FV:C_FV + FOX_W] = dfv_r[...].astype(BF16)
        group(SWA_H, C_SQ, R_SQ, dsq_r)
        group(SWA_KV, C_SK, R_SK, dsk_r)
        dp_ref[:, C_SV:C_SV + SWA_KV_W] = dsv_r[...].astype(BF16)
        group(MEM_H, C_MQ, R_MQ, dmq_r)
        dc = dcc_r[...] - dcr_r[...]
        rc = jnp.dot(_tri(tm, False), dc, precision=lax.Precision.HIGHEST, preferred_element_type=F32)
        rc = rc + carry[0:1, :]
        carry[...] = jnp.broadcast_to(rc[0:1, :], carry.shape)
        z = p_ref[:, C_FL:C_FL + HEAD] + sp_ref[R_FB:R_FB + 1, :]
        dz = rc * _sigmoid(-z)
        dp_ref[:, C_FL:C_FL + HEAD] = dz.astype(BF16)
        dsp_ref[R_FB:R_FB + 1, :] += jnp.sum(dz, axis=0, keepdims=True)

    def rows(w):
        return pl.BlockSpec((tm, w), lambda i: (nb - 1 - i, 0))

    small = pl.BlockSpec((16, 128), lambda i: (0, 0))
    widths = [FOX_W, FOX_W, FOX_W, SWA_W, SWA_KV_W, SWA_KV_W, MEM_W, HEAD, HEAD]
    return pl.pallas_call(
        body, grid=(nb,), in_specs=[rows(IN_W), small] + [rows(w) for w in widths],
        out_specs=[rows(IN_W), small],
        out_shape=[jax.ShapeDtypeStruct((t, IN_W), BF16), jax.ShapeDtypeStruct((16, 128), F32)],
        scratch_shapes=[pltpu.VMEM((8, 128), F32)], compiler_params=_params(1), name=name,
    )(proj, sp, dfq, dfk, dfv, dsq, dsk, dsv, dmq, dc_col, dc_row_t)


def _head_column(values):
    rows = values[0].shape[0]
    lane = lax.broadcasted_iota(jnp.int32, (rows, HEAD), 1)
    out = jnp.zeros((rows, HEAD), F32)
    for h, v in enumerate(values):
        out = jnp.where(lane == h, v, out)
    return out


def _head_row(values, n_rows=8):
    cols = values[0].shape[1]
    sub = lax.broadcasted_iota(jnp.int32, (n_rows, cols), 0)
    out = jnp.zeros((n_rows, cols), F32)
    for h, v in enumerate(values):
        out = jnp.where(sub == h, v, out)
    return out


def _delta(dmixed, o_a, o_b, o_c, name):
    t = dmixed.shape[0]
    tm = min(t, 512)

    def body(d_ref, a_ref, b_ref, c_ref, o_ref, rep_ref):
        cols = []
        for ref, n_heads, base in ((a_ref, FOX_H, 0), (b_ref, SWA_H, FOX_W), (c_ref, MEM_H, FOX_W + SWA_W)):
            for h in range(n_heads):
                cols.append(jnp.sum(d_ref[:, _hs(h, base)] * ref[:, _hs(h)], axis=-1, keepdims=True))
        o_ref[...] = _head_column(cols)
        for h in range(FOX_H):
            rep_ref[h] = jnp.broadcast_to(cols[h], (tm, HEAD))

    def rows(w):
        return pl.BlockSpec((tm, w), lambda i: (i, 0))

    return pl.pallas_call(body, grid=(t // tm,), in_specs=[rows(dmixed.shape[1]), rows(FOX_W), rows(SWA_W), rows(MEM_W)],
                          out_specs=[rows(HEAD), pl.BlockSpec((FOX_H, tm, HEAD), lambda i: (0, i, 0))],
                          out_shape=[jax.ShapeDtypeStruct((t, HEAD), F32), jax.ShapeDtypeStruct((FOX_H, t, HEAD), F32)],
                          compiler_params=_params(1), name=name)(dmixed, o_a, o_b, o_c)


def _fox_fwd(fq, fk, fv, c_rep, c_row, name, rides=()):
    t = fq.shape[0]
    tb = min(t, 512)
    nb = t // tb
    n_tiles = tb // HEAD

    def body(q_ref, k_ref, v_ref, cc_ref, cr_ref, o_ref, lse_ref, m_s, l_s, acc_s):
        qi, ki = pl.program_id(0), pl.program_id(1)

        @pl.when(ki == 0)
        def _():
            m_s[...] = jnp.full_like(m_s, NEG_INF)
            l_s[...] = jnp.zeros_like(l_s)
            acc_s[...] = jnp.zeros_like(acc_s)

        def step(diagonal):
            if diagonal:
                r = lax.broadcasted_iota(jnp.int32, (tb, HEAD), 0)
                c = lax.broadcasted_iota(jnp.int32, (tb, HEAD), 1)
            for h in range(FOX_H):
                s = _dot(q_ref[:, _hs(h)], k_ref[:, _hs(h)], "nt")
                cc = cc_ref[h]
                tiles, m_cur = [], None
                for j in range(n_tiles):
                    st = s[:, _hs(j)] * SCALE + cc - cr_ref[h:h + 1, _hs(j)]
                    if diagonal:
                        st = jnp.where(r >= c + j * HEAD, st, NEG_INF)
                    tiles.append(st)
                    m_cur = st if m_cur is None else jnp.maximum(m_cur, st)
                m_prev = m_s[h]
                m_new = jnp.maximum(m_prev, jnp.max(m_cur, axis=-1, keepdims=True))
                alpha = jnp.exp(m_prev - m_new)
                ps = [jnp.exp(st - m_new) for st in tiles]
                l_cur = ps[0]
                for p in ps[1:]:
                    l_cur = l_cur + p
                l_s[h] = alpha * l_s[h] + jnp.sum(l_cur, axis=-1, keepdims=True)
                p = jnp.concatenate([p.astype(BF16) for p in ps], axis=1)
                acc_s[:, _hs(h)] = alpha * acc_s[:, _hs(h)] + _dot(p, v_ref[:, _hs(h)])
                m_s[h] = m_new

        @pl.when(ki < qi)
        def _():
            step(False)

        @pl.when(ki == qi)
        def _():
            step(True)
            for h in range(FOX_H):
                o_ref[:, _hs(h)] = acc_s[:, _hs(h)] / l_s[h]
                lse_ref[h] = m_s[h] + jnp.log(l_s[h])

    qspec = pl.BlockSpec((tb, FOX_W), lambda i, j: (i, 0))
    kspec = pl.BlockSpec((tb, FOX_W), lambda i, j: (jnp.minimum(i, j), 0))
    rep = pl.BlockSpec((FOX_H, tb, HEAD), lambda i, j: (0, i, 0))
    return _call(
        name, body, (nb, nb),
        [qspec, kspec, kspec, rep, pl.BlockSpec((8, tb), lambda i, j: (0, jnp.minimum(i, j)))],
        [qspec, rep],
        [jax.ShapeDtypeStruct((t, FOX_W), F32), jax.ShapeDtypeStruct((FOX_H, t, HEAD), F32)],
        [fq, fk, fv, c_rep, c_row],
        scratch=[pltpu.VMEM((FOX_H, tb, HEAD), F32), pltpu.VMEM((FOX_H, tb, HEAD), F32), pltpu.VMEM((tb, FOX_W), F32)],
        rides=rides)


def _fox_bwd(fq, fk, fv, c_rep, c_row, dmixed, lse, delta, name, rides=()):
    t = fq.shape[0]
    tb = min(t, 512)
    nb = t // tb
    n_tiles = tb // HEAD

    def body(q_ref, k_ref, v_ref, cc_ref, cr_ref, do_ref, lse_ref, dl_ref,
             dq_ref, dk_ref, dv_ref, dcc_ref, dcr_ref):
        ki, qi = pl.program_id(0), pl.program_id(1)

        @pl.when((ki == 0) & (qi == 0))
        def _():
            dq_ref[...] = jnp.zeros_like(dq_ref)
            dcc_ref[...] = jnp.zeros_like(dcc_ref)

        @pl.when(qi == 0)
        def _():
            dk_ref[...] = jnp.zeros_like(dk_ref)
            dv_ref[...] = jnp.zeros_like(dv_ref)
            dcr_ref[...] = jnp.zeros_like(dcr_ref)

        def step(diagonal):
            rows = pl.ds(pl.multiple_of(qi * tb, tb), tb)
            if diagonal:
                r = lax.broadcasted_iota(jnp.int32, (tb, HEAD), 0)
                c = lax.broadcasted_iota(jnp.int32, (tb, HEAD), 1)
            row_sums, col_sums = [], []
            for h in range(FOX_H):
                q, k, v, do = q_ref[:, _hs(h)], k_ref[:, _hs(h)], v_ref[:, _hs(h)], do_ref[:, _hs(h)]
                s = _dot(q, k, "nt")
                dp = _dot(do, v, "nt")
                cc, lse_h, dl_h = cc_ref[h], lse_ref[h], dl_ref[h]
                ps, dss, row = [], [], None
                for j in range(n_tiles):
                    st = s[:, _hs(j)] * SCALE + cc - cr_ref[h:h + 1, _hs(j)]
                    if diagonal:
                        st = jnp.where(r >= c + j * HEAD, st, NEG_INF)
                    pt = jnp.exp(st - lse_h)
                    dst = pt * (dp[:, _hs(j)] - dl_h)
                    ps.append(pt.astype(BF16))
                    dss.append(dst)
                    row = dst if row is None else row + dst
                p = jnp.concatenate(ps, axis=1)
                ds = jnp.concatenate(dss, axis=1)
                dsb = ds.astype(BF16)
                dv_ref[:, _hs(h)] += _dot(p, do, "tn")
                dk_ref[:, _hs(h)] += _dot(dsb, q, "tn") * SCALE
                dq_ref[rows, _hs(h)] += _dot(dsb, k) * SCALE
                row_sums.append(jnp.sum(row, axis=1, keepdims=True))
                col_sums.append(jnp.sum(ds, axis=0, keepdims=True))
            dcc_ref[rows, :] += _head_column(row_sums)
            dcr_ref[...] += _head_row(col_sums)

        @pl.when(qi > ki)
        def _():
            step(False)

        @pl.when(qi == ki)
        def _():
            step(True)

    def qmap(j, i):
        return (jnp.maximum(i, j), 0)

    qspec = pl.BlockSpec((tb, FOX_W), qmap)
    kspec = pl.BlockSpec((tb, FOX_W), lambda j, i: (j, 0))
    rep = pl.BlockSpec((FOX_H, tb, HEAD), lambda j, i: (0, jnp.maximum(i, j), 0))
    rowspec = pl.BlockSpec((8, tb), lambda j, i: (0, j))
    return _call(
        name, body, (nb, nb), [qspec, kspec, kspec, rep, rowspec, qspec, rep, rep],
        [pl.BlockSpec((t, FOX_W), lambda j, i: (0, 0)), kspec, kspec,
         pl.BlockSpec((t, HEAD), lambda j, i: (0, 0)), rowspec],
        [jax.ShapeDtypeStruct((t, FOX_W), F32)] * 3 + [jax.ShapeDtypeStruct((t, HEAD), F32),
                                                       jax.ShapeDtypeStruct((8, t), F32)],
        [fq, fk, fv, c_rep, c_row, dmixed, lse, delta], rides=rides)


def _swa_logits(q, k_cur, k_prev, slope, first_block):
    w = SWA_BLOCK
    r = lax.broadcasted_iota(jnp.int32, (w, w), 0)
    j = lax.broadcasted_iota(jnp.int32, (w, w), 1)
    dist_cur = r - j
    dist_prev = w + r - j
    s_cur = _dot(q, k_cur, "nt") * SCALE - slope * dist_cur.astype(F32)
    s_cur = jnp.where(dist_cur >= 0, s_cur, NEG_INF)
    s_prev = _dot(q, k_prev, "nt") * SCALE - slope * dist_prev.astype(F32)
    s_prev = jnp.where((j > r) & jnp.logical_not(first_block), s_prev, NEG_INF)
    return s_cur, s_prev


def _slope(h):
    return float(2.0 ** (-8.0 * (h + 1) / SWA_H))


def _swa_fwd(sq, sk, sv, sp, name, rides=()):
    t = sq.shape[0]
    w = SWA_BLOCK
    nb = t // w
    group = SWA_H // SWA_KV

    def body(q_ref, kp_ref, kc_ref, vp_ref, vc_ref, sp_ref, o_ref, lse_ref):
        first = pl.program_id(0) == 0
        lses = []
        for h in range(SWA_H):
            kv = h // group
            s_cur, s_prev = _swa_logits(q_ref[:, _hs(h)], kc_ref[:, _hs(kv)], kp_ref[:, _hs(kv)], _slope(h), first)
            sink = sp_ref[R_SINK:R_SINK + 1, h:h + 1]
            m = jnp.maximum(jnp.maximum(jnp.max(s_cur, axis=-1, keepdims=True),
                                        jnp.max(s_prev, axis=-1, keepdims=True)), sink)
            p_cur = jnp.exp(s_cur - m)
            p_prev = jnp.exp(s_prev - m)
            l = jnp.sum(p_cur, axis=-1, keepdims=True) + jnp.sum(p_prev, axis=-1, keepdims=True) + jnp.exp(sink - m)
            o_ref[:, _hs(h)] = (_dot(p_cur, vc_ref[:, _hs(kv)]) + _dot(p_prev, vp_ref[:, _hs(kv)])) / l
            lses.append(m + jnp.log(l))
        lse_ref[...] = _head_column(lses)

    qspec = pl.BlockSpec((w, SWA_W), lambda n: (n, 0))
    cur = pl.BlockSpec((w, SWA_KV_W), lambda n: (n, 0))
    prev = pl.BlockSpec((w, SWA_KV_W), lambda n: (jnp.maximum(n - 1, 0), 0))
    return _call(
        name, body, (nb,), [qspec, prev, cur, prev, cur, pl.BlockSpec((16, 128), lambda n: (0, 0))],
        [qspec, pl.BlockSpec((w, HEAD), lambda n: (n, 0))],
        [jax.ShapeDtypeStruct((t, SWA_W), F32), jax.ShapeDtypeStruct((t, HEAD), F32)],
        [sq, sk, sk, sv, sv, sp], rides=rides)


def _swa_bwd(sq, sk, sv, sp, dmixed, lse, delta, name):
    t = sq.shape[0]
    w = SWA_BLOCK
    nb = t // w
    group = SWA_H // SWA_KV
    do_block = FOX_W // SWA_W
    assert FOX_W % SWA_W == 0

    def body(q_ref, kp_ref, kc_ref, vp_ref, vc_ref, sp_ref, do_ref, lse_ref, dl_ref,
             dq_ref, dk_ref, dv_ref, dsp_ref, ck, cv):
        step = pl.program_id(0)
        first = step == nb - 1

        @pl.when(step == 0)
        def _():
            ck[...] = jnp.zeros_like(ck)
            cv[...] = jnp.zeros_like(cv)
            dsp_ref[...] = jnp.zeros_like(dsp_ref)

        dk_cur = [None] * SWA_KV
        dk_prev = [None] * SWA_KV
        dv_cur = [None] * SWA_KV
        dv_prev = [None] * SWA_KV
        dsinks = []

        def add(lst, i, v):
            lst[i] = v if lst[i] is None else lst[i] + v

        for h in range(SWA_H):
            kv = h // group
            q, do = q_ref[:, _hs(h)], do_ref[:, _hs(h)]
            kc, kp, vc, vp = kc_ref[:, _hs(kv)], kp_ref[:, _hs(kv)], vc_ref[:, _hs(kv)], vp_ref[:, _hs(kv)]
            s_cur, s_prev = _swa_logits(q, kc, kp, _slope(h), first)
            lse_h = lse_ref[:, h:h + 1]
            dl_h = dl_ref[:, FOX_H + h:FOX_H + h + 1]
            p_cur = jnp.exp(s_cur - lse_h)
            p_prev = jnp.exp(s_prev - lse_h)
            p_sink = jnp.exp(sp_ref[R_SINK:R_SINK + 1, h:h + 1] - lse_h)
            ds_cur = p_cur * (_dot(do, vc, "nt") - dl_h)
            ds_prev = p_prev * (_dot(do, vp, "nt") - dl_h)
            dq_ref[:, _hs(h)] = (_dot(ds_cur, kc) + _dot(ds_prev, kp)) * SCALE
            add(dk_cur, kv, _dot(ds_cur, q, "tn") * SCALE)
            add(dk_prev, kv, _dot(ds_prev, q, "tn") * SCALE)
            add(dv_cur, kv, _dot(p_cur, do, "tn"))
            add(dv_prev, kv, _dot(p_prev, do, "tn"))
            dsinks.append(-jnp.sum(p_sink * dl_h, axis=0, keepdims=True))
        for kv in range(SWA_KV):
            dk_ref[:, _hs(kv)] = dk_cur[kv] + ck[:, _hs(kv)]
            dv_ref[:, _hs(kv)] = dv_cur[kv] + cv[:, _hs(kv)]
            ck[:, _hs(kv)] = dk_prev[kv]
            cv[:, _hs(kv)] = dv_prev[kv]
        lane = lax.broadcasted_iota(jnp.int32, (1, HEAD), 1)
        row = jnp.zeros((1, HEAD), F32)
        for h in range(SWA_H):
            row = jnp.where(lane == h, dsinks[h], row)
        dsp_ref[R_SINK:R_SINK + 1, :] += row

    def rev(n):
        return nb - 1 - n

    qspec = pl.BlockSpec((w, SWA_W), lambda n: (rev(n), 0))
    cur = pl.BlockSpec((w, SWA_KV_W), lambda n: (rev(n), 0))
    prev = pl.BlockSpec((w, SWA_KV_W), lambda n: (jnp.maximum(rev(n) - 1, 0), 0))
    col = pl.BlockSpec((w, HEAD), lambda n: (rev(n), 0))
    small = pl.BlockSpec((16, 128), lambda n: (0, 0))
    return pl.pallas_call(
        body, grid=(nb,),
        in_specs=[qspec, prev, cur, prev, cur, small, pl.BlockSpec((w, SWA_W), lambda n: (rev(n), do_block)), col, col],
        out_specs=[qspec, cur, cur, small],
        out_shape=[jax.ShapeDtypeStruct((t, SWA_W), F32), jax.ShapeDtypeStruct((t, SWA_KV_W), F32),
                   jax.ShapeDtypeStruct((t, SWA_KV_W), F32), jax.ShapeDtypeStruct((16, 128), F32)],
        scratch_shapes=[pltpu.VMEM((w, SWA_KV_W), F32), pltpu.VMEM((w, SWA_KV_W), F32)],
        compiler_params=_params(1), name=name)(sq, sk, sk, sv, sv, sp, dmixed, lse, delta)


def _mem_pre(mkv, sp, name):
    m = mkv.shape[0]

    def body(x_ref, sp_ref, k_ref, v_ref):
        for h in range(MEM_H):
            k_ref[:, _hs(h)] = _head_norm(x_ref[:, _hs(h)], sp_ref[R_MK:R_MK + 1, :]).astype(BF16)
        v_ref[...] = x_ref[:, MEM_W:2 * MEM_W].astype(BF16)

    out = jax.ShapeDtypeStruct((m, MEM_W), BF16)
    return pl.pallas_call(body, out_shape=[out, out], name=name)(mkv, sp)


def _mem_post_bwd(mkv, sp, dmk, dmv, name):
    m = mkv.shape[0]

    def body(x_ref, sp_ref, dk_ref, dv_ref, d_ref, dsp_ref):
        dsp_ref[...] = jnp.zeros_like(dsp_ref)
        total = None
        for h in range(MEM_H):
            dx, dg = _head_norm_bwd(x_ref[:, _hs(h)], sp_ref[R_MK:R_MK + 1, :], dk_ref[:, _hs(h)])
            d_ref[:, _hs(h)] = dx.astype(BF16)
            total = dg if total is None else total + dg
        d_ref[:, MEM_W:2 * MEM_W] = dv_ref[...].astype(BF16)
        dsp_ref[R_MK:R_MK + 1, :] = total

    return pl.pallas_call(body, out_shape=[jax.ShapeDtypeStruct((m, 2 * MEM_W), BF16),
                                           jax.ShapeDtypeStruct((16, 128), F32)], name=name)(mkv, sp, dmk, dmv)


def _mem_fwd(mq, mk, mv, name):
    t = mq.shape[0]
    m = mk.shape[0]
    tq = min(t, 512)

    def body(q_ref, k_ref, v_ref, o_ref, lse_ref):
        lses = []
        for h in range(MEM_H):
            s = _dot(q_ref[:, _hs(h)], k_ref[:, _hs(h)], "nt") * SCALE
            mx = jnp.max(s, axis=-1, keepdims=True)
            p = jnp.exp(s - mx)
            l = jnp.sum(p, axis=-1, keepdims=True)
            o_ref[:, _hs(h)] = _dot(p, v_ref[:, _hs(h)]) / l
            lses.append(mx + jnp.log(l))
        lse_ref[...] = _head_column(lses)

    qspec = pl.BlockSpec((tq, MEM_W), lambda i: (i, 0))
    kspec = pl.BlockSpec((m, MEM_W), lambda i: (0, 0))
    return pl.pallas_call(
        body, grid=(t // tq,), in_specs=[qspec, kspec, kspec],
        out_specs=[qspec, pl.BlockSpec((tq, HEAD), lambda i: (i, 0))],
        out_shape=[jax.ShapeDtypeStruct((t, MEM_W), F32), jax.ShapeDtypeStruct((t, HEAD), F32)],
        compiler_params=_params(1), name=name)(mq, mk, mv)


def _mem_bwd(mq, mk, mv, dmixed, lse, delta, name):
    t = mq.shape[0]
    m = mk.shape[0]
    tq = min(t, 512)
    do_block = (FOX_W + SWA_W) // MEM_W
    assert (FOX_W + SWA_W) % MEM_W == 0

    def body(q_ref, k_ref, v_ref, do_ref, lse_ref, dl_ref, dq_ref, dk_ref, dv_ref):
        @pl.when(pl.program_id(0) == 0)
        def _():
            dk_ref[...] = jnp.zeros_like(dk_ref)
            dv_ref[...] = jnp.zeros_like(dv_ref)

        for h in range(MEM_H):
            q, k, v, do = q_ref[:, _hs(h)], k_ref[:, _hs(h)], v_ref[:, _hs(h)], do_ref[:, _hs(h)]
            s = _dot(q, k, "nt") * SCALE
            p = jnp.exp(s - lse_ref[:, h:h + 1])
            col = FOX_H + SWA_H + h
            ds = p * (_dot(do, v, "nt") - dl_ref[:, col:col + 1])
            dq_ref[:, _hs(h)] = _dot(ds, k) * SCALE
            dk_ref[:, _hs(h)] += _dot(ds, q, "tn") * SCALE
            dv_ref[:, _hs(h)] += _dot(p, do, "tn")

    qspec = pl.BlockSpec((tq, MEM_W), lambda i: (i, 0))
    kspec = pl.BlockSpec((m, MEM_W), lambda i: (0, 0))
    col = pl.BlockSpec((tq, HEAD), lambda i: (i, 0))
    return pl.pallas_call(
        body, grid=(t // tq,),
        in_specs=[qspec, kspec, kspec, pl.BlockSpec((tq, MEM_W), lambda i: (i, do_block)), col, col],
        out_specs=[qspec, kspec, kspec],
        out_shape=[jax.ShapeDtypeStruct((t, MEM_W), F32), jax.ShapeDtypeStruct((m, MEM_W), F32),
                   jax.ShapeDtypeStruct((m, MEM_W), F32)],
        compiler_params=_params(1), name=name)(mq, mk, mv, dmixed, lse, delta)


def _all_gather(xs, name):
    n = len(xs)

    def body(*refs):
        x_refs, o_refs = refs[:n], refs[n:2 * n]
        send_sems, recv_sems, local_sems = refs[2 * n:]
        x, y, c = _me()
        me, sibling = (x, y, c), (x, y, 1 - c)
        chips = [(1 - x, y), (x, 1 - y), (1 - x, 1 - y)]

        def copy(a, k, block, to, src=None):
            slot = o_refs[a].at[_lin(block)]
            return pltpu.make_async_remote_copy(
                src_ref=slot if src is None else src, dst_ref=slot, send_sem=send_sems.at[a, k],
                recv_sem=recv_sems.at[a, k], device_id=to, device_id_type=MESH)

        mine = [pltpu.make_async_copy(x_refs[a], o_refs[a].at[_lin(me)], local_sems.at[a]) for a in range(n)]
        for cp in mine:
            cp.start()
        first = []
        for a in range(n):
            first.append(copy(a, 0, me, sibling, src=x_refs[a]))
            first += [copy(a, 1 + j, me, (*chip, c), src=x_refs[a]) for j, chip in enumerate(chips)]
        for cp in first:
            cp.start()
        passed = []
        for j, chip in enumerate(chips):
            for a in range(n):
                copy(a, 1 + j, (*chip, c), me).wait_recv()
                cp = copy(a, 4 + j, (*chip, c), sibling)
                cp.start()
                passed.append(cp)
        for a in range(n):
            copy(a, 0, sibling, me).wait_recv()
            for j, chip in enumerate(chips):
                copy(a, 4 + j, (*chip, 1 - c), me).wait_recv()
        for cp in first + passed:
            cp.wait_send()
        for cp in mine:
            cp.wait()

    return pl.pallas_call(
        body, in_specs=[ANY] * n, out_specs=[ANY] * n,
        out_shape=[jax.ShapeDtypeStruct((N_DEV,) + x.shape, x.dtype) for x in xs],
        scratch_shapes=[pltpu.SemaphoreType.DMA((n, 7)), pltpu.SemaphoreType.DMA((n, 7)),
                        pltpu.SemaphoreType.DMA((n,))],
        name=name)(*xs)


def _peers():
    x, y, c = _me()
    out = []
    for k in range(1, N_DEV):
        kx, ky, kc = (k >> 2) & 1, (k >> 1) & 1, k & 1
        out.append(((1 - x) if kx else x, (1 - y) if ky else y, (1 - c) if kc else c))
    return out


def _all_reduce_small(xs, name):
    n = len(xs)

    def body(*refs):
        x_refs, o_refs = refs[:n], refs[n:2 * n]
        bufs = refs[2 * n:3 * n]
        send_sems, recv_sems = refs[3 * n:]
        me = _lin(_me())
        peers = _peers()
        for a in range(n):
            bufs[a][me] = x_refs[a][...]
        sends = []
        for a in range(n):
            for k, peer in enumerate(peers):
                sends.append(pltpu.make_async_remote_copy(
                    src_ref=bufs[a].at[me], dst_ref=bufs[a].at[me], send_sem=send_sems.at[a, k],
                    recv_sem=recv_sems.at[a, k], device_id=peer, device_id_type=MESH))
        for cp in sends:
            cp.start()
        for a in range(n):
            for k, peer in enumerate(peers):
                pltpu.make_async_remote_copy(
                    src_ref=bufs[a].at[me], dst_ref=bufs[a].at[_lin(peer)], send_sem=send_sems.at[a, k],
                    recv_sem=recv_sems.at[a, k], device_id=peer, device_id_type=MESH).wait_recv()
        for cp in sends:
            cp.wait_send()
        for a in range(n):
            total = bufs[a][0]
            for q in range(1, N_DEV):
                total = total + bufs[a][q]
            o_refs[a][...] = total

    vmem = pl.BlockSpec(memory_space=pltpu.VMEM)
    return pl.pallas_call(
        body, in_specs=[vmem] * n, out_specs=[vmem] * n,
        out_shape=[jax.ShapeDtypeStruct(x.shape, F32) for x in xs],
        scratch_shapes=[pltpu.VMEM((N_DEV,) + x.shape, F32) for x in xs]
        + [pltpu.SemaphoreType.DMA((n, 7)), pltpu.SemaphoreType.DMA((n, 7))],
        name=name)(*xs)


def _pair_add(part, got, name):
    _, rows, cols = part.shape
    tm = _rows_tile(rows, cols * 2, budget=2 << 20)
    core = jnp.reshape(lax.axis_index("c"), (1,)).astype(jnp.int32)

    def body(c_ref, p_ref, g_ref, o_ref):
        o_ref[...] = (p_ref[...].astype(F32) + g_ref[...].astype(F32)).astype(BF16)

    spec = pl.BlockSpec((None, tm, cols), lambda q, i, c: (q, i, 0))
    grid_spec = pltpu.PrefetchScalarGridSpec(
        num_scalar_prefetch=1, grid=(4, rows // tm),
        in_specs=[pl.BlockSpec((None, tm, cols), lambda q, i, c: (2 * q + c[0], i, 0)), spec], out_specs=spec)
    return pl.pallas_call(body, grid_spec=grid_spec, out_shape=jax.ShapeDtypeStruct((4, rows, cols), BF16),
                          compiler_params=_params(2), name=name)(core, part, got)


def _adam_math(w, g, m, v):
    nm = ADAM_B1 * m + (1.0 - ADAM_B1) * g
    nv = ADAM_B2 * v + (1.0 - ADAM_B2) * (g * g)
    m_hat = nm / (1.0 - ADAM_B1 ** ADAM_STEP)
    v_hat = nv / (1.0 - ADAM_B2 ** ADAM_STEP)
    return -ADAM_LR * (m_hat / (jnp.sqrt(v_hat) + ADAM_EPS) + ADAM_WD * w), nm, nv


def _sum_chips(got, name):
    _, rows, cols = got.shape
    tm = _rows_tile(rows, cols * 2 * 4, budget=2 << 20)

    def body(r_ref, o_ref):
        o_ref[...] = ((r_ref[0].astype(F32) + r_ref[1].astype(F32)) + r_ref[2].astype(F32)) + r_ref[3].astype(F32)

    return pl.pallas_call(
        body, grid=(rows // tm,), in_specs=[pl.BlockSpec((4, tm, cols), lambda i: (0, i, 0))],
        out_specs=pl.BlockSpec((tm, cols), lambda i: (i, 0)), out_shape=jax.ShapeDtypeStruct((rows, cols), F32),
        compiler_params=_params(1), name=name)(got)


def _sum_adamw(got, col_block, w, m, v, name):
    _, rows, cols = w.shape
    tm = _rows_tile(rows, cols * 4, budget=1 << 20)

    def body(r_ref, w_ref, m_ref, v_ref, g_ref, d_ref, nm_ref, nv_ref):
        g = ((r_ref[0].astype(F32) + r_ref[1].astype(F32)) + r_ref[2].astype(F32)) + r_ref[3].astype(F32)
        g_ref[...] = g
        d_ref[...], nm_ref[...], nv_ref[...] = _adam_math(w_ref[...], g, m_ref[...], v_ref[...])

    spec = pl.BlockSpec((None, tm, cols), lambda i: (0, i, 0))
    out = jax.ShapeDtypeStruct(w.shape, F32)
    return pl.pallas_call(
        body, grid=(rows // tm,), in_specs=[pl.BlockSpec((4, tm, cols), lambda i: (0, i, col_block)), spec, spec, spec],
        out_specs=[spec] * 4, out_shape=[out] * 4, compiler_params=_params(1), name=name)(got, w, m, v)


def _adamw(w, g, m, v, name):
    rows, cols = w.shape

    def body(w_ref, g_ref, m_ref, v_ref, d_ref, nm_ref, nv_ref):
        d_ref[...], nm_ref[...], nv_ref[...] = _adam_math(w_ref[...], g_ref[...], m_ref[...], v_ref[...])

    tm = _rows_tile(rows, cols * 4, budget=2 << 20, mult=8)
    spec = pl.BlockSpec((tm, cols), lambda i: (i, 0))
    out = jax.ShapeDtypeStruct(w.shape, F32)
    return pl.pallas_call(body, grid=(rows // tm,), in_specs=[spec] * 4, out_specs=[spec] * 3,
                          out_shape=[out] * 3, compiler_params=_params(1), name=name)(w, g, m, v)


def _permute_in(w):
    logit0 = 3 * FOX_W
    pad = jnp.zeros(w.shape[:-1] + (HEAD - N_LOGIT,), w.dtype)
    return jnp.concatenate([w[..., :logit0], w[..., logit0 + N_LOGIT:], w[..., logit0:logit0 + N_LOGIT], pad], axis=-1)


def _unpermute_in(w):
    logit0 = 3 * FOX_W
    return jnp.concatenate([w[..., :logit0], w[..., C_FL:C_FL + N_LOGIT], w[..., logit0:C_FL]], axis=-1)


def _pad_row(v, width):
    return jnp.pad(v, ((0, 0), (0, width - v.shape[1])))


def _pack_small(fq, fk, sq, sk, mq, mk, fb, sinks):
    rows = [fq, fk, sq, sk, mq, mk, _pad_row(fb, HEAD), _pad_row(sinks, HEAD)]
    return jnp.concatenate(rows + [jnp.zeros((8, HEAD), F32)], axis=0)


def _pack_norms(a, b, c, d):
    return jnp.concatenate([a, b, c, d, jnp.zeros((4, a.shape[1]), F32)], axis=0)


def kernel(x, mem, ffn1_norm, ffn1_gate, ffn1_up, ffn1_down, mix_norm, mem_norm, w_in, forget_bias, w_mem_k, w_mem_v, fox_q_gain, fox_k_gain, swa_q_gain, swa_k_gain, swa_sinks, mem_q_gain, mem_k_gain, w_out, ffn2_norm, ffn2_gate, ffn2_up, ffn2_down, loss_target, m_ffn1_norm, m_ffn1_gate, m_ffn1_up, m_ffn1_down, m_mix_norm, m_mem_norm, m_w_in, m_forget_bias, m_w_mem_k, m_w_mem_v, m_fox_q_gain, m_fox_k_gain, m_swa_q_gain, m_swa_k_gain, m_swa_sinks, m_mem_q_gain, m_mem_k_gain, m_w_out, m_ffn2_norm, m_ffn2_gate, m_ffn2_up, m_ffn2_down, v_ffn1_norm, v_ffn1_gate, v_ffn1_up, v_ffn1_down, v_mix_norm, v_mem_norm, v_w_in, v_forget_bias, v_w_mem_k, v_w_mem_v, v_fox_q_gain, v_fox_k_gain, v_swa_q_gain, v_swa_k_gain, v_swa_sinks, v_mem_q_gain, v_mem_k_gain, v_w_out, v_ffn2_norm, v_ffn2_gate, v_ffn2_up, v_ffn2_down):
    x0 = x[0]
    mem0 = mem[0]
    target = loss_target[0]
    t, d = x0.shape
    d_shard = w_in.shape[1]
    m_len = mem0.shape[0]
    tm = min(t, 512)
    tk = min(t, 512)
    tn = IN_W // 3
    tkw, tnw = min(t, 1024), IN_W // 3

    def swap(a):
        return jnp.swapaxes(a, 1, 2)

    gate1, up1, gate2, up2 = swap(ffn1_gate), swap(ffn1_up), swap(ffn2_gate), swap(ffn2_up)

    local = {
        "g1": gate1[0], "u1": up1[0], "d1": ffn1_down[0],
        "g2": gate2[0], "u2": up2[0], "d2": ffn2_down[0],
        "in": _permute_in(w_in[0]), "out": w_out[0],
        "mkv": jnp.concatenate([w_mem_k[0], w_mem_v[0]], axis=1),
    }
    shard = {k: _cast_bf16(v, f"cast_{k}") for k, v in local.items()}
    sp = _pack_small(fox_q_gain, fox_k_gain, swa_q_gain, swa_k_gain, mem_q_gain, mem_k_gain, forget_bias, swa_sinks)
    wt = {}

    wt["g1"], wt["u1"] = _all_gather([shard["g1"], shard["u1"]], "gather_ffn1_in")
    xn1 = _rms_fwd(x0, ffn1_norm, "ffn1_norm")
    half_fs = shard["g2"].shape[0] // 2
    (a1, b1, h1), ((wt["d1"], wt["in"]),) = _ffn_up(
        xn1, wt["g1"], wt["u1"], "ffn1", rides=[_ride_gather([shard["d1"], shard["in"]], 0.87)])
    tc = min(t, CONTRACT_ROWS)
    (x1, hn), ((wt["out"], wt["mkv"]),) = _ffn_down(
        x0, h1, wt["d1"], "ffn1", rides=[_ride_gather([shard["out"], shard["mkv"]], 0.6)],
        tail=_tail_norm(mix_norm, t, d, tc))
    w_in_full = wt["in"].reshape(d, IN_W)

    proj, (half,) = _mm(
        "proj", [(hn, pl.BlockSpec((tm, d), lambda n, i, k: (i, 0)),
                  w_in_full, pl.BlockSpec((d, tn), lambda n, i, k: (0, n)))],
        "nn", (3, t // tm, 1), jax.ShapeDtypeStruct((t, IN_W), F32), pl.BlockSpec((tm, tn), lambda n, i, k: (i, n)),
        rides=[_ride_gather_chips([shard["g2"]], rows=(0, half_fs))])
    w_out_full = wt["out"].reshape(d, d)
    w_mkv_full = wt["mkv"].reshape(d, 2 * MEM_W)
    fq, fk, fv, sq, sk, sv, mq, c_col = _attn_pre(proj, sp, "attn_pre")
    c_row = jnp.transpose(c_col[:, :8])
    c_rep = jnp.broadcast_to(c_row[:FOX_H, :, None], (FOX_H, t, HEAD))

    mn = _rms_fwd(mem0, mem_norm, "mem_norm")
    mkv = _mm("mem_kv", [(mn, pl.BlockSpec((m_len, d), lambda k: (0, 0)),
                          w_mkv_full, pl.BlockSpec((d, 2 * MEM_W), lambda k: (0, 0)))],
              "nn", (1,), jax.ShapeDtypeStruct((m_len, 2 * MEM_W), F32),
              pl.BlockSpec((m_len, 2 * MEM_W), lambda k: (0, 0)))
    mk, mv = _mem_pre(mkv, sp, "mem_pre")

    (o_a, lse_a), (half, half_u2) = _fox_fwd(
        fq, fk, fv, c_rep, c_row, "fox_fwd",
        rides=[_ride_gather_chips([shard["g2"]], rows=(half_fs, half_fs), into=half),
               _ride_gather_chips([shard["u2"]], rows=(0, half_fs))])
    (o_b, lse_b), ((wt["g2"],), half_u2) = _swa_fwd(
        sq, sk, sv, sp, "swa_fwd",
        rides=[_ride_gather_sibling(half), _ride_gather_chips([shard["u2"]], rows=(half_fs, half_fs), into=half_u2)])
    o_c, lse_c = _mem_fwd(mq, mk, mv, "mem_fwd")

    def rows_spec(width):
        return pl.BlockSpec((tm, width), lambda i, k: (i, 0))

    def wout_rows(first, width):
        assert first % width == 0
        return pl.BlockSpec((width, d), lambda i, k: (first // width, 0), pipeline_mode=pl.Buffered(1))

    xspec = pl.BlockSpec((tm, d), lambda i, k: (i, 0))
    (x2, xn2), ((wt["u2"],),) = _mm(
        "mix_out",
        [(o_a, rows_spec(FOX_W), w_out_full, wout_rows(0, FOX_W)),
         (o_b, rows_spec(SWA_W), w_out_full, wout_rows(FOX_W, SWA_W)),
         (o_c, rows_spec(MEM_W), w_out_full, wout_rows(FOX_W + SWA_W, MEM_W))],
        "nn", (t // tm, 1), jax.ShapeDtypeStruct((t, d), F32), xspec, res=x1, res_spec=xspec,
        rides=[_ride_gather_sibling(half_u2)], tail=_tail_norm(ffn2_norm, t, d, tm))

    (a2, b2, h2), ((wt["d2"],),) = _ffn_up(xn2, wt["g2"], wt["u2"], "ffn2", rides=[_ride_gather([shard["d2"]], 0.75)])
    dy, dyb, sq_err = _ffn_down(x2, h2, wt["d2"], "ffn2", tail=_tail_loss(target, t, d, tc))
    loss = lax.psum(0.5 * sq_err[0, 0] / d, ("x", "y", "c"))

    got = {}
    paired = {}
    landed = {}

    def pair(k, part):
        paired[k] = _pair_add(part, got[k], f"pair_{k}")

    (dg2, du2), _ = _ffn_dact(dyb, wt["d2"], a2, b2, "ffn2")
    part_d2 = _ffn_dw(h2, dyb, 0.5, "ffn2_dwd")
    part_g2, ((got["d2"],),) = _ffn_dw(dg2, xn2, 1.0, "ffn2_dwg", rides=[_ride_scatter_sibling([part_d2])])
    pair("d2", part_d2)
    half_rows = part_d2.shape[1] // 2
    first, second = (0, half_rows), (half_rows, half_rows)
    part_u2, (half, (got["g2"],)) = _ffn_dw(
        du2, xn2, 1.0, "ffn2_dwu",
        rides=[_ride_scatter_chips([paired["d2"]], first), _ride_scatter_sibling([part_g2])])
    pair("g2", part_g2)
    dxn2, ((landed["d2"],),) = _ffn_contract(
        dg2, wt["g2"], "ffn2_dxn_g", rides=[_ride_scatter_chips([paired["d2"]], second, into=half)])
    (dx2, dx2b, dgain_ffn2), (half_g2, (got["u2"],)) = _ffn_contract(
        du2, wt["u2"], "ffn2_dxn_u", res=dxn2,
        rides=[_ride_scatter_chips([paired["g2"]], first), _ride_scatter_sibling([part_u2])],
        tail=_tail_norm_bwd(x2, ffn2_norm, dy, t, d, tc))
    pair("u2", part_u2)

    dmixed = _mm("mix_out_dx", [(dx2b, xspec, w_out_full, pl.BlockSpec((d, d), lambda i, k: (0, 0)))],
                 "nt", (t // tm, 1), jax.ShapeDtypeStruct((t, d), F32), xspec)

    def k_rows(width):
        return pl.BlockSpec((tk, width), lambda j, k: (k, 0))

    part_out = [
        _mm(f"mix_out_dw{i}", [(o, k_rows(width), dx2b, k_rows(d))], "tn", (1, t // tk),
            jax.ShapeDtypeStruct((width, d), BF16), pl.BlockSpec((width, d), lambda j, k: (0, 0)))
        for i, (o, width) in enumerate(((o_a, FOX_W), (o_b, SWA_W), (o_c, MEM_W)))
    ]
    part_out = jnp.concatenate(part_out, axis=0).reshape(N_DEV, d_shard, d)

    delta, delta_rep = _delta(dmixed, o_a, o_b, o_c, "attn_delta")
    (dfq, dfk, dfv, dc_col, dc_row), ((landed["g2"],), (landed["u2"],)) = _fox_bwd(
        fq, fk, fv, c_rep, c_row, dmixed, lse_a, delta_rep, "fox_bwd",
        rides=[_ride_scatter_chips([paired["g2"]], second, into=half_g2), _ride_scatter_chips([paired["u2"]])])
    dsq, dsk, dsv, dsp_sink = _swa_bwd(sq, sk, sv, sp, dmixed, lse_b, delta, "swa_bwd")
    dmq, dmk, dmv = _mem_bwd(mq, mk, mv, dmixed, lse_c, delta, "mem_bwd")

    dmkv, dsp_mem = _mem_post_bwd(mkv, sp, dmk, dmv, "mem_post_bwd")
    part_mkv = _mm("mem_kv_dw", [(mn, pl.BlockSpec((m_len, d), lambda k: (0, 0)),
                                  dmkv, pl.BlockSpec((m_len, 2 * MEM_W), lambda k: (0, 0)))],
                   "tn", (1,), jax.ShapeDtypeStruct((d, 2 * MEM_W), BF16),
                   pl.BlockSpec((d, 2 * MEM_W), lambda k: (0, 0))).reshape(N_DEV, d_shard, 2 * MEM_W)
    dmn = _mm("mem_kv_dx", [(dmkv, pl.BlockSpec((m_len, 2 * MEM_W), lambda k: (0, 0)),
                             w_mkv_full, pl.BlockSpec((d, 2 * MEM_W), lambda k: (0, 0)))],
              "nt", (1,), jax.ShapeDtypeStruct((m_len, d), F32), pl.BlockSpec((m_len, d), lambda k: (0, 0)))
    _, _, dgain_mem = _rms_bwd(mem0, mem_norm, dmn, None, "mem_norm_bwd")

    dc_row_t = _pad_row(jnp.transpose(dc_row), HEAD)
    dproj, dsp_attn = _attn_post_bwd(proj, sp, dfq, dfk, dfv, dsq, dsk, dsv, dmq, dc_col, dc_row_t, "attn_post_bwd")
    (dx1, dx1b, dgain_mix), ((got["out"], got["mkv"]),) = _mm(
        "proj_dx", [(dproj, pl.BlockSpec((tc, IN_W), lambda i, k: (i, 0)),
                     w_in_full, pl.BlockSpec((d, IN_W), lambda i, k: (0, 0), pipeline_mode=pl.Buffered(1)))],
        "nt", (t // tc, 1), jax.ShapeDtypeStruct((t, d), F32), pl.BlockSpec((tc, d), lambda i, k: (i, 0)),
        rides=[_ride_scatter_sibling([part_out, part_mkv])], tail=_tail_norm_bwd(x1, mix_norm, dx2, t, d, tc))
    pair("out", part_out)
    pair("mkv", part_mkv)
    part_in, ((landed["out"], landed["mkv"]),) = _mm(
        "proj_dw", [(hn, pl.BlockSpec((tkw, d), lambda n, k: (k, 0)),
                     dproj, pl.BlockSpec((tkw, tnw), lambda n, k: (k, n)))],
        "tn", (IN_W // tnw, t // tkw), jax.ShapeDtypeStruct((d, IN_W), BF16), pl.BlockSpec((d, tnw), lambda n, k: (0, n)),
        rides=[_ride_scatter_chips([paired["out"], paired["mkv"]])])
    part_in = part_in.reshape(N_DEV, d_shard, IN_W)

    (dg1, du1), ((got["in"],),) = _ffn_dact(dx1b, wt["d1"], a1, b1, "ffn1", rides=[_ride_scatter_sibling([part_in])])
    pair("in", part_in)
    part_d1, ((landed["in"],),) = _ffn_dw(h1, dx1b, 0.5, "ffn1_dwd", rides=[_ride_scatter_chips([paired["in"]])])
    part_g1, ((got["d1"],),) = _ffn_dw(dg1, xn1, 1.0, "ffn1_dwg", rides=[_ride_scatter_sibling([part_d1])])
    pair("d1", part_d1)
    part_u1, ((landed["d1"],), (got["g1"],)) = _ffn_dw(
        du1, xn1, 1.0, "ffn1_dwu", rides=[_ride_scatter_chips([paired["d1"]]), _ride_scatter_sibling([part_g1])])
    pair("g1", part_g1)
    dxn1, ((landed["g1"],), (got["u1"],)) = _ffn_contract(
        dg1, wt["g1"], "ffn1_dxn_g", rides=[_ride_scatter_chips([paired["g1"]]), _ride_scatter_sibling([part_u1])])
    pair("u1", part_u1)
    (grad_x, _, dgain_ffn1), ((landed["u1"],),) = _ffn_contract(
        du1, wt["u1"], "ffn1_dxn_u", res=dxn1, rides=[_ride_scatter_chips([paired["u1"]])],
        tail=_tail_norm_bwd(x0, ffn1_norm, dx1, t, d, tc))

    norms_sum, small_sum = _all_reduce_small(
        [_pack_norms(dgain_ffn1, dgain_mix, dgain_mem, dgain_ffn2), dsp_attn + dsp_sink + dsp_mem], "reduce_small")

    result = {
        "ffn1_gate": map(swap, _sum_adamw(landed["g1"], 0, gate1, swap(m_ffn1_gate), swap(v_ffn1_gate), "adamw_ffn1_gate")),
        "ffn1_up": map(swap, _sum_adamw(landed["u1"], 0, up1, swap(m_ffn1_up), swap(v_ffn1_up), "adamw_ffn1_up")),
        "ffn1_down": _sum_adamw(landed["d1"], 0, ffn1_down, m_ffn1_down, v_ffn1_down, "adamw_ffn1_down"),
        "w_mem_k": _sum_adamw(landed["mkv"], 0, w_mem_k, m_w_mem_k, v_w_mem_k, "adamw_w_mem_k"),
        "w_mem_v": _sum_adamw(landed["mkv"], 1, w_mem_v, m_w_mem_v, v_w_mem_v, "adamw_w_mem_v"),
        "w_out": _sum_adamw(landed["out"], 0, w_out, m_w_out, v_w_out, "adamw_w_out"),
        "ffn2_gate": map(swap, _sum_adamw(landed["g2"], 0, gate2, swap(m_ffn2_gate), swap(v_ffn2_gate), "adamw_ffn2_gate")),
        "ffn2_up": map(swap, _sum_adamw(landed["u2"], 0, up2, swap(m_ffn2_up), swap(v_ffn2_up), "adamw_ffn2_up")),
        "ffn2_down": _sum_adamw(landed["d2"], 0, ffn2_down, m_ffn2_down, v_ffn2_down, "adamw_ffn2_down"),
    }
    grad_in = _unpermute_in(_sum_chips(landed["in"], "sum_w_in"))
    result["w_in"] = (grad_in[None],) + tuple(
        o[None] for o in _adamw(w_in[0], grad_in, m_w_in[0], v_w_in[0], "adamw_w_in"))

    norm_names = ["ffn1_norm", "mix_norm", "mem_norm", "ffn2_norm"]
    norm_w = _pack_norms(ffn1_norm, mix_norm, mem_norm, ffn2_norm)
    norm_m = _pack_norms(m_ffn1_norm, m_mix_norm, m_mem_norm, m_ffn2_norm)
    norm_v = _pack_norms(v_ffn1_norm, v_mix_norm, v_mem_norm, v_ffn2_norm)
    outs = (norms_sum,) + tuple(_adamw(norm_w, norms_sum, norm_m, norm_v, "adamw_norms"))
    for i, k in enumerate(norm_names):
        result[k] = tuple(o[i:i + 1] for o in outs)

    small_names = ["fox_q_gain", "fox_k_gain", "swa_q_gain", "swa_k_gain", "mem_q_gain", "mem_k_gain",
                   "forget_bias", "swa_sinks"]
    small_m = _pack_small(m_fox_q_gain, m_fox_k_gain, m_swa_q_gain, m_swa_k_gain, m_mem_q_gain, m_mem_k_gain,
                          m_forget_bias, m_swa_sinks)
    small_v = _pack_small(v_fox_q_gain, v_fox_k_gain, v_swa_q_gain, v_swa_k_gain, v_mem_q_gain, v_mem_k_gain,
                          v_forget_bias, v_swa_sinks)
    outs = (small_sum,) + tuple(_adamw(sp, small_sum, small_m, small_v, "adamw_small"))
    for i, k in enumerate(small_names):
        width = N_LOGIT if k in ("forget_bias", "swa_sinks") else HEAD
        result[k] = tuple(o[i:i + 1, :width] for o in outs)

    order = ["ffn1_norm", "ffn1_gate", "ffn1_up", "ffn1_down", "mix_norm", "mem_norm", "w_in", "forget_bias",
             "w_mem_k", "w_mem_v", "fox_q_gain", "fox_k_gain", "swa_q_gain", "swa_k_gain", "swa_sinks",
             "mem_q_gain", "mem_k_gain", "w_out", "ffn2_norm", "ffn2_gate", "ffn2_up", "ffn2_down"]
    result = {k: tuple(v) for k, v in result.items()}
    flat = [loss, grad_x[None]]
    for kind in range(4):
        flat += [result[k][kind] for k in order]
    return tuple(flat)
```

```python
import functools

import jax
import jax.numpy as jnp
from jax import lax
from jax.experimental import pallas as pl
from jax.experimental.pallas import tpu as pltpu

F32 = jnp.float32
BF16 = jnp.bfloat16
MESH = pl.DeviceIdType.MESH
ANY = pl.BlockSpec(memory_space=pl.ANY)

N_DEV = 8
EPS = 1e-6
NEG_INF = -1e30
HEAD = 128
FOX_H, SWA_H, SWA_KV, MEM_H = 6, 6, 2, 4
FOX_W, SWA_W, SWA_KV_W, MEM_W = FOX_H * HEAD, SWA_H * HEAD, SWA_KV * HEAD, MEM_H * HEAD
SCALE = HEAD ** -0.5
SWA_BLOCK = 128
C_FQ, C_FK, C_FV = 0, FOX_W, 2 * FOX_W
C_SQ = 3 * FOX_W
C_SK = C_SQ + SWA_W
C_SV = C_SK + SWA_KV_W
C_MQ = C_SV + SWA_KV_W
C_FL = C_MQ + MEM_W
IN_W = C_FL + HEAD
N_LOGIT = FOX_H
R_FQ, R_FK, R_SQ, R_SK, R_MQ, R_MK, R_FB, R_SINK = range(8)
ADAM_LR, ADAM_B1, ADAM_B2, ADAM_EPS, ADAM_WD, ADAM_STEP = 0.001, 0.9, 0.999, 1e-08, 0.01, 10
VMEM_BYTES = 56 * 1024 * 1024

DN = {
    "nn": (((1,), (0,)), ((), ())),
    "nt": (((1,), (1,)), ((), ())),
    "tn": (((0,), (0,)), ((), ())),
}


def _params(n_axes):
    return pltpu.CompilerParams(dimension_semantics=("arbitrary",) * n_axes, vmem_limit_bytes=VMEM_BYTES)


def _dot(a, b, dims="nn"):
    return lax.dot_general(a.astype(BF16), b.astype(BF16), DN[dims], preferred_element_type=F32)


def _sigmoid(x):
    return 0.5 * jnp.tanh(0.5 * x) + 0.5


def _me():
    return lax.axis_index("x"), lax.axis_index("y"), lax.axis_index("c")


def _lin(p):
    return 4 * p[0] + 2 * p[1] + p[2]


def _rows_tile(rows, row_bytes, budget=4 << 20, mult=16):
    best = None
    for k in range(1, rows + 1):
        if rows % k == 0 and (rows // k) % mult == 0 and (rows // k) * row_bytes <= budget:
            best = rows // k
            break
    assert best is not None, (rows, row_bytes)
    return best


class _Ride:
    def __init__(self, inputs, out_shapes, aliases, n_remote, n_local, start, wait):
        self.inputs, self.out_shapes, self.aliases = list(inputs), list(out_shapes), dict(aliases)
        self.n_remote, self.n_local, self.start, self.wait = n_remote, n_local, start, wait


def _remote(src, dst, send, recv, k, to):
    return pltpu.make_async_remote_copy(src_ref=src, dst_ref=dst, send_sem=send.at[k], recv_sem=recv.at[k],
                                        device_id=to, device_id_type=MESH)


def _other_chips(x, y):
    return [(1 - x, y), (x, 1 - y), (1 - x, 1 - y)]


ALL_CHIPS = [(0, 0), (0, 1), (1, 0), (1, 1)]


def _rows_of(ref, rows, slot=None):
    if slot is None:
        return ref if rows is None else ref.at[pl.ds(rows[0], rows[1])]
    return ref.at[slot] if rows is None else ref.at[slot, pl.ds(rows[0], rows[1])]


def _ride_gather_chips(xs, rows=None, into=None):
    n = len(xs)

    def copies(ins, outs, send, recv):
        x, y, c = _me()
        out = []
        for a in range(n):
            for j, chip in enumerate(_other_chips(x, y)):
                peer = (*chip, c)
                src = _rows_of(ins[a], rows)
                out.append((_remote(src, _rows_of(outs[a], rows, _lin((x, y, c))), send, recv, 3 * a + j, peer),
                            _remote(src, _rows_of(outs[a], rows, _lin(peer)), send, recv, 3 * a + j, peer)))
        return out

    def mine(ins, outs, local):
        me = _lin(_me())
        return [pltpu.make_async_copy(_rows_of(ins[a], rows), _rows_of(outs[a], rows, me), local.at[a])
                for a in range(n)]

    def start(ins, outs, send, recv, local):
        for cp in mine(ins, outs, local):
            cp.start()
        for sent, _ in copies(ins, outs, send, recv):
            sent.start()

    def wait(ins, outs, send, recv, local):
        for sent, landed in copies(ins, outs, send, recv):
            landed.wait_recv()
            sent.wait_send()
        for cp in mine(ins, outs, local):
            cp.wait()

    shapes = [jax.ShapeDtypeStruct((N_DEV,) + x.shape, x.dtype) for x in xs]
    if into is None:
        return _Ride(xs, shapes, {}, 3 * n, n, start, wait)
    return _Ride(list(xs) + list(into), shapes, {n + a: a for a in range(n)}, 3 * n, n, start, wait)


def _ride_gather_sibling(bufs):
    n = len(bufs)

    def copies(outs, send, recv):
        x, y, c = _me()
        out = []
        for a in range(n):
            for q, (px, py) in enumerate(ALL_CHIPS):
                there = outs[a].at[4 * px + 2 * py + c]
                here = outs[a].at[4 * px + 2 * py + 1 - c]
                out.append((_remote(there, there, send, recv, 4 * a + q, (x, y, 1 - c)),
                            _remote(here, here, send, recv, 4 * a + q, (x, y, 1 - c))))
        return out

    def start(ins, outs, send, recv, local):
        for sent, _ in copies(outs, send, recv):
            sent.start()

    def wait(ins, outs, send, recv, local):
        for sent, landed in copies(outs, send, recv):
            landed.wait_recv()
            sent.wait_send()

    shapes = [jax.ShapeDtypeStruct(b.shape, b.dtype) for b in bufs]
    return _Ride(bufs, shapes, {a: a for a in range(n)}, 4 * n, 0, start, wait)


def _ride_gather(xs, mid_frac, rows=None, into=None):
    n = len(xs)
    chips = _ride_gather_chips(xs, rows, into)

    def sibling_copies(outs, send, recv):
        x, y, c = _me()
        out = []
        for a in range(n):
            for q, (px, py) in enumerate(ALL_CHIPS):
                there = _rows_of(outs[a], rows, 4 * px + 2 * py + c)
                here = _rows_of(outs[a], rows, 4 * px + 2 * py + 1 - c)
                k = 3 * n + 4 * a + q
                out.append((_remote(there, there, send, recv, k, (x, y, 1 - c)),
                            _remote(here, here, send, recv, k, (x, y, 1 - c))))
        return out

    def mid(ins, outs, send, recv, local):
        chips.wait(ins, outs, send, recv, local)
        for sent, _ in sibling_copies(outs, send, recv):
            sent.start()

    def wait(ins, outs, send, recv, local):
        for sent, landed in sibling_copies(outs, send, recv):
            landed.wait_recv()
            sent.wait_send()

    ride = _Ride(chips.inputs, chips.out_shapes, chips.aliases, 7 * n, n, chips.start, wait)
    ride.mid, ride.mid_frac = mid, mid_frac
    return ride


def _ride_scatter_sibling(parts):
    n = len(parts)

    def copies(ins, outs, send, recv):
        x, y, c = _me()
        out = []
        for a in range(n):
            for q, (px, py) in enumerate(ALL_CHIPS):
                cp = _remote(ins[a].at[4 * px + 2 * py + 1 - c], outs[a].at[q], send, recv, 4 * a + q, (x, y, 1 - c))
                out.append(cp)
        return out

    def start(ins, outs, send, recv, local):
        for cp in copies(ins, outs, send, recv):
            cp.start()

    def wait(ins, outs, send, recv, local):
        for cp in copies(ins, outs, send, recv):
            cp.wait_recv()
            cp.wait_send()

    shapes = [jax.ShapeDtypeStruct((4,) + p.shape[1:], p.dtype) for p in parts]
    return _Ride(parts, shapes, {}, 4 * n, 0, start, wait)


def _ride_scatter_chips(pairs, rows=None, into=None):
    n = len(pairs)

    def part(ref, slot):
        return _rows_of(ref, rows, slot)

    def copies(ins, outs, send, recv):
        x, y, c = _me()
        out = []
        for a in range(n):
            for j, (px, py) in enumerate(_other_chips(x, y)):
                peer = (px, py, c)
                src = part(ins[a], 2 * px + py)
                out.append((_remote(src, part(outs[a], 2 * x + y), send, recv, 3 * a + j, peer),
                            _remote(src, part(outs[a], 2 * px + py), send, recv, 3 * a + j, peer)))
        return out

    def mine(ins, outs, local):
        x, y, _ = _me()
        return [pltpu.make_async_copy(part(ins[a], 2 * x + y), part(outs[a], 2 * x + y), local.at[a])
                for a in range(n)]

    def start(ins, outs, send, recv, local):
        for cp in mine(ins, outs, local):
            cp.start()
        for sent, _ in copies(ins, outs, send, recv):
            sent.start()

    def wait(ins, outs, send, recv, local):
        for sent, landed in copies(ins, outs, send, recv):
            landed.wait_recv()
            sent.wait_send()
        for cp in mine(ins, outs, local):
            cp.wait()

    shapes = [jax.ShapeDtypeStruct(p.shape, p.dtype) for p in pairs]
    if into is None:
        return _Ride(pairs, shapes, {}, 3 * n, n, start, wait)
    return _Ride(list(pairs) + list(into), shapes, {n + a: a for a in range(n)}, 3 * n, n, start, wait)


def _call(name, body, grid, in_specs, out_specs, out_shape, operands, scratch=(), rides=()):
    n_in, n_out, n_scr = len(operands), len(out_shape), len(scratch)
    ride_in, ride_out, ride_scr, aliases, spans = [], [], [], {}, []
    for r in rides:
        for i, o in r.aliases.items():
            aliases[n_in + len(ride_in) + i] = n_out + len(ride_out) + o
        spans.append((len(ride_in), len(r.inputs), len(ride_out), len(r.out_shapes)))
        ride_in += r.inputs
        ride_out += r.out_shapes
        ride_scr += [pltpu.SemaphoreType.DMA((r.n_remote,)), pltpu.SemaphoreType.DMA((r.n_remote,)),
                     pltpu.SemaphoreType.DMA((max(r.n_local, 1),))]

    def wrapped(*refs):
        c_in, r_in = refs[:n_in], refs[n_in:n_in + len(ride_in)]
        p = n_in + len(ride_in)
        c_out, r_out = refs[p:p + n_out], refs[p + n_out:p + n_out + len(ride_out)]
        p += n_out + len(ride_out)
        c_scr, r_scr = refs[p:p + n_scr], refs[p + n_scr:]

        n_steps = functools.reduce(lambda a, b: a * b, grid, 1)
        step = functools.reduce(lambda acc, ax: acc * grid[ax] + pl.program_id(ax), range(len(grid)), 0)

        def each(method, at):
            for k, (r, (i0, ni, o0, no)) in enumerate(zip(rides, spans)):
                fn = getattr(r, method, None)
                if fn is None:
                    continue
                run = functools.partial(fn, r_in[i0:i0 + ni], r_out[o0:o0 + no], *r_scr[3 * k:3 * k + 3])
                if grid:
                    pl.when(step == at(r))(run)
                else:
                    run()

        each("start", lambda r: 0)
        each("mid", lambda r: int(r.mid_frac * (n_steps - 1)))
        body(*c_in, *c_out, *c_scr)
        each("wait", lambda r: n_steps - 1)

    outs = pl.pallas_call(
        wrapped, grid=grid, in_specs=list(in_specs) + [ANY] * len(ride_in),
        out_specs=list(out_specs) + [ANY] * len(ride_out), out_shape=list(out_shape) + ride_out,
        scratch_shapes=list(scratch) + ride_scr, input_output_aliases=aliases,
        compiler_params=_params(len(grid)), name=name)(*operands, *ride_in)
    outs = list(outs)
    ride_results = [outs[n_out + o0:n_out + o0 + no] for (_, _, o0, no) in spans]
    return outs[:n_out], ride_results


def _only_copies(name, rides):
    return _call(name, lambda: None, (), [], [], [], [], rides=rides)[1]


class _Tail:
    def __init__(self, extra, out_shapes, out_specs, fn):
        self.extra, self.out_shapes, self.out_specs, self.fn = list(extra), list(out_shapes), list(out_specs), fn


def _mm(name, pairs, dims, grid, out_shape, out_spec, res=None, res_spec=None, alpha=1.0, rides=(), tail=None):
    n = len(pairs)
    nk = grid[-1]
    kax = len(grid) - 1
    acc_shape = tuple(d for d in out_spec.block_shape if d is not None)
    n_extra = len(tail.extra) if tail else 0
    n_outs = len(tail.out_shapes) if tail else 1

    def body(*refs):
        pos = 2 * n
        r_ref = None
        if res is not None:
            r_ref = refs[pos]
            pos += 1
        x_refs = refs[pos:pos + n_extra]
        o_refs = refs[pos + n_extra:pos + n_extra + n_outs]
        pos += n_extra + n_outs
        part = None
        for p in range(n):
            d = _dot(refs[2 * p][...], refs[2 * p + 1][...], dims)
            part = d if part is None else part + d

        def finish(acc):
            if alpha != 1.0:
                acc = acc * alpha
            if r_ref is not None:
                acc = r_ref[...] + acc
            if tail:
                tail.fn(acc, x_refs, o_refs)
            else:
                o_refs[0][...] = acc.astype(o_refs[0].dtype)

        if nk == 1:
            finish(part)
        else:
            acc_ref = refs[pos]
            k = pl.program_id(kax)

            @pl.when(k == 0)
            def _():
                acc_ref[...] = part

            @pl.when(k > 0)
            def _():
                acc_ref[...] += part

            @pl.when(k == nk - 1)
            def _():
                finish(acc_ref[...])

    operands, in_specs = [], []
    for a, a_spec, b, b_spec in pairs:
        operands += [a, b]
        in_specs += [a_spec, b_spec]
    if res is not None:
        operands.append(res)
        in_specs.append(res_spec)
    for a, a_spec in (tail.extra if tail else []):
        operands.append(a)
        in_specs.append(a_spec)
    outs, ride_results = _call(name, body, grid, in_specs, tail.out_specs if tail else [out_spec],
                               tail.out_shapes if tail else [out_shape], operands,
                               scratch=[pltpu.VMEM(acc_shape, F32)] if nk > 1 else [], rides=rides)
    outs = outs if tail else outs[0]
    return (outs, ride_results) if rides else outs


def _accumulate(ref, part):
    @pl.when(pl.program_id(0) == 0)
    def _():
        ref[...] = part

    @pl.when(pl.program_id(0) > 0)
    def _():
        ref[...] += part


def _tail_norm(gain, t, d, tm):
    def fn(v, x_refs, o_refs):
        o_refs[0][...] = v
        r = lax.rsqrt(jnp.mean(v * v, axis=-1, keepdims=True) + EPS)
        o_refs[1][...] = (v * r * x_refs[0][...]).astype(BF16)

    rows = pl.BlockSpec((tm, d), lambda i, k: (i, 0))
    return _Tail([(gain, pl.BlockSpec((1, d), lambda i, k: (0, 0)))],
                 [jax.ShapeDtypeStruct((t, d), F32), jax.ShapeDtypeStruct((t, d), BF16)], [rows, rows], fn)


def _tail_loss(target, t, d, tm):
    def fn(v, x_refs, o_refs):
        err = v - x_refs[0][...]
        dy = err * (1.0 / d)
        o_refs[0][...] = dy
        o_refs[1][...] = dy.astype(BF16)
        _accumulate(o_refs[2], jnp.zeros((8, 128), F32) + jnp.sum(err * err))

    rows = pl.BlockSpec((tm, d), lambda i, k: (i, 0))
    return _Tail([(target, rows)],
                 [jax.ShapeDtypeStruct((t, d), F32), jax.ShapeDtypeStruct((t, d), BF16),
                  jax.ShapeDtypeStruct((8, 128), F32)],
                 [rows, rows, pl.BlockSpec((8, 128), lambda i, k: (0, 0))], fn)


def _tail_norm_bwd(x, gain, dres, t, d, tm):
    def fn(dy, x_refs, o_refs):
        xv = x_refs[0][...]
        r = lax.rsqrt(jnp.mean(xv * xv, axis=-1, keepdims=True) + EPS)
        xh = xv * r
        dxh = dy * x_refs[1][...]
        dx = r * (dxh - xh * jnp.mean(dxh * xh, axis=-1, keepdims=True)) + x_refs[2][...]
        o_refs[0][...] = dx
        o_refs[1][...] = dx.astype(BF16)
        _accumulate(o_refs[2], jnp.sum(dy * xh, axis=0, keepdims=True))

    rows = pl.BlockSpec((tm, d), lambda i, k: (i, 0))
    vec = pl.BlockSpec((1, d), lambda i, k: (0, 0))
    return _Tail([(x, rows), (gain, vec), (dres, rows)],
                 [jax.ShapeDtypeStruct((t, d), F32), jax.ShapeDtypeStruct((t, d), BF16),
                  jax.ShapeDtypeStruct((1, d), F32)], [rows, rows, vec], fn)


def _cast_bf16(x, name):
    rows, cols = x.shape
    tm = _rows_tile(rows, cols * 4)

    def body(x_ref, o_ref):
        o_ref[...] = x_ref[...].astype(BF16)

    spec = pl.BlockSpec((tm, cols), lambda i: (i, 0))
    return pl.pallas_call(body, grid=(rows // tm,), in_specs=[spec], out_specs=spec,
                          out_shape=jax.ShapeDtypeStruct(x.shape, BF16), compiler_params=_params(1), name=name)(x)


def _rms_fwd(x, gain, name):
    rows, d = x.shape
    tm = min(rows, 512)

    def body(x_ref, g_ref, o_ref):
        xv = x_ref[...]
        r = lax.rsqrt(jnp.mean(xv * xv, axis=-1, keepdims=True) + EPS)
        o_ref[...] = (xv * r * g_ref[...]).astype(BF16)

    spec = pl.BlockSpec((tm, d), lambda i: (i, 0))
    return pl.pallas_call(body, grid=(rows // tm,), in_specs=[spec, pl.BlockSpec((1, d), lambda i: (0, 0))],
                          out_specs=spec, out_shape=jax.ShapeDtypeStruct(x.shape, BF16),
                          compiler_params=_params(1), name=name)(x, gain)


def _rms_bwd(x, gain, dxn, dres, name, rides=()):
    rows, d = x.shape
    tm = min(rows, 256)
    with_res = dres is not None

    def body(*refs):
        if with_res:
            x_ref, g_ref, dy_ref, r_ref, dx_ref, dxb_ref, dg_ref = refs
        else:
            x_ref, g_ref, dy_ref, dx_ref, dxb_ref, dg_ref = refs
        xv = x_ref[...]
        r = lax.rsqrt(jnp.mean(xv * xv, axis=-1, keepdims=True) + EPS)
        xh = xv * r
        dy = dy_ref[...]
        dxh = dy * g_ref[...]
        dx = r * (dxh - xh * jnp.mean(dxh * xh, axis=-1, keepdims=True))
        if with_res:
            dx = dx + r_ref[...]
        dx_ref[...] = dx
        dxb_ref[...] = dx.astype(BF16)
        part = jnp.sum(dy * xh, axis=0, keepdims=True)

        @pl.when(pl.program_id(0) == 0)
        def _():
            dg_ref[...] = part

        @pl.when(pl.program_id(0) > 0)
        def _():
            dg_ref[...] += part

    spec = pl.BlockSpec((tm, d), lambda i: (i, 0))
    vec = pl.BlockSpec((1, d), lambda i: (0, 0))
    ops = [x, gain, dxn] + ([dres] if with_res else [])
    outs, ride_results = _call(
        name, body, (rows // tm,), [spec, vec, spec] + ([spec] if with_res else []), [spec, spec, vec],
        [jax.ShapeDtypeStruct(x.shape, F32), jax.ShapeDtypeStruct(x.shape, BF16), jax.ShapeDtypeStruct((1, d), F32)],
        ops, rides=rides)
    return (outs, ride_results) if rides else outs


ROW_CHUNK = 256


def _ffn_up(xn, wg, wu, tag, rides=()):
    t, d = xn.shape
    nd, fs, _ = wg.shape
    tm = min(t, 512)
    rc = min(tm, ROW_CHUNK)

    def body(x_ref, wg_ref, wu_ref, a_ref, b_ref, h_ref):
        for r in range(0, tm, rc):
            xv = x_ref[r:r + rc, :]
            g = _dot(xv, wg_ref[...], "nt")
            u = _dot(xv, wu_ref[...], "nt")
            sig = _sigmoid(g)
            silu = g * sig
            a_ref[r:r + rc, :] = (0.5 * u * (sig + silu * (1.0 - sig))).astype(BF16)
            b_ref[r:r + rc, :] = (0.5 * silu).astype(BF16)
            h_ref[r:r + rc, :] = (silu * u).astype(BF16)

    wspec = pl.BlockSpec((None, fs, d), lambda j, i: (j, 0, 0))
    hspec = pl.BlockSpec((None, tm, fs), lambda j, i: (j, i, 0))
    hid = jax.ShapeDtypeStruct((nd, t, fs), BF16)
    return _call(f"{tag}_up", body, (nd, t // tm), [pl.BlockSpec((tm, d), lambda j, i: (i, 0)), wspec, wspec],
                 [hspec] * 3, [hid] * 3, [xn, wg, wu], rides=rides)


CONTRACT_ROWS = 256


def _ffn_contract(hid, w, name, res=None, alpha=1.0, rides=(), tail=None):
    nd, t, fs = hid.shape
    d = w.shape[2]
    tm = min(t, CONTRACT_ROWS)
    xspec = pl.BlockSpec((tm, d), lambda i, k: (i, 0))
    pairs = [(hid, pl.BlockSpec((None, tm, fs), lambda i, k, s=s: (s, i, 0)),
              w, pl.BlockSpec((None, fs, d), lambda i, k, s=s: (s, 0, 0), pipeline_mode=pl.Buffered(1)))
             for s in range(nd)]
    return _mm(name, pairs, "nn", (t // tm, 1), jax.ShapeDtypeStruct((t, d), F32), xspec,
               res=res, res_spec=xspec if res is not None else None, alpha=alpha, rides=rides, tail=tail)


def _ffn_down(x, h, wd, tag, rides=(), tail=None):
    return _ffn_contract(h, wd, f"{tag}_down", res=x, alpha=0.5, rides=rides, tail=tail)


def _ffn_dact(dyb, wd, a, b, tag, rides=()):
    nd, t, fs = a.shape
    d = dyb.shape[1]
    tm = min(t, 512)
    rc = min(tm, ROW_CHUNK)

    def body(dy_ref, wd_ref, a_ref, b_ref, dg_ref, du_ref):
        for r in range(0, tm, rc):
            dh = _dot(dy_ref[r:r + rc, :], wd_ref[...], "nt")
            dg_ref[r:r + rc, :] = (dh * a_ref[r:r + rc, :].astype(F32)).astype(BF16)
            du_ref[r:r + rc, :] = (dh * b_ref[r:r + rc, :].astype(F32)).astype(BF16)

    hspec = pl.BlockSpec((None, tm, fs), lambda j, i: (j, i, 0))
    hid = jax.ShapeDtypeStruct((nd, t, fs), BF16)
    return _call(f"{tag}_dact", body, (nd, t // tm),
                 [pl.BlockSpec((tm, d), lambda j, i: (i, 0)), pl.BlockSpec((None, fs, d), lambda j, i: (j, 0, 0)),
                  hspec, hspec], [hspec] * 2, [hid] * 2, [dyb, wd, a, b], rides=rides)


def _ffn_dw(hid, act, alpha, name, rides=()):
    nd, t, fs = hid.shape
    d = act.shape[1]
    tk = t
    return _mm(name, [(hid, pl.BlockSpec((None, tk, fs), lambda j, k: (j, k, 0)),
                       act, pl.BlockSpec((tk, d), lambda j, k: (k, 0), pipeline_mode=pl.Buffered(1)))],
               "tn", (nd, t // tk),
               jax.ShapeDtypeStruct((nd, fs, d), BF16), pl.BlockSpec((None, fs, d), lambda j, k: (j, 0, 0)),
               alpha=alpha, rides=rides)


def _head_norm(x, gain):
    r = lax.rsqrt(jnp.mean(x * x, axis=-1, keepdims=True) + EPS)
    return x * r * gain


def _head_norm_bwd(x, gain, dy):
    r = lax.rsqrt(jnp.mean(x * x, axis=-1, keepdims=True) + EPS)
    xh = x * r
    dxh = dy * gain
    dx = r * (dxh - xh * jnp.mean(dxh * xh, axis=-1, keepdims=True))
    return dx, jnp.sum(dy * xh, axis=0, keepdims=True)


def _hs(h, base=0):
    return slice(base + h * HEAD, base + (h + 1) * HEAD)


def _tri(n, lower):
    r = lax.broadcasted_iota(jnp.int32, (n, n), 0)
    c = lax.broadcasted_iota(jnp.int32, (n, n), 1)
    return ((r >= c) if lower else (r <= c)).astype(F32)


def _attn_pre(proj, sp, name):
    t = proj.shape[0]
    tm = min(t, 256)

    def body(p_ref, sp_ref, fq, fk, fv, sq, sk, sv, mq, cc, carry):
        @pl.when(pl.program_id(0) == 0)
        def _():
            carry[...] = jnp.zeros_like(carry)

        for h in range(FOX_H):
            fq[:, _hs(h)] = _head_norm(p_ref[:, _hs(h, C_FQ)], sp_ref[R_FQ:R_FQ + 1, :]).astype(BF16)
            fk[:, _hs(h)] = _head_norm(p_ref[:, _hs(h, C_FK)], sp_ref[R_FK:R_FK + 1, :]).astype(BF16)
        fv[...] = p_ref[:, C_FV:C_FV + FOX_W].astype(BF16)
        for h in range(SWA_H):
            sq[:, _hs(h)] = _head_norm(p_ref[:, _hs(h, C_SQ)], sp_ref[R_SQ:R_SQ + 1, :]).astype(BF16)
        for h in range(SWA_KV):
            sk[:, _hs(h)] = _head_norm(p_ref[:, _hs(h, C_SK)], sp_ref[R_SK:R_SK + 1, :]).astype(BF16)
        sv[...] = p_ref[:, C_SV:C_SV + SWA_KV_W].astype(BF16)
        for h in range(MEM_H):
            mq[:, _hs(h)] = _head_norm(p_ref[:, _hs(h, C_MQ)], sp_ref[R_MQ:R_MQ + 1, :]).astype(BF16)
        z = p_ref[:, C_FL:C_FL + HEAD] + sp_ref[R_FB:R_FB + 1, :]
        lane = lax.broadcasted_iota(jnp.int32, z.shape, 1)
        log_f = jnp.minimum(z, 0.0) - jnp.log(1.0 + jnp.exp(-jnp.abs(z)))
        log_f = jnp.where(lane < N_LOGIT, log_f, 0.0)
        c = jnp.dot(_tri(tm, True), log_f, precision=lax.Precision.HIGHEST, preferred_element_type=F32)
        c = c + carry[0:1, :]
        cc[...] = c
        carry[...] = jnp.broadcast_to(c[tm - 1:tm, :], carry.shape)

    def rows(w):
        return pl.BlockSpec((tm, w), lambda i: (i, 0))

    def shape(w, dt):
        return jax.ShapeDtypeStruct((t, w), dt)

    widths = [FOX_W, FOX_W, FOX_W, SWA_W, SWA_KV_W, SWA_KV_W, MEM_W]
    return pl.pallas_call(
        body, grid=(t // tm,), in_specs=[rows(IN_W), pl.BlockSpec((16, 128), lambda i: (0, 0))],
        out_specs=[rows(w) for w in widths] + [rows(HEAD)],
        out_shape=[shape(w, BF16) for w in widths] + [shape(HEAD, F32)],
        scratch_shapes=[pltpu.VMEM((8, 128), F32)], compiler_params=_params(1), name=name)(proj, sp)


def _attn_post_bwd(proj, sp, dfq, dfk, dfv, dsq, dsk, dsv, dmq, dc_col, dc_row_t, name):
    t = proj.shape[0]
    tm = min(t, 256)
    nb = t // tm

    def body(p_ref, sp_ref, dfq_r, dfk_r, dfv_r, dsq_r, dsk_r, dsv_r, dmq_r, dcc_r, dcr_r, dp_ref, dsp_ref, carry):
        @pl.when(pl.program_id(0) == 0)
        def _():
            carry[...] = jnp.zeros_like(carry)
            dsp_ref[...] = jnp.zeros_like(dsp_ref)

        def group(n_heads, col, row, d_ref):
            total = None
            for h in range(n_heads):
                dx, dg = _head_norm_bwd(p_ref[:, _hs(h, col)], sp_ref[row:row + 1, :], d_ref[:, _hs(h)])
                dp_ref[:, _hs(h, col)] = dx.astype(BF16)
                total = dg if total is None else total + dg
            dsp_ref[row:row + 1, :] += total

        group(FOX_H, C_FQ, R_FQ, dfq_r)
        group(FOX_H, C_FK, R_FK, dfk_r)
        dp_ref[:, C_FV:C_FV + FOX_W] = dfv_r[...].astype(BF16)
        group(SWA_H, C_SQ, R_SQ, dsq_r)
        group(SWA_KV, C_SK, R_SK, dsk_r)
        dp_ref[:, C_SV:C_SV + SWA_KV_W] = dsv_r[...].astype(BF16)
        group(MEM_H, C_MQ, R_MQ, dmq_r)
        dc = dcc_r[...] - dcr_r[...]
        rc = jnp.dot(_tri(tm, False), dc, precision=lax.Precision.HIGHEST, preferred_element_type=F32)
        rc = rc + carry[0:1, :]
        carry[...] = jnp.broadcast_to(rc[0:1, :], carry.shape)
        z = p_ref[:, C_FL:C_FL + HEAD] + sp_ref[R_FB:R_FB + 1, :]
        dz = rc * _sigmoid(-z)
        dp_ref[:, C_FL:C_FL + HEAD] = dz.astype(BF16)
        dsp_ref[R_FB:R_FB + 1, :] += jnp.sum(dz, axis=0, keepdims=True)

    def rows(w):
        return pl.BlockSpec((tm, w), lambda i: (nb - 1 - i, 0))

    small = pl.BlockSpec((16, 128), lambda i: (0, 0))
    widths = [FOX_W, FOX_W, FOX_W, SWA_W, SWA_KV_W, SWA_KV_W, MEM_W, HEAD, HEAD]
    return pl.pallas_call(
        body, grid=(nb,), in_specs=[rows(IN_W), small] + [rows(w) for w in widths],
        out_specs=[rows(IN_W), small],
        out_shape=[jax.ShapeDtypeStruct((t, IN_W), BF16), jax.ShapeDtypeStruct((16, 128), F32)],
        scratch_shapes=[pltpu.VMEM((8, 128), F32)], compiler_params=_params(1), name=name,
    )(proj, sp, dfq, dfk, dfv, dsq, dsk, dsv, dmq, dc_col, dc_row_t)


def _head_column(values):
    rows = values[0].shape[0]
    lane = lax.broadcasted_iota(jnp.int32, (rows, HEAD), 1)
    out = jnp.zeros((rows, HEAD), F32)
    for h, v in enumerate(values):
        out = jnp.where(lane == h, v, out)
    return out


def _head_row(values, n_rows=8):
    cols = values[0].shape[1]
    sub = lax.broadcasted_iota(jnp.int32, (n_rows, cols), 0)
    out = jnp.zeros((n_rows, cols), F32)
    for h, v in enumerate(values):
        out = jnp.where(sub == h, v, out)
    return out


def _delta(dmixed, o_a, o_b, o_c, name):
    t = dmixed.shape[0]
    tm = min(t, 512)

    def body(d_ref, a_ref, b_ref, c_ref, o_ref, rep_ref):
        cols = []
        for ref, n_heads, base in ((a_ref, FOX_H, 0), (b_ref, SWA_H, FOX_W), (c_ref, MEM_H, FOX_W + SWA_W)):
            for h in range(n_heads):
                cols.append(jnp.sum(d_ref[:, _hs(h, base)] * ref[:, _hs(h)], axis=-1, keepdims=True))
        o_ref[...] = _head_column(cols)
        for h in range(FOX_H):
            rep_ref[h] = jnp.broadcast_to(cols[h], (tm, HEAD))

    def rows(w):
        return pl.BlockSpec((tm, w), lambda i: (i, 0))

    return pl.pallas_call(body, grid=(t // tm,), in_specs=[rows(dmixed.shape[1]), rows(FOX_W), rows(SWA_W), rows(MEM_W)],
                          out_specs=[rows(HEAD), pl.BlockSpec((FOX_H, tm, HEAD), lambda i: (0, i, 0))],
                          out_shape=[jax.ShapeDtypeStruct((t, HEAD), F32), jax.ShapeDtypeStruct((FOX_H, t, HEAD), F32)],
                          compiler_params=_params(1), name=name)(dmixed, o_a, o_b, o_c)


def _fox_fwd(fq, fk, fv, c_rep, c_row, name, rides=()):
    t = fq.shape[0]
    tb = min(t, 512)
    nb = t // tb
    n_tiles = tb // HEAD

    def body(q_ref, k_ref, v_ref, cc_ref, cr_ref, o_ref, lse_ref, m_s, l_s, acc_s):
        qi, ki = pl.program_id(0), pl.program_id(1)

        @pl.when(ki == 0)
        def _():
            m_s[...] = jnp.full_like(m_s, NEG_INF)
            l_s[...] = jnp.zeros_like(l_s)
            acc_s[...] = jnp.zeros_like(acc_s)

        def step(diagonal):
            if diagonal:
                r = lax.broadcasted_iota(jnp.int32, (tb, HEAD), 0)
                c = lax.broadcasted_iota(jnp.int32, (tb, HEAD), 1)
            for h in range(FOX_H):
                s = _dot(q_ref[:, _hs(h)], k_ref[:, _hs(h)], "nt")
                cc = cc_ref[h]
                tiles, m_cur = [], None
                for j in range(n_tiles):
                    st = s[:, _hs(j)] * SCALE + cc - cr_ref[h:h + 1, _hs(j)]
                    if diagonal:
                        st = jnp.where(r >= c + j * HEAD, st, NEG_INF)
                    tiles.append(st)
                    m_cur = st if m_cur is None else jnp.maximum(m_cur, st)
                m_prev = m_s[h]
                m_new = jnp.maximum(m_prev, jnp.max(m_cur, axis=-1, keepdims=True))
                alpha = jnp.exp(m_prev - m_new)
                ps = [jnp.exp(st - m_new) for st in tiles]
                l_cur = ps[0]
                for p in ps[1:]:
                    l_cur = l_cur + p
                l_s[h] = alpha * l_s[h] + jnp.sum(l_cur, axis=-1, keepdims=True)
                p = jnp.concatenate([p.astype(BF16) for p in ps], axis=1)
                acc_s[:, _hs(h)] = alpha * acc_s[:, _hs(h)] + _dot(p, v_ref[:, _hs(h)])
                m_s[h] = m_new

        @pl.when(ki < qi)
        def _():
            step(False)

        @pl.when(ki == qi)
        def _():
            step(True)
            for h in range(FOX_H):
                o_ref[:, _hs(h)] = acc_s[:, _hs(h)] / l_s[h]
                lse_ref[h] = m_s[h] + jnp.log(l_s[h])

    qspec = pl.BlockSpec((tb, FOX_W), lambda i, j: (i, 0))
    kspec = pl.BlockSpec((tb, FOX_W), lambda i, j: (jnp.minimum(i, j), 0))
    rep = pl.BlockSpec((FOX_H, tb, HEAD), lambda i, j: (0, i, 0))
    return _call(
        name, body, (nb, nb),
        [qspec, kspec, kspec, rep, pl.BlockSpec((8, tb), lambda i, j: (0, jnp.minimum(i, j)))],
        [qspec, rep],
        [jax.ShapeDtypeStruct((t, FOX_W), F32), jax.ShapeDtypeStruct((FOX_H, t, HEAD), F32)],
        [fq, fk, fv, c_rep, c_row],
        scratch=[pltpu.VMEM((FOX_H, tb, HEAD), F32), pltpu.VMEM((FOX_H, tb, HEAD), F32), pltpu.VMEM((tb, FOX_W), F32)],
        rides=rides)


def _fox_bwd(fq, fk, fv, c_rep, c_row, dmixed, lse, delta, name, rides=()):
    t = fq.shape[0]
    tb = min(t, 512)
    nb = t // tb
    n_tiles = tb // HEAD

    def body(q_ref, k_ref, v_ref, cc_ref, cr_ref, do_ref, lse_ref, dl_ref,
             dq_ref, dk_ref, dv_ref, dcc_ref, dcr_ref):
        ki, qi = pl.program_id(0), pl.program_id(1)

        @pl.when((ki == 0) & (qi == 0))
        def _():
            dq_ref[...] = jnp.zeros_like(dq_ref)
            dcc_ref[...] = jnp.zeros_like(dcc_ref)

        @pl.when(qi == 0)
        def _():
            dk_ref[...] = jnp.zeros_like(dk_ref)
            dv_ref[...] = jnp.zeros_like(dv_ref)
            dcr_ref[...] = jnp.zeros_like(dcr_ref)

        def step(diagonal):
            rows = pl.ds(pl.multiple_of(qi * tb, tb), tb)
            if diagonal:
                r = lax.broadcasted_iota(jnp.int32, (tb, HEAD), 0)
                c = lax.broadcasted_iota(jnp.int32, (tb, HEAD), 1)
            row_sums, col_sums = [], []
            for h in range(FOX_H):
                q, k, v, do = q_ref[:, _hs(h)], k_ref[:, _hs(h)], v_ref[:, _hs(h)], do_ref[:, _hs(h)]
                s = _dot(q, k, "nt")
                dp = _dot(do, v, "nt")
                cc, lse_h, dl_h = cc_ref[h], lse_ref[h], dl_ref[h]
                ps, dss, row = [], [], None
                for j in range(n_tiles):
                    st = s[:, _hs(j)] * SCALE + cc - cr_ref[h:h + 1, _hs(j)]
                    if diagonal:
                        st = jnp.where(r >= c + j * HEAD, st, NEG_INF)
                    pt = jnp.exp(st - lse_h)
                    dst = pt * (dp[:, _hs(j)] - dl_h)
                    ps.append(pt.astype(BF16))
                    dss.append(dst)
                    row = dst if row is None else row + dst
                p = jnp.concatenate(ps, axis=1)
                ds = jnp.concatenate(dss, axis=1)
                dsb = ds.astype(BF16)
                dv_ref[:, _hs(h)] += _dot(p, do, "tn")
                dk_ref[:, _hs(h)] += _dot(dsb, q, "tn") * SCALE
                dq_ref[rows, _hs(h)] += _dot(dsb, k) * SCALE
                row_sums.append(jnp.sum(row, axis=1, keepdims=True))
                col_sums.append(jnp.sum(ds, axis=0, keepdims=True))
            dcc_ref[rows, :] += _head_column(row_sums)
            dcr_ref[...] += _head_row(col_sums)

        @pl.when(qi > ki)
        def _():
            step(False)

        @pl.when(qi == ki)
        def _():
            step(True)

    def qmap(j, i):
        return (jnp.maximum(i, j), 0)

    qspec = pl.BlockSpec((tb, FOX_W), qmap)
    kspec = pl.BlockSpec((tb, FOX_W), lambda j, i: (j, 0))
    rep = pl.BlockSpec((FOX_H, tb, HEAD), lambda j, i: (0, jnp.maximum(i, j), 0))
    rowspec = pl.BlockSpec((8, tb), lambda j, i: (0, j))
    return _call(
        name, body, (nb, nb), [qspec, kspec, kspec, rep, rowspec, qspec, rep, rep],
        [pl.BlockSpec((t, FOX_W), lambda j, i: (0, 0)), kspec, kspec,
         pl.BlockSpec((t, HEAD), lambda j, i: (0, 0)), rowspec],
        [jax.ShapeDtypeStruct((t, FOX_W), F32)] * 3 + [jax.ShapeDtypeStruct((t, HEAD), F32),
                                                       jax.ShapeDtypeStruct((8, t), F32)],
        [fq, fk, fv, c_rep, c_row, dmixed, lse, delta], rides=rides)


def _swa_logits(q, k_cur, k_prev, slope, first_block):
    w = SWA_BLOCK
    r = lax.broadcasted_iota(jnp.int32, (w, w), 0)
    j = lax.broadcasted_iota(jnp.int32, (w, w), 1)
    dist_cur = r - j
    dist_prev = w + r - j
    s_cur = _dot(q, k_cur, "nt") * SCALE - slope * dist_cur.astype(F32)
    s_cur = jnp.where(dist_cur >= 0, s_cur, NEG_INF)
    s_prev = _dot(q, k_prev, "nt") * SCALE - slope * dist_prev.astype(F32)
    s_prev = jnp.where((j > r) & jnp.logical_not(first_block), s_prev, NEG_INF)
    return s_cur, s_prev


def _slope(h):
    return float(2.0 ** (-8.0 * (h + 1) / SWA_H))


def _swa_fwd(sq, sk, sv, sp, name, rides=()):
    t = sq.shape[0]
    w = SWA_BLOCK
    nb = t // w
    group = SWA_H // SWA_KV

    def body(q_ref, kp_ref, kc_ref, vp_ref, vc_ref, sp_ref, o_ref, lse_ref):
        first = pl.program_id(0) == 0
        lses = []
        for h in range(SWA_H):
            kv = h // group
            s_cur, s_prev = _swa_logits(q_ref[:, _hs(h)], kc_ref[:, _hs(kv)], kp_ref[:, _hs(kv)], _slope(h), first)
            sink = sp_ref[R_SINK:R_SINK + 1, h:h + 1]
            m = jnp.maximum(jnp.maximum(jnp.max(s_cur, axis=-1, keepdims=True),
                                        jnp.max(s_prev, axis=-1, keepdims=True)), sink)
            p_cur = jnp.exp(s_cur - m)
            p_prev = jnp.exp(s_prev - m)
            l = jnp.sum(p_cur, axis=-1, keepdims=True) + jnp.sum(p_prev, axis=-1, keepdims=True) + jnp.exp(sink - m)
            o_ref[:, _hs(h)] = (_dot(p_cur, vc_ref[:, _hs(kv)]) + _dot(p_prev, vp_ref[:, _hs(kv)])) / l
            lses.append(m + jnp.log(l))
        lse_ref[...] = _head_column(lses)

    qspec = pl.BlockSpec((w, SWA_W), lambda n: (n, 0))
    cur = pl.BlockSpec((w, SWA_KV_W), lambda n: (n, 0))
    prev = pl.BlockSpec((w, SWA_KV_W), lambda n: (jnp.maximum(n - 1, 0), 0))
    return _call(
        name, body, (nb,), [qspec, prev, cur, prev, cur, pl.BlockSpec((16, 128), lambda n: (0, 0))],
        [qspec, pl.BlockSpec((w, HEAD), lambda n: (n, 0))],
        [jax.ShapeDtypeStruct((t, SWA_W), F32), jax.ShapeDtypeStruct((t, HEAD), F32)],
        [sq, sk, sk, sv, sv, sp], rides=rides)


def _swa_bwd(sq, sk, sv, sp, dmixed, lse, delta, name):
    t = sq.shape[0]
    w = SWA_BLOCK
    nb = t // w
    group = SWA_H // SWA_KV
    do_block = FOX_W // SWA_W
    assert FOX_W % SWA_W == 0

    def body(q_ref, kp_ref, kc_ref, vp_ref, vc_ref, sp_ref, do_ref, lse_ref, dl_ref,
             dq_ref, dk_ref, dv_ref, dsp_ref, ck, cv):
        step = pl.program_id(0)
        first = step == nb - 1

        @pl.when(step == 0)
        def _():
            ck[...] = jnp.zeros_like(ck)
            cv[...] = jnp.zeros_like(cv)
            dsp_ref[...] = jnp.zeros_like(dsp_ref)

        dk_cur = [None] * SWA_KV
        dk_prev = [None] * SWA_KV
        dv_cur = [None] * SWA_KV
        dv_prev = [None] * SWA_KV
        dsinks = []

        def add(lst, i, v):
            lst[i] = v if lst[i] is None else lst[i] + v

        for h in range(SWA_H):
            kv = h // group
            q, do = q_ref[:, _hs(h)], do_ref[:, _hs(h)]
            kc, kp, vc, vp = kc_ref[:, _hs(kv)], kp_ref[:, _hs(kv)], vc_ref[:, _hs(kv)], vp_ref[:, _hs(kv)]
            s_cur, s_prev = _swa_logits(q, kc, kp, _slope(h), first)
            lse_h = lse_ref[:, h:h + 1]
            dl_h = dl_ref[:, FOX_H + h:FOX_H + h + 1]
            p_cur = jnp.exp(s_cur - lse_h)
            p_prev = jnp.exp(s_prev - lse_h)
            p_sink = jnp.exp(sp_ref[R_SINK:R_SINK + 1, h:h + 1] - lse_h)
            ds_cur = p_cur * (_dot(do, vc, "nt") - dl_h)
            ds_prev = p_prev * (_dot(do, vp, "nt") - dl_h)
            dq_ref[:, _hs(h)] = (_dot(ds_cur, kc) + _dot(ds_prev, kp)) * SCALE
            add(dk_cur, kv, _dot(ds_cur, q, "tn") * SCALE)
            add(dk_prev, kv, _dot(ds_prev, q, "tn") * SCALE)
            add(dv_cur, kv, _dot(p_cur, do, "tn"))
            add(dv_prev, kv, _dot(p_prev, do, "tn"))
            dsinks.append(-jnp.sum(p_sink * dl_h, axis=0, keepdims=True))
        for kv in range(SWA_KV):
            dk_ref[:, _hs(kv)] = dk_cur[kv] + ck[:, _hs(kv)]
            dv_ref[:, _hs(kv)] = dv_cur[kv] + cv[:, _hs(kv)]
            ck[:, _hs(kv)] = dk_prev[kv]
            cv[:, _hs(kv)] = dv_prev[kv]
        lane = lax.broadcasted_iota(jnp.int32, (1, HEAD), 1)
        row = jnp.zeros((1, HEAD), F32)
        for h in range(SWA_H):
            row = jnp.where(lane == h, dsinks[h], row)
        dsp_ref[R_SINK:R_SINK + 1, :] += row

    def rev(n):
        return nb - 1 - n

    qspec = pl.BlockSpec((w, SWA_W), lambda n: (rev(n), 0))
    cur = pl.BlockSpec((w, SWA_KV_W), lambda n: (rev(n), 0))
    prev = pl.BlockSpec((w, SWA_KV_W), lambda n: (jnp.maximum(rev(n) - 1, 0), 0))
    col = pl.BlockSpec((w, HEAD), lambda n: (rev(n), 0))
    small = pl.BlockSpec((16, 128), lambda n: (0, 0))
    return pl.pallas_call(
        body, grid=(nb,),
        in_specs=[qspec, prev, cur, prev, cur, small, pl.BlockSpec((w, SWA_W), lambda n: (rev(n), do_block)), col, col],
        out_specs=[qspec, cur, cur, small],
        out_shape=[jax.ShapeDtypeStruct((t, SWA_W), F32), jax.ShapeDtypeStruct((t, SWA_KV_W), F32),
                   jax.ShapeDtypeStruct((t, SWA_KV_W), F32), jax.ShapeDtypeStruct((16, 128), F32)],
        scratch_shapes=[pltpu.VMEM((w, SWA_KV_W), F32), pltpu.VMEM((w, SWA_KV_W), F32)],
        compiler_params=_params(1), name=name)(sq, sk, sk, sv, sv, sp, dmixed, lse, delta)


def _mem_pre(mkv, sp, name):
    m = mkv.shape[0]

    def body(x_ref, sp_ref, k_ref, v_ref):
        for h in range(MEM_H):
            k_ref[:, _hs(h)] = _head_norm(x_ref[:, _hs(h)], sp_ref[R_MK:R_MK + 1, :]).astype(BF16)
        v_ref[...] = x_ref[:, MEM_W:2 * MEM_W].astype(BF16)

    out = jax.ShapeDtypeStruct((m, MEM_W), BF16)
    return pl.pallas_call(body, out_shape=[out, out], name=name)(mkv, sp)


def _mem_post_bwd(mkv, sp, dmk, dmv, name):
    m = mkv.shape[0]

    def body(x_ref, sp_ref, dk_ref, dv_ref, d_ref, dsp_ref):
        dsp_ref[...] = jnp.zeros_like(dsp_ref)
        total = None
        for h in range(MEM_H):
            dx, dg = _head_norm_bwd(x_ref[:, _hs(h)], sp_ref[R_MK:R_MK + 1, :], dk_ref[:, _hs(h)])
            d_ref[:, _hs(h)] = dx.astype(BF16)
            total = dg if total is None else total + dg
        d_ref[:, MEM_W:2 * MEM_W] = dv_ref[...].astype(BF16)
        dsp_ref[R_MK:R_MK + 1, :] = total

    return pl.pallas_call(body, out_shape=[jax.ShapeDtypeStruct((m, 2 * MEM_W), BF16),
                                           jax.ShapeDtypeStruct((16, 128), F32)], name=name)(mkv, sp, dmk, dmv)


def _mem_fwd(mq, mk, mv, name):
    t = mq.shape[0]
    m = mk.shape[0]
    tq = min(t, 512)

    def body(q_ref, k_ref, v_ref, o_ref, lse_ref):
        lses = []
        for h in range(MEM_H):
            s = _dot(q_ref[:, _hs(h)], k_ref[:, _hs(h)], "nt") * SCALE
            mx = jnp.max(s, axis=-1, keepdims=True)
            p = jnp.exp(s - mx)
            l = jnp.sum(p, axis=-1, keepdims=True)
            o_ref[:, _hs(h)] = _dot(p, v_ref[:, _hs(h)]) / l
            lses.append(mx + jnp.log(l))
        lse_ref[...] = _head_column(lses)

    qspec = pl.BlockSpec((tq, MEM_W), lambda i: (i, 0))
    kspec = pl.BlockSpec((m, MEM_W), lambda i: (0, 0))
    return pl.pallas_call(
        body, grid=(t // tq,), in_specs=[qspec, kspec, kspec],
        out_specs=[qspec, pl.BlockSpec((tq, HEAD), lambda i: (i, 0))],
        out_shape=[jax.ShapeDtypeStruct((t, MEM_W), F32), jax.ShapeDtypeStruct((t, HEAD), F32)],
        compiler_params=_params(1), name=name)(mq, mk, mv)


def _mem_bwd(mq, mk, mv, dmixed, lse, delta, name):
    t = mq.shape[0]
    m = mk.shape[0]
    tq = min(t, 512)
    do_block = (FOX_W + SWA_W) // MEM_W
    assert (FOX_W + SWA_W) % MEM_W == 0

    def body(q_ref, k_ref, v_ref, do_ref, lse_ref, dl_ref, dq_ref, dk_ref, dv_ref):
        @pl.when(pl.program_id(0) == 0)
        def _():
            dk_ref[...] = jnp.zeros_like(dk_ref)
            dv_ref[...] = jnp.zeros_like(dv_ref)

        for h in range(MEM_H):
            q, k, v, do = q_ref[:, _hs(h)], k_ref[:, _hs(h)], v_ref[:, _hs(h)], do_ref[:, _hs(h)]
            s = _dot(q, k, "nt") * SCALE
            p = jnp.exp(s - lse_ref[:, h:h + 1])
            col = FOX_H + SWA_H + h
            ds = p * (_dot(do, v, "nt") - dl_ref[:, col:col + 1])
            dq_ref[:, _hs(h)] = _dot(ds, k) * SCALE
            dk_ref[:, _hs(h)] += _dot(ds, q, "tn") * SCALE
            dv_ref[:, _hs(h)] += _dot(p, do, "tn")

    qspec = pl.BlockSpec((tq, MEM_W), lambda i: (i, 0))
    kspec = pl.BlockSpec((m, MEM_W), lambda i: (0, 0))
    col = pl.BlockSpec((tq, HEAD), lambda i: (i, 0))
    return pl.pallas_call(
        body, grid=(t // tq,),
        in_specs=[qspec, kspec, kspec, pl.BlockSpec((tq, MEM_W), lambda i: (i, do_block)), col, col],
        out_specs=[qspec, kspec, kspec],
        out_shape=[jax.ShapeDtypeStruct((t, MEM_W), F32), jax.ShapeDtypeStruct((m, MEM_W), F32),
                   jax.ShapeDtypeStruct((m, MEM_W), F32)],
        compiler_params=_params(1), name=name)(mq, mk, mv, dmixed, lse, delta)


def _all_gather(xs, name):
    n = len(xs)

    def body(*refs):
        x_refs, o_refs = refs[:n], refs[n:2 * n]
        send_sems, recv_sems, local_sems = refs[2 * n:]
        x, y, c = _me()
        me, sibling = (x, y, c), (x, y, 1 - c)
        x_nb, y_nb, diag = (1 - x, y, c), (x, 1 - y, c), (1 - x, 1 - y, c)
        relay_from = (x + (1 - c) * (1 - 2 * x), y + c * (1 - 2 * y), c)
        relay_to = (x + c * (1 - 2 * x), y + (1 - c) * (1 - 2 * y), c)

        def copy(a, k, block, to, src=None):
            slot = o_refs[a].at[_lin(block)]
            return pltpu.make_async_remote_copy(
                src_ref=slot if src is None else src, dst_ref=slot, send_sem=send_sems.at[a, k],
                recv_sem=recv_sems.at[a, k], device_id=to, device_id_type=MESH)

        mine = [pltpu.make_async_copy(x_refs[a], o_refs[a].at[_lin(me)], local_sems.at[a]) for a in range(n)]
        for cp in mine:
            cp.start()
        sent = []
        for a in range(n):
            sent += [copy(a, 0, me, sibling, src=x_refs[a]), copy(a, 1, me, x_nb, src=x_refs[a]),
                     copy(a, 2, me, y_nb, src=x_refs[a])]
        for cp in sent:
            cp.start()

        def pass_on(cp):
            cp.start()
            sent.append(cp)

        for a in range(n):
            copy(a, 1, x_nb, me).wait_recv()
            copy(a, 2, y_nb, me).wait_recv()
            pass_on(copy(a, 3, relay_from, relay_to))
            pass_on(copy(a, 4, x_nb, sibling))
            pass_on(copy(a, 5, y_nb, sibling))
        for a in range(n):
            copy(a, 3, diag, me).wait_recv()
            pass_on(copy(a, 6, diag, sibling))
        for a in range(n):
            copy(a, 0, sibling, me).wait_recv()
            for k, chip in ((4, (1 - x, y)), (5, (x, 1 - y)), (6, (1 - x, 1 - y))):
                copy(a, k, (*chip, 1 - c), me).wait_recv()
        for cp in sent:
            cp.wait_send()
        for cp in mine:
            cp.wait()

    return pl.pallas_call(
        body, in_specs=[ANY] * n, out_specs=[ANY] * n,
        out_shape=[jax.ShapeDtypeStruct((N_DEV,) + x.shape, x.dtype) for x in xs],
        scratch_shapes=[pltpu.SemaphoreType.DMA((n, 7)), pltpu.SemaphoreType.DMA((n, 7)),
                        pltpu.SemaphoreType.DMA((n,))],
        name=name)(*xs)


def _peers():
    x, y, c = _me()
    out = []
    for k in range(1, N_DEV):
        kx, ky, kc = (k >> 2) & 1, (k >> 1) & 1, k & 1
        out.append(((1 - x) if kx else x, (1 - y) if ky else y, (1 - c) if kc else c))
    return out


def _all_reduce_small(xs, name):
    n = len(xs)

    def body(*refs):
        x_refs, o_refs = refs[:n], refs[n:2 * n]
        bufs = refs[2 * n:3 * n]
        send_sems, recv_sems = refs[3 * n:]
        me = _lin(_me())
        peers = _peers()
        for a in range(n):
            bufs[a][me] = x_refs[a][...]
        sends = []
        for a in range(n):
            for k, peer in enumerate(peers):
                sends.append(pltpu.make_async_remote_copy(
                    src_ref=bufs[a].at[me], dst_ref=bufs[a].at[me], send_sem=send_sems.at[a, k],
                    recv_sem=recv_sems.at[a, k], device_id=peer, device_id_type=MESH))
        for cp in sends:
            cp.start()
        for a in range(n):
            for k, peer in enumerate(peers):
                pltpu.make_async_remote_copy(
                    src_ref=bufs[a].at[me], dst_ref=bufs[a].at[_lin(peer)], send_sem=send_sems.at[a, k],
                    recv_sem=recv_sems.at[a, k], device_id=peer, device_id_type=MESH).wait_recv()
        for cp in sends:
            cp.wait_send()
        for a in range(n):
            total = bufs[a][0]
            for q in range(1, N_DEV):
                total = total + bufs[a][q]
            o_refs[a][...] = total

    vmem = pl.BlockSpec(memory_space=pltpu.VMEM)
    return pl.pallas_call(
        body, in_specs=[vmem] * n, out_specs=[vmem] * n,
        out_shape=[jax.ShapeDtypeStruct(x.shape, F32) for x in xs],
        scratch_shapes=[pltpu.VMEM((N_DEV,) + x.shape, F32) for x in xs]
        + [pltpu.SemaphoreType.DMA((n, 7)), pltpu.SemaphoreType.DMA((n, 7))],
        name=name)(*xs)


def _pair_add(part, got, name):
    _, rows, cols = part.shape
    tm = _rows_tile(rows, cols * 2, budget=2 << 20)
    core = jnp.reshape(lax.axis_index("c"), (1,)).astype(jnp.int32)

    def body(c_ref, p_ref, g_ref, o_ref):
        o_ref[...] = (p_ref[...].astype(F32) + g_ref[...].astype(F32)).astype(BF16)

    spec = pl.BlockSpec((None, tm, cols), lambda q, i, c: (q, i, 0))
    grid_spec = pltpu.PrefetchScalarGridSpec(
        num_scalar_prefetch=1, grid=(4, rows // tm),
        in_specs=[pl.BlockSpec((None, tm, cols), lambda q, i, c: (2 * q + c[0], i, 0)), spec], out_specs=spec)
    return pl.pallas_call(body, grid_spec=grid_spec, out_shape=jax.ShapeDtypeStruct((4, rows, cols), BF16),
                          compiler_params=_params(2), name=name)(core, part, got)


def _adam_math(w, g, m, v):
    nm = ADAM_B1 * m + (1.0 - ADAM_B1) * g
    nv = ADAM_B2 * v + (1.0 - ADAM_B2) * (g * g)
    m_hat = nm / (1.0 - ADAM_B1 ** ADAM_STEP)
    v_hat = nv / (1.0 - ADAM_B2 ** ADAM_STEP)
    return -ADAM_LR * (m_hat / (jnp.sqrt(v_hat) + ADAM_EPS) + ADAM_WD * w), nm, nv


def _sum_chips(got, name):
    _, rows, cols = got.shape
    tm = _rows_tile(rows, cols * 2 * 4, budget=2 << 20)

    def body(r_ref, o_ref):
        o_ref[...] = ((r_ref[0].astype(F32) + r_ref[1].astype(F32)) + r_ref[2].astype(F32)) + r_ref[3].astype(F32)

    return pl.pallas_call(
        body, grid=(rows // tm,), in_specs=[pl.BlockSpec((4, tm, cols), lambda i: (0, i, 0))],
        out_specs=pl.BlockSpec((tm, cols), lambda i: (i, 0)), out_shape=jax.ShapeDtypeStruct((rows, cols), F32),
        compiler_params=_params(1), name=name)(got)


def _sum_adamw(got, col_block, w, m, v, name):
    _, rows, cols = w.shape
    tm = _rows_tile(rows, cols * 4, budget=1 << 20)

    def body(r_ref, w_ref, m_ref, v_ref, g_ref, d_ref, nm_ref, nv_ref):
        g = ((r_ref[0].astype(F32) + r_ref[1].astype(F32)) + r_ref[2].astype(F32)) + r_ref[3].astype(F32)
        g_ref[...] = g
        d_ref[...], nm_ref[...], nv_ref[...] = _adam_math(w_ref[...], g, m_ref[...], v_ref[...])

    spec = pl.BlockSpec((None, tm, cols), lambda i: (0, i, 0))
    out = jax.ShapeDtypeStruct(w.shape, F32)
    return pl.pallas_call(
        body, grid=(rows // tm,), in_specs=[pl.BlockSpec((4, tm, cols), lambda i: (0, i, col_block)), spec, spec, spec],
        out_specs=[spec] * 4, out_shape=[out] * 4, compiler_params=_params(1), name=name)(got, w, m, v)


def _adamw(w, g, m, v, name):
    rows, cols = w.shape

    def body(w_ref, g_ref, m_ref, v_ref, d_ref, nm_ref, nv_ref):
        d_ref[...], nm_ref[...], nv_ref[...] = _adam_math(w_ref[...], g_ref[...], m_ref[...], v_ref[...])

    tm = _rows_tile(rows, cols * 4, budget=2 << 20, mult=8)
    spec = pl.BlockSpec((tm, cols), lambda i: (i, 0))
    out = jax.ShapeDtypeStruct(w.shape, F32)
    return pl.pallas_call(body, grid=(rows // tm,), in_specs=[spec] * 4, out_specs=[spec] * 3,
                          out_shape=[out] * 3, compiler_params=_params(1), name=name)(w, g, m, v)


def _permute_in(w):
    logit0 = 3 * FOX_W
    pad = jnp.zeros(w.shape[:-1] + (HEAD - N_LOGIT,), w.dtype)
    return jnp.concatenate([w[..., :logit0], w[..., logit0 + N_LOGIT:], w[..., logit0:logit0 + N_LOGIT], pad], axis=-1)


def _unpermute_in(w):
    logit0 = 3 * FOX_W
    return jnp.concatenate([w[..., :logit0], w[..., C_FL:C_FL + N_LOGIT], w[..., logit0:C_FL]], axis=-1)


def _pad_row(v, width):
    return jnp.pad(v, ((0, 0), (0, width - v.shape[1])))


def _pack_small(fq, fk, sq, sk, mq, mk, fb, sinks):
    rows = [fq, fk, sq, sk, mq, mk, _pad_row(fb, HEAD), _pad_row(sinks, HEAD)]
    return jnp.concatenate(rows + [jnp.zeros((8, HEAD), F32)], axis=0)


def _pack_norms(a, b, c, d):
    return jnp.concatenate([a, b, c, d, jnp.zeros((4, a.shape[1]), F32)], axis=0)


def kernel(x, mem, ffn1_norm, ffn1_gate, ffn1_up, ffn1_down, mix_norm, mem_norm, w_in, forget_bias, w_mem_k, w_mem_v, fox_q_gain, fox_k_gain, swa_q_gain, swa_k_gain, swa_sinks, mem_q_gain, mem_k_gain, w_out, ffn2_norm, ffn2_gate, ffn2_up, ffn2_down, loss_target, m_ffn1_norm, m_ffn1_gate, m_ffn1_up, m_ffn1_down, m_mix_norm, m_mem_norm, m_w_in, m_forget_bias, m_w_mem_k, m_w_mem_v, m_fox_q_gain, m_fox_k_gain, m_swa_q_gain, m_swa_k_gain, m_swa_sinks, m_mem_q_gain, m_mem_k_gain, m_w_out, m_ffn2_norm, m_ffn2_gate, m_ffn2_up, m_ffn2_down, v_ffn1_norm, v_ffn1_gate, v_ffn1_up, v_ffn1_down, v_mix_norm, v_mem_norm, v_w_in, v_forget_bias, v_w_mem_k, v_w_mem_v, v_fox_q_gain, v_fox_k_gain, v_swa_q_gain, v_swa_k_gain, v_swa_sinks, v_mem_q_gain, v_mem_k_gain, v_w_out, v_ffn2_norm, v_ffn2_gate, v_ffn2_up, v_ffn2_down):
    x0 = x[0]
    mem0 = mem[0]
    target = loss_target[0]
    t, d = x0.shape
    d_shard = w_in.shape[1]
    m_len = mem0.shape[0]
    tm = min(t, 512)
    tk = min(t, 512)
    tn = IN_W // 3
    tkw, tnw = min(t, 1024), IN_W // 3

    def swap(a):
        return jnp.swapaxes(a, 1, 2)

    gate1, up1, gate2, up2 = swap(ffn1_gate), swap(ffn1_up), swap(ffn2_gate), swap(ffn2_up)

    local = {
        "g1": gate1[0], "u1": up1[0], "d1": ffn1_down[0],
        "g2": gate2[0], "u2": up2[0], "d2": ffn2_down[0],
        "in": _permute_in(w_in[0]), "out": w_out[0],
        "mkv": jnp.concatenate([w_mem_k[0], w_mem_v[0]], axis=1),
    }
    shard = {k: _cast_bf16(v, f"cast_{k}") for k, v in local.items()}
    sp = _pack_small(fox_q_gain, fox_k_gain, swa_q_gain, swa_k_gain, mem_q_gain, mem_k_gain, forget_bias, swa_sinks)
    wt = {}

    wt["g1"], wt["u1"] = _all_gather([shard["g1"], shard["u1"]], "gather_ffn1_in")
    xn1 = _rms_fwd(x0, ffn1_norm, "ffn1_norm")
    half_fs = shard["g2"].shape[0] // 2
    (a1, b1, h1), ((wt["d1"], wt["in"]),) = _ffn_up(
        xn1, wt["g1"], wt["u1"], "ffn1", rides=[_ride_gather([shard["d1"], shard["in"]], 0.87)])
    tc = min(t, CONTRACT_ROWS)
    (x1, hn), ((wt["out"], wt["mkv"]),) = _ffn_down(
        x0, h1, wt["d1"], "ffn1", rides=[_ride_gather([shard["out"], shard["mkv"]], 0.6)],
        tail=_tail_norm(mix_norm, t, d, tc))
    w_in_full = wt["in"].reshape(d, IN_W)

    proj, (half,) = _mm(
        "proj", [(hn, pl.BlockSpec((tm, d), lambda n, i, k: (i, 0)),
                  w_in_full, pl.BlockSpec((d, tn), lambda n, i, k: (0, n)))],
        "nn", (3, t // tm, 1), jax.ShapeDtypeStruct((t, IN_W), F32), pl.BlockSpec((tm, tn), lambda n, i, k: (i, n)),
        rides=[_ride_gather_chips([shard["g2"]], rows=(0, half_fs))])
    w_out_full = wt["out"].reshape(d, d)
    w_mkv_full = wt["mkv"].reshape(d, 2 * MEM_W)
    fq, fk, fv, sq, sk, sv, mq, c_col = _attn_pre(proj, sp, "attn_pre")
    c_row = jnp.transpose(c_col[:, :8])
    c_rep = jnp.broadcast_to(c_row[:FOX_H, :, None], (FOX_H, t, HEAD))

    mn = _rms_fwd(mem0, mem_norm, "mem_norm")
    mkv = _mm("mem_kv", [(mn, pl.BlockSpec((m_len, d), lambda k: (0, 0)),
                          w_mkv_full, pl.BlockSpec((d, 2 * MEM_W), lambda k: (0, 0)))],
              "nn", (1,), jax.ShapeDtypeStruct((m_len, 2 * MEM_W), F32),
              pl.BlockSpec((m_len, 2 * MEM_W), lambda k: (0, 0)))
    mk, mv = _mem_pre(mkv, sp, "mem_pre")

    (o_a, lse_a), (half, half_u2) = _fox_fwd(
        fq, fk, fv, c_rep, c_row, "fox_fwd",
        rides=[_ride_gather_chips([shard["g2"]], rows=(half_fs, half_fs), into=half),
               _ride_gather_chips([shard["u2"]], rows=(0, half_fs))])
    (o_b, lse_b), ((wt["g2"],), half_u2) = _swa_fwd(
        sq, sk, sv, sp, "swa_fwd",
        rides=[_ride_gather_sibling(half), _ride_gather_chips([shard["u2"]], rows=(half_fs, half_fs), into=half_u2)])
    o_c, lse_c = _mem_fwd(mq, mk, mv, "mem_fwd")

    def rows_spec(width):
        return pl.BlockSpec((tm, width), lambda i, k: (i, 0))

    def wout_rows(first, width):
        assert first % width == 0
        return pl.BlockSpec((width, d), lambda i, k: (first // width, 0), pipeline_mode=pl.Buffered(1))

    xspec = pl.BlockSpec((tm, d), lambda i, k: (i, 0))
    (x2, xn2), ((wt["u2"],),) = _mm(
        "mix_out",
        [(o_a, rows_spec(FOX_W), w_out_full, wout_rows(0, FOX_W)),
         (o_b, rows_spec(SWA_W), w_out_full, wout_rows(FOX_W, SWA_W)),
         (o_c, rows_spec(MEM_W), w_out_full, wout_rows(FOX_W + SWA_W, MEM_W))],
        "nn", (t // tm, 1), jax.ShapeDtypeStruct((t, d), F32), xspec, res=x1, res_spec=xspec,
        rides=[_ride_gather_sibling(half_u2)], tail=_tail_norm(ffn2_norm, t, d, tm))

    (a2, b2, h2), ((wt["d2"],),) = _ffn_up(xn2, wt["g2"], wt["u2"], "ffn2", rides=[_ride_gather([shard["d2"]], 0.75)])
    dy, dyb, sq_err = _ffn_down(x2, h2, wt["d2"], "ffn2", tail=_tail_loss(target, t, d, tc))
    loss = lax.psum(0.5 * sq_err[0, 0] / d, ("x", "y", "c"))

    got = {}
    paired = {}
    landed = {}

    def pair(k, part):
        paired[k] = _pair_add(part, got[k], f"pair_{k}")

    (dg2, du2), _ = _ffn_dact(dyb, wt["d2"], a2, b2, "ffn2")
    part_d2 = _ffn_dw(h2, dyb, 0.5, "ffn2_dwd")
    part_g2, ((got["d2"],),) = _ffn_dw(dg2, xn2, 1.0, "ffn2_dwg", rides=[_ride_scatter_sibling([part_d2])])
    pair("d2", part_d2)
    half_rows = part_d2.shape[1] // 2
    first, second = (0, half_rows), (half_rows, half_rows)
    part_u2, (half, (got["g2"],)) = _ffn_dw(
        du2, xn2, 1.0, "ffn2_dwu",
        rides=[_ride_scatter_chips([paired["d2"]], first), _ride_scatter_sibling([part_g2])])
    pair("g2", part_g2)
    dxn2, ((landed["d2"],),) = _ffn_contract(
        dg2, wt["g2"], "ffn2_dxn_g", rides=[_ride_scatter_chips([paired["d2"]], second, into=half)])
    (dx2, dx2b, dgain_ffn2), (half_g2, (got["u2"],)) = _ffn_contract(
        du2, wt["u2"], "ffn2_dxn_u", res=dxn2,
        rides=[_ride_scatter_chips([paired["g2"]], first), _ride_scatter_sibling([part_u2])],
        tail=_tail_norm_bwd(x2, ffn2_norm, dy, t, d, tc))
    pair("u2", part_u2)

    dmixed = _mm("mix_out_dx", [(dx2b, xspec, w_out_full, pl.BlockSpec((d, d), lambda i, k: (0, 0)))],
                 "nt", (t // tm, 1), jax.ShapeDtypeStruct((t, d), F32), xspec)

    def k_rows(width):
        return pl.BlockSpec((tk, width), lambda j, k: (k, 0))

    part_out = [
        _mm(f"mix_out_dw{i}", [(o, k_rows(width), dx2b, k_rows(d))], "tn", (1, t // tk),
            jax.ShapeDtypeStruct((width, d), BF16), pl.BlockSpec((width, d), lambda j, k: (0, 0)))
        for i, (o, width) in enumerate(((o_a, FOX_W), (o_b, SWA_W), (o_c, MEM_W)))
    ]
    part_out = jnp.concatenate(part_out, axis=0).reshape(N_DEV, d_shard, d)

    delta, delta_rep = _delta(dmixed, o_a, o_b, o_c, "attn_delta")
    (dfq, dfk, dfv, dc_col, dc_row), ((landed["g2"],), (landed["u2"],)) = _fox_bwd(
        fq, fk, fv, c_rep, c_row, dmixed, lse_a, delta_rep, "fox_bwd",
        rides=[_ride_scatter_chips([paired["g2"]], second, into=half_g2), _ride_scatter_chips([paired["u2"]])])
    dsq, dsk, dsv, dsp_sink = _swa_bwd(sq, sk, sv, sp, dmixed, lse_b, delta, "swa_bwd")
    dmq, dmk, dmv = _mem_bwd(mq, mk, mv, dmixed, lse_c, delta, "mem_bwd")

    dmkv, dsp_mem = _mem_post_bwd(mkv, sp, dmk, dmv, "mem_post_bwd")
    part_mkv = _mm("mem_kv_dw", [(mn, pl.BlockSpec((m_len, d), lambda k: (0, 0)),
                                  dmkv, pl.BlockSpec((m_len, 2 * MEM_W), lambda k: (0, 0)))],
                   "tn", (1,), jax.ShapeDtypeStruct((d, 2 * MEM_W), BF16),
                   pl.BlockSpec((d, 2 * MEM_W), lambda k: (0, 0))).reshape(N_DEV, d_shard, 2 * MEM_W)
    dmn = _mm("mem_kv_dx", [(dmkv, pl.BlockSpec((m_len, 2 * MEM_W), lambda k: (0, 0)),
                             w_mkv_full, pl.BlockSpec((d, 2 * MEM_W), lambda k: (0, 0)))],
              "nt", (1,), jax.ShapeDtypeStruct((m_len, d), F32), pl.BlockSpec((m_len, d), lambda k: (0, 0)))
    _, _, dgain_mem = _rms_bwd(mem0, mem_norm, dmn, None, "mem_norm_bwd")

    dc_row_t = _pad_row(jnp.transpose(dc_row), HEAD)
    dproj, dsp_attn = _attn_post_bwd(proj, sp, dfq, dfk, dfv, dsq, dsk, dsv, dmq, dc_col, dc_row_t, "attn_post_bwd")
    (dx1, dx1b, dgain_mix), ((got["out"], got["mkv"]),) = _mm(
        "proj_dx", [(dproj, pl.BlockSpec((tc, IN_W), lambda i, k: (i, 0)),
                     w_in_full, pl.BlockSpec((d, IN_W), lambda i, k: (0, 0), pipeline_mode=pl.Buffered(1)))],
        "nt", (t // tc, 1), jax.ShapeDtypeStruct((t, d), F32), pl.BlockSpec((tc, d), lambda i, k: (i, 0)),
        rides=[_ride_scatter_sibling([part_out, part_mkv])], tail=_tail_norm_bwd(x1, mix_norm, dx2, t, d, tc))
    pair("out", part_out)
    pair("mkv", part_mkv)
    part_in, ((landed["out"], landed["mkv"]),) = _mm(
        "proj_dw", [(hn, pl.BlockSpec((tkw, d), lambda n, k: (k, 0)),
                     dproj, pl.BlockSpec((tkw, tnw), lambda n, k: (k, n)))],
        "tn", (IN_W // tnw, t // tkw), jax.ShapeDtypeStruct((d, IN_W), BF16), pl.BlockSpec((d, tnw), lambda n, k: (0, n)),
        rides=[_ride_scatter_chips([paired["out"], paired["mkv"]])])
    part_in = part_in.reshape(N_DEV, d_shard, IN_W)

    (dg1, du1), ((got["in"],),) = _ffn_dact(dx1b, wt["d1"], a1, b1, "ffn1", rides=[_ride_scatter_sibling([part_in])])
    pair("in", part_in)
    part_d1, ((landed["in"],),) = _ffn_dw(h1, dx1b, 0.5, "ffn1_dwd", rides=[_ride_scatter_chips([paired["in"]])])
    part_g1, ((got["d1"],),) = _ffn_dw(dg1, xn1, 1.0, "ffn1_dwg", rides=[_ride_scatter_sibling([part_d1])])
    pair("d1", part_d1)
    part_u1, ((landed["d1"],), (got["g1"],)) = _ffn_dw(
        du1, xn1, 1.0, "ffn1_dwu", rides=[_ride_scatter_chips([paired["d1"]]), _ride_scatter_sibling([part_g1])])
    pair("g1", part_g1)
    dxn1, ((landed["g1"],), (got["u1"],)) = _ffn_contract(
        dg1, wt["g1"], "ffn1_dxn_g", rides=[_ride_scatter_chips([paired["g1"]]), _ride_scatter_sibling([part_u1])])
    pair("u1", part_u1)
    (grad_x, _, dgain_ffn1), ((landed["u1"],),) = _ffn_contract(
        du1, wt["u1"], "ffn1_dxn_u", res=dxn1, rides=[_ride_scatter_chips([paired["u1"]])],
        tail=_tail_norm_bwd(x0, ffn1_norm, dx1, t, d, tc))

    norms_sum, small_sum = _all_reduce_small(
        [_pack_norms(dgain_ffn1, dgain_mix, dgain_mem, dgain_ffn2), dsp_attn + dsp_sink + dsp_mem], "reduce_small")

    result = {
        "ffn1_gate": map(swap, _sum_adamw(landed["g1"], 0, gate1, swap(m_ffn1_gate), swap(v_ffn1_gate), "adamw_ffn1_gate")),
        "ffn1_up": map(swap, _sum_adamw(landed["u1"], 0, up1, swap(m_ffn1_up), swap(v_ffn1_up), "adamw_ffn1_up")),
        "ffn1_down": _sum_adamw(landed["d1"], 0, ffn1_down, m_ffn1_down, v_ffn1_down, "adamw_ffn1_down"),
        "w_mem_k": _sum_adamw(landed["mkv"], 0, w_mem_k, m_w_mem_k, v_w_mem_k, "adamw_w_mem_k"),
        "w_mem_v": _sum_adamw(landed["mkv"], 1, w_mem_v, m_w_mem_v, v_w_mem_v, "adamw_w_mem_v"),
        "w_out": _sum_adamw(landed["out"], 0, w_out, m_w_out, v_w_out, "adamw_w_out"),
        "ffn2_gate": map(swap, _sum_adamw(landed["g2"], 0, gate2, swap(m_ffn2_gate), swap(v_ffn2_gate), "adamw_ffn2_gate")),
        "ffn2_up": map(swap, _sum_adamw(landed["u2"], 0, up2, swap(m_ffn2_up), swap(v_ffn2_up), "adamw_ffn2_up")),
        "ffn2_down": _sum_adamw(landed["d2"], 0, ffn2_down, m_ffn2_down, v_ffn2_down, "adamw_ffn2_down"),
    }
    grad_in = _unpermute_in(_sum_chips(landed["in"], "sum_w_in"))
    result["w_in"] = (grad_in[None],) + tuple(
        o[None] for o in _adamw(w_in[0], grad_in, m_w_in[0], v_w_in[0], "adamw_w_in"))

    norm_names = ["ffn1_norm", "mix_norm", "mem_norm", "ffn2_norm"]
    norm_w = _pack_norms(ffn1_norm, mix_norm, mem_norm, ffn2_norm)
    norm_m = _pack_norms(m_ffn1_norm, m_mix_norm, m_mem_norm, m_ffn2_norm)
    norm_v = _pack_norms(v_ffn1_norm, v_mix_norm, v_mem_norm, v_ffn2_norm)
    outs = (norms_sum,) + tuple(_adamw(norm_w, norms_sum, norm_m, norm_v, "adamw_norms"))
    for i, k in enumerate(norm_names):
        result[k] = tuple(o[i:i + 1] for o in outs)

    small_names = ["fox_q_gain", "fox_k_gain", "swa_q_gain", "swa_k_gain", "mem_q_gain", "mem_k_gain",
                   "forget_bias", "swa_sinks"]
    small_m = _pack_small(m_fox_q_gain, m_fox_k_gain, m_swa_q_gain, m_swa_k_gain, m_mem_q_gain, m_mem_k_gain,
                          m_forget_bias, m_swa_sinks)
    small_v = _pack_small(v_fox_q_gain, v_fox_k_gain, v_swa_q_gain, v_swa_k_gain, v_mem_q_gain, v_mem_k_gain,
                          v_forget_bias, v_swa_sinks)
    outs = (small_sum,) + tuple(_adamw(sp, small_sum, small_m, small_v, "adamw_small"))
    for i, k in enumerate(small_names):
        width = N_LOGIT if k in ("forget_bias", "swa_sinks") else HEAD
        result[k] = tuple(o[i:i + 1, :width] for o in outs)

    order = ["ffn1_norm", "ffn1_gate", "ffn1_up", "ffn1_down", "mix_norm", "mem_norm", "w_in", "forget_bias",
             "w_mem_k", "w_mem_v", "fox_q_gain", "fox_k_gain", "swa_q_gain", "swa_k_gain", "swa_sinks",
             "mem_q_gain", "mem_k_gain", "w_out", "ffn2_norm", "ffn2_gate", "ffn2_up", "ffn2_down"]
    result = {k: tuple(v) for k, v in result.items()}
    flat = [loss, grad_x[None]]
    for kind in range(4):
        flat += [result[k][kind] for k in order]
    return tuple(flat)
```

```python
import functools

import jax
import jax.numpy as jnp
from jax import lax
from jax.experimental import pallas as pl
from jax.experimental.pallas import tpu as pltpu

F32 = jnp.float32
BF16 = jnp.bfloat16
MESH = pl.DeviceIdType.MESH
ANY = pl.BlockSpec(memory_space=pl.ANY)

N_DEV = 8
EPS = 1e-6
NEG_INF = -1e30
HEAD = 128
FOX_H, SWA_H, SWA_KV, MEM_H = 6, 6, 2, 4
FOX_W, SWA_W, SWA_KV_W, MEM_W = FOX_H * HEAD, SWA_H * HEAD, SWA_KV * HEAD, MEM_H * HEAD
SCALE = HEAD ** -0.5
SWA_BLOCK = 128
C_FQ, C_FK, C_FV = 0, FOX_W, 2 * FOX_W
C_SQ = 3 * FOX_W
C_SK = C_SQ + SWA_W
C_SV = C_SK + SWA_KV_W
C_MQ = C_SV + SWA_KV_W
C_FL = C_MQ + MEM_W
IN_W = C_FL + HEAD
N_LOGIT = FOX_H
R_FQ, R_FK, R_SQ, R_SK, R_MQ, R_MK, R_FB, R_SINK = range(8)
ADAM_LR, ADAM_B1, ADAM_B2, ADAM_EPS, ADAM_WD, ADAM_STEP = 0.001, 0.9, 0.999, 1e-08, 0.01, 10
VMEM_BYTES = 56 * 1024 * 1024

DN = {
    "nn": (((1,), (0,)), ((), ())),
    "nt": (((1,), (1,)), ((), ())),
    "tn": (((0,), (0,)), ((), ())),
}


def _params(n_axes):
    return pltpu.CompilerParams(dimension_semantics=("arbitrary",) * n_axes, vmem_limit_bytes=VMEM_BYTES)


def _dot(a, b, dims="nn"):
    return lax.dot_general(a.astype(BF16), b.astype(BF16), DN[dims], preferred_element_type=F32)


def _sigmoid(x):
    return 0.5 * jnp.tanh(0.5 * x) + 0.5


def _me():
    return lax.axis_index("x"), lax.axis_index("y"), lax.axis_index("c")


def _lin(p):
    return 4 * p[0] + 2 * p[1] + p[2]


def _rows_tile(rows, row_bytes, budget=4 << 20, mult=16):
    best = None
    for k in range(1, rows + 1):
        if rows % k == 0 and (rows // k) % mult == 0 and (rows // k) * row_bytes <= budget:
            best = rows // k
            break
    assert best is not None, (rows, row_bytes)
    return best


class _Ride:
    def __init__(self, inputs, out_shapes, aliases, n_remote, n_local, start, wait):
        self.inputs, self.out_shapes, self.aliases = list(inputs), list(out_shapes), dict(aliases)
        self.n_remote, self.n_local, self.start, self.wait = n_remote, n_local, start, wait


def _remote(src, dst, send, recv, k, to):
    return pltpu.make_async_remote_copy(src_ref=src, dst_ref=dst, send_sem=send.at[k], recv_sem=recv.at[k],
                                        device_id=to, device_id_type=MESH)


def _other_chips(x, y):
    return [(1 - x, y), (x, 1 - y), (1 - x, 1 - y)]


ALL_CHIPS = [(0, 0), (0, 1), (1, 0), (1, 1)]


def _rows_of(ref, rows, slot=None):
    if slot is None:
        return ref if rows is None else ref.at[pl.ds(rows[0], rows[1])]
    return ref.at[slot] if rows is None else ref.at[slot, pl.ds(rows[0], rows[1])]


def _ride_gather_chips(xs, rows=None, into=None):
    n = len(xs)

    def copies(ins, outs, send, recv):
        x, y, c = _me()
        out = []
        for a in range(n):
            for j, chip in enumerate(_other_chips(x, y)):
                peer = (*chip, c)
                src = _rows_of(ins[a], rows)
                out.append((_remote(src, _rows_of(outs[a], rows, _lin((x, y, c))), send, recv, 3 * a + j, peer),
                            _remote(src, _rows_of(outs[a], rows, _lin(peer)), send, recv, 3 * a + j, peer)))
        return out

    def mine(ins, outs, local):
        me = _lin(_me())
        return [pltpu.make_async_copy(_rows_of(ins[a], rows), _rows_of(outs[a], rows, me), local.at[a])
                for a in range(n)]

    def start(ins, outs, send, recv, local):
        for cp in mine(ins, outs, local):
            cp.start()
        for sent, _ in copies(ins, outs, send, recv):
            sent.start()

    def wait(ins, outs, send, recv, local):
        for sent, landed in copies(ins, outs, send, recv):
            landed.wait_recv()
            sent.wait_send()
        for cp in mine(ins, outs, local):
            cp.wait()

    shapes = [jax.ShapeDtypeStruct((N_DEV,) + x.shape, x.dtype) for x in xs]
    if into is None:
        return _Ride(xs, shapes, {}, 3 * n, n, start, wait)
    return _Ride(list(xs) + list(into), shapes, {n + a: a for a in range(n)}, 3 * n, n, start, wait)


def _ride_gather_sibling(bufs):
    n = len(bufs)

    def copies(outs, send, recv):
        x, y, c = _me()
        out = []
        for a in range(n):
            for q, (px, py) in enumerate(ALL_CHIPS):
                there = outs[a].at[4 * px + 2 * py + c]
                here = outs[a].at[4 * px + 2 * py + 1 - c]
                out.append((_remote(there, there, send, recv, 4 * a + q, (x, y, 1 - c)),
                            _remote(here, here, send, recv, 4 * a + q, (x, y, 1 - c))))
        return out

    def start(ins, outs, send, recv, local):
        for sent, _ in copies(outs, send, recv):
            sent.start()

    def wait(ins, outs, send, recv, local):
        for sent, landed in copies(outs, send, recv):
            landed.wait_recv()
            sent.wait_send()

    shapes = [jax.ShapeDtypeStruct(b.shape, b.dtype) for b in bufs]
    return _Ride(bufs, shapes, {a: a for a in range(n)}, 4 * n, 0, start, wait)


def _ride_gather(xs, mid_frac, rows=None, into=None):
    n = len(xs)
    chips = _ride_gather_chips(xs, rows, into)

    def sibling_copies(outs, send, recv):
        x, y, c = _me()
        out = []
        for a in range(n):
            for q, (px, py) in enumerate(ALL_CHIPS):
                there = _rows_of(outs[a], rows, 4 * px + 2 * py + c)
                here = _rows_of(outs[a], rows, 4 * px + 2 * py + 1 - c)
                k = 3 * n + 4 * a + q
                out.append((_remote(there, there, send, recv, k, (x, y, 1 - c)),
                            _remote(here, here, send, recv, k, (x, y, 1 - c))))
        return out

    def mid(ins, outs, send, recv, local):
        chips.wait(ins, outs, send, recv, local)
        for sent, _ in sibling_copies(outs, send, recv):
            sent.start()

    def wait(ins, outs, send, recv, local):
        for sent, landed in sibling_copies(outs, send, recv):
            landed.wait_recv()
            sent.wait_send()

    ride = _Ride(chips.inputs, chips.out_shapes, chips.aliases, 7 * n, n, chips.start, wait)
    ride.mid, ride.mid_frac = mid, mid_frac
    return ride


def _ride_scatter_sibling(parts):
    n = len(parts)

    def copies(ins, outs, send, recv):
        x, y, c = _me()
        out = []
        for a in range(n):
            for q, (px, py) in enumerate(ALL_CHIPS):
                cp = _remote(ins[a].at[4 * px + 2 * py + 1 - c], outs[a].at[q], send, recv, 4 * a + q, (x, y, 1 - c))
                out.append(cp)
        return out

    def start(ins, outs, send, recv, local):
        for cp in copies(ins, outs, send, recv):
            cp.start()

    def wait(ins, outs, send, recv, local):
        for cp in copies(ins, outs, send, recv):
            cp.wait_recv()
            cp.wait_send()

    shapes = [jax.ShapeDtypeStruct((4,) + p.shape[1:], p.dtype) for p in parts]
    return _Ride(parts, shapes, {}, 4 * n, 0, start, wait)


def _ride_scatter_chips(pairs, rows=None, into=None):
    n = len(pairs)

    def part(ref, slot):
        return _rows_of(ref, rows, slot)

    def copies(ins, outs, send, recv):
        x, y, c = _me()
        out = []
        for a in range(n):
            for j, (px, py) in enumerate(_other_chips(x, y)):
                peer = (px, py, c)
                src = part(ins[a], 2 * px + py)
                out.append((_remote(src, part(outs[a], 2 * x + y), send, recv, 3 * a + j, peer),
                            _remote(src, part(outs[a], 2 * px + py), send, recv, 3 * a + j, peer)))
        return out

    def mine(ins, outs, local):
        x, y, _ = _me()
        return [pltpu.make_async_copy(part(ins[a], 2 * x + y), part(outs[a], 2 * x + y), local.at[a])
                for a in range(n)]

    def start(ins, outs, send, recv, local):
        for cp in mine(ins, outs, local):
            cp.start()
        for sent, _ in copies(ins, outs, send, recv):
            sent.start()

    def wait(ins, outs, send, recv, local):
        for sent, landed in copies(ins, outs, send, recv):
            landed.wait_recv()
            sent.wait_send()
        for cp in mine(ins, outs, local):
            cp.wait()

    shapes = [jax.ShapeDtypeStruct(p.shape, p.dtype) for p in pairs]
    if into is None:
        return _Ride(pairs, shapes, {}, 3 * n, n, start, wait)
    return _Ride(list(pairs) + list(into), shapes, {n + a: a for a in range(n)}, 3 * n, n, start, wait)


def _call(name, body, grid, in_specs, out_specs, out_shape, operands, scratch=(), rides=()):
    n_in, n_out, n_scr = len(operands), len(out_shape), len(scratch)
    ride_in, ride_out, ride_scr, aliases, spans = [], [], [], {}, []
    for r in rides:
        for i, o in r.aliases.items():
            aliases[n_in + len(ride_in) + i] = n_out + len(ride_out) + o
        spans.append((len(ride_in), len(r.inputs), len(ride_out), len(r.out_shapes)))
        ride_in += r.inputs
        ride_out += r.out_shapes
        ride_scr += [pltpu.SemaphoreType.DMA((r.n_remote,)), pltpu.SemaphoreType.DMA((r.n_remote,)),
                     pltpu.SemaphoreType.DMA((max(r.n_local, 1),))]

    def wrapped(*refs):
        c_in, r_in = refs[:n_in], refs[n_in:n_in + len(ride_in)]
        p = n_in + len(ride_in)
        c_out, r_out = refs[p:p + n_out], refs[p + n_out:p + n_out + len(ride_out)]
        p += n_out + len(ride_out)
        c_scr, r_scr = refs[p:p + n_scr], refs[p + n_scr:]

        n_steps = functools.reduce(lambda a, b: a * b, grid, 1)
        step = functools.reduce(lambda acc, ax: acc * grid[ax] + pl.program_id(ax), range(len(grid)), 0)

        def each(method, at):
            for k, (r, (i0, ni, o0, no)) in enumerate(zip(rides, spans)):
                fn = getattr(r, method, None)
                if fn is None:
                    continue
                run = functools.partial(fn, r_in[i0:i0 + ni], r_out[o0:o0 + no], *r_scr[3 * k:3 * k + 3])
                if grid:
                    pl.when(step == at(r))(run)
                else:
                    run()

        each("start", lambda r: 0)
        each("mid", lambda r: int(r.mid_frac * (n_steps - 1)))
        body(*c_in, *c_out, *c_scr)
        each("wait", lambda r: n_steps - 1)

    outs = pl.pallas_call(
        wrapped, grid=grid, in_specs=list(in_specs) + [ANY] * len(ride_in),
        out_specs=list(out_specs) + [ANY] * len(ride_out), out_shape=list(out_shape) + ride_out,
        scratch_shapes=list(scratch) + ride_scr, input_output_aliases=aliases,
        compiler_params=_params(len(grid)), name=name)(*operands, *ride_in)
    outs = list(outs)
    ride_results = [outs[n_out + o0:n_out + o0 + no] for (_, _, o0, no) in spans]
    return outs[:n_out], ride_results


def _only_copies(name, rides):
    return _call(name, lambda: None, (), [], [], [], [], rides=rides)[1]


class _Tail:
    def __init__(self, extra, out_shapes, out_specs, fn):
        self.extra, self.out_shapes, self.out_specs, self.fn = list(extra), list(out_shapes), list(out_specs), fn


def _mm(name, pairs, dims, grid, out_shape, out_spec, res=None, res_spec=None, alpha=1.0, rides=(), tail=None):
    n = len(pairs)
    nk = grid[-1]
    kax = len(grid) - 1
    acc_shape = tuple(d for d in out_spec.block_shape if d is not None)
    n_extra = len(tail.extra) if tail else 0
    n_outs = len(tail.out_shapes) if tail else 1

    def body(*refs):
        pos = 2 * n
        r_ref = None
        if res is not None:
            r_ref = refs[pos]
            pos += 1
        x_refs = refs[pos:pos + n_extra]
        o_refs = refs[pos + n_extra:pos + n_extra + n_outs]
        pos += n_extra + n_outs
        part = None
        for p in range(n):
            d = _dot(refs[2 * p][...], refs[2 * p + 1][...], dims)
            part = d if part is None else part + d

        def finish(acc):
            if alpha != 1.0:
                acc = acc * alpha
            if r_ref is not None:
                acc = r_ref[...] + acc
            if tail:
                tail.fn(acc, x_refs, o_refs)
            else:
                o_refs[0][...] = acc.astype(o_refs[0].dtype)

        if nk == 1:
            finish(part)
        else:
            acc_ref = refs[pos]
            k = pl.program_id(kax)

            @pl.when(k == 0)
            def _():
                acc_ref[...] = part

            @pl.when(k > 0)
            def _():
                acc_ref[...] += part

            @pl.when(k == nk - 1)
            def _():
                finish(acc_ref[...])

    operands, in_specs = [], []
    for a, a_spec, b, b_spec in pairs:
        operands += [a, b]
        in_specs += [a_spec, b_spec]
    if res is not None:
        operands.append(res)
        in_specs.append(res_spec)
    for a, a_spec in (tail.extra if tail else []):
        operands.append(a)
        in_specs.append(a_spec)
    outs, ride_results = _call(name, body, grid, in_specs, tail.out_specs if tail else [out_spec],
                               tail.out_shapes if tail else [out_shape], operands,
                               scratch=[pltpu.VMEM(acc_shape, F32)] if nk > 1 else [], rides=rides)
    outs = outs if tail else outs[0]
    return (outs, ride_results) if rides else outs


def _accumulate(ref, part):
    @pl.when(pl.program_id(0) == 0)
    def _():
        ref[...] = part

    @pl.when(pl.program_id(0) > 0)
    def _():
        ref[...] += part


def _tail_norm(gain, t, d, tm):
    def fn(v, x_refs, o_refs):
        o_refs[0][...] = v
        r = lax.rsqrt(jnp.mean(v * v, axis=-1, keepdims=True) + EPS)
        o_refs[1][...] = (v * r * x_refs[0][...]).astype(BF16)

    rows = pl.BlockSpec((tm, d), lambda i, k: (i, 0))
    return _Tail([(gain, pl.BlockSpec((1, d), lambda i, k: (0, 0)))],
                 [jax.ShapeDtypeStruct((t, d), F32), jax.ShapeDtypeStruct((t, d), BF16)], [rows, rows], fn)


def _tail_loss(target, t, d, tm):
    def fn(v, x_refs, o_refs):
        err = v - x_refs[0][...]
        dy = err * (1.0 / d)
        o_refs[0][...] = dy
        o_refs[1][...] = dy.astype(BF16)
        _accumulate(o_refs[2], jnp.zeros((8, 128), F32) + jnp.sum(err * err))

    rows = pl.BlockSpec((tm, d), lambda i, k: (i, 0))
    return _Tail([(target, rows)],
                 [jax.ShapeDtypeStruct((t, d), F32), jax.ShapeDtypeStruct((t, d), BF16),
                  jax.ShapeDtypeStruct((8, 128), F32)],
                 [rows, rows, pl.BlockSpec((8, 128), lambda i, k: (0, 0))], fn)


def _tail_norm_bwd(x, gain, dres, t, d, tm):
    def fn(dy, x_refs, o_refs):
        xv = x_refs[0][...]
        r = lax.rsqrt(jnp.mean(xv * xv, axis=-1, keepdims=True) + EPS)
        xh = xv * r
        dxh = dy * x_refs[1][...]
        dx = r * (dxh - xh * jnp.mean(dxh * xh, axis=-1, keepdims=True)) + x_refs[2][...]
        o_refs[0][...] = dx
        o_refs[1][...] = dx.astype(BF16)
        _accumulate(o_refs[2], jnp.sum(dy * xh, axis=0, keepdims=True))

    rows = pl.BlockSpec((tm, d), lambda i, k: (i, 0))
    vec = pl.BlockSpec((1, d), lambda i, k: (0, 0))
    return _Tail([(x, rows), (gain, vec), (dres, rows)],
                 [jax.ShapeDtypeStruct((t, d), F32), jax.ShapeDtypeStruct((t, d), BF16),
                  jax.ShapeDtypeStruct((1, d), F32)], [rows, rows, vec], fn)


def _cast_bf16(x, name):
    rows, cols = x.shape
    tm = _rows_tile(rows, cols * 4)

    def body(x_ref, o_ref):
        o_ref[...] = x_ref[...].astype(BF16)

    spec = pl.BlockSpec((tm, cols), lambda i: (i, 0))
    return pl.pallas_call(body, grid=(rows // tm,), in_specs=[spec], out_specs=spec,
                          out_shape=jax.ShapeDtypeStruct(x.shape, BF16), compiler_params=_params(1), name=name)(x)


def _rms_fwd(x, gain, name):
    rows, d = x.shape
    tm = min(rows, 512)

    def body(x_ref, g_ref, o_ref):
        xv = x_ref[...]
        r = lax.rsqrt(jnp.mean(xv * xv, axis=-1, keepdims=True) + EPS)
        o_ref[...] = (xv * r * g_ref[...]).astype(BF16)

    spec = pl.BlockSpec((tm, d), lambda i: (i, 0))
    return pl.pallas_call(body, grid=(rows // tm,), in_specs=[spec, pl.BlockSpec((1, d), lambda i: (0, 0))],
                          out_specs=spec, out_shape=jax.ShapeDtypeStruct(x.shape, BF16),
                          compiler_params=_params(1), name=name)(x, gain)


def _rms_bwd(x, gain, dxn, dres, name, rides=()):
    rows, d = x.shape
    tm = min(rows, 256)
    with_res = dres is not None

    def body(*refs):
        if with_res:
            x_ref, g_ref, dy_ref, r_ref, dx_ref, dxb_ref, dg_ref = refs
        else:
            x_ref, g_ref, dy_ref, dx_ref, dxb_ref, dg_ref = refs
        xv = x_ref[...]
        r = lax.rsqrt(jnp.mean(xv * xv, axis=-1, keepdims=True) + EPS)
        xh = xv * r
        dy = dy_ref[...]
        dxh = dy * g_ref[...]
        dx = r * (dxh - xh * jnp.mean(dxh * xh, axis=-1, keepdims=True))
        if with_res:
            dx = dx + r_ref[...]
        dx_ref[...] = dx
        dxb_ref[...] = dx.astype(BF16)
        part = jnp.sum(dy * xh, axis=0, keepdims=True)

        @pl.when(pl.program_id(0) == 0)
        def _():
            dg_ref[...] = part

        @pl.when(pl.program_id(0) > 0)
        def _():
            dg_ref[...] += part

    spec = pl.BlockSpec((tm, d), lambda i: (i, 0))
    vec = pl.BlockSpec((1, d), lambda i: (0, 0))
    ops = [x, gain, dxn] + ([dres] if with_res else [])
    outs, ride_results = _call(
        name, body, (rows // tm,), [spec, vec, spec] + ([spec] if with_res else []), [spec, spec, vec],
        [jax.ShapeDtypeStruct(x.shape, F32), jax.ShapeDtypeStruct(x.shape, BF16), jax.ShapeDtypeStruct((1, d), F32)],
        ops, rides=rides)
    return (outs, ride_results) if rides else outs


ROW_CHUNK = 256


def _ffn_up(xn, wg, wu, tag, rides=()):
    t, d = xn.shape
    nd, fs, _ = wg.shape
    tm = min(t, 512)
    rc = min(tm, ROW_CHUNK)

    def body(x_ref, wg_ref, wu_ref, a_ref, b_ref, h_ref):
        for r in range(0, tm, rc):
            xv = x_ref[r:r + rc, :]
            g = _dot(xv, wg_ref[...], "nt")
            u = _dot(xv, wu_ref[...], "nt")
            sig = _sigmoid(g)
            silu = g * sig
            a_ref[r:r + rc, :] = (0.5 * u * (sig + silu * (1.0 - sig))).astype(BF16)
            b_ref[r:r + rc, :] = (0.5 * silu).astype(BF16)
            h_ref[r:r + rc, :] = (silu * u).astype(BF16)

    wspec = pl.BlockSpec((None, fs, d), lambda j, i: (j, 0, 0))
    hspec = pl.BlockSpec((None, tm, fs), lambda j, i: (j, i, 0))
    hid = jax.ShapeDtypeStruct((nd, t, fs), BF16)
    return _call(f"{tag}_up", body, (nd, t // tm), [pl.BlockSpec((tm, d), lambda j, i: (i, 0)), wspec, wspec],
                 [hspec] * 3, [hid] * 3, [xn, wg, wu], rides=rides)


CONTRACT_ROWS = 256


def _ffn_contract(hid, w, name, res=None, alpha=1.0, rides=(), tail=None):
    nd, t, fs = hid.shape
    d = w.shape[2]
    tm = min(t, CONTRACT_ROWS)
    xspec = pl.BlockSpec((tm, d), lambda i, k: (i, 0))
    pairs = [(hid, pl.BlockSpec((None, tm, fs), lambda i, k, s=s: (s, i, 0)),
              w, pl.BlockSpec((None, fs, d), lambda i, k, s=s: (s, 0, 0), pipeline_mode=pl.Buffered(1)))
             for s in range(nd)]
    return _mm(name, pairs, "nn", (t // tm, 1), jax.ShapeDtypeStruct((t, d), F32), xspec,
               res=res, res_spec=xspec if res is not None else None, alpha=alpha, rides=rides, tail=tail)


def _ffn_down(x, h, wd, tag, rides=(), tail=None):
    return _ffn_contract(h, wd, f"{tag}_down", res=x, alpha=0.5, rides=rides, tail=tail)


def _ffn_dact(dyb, wd, a, b, tag, rides=()):
    nd, t, fs = a.shape
    d = dyb.shape[1]
    tm = min(t, 512)
    rc = min(tm, ROW_CHUNK)

    def body(dy_ref, wd_ref, a_ref, b_ref, dg_ref, du_ref):
        for r in range(0, tm, rc):
            dh = _dot(dy_ref[r:r + rc, :], wd_ref[...], "nt")
            dg_ref[r:r + rc, :] = (dh * a_ref[r:r + rc, :].astype(F32)).astype(BF16)
            du_ref[r:r + rc, :] = (dh * b_ref[r:r + rc, :].astype(F32)).astype(BF16)

    hspec = pl.BlockSpec((None, tm, fs), lambda j, i: (j, i, 0))
    hid = jax.ShapeDtypeStruct((nd, t, fs), BF16)
    return _call(f"{tag}_dact", body, (nd, t // tm),
                 [pl.BlockSpec((tm, d), lambda j, i: (i, 0)), pl.BlockSpec((None, fs, d), lambda j, i: (j, 0, 0)),
                  hspec, hspec], [hspec] * 2, [hid] * 2, [dyb, wd, a, b], rides=rides)


def _ffn_dw(hid, act, alpha, name, rides=()):
    nd, t, fs = hid.shape
    d = act.shape[1]
    tk = t
    return _mm(name, [(hid, pl.BlockSpec((None, tk, fs), lambda j, k: (j, k, 0)),
                       act, pl.BlockSpec((tk, d), lambda j, k: (k, 0), pipeline_mode=pl.Buffered(1)))],
               "tn", (nd, t // tk),
               jax.ShapeDtypeStruct((nd, fs, d), BF16), pl.BlockSpec((None, fs, d), lambda j, k: (j, 0, 0)),
               alpha=alpha, rides=rides)


def _head_norm(x, gain):
    r = lax.rsqrt(jnp.mean(x * x, axis=-1, keepdims=True) + EPS)
    return x * r * gain


def _head_norm_bwd(x, gain, dy):
    r = lax.rsqrt(jnp.mean(x * x, axis=-1, keepdims=True) + EPS)
    xh = x * r
    dxh = dy * gain
    dx = r * (dxh - xh * jnp.mean(dxh * xh, axis=-1, keepdims=True))
    return dx, jnp.sum(dy * xh, axis=0, keepdims=True)


def _hs(h, base=0):
    return slice(base + h * HEAD, base + (h + 1) * HEAD)


def _tri(n, lower):
    r = lax.broadcasted_iota(jnp.int32, (n, n), 0)
    c = lax.broadcasted_iota(jnp.int32, (n, n), 1)
    return ((r >= c) if lower else (r <= c)).astype(F32)


def _attn_pre(proj, sp, name):
    t = proj.shape[0]
    tm = min(t, 256)

    def body(p_ref, sp_ref, fq, fk, fv, sq, sk, sv, mq, cc, carry):
        @pl.when(pl.program_id(0) == 0)
        def _():
            carry[...] = jnp.zeros_like(carry)

        for h in range(FOX_H):
            fq[:, _hs(h)] = _head_norm(p_ref[:, _hs(h, C_FQ)], sp_ref[R_FQ:R_FQ + 1, :]).astype(BF16)
            fk[:, _hs(h)] = _head_norm(p_ref[:, _hs(h, C_FK)], sp_ref[R_FK:R_FK + 1, :]).astype(BF16)
        fv[...] = p_ref[:, C_FV:C_FV + FOX_W].astype(BF16)
        for h in range(SWA_H):
            sq[:, _hs(h)] = _head_norm(p_ref[:, _hs(h, C_SQ)], sp_ref[R_SQ:R_SQ + 1, :]).astype(BF16)
        for h in range(SWA_KV):
            sk[:, _hs(h)] = _head_norm(p_ref[:, _hs(h, C_SK)], sp_ref[R_SK:R_SK + 1, :]).astype(BF16)
        sv[...] = p_ref[:, C_SV:C_SV + SWA_KV_W].astype(BF16)
        for h in range(MEM_H):
            mq[:, _hs(h)] = _head_norm(p_ref[:, _hs(h, C_MQ)], sp_ref[R_MQ:R_MQ + 1, :]).astype(BF16)
        z = p_ref[:, C_FL:C_FL + HEAD] + sp_ref[R_FB:R_FB + 1, :]
        lane = lax.broadcasted_iota(jnp.int32, z.shape, 1)
        log_f = jnp.minimum(z, 0.0) - jnp.log(1.0 + jnp.exp(-jnp.abs(z)))
        log_f = jnp.where(lane < N_LOGIT, log_f, 0.0)
        c = jnp.dot(_tri(tm, True), log_f, precision=lax.Precision.HIGHEST, preferred_element_type=F32)
        c = c + carry[0:1, :]
        cc[...] = c
        carry[...] = jnp.broadcast_to(c[tm - 1:tm, :], carry.shape)

    def rows(w):
        return pl.BlockSpec((tm, w), lambda i: (i, 0))

    def shape(w, dt):
        return jax.ShapeDtypeStruct((t, w), dt)

    widths = [FOX_W, FOX_W, FOX_W, SWA_W, SWA_KV_W, SWA_KV_W, MEM_W]
    return pl.pallas_call(
        body, grid=(t // tm,), in_specs=[rows(IN_W), pl.BlockSpec((16, 128), lambda i: (0, 0))],
        out_specs=[rows(w) for w in widths] + [rows(HEAD)],
        out_shape=[shape(w, BF16) for w in widths] + [shape(HEAD, F32)],
        scratch_shapes=[pltpu.VMEM((8, 128), F32)], compiler_params=_params(1), name=name)(proj, sp)


def _attn_post_bwd(proj, sp, dfq, dfk, dfv, dsq, dsk, dsv, dmq, dc_col, dc_row_t, name):
    t = proj.shape[0]
    tm = min(t, 256)
    nb = t // tm

    def body(p_ref, sp_ref, dfq_r, dfk_r, dfv_r, dsq_r, dsk_r, dsv_r, dmq_r, dcc_r, dcr_r, dp_ref, dsp_ref, carry):
        @pl.when(pl.program_id(0) == 0)
        def _():
            carry[...] = jnp.zeros_like(carry)
            dsp_ref[...] = jnp.zeros_like(dsp_ref)

        def group(n_heads, col, row, d_ref):
            total = None
            for h in range(n_heads):
                dx, dg = _head_norm_bwd(p_ref[:, _hs(h, col)], sp_ref[row:row + 1, :], d_ref[:, _hs(h)])
                dp_ref[:, _hs(h, col)] = dx.astype(BF16)
                total = dg if total is None else total + dg
            dsp_ref[row:row + 1, :] += total

        group(FOX_H, C_FQ, R_FQ, dfq_r)
        group(FOX_H, C_FK, R_FK, dfk_r)
        dp_ref[:, C_FV:C_FV + FOX_W] = dfv_r[...].astype(BF16)
        group(SWA_H, C_SQ, R_SQ, dsq_r)
        group(SWA_KV, C_SK, R_SK, dsk_r)
        dp_ref[:, C_SV:C_SV + SWA_KV_W] = dsv_r[...].astype(BF16)
        group(MEM_H, C_MQ, R_MQ, dmq_r)
        dc = dcc_r[...] - dcr_r[...]
        rc = jnp.dot(_tri(tm, False), dc, precision=lax.Precision.HIGHEST, preferred_element_type=F32)
        rc = rc + carry[0:1, :]
        carry[...] = jnp.broadcast_to(rc[0:1, :], carry.shape)
        z = p_ref[:, C_FL:C_FL + HEAD] + sp_ref[R_FB:R_FB + 1, :]
        dz = rc * _sigmoid(-z)
        dp_ref[:, C_FL:C_FL + HEAD] = dz.astype(BF16)
        dsp_ref[R_FB:R_FB + 1, :] += jnp.sum(dz, axis=0, keepdims=True)

    def rows(w):
        return pl.BlockSpec((tm, w), lambda i: (nb - 1 - i, 0))

    small = pl.BlockSpec((16, 128), lambda i: (0, 0))
    widths = [FOX_W, FOX_W, FOX_W, SWA_W, SWA_KV_W, SWA_KV_W, MEM_W, HEAD, HEAD]
    return pl.pallas_call(
        body, grid=(nb,), in_specs=[rows(IN_W), small] + [rows(w) for w in widths],
        out_specs=[rows(IN_W), small],
        out_shape=[jax.ShapeDtypeStruct((t, IN_W), BF16), jax.ShapeDtypeStruct((16, 128), F32)],
        scratch_shapes=[pltpu.VMEM((8, 128), F32)], compiler_params=_params(1), name=name,
    )(proj, sp, dfq, dfk, dfv, dsq, dsk, dsv, dmq, dc_col, dc_row_t)


def _head_column(values):
    rows = values[0].shape[0]
    lane = lax.broadcasted_iota(jnp.int32, (rows, HEAD), 1)
    out = jnp.zeros((rows, HEAD), F32)
    for h, v in enumerate(values):
        out = jnp.where(lane == h, v, out)
    return out


def _head_row(values, n_rows=8):
    cols = values[0].shape[1]
    sub = lax.broadcasted_iota(jnp.int32, (n_rows, cols), 0)
    out = jnp.zeros((n_rows, cols), F32)
    for h, v in enumerate(values):
        out = jnp.where(sub == h, v, out)
    return out


def _delta(dmixed, o_a, o_b, o_c, name):
    t = dmixed.shape[0]
    tm = min(t, 512)

    def body(d_ref, a_ref, b_ref, c_ref, o_ref, rep_ref):
        cols = []
        for ref, n_heads, base in ((a_ref, FOX_H, 0), (b_ref, SWA_H, FOX_W), (c_ref, MEM_H, FOX_W + SWA_W)):
            for h in range(n_heads):
                cols.append(jnp.sum(d_ref[:, _hs(h, base)] * ref[:, _hs(h)], axis=-1, keepdims=True))
        o_ref[...] = _head_column(cols)
        for h in range(FOX_H):
            rep_ref[h] = jnp.broadcast_to(cols[h], (tm, HEAD))

    def rows(w):
        return pl.BlockSpec((tm, w), lambda i: (i, 0))

    return pl.pallas_call(body, grid=(t // tm,), in_specs=[rows(dmixed.shape[1]), rows(FOX_W), rows(SWA_W), rows(MEM_W)],
                          out_specs=[rows(HEAD), pl.BlockSpec((FOX_H, tm, HEAD), lambda i: (0, i, 0))],
                          out_shape=[jax.ShapeDtypeStruct((t, HEAD), F32), jax.ShapeDtypeStruct((FOX_H, t, HEAD), F32)],
                          compiler_params=_params(1), name=name)(dmixed, o_a, o_b, o_c)


def _fox_fwd(fq, fk, fv, c_rep, c_row, name, rides=()):
    t = fq.shape[0]
    tb = min(t, 512)
    nb = t // tb
    n_tiles = tb // HEAD

    def body(q_ref, k_ref, v_ref, cc_ref, cr_ref, o_ref, lse_ref, m_s, l_s, acc_s):
        qi, ki = pl.program_id(0), pl.program_id(1)

        @pl.when(ki == 0)
        def _():
            m_s[...] = jnp.full_like(m_s, NEG_INF)
            l_s[...] = jnp.zeros_like(l_s)
            acc_s[...] = jnp.zeros_like(acc_s)

        def step(diagonal):
            if diagonal:
                r = lax.broadcasted_iota(jnp.int32, (tb, HEAD), 0)
                c = lax.broadcasted_iota(jnp.int32, (tb, HEAD), 1)
            for h in range(FOX_H):
                s = _dot(q_ref[:, _hs(h)], k_ref[:, _hs(h)], "nt")
                cc = cc_ref[h]
                tiles, m_cur = [], None
                for j in range(n_tiles):
                    st = s[:, _hs(j)] * SCALE + cc - cr_ref[h:h + 1, _hs(j)]
                    if diagonal:
                        st = jnp.where(r >= c + j * HEAD, st, NEG_INF)
                    tiles.append(st)
                    m_cur = st if m_cur is None else jnp.maximum(m_cur, st)
                m_prev = m_s[h]
                m_new = jnp.maximum(m_prev, jnp.max(m_cur, axis=-1, keepdims=True))
                alpha = jnp.exp(m_prev - m_new)
                ps = [jnp.exp(st - m_new) for st in tiles]
                l_cur = ps[0]
                for p in ps[1:]:
                    l_cur = l_cur + p
                l_s[h] = alpha * l_s[h] + jnp.sum(l_cur, axis=-1, keepdims=True)
                p = jnp.concatenate([p.astype(BF16) for p in ps], axis=1)
                acc_s[:, _hs(h)] = alpha * acc_s[:, _hs(h)] + _dot(p, v_ref[:, _hs(h)])
                m_s[h] = m_new

        @pl.when(ki < qi)
        def _():
            step(False)

        @pl.when(ki == qi)
        def _():
            step(True)
            for h in range(FOX_H):
                o_ref[:, _hs(h)] = acc_s[:, _hs(h)] / l_s[h]
                lse_ref[h] = m_s[h] + jnp.log(l_s[h])

    qspec = pl.BlockSpec((tb, FOX_W), lambda i, j: (i, 0))
    kspec = pl.BlockSpec((tb, FOX_W), lambda i, j: (jnp.minimum(i, j), 0))
    rep = pl.BlockSpec((FOX_H, tb, HEAD), lambda i, j: (0, i, 0))
    return _call(
        name, body, (nb, nb),
        [qspec, kspec, kspec, rep, pl.BlockSpec((8, tb), lambda i, j: (0, jnp.minimum(i, j)))],
        [qspec, rep],
        [jax.ShapeDtypeStruct((t, FOX_W), F32), jax.ShapeDtypeStruct((FOX_H, t, HEAD), F32)],
        [fq, fk, fv, c_rep, c_row],
        scratch=[pltpu.VMEM((FOX_H, tb, HEAD), F32), pltpu.VMEM((FOX_H, tb, HEAD), F32), pltpu.VMEM((tb, FOX_W), F32)],
        rides=rides)


def _fox_bwd(fq, fk, fv, c_rep, c_row, dmixed, lse, delta, name, rides=()):
    t = fq.shape[0]
    tb = min(t, 512)
    nb = t // tb
    n_tiles = tb // HEAD

    def body(q_ref, k_ref, v_ref, cc_ref, cr_ref, do_ref, lse_ref, dl_ref,
             dq_ref, dk_ref, dv_ref, dcc_ref, dcr_ref):
        ki, qi = pl.program_id(0), pl.program_id(1)

        @pl.when((ki == 0) & (qi == 0))
        def _():
            dq_ref[...] = jnp.zeros_like(dq_ref)
            dcc_ref[...] = jnp.zeros_like(dcc_ref)

        @pl.when(qi == 0)
        def _():
            dk_ref[...] = jnp.zeros_like(dk_ref)
            dv_ref[...] = jnp.zeros_like(dv_ref)
            dcr_ref[...] = jnp.zeros_like(dcr_ref)

        def step(diagonal):
            rows = pl.ds(pl.multiple_of(qi * tb, tb), tb)
            if diagonal:
                r = lax.broadcasted_iota(jnp.int32, (tb, HEAD), 0)
                c = lax.broadcasted_iota(jnp.int32, (tb, HEAD), 1)
            row_sums, col_sums = [], []
            for h in range(FOX_H):
                q, k, v, do = q_ref[:, _hs(h)], k_ref[:, _hs(h)], v_ref[:, _hs(h)], do_ref[:, _hs(h)]
                s = _dot(q, k, "nt")
                dp = _dot(do, v, "nt")
                cc, lse_h, dl_h = cc_ref[h], lse_ref[h], dl_ref[h]
                ps, dss, row = [], [], None
                for j in range(n_tiles):
                    st = s[:, _hs(j)] * SCALE + cc - cr_ref[h:h + 1, _hs(j)]
                    if diagonal:
                        st = jnp.where(r >= c + j * HEAD, st, NEG_INF)
                    pt = jnp.exp(st - lse_h)
                    dst = pt * (dp[:, _hs(j)] - dl_h)
                    ps.append(pt.astype(BF16))
                    dss.append(dst)
                    row = dst if row is None else row + dst
                p = jnp.concatenate(ps, axis=1)
                ds = jnp.concatenate(dss, axis=1)
                dsb = ds.astype(BF16)
                dv_ref[:, _hs(h)] += _dot(p, do, "tn")
                dk_ref[:, _hs(h)] += _dot(dsb, q, "tn") * SCALE
                dq_ref[rows, _hs(h)] += _dot(dsb, k) * SCALE
                row_sums.append(jnp.sum(row, axis=1, keepdims=True))
                col_sums.append(jnp.sum(ds, axis=0, keepdims=True))
            dcc_ref[rows, :] += _head_column(row_sums)
            dcr_ref[...] += _head_row(col_sums)

        @pl.when(qi > ki)
        def _():
            step(False)

        @pl.when(qi == ki)
        def _():
            step(True)

    def qmap(j, i):
        return (jnp.maximum(i, j), 0)

    qspec = pl.BlockSpec((tb, FOX_W), qmap)
    kspec = pl.BlockSpec((tb, FOX_W), lambda j, i: (j, 0))
    rep = pl.BlockSpec((FOX_H, tb, HEAD), lambda j, i: (0, jnp.maximum(i, j), 0))
    rowspec = pl.BlockSpec((8, tb), lambda j, i: (0, j))
    return _call(
        name, body, (nb, nb), [qspec, kspec, kspec, rep, rowspec, qspec, rep, rep],
        [pl.BlockSpec((t, FOX_W), lambda j, i: (0, 0)), kspec, kspec,
         pl.BlockSpec((t, HEAD), lambda j, i: (0, 0)), rowspec],
        [jax.ShapeDtypeStruct((t, FOX_W), F32)] * 3 + [jax.ShapeDtypeStruct((t, HEAD), F32),
                                                       jax.ShapeDtypeStruct((8, t), F32)],
        [fq, fk, fv, c_rep, c_row, dmixed, lse, delta], rides=rides)


def _swa_logits(q, k_cur, k_prev, slope, first_block):
    w = SWA_BLOCK
    r = lax.broadcasted_iota(jnp.int32, (w, w), 0)
    j = lax.broadcasted_iota(jnp.int32, (w, w), 1)
    dist_cur = r - j
    dist_prev = w + r - j
    s_cur = _dot(q, k_cur, "nt") * SCALE - slope * dist_cur.astype(F32)
    s_cur = jnp.where(dist_cur >= 0, s_cur, NEG_INF)
    s_prev = _dot(q, k_prev, "nt") * SCALE - slope * dist_prev.astype(F32)
    s_prev = jnp.where((j > r) & jnp.logical_not(first_block), s_prev, NEG_INF)
    return s_cur, s_prev


def _slope(h):
    return float(2.0 ** (-8.0 * (h + 1) / SWA_H))


def _swa_fwd(sq, sk, sv, sp, name, rides=()):
    t = sq.shape[0]
    w = SWA_BLOCK
    nb = t // w
    group = SWA_H // SWA_KV

    def body(q_ref, kp_ref, kc_ref, vp_ref, vc_ref, sp_ref, o_ref, lse_ref):
        first = pl.program_id(0) == 0
        lses = []
        for h in range(SWA_H):
            kv = h // group
            s_cur, s_prev = _swa_logits(q_ref[:, _hs(h)], kc_ref[:, _hs(kv)], kp_ref[:, _hs(kv)], _slope(h), first)
            sink = sp_ref[R_SINK:R_SINK + 1, h:h + 1]
            m = jnp.maximum(jnp.maximum(jnp.max(s_cur, axis=-1, keepdims=True),
                                        jnp.max(s_prev, axis=-1, keepdims=True)), sink)
            p_cur = jnp.exp(s_cur - m)
            p_prev = jnp.exp(s_prev - m)
            l = jnp.sum(p_cur, axis=-1, keepdims=True) + jnp.sum(p_prev, axis=-1, keepdims=True) + jnp.exp(sink - m)
            o_ref[:, _hs(h)] = (_dot(p_cur, vc_ref[:, _hs(kv)]) + _dot(p_prev, vp_ref[:, _hs(kv)])) / l
            lses.append(m + jnp.log(l))
        lse_ref[...] = _head_column(lses)

    qspec = pl.BlockSpec((w, SWA_W), lambda n: (n, 0))
    cur = pl.BlockSpec((w, SWA_KV_W), lambda n: (n, 0))
    prev = pl.BlockSpec((w, SWA_KV_W), lambda n: (jnp.maximum(n - 1, 0), 0))
    return _call(
        name, body, (nb,), [qspec, prev, cur, prev, cur, pl.BlockSpec((16, 128), lambda n: (0, 0))],
        [qspec, pl.BlockSpec((w, HEAD), lambda n: (n, 0))],
        [jax.ShapeDtypeStruct((t, SWA_W), F32), jax.ShapeDtypeStruct((t, HEAD), F32)],
        [sq, sk, sk, sv, sv, sp], rides=rides)


def _swa_bwd(sq, sk, sv, sp, dmixed, lse, delta, name):
    t = sq.shape[0]
    w = SWA_BLOCK
    nb = t // w
    group = SWA_H // SWA_KV
    do_block = FOX_W // SWA_W
    assert FOX_W % SWA_W == 0

    def body(q_ref, kp_ref, kc_ref, vp_ref, vc_ref, sp_ref, do_ref, lse_ref, dl_ref,
             dq_ref, dk_ref, dv_ref, dsp_ref, ck, cv):
        step = pl.program_id(0)
        first = step == nb - 1

        @pl.when(step == 0)
        def _():
            ck[...] = jnp.zeros_like(ck)
            cv[...] = jnp.zeros_like(cv)
            dsp_ref[...] = jnp.zeros_like(dsp_ref)

        dk_cur = [None] * SWA_KV
        dk_prev = [None] * SWA_KV
        dv_cur = [None] * SWA_KV
        dv_prev = [None] * SWA_KV
        dsinks = []

        def add(lst, i, v):
            lst[i] = v if lst[i] is None else lst[i] + v

        for h in range(SWA_H):
            kv = h // group
            q, do = q_ref[:, _hs(h)], do_ref[:, _hs(h)]
            kc, kp, vc, vp = kc_ref[:, _hs(kv)], kp_ref[:, _hs(kv)], vc_ref[:, _hs(kv)], vp_ref[:, _hs(kv)]
            s_cur, s_prev = _swa_logits(q, kc, kp, _slope(h), first)
            lse_h = lse_ref[:, h:h + 1]
            dl_h = dl_ref[:, FOX_H + h:FOX_H + h + 1]
            p_cur = jnp.exp(s_cur - lse_h)
            p_prev = jnp.exp(s_prev - lse_h)
            p_sink = jnp.exp(sp_ref[R_SINK:R_SINK + 1, h:h + 1] - lse_h)
            ds_cur = p_cur * (_dot(do, vc, "nt") - dl_h)
            ds_prev = p_prev * (_dot(do, vp, "nt") - dl_h)
            dq_ref[:, _hs(h)] = (_dot(ds_cur, kc) + _dot(ds_prev, kp)) * SCALE
            add(dk_cur, kv, _dot(ds_cur, q, "tn") * SCALE)
            add(dk_prev, kv, _dot(ds_prev, q, "tn") * SCALE)
            add(dv_cur, kv, _dot(p_cur, do, "tn"))
            add(dv_prev, kv, _dot(p_prev, do, "tn"))
            dsinks.append(-jnp.sum(p_sink * dl_h, axis=0, keepdims=True))
        for kv in range(SWA_KV):
            dk_ref[:, _hs(kv)] = dk_cur[kv] + ck[:, _hs(kv)]
            dv_ref[:, _hs(kv)] = dv_cur[kv] + cv[:, _hs(kv)]
            ck[:, _hs(kv)] = dk_prev[kv]
            cv[:, _hs(kv)] = dv_prev[kv]
        lane = lax.broadcasted_iota(jnp.int32, (1, HEAD), 1)
        row = jnp.zeros((1, HEAD), F32)
        for h in range(SWA_H):
            row = jnp.where(lane == h, dsinks[h], row)
        dsp_ref[R_SINK:R_SINK + 1, :] += row

    def rev(n):
        return nb - 1 - n

    qspec = pl.BlockSpec((w, SWA_W), lambda n: (rev(n), 0))
    cur = pl.BlockSpec((w, SWA_KV_W), lambda n: (rev(n), 0))
    prev = pl.BlockSpec((w, SWA_KV_W), lambda n: (jnp.maximum(rev(n) - 1, 0), 0))
    col = pl.BlockSpec((w, HEAD), lambda n: (rev(n), 0))
    small = pl.BlockSpec((16, 128), lambda n: (0, 0))
    return pl.pallas_call(
        body, grid=(nb,),
        in_specs=[qspec, prev, cur, prev, cur, small, pl.BlockSpec((w, SWA_W), lambda n: (rev(n), do_block)), col, col],
        out_specs=[qspec, cur, cur, small],
        out_shape=[jax.ShapeDtypeStruct((t, SWA_W), F32), jax.ShapeDtypeStruct((t, SWA_KV_W), F32),
                   jax.ShapeDtypeStruct((t, SWA_KV_W), F32), jax.ShapeDtypeStruct((16, 128), F32)],
        scratch_shapes=[pltpu.VMEM((w, SWA_KV_W), F32), pltpu.VMEM((w, SWA_KV_W), F32)],
        compiler_params=_params(1), name=name)(sq, sk, sk, sv, sv, sp, dmixed, lse, delta)


def _mem_pre(mkv, sp, name):
    m = mkv.shape[0]

    def body(x_ref, sp_ref, k_ref, v_ref):
        for h in range(MEM_H):
            k_ref[:, _hs(h)] = _head_norm(x_ref[:, _hs(h)], sp_ref[R_MK:R_MK + 1, :]).astype(BF16)
        v_ref[...] = x_ref[:, MEM_W:2 * MEM_W].astype(BF16)

    out = jax.ShapeDtypeStruct((m, MEM_W), BF16)
    return pl.pallas_call(body, out_shape=[out, out], name=name)(mkv, sp)


def _mem_post_bwd(mkv, sp, dmk, dmv, name):
    m = mkv.shape[0]

    def body(x_ref, sp_ref, dk_ref, dv_ref, d_ref, dsp_ref):
        dsp_ref[...] = jnp.zeros_like(dsp_ref)
        total = None
        for h in range(MEM_H):
            dx, dg = _head_norm_bwd(x_ref[:, _hs(h)], sp_ref[R_MK:R_MK + 1, :], dk_ref[:, _hs(h)])
            d_ref[:, _hs(h)] = dx.astype(BF16)
            total = dg if total is None else total + dg
        d_ref[:, MEM_W:2 * MEM_W] = dv_ref[...].astype(BF16)
        dsp_ref[R_MK:R_MK + 1, :] = total

    return pl.pallas_call(body, out_shape=[jax.ShapeDtypeStruct((m, 2 * MEM_W), BF16),
                                           jax.ShapeDtypeStruct((16, 128), F32)], name=name)(mkv, sp, dmk, dmv)


def _mem_fwd(mq, mk, mv, name):
    t = mq.shape[0]
    m = mk.shape[0]
    tq = min(t, 512)

    def body(q_ref, k_ref, v_ref, o_ref, lse_ref):
        lses = []
        for h in range(MEM_H):
            s = _dot(q_ref[:, _hs(h)], k_ref[:, _hs(h)], "nt") * SCALE
            mx = jnp.max(s, axis=-1, keepdims=True)
            p = jnp.exp(s - mx)
            l = jnp.sum(p, axis=-1, keepdims=True)
            o_ref[:, _hs(h)] = _dot(p, v_ref[:, _hs(h)]) / l
            lses.append(mx + jnp.log(l))
        lse_ref[...] = _head_column(lses)

    qspec = pl.BlockSpec((tq, MEM_W), lambda i: (i, 0))
    kspec = pl.BlockSpec((m, MEM_W), lambda i: (0, 0))
    return pl.pallas_call(
        body, grid=(t // tq,), in_specs=[qspec, kspec, kspec],
        out_specs=[qspec, pl.BlockSpec((tq, HEAD), lambda i: (i, 0))],
        out_shape=[jax.ShapeDtypeStruct((t, MEM_W), F32), jax.ShapeDtypeStruct((t, HEAD), F32)],
        compiler_params=_params(1), name=name)(mq, mk, mv)


def _mem_bwd(mq, mk, mv, dmixed, lse, delta, name):
    t = mq.shape[0]
    m = mk.shape[0]
    tq = min(t, 512)
    do_block = (FOX_W + SWA_W) // MEM_W
    assert (FOX_W + SWA_W) % MEM_W == 0

    def body(q_ref, k_ref, v_ref, do_ref, lse_ref, dl_ref, dq_ref, dk_ref, dv_ref):
        @pl.when(pl.program_id(0) == 0)
        def _():
            dk_ref[...] = jnp.zeros_like(dk_ref)
            dv_ref[...] = jnp.zeros_like(dv_ref)

        for h in range(MEM_H):
            q, k, v, do = q_ref[:, _hs(h)], k_ref[:, _hs(h)], v_ref[:, _hs(h)], do_ref[:, _hs(h)]
            s = _dot(q, k, "nt") * SCALE
            p = jnp.exp(s - lse_ref[:, h:h + 1])
            col = FOX_H + SWA_H + h
            ds = p * (_dot(do, v, "nt") - dl_ref[:, col:col + 1])
            dq_ref[:, _hs(h)] = _dot(ds, k) * SCALE
            dk_ref[:, _hs(h)] += _dot(ds, q, "tn") * SCALE
            dv_ref[:, _hs(h)] += _dot(p, do, "tn")

    qspec = pl.BlockSpec((tq, MEM_W), lambda i: (i, 0))
    kspec = pl.BlockSpec((m, MEM_W), lambda i: (0, 0))
    col = pl.BlockSpec((tq, HEAD), lambda i: (i, 0))
    return pl.pallas_call(
        body, grid=(t // tq,),
        in_specs=[qspec, kspec, kspec, pl.BlockSpec((tq, MEM_W), lambda i: (i, do_block)), col, col],
        out_specs=[qspec, kspec, kspec],
        out_shape=[jax.ShapeDtypeStruct((t, MEM_W), F32), jax.ShapeDtypeStruct((m, MEM_W), F32),
                   jax.ShapeDtypeStruct((m, MEM_W), F32)],
        compiler_params=_params(1), name=name)(mq, mk, mv, dmixed, lse, delta)


def _all_gather(xs, name):
    n = len(xs)

    def body(*refs):
        x_refs, o_refs = refs[:n], refs[n:2 * n]
        send_sems, recv_sems, local_sems = refs[2 * n:]
        x, y, c = _me()
        me, sibling = (x, y, c), (x, y, 1 - c)
        x_nb, y_nb, diag = (1 - x, y, c), (x, 1 - y, c), (1 - x, 1 - y, c)
        relay_from = (x + (1 - c) * (1 - 2 * x), y + c * (1 - 2 * y), c)
        relay_to = (x + c * (1 - 2 * x), y + (1 - c) * (1 - 2 * y), c)

        def copy(a, k, block, to, src=None):
            slot = o_refs[a].at[_lin(block)]
            return pltpu.make_async_remote_copy(
                src_ref=slot if src is None else src, dst_ref=slot, send_sem=send_sems.at[a, k],
                recv_sem=recv_sems.at[a, k], device_id=to, device_id_type=MESH)

        mine = [pltpu.make_async_copy(x_refs[a], o_refs[a].at[_lin(me)], local_sems.at[a]) for a in range(n)]
        for cp in mine:
            cp.start()
        sent = []
        for a in range(n):
            sent += [copy(a, 0, me, sibling, src=x_refs[a]), copy(a, 1, me, x_nb, src=x_refs[a]),
                     copy(a, 2, me, y_nb, src=x_refs[a])]
        for cp in sent:
            cp.start()

        def pass_on(cp):
            cp.start()
            sent.append(cp)

        for a in range(n):
            copy(a, 1, x_nb, me).wait_recv()
            copy(a, 2, y_nb, me).wait_recv()
            pass_on(copy(a, 3, relay_from, relay_to))
            pass_on(copy(a, 4, x_nb, sibling))
            pass_on(copy(a, 5, y_nb, sibling))
        for a in range(n):
            copy(a, 3, diag, me).wait_recv()
            pass_on(copy(a, 6, diag, sibling))
        for a in range(n):
            copy(a, 0, sibling, me).wait_recv()
            for k, chip in ((4, (1 - x, y)), (5, (x, 1 - y)), (6, (1 - x, 1 - y))):
                copy(a, k, (*chip, 1 - c), me).wait_recv()
        for cp in sent:
            cp.wait_send()
        for cp in mine:
            cp.wait()

    return pl.pallas_call(
        body, in_specs=[ANY] * n, out_specs=[ANY] * n,
        out_shape=[jax.ShapeDtypeStruct((N_DEV,) + x.shape, x.dtype) for x in xs],
        scratch_shapes=[pltpu.SemaphoreType.DMA((n, 7)), pltpu.SemaphoreType.DMA((n, 7)),
                        pltpu.SemaphoreType.DMA((n,))],
        name=name)(*xs)


def _peers():
    x, y, c = _me()
    out = []
    for k in range(1, N_DEV):
        kx, ky, kc = (k >> 2) & 1, (k >> 1) & 1, k & 1
        out.append(((1 - x) if kx else x, (1 - y) if ky else y, (1 - c) if kc else c))
    return out


def _all_reduce_small(xs, name):
    n = len(xs)

    def body(*refs):
        x_refs, o_refs = refs[:n], refs[n:2 * n]
        bufs = refs[2 * n:3 * n]
        send_sems, recv_sems = refs[3 * n:]
        me = _lin(_me())
        peers = _peers()
        for a in range(n):
            bufs[a][me] = x_refs[a][...]
        sends = []
        for a in range(n):
            for k, peer in enumerate(peers):
                sends.append(pltpu.make_async_remote_copy(
                    src_ref=bufs[a].at[me], dst_ref=bufs[a].at[me], send_sem=send_sems.at[a, k],
                    recv_sem=recv_sems.at[a, k], device_id=peer, device_id_type=MESH))
        for cp in sends:
            cp.start()
        for a in range(n):
            for k, peer in enumerate(peers):
                pltpu.make_async_remote_copy(
                    src_ref=bufs[a].at[me], dst_ref=bufs[a].at[_lin(peer)], send_sem=send_sems.at[a, k],
                    recv_sem=recv_sems.at[a, k], device_id=peer, device_id_type=MESH).wait_recv()
        for cp in sends:
            cp.wait_send()
        for a in range(n):
            total = bufs[a][0]
            for q in range(1, N_DEV):
                total = total + bufs[a][q]
            o_refs[a][...] = total

    vmem = pl.BlockSpec(memory_space=pltpu.VMEM)
    return pl.pallas_call(
        body, in_specs=[vmem] * n, out_specs=[vmem] * n,
        out_shape=[jax.ShapeDtypeStruct(x.shape, F32) for x in xs],
        scratch_shapes=[pltpu.VMEM((N_DEV,) + x.shape, F32) for x in xs]
        + [pltpu.SemaphoreType.DMA((n, 7)), pltpu.SemaphoreType.DMA((n, 7))],
        name=name)(*xs)


def _pair_add(part, got, name):
    _, rows, cols = part.shape
    tm = _rows_tile(rows, cols * 2, budget=2 << 20)
    core = jnp.reshape(lax.axis_index("c"), (1,)).astype(jnp.int32)

    def body(c_ref, p_ref, g_ref, o_ref):
        o_ref[...] = (p_ref[...].astype(F32) + g_ref[...].astype(F32)).astype(BF16)

    spec = pl.BlockSpec((None, tm, cols), lambda q, i, c: (q, i, 0))
    grid_spec = pltpu.PrefetchScalarGridSpec(
        num_scalar_prefetch=1, grid=(4, rows // tm),
        in_specs=[pl.BlockSpec((None, tm, cols), lambda q, i, c: (2 * q + c[0], i, 0)), spec], out_specs=spec)
    return pl.pallas_call(body, grid_spec=grid_spec, out_shape=jax.ShapeDtypeStruct((4, rows, cols), BF16),
                          compiler_params=_params(2), name=name)(core, part, got)


def _adam_math(w, g, m, v):
    nm = ADAM_B1 * m + (1.0 - ADAM_B1) * g
    nv = ADAM_B2 * v + (1.0 - ADAM_B2) * (g * g)
    m_hat = nm / (1.0 - ADAM_B1 ** ADAM_STEP)
    v_hat = nv / (1.0 - ADAM_B2 ** ADAM_STEP)
    return -ADAM_LR * (m_hat / (jnp.sqrt(v_hat) + ADAM_EPS) + ADAM_WD * w), nm, nv


def _sum_chips(got, name):
    _, rows, cols = got.shape
    tm = _rows_tile(rows, cols * 2 * 4, budget=2 << 20)

    def body(r_ref, o_ref):
        o_ref[...] = ((r_ref[0].astype(F32) + r_ref[1].astype(F32)) + r_ref[2].astype(F32)) + r_ref[3].astype(F32)

    return pl.pallas_call(
        body, grid=(rows // tm,), in_specs=[pl.BlockSpec((4, tm, cols), lambda i: (0, i, 0))],
        out_specs=pl.BlockSpec((tm, cols), lambda i: (i, 0)), out_shape=jax.ShapeDtypeStruct((rows, cols), F32),
        compiler_params=_params(1), name=name)(got)


def _sum_adamw(got, col_block, w, m, v, name):
    _, rows, cols = w.shape
    tm = _rows_tile(rows, cols * 4, budget=1 << 20)

    def body(r_ref, w_ref, m_ref, v_ref, g_ref, d_ref, nm_ref, nv_ref):
        g = ((r_ref[0].astype(F32) + r_ref[1].astype(F32)) + r_ref[2].astype(F32)) + r_ref[3].astype(F32)
        g_ref[...] = g
        d_ref[...], nm_ref[...], nv_ref[...] = _adam_math(w_ref[...], g, m_ref[...], v_ref[...])

    spec = pl.BlockSpec((None, tm, cols), lambda i: (0, i, 0))
    out = jax.ShapeDtypeStruct(w.shape, F32)
    return pl.pallas_call(
        body, grid=(rows // tm,), in_specs=[pl.BlockSpec((4, tm, cols), lambda i: (0, i, col_block)), spec, spec, spec],
        out_specs=[spec] * 4, out_shape=[out] * 4, compiler_params=_params(1), name=name)(got, w, m, v)


def _adamw(w, g, m, v, name):
    rows, cols = w.shape

    def body(w_ref, g_ref, m_ref, v_ref, d_ref, nm_ref, nv_ref):
        d_ref[...], nm_ref[...], nv_ref[...] = _adam_math(w_ref[...], g_ref[...], m_ref[...], v_ref[...])

    tm = _rows_tile(rows, cols * 4, budget=2 << 20, mult=8)
    spec = pl.BlockSpec((tm, cols), lambda i: (i, 0))
    out = jax.ShapeDtypeStruct(w.shape, F32)
    return pl.pallas_call(body, grid=(rows // tm,), in_specs=[spec] * 4, out_specs=[spec] * 3,
                          out_shape=[out] * 3, compiler_params=_params(1), name=name)(w, g, m, v)


def _permute_in(w):
    logit0 = 3 * FOX_W
    pad = jnp.zeros(w.shape[:-1] + (HEAD - N_LOGIT,), w.dtype)
    return jnp.concatenate([w[..., :logit0], w[..., logit0 + N_LOGIT:], w[..., logit0:logit0 + N_LOGIT], pad], axis=-1)


def _unpermute_in(w):
    logit0 = 3 * FOX_W
    return jnp.concatenate([w[..., :logit0], w[..., C_FL:C_FL + N_LOGIT], w[..., logit0:C_FL]], axis=-1)


def _pad_row(v, width):
    return jnp.pad(v, ((0, 0), (0, width - v.shape[1])))


def _pack_small(fq, fk, sq, sk, mq, mk, fb, sinks):
    rows = [fq, fk, sq, sk, mq, mk, _pad_row(fb, HEAD), _pad_row(sinks, HEAD)]
    return jnp.concatenate(rows + [jnp.zeros((8, HEAD), F32)], axis=0)


def _pack_norms(a, b, c, d):
    return jnp.concatenate([a, b, c, d, jnp.zeros((4, a.shape[1]), F32)], axis=0)


def kernel(x, mem, ffn1_norm, ffn1_gate, ffn1_up, ffn1_down, mix_norm, mem_norm, w_in, forget_bias, w_mem_k, w_mem_v, fox_q_gain, fox_k_gain, swa_q_gain, swa_k_gain, swa_sinks, mem_q_gain, mem_k_gain, w_out, ffn2_norm, ffn2_gate, ffn2_up, ffn2_down, loss_target, m_ffn1_norm, m_ffn1_gate, m_ffn1_up, m_ffn1_down, m_mix_norm, m_mem_norm, m_w_in, m_forget_bias, m_w_mem_k, m_w_mem_v, m_fox_q_gain, m_fox_k_gain, m_swa_q_gain, m_swa_k_gain, m_swa_sinks, m_mem_q_gain, m_mem_k_gain, m_w_out, m_ffn2_norm, m_ffn2_gate, m_ffn2_up, m_ffn2_down, v_ffn1_norm, v_ffn1_gate, v_ffn1_up, v_ffn1_down, v_mix_norm, v_mem_norm, v_w_in, v_forget_bias, v_w_mem_k, v_w_mem_v, v_fox_q_gain, v_fox_k_gain, v_swa_q_gain, v_swa_k_gain, v_swa_sinks, v_mem_q_gain, v_mem_k_gain, v_w_out, v_ffn2_norm, v_ffn2_gate, v_ffn2_up, v_ffn2_down):
    x0 = x[0]
    mem0 = mem[0]
    target = loss_target[0]
    t, d = x0.shape
    d_shard = w_in.shape[1]
    m_len = mem0.shape[0]
    tm = min(t, 512)
    tk = min(t, 512)
    tn = IN_W // 3
    tkw, tnw = min(t, 1024), IN_W // 3

    def swap(a):
        return jnp.swapaxes(a, 1, 2)

    gate1, up1, gate2, up2 = swap(ffn1_gate), swap(ffn1_up), swap(ffn2_gate), swap(ffn2_up)

    local = {
        "g1": gate1[0], "u1": up1[0], "d1": ffn1_down[0],
        "g2": gate2[0], "u2": up2[0], "d2": ffn2_down[0],
        "in": _permute_in(w_in[0]), "out": w_out[0],
        "mkv": jnp.concatenate([w_mem_k[0], w_mem_v[0]], axis=1),
    }
    shard = {k: _cast_bf16(v, f"cast_{k}") for k, v in local.items()}
    sp = _pack_small(fox_q_gain, fox_k_gain, swa_q_gain, swa_k_gain, mem_q_gain, mem_k_gain, forget_bias, swa_sinks)
    wt = {}

    wt["g1"], wt["u1"] = _all_gather([shard["g1"], shard["u1"]], "gather_ffn1_in")
    xn1 = _rms_fwd(x0, ffn1_norm, "ffn1_norm")
    half_fs = shard["g2"].shape[0] // 2
    (a1, b1, h1), ((wt["d1"], wt["in"]),) = _ffn_up(
        xn1, wt["g1"], wt["u1"], "ffn1", rides=[_ride_gather([shard["d1"], shard["in"]], 0.87)])
    tc = min(t, CONTRACT_ROWS)
    (x1, hn), ((wt["out"], wt["mkv"]),) = _ffn_down(
        x0, h1, wt["d1"], "ffn1", rides=[_ride_gather([shard["out"], shard["mkv"]], 0.6)],
        tail=_tail_norm(mix_norm, t, d, tc))
    w_in_full = wt["in"].reshape(d, IN_W)

    proj, (half,) = _mm(
        "proj", [(hn, pl.BlockSpec((tm, d), lambda n, i, k: (i, 0)),
                  w_in_full, pl.BlockSpec((d, tn), lambda n, i, k: (0, n)))],
        "nn", (3, t // tm, 1), jax.ShapeDtypeStruct((t, IN_W), F32), pl.BlockSpec((tm, tn), lambda n, i, k: (i, n)),
        rides=[_ride_gather_chips([shard["g2"]], rows=(0, half_fs))])
    w_out_full = wt["out"].reshape(d, d)
    w_mkv_full = wt["mkv"].reshape(d, 2 * MEM_W)
    fq, fk, fv, sq, sk, sv, mq, c_col = _attn_pre(proj, sp, "attn_pre")
    c_row = jnp.transpose(c_col[:, :8])
    c_rep = jnp.broadcast_to(c_row[:FOX_H, :, None], (FOX_H, t, HEAD))

    mn = _rms_fwd(mem0, mem_norm, "mem_norm")
    mkv = _mm("mem_kv", [(mn, pl.BlockSpec((m_len, d), lambda k: (0, 0)),
                          w_mkv_full, pl.BlockSpec((d, 2 * MEM_W), lambda k: (0, 0)))],
              "nn", (1,), jax.ShapeDtypeStruct((m_len, 2 * MEM_W), F32),
              pl.BlockSpec((m_len, 2 * MEM_W), lambda k: (0, 0)))
    mk, mv = _mem_pre(mkv, sp, "mem_pre")

    (o_a, lse_a), (half, half_u2) = _fox_fwd(
        fq, fk, fv, c_rep, c_row, "fox_fwd",
        rides=[_ride_gather_chips([shard["g2"]], rows=(half_fs, half_fs), into=half),
               _ride_gather_chips([shard["u2"]], rows=(0, half_fs))])
    (o_b, lse_b), ((wt["g2"],), half_u2) = _swa_fwd(
        sq, sk, sv, sp, "swa_fwd",
        rides=[_ride_gather_sibling(half), _ride_gather_chips([shard["u2"]], rows=(half_fs, half_fs), into=half_u2)])
    o_c, lse_c = _mem_fwd(mq, mk, mv, "mem_fwd")

    def rows_spec(width):
        return pl.BlockSpec((tm, width), lambda i, k: (i, 0))

    def wout_rows(first, width):
        assert first % width == 0
        return pl.BlockSpec((width, d), lambda i, k: (first // width, 0), pipeline_mode=pl.Buffered(1))

    xspec = pl.BlockSpec((tm, d), lambda i, k: (i, 0))
    (x2, xn2), ((wt["u2"],),) = _mm(
        "mix_out",
        [(o_a, rows_spec(FOX_W), w_out_full, wout_rows(0, FOX_W)),
         (o_b, rows_spec(SWA_W), w_out_full, wout_rows(FOX_W, SWA_W)),
         (o_c, rows_spec(MEM_W), w_out_full, wout_rows(FOX_W + SWA_W, MEM_W))],
        "nn", (t // tm, 1), jax.ShapeDtypeStruct((t, d), F32), xspec, res=x1, res_spec=xspec,
        rides=[_ride_gather_sibling(half_u2)], tail=_tail_norm(ffn2_norm, t, d, tm))

    (a2, b2, h2), ((wt["d2"],),) = _ffn_up(xn2, wt["g2"], wt["u2"], "ffn2", rides=[_ride_gather([shard["d2"]], 0.75)])
    dy, dyb, sq_err = _ffn_down(x2, h2, wt["d2"], "ffn2", tail=_tail_loss(target, t, d, tc))
    loss = lax.psum(0.5 * sq_err[0, 0] / d, ("x", "y", "c"))

    got = {}
    paired = {}
    landed = {}

    def pair(k, part):
        paired[k] = _pair_add(part, got[k], f"pair_{k}")

    (dg2, du2), _ = _ffn_dact(dyb, wt["d2"], a2, b2, "ffn2")
    part_d2 = _ffn_dw(h2, dyb, 0.5, "ffn2_dwd")
    part_g2, ((got["d2"],),) = _ffn_dw(dg2, xn2, 1.0, "ffn2_dwg", rides=[_ride_scatter_sibling([part_d2])])
    pair("d2", part_d2)
    half_rows = part_d2.shape[1] // 2
    first, second = (0, half_rows), (half_rows, half_rows)
    part_u2, (half, (got["g2"],)) = _ffn_dw(
        du2, xn2, 1.0, "ffn2_dwu",
        rides=[_ride_scatter_chips([paired["d2"]], first), _ride_scatter_sibling([part_g2])])
    pair("g2", part_g2)
    dxn2, ((landed["d2"],),) = _ffn_contract(
        dg2, wt["g2"], "ffn2_dxn_g", rides=[_ride_scatter_chips([paired["d2"]], second, into=half)])
    (dx2, dx2b, dgain_ffn2), (half_g2, (got["u2"],)) = _ffn_contract(
        du2, wt["u2"], "ffn2_dxn_u", res=dxn2,
        rides=[_ride_scatter_chips([paired["g2"]], first), _ride_scatter_sibling([part_u2])],
        tail=_tail_norm_bwd(x2, ffn2_norm, dy, t, d, tc))
    pair("u2", part_u2)

    dmixed = _mm("mix_out_dx", [(dx2b, xspec, w_out_full, pl.BlockSpec((d, d), lambda i, k: (0, 0)))],
                 "nt", (t // tm, 1), jax.ShapeDtypeStruct((t, d), F32), xspec)

    def k_rows(width):
        return pl.BlockSpec((tk, width), lambda j, k: (k, 0))

    part_out = [
        _mm(f"mix_out_dw{i}", [(o, k_rows(width), dx2b, k_rows(d))], "tn", (1, t // tk),
            jax.ShapeDtypeStruct((width, d), BF16), pl.BlockSpec((width, d), lambda j, k: (0, 0)))
        for i, (o, width) in enumerate(((o_a, FOX_W), (o_b, SWA_W), (o_c, MEM_W)))
    ]
    part_out = jnp.concatenate(part_out, axis=0).reshape(N_DEV, d_shard, d)

    delta, delta_rep = _delta(dmixed, o_a, o_b, o_c, "attn_delta")
    (dfq, dfk, dfv, dc_col, dc_row), ((landed["g2"],), (landed["u2"],)) = _fox_bwd(
        fq, fk, fv, c_rep, c_row, dmixed, lse_a, delta_rep, "fox_bwd",
        rides=[_ride_scatter_chips([paired["g2"]], second, into=half_g2), _ride_scatter_chips([paired["u2"]])])
    dsq, dsk, dsv, dsp_sink = _swa_bwd(sq, sk, sv, sp, dmixed, lse_b, delta, "swa_bwd")
    dmq, dmk, dmv = _mem_bwd(mq, mk, mv, dmixed, lse_c, delta, "mem_bwd")

    dmkv, dsp_mem = _mem_post_bwd(mkv, sp, dmk, dmv, "mem_post_bwd")
    part_mkv = _mm("mem_kv_dw", [(mn, pl.BlockSpec((m_len, d), lambda k: (0, 0)),
                                  dmkv, pl.BlockSpec((m_len, 2 * MEM_W), lambda k: (0, 0)))],
                   "tn", (1,), jax.ShapeDtypeStruct((d, 2 * MEM_W), BF16),
                   pl.BlockSpec((d, 2 * MEM_W), lambda k: (0, 0))).reshape(N_DEV, d_shard, 2 * MEM_W)
    dmn = _mm("mem_kv_dx", [(dmkv, pl.BlockSpec((m_len, 2 * MEM_W), lambda k: (0, 0)),
                             w_mkv_full, pl.BlockSpec((d, 2 * MEM_W), lambda k: (0, 0)))],
              "nt", (1,), jax.ShapeDtypeStruct((m_len, d), F32), pl.BlockSpec((m_len, d), lambda k: (0, 0)))
    _, _, dgain_mem = _rms_bwd(mem0, mem_norm, dmn, None, "mem_norm_bwd")

    dc_row_t = _pad_row(jnp.transpose(dc_row), HEAD)
    dproj, dsp_attn = _attn_post_bwd(proj, sp, dfq, dfk, dfv, dsq, dsk, dsv, dmq, dc_col, dc_row_t, "attn_post_bwd")
    (dx1, dx1b, dgain_mix), ((got["out"], got["mkv"]),) = _mm(
        "proj_dx", [(dproj, pl.BlockSpec((tc, IN_W), lambda i, k: (i, 0)),
                     w_in_full, pl.BlockSpec((d, IN_W), lambda i, k: (0, 0), pipeline_mode=pl.Buffered(1)))],
        "nt", (t // tc, 1), jax.ShapeDtypeStruct((t, d), F32), pl.BlockSpec((tc, d), lambda i, k: (i, 0)),
        rides=[_ride_scatter_sibling([part_out, part_mkv])], tail=_tail_norm_bwd(x1, mix_norm, dx2, t, d, tc))
    pair("out", part_out)
    pair("mkv", part_mkv)
    part_in, ((landed["out"], landed["mkv"]),) = _mm(
        "proj_dw", [(hn, pl.BlockSpec((tkw, d), lambda n, k: (k, 0)),
                     dproj, pl.BlockSpec((tkw, tnw), lambda n, k: (k, n)))],
        "tn", (IN_W // tnw, t // tkw), jax.ShapeDtypeStruct((d, IN_W), BF16), pl.BlockSpec((d, tnw), lambda n, k: (0, n)),
        rides=[_ride_scatter_chips([paired["out"], paired["mkv"]])])
    part_in = part_in.reshape(N_DEV, d_shard, IN_W)

    part_d1, ((got["in"],),) = _ffn_dw(h1, dx1b, 0.5, "ffn1_dwd", rides=[_ride_scatter_sibling([part_in])])
    pair("in", part_in)
    (dg1, du1), ((landed["in"],), (got["d1"],)) = _ffn_dact(
        dx1b, wt["d1"], a1, b1, "ffn1",
        rides=[_ride_scatter_chips([paired["in"]]), _ride_scatter_sibling([part_d1])])
    pair("d1", part_d1)
    part_g1, (half_d1,) = _ffn_dw(dg1, xn1, 1.0, "ffn1_dwg", rides=[_ride_scatter_chips([paired["d1"]], first)])
    part_u1, ((landed["d1"],), (got["g1"],)) = _ffn_dw(
        du1, xn1, 1.0, "ffn1_dwu",
        rides=[_ride_scatter_chips([paired["d1"]], second, into=half_d1), _ride_scatter_sibling([part_g1])])
    pair("g1", part_g1)
    dxn1, ((landed["g1"],), (got["u1"],)) = _ffn_contract(
        dg1, wt["g1"], "ffn1_dxn_g", rides=[_ride_scatter_chips([paired["g1"]]), _ride_scatter_sibling([part_u1])])
    pair("u1", part_u1)
    (grad_x, _, dgain_ffn1), ((landed["u1"],),) = _ffn_contract(
        du1, wt["u1"], "ffn1_dxn_u", res=dxn1, rides=[_ride_scatter_chips([paired["u1"]])],
        tail=_tail_norm_bwd(x0, ffn1_norm, dx1, t, d, tc))

    norms_sum, small_sum = _all_reduce_small(
        [_pack_norms(dgain_ffn1, dgain_mix, dgain_mem, dgain_ffn2), dsp_attn + dsp_sink + dsp_mem], "reduce_small")

    result = {
        "ffn1_gate": map(swap, _sum_adamw(landed["g1"], 0, gate1, swap(m_ffn1_gate), swap(v_ffn1_gate), "adamw_ffn1_gate")),
        "ffn1_up": map(swap, _sum_adamw(landed["u1"], 0, up1, swap(m_ffn1_up), swap(v_ffn1_up), "adamw_ffn1_up")),
        "ffn1_down": _sum_adamw(landed["d1"], 0, ffn1_down, m_ffn1_down, v_ffn1_down, "adamw_ffn1_down"),
        "w_mem_k": _sum_adamw(landed["mkv"], 0, w_mem_k, m_w_mem_k, v_w_mem_k, "adamw_w_mem_k"),
        "w_mem_v": _sum_adamw(landed["mkv"], 1, w_mem_v, m_w_mem_v, v_w_mem_v, "adamw_w_mem_v"),
        "w_out": _sum_adamw(landed["out"], 0, w_out, m_w_out, v_w_out, "adamw_w_out"),
        "ffn2_gate": map(swap, _sum_adamw(landed["g2"], 0, gate2, swap(m_ffn2_gate), swap(v_ffn2_gate), "adamw_ffn2_gate")),
        "ffn2_up": map(swap, _sum_adamw(landed["u2"], 0, up2, swap(m_ffn2_up), swap(v_ffn2_up), "adamw_ffn2_up")),
        "ffn2_down": _sum_adamw(landed["d2"], 0, ffn2_down, m_ffn2_down, v_ffn2_down, "adamw_ffn2_down"),
    }
    grad_in = _unpermute_in(_sum_chips(landed["in"], "sum_w_in"))
    result["w_in"] = (grad_in[None],) + tuple(
        o[None] for o in _adamw(w_in[0], grad_in, m_w_in[0], v_w_in[0], "adamw_w_in"))

    norm_names = ["ffn1_norm", "mix_norm", "mem_norm", "ffn2_norm"]
    norm_w = _pack_norms(ffn1_norm, mix_norm, mem_norm, ffn2_norm)
    norm_m = _pack_norms(m_ffn1_norm, m_mix_norm, m_mem_norm, m_ffn2_norm)
    norm_v = _pack_norms(v_ffn1_norm, v_mix_norm, v_mem_norm, v_ffn2_norm)
    outs = (norms_sum,) + tuple(_adamw(norm_w, norms_sum, norm_m, norm_v, "adamw_norms"))
    for i, k in enumerate(norm_names):
        result[k] = tuple(o[i:i + 1] for o in outs)

    small_names = ["fox_q_gain", "fox_k_gain", "swa_q_gain", "swa_k_gain", "mem_q_gain", "mem_k_gain",
                   "forget_bias", "swa_sinks"]
    small_m = _pack_small(m_fox_q_gain, m_fox_k_gain, m_swa_q_gain, m_swa_k_gain, m_mem_q_gain, m_mem_k_gain,
                          m_forget_bias, m_swa_sinks)
    small_v = _pack_small(v_fox_q_gain, v_fox_k_gain, v_swa_q_gain, v_swa_k_gain, v_mem_q_gain, v_mem_k_gain,
                          v_forget_bias, v_swa_sinks)
    outs = (small_sum,) + tuple(_adamw(sp, small_sum, small_m, small_v, "adamw_small"))
    for i, k in enumerate(small_names):
        width = N_LOGIT if k in ("forget_bias", "swa_sinks") else HEAD
        result[k] = tuple(o[i:i + 1, :width] for o in outs)

    order = ["ffn1_norm", "ffn1_gate", "ffn1_up", "ffn1_down", "mix_norm", "mem_norm", "w_in", "forget_bias",
             "w_mem_k", "w_mem_v", "fox_q_gain", "fox_k_gain", "swa_q_gain", "swa_k_gain", "swa_sinks",
             "mem_q_gain", "mem_k_gain", "w_out", "ffn2_norm", "ffn2_gate", "ffn2_up", "ffn2_down"]
    result = {k: tuple(v) for k, v in result.items()}
    flat = [loss, grad_x[None]]
    for kind in range(4):
        flat += [result[k][kind] for k in order]
    return tuple(flat)
```

```python
import functools

import jax
import jax.numpy as jnp
from jax import lax
from jax.experimental import pallas as pl
from jax.experimental.pallas import tpu as pltpu

F32 = jnp.float32
BF16 = jnp.bfloat16
MESH = pl.DeviceIdType.MESH
ANY = pl.BlockSpec(memory_space=pl.ANY)

N_DEV = 8
EPS = 1e-6
NEG_INF = -1e30
HEAD = 128
FOX_H, SWA_H, SWA_KV, MEM_H = 6, 6, 2, 4
FOX_W, SWA_W, SWA_KV_W, MEM_W = FOX_H * HEAD, SWA_H * HEAD, SWA_KV * HEAD, MEM_H * HEAD
SCALE = HEAD ** -0.5
SWA_BLOCK = 128
C_FQ, C_FK, C_FV = 0, FOX_W, 2 * FOX_W
C_SQ = 3 * FOX_W
C_SK = C_SQ + SWA_W
C_SV = C_SK + SWA_KV_W
C_MQ = C_SV + SWA_KV_W
C_FL = C_MQ + MEM_W
IN_W = C_FL + HEAD
N_LOGIT = FOX_H
R_FQ, R_FK, R_SQ, R_SK, R_MQ, R_MK, R_FB, R_SINK = range(8)
ADAM_LR, ADAM_B1, ADAM_B2, ADAM_EPS, ADAM_WD, ADAM_STEP = 0.001, 0.9, 0.999, 1e-08, 0.01, 10
VMEM_BYTES = 56 * 1024 * 1024

DN = {
    "nn": (((1,), (0,)), ((), ())),
    "nt": (((1,), (1,)), ((), ())),
    "tn": (((0,), (0,)), ((), ())),
}


def _params(n_axes):
    return pltpu.CompilerParams(dimension_semantics=("arbitrary",) * n_axes, vmem_limit_bytes=VMEM_BYTES)


def _dot(a, b, dims="nn"):
    return lax.dot_general(a.astype(BF16), b.astype(BF16), DN[dims], preferred_element_type=F32)


def _sigmoid(x):
    return 0.5 * jnp.tanh(0.5 * x) + 0.5


def _me():
    return lax.axis_index("x"), lax.axis_index("y"), lax.axis_index("c")


def _lin(p):
    return 4 * p[0] + 2 * p[1] + p[2]


def _rows_tile(rows, row_bytes, budget=4 << 20, mult=16):
    best = None
    for k in range(1, rows + 1):
        if rows % k == 0 and (rows // k) % mult == 0 and (rows // k) * row_bytes <= budget:
            best = rows // k
            break
    assert best is not None, (rows, row_bytes)
    return best


class _Ride:
    def __init__(self, inputs, out_shapes, aliases, n_remote, n_local, start, wait):
        self.inputs, self.out_shapes, self.aliases = list(inputs), list(out_shapes), dict(aliases)
        self.n_remote, self.n_local, self.start, self.wait = n_remote, n_local, start, wait


def _remote(src, dst, send, recv, k, to):
    return pltpu.make_async_remote_copy(src_ref=src, dst_ref=dst, send_sem=send.at[k], recv_sem=recv.at[k],
                                        device_id=to, device_id_type=MESH)


def _other_chips(x, y):
    return [(1 - x, y), (x, 1 - y), (1 - x, 1 - y)]


ALL_CHIPS = [(0, 0), (0, 1), (1, 0), (1, 1)]


def _rows_of(ref, rows, slot=None):
    if slot is None:
        return ref if rows is None else ref.at[pl.ds(rows[0], rows[1])]
    return ref.at[slot] if rows is None else ref.at[slot, pl.ds(rows[0], rows[1])]


def _ride_gather_chips(xs, rows=None, into=None):
    n = len(xs)

    def copies(ins, outs, send, recv):
        x, y, c = _me()
        out = []
        for a in range(n):
            for j, chip in enumerate(_other_chips(x, y)):
                peer = (*chip, c)
                src = _rows_of(ins[a], rows)
                out.append((_remote(src, _rows_of(outs[a], rows, _lin((x, y, c))), send, recv, 3 * a + j, peer),
                            _remote(src, _rows_of(outs[a], rows, _lin(peer)), send, recv, 3 * a + j, peer)))
        return out

    def mine(ins, outs, local):
        me = _lin(_me())
        return [pltpu.make_async_copy(_rows_of(ins[a], rows), _rows_of(outs[a], rows, me), local.at[a])
                for a in range(n)]

    def start(ins, outs, send, recv, local):
        for cp in mine(ins, outs, local):
            cp.start()
        for sent, _ in copies(ins, outs, send, recv):
            sent.start()

    def wait(ins, outs, send, recv, local):
        for sent, landed in copies(ins, outs, send, recv):
            landed.wait_recv()
            sent.wait_send()
        for cp in mine(ins, outs, local):
            cp.wait()

    shapes = [jax.ShapeDtypeStruct((N_DEV,) + x.shape, x.dtype) for x in xs]
    if into is None:
        return _Ride(xs, shapes, {}, 3 * n, n, start, wait)
    return _Ride(list(xs) + list(into), shapes, {n + a: a for a in range(n)}, 3 * n, n, start, wait)


def _ride_gather_sibling(bufs):
    n = len(bufs)

    def copies(outs, send, recv):
        x, y, c = _me()
        out = []
        for a in range(n):
            for q, (px, py) in enumerate(ALL_CHIPS):
                there = outs[a].at[4 * px + 2 * py + c]
                here = outs[a].at[4 * px + 2 * py + 1 - c]
                out.append((_remote(there, there, send, recv, 4 * a + q, (x, y, 1 - c)),
                            _remote(here, here, send, recv, 4 * a + q, (x, y, 1 - c))))
        return out

    def start(ins, outs, send, recv, local):
        for sent, _ in copies(outs, send, recv):
            sent.start()

    def wait(ins, outs, send, recv, local):
        for sent, landed in copies(outs, send, recv):
            landed.wait_recv()
            sent.wait_send()

    shapes = [jax.ShapeDtypeStruct(b.shape, b.dtype) for b in bufs]
    return _Ride(bufs, shapes, {a: a for a in range(n)}, 4 * n, 0, start, wait)


def _ride_gather(xs, mid_frac, rows=None, into=None):
    n = len(xs)
    chips = _ride_gather_chips(xs, rows, into)

    def sibling_copies(outs, send, recv):
        x, y, c = _me()
        out = []
        for a in range(n):
            for q, (px, py) in enumerate(ALL_CHIPS):
                there = _rows_of(outs[a], rows, 4 * px + 2 * py + c)
                here = _rows_of(outs[a], rows, 4 * px + 2 * py + 1 - c)
                k = 3 * n + 4 * a + q
                out.append((_remote(there, there, send, recv, k, (x, y, 1 - c)),
                            _remote(here, here, send, recv, k, (x, y, 1 - c))))
        return out

    def mid(ins, outs, send, recv, local):
        chips.wait(ins, outs, send, recv, local)
        for sent, _ in sibling_copies(outs, send, recv):
            sent.start()

    def wait(ins, outs, send, recv, local):
        for sent, landed in sibling_copies(outs, send, recv):
            landed.wait_recv()
            sent.wait_send()

    ride = _Ride(chips.inputs, chips.out_shapes, chips.aliases, 7 * n, n, chips.start, wait)
    ride.mid, ride.mid_frac = mid, mid_frac
    return ride


def _ride_scatter_sibling(parts):
    n = len(parts)

    def copies(ins, outs, send, recv):
        x, y, c = _me()
        out = []
        for a in range(n):
            for q, (px, py) in enumerate(ALL_CHIPS):
                cp = _remote(ins[a].at[4 * px + 2 * py + 1 - c], outs[a].at[q], send, recv, 4 * a + q, (x, y, 1 - c))
                out.append(cp)
        return out

    def start(ins, outs, send, recv, local):
        for cp in copies(ins, outs, send, recv):
            cp.start()

    def wait(ins, outs, send, recv, local):
        for cp in copies(ins, outs, send, recv):
            cp.wait_recv()
            cp.wait_send()

    shapes = [jax.ShapeDtypeStruct((4,) + p.shape[1:], p.dtype) for p in parts]
    return _Ride(parts, shapes, {}, 4 * n, 0, start, wait)


def _ride_scatter_chips(pairs, rows=None, into=None):
    n = len(pairs)

    def part(ref, slot):
        return _rows_of(ref, rows, slot)

    def copies(ins, outs, send, recv):
        x, y, c = _me()
        out = []
        for a in range(n):
            for j, (px, py) in enumerate(_other_chips(x, y)):
                peer = (px, py, c)
                src = part(ins[a], 2 * px + py)
                out.append((_remote(src, part(outs[a], 2 * x + y), send, recv, 3 * a + j, peer),
                            _remote(src, part(outs[a], 2 * px + py), send, recv, 3 * a + j, peer)))
        return out

    def mine(ins, outs, local):
        x, y, _ = _me()
        return [pltpu.make_async_copy(part(ins[a], 2 * x + y), part(outs[a], 2 * x + y), local.at[a])
                for a in range(n)]

    def start(ins, outs, send, recv, local):
        for cp in mine(ins, outs, local):
            cp.start()
        for sent, _ in copies(ins, outs, send, recv):
            sent.start()

    def wait(ins, outs, send, recv, local):
        for sent, landed in copies(ins, outs, send, recv):
            landed.wait_recv()
            sent.wait_send()
        for cp in mine(ins, outs, local):
            cp.wait()

    shapes = [jax.ShapeDtypeStruct(p.shape, p.dtype) for p in pairs]
    if into is None:
        return _Ride(pairs, shapes, {}, 3 * n, n, start, wait)
    return _Ride(list(pairs) + list(into), shapes, {n + a: a for a in range(n)}, 3 * n, n, start, wait)


def _call(name, body, grid, in_specs, out_specs, out_shape, operands, scratch=(), rides=()):
    n_in, n_out, n_scr = len(operands), len(out_shape), len(scratch)
    ride_in, ride_out, ride_scr, aliases, spans = [], [], [], {}, []
    for r in rides:
        for i, o in r.aliases.items():
            aliases[n_in + len(ride_in) + i] = n_out + len(ride_out) + o
        spans.append((len(ride_in), len(r.inputs), len(ride_out), len(r.out_shapes)))
        ride_in += r.inputs
        ride_out += r.out_shapes
        ride_scr += [pltpu.SemaphoreType.DMA((r.n_remote,)), pltpu.SemaphoreType.DMA((r.n_remote,)),
                     pltpu.SemaphoreType.DMA((max(r.n_local, 1),))]

    def wrapped(*refs):
        c_in, r_in = refs[:n_in], refs[n_in:n_in + len(ride_in)]
        p = n_in + len(ride_in)
        c_out, r_out = refs[p:p + n_out], refs[p + n_out:p + n_out + len(ride_out)]
        p += n_out + len(ride_out)
        c_scr, r_scr = refs[p:p + n_scr], refs[p + n_scr:]

        n_steps = functools.reduce(lambda a, b: a * b, grid, 1)
        step = functools.reduce(lambda acc, ax: acc * grid[ax] + pl.program_id(ax), range(len(grid)), 0)

        def each(method, at):
            for k, (r, (i0, ni, o0, no)) in enumerate(zip(rides, spans)):
                fn = getattr(r, method, None)
                if fn is None:
                    continue
                run = functools.partial(fn, r_in[i0:i0 + ni], r_out[o0:o0 + no], *r_scr[3 * k:3 * k + 3])
                if grid:
                    pl.when(step == at(r))(run)
                else:
                    run()

        each("start", lambda r: 0)
        each("mid", lambda r: int(r.mid_frac * (n_steps - 1)))
        body(*c_in, *c_out, *c_scr)
        each("wait", lambda r: n_steps - 1)

    outs = pl.pallas_call(
        wrapped, grid=grid, in_specs=list(in_specs) + [ANY] * len(ride_in),
        out_specs=list(out_specs) + [ANY] * len(ride_out), out_shape=list(out_shape) + ride_out,
        scratch_shapes=list(scratch) + ride_scr, input_output_aliases=aliases,
        compiler_params=_params(len(grid)), name=name)(*operands, *ride_in)
    outs = list(outs)
    ride_results = [outs[n_out + o0:n_out + o0 + no] for (_, _, o0, no) in spans]
    return outs[:n_out], ride_results


class _Tail:
    def __init__(self, extra, out_shapes, out_specs, fn):
        self.extra, self.out_shapes, self.out_specs, self.fn = list(extra), list(out_shapes), list(out_specs), fn


def _mm(name, pairs, dims, grid, out_shape, out_spec, res=None, res_spec=None, alpha=1.0, rides=(), tail=None):
    n = len(pairs)
    nk = grid[-1]
    kax = len(grid) - 1
    acc_shape = tuple(d for d in out_spec.block_shape if d is not None)
    n_extra = len(tail.extra) if tail else 0
    n_outs = len(tail.out_shapes) if tail else 1

    def body(*refs):
        pos = 2 * n
        r_ref = None
        if res is not None:
            r_ref = refs[pos]
            pos += 1
        x_refs = refs[pos:pos + n_extra]
        o_refs = refs[pos + n_extra:pos + n_extra + n_outs]
        pos += n_extra + n_outs
        part = None
        for p in range(n):
            d = _dot(refs[2 * p][...], refs[2 * p + 1][...], dims)
            part = d if part is None else part + d

        def finish(acc):
            if alpha != 1.0:
                acc = acc * alpha
            if r_ref is not None:
                acc = r_ref[...] + acc
            if tail:
                tail.fn(acc, x_refs, o_refs)
            else:
                o_refs[0][...] = acc.astype(o_refs[0].dtype)

        if nk == 1:
            finish(part)
        else:
            acc_ref = refs[pos]
            k = pl.program_id(kax)

            @pl.when(k == 0)
            def _():
                acc_ref[...] = part

            @pl.when(k > 0)
            def _():
                acc_ref[...] += part

            @pl.when(k == nk - 1)
            def _():
                finish(acc_ref[...])

    operands, in_specs = [], []
    for a, a_spec, b, b_spec in pairs:
        operands += [a, b]
        in_specs += [a_spec, b_spec]
    if res is not None:
        operands.append(res)
        in_specs.append(res_spec)
    for a, a_spec in (tail.extra if tail else []):
        operands.append(a)
        in_specs.append(a_spec)
    outs, ride_results = _call(name, body, grid, in_specs, tail.out_specs if tail else [out_spec],
                               tail.out_shapes if tail else [out_shape], operands,
                               scratch=[pltpu.VMEM(acc_shape, F32)] if nk > 1 else [], rides=rides)
    outs = outs if tail else outs[0]
    return (outs, ride_results) if rides else outs


def _accumulate(ref, part):
    @pl.when(pl.program_id(0) == 0)
    def _():
        ref[...] = part

    @pl.when(pl.program_id(0) > 0)
    def _():
        ref[...] += part


def _tail_norm(gain, t, d, tm):
    def fn(v, x_refs, o_refs):
        o_refs[0][...] = v
        r = lax.rsqrt(jnp.mean(v * v, axis=-1, keepdims=True) + EPS)
        o_refs[1][...] = (v * r * x_refs[0][...]).astype(BF16)

    rows = pl.BlockSpec((tm, d), lambda i, k: (i, 0))
    return _Tail([(gain, pl.BlockSpec((1, d), lambda i, k: (0, 0)))],
                 [jax.ShapeDtypeStruct((t, d), F32), jax.ShapeDtypeStruct((t, d), BF16)], [rows, rows], fn)


def _tail_loss(target, t, d, tm):
    def fn(v, x_refs, o_refs):
        err = v - x_refs[0][...]
        dy = err * (1.0 / d)
        o_refs[0][...] = dy
        o_refs[1][...] = dy.astype(BF16)
        _accumulate(o_refs[2], jnp.zeros((8, 128), F32) + jnp.sum(err * err))

    rows = pl.BlockSpec((tm, d), lambda i, k: (i, 0))
    return _Tail([(target, rows)],
                 [jax.ShapeDtypeStruct((t, d), F32), jax.ShapeDtypeStruct((t, d), BF16),
                  jax.ShapeDtypeStruct((8, 128), F32)],
                 [rows, rows, pl.BlockSpec((8, 128), lambda i, k: (0, 0))], fn)


def _tail_norm_bwd(x, gain, dres, t, d, tm):
    def fn(dy, x_refs, o_refs):
        xv = x_refs[0][...]
        r = lax.rsqrt(jnp.mean(xv * xv, axis=-1, keepdims=True) + EPS)
        xh = xv * r
        dxh = dy * x_refs[1][...]
        dx = r * (dxh - xh * jnp.mean(dxh * xh, axis=-1, keepdims=True)) + x_refs[2][...]
        o_refs[0][...] = dx
        o_refs[1][...] = dx.astype(BF16)
        _accumulate(o_refs[2], jnp.sum(dy * xh, axis=0, keepdims=True))

    rows = pl.BlockSpec((tm, d), lambda i, k: (i, 0))
    vec = pl.BlockSpec((1, d), lambda i, k: (0, 0))
    return _Tail([(x, rows), (gain, vec), (dres, rows)],
                 [jax.ShapeDtypeStruct((t, d), F32), jax.ShapeDtypeStruct((t, d), BF16),
                  jax.ShapeDtypeStruct((1, d), F32)], [rows, rows, vec], fn)


def _cast_bf16(x, name):
    rows, cols = x.shape
    tm = _rows_tile(rows, cols * 4)

    def body(x_ref, o_ref):
        o_ref[...] = x_ref[...].astype(BF16)

    spec = pl.BlockSpec((tm, cols), lambda i: (i, 0))
    return pl.pallas_call(body, grid=(rows // tm,), in_specs=[spec], out_specs=spec,
                          out_shape=jax.ShapeDtypeStruct(x.shape, BF16), compiler_params=_params(1), name=name)(x)


def _rms_fwd(x, gain, name):
    rows, d = x.shape
    tm = min(rows, 512)

    def body(x_ref, g_ref, o_ref):
        xv = x_ref[...]
        r = lax.rsqrt(jnp.mean(xv * xv, axis=-1, keepdims=True) + EPS)
        o_ref[...] = (xv * r * g_ref[...]).astype(BF16)

    spec = pl.BlockSpec((tm, d), lambda i: (i, 0))
    return pl.pallas_call(body, grid=(rows // tm,), in_specs=[spec, pl.BlockSpec((1, d), lambda i: (0, 0))],
                          out_specs=spec, out_shape=jax.ShapeDtypeStruct(x.shape, BF16),
                          compiler_params=_params(1), name=name)(x, gain)


def _rms_bwd(x, gain, dxn, dres, name, rides=()):
    rows, d = x.shape
    tm = min(rows, 256)
    with_res = dres is not None

    def body(*refs):
        if with_res:
            x_ref, g_ref, dy_ref, r_ref, dx_ref, dxb_ref, dg_ref = refs
        else:
            x_ref, g_ref, dy_ref, dx_ref, dxb_ref, dg_ref = refs
        xv = x_ref[...]
        r = lax.rsqrt(jnp.mean(xv * xv, axis=-1, keepdims=True) + EPS)
        xh = xv * r
        dy = dy_ref[...]
        dxh = dy * g_ref[...]
        dx = r * (dxh - xh * jnp.mean(dxh * xh, axis=-1, keepdims=True))
        if with_res:
            dx = dx + r_ref[...]
        dx_ref[...] = dx
        dxb_ref[...] = dx.astype(BF16)
        part = jnp.sum(dy * xh, axis=0, keepdims=True)

        @pl.when(pl.program_id(0) == 0)
        def _():
            dg_ref[...] = part

        @pl.when(pl.program_id(0) > 0)
        def _():
            dg_ref[...] += part

    spec = pl.BlockSpec((tm, d), lambda i: (i, 0))
    vec = pl.BlockSpec((1, d), lambda i: (0, 0))
    ops = [x, gain, dxn] + ([dres] if with_res else [])
    outs, ride_results = _call(
        name, body, (rows // tm,), [spec, vec, spec] + ([spec] if with_res else []), [spec, spec, vec],
        [jax.ShapeDtypeStruct(x.shape, F32), jax.ShapeDtypeStruct(x.shape, BF16), jax.ShapeDtypeStruct((1, d), F32)],
        ops, rides=rides)
    return (outs, ride_results) if rides else outs


ROW_CHUNK = 256


def _ffn_up(xn, wg, wu, tag, rides=()):
    t, d = xn.shape
    nd, fs, _ = wg.shape
    tm = min(t, 512)
    rc = min(tm, ROW_CHUNK)

    def body(x_ref, wg_ref, wu_ref, a_ref, b_ref, h_ref):
        for r in range(0, tm, rc):
            xv = x_ref[r:r + rc, :]
            g = _dot(xv, wg_ref[...], "nt")
            u = _dot(xv, wu_ref[...], "nt")
            sig = _sigmoid(g)
            silu = g * sig
            a_ref[r:r + rc, :] = (0.5 * u * (sig + silu * (1.0 - sig))).astype(BF16)
            b_ref[r:r + rc, :] = (0.5 * silu).astype(BF16)
            h_ref[r:r + rc, :] = (silu * u).astype(BF16)

    wspec = pl.BlockSpec((None, fs, d), lambda j, i: (j, 0, 0))
    hspec = pl.BlockSpec((None, tm, fs), lambda j, i: (j, i, 0))
    hid = jax.ShapeDtypeStruct((nd, t, fs), BF16)
    return _call(f"{tag}_up", body, (nd, t // tm), [pl.BlockSpec((tm, d), lambda j, i: (i, 0)), wspec, wspec],
                 [hspec] * 3, [hid] * 3, [xn, wg, wu], rides=rides)


CONTRACT_ROWS = 256


def _ffn_contract(hid, w, name, res=None, alpha=1.0, rides=(), tail=None):
    nd, t, fs = hid.shape
    d = w.shape[2]
    tm = min(t, CONTRACT_ROWS)
    xspec = pl.BlockSpec((tm, d), lambda i, k: (i, 0))
    pairs = [(hid, pl.BlockSpec((None, tm, fs), lambda i, k, s=s: (s, i, 0)),
              w, pl.BlockSpec((None, fs, d), lambda i, k, s=s: (s, 0, 0), pipeline_mode=pl.Buffered(1)))
             for s in range(nd)]
    return _mm(name, pairs, "nn", (t // tm, 1), jax.ShapeDtypeStruct((t, d), F32), xspec,
               res=res, res_spec=xspec if res is not None else None, alpha=alpha, rides=rides, tail=tail)


def _ffn_down(x, h, wd, tag, rides=(), tail=None):
    return _ffn_contract(h, wd, f"{tag}_down", res=x, alpha=0.5, rides=rides, tail=tail)


def _ffn_dact(dyb, wd, a, b, tag, rides=()):
    nd, t, fs = a.shape
    d = dyb.shape[1]
    tm = min(t, 512)
    rc = min(tm, ROW_CHUNK)

    def body(dy_ref, wd_ref, a_ref, b_ref, dg_ref, du_ref):
        for r in range(0, tm, rc):
            dh = _dot(dy_ref[r:r + rc, :], wd_ref[...], "nt")
            dg_ref[r:r + rc, :] = (dh * a_ref[r:r + rc, :].astype(F32)).astype(BF16)
            du_ref[r:r + rc, :] = (dh * b_ref[r:r + rc, :].astype(F32)).astype(BF16)

    hspec = pl.BlockSpec((None, tm, fs), lambda j, i: (j, i, 0))
    hid = jax.ShapeDtypeStruct((nd, t, fs), BF16)
    return _call(f"{tag}_dact", body, (nd, t // tm),
                 [pl.BlockSpec((tm, d), lambda j, i: (i, 0)), pl.BlockSpec((None, fs, d), lambda j, i: (j, 0, 0)),
                  hspec, hspec], [hspec] * 2, [hid] * 2, [dyb, wd, a, b], rides=rides)


def _ffn_dw(hid, act, alpha, name, rides=()):
    nd, t, fs = hid.shape
    d = act.shape[1]
    tk = t
    return _mm(name, [(hid, pl.BlockSpec((None, tk, fs), lambda j, k: (j, k, 0)),
                       act, pl.BlockSpec((tk, d), lambda j, k: (k, 0), pipeline_mode=pl.Buffered(1)))],
               "tn", (nd, t // tk),
               jax.ShapeDtypeStruct((nd, fs, d), BF16), pl.BlockSpec((None, fs, d), lambda j, k: (j, 0, 0)),
               alpha=alpha, rides=rides)


def _head_norm(x, gain):
    r = lax.rsqrt(jnp.mean(x * x, axis=-1, keepdims=True) + EPS)
    return x * r * gain


def _head_norm_bwd(x, gain, dy):
    r = lax.rsqrt(jnp.mean(x * x, axis=-1, keepdims=True) + EPS)
    xh = x * r
    dxh = dy * gain
    dx = r * (dxh - xh * jnp.mean(dxh * xh, axis=-1, keepdims=True))
    return dx, jnp.sum(dy * xh, axis=0, keepdims=True)


def _hs(h, base=0):
    return slice(base + h * HEAD, base + (h + 1) * HEAD)


def _tri(n, lower):
    r = lax.broadcasted_iota(jnp.int32, (n, n), 0)
    c = lax.broadcasted_iota(jnp.int32, (n, n), 1)
    return ((r >= c) if lower else (r <= c)).astype(F32)


def _attn_pre(proj, sp, name):
    t = proj.shape[0]
    tm = min(t, 256)

    def body(p_ref, sp_ref, fq, fk, fv, sq, sk, sv, mq, cc, carry):
        @pl.when(pl.program_id(0) == 0)
        def _():
            carry[...] = jnp.zeros_like(carry)

        for h in range(FOX_H):
            fq[:, _hs(h)] = _head_norm(p_ref[:, _hs(h, C_FQ)], sp_ref[R_FQ:R_FQ + 1, :]).astype(BF16)
            fk[:, _hs(h)] = _head_norm(p_ref[:, _hs(h, C_FK)], sp_ref[R_FK:R_FK + 1, :]).astype(BF16)
        fv[...] = p_ref[:, C_FV:C_FV + FOX_W].astype(BF16)
        for h in range(SWA_H):
            sq[:, _hs(h)] = _head_norm(p_ref[:, _hs(h, C_SQ)], sp_ref[R_SQ:R_SQ + 1, :]).astype(BF16)
        for h in range(SWA_KV):
            sk[:, _hs(h)] = _head_norm(p_ref[:, _hs(h, C_SK)], sp_ref[R_SK:R_SK + 1, :]).astype(BF16)
        sv[...] = p_ref[:, C_SV:C_SV + SWA_KV_W].astype(BF16)
        for h in range(MEM_H):
            mq[:, _hs(h)] = _head_norm(p_ref[:, _hs(h, C_MQ)], sp_ref[R_MQ:R_MQ + 1, :]).astype(BF16)
        z = p_ref[:, C_FL:C_FL + HEAD] + sp_ref[R_FB:R_FB + 1, :]
        lane = lax.broadcasted_iota(jnp.int32, z.shape, 1)
        log_f = jnp.minimum(z, 0.0) - jnp.log(1.0 + jnp.exp(-jnp.abs(z)))
        log_f = jnp.where(lane < N_LOGIT, log_f, 0.0)
        c = jnp.dot(_tri(tm, True), log_f, precision=lax.Precision.HIGHEST, preferred_element_type=F32)
        c = c + carry[0:1, :]
        cc[...] = c
        carry[...] = jnp.broadcast_to(c[tm - 1:tm, :], carry.shape)

    def rows(w):
        return pl.BlockSpec((tm, w), lambda i: (i, 0))

    def shape(w, dt):
        return jax.ShapeDtypeStruct((t, w), dt)

    widths = [FOX_W, FOX_W, FOX_W, SWA_W, SWA_KV_W, SWA_KV_W, MEM_W]
    return pl.pallas_call(
        body, grid=(t // tm,), in_specs=[rows(IN_W), pl.BlockSpec((16, 128), lambda i: (0, 0))],
        out_specs=[rows(w) for w in widths] + [rows(HEAD)],
        out_shape=[shape(w, BF16) for w in widths] + [shape(HEAD, F32)],
        scratch_shapes=[pltpu.VMEM((8, 128), F32)], compiler_params=_params(1), name=name)(proj, sp)


def _attn_post_bwd(proj, sp, dfq, dfk, dfv, dsq, dsk, dsv, dmq, dc_col, dc_row_t, name):
    t = proj.shape[0]
    tm = min(t, 256)
    nb = t // tm

    def body(p_ref, sp_ref, dfq_r, dfk_r, dfv_r, dsq_r, dsk_r, dsv_r, dmq_r, dcc_r, dcr_r, dp_ref, dsp_ref, carry):
        @pl.when(pl.program_id(0) == 0)
        def _():
            carry[...] = jnp.zeros_like(carry)
            dsp_ref[...] = jnp.zeros_like(dsp_ref)

        def group(n_heads, col, row, d_ref):
            total = None
            for h in range(n_heads):
                dx, dg = _head_norm_bwd(p_ref[:, _hs(h, col)], sp_ref[row:row + 1, :], d_ref[:, _hs(h)])
                dp_ref[:, _hs(h, col)] = dx.astype(BF16)
                total = dg if total is None else total + dg
            dsp_ref[row:row + 1, :] += total

        group(FOX_H, C_FQ, R_FQ, dfq_r)
        group(FOX_H, C_FK, R_FK, dfk_r)
        dp_ref[:, C_FV:C_FV + FOX_W] = dfv_r[...].astype(BF16)
        group(SWA_H, C_SQ, R_SQ, dsq_r)
        group(SWA_KV, C_SK, R_SK, dsk_r)
        dp_ref[:, C_SV:C_SV + SWA_KV_W] = dsv_r[...].astype(BF16)
        group(MEM_H, C_MQ, R_MQ, dmq_r)
        dc = dcc_r[...] - dcr_r[...]
        rc = jnp.dot(_tri(tm, False), dc, precision=lax.Precision.HIGHEST, preferred_element_type=F32)
        rc = rc + carry[0:1, :]
        carry[...] = jnp.broadcast_to(rc[0:1, :], carry.shape)
        z = p_ref[:, C_FL:C_FL + HEAD] + sp_ref[R_FB:R_FB + 1, :]
        dz = rc * _sigmoid(-z)
        dp_ref[:, C_FL:C_FL + HEAD] = dz.astype(BF16)
        dsp_ref[R_FB:R_FB + 1, :] += jnp.sum(dz, axis=0, keepdims=True)

    def rows(w):
        return pl.BlockSpec((tm, w), lambda i: (nb - 1 - i, 0))

    small = pl.BlockSpec((16, 128), lambda i: (0, 0))
    widths = [FOX_W, FOX_W, FOX_W, SWA_W, SWA_KV_W, SWA_KV_W, MEM_W, HEAD, HEAD]
    return pl.pallas_call(
        body, grid=(nb,), in_specs=[rows(IN_W), small] + [rows(w) for w in widths],
        out_specs=[rows(IN_W), small],
        out_shape=[jax.ShapeDtypeStruct((t, IN_W), BF16), jax.ShapeDtypeStruct((16, 128), F32)],
        scratch_shapes=[pltpu.VMEM((8, 128), F32)], compiler_params=_params(1), name=name,
    )(proj, sp, dfq, dfk, dfv, dsq, dsk, dsv, dmq, dc_col, dc_row_t)


def _head_column(values):
    rows = values[0].shape[0]
    lane = lax.broadcasted_iota(jnp.int32, (rows, HEAD), 1)
    out = jnp.zeros((rows, HEAD), F32)
    for h, v in enumerate(values):
        out = jnp.where(lane == h, v, out)
    return out


def _head_row(values, n_rows=8):
    cols = values[0].shape[1]
    sub = lax.broadcasted_iota(jnp.int32, (n_rows, cols), 0)
    out = jnp.zeros((n_rows, cols), F32)
    for h, v in enumerate(values):
        out = jnp.where(sub == h, v, out)
    return out


def _delta(dmixed, o_a, o_b, o_c, name):
    t = dmixed.shape[0]
    tm = min(t, 512)

    def body(d_ref, a_ref, b_ref, c_ref, o_ref, rep_ref):
        cols = []
        for ref, n_heads, base in ((a_ref, FOX_H, 0), (b_ref, SWA_H, FOX_W), (c_ref, MEM_H, FOX_W + SWA_W)):
            for h in range(n_heads):
                cols.append(jnp.sum(d_ref[:, _hs(h, base)] * ref[:, _hs(h)], axis=-1, keepdims=True))
        o_ref[...] = _head_column(cols)
        for h in range(FOX_H):
            rep_ref[h] = jnp.broadcast_to(cols[h], (tm, HEAD))

    def rows(w):
        return pl.BlockSpec((tm, w), lambda i: (i, 0))

    return pl.pallas_call(body, grid=(t // tm,), in_specs=[rows(dmixed.shape[1]), rows(FOX_W), rows(SWA_W), rows(MEM_W)],
                          out_specs=[rows(HEAD), pl.BlockSpec((FOX_H, tm, HEAD), lambda i: (0, i, 0))],
                          out_shape=[jax.ShapeDtypeStruct((t, HEAD), F32), jax.ShapeDtypeStruct((FOX_H, t, HEAD), F32)],
                          compiler_params=_params(1), name=name)(dmixed, o_a, o_b, o_c)


def _fox_fwd(fq, fk, fv, c_rep, c_row, name, rides=()):
    t = fq.shape[0]
    tb = min(t, 512)
    nb = t // tb
    n_tiles = tb // HEAD

    def body(q_ref, k_ref, v_ref, cc_ref, cr_ref, o_ref, lse_ref, m_s, l_s, acc_s):
        qi, ki = pl.program_id(0), pl.program_id(1)

        @pl.when(ki == 0)
        def _():
            m_s[...] = jnp.full_like(m_s, NEG_INF)
            l_s[...] = jnp.zeros_like(l_s)
            acc_s[...] = jnp.zeros_like(acc_s)

        def step(diagonal):
            if diagonal:
                r = lax.broadcasted_iota(jnp.int32, (tb, HEAD), 0)
                c = lax.broadcasted_iota(jnp.int32, (tb, HEAD), 1)
            for h in range(FOX_H):
                s = _dot(q_ref[:, _hs(h)], k_ref[:, _hs(h)], "nt")
                cc = cc_ref[h]
                tiles, m_cur = [], None
                for j in range(n_tiles):
                    st = s[:, _hs(j)] * SCALE + cc - cr_ref[h:h + 1, _hs(j)]
                    if diagonal:
                        st = jnp.where(r >= c + j * HEAD, st, NEG_INF)
                    tiles.append(st)
                    m_cur = st if m_cur is None else jnp.maximum(m_cur, st)
                m_prev = m_s[h]
                m_new = jnp.maximum(m_prev, jnp.max(m_cur, axis=-1, keepdims=True))
                alpha = jnp.exp(m_prev - m_new)
                ps = [jnp.exp(st - m_new) for st in tiles]
                l_cur = ps[0]
                for p in ps[1:]:
                    l_cur = l_cur + p
                l_s[h] = alpha * l_s[h] + jnp.sum(l_cur, axis=-1, keepdims=True)
                p = jnp.concatenate([p.astype(BF16) for p in ps], axis=1)
                acc_s[:, _hs(h)] = alpha * acc_s[:, _hs(h)] + _dot(p, v_ref[:, _hs(h)])
                m_s[h] = m_new

        @pl.when(ki < qi)
        def _():
            step(False)

        @pl.when(ki == qi)
        def _():
            step(True)
            for h in range(FOX_H):
                o_ref[:, _hs(h)] = acc_s[:, _hs(h)] / l_s[h]
                lse_ref[h] = m_s[h] + jnp.log(l_s[h])

    qspec = pl.BlockSpec((tb, FOX_W), lambda i, j: (i, 0))
    kspec = pl.BlockSpec((tb, FOX_W), lambda i, j: (jnp.minimum(i, j), 0))
    rep = pl.BlockSpec((FOX_H, tb, HEAD), lambda i, j: (0, i, 0))
    return _call(
        name, body, (nb, nb),
        [qspec, kspec, kspec, rep, pl.BlockSpec((8, tb), lambda i, j: (0, jnp.minimum(i, j)))],
        [qspec, rep],
        [jax.ShapeDtypeStruct((t, FOX_W), F32), jax.ShapeDtypeStruct((FOX_H, t, HEAD), F32)],
        [fq, fk, fv, c_rep, c_row],
        scratch=[pltpu.VMEM((FOX_H, tb, HEAD), F32), pltpu.VMEM((FOX_H, tb, HEAD), F32), pltpu.VMEM((tb, FOX_W), F32)],
        rides=rides)


def _fox_bwd(fq, fk, fv, c_rep, c_row, dmixed, lse, delta, name, rides=()):
    t = fq.shape[0]
    tb = min(t, 512)
    nb = t // tb
    n_tiles = tb // HEAD

    def body(q_ref, k_ref, v_ref, cc_ref, cr_ref, do_ref, lse_ref, dl_ref,
             dq_ref, dk_ref, dv_ref, dcc_ref, dcr_ref):
        ki, qi = pl.program_id(0), pl.program_id(1)

        @pl.when((ki == 0) & (qi == 0))
        def _():
            dq_ref[...] = jnp.zeros_like(dq_ref)
            dcc_ref[...] = jnp.zeros_like(dcc_ref)

        @pl.when(qi == 0)
        def _():
            dk_ref[...] = jnp.zeros_like(dk_ref)
            dv_ref[...] = jnp.zeros_like(dv_ref)
            dcr_ref[...] = jnp.zeros_like(dcr_ref)

        def step(diagonal):
            rows = pl.ds(pl.multiple_of(qi * tb, tb), tb)
            if diagonal:
                r = lax.broadcasted_iota(jnp.int32, (tb, HEAD), 0)
                c = lax.broadcasted_iota(jnp.int32, (tb, HEAD), 1)
            row_sums, col_sums = [], []
            for h in range(FOX_H):
                q, k, v, do = q_ref[:, _hs(h)], k_ref[:, _hs(h)], v_ref[:, _hs(h)], do_ref[:, _hs(h)]
                s = _dot(q, k, "nt")
                dp = _dot(do, v, "nt")
                cc, lse_h, dl_h = cc_ref[h], lse_ref[h], dl_ref[h]
                ps, dss, row = [], [], None
                for j in range(n_tiles):
                    st = s[:, _hs(j)] * SCALE + cc - cr_ref[h:h + 1, _hs(j)]
                    if diagonal:
                        st = jnp.where(r >= c + j * HEAD, st, NEG_INF)
                    pt = jnp.exp(st - lse_h)
                    dst = pt * (dp[:, _hs(j)] - dl_h)
                    ps.append(pt.astype(BF16))
                    dss.append(dst)
                    row = dst if row is None else row + dst
                p = jnp.concatenate(ps, axis=1)
                ds = jnp.concatenate(dss, axis=1)
                dsb = ds.astype(BF16)
                dv_ref[:, _hs(h)] += _dot(p, do, "tn")
                dk_ref[:, _hs(h)] += _dot(dsb, q, "tn") * SCALE
                dq_ref[rows, _hs(h)] += _dot(dsb, k) * SCALE
                row_sums.append(jnp.sum(row, axis=1, keepdims=True))
                col_sums.append(jnp.sum(ds, axis=0, keepdims=True))
            dcc_ref[rows, :] += _head_column(row_sums)
            dcr_ref[...] += _head_row(col_sums)

        @pl.when(qi > ki)
        def _():
            step(False)

        @pl.when(qi == ki)
        def _():
            step(True)

    def qmap(j, i):
        return (jnp.maximum(i, j), 0)

    qspec = pl.BlockSpec((tb, FOX_W), qmap)
    kspec = pl.BlockSpec((tb, FOX_W), lambda j, i: (j, 0))
    rep = pl.BlockSpec((FOX_H, tb, HEAD), lambda j, i: (0, jnp.maximum(i, j), 0))
    rowspec = pl.BlockSpec((8, tb), lambda j, i: (0, j))
    return _call(
        name, body, (nb, nb), [qspec, kspec, kspec, rep, rowspec, qspec, rep, rep],
        [pl.BlockSpec((t, FOX_W), lambda j, i: (0, 0)), kspec, kspec,
         pl.BlockSpec((t, HEAD), lambda j, i: (0, 0)), rowspec],
        [jax.ShapeDtypeStruct((t, FOX_W), F32)] * 3 + [jax.ShapeDtypeStruct((t, HEAD), F32),
                                                       jax.ShapeDtypeStruct((8, t), F32)],
        [fq, fk, fv, c_rep, c_row, dmixed, lse, delta], rides=rides)


def _swa_logits(q, k_cur, k_prev, slope, first_block):
    w = SWA_BLOCK
    r = lax.broadcasted_iota(jnp.int32, (w, w), 0)
    j = lax.broadcasted_iota(jnp.int32, (w, w), 1)
    dist_cur = r - j
    dist_prev = w + r - j
    s_cur = _dot(q, k_cur, "nt") * SCALE - slope * dist_cur.astype(F32)
    s_cur = jnp.where(dist_cur >= 0, s_cur, NEG_INF)
    s_prev = _dot(q, k_prev, "nt") * SCALE - slope * dist_prev.astype(F32)
    s_prev = jnp.where((j > r) & jnp.logical_not(first_block), s_prev, NEG_INF)
    return s_cur, s_prev


def _slope(h):
    return float(2.0 ** (-8.0 * (h + 1) / SWA_H))


def _swa_fwd(sq, sk, sv, sp, name, rides=()):
    t = sq.shape[0]
    w = SWA_BLOCK
    nb = t // w
    group = SWA_H // SWA_KV

    def body(q_ref, kp_ref, kc_ref, vp_ref, vc_ref, sp_ref, o_ref, lse_ref):
        first = pl.program_id(0) == 0
        lses = []
        for h in range(SWA_H):
            kv = h // group
            s_cur, s_prev = _swa_logits(q_ref[:, _hs(h)], kc_ref[:, _hs(kv)], kp_ref[:, _hs(kv)], _slope(h), first)
            sink = sp_ref[R_SINK:R_SINK + 1, h:h + 1]
            m = jnp.maximum(jnp.maximum(jnp.max(s_cur, axis=-1, keepdims=True),
                                        jnp.max(s_prev, axis=-1, keepdims=True)), sink)
            p_cur = jnp.exp(s_cur - m)
            p_prev = jnp.exp(s_prev - m)
            l = jnp.sum(p_cur, axis=-1, keepdims=True) + jnp.sum(p_prev, axis=-1, keepdims=True) + jnp.exp(sink - m)
            o_ref[:, _hs(h)] = (_dot(p_cur, vc_ref[:, _hs(kv)]) + _dot(p_prev, vp_ref[:, _hs(kv)])) / l
            lses.append(m + jnp.log(l))
        lse_ref[...] = _head_column(lses)

    qspec = pl.BlockSpec((w, SWA_W), lambda n: (n, 0))
    cur = pl.BlockSpec((w, SWA_KV_W), lambda n: (n, 0))
    prev = pl.BlockSpec((w, SWA_KV_W), lambda n: (jnp.maximum(n - 1, 0), 0))
    return _call(
        name, body, (nb,), [qspec, prev, cur, prev, cur, pl.BlockSpec((16, 128), lambda n: (0, 0))],
        [qspec, pl.BlockSpec((w, HEAD), lambda n: (n, 0))],
        [jax.ShapeDtypeStruct((t, SWA_W), F32), jax.ShapeDtypeStruct((t, HEAD), F32)],
        [sq, sk, sk, sv, sv, sp], rides=rides)


def _swa_bwd(sq, sk, sv, sp, dmixed, lse, delta, name):
    t = sq.shape[0]
    w = SWA_BLOCK
    nb = t // w
    group = SWA_H // SWA_KV
    do_block = FOX_W // SWA_W
    assert FOX_W % SWA_W == 0

    def body(q_ref, kp_ref, kc_ref, vp_ref, vc_ref, sp_ref, do_ref, lse_ref, dl_ref,
             dq_ref, dk_ref, dv_ref, dsp_ref, ck, cv):
        step = pl.program_id(0)
        first = step == nb - 1

        @pl.when(step == 0)
        def _():
            ck[...] = jnp.zeros_like(ck)
            cv[...] = jnp.zeros_like(cv)
            dsp_ref[...] = jnp.zeros_like(dsp_ref)

        dk_cur = [None] * SWA_KV
        dk_prev = [None] * SWA_KV
        dv_cur = [None] * SWA_KV
        dv_prev = [None] * SWA_KV
        dsinks = []

        def add(lst, i, v):
            lst[i] = v if lst[i] is None else lst[i] + v

        for h in range(SWA_H):
            kv = h // group
            q, do = q_ref[:, _hs(h)], do_ref[:, _hs(h)]
            kc, kp, vc, vp = kc_ref[:, _hs(kv)], kp_ref[:, _hs(kv)], vc_ref[:, _hs(kv)], vp_ref[:, _hs(kv)]
            s_cur, s_prev = _swa_logits(q, kc, kp, _slope(h), first)
            lse_h = lse_ref[:, h:h + 1]
            dl_h = dl_ref[:, FOX_H + h:FOX_H + h + 1]
            p_cur = jnp.exp(s_cur - lse_h)
            p_prev = jnp.exp(s_prev - lse_h)
            p_sink = jnp.exp(sp_ref[R_SINK:R_SINK + 1, h:h + 1] - lse_h)
            ds_cur = p_cur * (_dot(do, vc, "nt") - dl_h)
            ds_prev = p_prev * (_dot(do, vp, "nt") - dl_h)
            dq_ref[:, _hs(h)] = (_dot(ds_cur, kc) + _dot(ds_prev, kp)) * SCALE
            add(dk_cur, kv, _dot(ds_cur, q, "tn") * SCALE)
            add(dk_prev, kv, _dot(ds_prev, q, "tn") * SCALE)
            add(dv_cur, kv, _dot(p_cur, do, "tn"))
            add(dv_prev, kv, _dot(p_prev, do, "tn"))
            dsinks.append(-jnp.sum(p_sink * dl_h, axis=0, keepdims=True))
        for kv in range(SWA_KV):
            dk_ref[:, _hs(kv)] = dk_cur[kv] + ck[:, _hs(kv)]
            dv_ref[:, _hs(kv)] = dv_cur[kv] + cv[:, _hs(kv)]
            ck[:, _hs(kv)] = dk_prev[kv]
            cv[:, _hs(kv)] = dv_prev[kv]
        lane = lax.broadcasted_iota(jnp.int32, (1, HEAD), 1)
        row = jnp.zeros((1, HEAD), F32)
        for h in range(SWA_H):
            row = jnp.where(lane == h, dsinks[h], row)
        dsp_ref[R_SINK:R_SINK + 1, :] += row

    def rev(n):
        return nb - 1 - n

    qspec = pl.BlockSpec((w, SWA_W), lambda n: (rev(n), 0))
    cur = pl.BlockSpec((w, SWA_KV_W), lambda n: (rev(n), 0))
    prev = pl.BlockSpec((w, SWA_KV_W), lambda n: (jnp.maximum(rev(n) - 1, 0), 0))
    col = pl.BlockSpec((w, HEAD), lambda n: (rev(n), 0))
    small = pl.BlockSpec((16, 128), lambda n: (0, 0))
    return pl.pallas_call(
        body, grid=(nb,),
        in_specs=[qspec, prev, cur, prev, cur, small, pl.BlockSpec((w, SWA_W), lambda n: (rev(n), do_block)), col, col],
        out_specs=[qspec, cur, cur, small],
        out_shape=[jax.ShapeDtypeStruct((t, SWA_W), F32), jax.ShapeDtypeStruct((t, SWA_KV_W), F32),
                   jax.ShapeDtypeStruct((t, SWA_KV_W), F32), jax.ShapeDtypeStruct((16, 128), F32)],
        scratch_shapes=[pltpu.VMEM((w, SWA_KV_W), F32), pltpu.VMEM((w, SWA_KV_W), F32)],
        compiler_params=_params(1), name=name)(sq, sk, sk, sv, sv, sp, dmixed, lse, delta)


def _mem_pre(mkv, sp, name):
    m = mkv.shape[0]

    def body(x_ref, sp_ref, k_ref, v_ref):
        for h in range(MEM_H):
            k_ref[:, _hs(h)] = _head_norm(x_ref[:, _hs(h)], sp_ref[R_MK:R_MK + 1, :]).astype(BF16)
        v_ref[...] = x_ref[:, MEM_W:2 * MEM_W].astype(BF16)

    out = jax.ShapeDtypeStruct((m, MEM_W), BF16)
    return pl.pallas_call(body, out_shape=[out, out], name=name)(mkv, sp)


def _mem_post_bwd(mkv, sp, dmk, dmv, name):
    m = mkv.shape[0]

    def body(x_ref, sp_ref, dk_ref, dv_ref, d_ref, dsp_ref):
        dsp_ref[...] = jnp.zeros_like(dsp_ref)
        total = None
        for h in range(MEM_H):
            dx, dg = _head_norm_bwd(x_ref[:, _hs(h)], sp_ref[R_MK:R_MK + 1, :], dk_ref[:, _hs(h)])
            d_ref[:, _hs(h)] = dx.astype(BF16)
            total = dg if total is None else total + dg
        d_ref[:, MEM_W:2 * MEM_W] = dv_ref[...].astype(BF16)
        dsp_ref[R_MK:R_MK + 1, :] = total

    return pl.pallas_call(body, out_shape=[jax.ShapeDtypeStruct((m, 2 * MEM_W), BF16),
                                           jax.ShapeDtypeStruct((16, 128), F32)], name=name)(mkv, sp, dmk, dmv)


def _mem_fwd(mq, mk, mv, name):
    t = mq.shape[0]
    m = mk.shape[0]
    tq = min(t, 512)

    def body(q_ref, k_ref, v_ref, o_ref, lse_ref):
        lses = []
        for h in range(MEM_H):
            s = _dot(q_ref[:, _hs(h)], k_ref[:, _hs(h)], "nt") * SCALE
            mx = jnp.max(s, axis=-1, keepdims=True)
            p = jnp.exp(s - mx)
            l = jnp.sum(p, axis=-1, keepdims=True)
            o_ref[:, _hs(h)] = _dot(p, v_ref[:, _hs(h)]) / l
            lses.append(mx + jnp.log(l))
        lse_ref[...] = _head_column(lses)

    qspec = pl.BlockSpec((tq, MEM_W), lambda i: (i, 0))
    kspec = pl.BlockSpec((m, MEM_W), lambda i: (0, 0))
    return pl.pallas_call(
        body, grid=(t // tq,), in_specs=[qspec, kspec, kspec],
        out_specs=[qspec, pl.BlockSpec((tq, HEAD), lambda i: (i, 0))],
        out_shape=[jax.ShapeDtypeStruct((t, MEM_W), F32), jax.ShapeDtypeStruct((t, HEAD), F32)],
        compiler_params=_params(1), name=name)(mq, mk, mv)


def _mem_bwd(mq, mk, mv, dmixed, lse, delta, name):
    t = mq.shape[0]
    m = mk.shape[0]
    tq = min(t, 512)
    do_block = (FOX_W + SWA_W) // MEM_W
    assert (FOX_W + SWA_W) % MEM_W == 0

    def body(q_ref, k_ref, v_ref, do_ref, lse_ref, dl_ref, dq_ref, dk_ref, dv_ref):
        @pl.when(pl.program_id(0) == 0)
        def _():
            dk_ref[...] = jnp.zeros_like(dk_ref)
            dv_ref[...] = jnp.zeros_like(dv_ref)

        for h in range(MEM_H):
            q, k, v, do = q_ref[:, _hs(h)], k_ref[:, _hs(h)], v_ref[:, _hs(h)], do_ref[:, _hs(h)]
            s = _dot(q, k, "nt") * SCALE
            p = jnp.exp(s - lse_ref[:, h:h + 1])
            col = FOX_H + SWA_H + h
            ds = p * (_dot(do, v, "nt") - dl_ref[:, col:col + 1])
            dq_ref[:, _hs(h)] = _dot(ds, k) * SCALE
            dk_ref[:, _hs(h)] += _dot(ds, q, "tn") * SCALE
            dv_ref[:, _hs(h)] += _dot(p, do, "tn")

    qspec = pl.BlockSpec((tq, MEM_W), lambda i: (i, 0))
    kspec = pl.BlockSpec((m, MEM_W), lambda i: (0, 0))
    col = pl.BlockSpec((tq, HEAD), lambda i: (i, 0))
    return pl.pallas_call(
        body, grid=(t // tq,),
        in_specs=[qspec, kspec, kspec, pl.BlockSpec((tq, MEM_W), lambda i: (i, do_block)), col, col],
        out_specs=[qspec, kspec, kspec],
        out_shape=[jax.ShapeDtypeStruct((t, MEM_W), F32), jax.ShapeDtypeStruct((m, MEM_W), F32),
                   jax.ShapeDtypeStruct((m, MEM_W), F32)],
        compiler_params=_params(1), name=name)(mq, mk, mv, dmixed, lse, delta)


def _all_gather(xs, name):
    n = len(xs)

    def body(*refs):
        x_refs, o_refs = refs[:n], refs[n:2 * n]
        send_sems, recv_sems, local_sems = refs[2 * n:]
        x, y, c = _me()
        me, sibling = (x, y, c), (x, y, 1 - c)
        x_nb, y_nb, diag = (1 - x, y, c), (x, 1 - y, c), (1 - x, 1 - y, c)
        relay_from = (x + (1 - c) * (1 - 2 * x), y + c * (1 - 2 * y), c)
        relay_to = (x + c * (1 - 2 * x), y + (1 - c) * (1 - 2 * y), c)

        def copy(a, k, block, to, src=None):
            slot = o_refs[a].at[_lin(block)]
            return pltpu.make_async_remote_copy(
                src_ref=slot if src is None else src, dst_ref=slot, send_sem=send_sems.at[a, k],
                recv_sem=recv_sems.at[a, k], device_id=to, device_id_type=MESH)

        mine = [pltpu.make_async_copy(x_refs[a], o_refs[a].at[_lin(me)], local_sems.at[a]) for a in range(n)]
        for cp in mine:
            cp.start()
        sent = []
        for a in range(n):
            sent += [copy(a, 0, me, sibling, src=x_refs[a]), copy(a, 1, me, x_nb, src=x_refs[a]),
                     copy(a, 2, me, y_nb, src=x_refs[a])]
        for cp in sent:
            cp.start()

        def pass_on(cp):
            cp.start()
            sent.append(cp)

        for a in range(n):
            copy(a, 1, x_nb, me).wait_recv()
            copy(a, 2, y_nb, me).wait_recv()
            pass_on(copy(a, 3, relay_from, relay_to))
            pass_on(copy(a, 4, x_nb, sibling))
            pass_on(copy(a, 5, y_nb, sibling))
        for a in range(n):
            copy(a, 3, diag, me).wait_recv()
            pass_on(copy(a, 6, diag, sibling))
        for a in range(n):
            copy(a, 0, sibling, me).wait_recv()
            for k, chip in ((4, (1 - x, y)), (5, (x, 1 - y)), (6, (1 - x, 1 - y))):
                copy(a, k, (*chip, 1 - c), me).wait_recv()
        for cp in sent:
            cp.wait_send()
        for cp in mine:
            cp.wait()

    return pl.pallas_call(
        body, in_specs=[ANY] * n, out_specs=[ANY] * n,
        out_shape=[jax.ShapeDtypeStruct((N_DEV,) + x.shape, x.dtype) for x in xs],
        scratch_shapes=[pltpu.SemaphoreType.DMA((n, 7)), pltpu.SemaphoreType.DMA((n, 7)),
                        pltpu.SemaphoreType.DMA((n,))],
        name=name)(*xs)


def _peers():
    x, y, c = _me()
    out = []
    for k in range(1, N_DEV):
        kx, ky, kc = (k >> 2) & 1, (k >> 1) & 1, k & 1
        out.append(((1 - x) if kx else x, (1 - y) if ky else y, (1 - c) if kc else c))
    return out


def _all_reduce_small(xs, name):
    n = len(xs)

    def body(*refs):
        x_refs, o_refs = refs[:n], refs[n:2 * n]
        bufs = refs[2 * n:3 * n]
        send_sems, recv_sems = refs[3 * n:]
        me = _lin(_me())
        peers = _peers()
        for a in range(n):
            bufs[a][me] = x_refs[a][...]
        sends = []
        for a in range(n):
            for k, peer in enumerate(peers):
                sends.append(pltpu.make_async_remote_copy(
                    src_ref=bufs[a].at[me], dst_ref=bufs[a].at[me], send_sem=send_sems.at[a, k],
                    recv_sem=recv_sems.at[a, k], device_id=peer, device_id_type=MESH))
        for cp in sends:
            cp.start()
        for a in range(n):
            for k, peer in enumerate(peers):
                pltpu.make_async_remote_copy(
                    src_ref=bufs[a].at[me], dst_ref=bufs[a].at[_lin(peer)], send_sem=send_sems.at[a, k],
                    recv_sem=recv_sems.at[a, k], device_id=peer, device_id_type=MESH).wait_recv()
        for cp in sends:
            cp.wait_send()
        for a in range(n):
            total = bufs[a][0]
            for q in range(1, N_DEV):
                total = total + bufs[a][q]
            o_refs[a][...] = total

    vmem = pl.BlockSpec(memory_space=pltpu.VMEM)
    return pl.pallas_call(
        body, in_specs=[vmem] * n, out_specs=[vmem] * n,
        out_shape=[jax.ShapeDtypeStruct(x.shape, F32) for x in xs],
        scratch_shapes=[pltpu.VMEM((N_DEV,) + x.shape, F32) for x in xs]
        + [pltpu.SemaphoreType.DMA((n, 7)), pltpu.SemaphoreType.DMA((n, 7))],
        name=name)(*xs)


def _pair_add(part, got, name):
    _, rows, cols = part.shape
    tm = _rows_tile(rows, cols * 2, budget=2 << 20)
    core = jnp.reshape(lax.axis_index("c"), (1,)).astype(jnp.int32)

    def body(c_ref, p_ref, g_ref, o_ref):
        o_ref[...] = (p_ref[...].astype(F32) + g_ref[...].astype(F32)).astype(BF16)

    spec = pl.BlockSpec((None, tm, cols), lambda q, i, c: (q, i, 0))
    grid_spec = pltpu.PrefetchScalarGridSpec(
        num_scalar_prefetch=1, grid=(4, rows // tm),
        in_specs=[pl.BlockSpec((None, tm, cols), lambda q, i, c: (2 * q + c[0], i, 0)), spec], out_specs=spec)
    return pl.pallas_call(body, grid_spec=grid_spec, out_shape=jax.ShapeDtypeStruct((4, rows, cols), BF16),
                          compiler_params=_params(2), name=name)(core, part, got)


def _adam_math(w, g, m, v):
    nm = ADAM_B1 * m + (1.0 - ADAM_B1) * g
    nv = ADAM_B2 * v + (1.0 - ADAM_B2) * (g * g)
    m_hat = nm / (1.0 - ADAM_B1 ** ADAM_STEP)
    v_hat = nv / (1.0 - ADAM_B2 ** ADAM_STEP)
    return -ADAM_LR * (m_hat / (jnp.sqrt(v_hat) + ADAM_EPS) + ADAM_WD * w), nm, nv


def _sum_chips(got, name):
    _, rows, cols = got.shape
    tm = _rows_tile(rows, cols * 2 * 4, budget=2 << 20)

    def body(r_ref, o_ref):
        o_ref[...] = ((r_ref[0].astype(F32) + r_ref[1].astype(F32)) + r_ref[2].astype(F32)) + r_ref[3].astype(F32)

    return pl.pallas_call(
        body, grid=(rows // tm,), in_specs=[pl.BlockSpec((4, tm, cols), lambda i: (0, i, 0))],
        out_specs=pl.BlockSpec((tm, cols), lambda i: (i, 0)), out_shape=jax.ShapeDtypeStruct((rows, cols), F32),
        compiler_params=_params(1), name=name)(got)


def _sum_adamw(got, col_block, w, m, v, name):
    _, rows, cols = w.shape
    tm = _rows_tile(rows, cols * 4, budget=2 << 20)

    def body(r_ref, w_ref, m_ref, v_ref, g_ref, d_ref, nm_ref, nv_ref):
        g = ((r_ref[0].astype(F32) + r_ref[1].astype(F32)) + r_ref[2].astype(F32)) + r_ref[3].astype(F32)
        g_ref[...] = g
        d_ref[...], nm_ref[...], nv_ref[...] = _adam_math(w_ref[...], g, m_ref[...], v_ref[...])

    spec = pl.BlockSpec((None, tm, cols), lambda i: (0, i, 0))
    out = jax.ShapeDtypeStruct(w.shape, F32)
    return pl.pallas_call(
        body, grid=(rows // tm,), in_specs=[pl.BlockSpec((4, tm, cols), lambda i: (0, i, col_block)), spec, spec, spec],
        out_specs=[spec] * 4, out_shape=[out] * 4, compiler_params=_params(1), name=name)(got, w, m, v)


def _adamw(w, g, m, v, name):
    rows, cols = w.shape

    def body(w_ref, g_ref, m_ref, v_ref, d_ref, nm_ref, nv_ref):
        d_ref[...], nm_ref[...], nv_ref[...] = _adam_math(w_ref[...], g_ref[...], m_ref[...], v_ref[...])

    tm = _rows_tile(rows, cols * 4, budget=2 << 20, mult=8)
    spec = pl.BlockSpec((tm, cols), lambda i: (i, 0))
    out = jax.ShapeDtypeStruct(w.shape, F32)
    return pl.pallas_call(body, grid=(rows // tm,), in_specs=[spec] * 4, out_specs=[spec] * 3,
                          out_shape=[out] * 3, compiler_params=_params(1), name=name)(w, g, m, v)


def _permute_in(w):
    logit0 = 3 * FOX_W
    pad = jnp.zeros(w.shape[:-1] + (HEAD - N_LOGIT,), w.dtype)
    return jnp.concatenate([w[..., :logit0], w[..., logit0 + N_LOGIT:], w[..., logit0:logit0 + N_LOGIT], pad], axis=-1)


def _unpermute_in(w):
    logit0 = 3 * FOX_W
    return jnp.concatenate([w[..., :logit0], w[..., C_FL:C_FL + N_LOGIT], w[..., logit0:C_FL]], axis=-1)


def _pad_row(v, width):
    return jnp.pad(v, ((0, 0), (0, width - v.shape[1])))


def _pack_small(fq, fk, sq, sk, mq, mk, fb, sinks):
    rows = [fq, fk, sq, sk, mq, mk, _pad_row(fb, HEAD), _pad_row(sinks, HEAD)]
    return jnp.concatenate(rows + [jnp.zeros((8, HEAD), F32)], axis=0)


def _pack_norms(a, b, c, d):
    return jnp.concatenate([a, b, c, d, jnp.zeros((4, a.shape[1]), F32)], axis=0)


def kernel(x, mem, ffn1_norm, ffn1_gate, ffn1_up, ffn1_down, mix_norm, mem_norm, w_in, forget_bias, w_mem_k, w_mem_v, fox_q_gain, fox_k_gain, swa_q_gain, swa_k_gain, swa_sinks, mem_q_gain, mem_k_gain, w_out, ffn2_norm, ffn2_gate, ffn2_up, ffn2_down, loss_target, m_ffn1_norm, m_ffn1_gate, m_ffn1_up, m_ffn1_down, m_mix_norm, m_mem_norm, m_w_in, m_forget_bias, m_w_mem_k, m_w_mem_v, m_fox_q_gain, m_fox_k_gain, m_swa_q_gain, m_swa_k_gain, m_swa_sinks, m_mem_q_gain, m_mem_k_gain, m_w_out, m_ffn2_norm, m_ffn2_gate, m_ffn2_up, m_ffn2_down, v_ffn1_norm, v_ffn1_gate, v_ffn1_up, v_ffn1_down, v_mix_norm, v_mem_norm, v_w_in, v_forget_bias, v_w_mem_k, v_w_mem_v, v_fox_q_gain, v_fox_k_gain, v_swa_q_gain, v_swa_k_gain, v_swa_sinks, v_mem_q_gain, v_mem_k_gain, v_w_out, v_ffn2_norm, v_ffn2_gate, v_ffn2_up, v_ffn2_down):
    x0 = x[0]
    mem0 = mem[0]
    target = loss_target[0]
    t, d = x0.shape
    d_shard = w_in.shape[1]
    m_len = mem0.shape[0]
    tm = min(t, 512)
    tk = min(t, 2048)
    tn = IN_W // 3
    tkw, tnw = min(t, 1024), IN_W // 3

    def swap(a):
        return jnp.swapaxes(a, 1, 2)

    gate1, up1, gate2, up2 = swap(ffn1_gate), swap(ffn1_up), swap(ffn2_gate), swap(ffn2_up)

    local = {
        "g1": gate1[0], "u1": up1[0], "d1": ffn1_down[0],
        "g2": gate2[0], "u2": up2[0], "d2": ffn2_down[0],
        "in": _permute_in(w_in[0]), "out": w_out[0],
        "mkv": jnp.concatenate([w_mem_k[0], w_mem_v[0]], axis=1),
    }
    shard = {k: _cast_bf16(v, f"cast_{k}") for k, v in local.items()}
    sp = _pack_small(fox_q_gain, fox_k_gain, swa_q_gain, swa_k_gain, mem_q_gain, mem_k_gain, forget_bias, swa_sinks)
    wt = {}

    wt["g1"], wt["u1"] = _all_gather([shard["g1"], shard["u1"]], "gather_ffn1_in")
    xn1 = _rms_fwd(x0, ffn1_norm, "ffn1_norm")
    half_fs = shard["g2"].shape[0] // 2
    (a1, b1, h1), ((wt["d1"], wt["in"]),) = _ffn_up(
        xn1, wt["g1"], wt["u1"], "ffn1", rides=[_ride_gather([shard["d1"], shard["in"]], 0.87)])
    tc = min(t, CONTRACT_ROWS)
    (x1, hn), ((wt["out"], wt["mkv"]),) = _ffn_down(
        x0, h1, wt["d1"], "ffn1", rides=[_ride_gather([shard["out"], shard["mkv"]], 0.6)],
        tail=_tail_norm(mix_norm, t, d, tc))
    w_in_full = wt["in"].reshape(d, IN_W)

    proj, (half,) = _mm(
        "proj", [(hn, pl.BlockSpec((tm, d), lambda n, i, k: (i, 0)),
                  w_in_full, pl.BlockSpec((d, tn), lambda n, i, k: (0, n)))],
        "nn", (3, t // tm, 1), jax.ShapeDtypeStruct((t, IN_W), F32), pl.BlockSpec((tm, tn), lambda n, i, k: (i, n)),
        rides=[_ride_gather_chips([shard["g2"]], rows=(0, half_fs))])
    w_out_full = wt["out"].reshape(d, d)
    w_mkv_full = wt["mkv"].reshape(d, 2 * MEM_W)
    fq, fk, fv, sq, sk, sv, mq, c_col = _attn_pre(proj, sp, "attn_pre")
    c_row = jnp.transpose(c_col[:, :8])
    c_rep = jnp.broadcast_to(c_row[:FOX_H, :, None], (FOX_H, t, HEAD))

    mn = _rms_fwd(mem0, mem_norm, "mem_norm")
    mkv = _mm("mem_kv", [(mn, pl.BlockSpec((m_len, d), lambda k: (0, 0)),
                          w_mkv_full, pl.BlockSpec((d, 2 * MEM_W), lambda k: (0, 0)))],
              "nn", (1,), jax.ShapeDtypeStruct((m_len, 2 * MEM_W), F32),
              pl.BlockSpec((m_len, 2 * MEM_W), lambda k: (0, 0)))
    mk, mv = _mem_pre(mkv, sp, "mem_pre")

    (o_a, lse_a), (half, half_u2) = _fox_fwd(
        fq, fk, fv, c_rep, c_row, "fox_fwd",
        rides=[_ride_gather_chips([shard["g2"]], rows=(half_fs, half_fs), into=half),
               _ride_gather_chips([shard["u2"]], rows=(0, half_fs))])
    (o_b, lse_b), ((wt["g2"],), half_u2) = _swa_fwd(
        sq, sk, sv, sp, "swa_fwd",
        rides=[_ride_gather_sibling(half), _ride_gather_chips([shard["u2"]], rows=(half_fs, half_fs), into=half_u2)])
    o_c, lse_c = _mem_fwd(mq, mk, mv, "mem_fwd")

    def rows_spec(width):
        return pl.BlockSpec((tm, width), lambda i, k: (i, 0))

    def wout_rows(first, width):
        assert first % width == 0
        return pl.BlockSpec((width, d), lambda i, k: (first // width, 0), pipeline_mode=pl.Buffered(1))

    xspec = pl.BlockSpec((tm, d), lambda i, k: (i, 0))
    (x2, xn2), ((wt["u2"],),) = _mm(
        "mix_out",
        [(o_a, rows_spec(FOX_W), w_out_full, wout_rows(0, FOX_W)),
         (o_b, rows_spec(SWA_W), w_out_full, wout_rows(FOX_W, SWA_W)),
         (o_c, rows_spec(MEM_W), w_out_full, wout_rows(FOX_W + SWA_W, MEM_W))],
        "nn", (t // tm, 1), jax.ShapeDtypeStruct((t, d), F32), xspec, res=x1, res_spec=xspec,
        rides=[_ride_gather_sibling(half_u2)], tail=_tail_norm(ffn2_norm, t, d, tm))

    (a2, b2, h2), ((wt["d2"],),) = _ffn_up(xn2, wt["g2"], wt["u2"], "ffn2", rides=[_ride_gather([shard["d2"]], 0.75)])
    dy, dyb, sq_err = _ffn_down(x2, h2, wt["d2"], "ffn2", tail=_tail_loss(target, t, d, tc))
    loss = lax.psum(0.5 * sq_err[0, 0] / d, ("x", "y", "c"))

    got = {}
    paired = {}
    landed = {}

    def pair(k, part):
        paired[k] = _pair_add(part, got[k], f"pair_{k}")

    (dg2, du2), _ = _ffn_dact(dyb, wt["d2"], a2, b2, "ffn2")
    part_d2 = _ffn_dw(h2, dyb, 0.5, "ffn2_dwd")
    part_g2, ((got["d2"],),) = _ffn_dw(dg2, xn2, 1.0, "ffn2_dwg", rides=[_ride_scatter_sibling([part_d2])])
    pair("d2", part_d2)
    half_rows = part_d2.shape[1] // 2
    first, second = (0, half_rows), (half_rows, half_rows)
    part_u2, (half, (got["g2"],)) = _ffn_dw(
        du2, xn2, 1.0, "ffn2_dwu",
        rides=[_ride_scatter_chips([paired["d2"]], first), _ride_scatter_sibling([part_g2])])
    pair("g2", part_g2)
    dxn2, ((landed["d2"],),) = _ffn_contract(
        dg2, wt["g2"], "ffn2_dxn_g", rides=[_ride_scatter_chips([paired["d2"]], second, into=half)])
    (dx2, dx2b, dgain_ffn2), (half_g2, (got["u2"],)) = _ffn_contract(
        du2, wt["u2"], "ffn2_dxn_u", res=dxn2,
        rides=[_ride_scatter_chips([paired["g2"]], first), _ride_scatter_sibling([part_u2])],
        tail=_tail_norm_bwd(x2, ffn2_norm, dy, t, d, tc))
    pair("u2", part_u2)

    dmixed = _mm("mix_out_dx", [(dx2b, xspec, w_out_full, pl.BlockSpec((d, d), lambda i, k: (0, 0)))],
                 "nt", (t // tm, 1), jax.ShapeDtypeStruct((t, d), F32), xspec)

    def k_rows(width):
        return pl.BlockSpec((tk, width), lambda j, k: (k, 0))

    part_out = [
        _mm(f"mix_out_dw{i}", [(o, k_rows(width), dx2b, k_rows(d))], "tn", (1, t // tk),
            jax.ShapeDtypeStruct((width, d), BF16), pl.BlockSpec((width, d), lambda j, k: (0, 0)))
        for i, (o, width) in enumerate(((o_a, FOX_W), (o_b, SWA_W), (o_c, MEM_W)))
    ]
    part_out = jnp.concatenate(part_out, axis=0).reshape(N_DEV, d_shard, d)

    delta, delta_rep = _delta(dmixed, o_a, o_b, o_c, "attn_delta")
    (dfq, dfk, dfv, dc_col, dc_row), ((landed["g2"],), (landed["u2"],)) = _fox_bwd(
        fq, fk, fv, c_rep, c_row, dmixed, lse_a, delta_rep, "fox_bwd",
        rides=[_ride_scatter_chips([paired["g2"]], second, into=half_g2), _ride_scatter_chips([paired["u2"]])])
    dsq, dsk, dsv, dsp_sink = _swa_bwd(sq, sk, sv, sp, dmixed, lse_b, delta, "swa_bwd")
    dmq, dmk, dmv = _mem_bwd(mq, mk, mv, dmixed, lse_c, delta, "mem_bwd")

    dmkv, dsp_mem = _mem_post_bwd(mkv, sp, dmk, dmv, "mem_post_bwd")
    part_mkv = _mm("mem_kv_dw", [(mn, pl.BlockSpec((m_len, d), lambda k: (0, 0)),
                                  dmkv, pl.BlockSpec((m_len, 2 * MEM_W), lambda k: (0, 0)))],
                   "tn", (1,), jax.ShapeDtypeStruct((d, 2 * MEM_W), BF16),
                   pl.BlockSpec((d, 2 * MEM_W), lambda k: (0, 0))).reshape(N_DEV, d_shard, 2 * MEM_W)
    dmn = _mm("mem_kv_dx", [(dmkv, pl.BlockSpec((m_len, 2 * MEM_W), lambda k: (0, 0)),
                             w_mkv_full, pl.BlockSpec((d, 2 * MEM_W), lambda k: (0, 0)))],
              "nt", (1,), jax.ShapeDtypeStruct((m_len, d), F32), pl.BlockSpec((m_len, d), lambda k: (0, 0)))
    _, _, dgain_mem = _rms_bwd(mem0, mem_norm, dmn, None, "mem_norm_bwd")

    dc_row_t = _pad_row(jnp.transpose(dc_row), HEAD)
    dproj, dsp_attn = _attn_post_bwd(proj, sp, dfq, dfk, dfv, dsq, dsk, dsv, dmq, dc_col, dc_row_t, "attn_post_bwd")
    (dx1, dx1b, dgain_mix), ((got["out"], got["mkv"]),) = _mm(
        "proj_dx", [(dproj, pl.BlockSpec((tc, IN_W), lambda i, k: (i, 0)),
                     w_in_full, pl.BlockSpec((d, IN_W), lambda i, k: (0, 0), pipeline_mode=pl.Buffered(1)))],
        "nt", (t // tc, 1), jax.ShapeDtypeStruct((t, d), F32), pl.BlockSpec((tc, d), lambda i, k: (i, 0)),
        rides=[_ride_scatter_sibling([part_out, part_mkv])], tail=_tail_norm_bwd(x1, mix_norm, dx2, t, d, tc))
    pair("out", part_out)
    pair("mkv", part_mkv)
    part_in, ((landed["out"], landed["mkv"]),) = _mm(
        "proj_dw", [(hn, pl.BlockSpec((tkw, d), lambda n, k: (k, 0)),
                     dproj, pl.BlockSpec((tkw, tnw), lambda n, k: (k, n)))],
        "tn", (IN_W // tnw, t // tkw), jax.ShapeDtypeStruct((d, IN_W), BF16), pl.BlockSpec((d, tnw), lambda n, k: (0, n)),
        rides=[_ride_scatter_chips([paired["out"], paired["mkv"]])])
    part_in = part_in.reshape(N_DEV, d_shard, IN_W)

    part_d1, ((got["in"],),) = _ffn_dw(h1, dx1b, 0.5, "ffn1_dwd", rides=[_ride_scatter_sibling([part_in])])
    pair("in", part_in)
    (dg1, du1), ((landed["in"],), (got["d1"],)) = _ffn_dact(
        dx1b, wt["d1"], a1, b1, "ffn1",
        rides=[_ride_scatter_chips([paired["in"]]), _ride_scatter_sibling([part_d1])])
    pair("d1", part_d1)
    part_g1, (half_d1,) = _ffn_dw(dg1, xn1, 1.0, "ffn1_dwg", rides=[_ride_scatter_chips([paired["d1"]], first)])
    part_u1, ((landed["d1"],), (got["g1"],)) = _ffn_dw(
        du1, xn1, 1.0, "ffn1_dwu",
        rides=[_ride_scatter_chips([paired["d1"]], second, into=half_d1), _ride_scatter_sibling([part_g1])])
    pair("g1", part_g1)
    dxn1, ((landed["g1"],), (got["u1"],)) = _ffn_contract(
        dg1, wt["g1"], "ffn1_dxn_g", rides=[_ride_scatter_chips([paired["g1"]]), _ride_scatter_sibling([part_u1])])
    pair("u1", part_u1)
    (grad_x, _, dgain_ffn1), ((landed["u1"],),) = _ffn_contract(
        du1, wt["u1"], "ffn1_dxn_u", res=dxn1, rides=[_ride_scatter_chips([paired["u1"]])],
        tail=_tail_norm_bwd(x0, ffn1_norm, dx1, t, d, tc))

    norms_sum, small_sum = _all_reduce_small(
        [_pack_norms(dgain_ffn1, dgain_mix, dgain_mem, dgain_ffn2), dsp_attn + dsp_sink + dsp_mem], "reduce_small")

    result = {
        "ffn1_gate": map(swap, _sum_adamw(landed["g1"], 0, gate1, swap(m_ffn1_gate), swap(v_ffn1_gate), "adamw_ffn1_gate")),
        "ffn1_up": map(swap, _sum_adamw(landed["u1"], 0, up1, swap(m_ffn1_up), swap(v_ffn1_up), "adamw_ffn1_up")),
        "ffn1_down": _sum_adamw(landed["d1"], 0, ffn1_down, m_ffn1_down, v_ffn1_down, "adamw_ffn1_down"),
        "w_mem_k": _sum_adamw(landed["mkv"], 0, w_mem_k, m_w_mem_k, v_w_mem_k, "adamw_w_mem_k"),
        "w_mem_v": _sum_adamw(landed["mkv"], 1, w_mem_v, m_w_mem_v, v_w_mem_v, "adamw_w_mem_v"),
        "w_out": _sum_adamw(landed["out"], 0, w_out, m_w_out, v_w_out, "adamw_w_out"),
        "ffn2_gate": map(swap, _sum_adamw(landed["g2"], 0, gate2, swap(m_ffn2_gate), swap(v_ffn2_gate), "adamw_ffn2_gate")),
        "ffn2_up": map(swap, _sum_adamw(landed["u2"], 0, up2, swap(m_ffn2_up), swap(v_ffn2_up), "adamw_ffn2_up")),
        "ffn2_down": _sum_adamw(landed["d2"], 0, ffn2_down, m_ffn2_down, v_ffn2_down, "adamw_ffn2_down"),
    }
    grad_in = _unpermute_in(_sum_chips(landed["in"], "sum_w_in"))
    result["w_in"] = (grad_in[None],) + tuple(
        o[None] for o in _adamw(w_in[0], grad_in, m_w_in[0], v_w_in[0], "adamw_w_in"))

    norm_names = ["ffn1_norm", "mix_norm", "mem_norm", "ffn2_norm"]
    norm_w = _pack_norms(ffn1_norm, mix_norm, mem_norm, ffn2_norm)
    norm_m = _pack_norms(m_ffn1_norm, m_mix_norm, m_mem_norm, m_ffn2_norm)
    norm_v = _pack_norms(v_ffn1_norm, v_mix_norm, v_mem_norm, v_ffn2_norm)
    outs = (norms_sum,) + tuple(_adamw(norm_w, norms_sum, norm_m, norm_v, "adamw_norms"))
    for i, k in enumerate(norm_names):
        result[k] = tuple(o[i:i + 1] for o in outs)

    small_names = ["fox_q_gain", "fox_k_gain", "swa_q_gain", "swa_k_gain", "mem_q_gain", "mem_k_gain",
                   "forget_bias", "swa_sinks"]
    small_m = _pack_small(m_fox_q_gain, m_fox_k_gain, m_swa_q_gain, m_swa_k_gain, m_mem_q_gain, m_mem_k_gain,
                          m_forget_bias, m_swa_sinks)
    small_v = _pack_small(v_fox_q_gain, v_fox_k_gain, v_swa_q_gain, v_swa_k_gain, v_mem_q_gain, v_mem_k_gain,
                          v_forget_bias, v_swa_sinks)
    outs = (small_sum,) + tuple(_adamw(sp, small_sum, small_m, small_v, "adamw_small"))
    for i, k in enumerate(small_names):
        width = N_LOGIT if k in ("forget_bias", "swa_sinks") else HEAD
        result[k] = tuple(o[i:i + 1, :width] for o in outs)

    order = ["ffn1_norm", "ffn1_gate", "ffn1_up", "ffn1_down", "mix_norm", "mem_norm", "w_in", "forget_bias",
             "w_mem_k", "w_mem_v", "fox_q_gain", "fox_k_gain", "swa_q_gain", "swa_k_gain", "swa_sinks",
             "mem_q_gain", "mem_k_gain", "w_out", "ffn2_norm", "ffn2_gate", "ffn2_up", "ffn2_down"]
    result = {k: tuple(v) for k, v in result.items()}
    flat = [loss, grad_x[None]]
    for kind in range(4):
        flat += [result[k][kind] for k in order]
    return tuple(flat)
```

```python
import functools

import jax
import jax.numpy as jnp
from jax import lax
from jax.experimental import pallas as pl
from jax.experimental.pallas import tpu as pltpu

F32 = jnp.float32
BF16 = jnp.bfloat16
MESH = pl.DeviceIdType.MESH
ANY = pl.BlockSpec(memory_space=pl.ANY)

N_DEV = 8
EPS = 1e-6
NEG_INF = -1e30
HEAD = 128
FOX_H, SWA_H, SWA_KV, MEM_H = 6, 6, 2, 4
FOX_W, SWA_W, SWA_KV_W, MEM_W = FOX_H * HEAD, SWA_H * HEAD, SWA_KV * HEAD, MEM_H * HEAD
SCALE = HEAD ** -0.5
SWA_BLOCK = 128
C_FQ, C_FK, C_FV = 0, FOX_W, 2 * FOX_W
C_SQ = 3 * FOX_W
C_SK = C_SQ + SWA_W
C_SV = C_SK + SWA_KV_W
C_MQ = C_SV + SWA_KV_W
C_FL = C_MQ + MEM_W
IN_W = C_FL + HEAD
N_LOGIT = FOX_H
R_FQ, R_FK, R_SQ, R_SK, R_MQ, R_MK, R_FB, R_SINK = range(8)
ADAM_LR, ADAM_B1, ADAM_B2, ADAM_EPS, ADAM_WD, ADAM_STEP = 0.001, 0.9, 0.999, 1e-08, 0.01, 10
VMEM_BYTES = 56 * 1024 * 1024

DN = {
    "nn": (((1,), (0,)), ((), ())),
    "nt": (((1,), (1,)), ((), ())),
    "tn": (((0,), (0,)), ((), ())),
}


def _params(n_axes):
    return pltpu.CompilerParams(dimension_semantics=("arbitrary",) * n_axes, vmem_limit_bytes=VMEM_BYTES)


def _dot(a, b, dims="nn"):
    return lax.dot_general(a.astype(BF16), b.astype(BF16), DN[dims], preferred_element_type=F32)


def _sigmoid(x):
    return 0.5 * jnp.tanh(0.5 * x) + 0.5


def _me():
    return lax.axis_index("x"), lax.axis_index("y"), lax.axis_index("c")


def _lin(p):
    return 4 * p[0] + 2 * p[1] + p[2]


def _rows_tile(rows, row_bytes, budget=4 << 20, mult=16):
    best = None
    for k in range(1, rows + 1):
        if rows % k == 0 and (rows // k) % mult == 0 and (rows // k) * row_bytes <= budget:
            best = rows // k
            break
    assert best is not None, (rows, row_bytes)
    return best


class _Ride:
    def __init__(self, inputs, out_shapes, aliases, n_remote, n_local, start, wait):
        self.inputs, self.out_shapes, self.aliases = list(inputs), list(out_shapes), dict(aliases)
        self.n_remote, self.n_local, self.start, self.wait = n_remote, n_local, start, wait


def _remote(src, dst, send, recv, k, to):
    return pltpu.make_async_remote_copy(src_ref=src, dst_ref=dst, send_sem=send.at[k], recv_sem=recv.at[k],
                                        device_id=to, device_id_type=MESH)


def _other_chips(x, y):
    return [(1 - x, y), (x, 1 - y), (1 - x, 1 - y)]


ALL_CHIPS = [(0, 0), (0, 1), (1, 0), (1, 1)]


def _rows_of(ref, rows, slot=None):
    if slot is None:
        return ref if rows is None else ref.at[pl.ds(rows[0], rows[1])]
    return ref.at[slot] if rows is None else ref.at[slot, pl.ds(rows[0], rows[1])]


def _ride_gather_chips(xs, rows=None, into=None):
    n = len(xs)

    def copies(ins, outs, send, recv):
        x, y, c = _me()
        out = []
        for a in range(n):
            for j, chip in enumerate(_other_chips(x, y)):
                peer = (*chip, c)
                src = _rows_of(ins[a], rows)
                out.append((_remote(src, _rows_of(outs[a], rows, _lin((x, y, c))), send, recv, 3 * a + j, peer),
                            _remote(src, _rows_of(outs[a], rows, _lin(peer)), send, recv, 3 * a + j, peer)))
        return out

    def mine(ins, outs, local):
        me = _lin(_me())
        return [pltpu.make_async_copy(_rows_of(ins[a], rows), _rows_of(outs[a], rows, me), local.at[a])
                for a in range(n)]

    def start(ins, outs, send, recv, local):
        for cp in mine(ins, outs, local):
            cp.start()
        for sent, _ in copies(ins, outs, send, recv):
            sent.start()

    def wait(ins, outs, send, recv, local):
        for sent, landed in copies(ins, outs, send, recv):
            landed.wait_recv()
            sent.wait_send()
        for cp in mine(ins, outs, local):
            cp.wait()

    shapes = [jax.ShapeDtypeStruct((N_DEV,) + x.shape, x.dtype) for x in xs]
    if into is None:
        return _Ride(xs, shapes, {}, 3 * n, n, start, wait)
    return _Ride(list(xs) + list(into), shapes, {n + a: a for a in range(n)}, 3 * n, n, start, wait)


def _ride_gather_sibling(bufs):
    n = len(bufs)

    def copies(outs, send, recv):
        x, y, c = _me()
        out = []
        for a in range(n):
            for q, (px, py) in enumerate(ALL_CHIPS):
                there = outs[a].at[4 * px + 2 * py + c]
                here = outs[a].at[4 * px + 2 * py + 1 - c]
                out.append((_remote(there, there, send, recv, 4 * a + q, (x, y, 1 - c)),
                            _remote(here, here, send, recv, 4 * a + q, (x, y, 1 - c))))
        return out

    def start(ins, outs, send, recv, local):
        for sent, _ in copies(outs, send, recv):
            sent.start()

    def wait(ins, outs, send, recv, local):
        for sent, landed in copies(outs, send, recv):
            landed.wait_recv()
            sent.wait_send()

    shapes = [jax.ShapeDtypeStruct(b.shape, b.dtype) for b in bufs]
    return _Ride(bufs, shapes, {a: a for a in range(n)}, 4 * n, 0, start, wait)


def _ride_gather(xs, mid_frac, rows=None, into=None):
    n = len(xs)
    chips = _ride_gather_chips(xs, rows, into)

    def sibling_copies(outs, send, recv):
        x, y, c = _me()
        out = []
        for a in range(n):
            for q, (px, py) in enumerate(ALL_CHIPS):
                there = _rows_of(outs[a], rows, 4 * px + 2 * py + c)
                here = _rows_of(outs[a], rows, 4 * px + 2 * py + 1 - c)
                k = 3 * n + 4 * a + q
                out.append((_remote(there, there, send, recv, k, (x, y, 1 - c)),
                            _remote(here, here, send, recv, k, (x, y, 1 - c))))
        return out

    def mid(ins, outs, send, recv, local):
        chips.wait(ins, outs, send, recv, local)
        for sent, _ in sibling_copies(outs, send, recv):
            sent.start()

    def wait(ins, outs, send, recv, local):
        for sent, landed in sibling_copies(outs, send, recv):
            landed.wait_recv()
            sent.wait_send()

    ride = _Ride(chips.inputs, chips.out_shapes, chips.aliases, 7 * n, n, chips.start, wait)
    ride.mid, ride.mid_frac = mid, mid_frac
    return ride


def _ride_scatter_sibling(parts):
    n = len(parts)

    def copies(ins, outs, send, recv):
        x, y, c = _me()
        out = []
        for a in range(n):
            for q, (px, py) in enumerate(ALL_CHIPS):
                cp = _remote(ins[a].at[4 * px + 2 * py + 1 - c], outs[a].at[q], send, recv, 4 * a + q, (x, y, 1 - c))
                out.append(cp)
        return out

    def start(ins, outs, send, recv, local):
        for cp in copies(ins, outs, send, recv):
            cp.start()

    def wait(ins, outs, send, recv, local):
        for cp in copies(ins, outs, send, recv):
            cp.wait_recv()
            cp.wait_send()

    shapes = [jax.ShapeDtypeStruct((4,) + p.shape[1:], p.dtype) for p in parts]
    return _Ride(parts, shapes, {}, 4 * n, 0, start, wait)


def _ride_scatter_chips(pairs, rows=None, into=None):
    n = len(pairs)

    def part(ref, slot):
        return _rows_of(ref, rows, slot)

    def copies(ins, outs, send, recv):
        x, y, c = _me()
        out = []
        for a in range(n):
            for j, (px, py) in enumerate(_other_chips(x, y)):
                peer = (px, py, c)
                src = part(ins[a], 2 * px + py)
                out.append((_remote(src, part(outs[a], 2 * x + y), send, recv, 3 * a + j, peer),
                            _remote(src, part(outs[a], 2 * px + py), send, recv, 3 * a + j, peer)))
        return out

    def mine(ins, outs, local):
        x, y, _ = _me()
        return [pltpu.make_async_copy(part(ins[a], 2 * x + y), part(outs[a], 2 * x + y), local.at[a])
                for a in range(n)]

    def start(ins, outs, send, recv, local):
        for cp in mine(ins, outs, local):
            cp.start()
        for sent, _ in copies(ins, outs, send, recv):
            sent.start()

    def wait(ins, outs, send, recv, local):
        for sent, landed in copies(ins, outs, send, recv):
            landed.wait_recv()
            sent.wait_send()
        for cp in mine(ins, outs, local):
            cp.wait()

    shapes = [jax.ShapeDtypeStruct(p.shape, p.dtype) for p in pairs]
    if into is None:
        return _Ride(pairs, shapes, {}, 3 * n, n, start, wait)
    return _Ride(list(pairs) + list(into), shapes, {n + a: a for a in range(n)}, 3 * n, n, start, wait)


def _call(name, body, grid, in_specs, out_specs, out_shape, operands, scratch=(), rides=()):
    n_in, n_out, n_scr = len(operands), len(out_shape), len(scratch)
    ride_in, ride_out, ride_scr, aliases, spans = [], [], [], {}, []
    for r in rides:
        for i, o in r.aliases.items():
            aliases[n_in + len(ride_in) + i] = n_out + len(ride_out) + o
        spans.append((len(ride_in), len(r.inputs), len(ride_out), len(r.out_shapes)))
        ride_in += r.inputs
        ride_out += r.out_shapes
        ride_scr += [pltpu.SemaphoreType.DMA((r.n_remote,)), pltpu.SemaphoreType.DMA((r.n_remote,)),
                     pltpu.SemaphoreType.DMA((max(r.n_local, 1),))]

    def wrapped(*refs):
        c_in, r_in = refs[:n_in], refs[n_in:n_in + len(ride_in)]
        p = n_in + len(ride_in)
        c_out, r_out = refs[p:p + n_out], refs[p + n_out:p + n_out + len(ride_out)]
        p += n_out + len(ride_out)
        c_scr, r_scr = refs[p:p + n_scr], refs[p + n_scr:]

        n_steps = functools.reduce(lambda a, b: a * b, grid, 1)
        step = functools.reduce(lambda acc, ax: acc * grid[ax] + pl.program_id(ax), range(len(grid)), 0)

        def each(method, at):
            for k, (r, (i0, ni, o0, no)) in enumerate(zip(rides, spans)):
                fn = getattr(r, method, None)
                if fn is None:
                    continue
                run = functools.partial(fn, r_in[i0:i0 + ni], r_out[o0:o0 + no], *r_scr[3 * k:3 * k + 3])
                if grid:
                    pl.when(step == at(r))(run)
                else:
                    run()

        each("start", lambda r: 0)
        each("mid", lambda r: int(r.mid_frac * (n_steps - 1)))
        body(*c_in, *c_out, *c_scr)
        each("wait", lambda r: n_steps - 1)

    outs = pl.pallas_call(
        wrapped, grid=grid, in_specs=list(in_specs) + [ANY] * len(ride_in),
        out_specs=list(out_specs) + [ANY] * len(ride_out), out_shape=list(out_shape) + ride_out,
        scratch_shapes=list(scratch) + ride_scr, input_output_aliases=aliases,
        compiler_params=_params(len(grid)), name=name)(*operands, *ride_in)
    outs = list(outs)
    ride_results = [outs[n_out + o0:n_out + o0 + no] for (_, _, o0, no) in spans]
    return outs[:n_out], ride_results


class _Tail:
    def __init__(self, extra, out_shapes, out_specs, fn):
        self.extra, self.out_shapes, self.out_specs, self.fn = list(extra), list(out_shapes), list(out_specs), fn


def _mm(name, pairs, dims, grid, out_shape, out_spec, res=None, res_spec=None, alpha=1.0, rides=(), tail=None):
    n = len(pairs)
    nk = grid[-1]
    kax = len(grid) - 1
    acc_shape = tuple(d for d in out_spec.block_shape if d is not None)
    n_extra = len(tail.extra) if tail else 0
    n_outs = len(tail.out_shapes) if tail else 1

    def body(*refs):
        pos = 2 * n
        r_ref = None
        if res is not None:
            r_ref = refs[pos]
            pos += 1
        x_refs = refs[pos:pos + n_extra]
        o_refs = refs[pos + n_extra:pos + n_extra + n_outs]
        pos += n_extra + n_outs
        part = None
        for p in range(n):
            d = _dot(refs[2 * p][...], refs[2 * p + 1][...], dims)
            part = d if part is None else part + d

        def finish(acc):
            if alpha != 1.0:
                acc = acc * alpha
            if r_ref is not None:
                acc = r_ref[...] + acc
            if tail:
                tail.fn(acc, x_refs, o_refs)
            else:
                o_refs[0][...] = acc.astype(o_refs[0].dtype)

        if nk == 1:
            finish(part)
        else:
            acc_ref = refs[pos]
            k = pl.program_id(kax)

            @pl.when(k == 0)
            def _():
                acc_ref[...] = part

            @pl.when(k > 0)
            def _():
                acc_ref[...] += part

            @pl.when(k == nk - 1)
            def _():
                finish(acc_ref[...])

    operands, in_specs = [], []
    for a, a_spec, b, b_spec in pairs:
        operands += [a, b]
        in_specs += [a_spec, b_spec]
    if res is not None:
        operands.append(res)
        in_specs.append(res_spec)
    for a, a_spec in (tail.extra if tail else []):
        operands.append(a)
        in_specs.append(a_spec)
    outs, ride_results = _call(name, body, grid, in_specs, tail.out_specs if tail else [out_spec],
                               tail.out_shapes if tail else [out_shape], operands,
                               scratch=[pltpu.VMEM(acc_shape, F32)] if nk > 1 else [], rides=rides)
    outs = outs if tail else outs[0]
    return (outs, ride_results) if rides else outs


def _accumulate(ref, part):
    @pl.when(pl.program_id(0) == 0)
    def _():
        ref[...] = part

    @pl.when(pl.program_id(0) > 0)
    def _():
        ref[...] += part


def _tail_norm(gain, t, d, tm):
    def fn(v, x_refs, o_refs):
        o_refs[0][...] = v
        r = lax.rsqrt(jnp.mean(v * v, axis=-1, keepdims=True) + EPS)
        o_refs[1][...] = (v * r * x_refs[0][...]).astype(BF16)

    rows = pl.BlockSpec((tm, d), lambda i, k: (i, 0))
    return _Tail([(gain, pl.BlockSpec((1, d), lambda i, k: (0, 0)))],
                 [jax.ShapeDtypeStruct((t, d), F32), jax.ShapeDtypeStruct((t, d), BF16)], [rows, rows], fn)


def _tail_loss(target, t, d, tm):
    def fn(v, x_refs, o_refs):
        err = v - x_refs[0][...]
        dy = err * (1.0 / d)
        o_refs[0][...] = dy
        o_refs[1][...] = dy.astype(BF16)
        _accumulate(o_refs[2], jnp.zeros((8, 128), F32) + jnp.sum(err * err))

    rows = pl.BlockSpec((tm, d), lambda i, k: (i, 0))
    return _Tail([(target, rows)],
                 [jax.ShapeDtypeStruct((t, d), F32), jax.ShapeDtypeStruct((t, d), BF16),
                  jax.ShapeDtypeStruct((8, 128), F32)],
                 [rows, rows, pl.BlockSpec((8, 128), lambda i, k: (0, 0))], fn)


def _tail_norm_bwd(x, gain, dres, t, d, tm):
    def fn(dy, x_refs, o_refs):
        xv = x_refs[0][...]
        r = lax.rsqrt(jnp.mean(xv * xv, axis=-1, keepdims=True) + EPS)
        xh = xv * r
        dxh = dy * x_refs[1][...]
        dx = r * (dxh - xh * jnp.mean(dxh * xh, axis=-1, keepdims=True)) + x_refs[2][...]
        o_refs[0][...] = dx
        o_refs[1][...] = dx.astype(BF16)
        _accumulate(o_refs[2], jnp.sum(dy * xh, axis=0, keepdims=True))

    rows = pl.BlockSpec((tm, d), lambda i, k: (i, 0))
    vec = pl.BlockSpec((1, d), lambda i, k: (0, 0))
    return _Tail([(x, rows), (gain, vec), (dres, rows)],
                 [jax.ShapeDtypeStruct((t, d), F32), jax.ShapeDtypeStruct((t, d), BF16),
                  jax.ShapeDtypeStruct((1, d), F32)], [rows, rows, vec], fn)


def _cast_bf16(x, name):
    rows, cols = x.shape
    tm = _rows_tile(rows, cols * 4)

    def body(x_ref, o_ref):
        o_ref[...] = x_ref[...].astype(BF16)

    spec = pl.BlockSpec((tm, cols), lambda i: (i, 0))
    return pl.pallas_call(body, grid=(rows // tm,), in_specs=[spec], out_specs=spec,
                          out_shape=jax.ShapeDtypeStruct(x.shape, BF16), compiler_params=_params(1), name=name)(x)


def _rms_fwd(x, gain, name):
    rows, d = x.shape
    tm = min(rows, 512)

    def body(x_ref, g_ref, o_ref):
        xv = x_ref[...]
        r = lax.rsqrt(jnp.mean(xv * xv, axis=-1, keepdims=True) + EPS)
        o_ref[...] = (xv * r * g_ref[...]).astype(BF16)

    spec = pl.BlockSpec((tm, d), lambda i: (i, 0))
    return pl.pallas_call(body, grid=(rows // tm,), in_specs=[spec, pl.BlockSpec((1, d), lambda i: (0, 0))],
                          out_specs=spec, out_shape=jax.ShapeDtypeStruct(x.shape, BF16),
                          compiler_params=_params(1), name=name)(x, gain)


def _rms_bwd(x, gain, dxn, dres, name, rides=()):
    rows, d = x.shape
    tm = min(rows, 256)
    with_res = dres is not None

    def body(*refs):
        if with_res:
            x_ref, g_ref, dy_ref, r_ref, dx_ref, dxb_ref, dg_ref = refs
        else:
            x_ref, g_ref, dy_ref, dx_ref, dxb_ref, dg_ref = refs
        xv = x_ref[...]
        r = lax.rsqrt(jnp.mean(xv * xv, axis=-1, keepdims=True) + EPS)
        xh = xv * r
        dy = dy_ref[...]
        dxh = dy * g_ref[...]
        dx = r * (dxh - xh * jnp.mean(dxh * xh, axis=-1, keepdims=True))
        if with_res:
            dx = dx + r_ref[...]
        dx_ref[...] = dx
        dxb_ref[...] = dx.astype(BF16)
        part = jnp.sum(dy * xh, axis=0, keepdims=True)

        @pl.when(pl.program_id(0) == 0)
        def _():
            dg_ref[...] = part

        @pl.when(pl.program_id(0) > 0)
        def _():
            dg_ref[...] += part

    spec = pl.BlockSpec((tm, d), lambda i: (i, 0))
    vec = pl.BlockSpec((1, d), lambda i: (0, 0))
    ops = [x, gain, dxn] + ([dres] if with_res else [])
    outs, ride_results = _call(
        name, body, (rows // tm,), [spec, vec, spec] + ([spec] if with_res else []), [spec, spec, vec],
        [jax.ShapeDtypeStruct(x.shape, F32), jax.ShapeDtypeStruct(x.shape, BF16), jax.ShapeDtypeStruct((1, d), F32)],
        ops, rides=rides)
    return (outs, ride_results) if rides else outs


ROW_CHUNK = 256


def _ffn_up(xn, wg, wu, tag, rides=()):
    t, d = xn.shape
    nd, fs, _ = wg.shape
    tm = min(t, 512)
    rc = min(tm, ROW_CHUNK)

    def body(x_ref, wg_ref, wu_ref, a_ref, b_ref, h_ref):
        for r in range(0, tm, rc):
            xv = x_ref[r:r + rc, :]
            g = _dot(xv, wg_ref[...], "nt")
            u = _dot(xv, wu_ref[...], "nt")
            sig = _sigmoid(g)
            silu = g * sig
            a_ref[r:r + rc, :] = (0.5 * u * (sig + silu * (1.0 - sig))).astype(BF16)
            b_ref[r:r + rc, :] = (0.5 * silu).astype(BF16)
            h_ref[r:r + rc, :] = (silu * u).astype(BF16)

    wspec = pl.BlockSpec((None, fs, d), lambda j, i: (j, 0, 0))
    hspec = pl.BlockSpec((None, tm, fs), lambda j, i: (j, i, 0))
    hid = jax.ShapeDtypeStruct((nd, t, fs), BF16)
    return _call(f"{tag}_up", body, (nd, t // tm), [pl.BlockSpec((tm, d), lambda j, i: (i, 0)), wspec, wspec],
                 [hspec] * 3, [hid] * 3, [xn, wg, wu], rides=rides)


CONTRACT_ROWS = 256


def _ffn_contract(hid, w, name, res=None, alpha=1.0, rides=(), tail=None):
    nd, t, fs = hid.shape
    d = w.shape[2]
    tm = min(t, CONTRACT_ROWS)
    xspec = pl.BlockSpec((tm, d), lambda i, k: (i, 0))
    pairs = [(hid, pl.BlockSpec((None, tm, fs), lambda i, k, s=s: (s, i, 0)),
              w, pl.BlockSpec((None, fs, d), lambda i, k, s=s: (s, 0, 0), pipeline_mode=pl.Buffered(1)))
             for s in range(nd)]
    return _mm(name, pairs, "nn", (t // tm, 1), jax.ShapeDtypeStruct((t, d), F32), xspec,
               res=res, res_spec=xspec if res is not None else None, alpha=alpha, rides=rides, tail=tail)


def _ffn_down(x, h, wd, tag, rides=(), tail=None):
    return _ffn_contract(h, wd, f"{tag}_down", res=x, alpha=0.5, rides=rides, tail=tail)


def _ffn_dact(dyb, wd, a, b, tag, rides=()):
    nd, t, fs = a.shape
    d = dyb.shape[1]
    tm = min(t, 512)
    rc = min(tm, ROW_CHUNK)

    def body(dy_ref, wd_ref, a_ref, b_ref, dg_ref, du_ref):
        for r in range(0, tm, rc):
            dh = _dot(dy_ref[r:r + rc, :], wd_ref[...], "nt")
            dg_ref[r:r + rc, :] = (dh * a_ref[r:r + rc, :].astype(F32)).astype(BF16)
            du_ref[r:r + rc, :] = (dh * b_ref[r:r + rc, :].astype(F32)).astype(BF16)

    hspec = pl.BlockSpec((None, tm, fs), lambda j, i: (j, i, 0))
    hid = jax.ShapeDtypeStruct((nd, t, fs), BF16)
    return _call(f"{tag}_dact", body, (nd, t // tm),
                 [pl.BlockSpec((tm, d), lambda j, i: (i, 0)), pl.BlockSpec((None, fs, d), lambda j, i: (j, 0, 0)),
                  hspec, hspec], [hspec] * 2, [hid] * 2, [dyb, wd, a, b], rides=rides)


def _ffn_dw(hid, act, alpha, name, rides=()):
    nd, t, fs = hid.shape
    d = act.shape[1]
    tk = t
    return _mm(name, [(hid, pl.BlockSpec((None, tk, fs), lambda j, k: (j, k, 0)),
                       act, pl.BlockSpec((tk, d), lambda j, k: (k, 0), pipeline_mode=pl.Buffered(1)))],
               "tn", (nd, t // tk),
               jax.ShapeDtypeStruct((nd, fs, d), BF16), pl.BlockSpec((None, fs, d), lambda j, k: (j, 0, 0)),
               alpha=alpha, rides=rides)


def _head_norm(x, gain):
    r = lax.rsqrt(jnp.mean(x * x, axis=-1, keepdims=True) + EPS)
    return x * r * gain


def _head_norm_bwd(x, gain, dy):
    r = lax.rsqrt(jnp.mean(x * x, axis=-1, keepdims=True) + EPS)
    xh = x * r
    dxh = dy * gain
    dx = r * (dxh - xh * jnp.mean(dxh * xh, axis=-1, keepdims=True))
    return dx, jnp.sum(dy * xh, axis=0, keepdims=True)


def _hs(h, base=0):
    return slice(base + h * HEAD, base + (h + 1) * HEAD)


def _tri(n, lower):
    r = lax.broadcasted_iota(jnp.int32, (n, n), 0)
    c = lax.broadcasted_iota(jnp.int32, (n, n), 1)
    return ((r >= c) if lower else (r <= c)).astype(F32)


def _attn_pre(proj, sp, name):
    t = proj.shape[0]
    tm = min(t, 256)

    def body(p_ref, sp_ref, fq, fk, fv, sq, sk, sv, mq, cc, carry):
        @pl.when(pl.program_id(0) == 0)
        def _():
            carry[...] = jnp.zeros_like(carry)

        for h in range(FOX_H):
            fq[:, _hs(h)] = _head_norm(p_ref[:, _hs(h, C_FQ)], sp_ref[R_FQ:R_FQ + 1, :]).astype(BF16)
            fk[:, _hs(h)] = _head_norm(p_ref[:, _hs(h, C_FK)], sp_ref[R_FK:R_FK + 1, :]).astype(BF16)
        fv[...] = p_ref[:, C_FV:C_FV + FOX_W].astype(BF16)
        for h in range(SWA_H):
            sq[:, _hs(h)] = _head_norm(p_ref[:, _hs(h, C_SQ)], sp_ref[R_SQ:R_SQ + 1, :]).astype(BF16)
        for h in range(SWA_KV):
            sk[:, _hs(h)] = _head_norm(p_ref[:, _hs(h, C_SK)], sp_ref[R_SK:R_SK + 1, :]).astype(BF16)
        sv[...] = p_ref[:, C_SV:C_SV + SWA_KV_W].astype(BF16)
        for h in range(MEM_H):
            mq[:, _hs(h)] = _head_norm(p_ref[:, _hs(h, C_MQ)], sp_ref[R_MQ:R_MQ + 1, :]).astype(BF16)
        z = p_ref[:, C_FL:C_FL + HEAD] + sp_ref[R_FB:R_FB + 1, :]
        lane = lax.broadcasted_iota(jnp.int32, z.shape, 1)
        log_f = jnp.minimum(z, 0.0) - jnp.log(1.0 + jnp.exp(-jnp.abs(z)))
        log_f = jnp.where(lane < N_LOGIT, log_f, 0.0)
        c = jnp.dot(_tri(tm, True), log_f, precision=lax.Precision.HIGHEST, preferred_element_type=F32)
        c = c + carry[0:1, :]
        cc[...] = c
        carry[...] = jnp.broadcast_to(c[tm - 1:tm, :], carry.shape)

    def rows(w):
        return pl.BlockSpec((tm, w), lambda i: (i, 0))

    def shape(w, dt):
        return jax.ShapeDtypeStruct((t, w), dt)

    widths = [FOX_W, FOX_W, FOX_W, SWA_W, SWA_KV_W, SWA_KV_W, MEM_W]
    return pl.pallas_call(
        body, grid=(t // tm,), in_specs=[rows(IN_W), pl.BlockSpec((16, 128), lambda i: (0, 0))],
        out_specs=[rows(w) for w in widths] + [rows(HEAD)],
        out_shape=[shape(w, BF16) for w in widths] + [shape(HEAD, F32)],
        scratch_shapes=[pltpu.VMEM((8, 128), F32)], compiler_params=_params(1), name=name)(proj, sp)


def _attn_post_bwd(proj, sp, dfq, dfk, dfv, dsq, dsk, dsv, dmq, dc_col, dc_row_t, name):
    t = proj.shape[0]
    tm = min(t, 256)
    nb = t // tm

    def body(p_ref, sp_ref, dfq_r, dfk_r, dfv_r, dsq_r, dsk_r, dsv_r, dmq_r, dcc_r, dcr_r, dp_ref, dsp_ref, carry):
        @pl.when(pl.program_id(0) == 0)
        def _():
            carry[...] = jnp.zeros_like(carry)
            dsp_ref[...] = jnp.zeros_like(dsp_ref)

        def group(n_heads, col, row, d_ref):
            total = None
            for h in range(n_heads):
                dx, dg = _head_norm_bwd(p_ref[:, _hs(h, col)], sp_ref[row:row + 1, :], d_ref[:, _hs(h)])
                dp_ref[:, _hs(h, col)] = dx.astype(BF16)
                total = dg if total is None else total + dg
            dsp_ref[row:row + 1, :] += total

        group(FOX_H, C_FQ, R_FQ, dfq_r)
        group(FOX_H, C_FK, R_FK, dfk_r)
        dp_ref[:, C_FV:C_FV + FOX_W] = dfv_r[...].astype(BF16)
        group(SWA_H, C_SQ, R_SQ, dsq_r)
        group(SWA_KV, C_SK, R_SK, dsk_r)
        dp_ref[:, C_SV:C_SV + SWA_KV_W] = dsv_r[...].astype(BF16)
        group(MEM_H, C_MQ, R_MQ, dmq_r)
        dc = dcc_r[...] - dcr_r[...]
        rc = jnp.dot(_tri(tm, False), dc, precision=lax.Precision.HIGHEST, preferred_element_type=F32)
        rc = rc + carry[0:1, :]
        carry[...] = jnp.broadcast_to(rc[0:1, :], carry.shape)
        z = p_ref[:, C_FL:C_FL + HEAD] + sp_ref[R_FB:R_FB + 1, :]
        dz = rc * _sigmoid(-z)
        dp_ref[:, C_FL:C_FL + HEAD] = dz.astype(BF16)
        dsp_ref[R_FB:R_FB + 1, :] += jnp.sum(dz, axis=0, keepdims=True)

    def rows(w):
        return pl.BlockSpec((tm, w), lambda i: (nb - 1 - i, 0))

    small = pl.BlockSpec((16, 128), lambda i: (0, 0))
    widths = [FOX_W, FOX_W, FOX_W, SWA_W, SWA_KV_W, SWA_KV_W, MEM_W, HEAD, HEAD]
    return pl.pallas_call(
        body, grid=(nb,), in_specs=[rows(IN_W), small] + [rows(w) for w in widths],
        out_specs=[rows(IN_W), small],
        out_shape=[jax.ShapeDtypeStruct((t, IN_W), BF16), jax.ShapeDtypeStruct((16, 128), F32)],
        scratch_shapes=[pltpu.VMEM((8, 128), F32)], compiler_params=_params(1), name=name,
    )(proj, sp, dfq, dfk, dfv, dsq, dsk, dsv, dmq, dc_col, dc_row_t)


def _head_column(values):
    rows = values[0].shape[0]
    lane = lax.broadcasted_iota(jnp.int32, (rows, HEAD), 1)
    out = jnp.zeros((rows, HEAD), F32)
    for h, v in enumerate(values):
        out = jnp.where(lane == h, v, out)
    return out


def _head_row(values, n_rows=8):
    cols = values[0].shape[1]
    sub = lax.broadcasted_iota(jnp.int32, (n_rows, cols), 0)
    out = jnp.zeros((n_rows, cols), F32)
    for h, v in enumerate(values):
        out = jnp.where(sub == h, v, out)
    return out


def _delta(dmixed, o_a, o_b, o_c, name):
    t = dmixed.shape[0]
    tm = min(t, 512)

    def body(d_ref, a_ref, b_ref, c_ref, o_ref, rep_ref):
        cols = []
        for ref, n_heads, base in ((a_ref, FOX_H, 0), (b_ref, SWA_H, FOX_W), (c_ref, MEM_H, FOX_W + SWA_W)):
            for h in range(n_heads):
                cols.append(jnp.sum(d_ref[:, _hs(h, base)] * ref[:, _hs(h)], axis=-1, keepdims=True))
        o_ref[...] = _head_column(cols)
        for h in range(FOX_H):
            rep_ref[h] = jnp.broadcast_to(cols[h], (tm, HEAD))

    def rows(w):
        return pl.BlockSpec((tm, w), lambda i: (i, 0))

    return pl.pallas_call(body, grid=(t // tm,), in_specs=[rows(dmixed.shape[1]), rows(FOX_W), rows(SWA_W), rows(MEM_W)],
                          out_specs=[rows(HEAD), pl.BlockSpec((FOX_H, tm, HEAD), lambda i: (0, i, 0))],
                          out_shape=[jax.ShapeDtypeStruct((t, HEAD), F32), jax.ShapeDtypeStruct((FOX_H, t, HEAD), F32)],
                          compiler_params=_params(1), name=name)(dmixed, o_a, o_b, o_c)


def _fox_fwd(fq, fk, fv, c_rep, c_row, name, rides=()):
    t = fq.shape[0]
    tb = min(t, 512)
    nb = t // tb
    n_tiles = tb // HEAD

    def body(q_ref, k_ref, v_ref, cc_ref, cr_ref, o_ref, lse_ref, m_s, l_s, acc_s):
        qi, ki = pl.program_id(0), pl.program_id(1)

        @pl.when(ki == 0)
        def _():
            m_s[...] = jnp.full_like(m_s, NEG_INF)
            l_s[...] = jnp.zeros_like(l_s)
            acc_s[...] = jnp.zeros_like(acc_s)

        def step(diagonal):
            if diagonal:
                r = lax.broadcasted_iota(jnp.int32, (tb, HEAD), 0)
                c = lax.broadcasted_iota(jnp.int32, (tb, HEAD), 1)
            for h in range(FOX_H):
                s = _dot(q_ref[:, _hs(h)], k_ref[:, _hs(h)], "nt")
                cc = cc_ref[h]
                tiles, m_cur = [], None
                for j in range(n_tiles):
                    st = s[:, _hs(j)] * SCALE + cc - cr_ref[h:h + 1, _hs(j)]
                    if diagonal:
                        st = jnp.where(r >= c + j * HEAD, st, NEG_INF)
                    tiles.append(st)
                    m_cur = st if m_cur is None else jnp.maximum(m_cur, st)
                m_prev = m_s[h]
                m_new = jnp.maximum(m_prev, jnp.max(m_cur, axis=-1, keepdims=True))
                alpha = jnp.exp(m_prev - m_new)
                ps = [jnp.exp(st - m_new) for st in tiles]
                l_cur = ps[0]
                for p in ps[1:]:
                    l_cur = l_cur + p
                l_s[h] = alpha * l_s[h] + jnp.sum(l_cur, axis=-1, keepdims=True)
                p = jnp.concatenate([p.astype(BF16) for p in ps], axis=1)
                acc_s[:, _hs(h)] = alpha * acc_s[:, _hs(h)] + _dot(p, v_ref[:, _hs(h)])
                m_s[h] = m_new

        @pl.when(ki < qi)
        def _():
            step(False)

        @pl.when(ki == qi)
        def _():
            step(True)
            for h in range(FOX_H):
                o_ref[:, _hs(h)] = acc_s[:, _hs(h)] / l_s[h]
                lse_ref[h] = m_s[h] + jnp.log(l_s[h])

    qspec = pl.BlockSpec((tb, FOX_W), lambda i, j: (i, 0))
    kspec = pl.BlockSpec((tb, FOX_W), lambda i, j: (jnp.minimum(i, j), 0))
    rep = pl.BlockSpec((FOX_H, tb, HEAD), lambda i, j: (0, i, 0))
    return _call(
        name, body, (nb, nb),
        [qspec, kspec, kspec, rep, pl.BlockSpec((8, tb), lambda i, j: (0, jnp.minimum(i, j)))],
        [qspec, rep],
        [jax.ShapeDtypeStruct((t, FOX_W), F32), jax.ShapeDtypeStruct((FOX_H, t, HEAD), F32)],
        [fq, fk, fv, c_rep, c_row],
        scratch=[pltpu.VMEM((FOX_H, tb, HEAD), F32), pltpu.VMEM((FOX_H, tb, HEAD), F32), pltpu.VMEM((tb, FOX_W), F32)],
        rides=rides)


def _fox_bwd(fq, fk, fv, c_rep, c_row, dmixed, lse, delta, name, rides=()):
    t = fq.shape[0]
    tb = min(t, 512)
    nb = t // tb
    n_tiles = tb // HEAD

    def body(q_ref, k_ref, v_ref, cc_ref, cr_ref, do_ref, lse_ref, dl_ref,
             dq_ref, dk_ref, dv_ref, dcc_ref, dcr_ref):
        ki, qi = pl.program_id(0), pl.program_id(1)

        @pl.when((ki == 0) & (qi == 0))
        def _():
            dq_ref[...] = jnp.zeros_like(dq_ref)
            dcc_ref[...] = jnp.zeros_like(dcc_ref)

        @pl.when(qi == 0)
        def _():
            dk_ref[...] = jnp.zeros_like(dk_ref)
            dv_ref[...] = jnp.zeros_like(dv_ref)
            dcr_ref[...] = jnp.zeros_like(dcr_ref)

        def step(diagonal):
            rows = pl.ds(pl.multiple_of(qi * tb, tb), tb)
            if diagonal:
                r = lax.broadcasted_iota(jnp.int32, (tb, HEAD), 0)
                c = lax.broadcasted_iota(jnp.int32, (tb, HEAD), 1)
            row_sums, col_sums = [], []
            for h in range(FOX_H):
                q, k, v, do = q_ref[:, _hs(h)], k_ref[:, _hs(h)], v_ref[:, _hs(h)], do_ref[:, _hs(h)]
                s = _dot(q, k, "nt")
                dp = _dot(do, v, "nt")
                cc, lse_h, dl_h = cc_ref[h], lse_ref[h], dl_ref[h]
                ps, dss, row = [], [], None
                for j in range(n_tiles):
                    st = s[:, _hs(j)] * SCALE + cc - cr_ref[h:h + 1, _hs(j)]
                    if diagonal:
                        st = jnp.where(r >= c + j * HEAD, st, NEG_INF)
                    pt = jnp.exp(st - lse_h)
                    dst = pt * (dp[:, _hs(j)] - dl_h)
                    ps.append(pt.astype(BF16))
                    dss.append(dst)
                    row = dst if row is None else row + dst
                p = jnp.concatenate(ps, axis=1)
                ds = jnp.concatenate(dss, axis=1)
                dsb = ds.astype(BF16)
                dv_ref[:, _hs(h)] += _dot(p, do, "tn")
                dk_ref[:, _hs(h)] += _dot(dsb, q, "tn") * SCALE
                dq_ref[rows, _hs(h)] += _dot(dsb, k) * SCALE
                row_sums.append(jnp.sum(row, axis=1, keepdims=True))
                col_sums.append(jnp.sum(ds, axis=0, keepdims=True))
            dcc_ref[rows, :] += _head_column(row_sums)
            dcr_ref[...] += _head_row(col_sums)

        @pl.when(qi > ki)
        def _():
            step(False)

        @pl.when(qi == ki)
        def _():
            step(True)

    def qmap(j, i):
        return (jnp.maximum(i, j), 0)

    qspec = pl.BlockSpec((tb, FOX_W), qmap)
    kspec = pl.BlockSpec((tb, FOX_W), lambda j, i: (j, 0))
    rep = pl.BlockSpec((FOX_H, tb, HEAD), lambda j, i: (0, jnp.maximum(i, j), 0))
    rowspec = pl.BlockSpec((8, tb), lambda j, i: (0, j))
    return _call(
        name, body, (nb, nb), [qspec, kspec, kspec, rep, rowspec, qspec, rep, rep],
        [pl.BlockSpec((t, FOX_W), lambda j, i: (0, 0)), kspec, kspec,
         pl.BlockSpec((t, HEAD), lambda j, i: (0, 0)), rowspec],
        [jax.ShapeDtypeStruct((t, FOX_W), F32)] * 3 + [jax.ShapeDtypeStruct((t, HEAD), F32),
                                                       jax.ShapeDtypeStruct((8, t), F32)],
        [fq, fk, fv, c_rep, c_row, dmixed, lse, delta], rides=rides)


def _swa_logits(q, k_cur, k_prev, slope, first_block):
    w = SWA_BLOCK
    r = lax.broadcasted_iota(jnp.int32, (w, w), 0)
    j = lax.broadcasted_iota(jnp.int32, (w, w), 1)
    dist_cur = r - j
    dist_prev = w + r - j
    s_cur = _dot(q, k_cur, "nt") * SCALE - slope * dist_cur.astype(F32)
    s_cur = jnp.where(dist_cur >= 0, s_cur, NEG_INF)
    s_prev = _dot(q, k_prev, "nt") * SCALE - slope * dist_prev.astype(F32)
    s_prev = jnp.where((j > r) & jnp.logical_not(first_block), s_prev, NEG_INF)
    return s_cur, s_prev


def _slope(h):
    return float(2.0 ** (-8.0 * (h + 1) / SWA_H))


def _swa_fwd(sq, sk, sv, sp, name, rides=()):
    t = sq.shape[0]
    w = SWA_BLOCK
    nb = t // w
    group = SWA_H // SWA_KV

    def body(q_ref, kp_ref, kc_ref, vp_ref, vc_ref, sp_ref, o_ref, lse_ref):
        first = pl.program_id(0) == 0
        lses = []
        for h in range(SWA_H):
            kv = h // group
            s_cur, s_prev = _swa_logits(q_ref[:, _hs(h)], kc_ref[:, _hs(kv)], kp_ref[:, _hs(kv)], _slope(h), first)
            sink = sp_ref[R_SINK:R_SINK + 1, h:h + 1]
            m = jnp.maximum(jnp.maximum(jnp.max(s_cur, axis=-1, keepdims=True),
                                        jnp.max(s_prev, axis=-1, keepdims=True)), sink)
            p_cur = jnp.exp(s_cur - m)
            p_prev = jnp.exp(s_prev - m)
            l = jnp.sum(p_cur, axis=-1, keepdims=True) + jnp.sum(p_prev, axis=-1, keepdims=True) + jnp.exp(sink - m)
            o_ref[:, _hs(h)] = (_dot(p_cur, vc_ref[:, _hs(kv)]) + _dot(p_prev, vp_ref[:, _hs(kv)])) / l
            lses.append(m + jnp.log(l))
        lse_ref[...] = _head_column(lses)

    qspec = pl.BlockSpec((w, SWA_W), lambda n: (n, 0))
    cur = pl.BlockSpec((w, SWA_KV_W), lambda n: (n, 0))
    prev = pl.BlockSpec((w, SWA_KV_W), lambda n: (jnp.maximum(n - 1, 0), 0))
    return _call(
        name, body, (nb,), [qspec, prev, cur, prev, cur, pl.BlockSpec((16, 128), lambda n: (0, 0))],
        [qspec, pl.BlockSpec((w, HEAD), lambda n: (n, 0))],
        [jax.ShapeDtypeStruct((t, SWA_W), F32), jax.ShapeDtypeStruct((t, HEAD), F32)],
        [sq, sk, sk, sv, sv, sp], rides=rides)


def _swa_bwd(sq, sk, sv, sp, dmixed, lse, delta, name):
    t = sq.shape[0]
    w = SWA_BLOCK
    nb = t // w
    group = SWA_H // SWA_KV
    do_block = FOX_W // SWA_W
    assert FOX_W % SWA_W == 0

    def body(q_ref, kp_ref, kc_ref, vp_ref, vc_ref, sp_ref, do_ref, lse_ref, dl_ref,
             dq_ref, dk_ref, dv_ref, dsp_ref, ck, cv):
        step = pl.program_id(0)
        first = step == nb - 1

        @pl.when(step == 0)
        def _():
            ck[...] = jnp.zeros_like(ck)
            cv[...] = jnp.zeros_like(cv)
            dsp_ref[...] = jnp.zeros_like(dsp_ref)

        dk_cur = [None] * SWA_KV
        dk_prev = [None] * SWA_KV
        dv_cur = [None] * SWA_KV
        dv_prev = [None] * SWA_KV
        dsinks = []

        def add(lst, i, v):
            lst[i] = v if lst[i] is None else lst[i] + v

        for h in range(SWA_H):
            kv = h // group
            q, do = q_ref[:, _hs(h)], do_ref[:, _hs(h)]
            kc, kp, vc, vp = kc_ref[:, _hs(kv)], kp_ref[:, _hs(kv)], vc_ref[:, _hs(kv)], vp_ref[:, _hs(kv)]
            s_cur, s_prev = _swa_logits(q, kc, kp, _slope(h), first)
            lse_h = lse_ref[:, h:h + 1]
            dl_h = dl_ref[:, FOX_H + h:FOX_H + h + 1]
            p_cur = jnp.exp(s_cur - lse_h)
            p_prev = jnp.exp(s_prev - lse_h)
            p_sink = jnp.exp(sp_ref[R_SINK:R_SINK + 1, h:h + 1] - lse_h)
            ds_cur = p_cur * (_dot(do, vc, "nt") - dl_h)
            ds_prev = p_prev * (_dot(do, vp, "nt") - dl_h)
            dq_ref[:, _hs(h)] = (_dot(ds_cur, kc) + _dot(ds_prev, kp)) * SCALE
            add(dk_cur, kv, _dot(ds_cur, q, "tn") * SCALE)
            add(dk_prev, kv, _dot(ds_prev, q, "tn") * SCALE)
            add(dv_cur, kv, _dot(p_cur, do, "tn"))
            add(dv_prev, kv, _dot(p_prev, do, "tn"))
            dsinks.append(-jnp.sum(p_sink * dl_h, axis=0, keepdims=True))
        for kv in range(SWA_KV):
            dk_ref[:, _hs(kv)] = dk_cur[kv] + ck[:, _hs(kv)]
            dv_ref[:, _hs(kv)] = dv_cur[kv] + cv[:, _hs(kv)]
            ck[:, _hs(kv)] = dk_prev[kv]
            cv[:, _hs(kv)] = dv_prev[kv]
        lane = lax.broadcasted_iota(jnp.int32, (1, HEAD), 1)
        row = jnp.zeros((1, HEAD), F32)
        for h in range(SWA_H):
            row = jnp.where(lane == h, dsinks[h], row)
        dsp_ref[R_SINK:R_SINK + 1, :] += row

    def rev(n):
        return nb - 1 - n

    qspec = pl.BlockSpec((w, SWA_W), lambda n: (rev(n), 0))
    cur = pl.BlockSpec((w, SWA_KV_W), lambda n: (rev(n), 0))
    prev = pl.BlockSpec((w, SWA_KV_W), lambda n: (jnp.maximum(rev(n) - 1, 0), 0))
    col = pl.BlockSpec((w, HEAD), lambda n: (rev(n), 0))
    small = pl.BlockSpec((16, 128), lambda n: (0, 0))
    return pl.pallas_call(
        body, grid=(nb,),
        in_specs=[qspec, prev, cur, prev, cur, small, pl.BlockSpec((w, SWA_W), lambda n: (rev(n), do_block)), col, col],
        out_specs=[qspec, cur, cur, small],
        out_shape=[jax.ShapeDtypeStruct((t, SWA_W), F32), jax.ShapeDtypeStruct((t, SWA_KV_W), F32),
                   jax.ShapeDtypeStruct((t, SWA_KV_W), F32), jax.ShapeDtypeStruct((16, 128), F32)],
        scratch_shapes=[pltpu.VMEM((w, SWA_KV_W), F32), pltpu.VMEM((w, SWA_KV_W), F32)],
        compiler_params=_params(1), name=name)(sq, sk, sk, sv, sv, sp, dmixed, lse, delta)


def _mem_pre(mkv, sp, name):
    m = mkv.shape[0]

    def body(x_ref, sp_ref, k_ref, v_ref):
        for h in range(MEM_H):
            k_ref[:, _hs(h)] = _head_norm(x_ref[:, _hs(h)], sp_ref[R_MK:R_MK + 1, :]).astype(BF16)
        v_ref[...] = x_ref[:, MEM_W:2 * MEM_W].astype(BF16)

    out = jax.ShapeDtypeStruct((m, MEM_W), BF16)
    return pl.pallas_call(body, out_shape=[out, out], name=name)(mkv, sp)


def _mem_post_bwd(mkv, sp, dmk, dmv, name):
    m = mkv.shape[0]

    def body(x_ref, sp_ref, dk_ref, dv_ref, d_ref, dsp_ref):
        dsp_ref[...] = jnp.zeros_like(dsp_ref)
        total = None
        for h in range(MEM_H):
            dx, dg = _head_norm_bwd(x_ref[:, _hs(h)], sp_ref[R_MK:R_MK + 1, :], dk_ref[:, _hs(h)])
            d_ref[:, _hs(h)] = dx.astype(BF16)
            total = dg if total is None else total + dg
        d_ref[:, MEM_W:2 * MEM_W] = dv_ref[...].astype(BF16)
        dsp_ref[R_MK:R_MK + 1, :] = total

    return pl.pallas_call(body, out_shape=[jax.ShapeDtypeStruct((m, 2 * MEM_W), BF16),
                                           jax.ShapeDtypeStruct((16, 128), F32)], name=name)(mkv, sp, dmk, dmv)


def _mem_fwd(mq, mk, mv, name):
    t = mq.shape[0]
    m = mk.shape[0]
    tq = min(t, 512)

    def body(q_ref, k_ref, v_ref, o_ref, lse_ref):
        lses = []
        for h in range(MEM_H):
            s = _dot(q_ref[:, _hs(h)], k_ref[:, _hs(h)], "nt") * SCALE
            mx = jnp.max(s, axis=-1, keepdims=True)
            p = jnp.exp(s - mx)
            l = jnp.sum(p, axis=-1, keepdims=True)
            o_ref[:, _hs(h)] = _dot(p, v_ref[:, _hs(h)]) / l
            lses.append(mx + jnp.log(l))
        lse_ref[...] = _head_column(lses)

    qspec = pl.BlockSpec((tq, MEM_W), lambda i: (i, 0))
    kspec = pl.BlockSpec((m, MEM_W), lambda i: (0, 0))
    return pl.pallas_call(
        body, grid=(t // tq,), in_specs=[qspec, kspec, kspec],
        out_specs=[qspec, pl.BlockSpec((tq, HEAD), lambda i: (i, 0))],
        out_shape=[jax.ShapeDtypeStruct((t, MEM_W), F32), jax.ShapeDtypeStruct((t, HEAD), F32)],
        compiler_params=_params(1), name=name)(mq, mk, mv)


def _mem_bwd(mq, mk, mv, dmixed, lse, delta, name):
    t = mq.shape[0]
    m = mk.shape[0]
    tq = min(t, 512)
    do_block = (FOX_W + SWA_W) // MEM_W
    assert (FOX_W + SWA_W) % MEM_W == 0

    def body(q_ref, k_ref, v_ref, do_ref, lse_ref, dl_ref, dq_ref, dk_ref, dv_ref):
        @pl.when(pl.program_id(0) == 0)
        def _():
            dk_ref[...] = jnp.zeros_like(dk_ref)
            dv_ref[...] = jnp.zeros_like(dv_ref)

        for h in range(MEM_H):
            q, k, v, do = q_ref[:, _hs(h)], k_ref[:, _hs(h)], v_ref[:, _hs(h)], do_ref[:, _hs(h)]
            s = _dot(q, k, "nt") * SCALE
            p = jnp.exp(s - lse_ref[:, h:h + 1])
            col = FOX_H + SWA_H + h
            ds = p * (_dot(do, v, "nt") - dl_ref[:, col:col + 1])
            dq_ref[:, _hs(h)] = _dot(ds, k) * SCALE
            dk_ref[:, _hs(h)] += _dot(ds, q, "tn") * SCALE
            dv_ref[:, _hs(h)] += _dot(p, do, "tn")

    qspec = pl.BlockSpec((tq, MEM_W), lambda i: (i, 0))
    kspec = pl.BlockSpec((m, MEM_W), lambda i: (0, 0))
    col = pl.BlockSpec((tq, HEAD), lambda i: (i, 0))
    return pl.pallas_call(
        body, grid=(t // tq,),
        in_specs=[qspec, kspec, kspec, pl.BlockSpec((tq, MEM_W), lambda i: (i, do_block)), col, col],
        out_specs=[qspec, kspec, kspec],
        out_shape=[jax.ShapeDtypeStruct((t, MEM_W), F32), jax.ShapeDtypeStruct((m, MEM_W), F32),
                   jax.ShapeDtypeStruct((m, MEM_W), F32)],
        compiler_params=_params(1), name=name)(mq, mk, mv, dmixed, lse, delta)


def _all_gather(xs, name):
    n = len(xs)

    def body(*refs):
        x_refs, o_refs = refs[:n], refs[n:2 * n]
        send_sems, recv_sems, local_sems = refs[2 * n:]
        x, y, c = _me()
        me, sibling = (x, y, c), (x, y, 1 - c)
        x_nb, y_nb, diag = (1 - x, y, c), (x, 1 - y, c), (1 - x, 1 - y, c)
        relay_from = (x + (1 - c) * (1 - 2 * x), y + c * (1 - 2 * y), c)
        relay_to = (x + c * (1 - 2 * x), y + (1 - c) * (1 - 2 * y), c)

        def copy(a, k, block, to, src=None):
            slot = o_refs[a].at[_lin(block)]
            return pltpu.make_async_remote_copy(
                src_ref=slot if src is None else src, dst_ref=slot, send_sem=send_sems.at[a, k],
                recv_sem=recv_sems.at[a, k], device_id=to, device_id_type=MESH)

        mine = [pltpu.make_async_copy(x_refs[a], o_refs[a].at[_lin(me)], local_sems.at[a]) for a in range(n)]
        for cp in mine:
            cp.start()
        sent = []
        for a in range(n):
            sent += [copy(a, 0, me, sibling, src=x_refs[a]), copy(a, 1, me, x_nb, src=x_refs[a]),
                     copy(a, 2, me, y_nb, src=x_refs[a])]
        for cp in sent:
            cp.start()

        def pass_on(cp):
            cp.start()
            sent.append(cp)

        for a in range(n):
            copy(a, 1, x_nb, me).wait_recv()
            copy(a, 2, y_nb, me).wait_recv()
            pass_on(copy(a, 3, relay_from, relay_to))
            pass_on(copy(a, 4, x_nb, sibling))
            pass_on(copy(a, 5, y_nb, sibling))
        for a in range(n):
            copy(a, 3, diag, me).wait_recv()
            pass_on(copy(a, 6, diag, sibling))
        for a in range(n):
            copy(a, 0, sibling, me).wait_recv()
            for k, chip in ((4, (1 - x, y)), (5, (x, 1 - y)), (6, (1 - x, 1 - y))):
                copy(a, k, (*chip, 1 - c), me).wait_recv()
        for cp in sent:
            cp.wait_send()
        for cp in mine:
            cp.wait()

    return pl.pallas_call(
        body, in_specs=[ANY] * n, out_specs=[ANY] * n,
        out_shape=[jax.ShapeDtypeStruct((N_DEV,) + x.shape, x.dtype) for x in xs],
        scratch_shapes=[pltpu.SemaphoreType.DMA((n, 7)), pltpu.SemaphoreType.DMA((n, 7)),
                        pltpu.SemaphoreType.DMA((n,))],
        name=name)(*xs)


def _peers():
    x, y, c = _me()
    out = []
    for k in range(1, N_DEV):
        kx, ky, kc = (k >> 2) & 1, (k >> 1) & 1, k & 1
        out.append(((1 - x) if kx else x, (1 - y) if ky else y, (1 - c) if kc else c))
    return out


def _all_reduce_small(xs, name):
    n = len(xs)

    def body(*refs):
        x_refs, o_refs = refs[:n], refs[n:2 * n]
        bufs = refs[2 * n:3 * n]
        send_sems, recv_sems = refs[3 * n:]
        me = _lin(_me())
        peers = _peers()
        for a in range(n):
            bufs[a][me] = x_refs[a][...]
        sends = []
        for a in range(n):
            for k, peer in enumerate(peers):
                sends.append(pltpu.make_async_remote_copy(
                    src_ref=bufs[a].at[me], dst_ref=bufs[a].at[me], send_sem=send_sems.at[a, k],
                    recv_sem=recv_sems.at[a, k], device_id=peer, device_id_type=MESH))
        for cp in sends:
            cp.start()
        for a in range(n):
            for k, peer in enumerate(peers):
                pltpu.make_async_remote_copy(
                    src_ref=bufs[a].at[me], dst_ref=bufs[a].at[_lin(peer)], send_sem=send_sems.at[a, k],
                    recv_sem=recv_sems.at[a, k], device_id=peer, device_id_type=MESH).wait_recv()
        for cp in sends:
            cp.wait_send()
        for a in range(n):
            total = bufs[a][0]
            for q in range(1, N_DEV):
                total = total + bufs[a][q]
            o_refs[a][...] = total

    vmem = pl.BlockSpec(memory_space=pltpu.VMEM)
    return pl.pallas_call(
        body, in_specs=[vmem] * n, out_specs=[vmem] * n,
        out_shape=[jax.ShapeDtypeStruct(x.shape, F32) for x in xs],
        scratch_shapes=[pltpu.VMEM((N_DEV,) + x.shape, F32) for x in xs]
        + [pltpu.SemaphoreType.DMA((n, 7)), pltpu.SemaphoreType.DMA((n, 7))],
        name=name)(*xs)


def _pair_add(part, got, name):
    _, rows, cols = part.shape
    tm = _rows_tile(rows, cols * 2, budget=2 << 20)
    core = jnp.reshape(lax.axis_index("c"), (1,)).astype(jnp.int32)

    def body(c_ref, p_ref, g_ref, o_ref):
        o_ref[...] = (p_ref[...].astype(F32) + g_ref[...].astype(F32)).astype(BF16)

    spec = pl.BlockSpec((None, tm, cols), lambda q, i, c: (q, i, 0))
    grid_spec = pltpu.PrefetchScalarGridSpec(
        num_scalar_prefetch=1, grid=(4, rows // tm),
        in_specs=[pl.BlockSpec((None, tm, cols), lambda q, i, c: (2 * q + c[0], i, 0)), spec], out_specs=spec)
    return pl.pallas_call(body, grid_spec=grid_spec, out_shape=jax.ShapeDtypeStruct((4, rows, cols), BF16),
                          compiler_params=_params(2), name=name)(core, part, got)


def _adam_math(w, g, m, v):
    nm = ADAM_B1 * m + (1.0 - ADAM_B1) * g
    nv = ADAM_B2 * v + (1.0 - ADAM_B2) * (g * g)
    m_hat = nm / (1.0 - ADAM_B1 ** ADAM_STEP)
    v_hat = nv / (1.0 - ADAM_B2 ** ADAM_STEP)
    return -ADAM_LR * (m_hat / (jnp.sqrt(v_hat) + ADAM_EPS) + ADAM_WD * w), nm, nv


def _sum_chips(got, name):
    _, rows, cols = got.shape
    tm = _rows_tile(rows, cols * 2 * 4, budget=2 << 20)

    def body(r_ref, o_ref):
        o_ref[...] = ((r_ref[0].astype(F32) + r_ref[1].astype(F32)) + r_ref[2].astype(F32)) + r_ref[3].astype(F32)

    return pl.pallas_call(
        body, grid=(rows // tm,), in_specs=[pl.BlockSpec((4, tm, cols), lambda i: (0, i, 0))],
        out_specs=pl.BlockSpec((tm, cols), lambda i: (i, 0)), out_shape=jax.ShapeDtypeStruct((rows, cols), F32),
        compiler_params=_params(1), name=name)(got)


def _sum_adamw(got, col_block, w, m, v, name):
    _, rows, cols = w.shape
    tm = _rows_tile(rows, cols * 4, budget=2 << 20)

    def body(r_ref, w_ref, m_ref, v_ref, g_ref, d_ref, nm_ref, nv_ref):
        g = ((r_ref[0].astype(F32) + r_ref[1].astype(F32)) + r_ref[2].astype(F32)) + r_ref[3].astype(F32)
        g_ref[...] = g
        d_ref[...], nm_ref[...], nv_ref[...] = _adam_math(w_ref[...], g, m_ref[...], v_ref[...])

    spec = pl.BlockSpec((None, tm, cols), lambda i: (0, i, 0))
    out = jax.ShapeDtypeStruct(w.shape, F32)
    return pl.pallas_call(
        body, grid=(rows // tm,), in_specs=[pl.BlockSpec((4, tm, cols), lambda i: (0, i, col_block)), spec, spec, spec],
        out_specs=[spec] * 4, out_shape=[out] * 4, compiler_params=_params(1), name=name)(got, w, m, v)


def _adamw(w, g, m, v, name):
    rows, cols = w.shape

    def body(w_ref, g_ref, m_ref, v_ref, d_ref, nm_ref, nv_ref):
        d_ref[...], nm_ref[...], nv_ref[...] = _adam_math(w_ref[...], g_ref[...], m_ref[...], v_ref[...])

    tm = _rows_tile(rows, cols * 4, budget=2 << 20, mult=8)
    spec = pl.BlockSpec((tm, cols), lambda i: (i, 0))
    out = jax.ShapeDtypeStruct(w.shape, F32)
    return pl.pallas_call(body, grid=(rows // tm,), in_specs=[spec] * 4, out_specs=[spec] * 3,
                          out_shape=[out] * 3, compiler_params=_params(1), name=name)(w, g, m, v)


def _permute_in(w):
    logit0 = 3 * FOX_W
    pad = jnp.zeros(w.shape[:-1] + (HEAD - N_LOGIT,), w.dtype)
    return jnp.concatenate([w[..., :logit0], w[..., logit0 + N_LOGIT:], w[..., logit0:logit0 + N_LOGIT], pad], axis=-1)


def _unpermute_in(w):
    logit0 = 3 * FOX_W
    return jnp.concatenate([w[..., :logit0], w[..., C_FL:C_FL + N_LOGIT], w[..., logit0:C_FL]], axis=-1)


def _pad_row(v, width):
    return jnp.pad(v, ((0, 0), (0, width - v.shape[1])))


def _pack_small(fq, fk, sq, sk, mq, mk, fb, sinks):
    rows = [fq, fk, sq, sk, mq, mk, _pad_row(fb, HEAD), _pad_row(sinks, HEAD)]
    return jnp.concatenate(rows + [jnp.zeros((8, HEAD), F32)], axis=0)


def _pack_norms(a, b, c, d):
    return jnp.concatenate([a, b, c, d, jnp.zeros((4, a.shape[1]), F32)], axis=0)


def kernel(x, mem, ffn1_norm, ffn1_gate, ffn1_up, ffn1_down, mix_norm, mem_norm, w_in, forget_bias, w_mem_k, w_mem_v, fox_q_gain, fox_k_gain, swa_q_gain, swa_k_gain, swa_sinks, mem_q_gain, mem_k_gain, w_out, ffn2_norm, ffn2_gate, ffn2_up, ffn2_down, loss_target, m_ffn1_norm, m_ffn1_gate, m_ffn1_up, m_ffn1_down, m_mix_norm, m_mem_norm, m_w_in, m_forget_bias, m_w_mem_k, m_w_mem_v, m_fox_q_gain, m_fox_k_gain, m_swa_q_gain, m_swa_k_gain, m_swa_sinks, m_mem_q_gain, m_mem_k_gain, m_w_out, m_ffn2_norm, m_ffn2_gate, m_ffn2_up, m_ffn2_down, v_ffn1_norm, v_ffn1_gate, v_ffn1_up, v_ffn1_down, v_mix_norm, v_mem_norm, v_w_in, v_forget_bias, v_w_mem_k, v_w_mem_v, v_fox_q_gain, v_fox_k_gain, v_swa_q_gain, v_swa_k_gain, v_swa_sinks, v_mem_q_gain, v_mem_k_gain, v_w_out, v_ffn2_norm, v_ffn2_gate, v_ffn2_up, v_ffn2_down):
    x0 = x[0]
    mem0 = mem[0]
    target = loss_target[0]
    t, d = x0.shape
    d_shard = w_in.shape[1]
    m_len = mem0.shape[0]
    tm = min(t, 512)
    tk = min(t, 2048)
    tn = IN_W // 3
    tkw, tnw = min(t, 1024), IN_W // 3

    def swap(a):
        return jnp.swapaxes(a, 1, 2)

    gate1, up1, gate2, up2 = swap(ffn1_gate), swap(ffn1_up), swap(ffn2_gate), swap(ffn2_up)

    local = {
        "g1": gate1[0], "u1": up1[0], "d1": ffn1_down[0],
        "g2": gate2[0], "u2": up2[0], "d2": ffn2_down[0],
        "in": _permute_in(w_in[0]), "out": w_out[0],
        "mkv": jnp.concatenate([w_mem_k[0], w_mem_v[0]], axis=1),
    }
    shard = {k: _cast_bf16(v, f"cast_{k}") for k, v in local.items()}
    sp = _pack_small(fox_q_gain, fox_k_gain, swa_q_gain, swa_k_gain, mem_q_gain, mem_k_gain, forget_bias, swa_sinks)
    wt = {}

    wt["g1"], wt["u1"] = _all_gather([shard["g1"], shard["u1"]], "gather_ffn1_in")
    xn1 = _rms_fwd(x0, ffn1_norm, "ffn1_norm")
    half_fs = shard["g2"].shape[0] // 2
    (a1, b1, h1), ((wt["d1"], wt["in"]),) = _ffn_up(
        xn1, wt["g1"], wt["u1"], "ffn1", rides=[_ride_gather([shard["d1"], shard["in"]], 0.87)])
    tc = min(t, CONTRACT_ROWS)
    (x1, hn), ((wt["out"], wt["mkv"]),) = _ffn_down(
        x0, h1, wt["d1"], "ffn1", rides=[_ride_gather([shard["out"], shard["mkv"]], 0.6)],
        tail=_tail_norm(mix_norm, t, d, tc))
    w_in_full = wt["in"].reshape(d, IN_W)

    proj, (half,) = _mm(
        "proj", [(hn, pl.BlockSpec((tm, d), lambda n, i, k: (i, 0)),
                  w_in_full, pl.BlockSpec((d, tn), lambda n, i, k: (0, n)))],
        "nn", (3, t // tm, 1), jax.ShapeDtypeStruct((t, IN_W), F32), pl.BlockSpec((tm, tn), lambda n, i, k: (i, n)),
        rides=[_ride_gather_chips([shard["g2"]], rows=(0, half_fs))])
    w_out_full = wt["out"].reshape(d, d)
    w_mkv_full = wt["mkv"].reshape(d, 2 * MEM_W)
    fq, fk, fv, sq, sk, sv, mq, c_col = _attn_pre(proj, sp, "attn_pre")
    c_row = jnp.transpose(c_col[:, :8])
    c_rep = jnp.broadcast_to(c_row[:FOX_H, :, None], (FOX_H, t, HEAD))

    mn = _rms_fwd(mem0, mem_norm, "mem_norm")
    mkv = _mm("mem_kv", [(mn, pl.BlockSpec((m_len, d), lambda k: (0, 0)),
                          w_mkv_full, pl.BlockSpec((d, 2 * MEM_W), lambda k: (0, 0)))],
              "nn", (1,), jax.ShapeDtypeStruct((m_len, 2 * MEM_W), F32),
              pl.BlockSpec((m_len, 2 * MEM_W), lambda k: (0, 0)))
    mk, mv = _mem_pre(mkv, sp, "mem_pre")

    (o_a, lse_a), (half, half_u2) = _fox_fwd(
        fq, fk, fv, c_rep, c_row, "fox_fwd",
        rides=[_ride_gather_chips([shard["g2"]], rows=(half_fs, half_fs), into=half),
               _ride_gather_chips([shard["u2"]], rows=(0, half_fs))])
    (o_b, lse_b), ((wt["g2"],), half_u2) = _swa_fwd(
        sq, sk, sv, sp, "swa_fwd",
        rides=[_ride_gather_sibling(half), _ride_gather_chips([shard["u2"]], rows=(half_fs, half_fs), into=half_u2)])
    o_c, lse_c = _mem_fwd(mq, mk, mv, "mem_fwd")

    def rows_spec(width):
        return pl.BlockSpec((tm, width), lambda i, k: (i, 0))

    def wout_rows(first, width):
        assert first % width == 0
        return pl.BlockSpec((width, d), lambda i, k: (first // width, 0), pipeline_mode=pl.Buffered(1))

    xspec = pl.BlockSpec((tm, d), lambda i, k: (i, 0))
    (x2, xn2), ((wt["u2"],),) = _mm(
        "mix_out",
        [(o_a, rows_spec(FOX_W), w_out_full, wout_rows(0, FOX_W)),
         (o_b, rows_spec(SWA_W), w_out_full, wout_rows(FOX_W, SWA_W)),
         (o_c, rows_spec(MEM_W), w_out_full, wout_rows(FOX_W + SWA_W, MEM_W))],
        "nn", (t // tm, 1), jax.ShapeDtypeStruct((t, d), F32), xspec, res=x1, res_spec=xspec,
        rides=[_ride_gather_sibling(half_u2)], tail=_tail_norm(ffn2_norm, t, d, tm))

    (a2, b2, h2), ((wt["d2"],),) = _ffn_up(xn2, wt["g2"], wt["u2"], "ffn2", rides=[_ride_gather([shard["d2"]], 0.75)])
    dy, dyb, sq_err = _ffn_down(x2, h2, wt["d2"], "ffn2", tail=_tail_loss(target, t, d, tc))
    loss = lax.psum(0.5 * sq_err[0, 0] / d, ("x", "y", "c"))

    got = {}
    paired = {}
    landed = {}

    def pair(k, part):
        paired[k] = _pair_add(part, got[k], f"pair_{k}")

    part_d2 = _ffn_dw(h2, dyb, 0.5, "ffn2_dwd")
    (dg2, du2), ((got["d2"],),) = _ffn_dact(dyb, wt["d2"], a2, b2, "ffn2", rides=[_ride_scatter_sibling([part_d2])])
    pair("d2", part_d2)
    half_rows = part_d2.shape[1] // 2
    first, second = (0, half_rows), (half_rows, half_rows)
    part_g2, (half,) = _ffn_dw(dg2, xn2, 1.0, "ffn2_dwg", rides=[_ride_scatter_chips([paired["d2"]], first)])
    part_u2, ((landed["d2"],), (got["g2"],)) = _ffn_dw(
        du2, xn2, 1.0, "ffn2_dwu",
        rides=[_ride_scatter_chips([paired["d2"]], second, into=half), _ride_scatter_sibling([part_g2])])
    pair("g2", part_g2)
    dxn2, (half_g2,) = _ffn_contract(
        dg2, wt["g2"], "ffn2_dxn_g", rides=[_ride_scatter_chips([paired["g2"]], first)])
    (dx2, dx2b, dgain_ffn2), ((landed["g2"],), (got["u2"],)) = _ffn_contract(
        du2, wt["u2"], "ffn2_dxn_u", res=dxn2,
        rides=[_ride_scatter_chips([paired["g2"]], second, into=half_g2), _ride_scatter_sibling([part_u2])],
        tail=_tail_norm_bwd(x2, ffn2_norm, dy, t, d, tc))
    pair("u2", part_u2)

    dmixed = _mm("mix_out_dx", [(dx2b, xspec, w_out_full, pl.BlockSpec((d, d), lambda i, k: (0, 0)))],
                 "nt", (t // tm, 1), jax.ShapeDtypeStruct((t, d), F32), xspec)

    def k_rows(width):
        return pl.BlockSpec((tk, width), lambda j, k: (k, 0))

    part_out = [
        _mm(f"mix_out_dw{i}", [(o, k_rows(width), dx2b, k_rows(d))], "tn", (1, t // tk),
            jax.ShapeDtypeStruct((width, d), BF16), pl.BlockSpec((width, d), lambda j, k: (0, 0)))
        for i, (o, width) in enumerate(((o_a, FOX_W), (o_b, SWA_W), (o_c, MEM_W)))
    ]
    part_out = jnp.concatenate(part_out, axis=0).reshape(N_DEV, d_shard, d)

    delta, delta_rep = _delta(dmixed, o_a, o_b, o_c, "attn_delta")
    (dfq, dfk, dfv, dc_col, dc_row), ((landed["u2"],),) = _fox_bwd(
        fq, fk, fv, c_rep, c_row, dmixed, lse_a, delta_rep, "fox_bwd", rides=[_ride_scatter_chips([paired["u2"]])])
    dsq, dsk, dsv, dsp_sink = _swa_bwd(sq, sk, sv, sp, dmixed, lse_b, delta, "swa_bwd")
    dmq, dmk, dmv = _mem_bwd(mq, mk, mv, dmixed, lse_c, delta, "mem_bwd")

    dmkv, dsp_mem = _mem_post_bwd(mkv, sp, dmk, dmv, "mem_post_bwd")
    part_mkv = _mm("mem_kv_dw", [(mn, pl.BlockSpec((m_len, d), lambda k: (0, 0)),
                                  dmkv, pl.BlockSpec((m_len, 2 * MEM_W), lambda k: (0, 0)))],
                   "tn", (1,), jax.ShapeDtypeStruct((d, 2 * MEM_W), BF16),
                   pl.BlockSpec((d, 2 * MEM_W), lambda k: (0, 0))).reshape(N_DEV, d_shard, 2 * MEM_W)
    dmn = _mm("mem_kv_dx", [(dmkv, pl.BlockSpec((m_len, 2 * MEM_W), lambda k: (0, 0)),
                             w_mkv_full, pl.BlockSpec((d, 2 * MEM_W), lambda k: (0, 0)))],
              "nt", (1,), jax.ShapeDtypeStruct((m_len, d), F32), pl.BlockSpec((m_len, d), lambda k: (0, 0)))
    _, _, dgain_mem = _rms_bwd(mem0, mem_norm, dmn, None, "mem_norm_bwd")

    dc_row_t = _pad_row(jnp.transpose(dc_row), HEAD)
    dproj, dsp_attn = _attn_post_bwd(proj, sp, dfq, dfk, dfv, dsq, dsk, dsv, dmq, dc_col, dc_row_t, "attn_post_bwd")
    (dx1, dx1b, dgain_mix), ((got["out"], got["mkv"]),) = _mm(
        "proj_dx", [(dproj, pl.BlockSpec((tc, IN_W), lambda i, k: (i, 0)),
                     w_in_full, pl.BlockSpec((d, IN_W), lambda i, k: (0, 0), pipeline_mode=pl.Buffered(1)))],
        "nt", (t // tc, 1), jax.ShapeDtypeStruct((t, d), F32), pl.BlockSpec((tc, d), lambda i, k: (i, 0)),
        rides=[_ride_scatter_sibling([part_out, part_mkv])], tail=_tail_norm_bwd(x1, mix_norm, dx2, t, d, tc))
    pair("out", part_out)
    pair("mkv", part_mkv)
    part_in, ((landed["out"], landed["mkv"]),) = _mm(
        "proj_dw", [(hn, pl.BlockSpec((tkw, d), lambda n, k: (k, 0)),
                     dproj, pl.BlockSpec((tkw, tnw), lambda n, k: (k, n)))],
        "tn", (IN_W // tnw, t // tkw), jax.ShapeDtypeStruct((d, IN_W), BF16), pl.BlockSpec((d, tnw), lambda n, k: (0, n)),
        rides=[_ride_scatter_chips([paired["out"], paired["mkv"]])])
    part_in = part_in.reshape(N_DEV, d_shard, IN_W)

    part_d1, ((got["in"],),) = _ffn_dw(h1, dx1b, 0.5, "ffn1_dwd", rides=[_ride_scatter_sibling([part_in])])
    pair("in", part_in)
    (dg1, du1), ((landed["in"],), (got["d1"],)) = _ffn_dact(
        dx1b, wt["d1"], a1, b1, "ffn1",
        rides=[_ride_scatter_chips([paired["in"]]), _ride_scatter_sibling([part_d1])])
    pair("d1", part_d1)
    part_g1, (half_d1,) = _ffn_dw(dg1, xn1, 1.0, "ffn1_dwg", rides=[_ride_scatter_chips([paired["d1"]], first)])
    part_u1, ((landed["d1"],), (got["g1"],)) = _ffn_dw(
        du1, xn1, 1.0, "ffn1_dwu",
        rides=[_ride_scatter_chips([paired["d1"]], second, into=half_d1), _ride_scatter_sibling([part_g1])])
    pair("g1", part_g1)
    dxn1, ((landed["g1"],), (got["u1"],)) = _ffn_contract(
        dg1, wt["g1"], "ffn1_dxn_g", rides=[_ride_scatter_chips([paired["g1"]]), _ride_scatter_sibling([part_u1])])
    pair("u1", part_u1)
    (grad_x, _, dgain_ffn1), ((landed["u1"],),) = _ffn_contract(
        du1, wt["u1"], "ffn1_dxn_u", res=dxn1, rides=[_ride_scatter_chips([paired["u1"]])],
        tail=_tail_norm_bwd(x0, ffn1_norm, dx1, t, d, tc))

    norms_sum, small_sum = _all_reduce_small(
        [_pack_norms(dgain_ffn1, dgain_mix, dgain_mem, dgain_ffn2), dsp_attn + dsp_sink + dsp_mem], "reduce_small")

    result = {
        "ffn1_gate": map(swap, _sum_adamw(landed["g1"], 0, gate1, swap(m_ffn1_gate), swap(v_ffn1_gate), "adamw_ffn1_gate")),
        "ffn1_up": map(swap, _sum_adamw(landed["u1"], 0, up1, swap(m_ffn1_up), swap(v_ffn1_up), "adamw_ffn1_up")),
        "ffn1_down": _sum_adamw(landed["d1"], 0, ffn1_down, m_ffn1_down, v_ffn1_down, "adamw_ffn1_down"),
        "w_mem_k": _sum_adamw(landed["mkv"], 0, w_mem_k, m_w_mem_k, v_w_mem_k, "adamw_w_mem_k"),
        "w_mem_v": _sum_adamw(landed["mkv"], 1, w_mem_v, m_w_mem_v, v_w_mem_v, "adamw_w_mem_v"),
        "w_out": _sum_adamw(landed["out"], 0, w_out, m_w_out, v_w_out, "adamw_w_out"),
        "ffn2_gate": map(swap, _sum_adamw(landed["g2"], 0, gate2, swap(m_ffn2_gate), swap(v_ffn2_gate), "adamw_ffn2_gate")),
        "ffn2_up": map(swap, _sum_adamw(landed["u2"], 0, up2, swap(m_ffn2_up), swap(v_ffn2_up), "adamw_ffn2_up")),
        "ffn2_down": _sum_adamw(landed["d2"], 0, ffn2_down, m_ffn2_down, v_ffn2_down, "adamw_ffn2_down"),
    }
    grad_in = _unpermute_in(_sum_chips(landed["in"], "sum_w_in"))
    result["w_in"] = (grad_in[None],) + tuple(
        o[None] for o in _adamw(w_in[0], grad_in, m_w_in[0], v_w_in[0], "adamw_w_in"))

    norm_names = ["ffn1_norm", "mix_norm", "mem_norm", "ffn2_norm"]
    norm_w = _pack_norms(ffn1_norm, mix_norm, mem_norm, ffn2_norm)
    norm_m = _pack_norms(m_ffn1_norm, m_mix_norm, m_mem_norm, m_ffn2_norm)
    norm_v = _pack_norms(v_ffn1_norm, v_mix_norm, v_mem_norm, v_ffn2_norm)
    outs = (norms_sum,) + tuple(_adamw(norm_w, norms_sum, norm_m, norm_v, "adamw_norms"))
    for i, k in enumerate(norm_names):
        result[k] = tuple(o[i:i + 1] for o in outs)

    small_names = ["fox_q_gain", "fox_k_gain", "swa_q_gain", "swa_k_gain", "mem_q_gain", "mem_k_gain",
                   "forget_bias", "swa_sinks"]
    small_m = _pack_small(m_fox_q_gain, m_fox_k_gain, m_swa_q_gain, m_swa_k_gain, m_mem_q_gain, m_mem_k_gain,
                          m_forget_bias, m_swa_sinks)
    small_v = _pack_small(v_fox_q_gain, v_fox_k_gain, v_swa_q_gain, v_swa_k_gain, v_mem_q_gain, v_mem_k_gain,
                          v_forget_bias, v_swa_sinks)
    outs = (small_sum,) + tuple(_adamw(sp, small_sum, small_m, small_v, "adamw_small"))
    for i, k in enumerate(small_names):
        width = N_LOGIT if k in ("forget_bias", "swa_sinks") else HEAD
        result[k] = tuple(o[i:i + 1, :width] for o in outs)

    order = ["ffn1_norm", "ffn1_gate", "ffn1_up", "ffn1_down", "mix_norm", "mem_norm", "w_in", "forget_bias",
             "w_mem_k", "w_mem_v", "fox_q_gain", "fox_k_gain", "swa_q_gain", "swa_k_gain", "swa_sinks",
             "mem_q_gain", "mem_k_gain", "w_out", "ffn2_norm", "ffn2_gate", "ffn2_up", "ffn2_down"]
    result = {k: tuple(v) for k, v in result.items()}
    flat = [loss, grad_x[None]]
    for kind in range(4):
        flat += [result[k][kind] for k in order]
    return tuple(flat)
```

```python
import functools

import jax
import jax.numpy as jnp
from jax import lax
from jax.experimental import pallas as pl
from jax.experimental.pallas import tpu as pltpu

F32 = jnp.float32
BF16 = jnp.bfloat16
MESH = pl.DeviceIdType.MESH
ANY = pl.BlockSpec(memory_space=pl.ANY)

N_DEV = 8
EPS = 1e-6
NEG_INF = -1e30
HEAD = 128
FOX_H, SWA_H, SWA_KV, MEM_H = 6, 6, 2, 4
FOX_W, SWA_W, SWA_KV_W, MEM_W = FOX_H * HEAD, SWA_H * HEAD, SWA_KV * HEAD, MEM_H * HEAD
SCALE = HEAD ** -0.5
SWA_BLOCK = 128
C_FQ, C_FK, C_FV = 0, FOX_W, 2 * FOX_W
C_SQ = 3 * FOX_W
C_SK = C_SQ + SWA_W
C_SV = C_SK + SWA_KV_W
C_MQ = C_SV + SWA_KV_W
C_FL = C_MQ + MEM_W
IN_W = C_FL + HEAD
N_LOGIT = FOX_H
R_FQ, R_FK, R_SQ, R_SK, R_MQ, R_MK, R_FB, R_SINK = range(8)
ADAM_LR, ADAM_B1, ADAM_B2, ADAM_EPS, ADAM_WD, ADAM_STEP = 0.001, 0.9, 0.999, 1e-08, 0.01, 10
VMEM_BYTES = 56 * 1024 * 1024

DN = {
    "nn": (((1,), (0,)), ((), ())),
    "nt": (((1,), (1,)), ((), ())),
    "tn": (((0,), (0,)), ((), ())),
}


def _params(n_axes):
    return pltpu.CompilerParams(dimension_semantics=("arbitrary",) * n_axes, vmem_limit_bytes=VMEM_BYTES)


def _dot(a, b, dims="nn"):
    return lax.dot_general(a.astype(BF16), b.astype(BF16), DN[dims], preferred_element_type=F32)


def _sigmoid(x):
    return 0.5 * jnp.tanh(0.5 * x) + 0.5


def _me():
    return lax.axis_index("x"), lax.axis_index("y"), lax.axis_index("c")


def _lin(p):
    return 4 * p[0] + 2 * p[1] + p[2]


def _rows_tile(rows, row_bytes, budget=4 << 20, mult=16):
    best = None
    for k in range(1, rows + 1):
        if rows % k == 0 and (rows // k) % mult == 0 and (rows // k) * row_bytes <= budget:
            best = rows // k
            break
    assert best is not None, (rows, row_bytes)
    return best


class _Ride:
    def __init__(self, inputs, out_shapes, aliases, n_remote, n_local, start, wait):
        self.inputs, self.out_shapes, self.aliases = list(inputs), list(out_shapes), dict(aliases)
        self.n_remote, self.n_local, self.start, self.wait = n_remote, n_local, start, wait


def _remote(src, dst, send, recv, k, to):
    return pltpu.make_async_remote_copy(src_ref=src, dst_ref=dst, send_sem=send.at[k], recv_sem=recv.at[k],
                                        device_id=to, device_id_type=MESH)


def _other_chips(x, y):
    return [(1 - x, y), (x, 1 - y), (1 - x, 1 - y)]


ALL_CHIPS = [(0, 0), (0, 1), (1, 0), (1, 1)]


def _rows_of(ref, rows, slot=None):
    if slot is None:
        return ref if rows is None else ref.at[pl.ds(rows[0], rows[1])]
    return ref.at[slot] if rows is None else ref.at[slot, pl.ds(rows[0], rows[1])]


def _ride_gather_chips(xs, rows=None, into=None):
    n = len(xs)

    def copies(ins, outs, send, recv):
        x, y, c = _me()
        out = []
        for a in range(n):
            for j, chip in enumerate(_other_chips(x, y)):
                peer = (*chip, c)
                src = _rows_of(ins[a], rows)
                out.append((_remote(src, _rows_of(outs[a], rows, _lin((x, y, c))), send, recv, 3 * a + j, peer),
                            _remote(src, _rows_of(outs[a], rows, _lin(peer)), send, recv, 3 * a + j, peer)))
        return out

    def mine(ins, outs, local):
        me = _lin(_me())
        return [pltpu.make_async_copy(_rows_of(ins[a], rows), _rows_of(outs[a], rows, me), local.at[a])
                for a in range(n)]

    def start(ins, outs, send, recv, local):
        for cp in mine(ins, outs, local):
            cp.start()
        for sent, _ in copies(ins, outs, send, recv):
            sent.start()

    def wait(ins, outs, send, recv, local):
        for sent, landed in copies(ins, outs, send, recv):
            landed.wait_recv()
            sent.wait_send()
        for cp in mine(ins, outs, local):
            cp.wait()

    shapes = [jax.ShapeDtypeStruct((N_DEV,) + x.shape, x.dtype) for x in xs]
    if into is None:
        return _Ride(xs, shapes, {}, 3 * n, n, start, wait)
    return _Ride(list(xs) + list(into), shapes, {n + a: a for a in range(n)}, 3 * n, n, start, wait)


def _ride_gather_sibling(bufs):
    n = len(bufs)

    def copies(outs, send, recv):
        x, y, c = _me()
        out = []
        for a in range(n):
            for q, (px, py) in enumerate(ALL_CHIPS):
                there = outs[a].at[4 * px + 2 * py + c]
                here = outs[a].at[4 * px + 2 * py + 1 - c]
                out.append((_remote(there, there, send, recv, 4 * a + q, (x, y, 1 - c)),
                            _remote(here, here, send, recv, 4 * a + q, (x, y, 1 - c))))
        return out

    def start(ins, outs, send, recv, local):
        for sent, _ in copies(outs, send, recv):
            sent.start()

    def wait(ins, outs, send, recv, local):
        for sent, landed in copies(outs, send, recv):
            landed.wait_recv()
            sent.wait_send()

    shapes = [jax.ShapeDtypeStruct(b.shape, b.dtype) for b in bufs]
    return _Ride(bufs, shapes, {a: a for a in range(n)}, 4 * n, 0, start, wait)


def _ride_gather(xs, mid_frac, rows=None, into=None):
    n = len(xs)
    chips = _ride_gather_chips(xs, rows, into)

    def sibling_copies(outs, send, recv):
        x, y, c = _me()
        out = []
        for a in range(n):
            for q, (px, py) in enumerate(ALL_CHIPS):
                there = _rows_of(outs[a], rows, 4 * px + 2 * py + c)
                here = _rows_of(outs[a], rows, 4 * px + 2 * py + 1 - c)
                k = 3 * n + 4 * a + q
                out.append((_remote(there, there, send, recv, k, (x, y, 1 - c)),
                            _remote(here, here, send, recv, k, (x, y, 1 - c))))
        return out

    def mid(ins, outs, send, recv, local):
        chips.wait(ins, outs, send, recv, local)
        for sent, _ in sibling_copies(outs, send, recv):
            sent.start()

    def wait(ins, outs, send, recv, local):
        for sent, landed in sibling_copies(outs, send, recv):
            landed.wait_recv()
            sent.wait_send()

    ride = _Ride(chips.inputs, chips.out_shapes, chips.aliases, 7 * n, n, chips.start, wait)
    ride.mid, ride.mid_frac = mid, mid_frac
    return ride


def _ride_scatter_sibling(parts):
    n = len(parts)

    def copies(ins, outs, send, recv):
        x, y, c = _me()
        out = []
        for a in range(n):
            for q, (px, py) in enumerate(ALL_CHIPS):
                cp = _remote(ins[a].at[4 * px + 2 * py + 1 - c], outs[a].at[q], send, recv, 4 * a + q, (x, y, 1 - c))
                out.append(cp)
        return out

    def start(ins, outs, send, recv, local):
        for cp in copies(ins, outs, send, recv):
            cp.start()

    def wait(ins, outs, send, recv, local):
        for cp in copies(ins, outs, send, recv):
            cp.wait_recv()
            cp.wait_send()

    shapes = [jax.ShapeDtypeStruct((4,) + p.shape[1:], p.dtype) for p in parts]
    return _Ride(parts, shapes, {}, 4 * n, 0, start, wait)


def _ride_scatter_chips(pairs, rows=None, into=None):
    n = len(pairs)

    def part(ref, slot):
        return _rows_of(ref, rows, slot)

    def copies(ins, outs, send, recv):
        x, y, c = _me()
        out = []
        for a in range(n):
            for j, (px, py) in enumerate(_other_chips(x, y)):
                peer = (px, py, c)
                src = part(ins[a], 2 * px + py)
                out.append((_remote(src, part(outs[a], 2 * x + y), send, recv, 3 * a + j, peer),
                            _remote(src, part(outs[a], 2 * px + py), send, recv, 3 * a + j, peer)))
        return out

    def mine(ins, outs, local):
        x, y, _ = _me()
        return [pltpu.make_async_copy(part(ins[a], 2 * x + y), part(outs[a], 2 * x + y), local.at[a])
                for a in range(n)]

    def start(ins, outs, send, recv, local):
        for cp in mine(ins, outs, local):
            cp.start()
        for sent, _ in copies(ins, outs, send, recv):
            sent.start()

    def wait(ins, outs, send, recv, local):
        for sent, landed in copies(ins, outs, send, recv):
            landed.wait_recv()
            sent.wait_send()
        for cp in mine(ins, outs, local):
            cp.wait()

    shapes = [jax.ShapeDtypeStruct(p.shape, p.dtype) for p in pairs]
    if into is None:
        return _Ride(pairs, shapes, {}, 3 * n, n, start, wait)
    return _Ride(list(pairs) + list(into), shapes, {n + a: a for a in range(n)}, 3 * n, n, start, wait)


def _call(name, body, grid, in_specs, out_specs, out_shape, operands, scratch=(), rides=()):
    n_in, n_out, n_scr = len(operands), len(out_shape), len(scratch)
    ride_in, ride_out, ride_scr, aliases, spans = [], [], [], {}, []
    for r in rides:
        for i, o in r.aliases.items():
            aliases[n_in + len(ride_in) + i] = n_out + len(ride_out) + o
        spans.append((len(ride_in), len(r.inputs), len(ride_out), len(r.out_shapes)))
        ride_in += r.inputs
        ride_out += r.out_shapes
        ride_scr += [pltpu.SemaphoreType.DMA((r.n_remote,)), pltpu.SemaphoreType.DMA((r.n_remote,)),
                     pltpu.SemaphoreType.DMA((max(r.n_local, 1),))]

    def wrapped(*refs):
        c_in, r_in = refs[:n_in], refs[n_in:n_in + len(ride_in)]
        p = n_in + len(ride_in)
        c_out, r_out = refs[p:p + n_out], refs[p + n_out:p + n_out + len(ride_out)]
        p += n_out + len(ride_out)
        c_scr, r_scr = refs[p:p + n_scr], refs[p + n_scr:]

        n_steps = functools.reduce(lambda a, b: a * b, grid, 1)
        step = functools.reduce(lambda acc, ax: acc * grid[ax] + pl.program_id(ax), range(len(grid)), 0)

        def each(method, at):
            for k, (r, (i0, ni, o0, no)) in enumerate(zip(rides, spans)):
                fn = getattr(r, method, None)
                if fn is None:
                    continue
                run = functools.partial(fn, r_in[i0:i0 + ni], r_out[o0:o0 + no], *r_scr[3 * k:3 * k + 3])
                if grid:
                    pl.when(step == at(r))(run)
                else:
                    run()

        each("start", lambda r: 0)
        each("mid", lambda r: int(r.mid_frac * (n_steps - 1)))
        body(*c_in, *c_out, *c_scr)
        each("wait", lambda r: n_steps - 1)

    outs = pl.pallas_call(
        wrapped, grid=grid, in_specs=list(in_specs) + [ANY] * len(ride_in),
        out_specs=list(out_specs) + [ANY] * len(ride_out), out_shape=list(out_shape) + ride_out,
        scratch_shapes=list(scratch) + ride_scr, input_output_aliases=aliases,
        compiler_params=_params(len(grid)), name=name)(*operands, *ride_in)
    outs = list(outs)
    ride_results = [outs[n_out + o0:n_out + o0 + no] for (_, _, o0, no) in spans]
    return outs[:n_out], ride_results


class _Tail:
    def __init__(self, extra, out_shapes, out_specs, fn):
        self.extra, self.out_shapes, self.out_specs, self.fn = list(extra), list(out_shapes), list(out_specs), fn


def _mm(name, pairs, dims, grid, out_shape, out_spec, res=None, res_spec=None, alpha=1.0, rides=(), tail=None):
    n = len(pairs)
    nk = grid[-1]
    kax = len(grid) - 1
    acc_shape = tuple(d for d in out_spec.block_shape if d is not None)
    n_extra = len(tail.extra) if tail else 0
    n_outs = len(tail.out_shapes) if tail else 1

    def body(*refs):
        pos = 2 * n
        r_ref = None
        if res is not None:
            r_ref = refs[pos]
            pos += 1
        x_refs = refs[pos:pos + n_extra]
        o_refs = refs[pos + n_extra:pos + n_extra + n_outs]
        pos += n_extra + n_outs
        part = None
        for p in range(n):
            d = _dot(refs[2 * p][...], refs[2 * p + 1][...], dims)
            part = d if part is None else part + d

        def finish(acc):
            if alpha != 1.0:
                acc = acc * alpha
            if r_ref is not None:
                acc = r_ref[...] + acc
            if tail:
                tail.fn(acc, x_refs, o_refs)
            else:
                o_refs[0][...] = acc.astype(o_refs[0].dtype)

        if nk == 1:
            finish(part)
        else:
            acc_ref = refs[pos]
            k = pl.program_id(kax)

            @pl.when(k == 0)
            def _():
                acc_ref[...] = part

            @pl.when(k > 0)
            def _():
                acc_ref[...] += part

            @pl.when(k == nk - 1)
            def _():
                finish(acc_ref[...])

    operands, in_specs = [], []
    for a, a_spec, b, b_spec in pairs:
        operands += [a, b]
        in_specs += [a_spec, b_spec]
    if res is not None:
        operands.append(res)
        in_specs.append(res_spec)
    for a, a_spec in (tail.extra if tail else []):
        operands.append(a)
        in_specs.append(a_spec)
    outs, ride_results = _call(name, body, grid, in_specs, tail.out_specs if tail else [out_spec],
                               tail.out_shapes if tail else [out_shape], operands,
                               scratch=[pltpu.VMEM(acc_shape, F32)] if nk > 1 else [], rides=rides)
    outs = outs if tail else outs[0]
    return (outs, ride_results) if rides else outs


def _accumulate(ref, part):
    @pl.when(pl.program_id(0) == 0)
    def _():
        ref[...] = part

    @pl.when(pl.program_id(0) > 0)
    def _():
        ref[...] += part


def _tail_norm(gain, t, d, tm):
    def fn(v, x_refs, o_refs):
        o_refs[0][...] = v
        r = lax.rsqrt(jnp.mean(v * v, axis=-1, keepdims=True) + EPS)
        o_refs[1][...] = (v * r * x_refs[0][...]).astype(BF16)

    rows = pl.BlockSpec((tm, d), lambda i, k: (i, 0))
    return _Tail([(gain, pl.BlockSpec((1, d), lambda i, k: (0, 0)))],
                 [jax.ShapeDtypeStruct((t, d), F32), jax.ShapeDtypeStruct((t, d), BF16)], [rows, rows], fn)


def _tail_loss(target, t, d, tm):
    def fn(v, x_refs, o_refs):
        err = v - x_refs[0][...]
        dy = err * (1.0 / d)
        o_refs[0][...] = dy
        o_refs[1][...] = dy.astype(BF16)
        _accumulate(o_refs[2], jnp.zeros((8, 128), F32) + jnp.sum(err * err))

    rows = pl.BlockSpec((tm, d), lambda i, k: (i, 0))
    return _Tail([(target, rows)],
                 [jax.ShapeDtypeStruct((t, d), F32), jax.ShapeDtypeStruct((t, d), BF16),
                  jax.ShapeDtypeStruct((8, 128), F32)],
                 [rows, rows, pl.BlockSpec((8, 128), lambda i, k: (0, 0))], fn)


def _tail_norm_bwd(x, gain, dres, t, d, tm):
    def fn(dy, x_refs, o_refs):
        xv = x_refs[0][...]
        r = lax.rsqrt(jnp.mean(xv * xv, axis=-1, keepdims=True) + EPS)
        xh = xv * r
        dxh = dy * x_refs[1][...]
        dx = r * (dxh - xh * jnp.mean(dxh * xh, axis=-1, keepdims=True)) + x_refs[2][...]
        o_refs[0][...] = dx
        o_refs[1][...] = dx.astype(BF16)
        _accumulate(o_refs[2], jnp.sum(dy * xh, axis=0, keepdims=True))

    rows = pl.BlockSpec((tm, d), lambda i, k: (i, 0))
    vec = pl.BlockSpec((1, d), lambda i, k: (0, 0))
    return _Tail([(x, rows), (gain, vec), (dres, rows)],
                 [jax.ShapeDtypeStruct((t, d), F32), jax.ShapeDtypeStruct((t, d), BF16),
                  jax.ShapeDtypeStruct((1, d), F32)], [rows, rows, vec], fn)


def _cast_bf16(x, name):
    rows, cols = x.shape
    tm = _rows_tile(rows, cols * 4)

    def body(x_ref, o_ref):
        o_ref[...] = x_ref[...].astype(BF16)

    spec = pl.BlockSpec((tm, cols), lambda i: (i, 0))
    return pl.pallas_call(body, grid=(rows // tm,), in_specs=[spec], out_specs=spec,
                          out_shape=jax.ShapeDtypeStruct(x.shape, BF16), compiler_params=_params(1), name=name)(x)


def _rms_fwd(x, gain, name):
    rows, d = x.shape
    tm = min(rows, 512)

    def body(x_ref, g_ref, o_ref):
        xv = x_ref[...]
        r = lax.rsqrt(jnp.mean(xv * xv, axis=-1, keepdims=True) + EPS)
        o_ref[...] = (xv * r * g_ref[...]).astype(BF16)

    spec = pl.BlockSpec((tm, d), lambda i: (i, 0))
    return pl.pallas_call(body, grid=(rows // tm,), in_specs=[spec, pl.BlockSpec((1, d), lambda i: (0, 0))],
                          out_specs=spec, out_shape=jax.ShapeDtypeStruct(x.shape, BF16),
                          compiler_params=_params(1), name=name)(x, gain)


def _rms_bwd(x, gain, dxn, dres, name, rides=()):
    rows, d = x.shape
    tm = min(rows, 256)
    with_res = dres is not None

    def body(*refs):
        if with_res:
            x_ref, g_ref, dy_ref, r_ref, dx_ref, dxb_ref, dg_ref = refs
        else:
            x_ref, g_ref, dy_ref, dx_ref, dxb_ref, dg_ref = refs
        xv = x_ref[...]
        r = lax.rsqrt(jnp.mean(xv * xv, axis=-1, keepdims=True) + EPS)
        xh = xv * r
        dy = dy_ref[...]
        dxh = dy * g_ref[...]
        dx = r * (dxh - xh * jnp.mean(dxh * xh, axis=-1, keepdims=True))
        if with_res:
            dx = dx + r_ref[...]
        dx_ref[...] = dx
        dxb_ref[...] = dx.astype(BF16)
        part = jnp.sum(dy * xh, axis=0, keepdims=True)

        @pl.when(pl.program_id(0) == 0)
        def _():
            dg_ref[...] = part

        @pl.when(pl.program_id(0) > 0)
        def _():
            dg_ref[...] += part

    spec = pl.BlockSpec((tm, d), lambda i: (i, 0))
    vec = pl.BlockSpec((1, d), lambda i: (0, 0))
    ops = [x, gain, dxn] + ([dres] if with_res else [])
    outs, ride_results = _call(
        name, body, (rows // tm,), [spec, vec, spec] + ([spec] if with_res else []), [spec, spec, vec],
        [jax.ShapeDtypeStruct(x.shape, F32), jax.ShapeDtypeStruct(x.shape, BF16), jax.ShapeDtypeStruct((1, d), F32)],
        ops, rides=rides)
    return (outs, ride_results) if rides else outs


ROW_CHUNK = 256
SHARD_ROWS = 1024


def _ffn_up(xn, wg, wu, tag, rides=()):
    t, d = xn.shape
    nd, fs, _ = wg.shape
    tm = min(t, SHARD_ROWS)
    rc = min(tm, ROW_CHUNK)

    def body(x_ref, wg_ref, wu_ref, a_ref, b_ref, h_ref):
        for r in range(0, tm, rc):
            xv = x_ref[r:r + rc, :]
            g = _dot(xv, wg_ref[...], "nt")
            u = _dot(xv, wu_ref[...], "nt")
            sig = _sigmoid(g)
            silu = g * sig
            a_ref[r:r + rc, :] = (0.5 * u * (sig + silu * (1.0 - sig))).astype(BF16)
            b_ref[r:r + rc, :] = (0.5 * silu).astype(BF16)
            h_ref[r:r + rc, :] = (silu * u).astype(BF16)

    wspec = pl.BlockSpec((None, fs, d), lambda j, i: (j, 0, 0))
    hspec = pl.BlockSpec((None, tm, fs), lambda j, i: (j, i, 0))
    hid = jax.ShapeDtypeStruct((nd, t, fs), BF16)
    return _call(f"{tag}_up", body, (nd, t // tm), [pl.BlockSpec((tm, d), lambda j, i: (i, 0)), wspec, wspec],
                 [hspec] * 3, [hid] * 3, [xn, wg, wu], rides=rides)


CONTRACT_ROWS = 256


def _ffn_contract(hid, w, name, res=None, alpha=1.0, rides=(), tail=None):
    nd, t, fs = hid.shape
    d = w.shape[2]
    tm = min(t, CONTRACT_ROWS)
    xspec = pl.BlockSpec((tm, d), lambda i, k: (i, 0))
    pairs = [(hid, pl.BlockSpec((None, tm, fs), lambda i, k, s=s: (s, i, 0)),
              w, pl.BlockSpec((None, fs, d), lambda i, k, s=s: (s, 0, 0), pipeline_mode=pl.Buffered(1)))
             for s in range(nd)]
    return _mm(name, pairs, "nn", (t // tm, 1), jax.ShapeDtypeStruct((t, d), F32), xspec,
               res=res, res_spec=xspec if res is not None else None, alpha=alpha, rides=rides, tail=tail)


def _ffn_down(x, h, wd, tag, rides=(), tail=None):
    return _ffn_contract(h, wd, f"{tag}_down", res=x, alpha=0.5, rides=rides, tail=tail)


def _ffn_dact(dyb, wd, a, b, tag, rides=()):
    nd, t, fs = a.shape
    d = dyb.shape[1]
    tm = min(t, SHARD_ROWS)
    rc = min(tm, ROW_CHUNK)

    def body(dy_ref, wd_ref, a_ref, b_ref, dg_ref, du_ref):
        for r in range(0, tm, rc):
            dh = _dot(dy_ref[r:r + rc, :], wd_ref[...], "nt")
            dg_ref[r:r + rc, :] = (dh * a_ref[r:r + rc, :].astype(F32)).astype(BF16)
            du_ref[r:r + rc, :] = (dh * b_ref[r:r + rc, :].astype(F32)).astype(BF16)

    hspec = pl.BlockSpec((None, tm, fs), lambda j, i: (j, i, 0))
    hid = jax.ShapeDtypeStruct((nd, t, fs), BF16)
    return _call(f"{tag}_dact", body, (nd, t // tm),
                 [pl.BlockSpec((tm, d), lambda j, i: (i, 0)), pl.BlockSpec((None, fs, d), lambda j, i: (j, 0, 0)),
                  hspec, hspec], [hspec] * 2, [hid] * 2, [dyb, wd, a, b], rides=rides)


def _ffn_dw(hid, act, alpha, name, rides=()):
    nd, t, fs = hid.shape
    d = act.shape[1]
    tk = t
    return _mm(name, [(hid, pl.BlockSpec((None, tk, fs), lambda j, k: (j, k, 0)),
                       act, pl.BlockSpec((tk, d), lambda j, k: (k, 0), pipeline_mode=pl.Buffered(1)))],
               "tn", (nd, t // tk),
               jax.ShapeDtypeStruct((nd, fs, d), BF16), pl.BlockSpec((None, fs, d), lambda j, k: (j, 0, 0)),
               alpha=alpha, rides=rides)


def _head_norm(x, gain):
    r = lax.rsqrt(jnp.mean(x * x, axis=-1, keepdims=True) + EPS)
    return x * r * gain


def _head_norm_bwd(x, gain, dy):
    r = lax.rsqrt(jnp.mean(x * x, axis=-1, keepdims=True) + EPS)
    xh = x * r
    dxh = dy * gain
    dx = r * (dxh - xh * jnp.mean(dxh * xh, axis=-1, keepdims=True))
    return dx, jnp.sum(dy * xh, axis=0, keepdims=True)


def _hs(h, base=0):
    return slice(base + h * HEAD, base + (h + 1) * HEAD)


def _tri(n, lower):
    r = lax.broadcasted_iota(jnp.int32, (n, n), 0)
    c = lax.broadcasted_iota(jnp.int32, (n, n), 1)
    return ((r >= c) if lower else (r <= c)).astype(F32)


def _attn_pre(proj, sp, name):
    t = proj.shape[0]
    tm = min(t, 256)

    def body(p_ref, sp_ref, fq, fk, fv, sq, sk, sv, mq, cc, carry):
        @pl.when(pl.program_id(0) == 0)
        def _():
            carry[...] = jnp.zeros_like(carry)

        for h in range(FOX_H):
            fq[:, _hs(h)] = _head_norm(p_ref[:, _hs(h, C_FQ)], sp_ref[R_FQ:R_FQ + 1, :]).astype(BF16)
            fk[:, _hs(h)] = _head_norm(p_ref[:, _hs(h, C_FK)], sp_ref[R_FK:R_FK + 1, :]).astype(BF16)
        fv[...] = p_ref[:, C_FV:C_FV + FOX_W].astype(BF16)
        for h in range(SWA_H):
            sq[:, _hs(h)] = _head_norm(p_ref[:, _hs(h, C_SQ)], sp_ref[R_SQ:R_SQ + 1, :]).astype(BF16)
        for h in range(SWA_KV):
            sk[:, _hs(h)] = _head_norm(p_ref[:, _hs(h, C_SK)], sp_ref[R_SK:R_SK + 1, :]).astype(BF16)
        sv[...] = p_ref[:, C_SV:C_SV + SWA_KV_W].astype(BF16)
        for h in range(MEM_H):
            mq[:, _hs(h)] = _head_norm(p_ref[:, _hs(h, C_MQ)], sp_ref[R_MQ:R_MQ + 1, :]).astype(BF16)
        z = p_ref[:, C_FL:C_FL + HEAD] + sp_ref[R_FB:R_FB + 1, :]
        lane = lax.broadcasted_iota(jnp.int32, z.shape, 1)
        log_f = jnp.minimum(z, 0.0) - jnp.log(1.0 + jnp.exp(-jnp.abs(z)))
        log_f = jnp.where(lane < N_LOGIT, log_f, 0.0)
        c = jnp.dot(_tri(tm, True), log_f, precision=lax.Precision.HIGHEST, preferred_element_type=F32)
        c = c + carry[0:1, :]
        cc[...] = c
        carry[...] = jnp.broadcast_to(c[tm - 1:tm, :], carry.shape)

    def rows(w):
        return pl.BlockSpec((tm, w), lambda i: (i, 0))

    def shape(w, dt):
        return jax.ShapeDtypeStruct((t, w), dt)

    widths = [FOX_W, FOX_W, FOX_W, SWA_W, SWA_KV_W, SWA_KV_W, MEM_W]
    return pl.pallas_call(
        body, grid=(t // tm,), in_specs=[rows(IN_W), pl.BlockSpec((16, 128), lambda i: (0, 0))],
        out_specs=[rows(w) for w in widths] + [rows(HEAD)],
        out_shape=[shape(w, BF16) for w in widths] + [shape(HEAD, F32)],
        scratch_shapes=[pltpu.VMEM((8, 128), F32)], compiler_params=_params(1), name=name)(proj, sp)


def _attn_post_bwd(proj, sp, dfq, dfk, dfv, dsq, dsk, dsv, dmq, dc_col, dc_row_t, name):
    t = proj.shape[0]
    tm = min(t, 256)
    nb = t // tm

    def body(p_ref, sp_ref, dfq_r, dfk_r, dfv_r, dsq_r, dsk_r, dsv_r, dmq_r, dcc_r, dcr_r, dp_ref, dsp_ref, carry):
        @pl.when(pl.program_id(0) == 0)
        def _():
            carry[...] = jnp.zeros_like(carry)
            dsp_ref[...] = jnp.zeros_like(dsp_ref)

        def group(n_heads, col, row, d_ref):
            total = None
            for h in range(n_heads):
                dx, dg = _head_norm_bwd(p_ref[:, _hs(h, col)], sp_ref[row:row + 1, :], d_ref[:, _hs(h)])
                dp_ref[:, _hs(h, col)] = dx.astype(BF16)
                total = dg if total is None else total + dg
            dsp_ref[row:row + 1, :] += total

        group(FOX_H, C_FQ, R_FQ, dfq_r)
        group(FOX_H, C_FK, R_FK, dfk_r)
        dp_ref[:, C_FV:C_FV + FOX_W] = dfv_r[...].astype(BF16)
        group(SWA_H, C_SQ, R_SQ, dsq_r)
        group(SWA_KV, C_SK, R_SK, dsk_r)
        dp_ref[:, C_SV:C_SV + SWA_KV_W] = dsv_r[...].astype(BF16)
        group(MEM_H, C_MQ, R_MQ, dmq_r)
        dc = dcc_r[...] - dcr_r[...]
        rc = jnp.dot(_tri(tm, False), dc, precision=lax.Precision.HIGHEST, preferred_element_type=F32)
        rc = rc + carry[0:1, :]
        carry[...] = jnp.broadcast_to(rc[0:1, :], carry.shape)
        z = p_ref[:, C_FL:C_FL + HEAD] + sp_ref[R_FB:R_FB + 1, :]
        dz = rc * _sigmoid(-z)
        dp_ref[:, C_FL:C_FL + HEAD] = dz.astype(BF16)
        dsp_ref[R_FB:R_FB + 1, :] += jnp.sum(dz, axis=0, keepdims=True)

    def rows(w):
        return pl.BlockSpec((tm, w), lambda i: (nb - 1 - i, 0))

    small = pl.BlockSpec((16, 128), lambda i: (0, 0))
    widths = [FOX_W, FOX_W, FOX_W, SWA_W, SWA_KV_W, SWA_KV_W, MEM_W, HEAD, HEAD]
    return pl.pallas_call(
        body, grid=(nb,), in_specs=[rows(IN_W), small] + [rows(w) for w in widths],
        out_specs=[rows(IN_W), small],
        out_shape=[jax.ShapeDtypeStruct((t, IN_W), BF16), jax.ShapeDtypeStruct((16, 128), F32)],
        scratch_shapes=[pltpu.VMEM((8, 128), F32)], compiler_params=_params(1), name=name,
    )(proj, sp, dfq, dfk, dfv, dsq, dsk, dsv, dmq, dc_col, dc_row_t)


def _head_column(values):
    rows = values[0].shape[0]
    lane = lax.broadcasted_iota(jnp.int32, (rows, HEAD), 1)
    out = jnp.zeros((rows, HEAD), F32)
    for h, v in enumerate(values):
        out = jnp.where(lane == h, v, out)
    return out


def _head_row(values, n_rows=8):
    cols = values[0].shape[1]
    sub = lax.broadcasted_iota(jnp.int32, (n_rows, cols), 0)
    out = jnp.zeros((n_rows, cols), F32)
    for h, v in enumerate(values):
        out = jnp.where(sub == h, v, out)
    return out


def _delta(dmixed, o_a, o_b, o_c, name):
    t = dmixed.shape[0]
    tm = min(t, 512)

    def body(d_ref, a_ref, b_ref, c_ref, o_ref, rep_ref):
        cols = []
        for ref, n_heads, base in ((a_ref, FOX_H, 0), (b_ref, SWA_H, FOX_W), (c_ref, MEM_H, FOX_W + SWA_W)):
            for h in range(n_heads):
                cols.append(jnp.sum(d_ref[:, _hs(h, base)] * ref[:, _hs(h)], axis=-1, keepdims=True))
        o_ref[...] = _head_column(cols)
        for h in range(FOX_H):
            rep_ref[h] = jnp.broadcast_to(cols[h], (tm, HEAD))

    def rows(w):
        return pl.BlockSpec((tm, w), lambda i: (i, 0))

    return pl.pallas_call(body, grid=(t // tm,), in_specs=[rows(dmixed.shape[1]), rows(FOX_W), rows(SWA_W), rows(MEM_W)],
                          out_specs=[rows(HEAD), pl.BlockSpec((FOX_H, tm, HEAD), lambda i: (0, i, 0))],
                          out_shape=[jax.ShapeDtypeStruct((t, HEAD), F32), jax.ShapeDtypeStruct((FOX_H, t, HEAD), F32)],
                          compiler_params=_params(1), name=name)(dmixed, o_a, o_b, o_c)


def _fox_fwd(fq, fk, fv, c_rep, c_row, name, rides=()):
    t = fq.shape[0]
    tb = min(t, 512)
    nb = t // tb
    n_tiles = tb // HEAD

    def body(q_ref, k_ref, v_ref, cc_ref, cr_ref, o_ref, lse_ref, m_s, l_s, acc_s):
        qi, ki = pl.program_id(0), pl.program_id(1)

        @pl.when(ki == 0)
        def _():
            m_s[...] = jnp.full_like(m_s, NEG_INF)
            l_s[...] = jnp.zeros_like(l_s)
            acc_s[...] = jnp.zeros_like(acc_s)

        def step(diagonal):
            if diagonal:
                r = lax.broadcasted_iota(jnp.int32, (tb, HEAD), 0)
                c = lax.broadcasted_iota(jnp.int32, (tb, HEAD), 1)
            for h in range(FOX_H):
                s = _dot(q_ref[:, _hs(h)], k_ref[:, _hs(h)], "nt")
                cc = cc_ref[h]
                tiles, m_cur = [], None
                for j in range(n_tiles):
                    st = s[:, _hs(j)] * SCALE + cc - cr_ref[h:h + 1, _hs(j)]
                    if diagonal:
                        st = jnp.where(r >= c + j * HEAD, st, NEG_INF)
                    tiles.append(st)
                    m_cur = st if m_cur is None else jnp.maximum(m_cur, st)
                m_prev = m_s[h]
                m_new = jnp.maximum(m_prev, jnp.max(m_cur, axis=-1, keepdims=True))
                alpha = jnp.exp(m_prev - m_new)
                ps = [jnp.exp(st - m_new) for st in tiles]
                l_cur = ps[0]
                for p in ps[1:]:
                    l_cur = l_cur + p
                l_s[h] = alpha * l_s[h] + jnp.sum(l_cur, axis=-1, keepdims=True)
                p = jnp.concatenate([p.astype(BF16) for p in ps], axis=1)
                acc_s[:, _hs(h)] = alpha * acc_s[:, _hs(h)] + _dot(p, v_ref[:, _hs(h)])
                m_s[h] = m_new

        @pl.when(ki < qi)
        def _():
            step(False)

        @pl.when(ki == qi)
        def _():
            step(True)
            for h in range(FOX_H):
                o_ref[:, _hs(h)] = acc_s[:, _hs(h)] / l_s[h]
                lse_ref[h] = m_s[h] + jnp.log(l_s[h])

    qspec = pl.BlockSpec((tb, FOX_W), lambda i, j: (i, 0))
    kspec = pl.BlockSpec((tb, FOX_W), lambda i, j: (jnp.minimum(i, j), 0))
    rep = pl.BlockSpec((FOX_H, tb, HEAD), lambda i, j: (0, i, 0))
    return _call(
        name, body, (nb, nb),
        [qspec, kspec, kspec, rep, pl.BlockSpec((8, tb), lambda i, j: (0, jnp.minimum(i, j)))],
        [qspec, rep],
        [jax.ShapeDtypeStruct((t, FOX_W), F32), jax.ShapeDtypeStruct((FOX_H, t, HEAD), F32)],
        [fq, fk, fv, c_rep, c_row],
        scratch=[pltpu.VMEM((FOX_H, tb, HEAD), F32), pltpu.VMEM((FOX_H, tb, HEAD), F32), pltpu.VMEM((tb, FOX_W), F32)],
        rides=rides)


def _fox_bwd(fq, fk, fv, c_rep, c_row, dmixed, lse, delta, name, rides=()):
    t = fq.shape[0]
    tb = min(t, 512)
    nb = t // tb
    n_tiles = tb // HEAD

    def body(q_ref, k_ref, v_ref, cc_ref, cr_ref, do_ref, lse_ref, dl_ref,
             dq_ref, dk_ref, dv_ref, dcc_ref, dcr_ref):
        ki, qi = pl.program_id(0), pl.program_id(1)

        @pl.when((ki == 0) & (qi == 0))
        def _():
            dq_ref[...] = jnp.zeros_like(dq_ref)
            dcc_ref[...] = jnp.zeros_like(dcc_ref)

        @pl.when(qi == 0)
        def _():
            dk_ref[...] = jnp.zeros_like(dk_ref)
            dv_ref[...] = jnp.zeros_like(dv_ref)
            dcr_ref[...] = jnp.zeros_like(dcr_ref)

        def step(diagonal):
            rows = pl.ds(pl.multiple_of(qi * tb, tb), tb)
            if diagonal:
                r = lax.broadcasted_iota(jnp.int32, (tb, HEAD), 0)
                c = lax.broadcasted_iota(jnp.int32, (tb, HEAD), 1)
            row_sums, col_sums = [], []
            for h in range(FOX_H):
                q, k, v, do = q_ref[:, _hs(h)], k_ref[:, _hs(h)], v_ref[:, _hs(h)], do_ref[:, _hs(h)]
                s = _dot(q, k, "nt")
                dp = _dot(do, v, "nt")
                cc, lse_h, dl_h = cc_ref[h], lse_ref[h], dl_ref[h]
                ps, dss, row = [], [], None
                for j in range(n_tiles):
                    st = s[:, _hs(j)] * SCALE + cc - cr_ref[h:h + 1, _hs(j)]
                    if diagonal:
                        st = jnp.where(r >= c + j * HEAD, st, NEG_INF)
                    pt = jnp.exp(st - lse_h)
                    dst = pt * (dp[:, _hs(j)] - dl_h)
                    ps.append(pt.astype(BF16))
                    dss.append(dst)
                    row = dst if row is None else row + dst
                p = jnp.concatenate(ps, axis=1)
                ds = jnp.concatenate(dss, axis=1)
                dsb = ds.astype(BF16)
                dv_ref[:, _hs(h)] += _dot(p, do, "tn")
                dk_ref[:, _hs(h)] += _dot(dsb, q, "tn") * SCALE
                dq_ref[rows, _hs(h)] += _dot(dsb, k) * SCALE
                row_sums.append(jnp.sum(row, axis=1, keepdims=True))
                col_sums.append(jnp.sum(ds, axis=0, keepdims=True))
            dcc_ref[rows, :] += _head_column(row_sums)
            dcr_ref[...] += _head_row(col_sums)

        @pl.when(qi > ki)
        def _():
            step(False)

        @pl.when(qi == ki)
        def _():
            step(True)

    def qmap(j, i):
        return (jnp.maximum(i, j), 0)

    qspec = pl.BlockSpec((tb, FOX_W), qmap)
    kspec = pl.BlockSpec((tb, FOX_W), lambda j, i: (j, 0))
    rep = pl.BlockSpec((FOX_H, tb, HEAD), lambda j, i: (0, jnp.maximum(i, j), 0))
    rowspec = pl.BlockSpec((8, tb), lambda j, i: (0, j))
    return _call(
        name, body, (nb, nb), [qspec, kspec, kspec, rep, rowspec, qspec, rep, rep],
        [pl.BlockSpec((t, FOX_W), lambda j, i: (0, 0)), kspec, kspec,
         pl.BlockSpec((t, HEAD), lambda j, i: (0, 0)), rowspec],
        [jax.ShapeDtypeStruct((t, FOX_W), F32)] * 3 + [jax.ShapeDtypeStruct((t, HEAD), F32),
                                                       jax.ShapeDtypeStruct((8, t), F32)],
        [fq, fk, fv, c_rep, c_row, dmixed, lse, delta], rides=rides)


def _swa_logits(q, k_cur, k_prev, slope, first_block):
    w = SWA_BLOCK
    r = lax.broadcasted_iota(jnp.int32, (w, w), 0)
    j = lax.broadcasted_iota(jnp.int32, (w, w), 1)
    dist_cur = r - j
    dist_prev = w + r - j
    s_cur = _dot(q, k_cur, "nt") * SCALE - slope * dist_cur.astype(F32)
    s_cur = jnp.where(dist_cur >= 0, s_cur, NEG_INF)
    s_prev = _dot(q, k_prev, "nt") * SCALE - slope * dist_prev.astype(F32)
    s_prev = jnp.where((j > r) & jnp.logical_not(first_block), s_prev, NEG_INF)
    return s_cur, s_prev


def _slope(h):
    return float(2.0 ** (-8.0 * (h + 1) / SWA_H))


def _swa_fwd(sq, sk, sv, sp, name, rides=()):
    t = sq.shape[0]
    w = SWA_BLOCK
    nb = t // w
    group = SWA_H // SWA_KV

    def body(q_ref, kp_ref, kc_ref, vp_ref, vc_ref, sp_ref, o_ref, lse_ref):
        first = pl.program_id(0) == 0
        lses = []
        for h in range(SWA_H):
            kv = h // group
            s_cur, s_prev = _swa_logits(q_ref[:, _hs(h)], kc_ref[:, _hs(kv)], kp_ref[:, _hs(kv)], _slope(h), first)
            sink = sp_ref[R_SINK:R_SINK + 1, h:h + 1]
            m = jnp.maximum(jnp.maximum(jnp.max(s_cur, axis=-1, keepdims=True),
                                        jnp.max(s_prev, axis=-1, keepdims=True)), sink)
            p_cur = jnp.exp(s_cur - m)
            p_prev = jnp.exp(s_prev - m)
            l = jnp.sum(p_cur, axis=-1, keepdims=True) + jnp.sum(p_prev, axis=-1, keepdims=True) + jnp.exp(sink - m)
            o_ref[:, _hs(h)] = (_dot(p_cur, vc_ref[:, _hs(kv)]) + _dot(p_prev, vp_ref[:, _hs(kv)])) / l
            lses.append(m + jnp.log(l))
        lse_ref[...] = _head_column(lses)

    qspec = pl.BlockSpec((w, SWA_W), lambda n: (n, 0))
    cur = pl.BlockSpec((w, SWA_KV_W), lambda n: (n, 0))
    prev = pl.BlockSpec((w, SWA_KV_W), lambda n: (jnp.maximum(n - 1, 0), 0))
    return _call(
        name, body, (nb,), [qspec, prev, cur, prev, cur, pl.BlockSpec((16, 128), lambda n: (0, 0))],
        [qspec, pl.BlockSpec((w, HEAD), lambda n: (n, 0))],
        [jax.ShapeDtypeStruct((t, SWA_W), F32), jax.ShapeDtypeStruct((t, HEAD), F32)],
        [sq, sk, sk, sv, sv, sp], rides=rides)


def _swa_bwd(sq, sk, sv, sp, dmixed, lse, delta, name):
    t = sq.shape[0]
    w = SWA_BLOCK
    nb = t // w
    group = SWA_H // SWA_KV
    do_block = FOX_W // SWA_W
    assert FOX_W % SWA_W == 0

    def body(q_ref, kp_ref, kc_ref, vp_ref, vc_ref, sp_ref, do_ref, lse_ref, dl_ref,
             dq_ref, dk_ref, dv_ref, dsp_ref, ck, cv):
        step = pl.program_id(0)
        first = step == nb - 1

        @pl.when(step == 0)
        def _():
            ck[...] = jnp.zeros_like(ck)
            cv[...] = jnp.zeros_like(cv)
            dsp_ref[...] = jnp.zeros_like(dsp_ref)

        dk_cur = [None] * SWA_KV
        dk_prev = [None] * SWA_KV
        dv_cur = [None] * SWA_KV
        dv_prev = [None] * SWA_KV
        dsinks = []

        def add(lst, i, v):
            lst[i] = v if lst[i] is None else lst[i] + v

        for h in range(SWA_H):
            kv = h // group
            q, do = q_ref[:, _hs(h)], do_ref[:, _hs(h)]
            kc, kp, vc, vp = kc_ref[:, _hs(kv)], kp_ref[:, _hs(kv)], vc_ref[:, _hs(kv)], vp_ref[:, _hs(kv)]
            s_cur, s_prev = _swa_logits(q, kc, kp, _slope(h), first)
            lse_h = lse_ref[:, h:h + 1]
            dl_h = dl_ref[:, FOX_H + h:FOX_H + h + 1]
            p_cur = jnp.exp(s_cur - lse_h)
            p_prev = jnp.exp(s_prev - lse_h)
            p_sink = jnp.exp(sp_ref[R_SINK:R_SINK + 1, h:h + 1] - lse_h)
            ds_cur = p_cur * (_dot(do, vc, "nt") - dl_h)
            ds_prev = p_prev * (_dot(do, vp, "nt") - dl_h)
            dq_ref[:, _hs(h)] = (_dot(ds_cur, kc) + _dot(ds_prev, kp)) * SCALE
            add(dk_cur, kv, _dot(ds_cur, q, "tn") * SCALE)
            add(dk_prev, kv, _dot(ds_prev, q, "tn") * SCALE)
            add(dv_cur, kv, _dot(p_cur, do, "tn"))
            add(dv_prev, kv, _dot(p_prev, do, "tn"))
            dsinks.append(-jnp.sum(p_sink * dl_h, axis=0, keepdims=True))
        for kv in range(SWA_KV):
            dk_ref[:, _hs(kv)] = dk_cur[kv] + ck[:, _hs(kv)]
            dv_ref[:, _hs(kv)] = dv_cur[kv] + cv[:, _hs(kv)]
            ck[:, _hs(kv)] = dk_prev[kv]
            cv[:, _hs(kv)] = dv_prev[kv]
        lane = lax.broadcasted_iota(jnp.int32, (1, HEAD), 1)
        row = jnp.zeros((1, HEAD), F32)
        for h in range(SWA_H):
            row = jnp.where(lane == h, dsinks[h], row)
        dsp_ref[R_SINK:R_SINK + 1, :] += row

    def rev(n):
        return nb - 1 - n

    qspec = pl.BlockSpec((w, SWA_W), lambda n: (rev(n), 0))
    cur = pl.BlockSpec((w, SWA_KV_W), lambda n: (rev(n), 0))
    prev = pl.BlockSpec((w, SWA_KV_W), lambda n: (jnp.maximum(rev(n) - 1, 0), 0))
    col = pl.BlockSpec((w, HEAD), lambda n: (rev(n), 0))
    small = pl.BlockSpec((16, 128), lambda n: (0, 0))
    return pl.pallas_call(
        body, grid=(nb,),
        in_specs=[qspec, prev, cur, prev, cur, small, pl.BlockSpec((w, SWA_W), lambda n: (rev(n), do_block)), col, col],
        out_specs=[qspec, cur, cur, small],
        out_shape=[jax.ShapeDtypeStruct((t, SWA_W), F32), jax.ShapeDtypeStruct((t, SWA_KV_W), F32),
                   jax.ShapeDtypeStruct((t, SWA_KV_W), F32), jax.ShapeDtypeStruct((16, 128), F32)],
        scratch_shapes=[pltpu.VMEM((w, SWA_KV_W), F32), pltpu.VMEM((w, SWA_KV_W), F32)],
        compiler_params=_params(1), name=name)(sq, sk, sk, sv, sv, sp, dmixed, lse, delta)


def _mem_pre(mkv, sp, name):
    m = mkv.shape[0]

    def body(x_ref, sp_ref, k_ref, v_ref):
        for h in range(MEM_H):
            k_ref[:, _hs(h)] = _head_norm(x_ref[:, _hs(h)], sp_ref[R_MK:R_MK + 1, :]).astype(BF16)
        v_ref[...] = x_ref[:, MEM_W:2 * MEM_W].astype(BF16)

    out = jax.ShapeDtypeStruct((m, MEM_W), BF16)
    return pl.pallas_call(body, out_shape=[out, out], name=name)(mkv, sp)


def _mem_post_bwd(mkv, sp, dmk, dmv, name):
    m = mkv.shape[0]

    def body(x_ref, sp_ref, dk_ref, dv_ref, d_ref, dsp_ref):
        dsp_ref[...] = jnp.zeros_like(dsp_ref)
        total = None
        for h in range(MEM_H):
            dx, dg = _head_norm_bwd(x_ref[:, _hs(h)], sp_ref[R_MK:R_MK + 1, :], dk_ref[:, _hs(h)])
            d_ref[:, _hs(h)] = dx.astype(BF16)
            total = dg if total is None else total + dg
        d_ref[:, MEM_W:2 * MEM_W] = dv_ref[...].astype(BF16)
        dsp_ref[R_MK:R_MK + 1, :] = total

    return pl.pallas_call(body, out_shape=[jax.ShapeDtypeStruct((m, 2 * MEM_W), BF16),
                                           jax.ShapeDtypeStruct((16, 128), F32)], name=name)(mkv, sp, dmk, dmv)


def _mem_fwd(mq, mk, mv, name):
    t = mq.shape[0]
    m = mk.shape[0]
    tq = min(t, 512)

    def body(q_ref, k_ref, v_ref, o_ref, lse_ref):
        lses = []
        for h in range(MEM_H):
            s = _dot(q_ref[:, _hs(h)], k_ref[:, _hs(h)], "nt") * SCALE
            mx = jnp.max(s, axis=-1, keepdims=True)
            p = jnp.exp(s - mx)
            l = jnp.sum(p, axis=-1, keepdims=True)
            o_ref[:, _hs(h)] = _dot(p, v_ref[:, _hs(h)]) / l
            lses.append(mx + jnp.log(l))
        lse_ref[...] = _head_column(lses)

    qspec = pl.BlockSpec((tq, MEM_W), lambda i: (i, 0))
    kspec = pl.BlockSpec((m, MEM_W), lambda i: (0, 0))
    return pl.pallas_call(
        body, grid=(t // tq,), in_specs=[qspec, kspec, kspec],
        out_specs=[qspec, pl.BlockSpec((tq, HEAD), lambda i: (i, 0))],
        out_shape=[jax.ShapeDtypeStruct((t, MEM_W), F32), jax.ShapeDtypeStruct((t, HEAD), F32)],
        compiler_params=_params(1), name=name)(mq, mk, mv)


def _mem_bwd(mq, mk, mv, dmixed, lse, delta, name):
    t = mq.shape[0]
    m = mk.shape[0]
    tq = min(t, 512)
    do_block = (FOX_W + SWA_W) // MEM_W
    assert (FOX_W + SWA_W) % MEM_W == 0

    def body(q_ref, k_ref, v_ref, do_ref, lse_ref, dl_ref, dq_ref, dk_ref, dv_ref):
        @pl.when(pl.program_id(0) == 0)
        def _():
            dk_ref[...] = jnp.zeros_like(dk_ref)
            dv_ref[...] = jnp.zeros_like(dv_ref)

        for h in range(MEM_H):
            q, k, v, do = q_ref[:, _hs(h)], k_ref[:, _hs(h)], v_ref[:, _hs(h)], do_ref[:, _hs(h)]
            s = _dot(q, k, "nt") * SCALE
            p = jnp.exp(s - lse_ref[:, h:h + 1])
            col = FOX_H + SWA_H + h
            ds = p * (_dot(do, v, "nt") - dl_ref[:, col:col + 1])
            dq_ref[:, _hs(h)] = _dot(ds, k) * SCALE
            dk_ref[:, _hs(h)] += _dot(ds, q, "tn") * SCALE
            dv_ref[:, _hs(h)] += _dot(p, do, "tn")

    qspec = pl.BlockSpec((tq, MEM_W), lambda i: (i, 0))
    kspec = pl.BlockSpec((m, MEM_W), lambda i: (0, 0))
    col = pl.BlockSpec((tq, HEAD), lambda i: (i, 0))
    return pl.pallas_call(
        body, grid=(t // tq,),
        in_specs=[qspec, kspec, kspec, pl.BlockSpec((tq, MEM_W), lambda i: (i, do_block)), col, col],
        out_specs=[qspec, kspec, kspec],
        out_shape=[jax.ShapeDtypeStruct((t, MEM_W), F32), jax.ShapeDtypeStruct((m, MEM_W), F32),
                   jax.ShapeDtypeStruct((m, MEM_W), F32)],
        compiler_params=_params(1), name=name)(mq, mk, mv, dmixed, lse, delta)


def _all_gather(xs, name):
    n = len(xs)

    def body(*refs):
        x_refs, o_refs = refs[:n], refs[n:2 * n]
        send_sems, recv_sems, local_sems = refs[2 * n:]
        x, y, c = _me()
        me, sibling = (x, y, c), (x, y, 1 - c)
        x_nb, y_nb, diag = (1 - x, y, c), (x, 1 - y, c), (1 - x, 1 - y, c)
        relay_from = (x + (1 - c) * (1 - 2 * x), y + c * (1 - 2 * y), c)
        relay_to = (x + c * (1 - 2 * x), y + (1 - c) * (1 - 2 * y), c)

        def copy(a, k, block, to, src=None):
            slot = o_refs[a].at[_lin(block)]
            return pltpu.make_async_remote_copy(
                src_ref=slot if src is None else src, dst_ref=slot, send_sem=send_sems.at[a, k],
                recv_sem=recv_sems.at[a, k], device_id=to, device_id_type=MESH)

        mine = [pltpu.make_async_copy(x_refs[a], o_refs[a].at[_lin(me)], local_sems.at[a]) for a in range(n)]
        for cp in mine:
            cp.start()
        sent = []
        for a in range(n):
            sent += [copy(a, 0, me, sibling, src=x_refs[a]), copy(a, 1, me, x_nb, src=x_refs[a]),
                     copy(a, 2, me, y_nb, src=x_refs[a])]
        for cp in sent:
            cp.start()

        def pass_on(cp):
            cp.start()
            sent.append(cp)

        for a in range(n):
            copy(a, 1, x_nb, me).wait_recv()
            copy(a, 2, y_nb, me).wait_recv()
            pass_on(copy(a, 3, relay_from, relay_to))
            pass_on(copy(a, 4, x_nb, sibling))
            pass_on(copy(a, 5, y_nb, sibling))
        for a in range(n):
            copy(a, 3, diag, me).wait_recv()
            pass_on(copy(a, 6, diag, sibling))
        for a in range(n):
            copy(a, 0, sibling, me).wait_recv()
            for k, chip in ((4, (1 - x, y)), (5, (x, 1 - y)), (6, (1 - x, 1 - y))):
                copy(a, k, (*chip, 1 - c), me).wait_recv()
        for cp in sent:
            cp.wait_send()
        for cp in mine:
            cp.wait()

    return pl.pallas_call(
        body, in_specs=[ANY] * n, out_specs=[ANY] * n,
        out_shape=[jax.ShapeDtypeStruct((N_DEV,) + x.shape, x.dtype) for x in xs],
        scratch_shapes=[pltpu.SemaphoreType.DMA((n, 7)), pltpu.SemaphoreType.DMA((n, 7)),
                        pltpu.SemaphoreType.DMA((n,))],
        name=name)(*xs)


def _peers():
    x, y, c = _me()
    out = []
    for k in range(1, N_DEV):
        kx, ky, kc = (k >> 2) & 1, (k >> 1) & 1, k & 1
        out.append(((1 - x) if kx else x, (1 - y) if ky else y, (1 - c) if kc else c))
    return out


def _all_reduce_small(xs, name):
    n = len(xs)

    def body(*refs):
        x_refs, o_refs = refs[:n], refs[n:2 * n]
        bufs = refs[2 * n:3 * n]
        send_sems, recv_sems = refs[3 * n:]
        me = _lin(_me())
        peers = _peers()
        for a in range(n):
            bufs[a][me] = x_refs[a][...]
        sends = []
        for a in range(n):
            for k, peer in enumerate(peers):
                sends.append(pltpu.make_async_remote_copy(
                    src_ref=bufs[a].at[me], dst_ref=bufs[a].at[me], send_sem=send_sems.at[a, k],
                    recv_sem=recv_sems.at[a, k], device_id=peer, device_id_type=MESH))
        for cp in sends:
            cp.start()
        for a in range(n):
            for k, peer in enumerate(peers):
                pltpu.make_async_remote_copy(
                    src_ref=bufs[a].at[me], dst_ref=bufs[a].at[_lin(peer)], send_sem=send_sems.at[a, k],
                    recv_sem=recv_sems.at[a, k], device_id=peer, device_id_type=MESH).wait_recv()
        for cp in sends:
            cp.wait_send()
        for a in range(n):
            total = bufs[a][0]
            for q in range(1, N_DEV):
                total = total + bufs[a][q]
            o_refs[a][...] = total

    vmem = pl.BlockSpec(memory_space=pltpu.VMEM)
    return pl.pallas_call(
        body, in_specs=[vmem] * n, out_specs=[vmem] * n,
        out_shape=[jax.ShapeDtypeStruct(x.shape, F32) for x in xs],
        scratch_shapes=[pltpu.VMEM((N_DEV,) + x.shape, F32) for x in xs]
        + [pltpu.SemaphoreType.DMA((n, 7)), pltpu.SemaphoreType.DMA((n, 7))],
        name=name)(*xs)


def _pair_add(part, got, name):
    _, rows, cols = part.shape
    tm = _rows_tile(rows, cols * 2, budget=2 << 20)
    core = jnp.reshape(lax.axis_index("c"), (1,)).astype(jnp.int32)

    def body(c_ref, p_ref, g_ref, o_ref):
        o_ref[...] = (p_ref[...].astype(F32) + g_ref[...].astype(F32)).astype(BF16)

    spec = pl.BlockSpec((None, tm, cols), lambda q, i, c: (q, i, 0))
    grid_spec = pltpu.PrefetchScalarGridSpec(
        num_scalar_prefetch=1, grid=(4, rows // tm),
        in_specs=[pl.BlockSpec((None, tm, cols), lambda q, i, c: (2 * q + c[0], i, 0)), spec], out_specs=spec)
    return pl.pallas_call(body, grid_spec=grid_spec, out_shape=jax.ShapeDtypeStruct((4, rows, cols), BF16),
                          compiler_params=_params(2), name=name)(core, part, got)


def _adam_math(w, g, m, v):
    nm = ADAM_B1 * m + (1.0 - ADAM_B1) * g
    nv = ADAM_B2 * v + (1.0 - ADAM_B2) * (g * g)
    m_hat = nm / (1.0 - ADAM_B1 ** ADAM_STEP)
    v_hat = nv / (1.0 - ADAM_B2 ** ADAM_STEP)
    return -ADAM_LR * (m_hat / (jnp.sqrt(v_hat) + ADAM_EPS) + ADAM_WD * w), nm, nv


def _sum_chips(got, name):
    _, rows, cols = got.shape
    tm = _rows_tile(rows, cols * 2 * 4, budget=2 << 20)

    def body(r_ref, o_ref):
        o_ref[...] = ((r_ref[0].astype(F32) + r_ref[1].astype(F32)) + r_ref[2].astype(F32)) + r_ref[3].astype(F32)

    return pl.pallas_call(
        body, grid=(rows // tm,), in_specs=[pl.BlockSpec((4, tm, cols), lambda i: (0, i, 0))],
        out_specs=pl.BlockSpec((tm, cols), lambda i: (i, 0)), out_shape=jax.ShapeDtypeStruct((rows, cols), F32),
        compiler_params=_params(1), name=name)(got)


def _sum_adamw(got, col_block, w, m, v, name):
    _, rows, cols = w.shape
    tm = _rows_tile(rows, cols * 4, budget=2 << 20)

    def body(r_ref, w_ref, m_ref, v_ref, g_ref, d_ref, nm_ref, nv_ref):
        g = ((r_ref[0].astype(F32) + r_ref[1].astype(F32)) + r_ref[2].astype(F32)) + r_ref[3].astype(F32)
        g_ref[...] = g
        d_ref[...], nm_ref[...], nv_ref[...] = _adam_math(w_ref[...], g, m_ref[...], v_ref[...])

    spec = pl.BlockSpec((None, tm, cols), lambda i: (0, i, 0))
    out = jax.ShapeDtypeStruct(w.shape, F32)
    return pl.pallas_call(
        body, grid=(rows // tm,), in_specs=[pl.BlockSpec((4, tm, cols), lambda i: (0, i, col_block)), spec, spec, spec],
        out_specs=[spec] * 4, out_shape=[out] * 4, compiler_params=_params(1), name=name)(got, w, m, v)


def _adamw(w, g, m, v, name):
    rows, cols = w.shape

    def body(w_ref, g_ref, m_ref, v_ref, d_ref, nm_ref, nv_ref):
        d_ref[...], nm_ref[...], nv_ref[...] = _adam_math(w_ref[...], g_ref[...], m_ref[...], v_ref[...])

    tm = _rows_tile(rows, cols * 4, budget=2 << 20, mult=8)
    spec = pl.BlockSpec((tm, cols), lambda i: (i, 0))
    out = jax.ShapeDtypeStruct(w.shape, F32)
    return pl.pallas_call(body, grid=(rows // tm,), in_specs=[spec] * 4, out_specs=[spec] * 3,
                          out_shape=[out] * 3, compiler_params=_params(1), name=name)(w, g, m, v)


def _permute_in(w):
    logit0 = 3 * FOX_W
    pad = jnp.zeros(w.shape[:-1] + (HEAD - N_LOGIT,), w.dtype)
    return jnp.concatenate([w[..., :logit0], w[..., logit0 + N_LOGIT:], w[..., logit0:logit0 + N_LOGIT], pad], axis=-1)


def _unpermute_in(w):
    logit0 = 3 * FOX_W
    return jnp.concatenate([w[..., :logit0], w[..., C_FL:C_FL + N_LOGIT], w[..., logit0:C_FL]], axis=-1)


def _pad_row(v, width):
    return jnp.pad(v, ((0, 0), (0, width - v.shape[1])))


def _pack_small(fq, fk, sq, sk, mq, mk, fb, sinks):
    rows = [fq, fk, sq, sk, mq, mk, _pad_row(fb, HEAD), _pad_row(sinks, HEAD)]
    return jnp.concatenate(rows + [jnp.zeros((8, HEAD), F32)], axis=0)


def _pack_norms(a, b, c, d):
    return jnp.concatenate([a, b, c, d, jnp.zeros((4, a.shape[1]), F32)], axis=0)


def kernel(x, mem, ffn1_norm, ffn1_gate, ffn1_up, ffn1_down, mix_norm, mem_norm, w_in, forget_bias, w_mem_k, w_mem_v, fox_q_gain, fox_k_gain, swa_q_gain, swa_k_gain, swa_sinks, mem_q_gain, mem_k_gain, w_out, ffn2_norm, ffn2_gate, ffn2_up, ffn2_down, loss_target, m_ffn1_norm, m_ffn1_gate, m_ffn1_up, m_ffn1_down, m_mix_norm, m_mem_norm, m_w_in, m_forget_bias, m_w_mem_k, m_w_mem_v, m_fox_q_gain, m_fox_k_gain, m_swa_q_gain, m_swa_k_gain, m_swa_sinks, m_mem_q_gain, m_mem_k_gain, m_w_out, m_ffn2_norm, m_ffn2_gate, m_ffn2_up, m_ffn2_down, v_ffn1_norm, v_ffn1_gate, v_ffn1_up, v_ffn1_down, v_mix_norm, v_mem_norm, v_w_in, v_forget_bias, v_w_mem_k, v_w_mem_v, v_fox_q_gain, v_fox_k_gain, v_swa_q_gain, v_swa_k_gain, v_swa_sinks, v_mem_q_gain, v_mem_k_gain, v_w_out, v_ffn2_norm, v_ffn2_gate, v_ffn2_up, v_ffn2_down):
    x0 = x[0]
    mem0 = mem[0]
    target = loss_target[0]
    t, d = x0.shape
    d_shard = w_in.shape[1]
    m_len = mem0.shape[0]
    tm = min(t, 512)
    tk = min(t, 2048)
    tn = IN_W // 3
    tkw, tnw = min(t, 1024), IN_W // 3

    def swap(a):
        return jnp.swapaxes(a, 1, 2)

    gate1, up1, gate2, up2 = swap(ffn1_gate), swap(ffn1_up), swap(ffn2_gate), swap(ffn2_up)

    local = {
        "g1": gate1[0], "u1": up1[0], "d1": ffn1_down[0],
        "g2": gate2[0], "u2": up2[0], "d2": ffn2_down[0],
        "in": _permute_in(w_in[0]), "out": w_out[0],
        "mkv": jnp.concatenate([w_mem_k[0], w_mem_v[0]], axis=1),
    }
    shard = {k: _cast_bf16(v, f"cast_{k}") for k, v in local.items()}
    sp = _pack_small(fox_q_gain, fox_k_gain, swa_q_gain, swa_k_gain, mem_q_gain, mem_k_gain, forget_bias, swa_sinks)
    wt = {}

    wt["g1"], wt["u1"] = _all_gather([shard["g1"], shard["u1"]], "gather_ffn1_in")
    xn1 = _rms_fwd(x0, ffn1_norm, "ffn1_norm")
    half_fs = shard["g2"].shape[0] // 2
    (a1, b1, h1), ((wt["d1"], wt["in"]),) = _ffn_up(
        xn1, wt["g1"], wt["u1"], "ffn1", rides=[_ride_gather([shard["d1"], shard["in"]], 0.87)])
    tc = min(t, CONTRACT_ROWS)
    (x1, hn), ((wt["out"], wt["mkv"]),) = _ffn_down(
        x0, h1, wt["d1"], "ffn1", rides=[_ride_gather([shard["out"], shard["mkv"]], 0.6)],
        tail=_tail_norm(mix_norm, t, d, tc))
    w_in_full = wt["in"].reshape(d, IN_W)

    proj, (half,) = _mm(
        "proj", [(hn, pl.BlockSpec((tm, d), lambda n, i, k: (i, 0)),
                  w_in_full, pl.BlockSpec((d, tn), lambda n, i, k: (0, n)))],
        "nn", (3, t // tm, 1), jax.ShapeDtypeStruct((t, IN_W), F32), pl.BlockSpec((tm, tn), lambda n, i, k: (i, n)),
        rides=[_ride_gather_chips([shard["g2"]], rows=(0, half_fs))])
    w_out_full = wt["out"].reshape(d, d)
    w_mkv_full = wt["mkv"].reshape(d, 2 * MEM_W)
    fq, fk, fv, sq, sk, sv, mq, c_col = _attn_pre(proj, sp, "attn_pre")
    c_row = jnp.transpose(c_col[:, :8])
    c_rep = jnp.broadcast_to(c_row[:FOX_H, :, None], (FOX_H, t, HEAD))

    mn = _rms_fwd(mem0, mem_norm, "mem_norm")
    mkv = _mm("mem_kv", [(mn, pl.BlockSpec((m_len, d), lambda k: (0, 0)),
                          w_mkv_full, pl.BlockSpec((d, 2 * MEM_W), lambda k: (0, 0)))],
              "nn", (1,), jax.ShapeDtypeStruct((m_len, 2 * MEM_W), F32),
              pl.BlockSpec((m_len, 2 * MEM_W), lambda k: (0, 0)))
    mk, mv = _mem_pre(mkv, sp, "mem_pre")

    (o_a, lse_a), (half, half_u2) = _fox_fwd(
        fq, fk, fv, c_rep, c_row, "fox_fwd",
        rides=[_ride_gather_chips([shard["g2"]], rows=(half_fs, half_fs), into=half),
               _ride_gather_chips([shard["u2"]], rows=(0, half_fs))])
    (o_b, lse_b), ((wt["g2"],), half_u2) = _swa_fwd(
        sq, sk, sv, sp, "swa_fwd",
        rides=[_ride_gather_sibling(half), _ride_gather_chips([shard["u2"]], rows=(half_fs, half_fs), into=half_u2)])
    o_c, lse_c = _mem_fwd(mq, mk, mv, "mem_fwd")

    def rows_spec(width):
        return pl.BlockSpec((tm, width), lambda i, k: (i, 0))

    def wout_rows(first, width):
        assert first % width == 0
        return pl.BlockSpec((width, d), lambda i, k: (first // width, 0), pipeline_mode=pl.Buffered(1))

    xspec = pl.BlockSpec((tm, d), lambda i, k: (i, 0))
    (x2, xn2), ((wt["u2"],),) = _mm(
        "mix_out",
        [(o_a, rows_spec(FOX_W), w_out_full, wout_rows(0, FOX_W)),
         (o_b, rows_spec(SWA_W), w_out_full, wout_rows(FOX_W, SWA_W)),
         (o_c, rows_spec(MEM_W), w_out_full, wout_rows(FOX_W + SWA_W, MEM_W))],
        "nn", (t // tm, 1), jax.ShapeDtypeStruct((t, d), F32), xspec, res=x1, res_spec=xspec,
        rides=[_ride_gather_sibling(half_u2)], tail=_tail_norm(ffn2_norm, t, d, tm))

    (a2, b2, h2), ((wt["d2"],),) = _ffn_up(xn2, wt["g2"], wt["u2"], "ffn2", rides=[_ride_gather([shard["d2"]], 0.75)])
    dy, dyb, sq_err = _ffn_down(x2, h2, wt["d2"], "ffn2", tail=_tail_loss(target, t, d, tc))
    loss = lax.psum(0.5 * sq_err[0, 0] / d, ("x", "y", "c"))

    got = {}
    paired = {}
    landed = {}

    def pair(k, part):
        paired[k] = _pair_add(part, got[k], f"pair_{k}")

    (dg2, du2), _ = _ffn_dact(dyb, wt["d2"], a2, b2, "ffn2")
    part_d2 = _ffn_dw(h2, dyb, 0.5, "ffn2_dwd")
    part_g2, ((got["d2"],),) = _ffn_dw(dg2, xn2, 1.0, "ffn2_dwg", rides=[_ride_scatter_sibling([part_d2])])
    pair("d2", part_d2)
    half_rows = part_d2.shape[1] // 2
    first, second = (0, half_rows), (half_rows, half_rows)
    part_u2, (half, (got["g2"],)) = _ffn_dw(
        du2, xn2, 1.0, "ffn2_dwu",
        rides=[_ride_scatter_chips([paired["d2"]], first), _ride_scatter_sibling([part_g2])])
    pair("g2", part_g2)
    dxn2, ((landed["d2"],),) = _ffn_contract(
        dg2, wt["g2"], "ffn2_dxn_g", rides=[_ride_scatter_chips([paired["d2"]], second, into=half)])
    (dx2, dx2b, dgain_ffn2), (half_g2, (got["u2"],)) = _ffn_contract(
        du2, wt["u2"], "ffn2_dxn_u", res=dxn2,
        rides=[_ride_scatter_chips([paired["g2"]], first), _ride_scatter_sibling([part_u2])],
        tail=_tail_norm_bwd(x2, ffn2_norm, dy, t, d, tc))
    pair("u2", part_u2)

    dmixed = _mm("mix_out_dx", [(dx2b, xspec, w_out_full, pl.BlockSpec((d, d), lambda i, k: (0, 0)))],
                 "nt", (t // tm, 1), jax.ShapeDtypeStruct((t, d), F32), xspec)

    def k_rows(width):
        return pl.BlockSpec((tk, width), lambda j, k: (k, 0))

    part_out = [
        _mm(f"mix_out_dw{i}", [(o, k_rows(width), dx2b, k_rows(d))], "tn", (1, t // tk),
            jax.ShapeDtypeStruct((width, d), BF16), pl.BlockSpec((width, d), lambda j, k: (0, 0)))
        for i, (o, width) in enumerate(((o_a, FOX_W), (o_b, SWA_W), (o_c, MEM_W)))
    ]
    part_out = jnp.concatenate(part_out, axis=0).reshape(N_DEV, d_shard, d)

    delta, delta_rep = _delta(dmixed, o_a, o_b, o_c, "attn_delta")
    (dfq, dfk, dfv, dc_col, dc_row), ((landed["g2"],), (landed["u2"],)) = _fox_bwd(
        fq, fk, fv, c_rep, c_row, dmixed, lse_a, delta_rep, "fox_bwd",
        rides=[_ride_scatter_chips([paired["g2"]], second, into=half_g2), _ride_scatter_chips([paired["u2"]])])
    dsq, dsk, dsv, dsp_sink = _swa_bwd(sq, sk, sv, sp, dmixed, lse_b, delta, "swa_bwd")
    dmq, dmk, dmv = _mem_bwd(mq, mk, mv, dmixed, lse_c, delta, "mem_bwd")

    dmkv, dsp_mem = _mem_post_bwd(mkv, sp, dmk, dmv, "mem_post_bwd")
    part_mkv = _mm("mem_kv_dw", [(mn, pl.BlockSpec((m_len, d), lambda k: (0, 0)),
                                  dmkv, pl.BlockSpec((m_len, 2 * MEM_W), lambda k: (0, 0)))],
                   "tn", (1,), jax.ShapeDtypeStruct((d, 2 * MEM_W), BF16),
                   pl.BlockSpec((d, 2 * MEM_W), lambda k: (0, 0))).reshape(N_DEV, d_shard, 2 * MEM_W)
    dmn = _mm("mem_kv_dx", [(dmkv, pl.BlockSpec((m_len, 2 * MEM_W), lambda k: (0, 0)),
                             w_mkv_full, pl.BlockSpec((d, 2 * MEM_W), lambda k: (0, 0)))],
              "nt", (1,), jax.ShapeDtypeStruct((m_len, d), F32), pl.BlockSpec((m_len, d), lambda k: (0, 0)))
    _, _, dgain_mem = _rms_bwd(mem0, mem_norm, dmn, None, "mem_norm_bwd")

    dc_row_t = _pad_row(jnp.transpose(dc_row), HEAD)
    dproj, dsp_attn = _attn_post_bwd(proj, sp, dfq, dfk, dfv, dsq, dsk, dsv, dmq, dc_col, dc_row_t, "attn_post_bwd")
    (dx1, dx1b, dgain_mix), ((got["out"], got["mkv"]),) = _mm(
        "proj_dx", [(dproj, pl.BlockSpec((tc, IN_W), lambda i, k: (i, 0)),
                     w_in_full, pl.BlockSpec((d, IN_W), lambda i, k: (0, 0), pipeline_mode=pl.Buffered(1)))],
        "nt", (t // tc, 1), jax.ShapeDtypeStruct((t, d), F32), pl.BlockSpec((tc, d), lambda i, k: (i, 0)),
        rides=[_ride_scatter_sibling([part_out, part_mkv])], tail=_tail_norm_bwd(x1, mix_norm, dx2, t, d, tc))
    pair("out", part_out)
    pair("mkv", part_mkv)
    part_in, ((landed["out"], landed["mkv"]),) = _mm(
        "proj_dw", [(hn, pl.BlockSpec((tkw, d), lambda n, k: (k, 0)),
                     dproj, pl.BlockSpec((tkw, tnw), lambda n, k: (k, n)))],
        "tn", (IN_W // tnw, t // tkw), jax.ShapeDtypeStruct((d, IN_W), BF16), pl.BlockSpec((d, tnw), lambda n, k: (0, n)),
        rides=[_ride_scatter_chips([paired["out"], paired["mkv"]])])
    part_in = part_in.reshape(N_DEV, d_shard, IN_W)

    part_d1, ((got["in"],),) = _ffn_dw(h1, dx1b, 0.5, "ffn1_dwd", rides=[_ride_scatter_sibling([part_in])])
    pair("in", part_in)
    (dg1, du1), ((landed["in"],), (got["d1"],)) = _ffn_dact(
        dx1b, wt["d1"], a1, b1, "ffn1",
        rides=[_ride_scatter_chips([paired["in"]]), _ride_scatter_sibling([part_d1])])
    pair("d1", part_d1)
    part_g1, (half_d1,) = _ffn_dw(dg1, xn1, 1.0, "ffn1_dwg", rides=[_ride_scatter_chips([paired["d1"]], first)])
    part_u1, ((landed["d1"],), (got["g1"],)) = _ffn_dw(
        du1, xn1, 1.0, "ffn1_dwu",
        rides=[_ride_scatter_chips([paired["d1"]], second, into=half_d1), _ride_scatter_sibling([part_g1])])
    pair("g1", part_g1)
    dxn1, ((landed["g1"],), (got["u1"],)) = _ffn_contract(
        dg1, wt["g1"], "ffn1_dxn_g", rides=[_ride_scatter_chips([paired["g1"]]), _ride_scatter_sibling([part_u1])])
    pair("u1", part_u1)
    (grad_x, _, dgain_ffn1), ((landed["u1"],),) = _ffn_contract(
        du1, wt["u1"], "ffn1_dxn_u", res=dxn1, rides=[_ride_scatter_chips([paired["u1"]])],
        tail=_tail_norm_bwd(x0, ffn1_norm, dx1, t, d, tc))

    norms_sum, small_sum = _all_reduce_small(
        [_pack_norms(dgain_ffn1, dgain_mix, dgain_mem, dgain_ffn2), dsp_attn + dsp_sink + dsp_mem], "reduce_small")

    result = {
        "ffn1_gate": map(swap, _sum_adamw(landed["g1"], 0, gate1, swap(m_ffn1_gate), swap(v_ffn1_gate), "adamw_ffn1_gate")),
        "ffn1_up": map(swap, _sum_adamw(landed["u1"], 0, up1, swap(m_ffn1_up), swap(v_ffn1_up), "adamw_ffn1_up")),
        "ffn1_down": _sum_adamw(landed["d1"], 0, ffn1_down, m_ffn1_down, v_ffn1_down, "adamw_ffn1_down"),
        "w_mem_k": _sum_adamw(landed["mkv"], 0, w_mem_k, m_w_mem_k, v_w_mem_k, "adamw_w_mem_k"),
        "w_mem_v": _sum_adamw(landed["mkv"], 1, w_mem_v, m_w_mem_v, v_w_mem_v, "adamw_w_mem_v"),
        "w_out": _sum_adamw(landed["out"], 0, w_out, m_w_out, v_w_out, "adamw_w_out"),
        "ffn2_gate": map(swap, _sum_adamw(landed["g2"], 0, gate2, swap(m_ffn2_gate), swap(v_ffn2_gate), "adamw_ffn2_gate")),
        "ffn2_up": map(swap, _sum_adamw(landed["u2"], 0, up2, swap(m_ffn2_up), swap(v_ffn2_up), "adamw_ffn2_up")),
        "ffn2_down": _sum_adamw(landed["d2"], 0, ffn2_down, m_ffn2_down, v_ffn2_down, "adamw_ffn2_down"),
    }
    grad_in = _unpermute_in(_sum_chips(landed["in"], "sum_w_in"))
    result["w_in"] = (grad_in[None],) + tuple(
        o[None] for o in _adamw(w_in[0], grad_in, m_w_in[0], v_w_in[0], "adamw_w_in"))

    norm_names = ["ffn1_norm", "mix_norm", "mem_norm", "ffn2_norm"]
    norm_w = _pack_norms(ffn1_norm, mix_norm, mem_norm, ffn2_norm)
    norm_m = _pack_norms(m_ffn1_norm, m_mix_norm, m_mem_norm, m_ffn2_norm)
    norm_v = _pack_norms(v_ffn1_norm, v_mix_norm, v_mem_norm, v_ffn2_norm)
    outs = (norms_sum,) + tuple(_adamw(norm_w, norms_sum, norm_m, norm_v, "adamw_norms"))
    for i, k in enumerate(norm_names):
        result[k] = tuple(o[i:i + 1] for o in outs)

    small_names = ["fox_q_gain", "fox_k_gain", "swa_q_gain", "swa_k_gain", "mem_q_gain", "mem_k_gain",
                   "forget_bias", "swa_sinks"]
    small_m = _pack_small(m_fox_q_gain, m_fox_k_gain, m_swa_q_gain, m_swa_k_gain, m_mem_q_gain, m_mem_k_gain,
                          m_forget_bias, m_swa_sinks)
    small_v = _pack_small(v_fox_q_gain, v_fox_k_gain, v_swa_q_gain, v_swa_k_gain, v_mem_q_gain, v_mem_k_gain,
                          v_forget_bias, v_swa_sinks)
    outs = (small_sum,) + tuple(_adamw(sp, small_sum, small_m, small_v, "adamw_small"))
    for i, k in enumerate(small_names):
        width = N_LOGIT if k in ("forget_bias", "swa_sinks") else HEAD
        result[k] = tuple(o[i:i + 1, :width] for o in outs)

    order = ["ffn1_norm", "ffn1_gate", "ffn1_up", "ffn1_down", "mix_norm", "mem_norm", "w_in", "forget_bias",
             "w_mem_k", "w_mem_v", "fox_q_gain", "fox_k_gain", "swa_q_gain", "swa_k_gain", "swa_sinks",
             "mem_q_gain", "mem_k_gain", "w_out", "ffn2_norm", "ffn2_gate", "ffn2_up", "ffn2_down"]
    result = {k: tuple(v) for k, v in result.items()}
    flat = [loss, grad_x[None]]
    for kind in range(4):
        flat += [result[k][kind] for k in order]
    return tuple(flat)
```

```python
import functools

import jax
import jax.numpy as jnp
from jax import lax
from jax.experimental import pallas as pl
from jax.experimental.pallas import tpu as pltpu

F32 = jnp.float32
BF16 = jnp.bfloat16
MESH = pl.DeviceIdType.MESH
ANY = pl.BlockSpec(memory_space=pl.ANY)

N_DEV = 8
EPS = 1e-6
NEG_INF = -1e30
HEAD = 128
FOX_H, SWA_H, SWA_KV, MEM_H = 6, 6, 2, 4
FOX_W, SWA_W, SWA_KV_W, MEM_W = FOX_H * HEAD, SWA_H * HEAD, SWA_KV * HEAD, MEM_H * HEAD
SCALE = HEAD ** -0.5
SWA_BLOCK = 128
C_FQ, C_FK, C_FV = 0, FOX_W, 2 * FOX_W
C_SQ = 3 * FOX_W
C_SK = C_SQ + SWA_W
C_SV = C_SK + SWA_KV_W
C_MQ = C_SV + SWA_KV_W
C_FL = C_MQ + MEM_W
IN_W = C_FL + HEAD
N_LOGIT = FOX_H
R_FQ, R_FK, R_SQ, R_SK, R_MQ, R_MK, R_FB, R_SINK = range(8)
ADAM_LR, ADAM_B1, ADAM_B2, ADAM_EPS, ADAM_WD, ADAM_STEP = 0.001, 0.9, 0.999, 1e-08, 0.01, 10
VMEM_BYTES = 56 * 1024 * 1024

DN = {
    "nn": (((1,), (0,)), ((), ())),
    "nt": (((1,), (1,)), ((), ())),
    "tn": (((0,), (0,)), ((), ())),
}


def _params(n_axes):
    return pltpu.CompilerParams(dimension_semantics=("arbitrary",) * n_axes, vmem_limit_bytes=VMEM_BYTES)


def _dot(a, b, dims="nn"):
    return lax.dot_general(a.astype(BF16), b.astype(BF16), DN[dims], preferred_element_type=F32)


def _sigmoid(x):
    return 0.5 * jnp.tanh(0.5 * x) + 0.5


def _me():
    return lax.axis_index("x"), lax.axis_index("y"), lax.axis_index("c")


def _lin(p):
    return 4 * p[0] + 2 * p[1] + p[2]


def _rows_tile(rows, row_bytes, budget=4 << 20, mult=16):
    best = None
    for k in range(1, rows + 1):
        if rows % k == 0 and (rows // k) % mult == 0 and (rows // k) * row_bytes <= budget:
            best = rows // k
            break
    assert best is not None, (rows, row_bytes)
    return best


class _Ride:
    def __init__(self, inputs, out_shapes, aliases, n_remote, n_local, start, wait):
        self.inputs, self.out_shapes, self.aliases = list(inputs), list(out_shapes), dict(aliases)
        self.n_remote, self.n_local, self.start, self.wait = n_remote, n_local, start, wait


def _remote(src, dst, send, recv, k, to):
    return pltpu.make_async_remote_copy(src_ref=src, dst_ref=dst, send_sem=send.at[k], recv_sem=recv.at[k],
                                        device_id=to, device_id_type=MESH)


def _other_chips(x, y):
    return [(1 - x, y), (x, 1 - y), (1 - x, 1 - y)]


ALL_CHIPS = [(0, 0), (0, 1), (1, 0), (1, 1)]


def _rows_of(ref, rows, slot=None):
    if slot is None:
        return ref if rows is None else ref.at[pl.ds(rows[0], rows[1])]
    return ref.at[slot] if rows is None else ref.at[slot, pl.ds(rows[0], rows[1])]


def _ride_gather_chips(xs, rows=None, into=None):
    n = len(xs)

    def copies(ins, outs, send, recv):
        x, y, c = _me()
        out = []
        for a in range(n):
            for j, chip in enumerate(_other_chips(x, y)):
                peer = (*chip, c)
                src = _rows_of(ins[a], rows)
                out.append((_remote(src, _rows_of(outs[a], rows, _lin((x, y, c))), send, recv, 3 * a + j, peer),
                            _remote(src, _rows_of(outs[a], rows, _lin(peer)), send, recv, 3 * a + j, peer)))
        return out

    def mine(ins, outs, local):
        me = _lin(_me())
        return [pltpu.make_async_copy(_rows_of(ins[a], rows), _rows_of(outs[a], rows, me), local.at[a])
                for a in range(n)]

    def start(ins, outs, send, recv, local):
        for cp in mine(ins, outs, local):
            cp.start()
        for sent, _ in copies(ins, outs, send, recv):
            sent.start()

    def wait(ins, outs, send, recv, local):
        for sent, landed in copies(ins, outs, send, recv):
            landed.wait_recv()
            sent.wait_send()
        for cp in mine(ins, outs, local):
            cp.wait()

    shapes = [jax.ShapeDtypeStruct((N_DEV,) + x.shape, x.dtype) for x in xs]
    if into is None:
        return _Ride(xs, shapes, {}, 3 * n, n, start, wait)
    return _Ride(list(xs) + list(into), shapes, {n + a: a for a in range(n)}, 3 * n, n, start, wait)


def _ride_gather_sibling(bufs):
    n = len(bufs)

    def copies(outs, send, recv):
        x, y, c = _me()
        out = []
        for a in range(n):
            for q, (px, py) in enumerate(ALL_CHIPS):
                there = outs[a].at[4 * px + 2 * py + c]
                here = outs[a].at[4 * px + 2 * py + 1 - c]
                out.append((_remote(there, there, send, recv, 4 * a + q, (x, y, 1 - c)),
                            _remote(here, here, send, recv, 4 * a + q, (x, y, 1 - c))))
        return out

    def start(ins, outs, send, recv, local):
        for sent, _ in copies(outs, send, recv):
            sent.start()

    def wait(ins, outs, send, recv, local):
        for sent, landed in copies(outs, send, recv):
            landed.wait_recv()
            sent.wait_send()

    shapes = [jax.ShapeDtypeStruct(b.shape, b.dtype) for b in bufs]
    return _Ride(bufs, shapes, {a: a for a in range(n)}, 4 * n, 0, start, wait)


def _ride_gather(xs, mid_frac, rows=None, into=None):
    n = len(xs)
    chips = _ride_gather_chips(xs, rows, into)

    def sibling_copies(outs, send, recv):
        x, y, c = _me()
        out = []
        for a in range(n):
            for q, (px, py) in enumerate(ALL_CHIPS):
                there = _rows_of(outs[a], rows, 4 * px + 2 * py + c)
                here = _rows_of(outs[a], rows, 4 * px + 2 * py + 1 - c)
                k = 3 * n + 4 * a + q
                out.append((_remote(there, there, send, recv, k, (x, y, 1 - c)),
                            _remote(here, here, send, recv, k, (x, y, 1 - c))))
        return out

    def mid(ins, outs, send, recv, local):
        chips.wait(ins, outs, send, recv, local)
        for sent, _ in sibling_copies(outs, send, recv):
            sent.start()

    def wait(ins, outs, send, recv, local):
        for sent, landed in sibling_copies(outs, send, recv):
            landed.wait_recv()
            sent.wait_send()

    ride = _Ride(chips.inputs, chips.out_shapes, chips.aliases, 7 * n, n, chips.start, wait)
    ride.mid, ride.mid_frac = mid, mid_frac
    return ride


def _ride_scatter_sibling(parts):
    n = len(parts)

    def copies(ins, outs, send, recv):
        x, y, c = _me()
        out = []
        for a in range(n):
            for q, (px, py) in enumerate(ALL_CHIPS):
                cp = _remote(ins[a].at[4 * px + 2 * py + 1 - c], outs[a].at[q], send, recv, 4 * a + q, (x, y, 1 - c))
                out.append(cp)
        return out

    def start(ins, outs, send, recv, local):
        for cp in copies(ins, outs, send, recv):
            cp.start()

    def wait(ins, outs, send, recv, local):
        for cp in copies(ins, outs, send, recv):
            cp.wait_recv()
            cp.wait_send()

    shapes = [jax.ShapeDtypeStruct((4,) + p.shape[1:], p.dtype) for p in parts]
    return _Ride(parts, shapes, {}, 4 * n, 0, start, wait)


def _ride_scatter_chips(pairs, rows=None, into=None):
    n = len(pairs)

    def part(ref, slot):
        return _rows_of(ref, rows, slot)

    def copies(ins, outs, send, recv):
        x, y, c = _me()
        out = []
        for a in range(n):
            for j, (px, py) in enumerate(_other_chips(x, y)):
                peer = (px, py, c)
                src = part(ins[a], 2 * px + py)
                out.append((_remote(src, part(outs[a], 2 * x + y), send, recv, 3 * a + j, peer),
                            _remote(src, part(outs[a], 2 * px + py), send, recv, 3 * a + j, peer)))
        return out

    def mine(ins, outs, local):
        x, y, _ = _me()
        return [pltpu.make_async_copy(part(ins[a], 2 * x + y), part(outs[a], 2 * x + y), local.at[a])
                for a in range(n)]

    def start(ins, outs, send, recv, local):
        for cp in mine(ins, outs, local):
            cp.start()
        for sent, _ in copies(ins, outs, send, recv):
            sent.start()

    def wait(ins, outs, send, recv, local):
        for sent, landed in copies(ins, outs, send, recv):
            landed.wait_recv()
            sent.wait_send()
        for cp in mine(ins, outs, local):
            cp.wait()

    shapes = [jax.ShapeDtypeStruct(p.shape, p.dtype) for p in pairs]
    if into is None:
        return _Ride(pairs, shapes, {}, 3 * n, n, start, wait)
    return _Ride(list(pairs) + list(into), shapes, {n + a: a for a in range(n)}, 3 * n, n, start, wait)


def _call(name, body, grid, in_specs, out_specs, out_shape, operands, scratch=(), rides=()):
    n_in, n_out, n_scr = len(operands), len(out_shape), len(scratch)
    ride_in, ride_out, ride_scr, aliases, spans = [], [], [], {}, []
    for r in rides:
        for i, o in r.aliases.items():
            aliases[n_in + len(ride_in) + i] = n_out + len(ride_out) + o
        spans.append((len(ride_in), len(r.inputs), len(ride_out), len(r.out_shapes)))
        ride_in += r.inputs
        ride_out += r.out_shapes
        ride_scr += [pltpu.SemaphoreType.DMA((r.n_remote,)), pltpu.SemaphoreType.DMA((r.n_remote,)),
                     pltpu.SemaphoreType.DMA((max(r.n_local, 1),))]

    def wrapped(*refs):
        c_in, r_in = refs[:n_in], refs[n_in:n_in + len(ride_in)]
        p = n_in + len(ride_in)
        c_out, r_out = refs[p:p + n_out], refs[p + n_out:p + n_out + len(ride_out)]
        p += n_out + len(ride_out)
        c_scr, r_scr = refs[p:p + n_scr], refs[p + n_scr:]

        n_steps = functools.reduce(lambda a, b: a * b, grid, 1)
        step = functools.reduce(lambda acc, ax: acc * grid[ax] + pl.program_id(ax), range(len(grid)), 0)

        def each(method, at):
            for k, (r, (i0, ni, o0, no)) in enumerate(zip(rides, spans)):
                fn = getattr(r, method, None)
                if fn is None:
                    continue
                run = functools.partial(fn, r_in[i0:i0 + ni], r_out[o0:o0 + no], *r_scr[3 * k:3 * k + 3])
                if grid:
                    pl.when(step == at(r))(run)
                else:
                    run()

        each("start", lambda r: 0)
        each("mid", lambda r: int(r.mid_frac * (n_steps - 1)))
        body(*c_in, *c_out, *c_scr)
        each("wait", lambda r: n_steps - 1)

    outs = pl.pallas_call(
        wrapped, grid=grid, in_specs=list(in_specs) + [ANY] * len(ride_in),
        out_specs=list(out_specs) + [ANY] * len(ride_out), out_shape=list(out_shape) + ride_out,
        scratch_shapes=list(scratch) + ride_scr, input_output_aliases=aliases,
        compiler_params=_params(len(grid)), name=name)(*operands, *ride_in)
    outs = list(outs)
    ride_results = [outs[n_out + o0:n_out + o0 + no] for (_, _, o0, no) in spans]
    return outs[:n_out], ride_results


class _Tail:
    def __init__(self, extra, out_shapes, out_specs, fn):
        self.extra, self.out_shapes, self.out_specs, self.fn = list(extra), list(out_shapes), list(out_specs), fn


def _mm(name, pairs, dims, grid, out_shape, out_spec, res=None, res_spec=None, alpha=1.0, rides=(), tail=None):
    n = len(pairs)
    nk = grid[-1]
    kax = len(grid) - 1
    acc_shape = tuple(d for d in out_spec.block_shape if d is not None)
    n_extra = len(tail.extra) if tail else 0
    n_outs = len(tail.out_shapes) if tail else 1

    def body(*refs):
        pos = 2 * n
        r_ref = None
        if res is not None:
            r_ref = refs[pos]
            pos += 1
        x_refs = refs[pos:pos + n_extra]
        o_refs = refs[pos + n_extra:pos + n_extra + n_outs]
        pos += n_extra + n_outs
        part = None
        for p in range(n):
            d = _dot(refs[2 * p][...], refs[2 * p + 1][...], dims)
            part = d if part is None else part + d

        def finish(acc):
            if alpha != 1.0:
                acc = acc * alpha
            if r_ref is not None:
                acc = r_ref[...] + acc
            if tail:
                tail.fn(acc, x_refs, o_refs)
            else:
                o_refs[0][...] = acc.astype(o_refs[0].dtype)

        if nk == 1:
            finish(part)
        else:
            acc_ref = refs[pos]
            k = pl.program_id(kax)

            @pl.when(k == 0)
            def _():
                acc_ref[...] = part

            @pl.when(k > 0)
            def _():
                acc_ref[...] += part

            @pl.when(k == nk - 1)
            def _():
                finish(acc_ref[...])

    operands, in_specs = [], []
    for a, a_spec, b, b_spec in pairs:
        operands += [a, b]
        in_specs += [a_spec, b_spec]
    if res is not None:
        operands.append(res)
        in_specs.append(res_spec)
    for a, a_spec in (tail.extra if tail else []):
        operands.append(a)
        in_specs.append(a_spec)
    outs, ride_results = _call(name, body, grid, in_specs, tail.out_specs if tail else [out_spec],
                               tail.out_shapes if tail else [out_shape], operands,
                               scratch=[pltpu.VMEM(acc_shape, F32)] if nk > 1 else [], rides=rides)
    outs = outs if tail else outs[0]
    return (outs, ride_results) if rides else outs


def _accumulate(ref, part):
    @pl.when(pl.program_id(0) == 0)
    def _():
        ref[...] = part

    @pl.when(pl.program_id(0) > 0)
    def _():
        ref[...] += part


def _tail_norm(gain, t, d, tm):
    def fn(v, x_refs, o_refs):
        o_refs[0][...] = v
        r = lax.rsqrt(jnp.mean(v * v, axis=-1, keepdims=True) + EPS)
        o_refs[1][...] = (v * r * x_refs[0][...]).astype(BF16)

    rows = pl.BlockSpec((tm, d), lambda i, k: (i, 0))
    return _Tail([(gain, pl.BlockSpec((1, d), lambda i, k: (0, 0)))],
                 [jax.ShapeDtypeStruct((t, d), F32), jax.ShapeDtypeStruct((t, d), BF16)], [rows, rows], fn)


def _tail_loss(target, t, d, tm):
    def fn(v, x_refs, o_refs):
        err = v - x_refs[0][...]
        dy = err * (1.0 / d)
        o_refs[0][...] = dy
        o_refs[1][...] = dy.astype(BF16)
        _accumulate(o_refs[2], jnp.zeros((8, 128), F32) + jnp.sum(err * err))

    rows = pl.BlockSpec((tm, d), lambda i, k: (i, 0))
    return _Tail([(target, rows)],
                 [jax.ShapeDtypeStruct((t, d), F32), jax.ShapeDtypeStruct((t, d), BF16),
                  jax.ShapeDtypeStruct((8, 128), F32)],
                 [rows, rows, pl.BlockSpec((8, 128), lambda i, k: (0, 0))], fn)


def _tail_norm_bwd(x, gain, dres, t, d, tm):
    def fn(dy, x_refs, o_refs):
        xv = x_refs[0][...]
        r = lax.rsqrt(jnp.mean(xv * xv, axis=-1, keepdims=True) + EPS)
        xh = xv * r
        dxh = dy * x_refs[1][...]
        dx = r * (dxh - xh * jnp.mean(dxh * xh, axis=-1, keepdims=True)) + x_refs[2][...]
        o_refs[0][...] = dx
        o_refs[1][...] = dx.astype(BF16)
        _accumulate(o_refs[2], jnp.sum(dy * xh, axis=0, keepdims=True))

    rows = pl.BlockSpec((tm, d), lambda i, k: (i, 0))
    vec = pl.BlockSpec((1, d), lambda i, k: (0, 0))
    return _Tail([(x, rows), (gain, vec), (dres, rows)],
                 [jax.ShapeDtypeStruct((t, d), F32), jax.ShapeDtypeStruct((t, d), BF16),
                  jax.ShapeDtypeStruct((1, d), F32)], [rows, rows, vec], fn)


def _cast_bf16(x, name):
    rows, cols = x.shape
    tm = _rows_tile(rows, cols * 4)

    def body(x_ref, o_ref):
        o_ref[...] = x_ref[...].astype(BF16)

    spec = pl.BlockSpec((tm, cols), lambda i: (i, 0))
    return pl.pallas_call(body, grid=(rows // tm,), in_specs=[spec], out_specs=spec,
                          out_shape=jax.ShapeDtypeStruct(x.shape, BF16), compiler_params=_params(1), name=name)(x)


def _rms_fwd(x, gain, name):
    rows, d = x.shape
    tm = min(rows, 512)

    def body(x_ref, g_ref, o_ref):
        xv = x_ref[...]
        r = lax.rsqrt(jnp.mean(xv * xv, axis=-1, keepdims=True) + EPS)
        o_ref[...] = (xv * r * g_ref[...]).astype(BF16)

    spec = pl.BlockSpec((tm, d), lambda i: (i, 0))
    return pl.pallas_call(body, grid=(rows // tm,), in_specs=[spec, pl.BlockSpec((1, d), lambda i: (0, 0))],
                          out_specs=spec, out_shape=jax.ShapeDtypeStruct(x.shape, BF16),
                          compiler_params=_params(1), name=name)(x, gain)


def _rms_bwd(x, gain, dxn, dres, name, rides=()):
    rows, d = x.shape
    tm = min(rows, 256)
    with_res = dres is not None

    def body(*refs):
        if with_res:
            x_ref, g_ref, dy_ref, r_ref, dx_ref, dxb_ref, dg_ref = refs
        else:
            x_ref, g_ref, dy_ref, dx_ref, dxb_ref, dg_ref = refs
        xv = x_ref[...]
        r = lax.rsqrt(jnp.mean(xv * xv, axis=-1, keepdims=True) + EPS)
        xh = xv * r
        dy = dy_ref[...]
        dxh = dy * g_ref[...]
        dx = r * (dxh - xh * jnp.mean(dxh * xh, axis=-1, keepdims=True))
        if with_res:
            dx = dx + r_ref[...]
        dx_ref[...] = dx
        dxb_ref[...] = dx.astype(BF16)
        part = jnp.sum(dy * xh, axis=0, keepdims=True)

        @pl.when(pl.program_id(0) == 0)
        def _():
            dg_ref[...] = part

        @pl.when(pl.program_id(0) > 0)
        def _():
            dg_ref[...] += part

    spec = pl.BlockSpec((tm, d), lambda i: (i, 0))
    vec = pl.BlockSpec((1, d), lambda i: (0, 0))
    ops = [x, gain, dxn] + ([dres] if with_res else [])
    outs, ride_results = _call(
        name, body, (rows // tm,), [spec, vec, spec] + ([spec] if with_res else []), [spec, spec, vec],
        [jax.ShapeDtypeStruct(x.shape, F32), jax.ShapeDtypeStruct(x.shape, BF16), jax.ShapeDtypeStruct((1, d), F32)],
        ops, rides=rides)
    return (outs, ride_results) if rides else outs


ROW_CHUNK = 256
SHARD_ROWS = 2048


def _ffn_up(xn, wg, wu, tag, rides=()):
    t, d = xn.shape
    nd, fs, _ = wg.shape
    tm = min(t, SHARD_ROWS)
    rc = min(tm, ROW_CHUNK)

    def body(x_ref, wg_ref, wu_ref, a_ref, b_ref, h_ref):
        for r in range(0, tm, rc):
            xv = x_ref[r:r + rc, :]
            g = _dot(xv, wg_ref[...], "nt")
            u = _dot(xv, wu_ref[...], "nt")
            sig = _sigmoid(g)
            silu = g * sig
            a_ref[r:r + rc, :] = (0.5 * u * (sig + silu * (1.0 - sig))).astype(BF16)
            b_ref[r:r + rc, :] = (0.5 * silu).astype(BF16)
            h_ref[r:r + rc, :] = (silu * u).astype(BF16)

    wspec = pl.BlockSpec((None, fs, d), lambda j, i: (j, 0, 0))
    hspec = pl.BlockSpec((None, tm, fs), lambda j, i: (j, i, 0))
    hid = jax.ShapeDtypeStruct((nd, t, fs), BF16)
    return _call(f"{tag}_up", body, (nd, t // tm), [pl.BlockSpec((tm, d), lambda j, i: (i, 0)), wspec, wspec],
                 [hspec] * 3, [hid] * 3, [xn, wg, wu], rides=rides)


CONTRACT_ROWS = 256


def _ffn_contract(hid, w, name, res=None, alpha=1.0, rides=(), tail=None):
    nd, t, fs = hid.shape
    d = w.shape[2]
    tm = min(t, CONTRACT_ROWS if tail or res is not None else 2 * CONTRACT_ROWS)
    xspec = pl.BlockSpec((tm, d), lambda i, k: (i, 0))
    pairs = [(hid, pl.BlockSpec((None, tm, fs), lambda i, k, s=s: (s, i, 0)),
              w, pl.BlockSpec((None, fs, d), lambda i, k, s=s: (s, 0, 0), pipeline_mode=pl.Buffered(1)))
             for s in range(nd)]
    return _mm(name, pairs, "nn", (t // tm, 1), jax.ShapeDtypeStruct((t, d), F32), xspec,
               res=res, res_spec=xspec if res is not None else None, alpha=alpha, rides=rides, tail=tail)


def _ffn_down(x, h, wd, tag, rides=(), tail=None):
    return _ffn_contract(h, wd, f"{tag}_down", res=x, alpha=0.5, rides=rides, tail=tail)


def _ffn_dact(dyb, wd, a, b, tag, rides=()):
    nd, t, fs = a.shape
    d = dyb.shape[1]
    tm = min(t, SHARD_ROWS)
    rc = min(tm, ROW_CHUNK)

    def body(dy_ref, wd_ref, a_ref, b_ref, dg_ref, du_ref):
        for r in range(0, tm, rc):
            dh = _dot(dy_ref[r:r + rc, :], wd_ref[...], "nt")
            dg_ref[r:r + rc, :] = (dh * a_ref[r:r + rc, :].astype(F32)).astype(BF16)
            du_ref[r:r + rc, :] = (dh * b_ref[r:r + rc, :].astype(F32)).astype(BF16)

    hspec = pl.BlockSpec((None, tm, fs), lambda j, i: (j, i, 0))
    hid = jax.ShapeDtypeStruct((nd, t, fs), BF16)
    return _call(f"{tag}_dact", body, (nd, t // tm),
                 [pl.BlockSpec((tm, d), lambda j, i: (i, 0)), pl.BlockSpec((None, fs, d), lambda j, i: (j, 0, 0)),
                  hspec, hspec], [hspec] * 2, [hid] * 2, [dyb, wd, a, b], rides=rides)


def _ffn_dw(hid, act, alpha, name, rides=()):
    nd, t, fs = hid.shape
    d = act.shape[1]
    tk = t
    return _mm(name, [(hid, pl.BlockSpec((None, tk, fs), lambda j, k: (j, k, 0)),
                       act, pl.BlockSpec((tk, d), lambda j, k: (k, 0), pipeline_mode=pl.Buffered(1)))],
               "tn", (nd, t // tk),
               jax.ShapeDtypeStruct((nd, fs, d), BF16), pl.BlockSpec((None, fs, d), lambda j, k: (j, 0, 0)),
               alpha=alpha, rides=rides)


def _head_norm(x, gain):
    r = lax.rsqrt(jnp.mean(x * x, axis=-1, keepdims=True) + EPS)
    return x * r * gain


def _head_norm_bwd(x, gain, dy):
    r = lax.rsqrt(jnp.mean(x * x, axis=-1, keepdims=True) + EPS)
    xh = x * r
    dxh = dy * gain
    dx = r * (dxh - xh * jnp.mean(dxh * xh, axis=-1, keepdims=True))
    return dx, jnp.sum(dy * xh, axis=0, keepdims=True)


def _hs(h, base=0):
    return slice(base + h * HEAD, base + (h + 1) * HEAD)


def _tri(n, lower):
    r = lax.broadcasted_iota(jnp.int32, (n, n), 0)
    c = lax.broadcasted_iota(jnp.int32, (n, n), 1)
    return ((r >= c) if lower else (r <= c)).astype(F32)


def _attn_pre(proj, sp, name):
    t = proj.shape[0]
    tm = min(t, 256)

    def body(p_ref, sp_ref, fq, fk, fv, sq, sk, sv, mq, cc, carry):
        @pl.when(pl.program_id(0) == 0)
        def _():
            carry[...] = jnp.zeros_like(carry)

        for h in range(FOX_H):
            fq[:, _hs(h)] = _head_norm(p_ref[:, _hs(h, C_FQ)], sp_ref[R_FQ:R_FQ + 1, :]).astype(BF16)
            fk[:, _hs(h)] = _head_norm(p_ref[:, _hs(h, C_FK)], sp_ref[R_FK:R_FK + 1, :]).astype(BF16)
        fv[...] = p_ref[:, C_FV:C_FV + FOX_W].astype(BF16)
        for h in range(SWA_H):
            sq[:, _hs(h)] = _head_norm(p_ref[:, _hs(h, C_SQ)], sp_ref[R_SQ:R_SQ + 1, :]).astype(BF16)
        for h in range(SWA_KV):
            sk[:, _hs(h)] = _head_norm(p_ref[:, _hs(h, C_SK)], sp_ref[R_SK:R_SK + 1, :]).astype(BF16)
        sv[...] = p_ref[:, C_SV:C_SV + SWA_KV_W].astype(BF16)
        for h in range(MEM_H):
            mq[:, _hs(h)] = _head_norm(p_ref[:, _hs(h, C_MQ)], sp_ref[R_MQ:R_MQ + 1, :]).astype(BF16)
        z = p_ref[:, C_FL:C_FL + HEAD] + sp_ref[R_FB:R_FB + 1, :]
        lane = lax.broadcasted_iota(jnp.int32, z.shape, 1)
        log_f = jnp.minimum(z, 0.0) - jnp.log(1.0 + jnp.exp(-jnp.abs(z)))
        log_f = jnp.where(lane < N_LOGIT, log_f, 0.0)
        c = jnp.dot(_tri(tm, True), log_f, precision=lax.Precision.HIGHEST, preferred_element_type=F32)
        c = c + carry[0:1, :]
        cc[...] = c
        carry[...] = jnp.broadcast_to(c[tm - 1:tm, :], carry.shape)

    def rows(w):
        return pl.BlockSpec((tm, w), lambda i: (i, 0))

    def shape(w, dt):
        return jax.ShapeDtypeStruct((t, w), dt)

    widths = [FOX_W, FOX_W, FOX_W, SWA_W, SWA_KV_W, SWA_KV_W, MEM_W]
    return pl.pallas_call(
        body, grid=(t // tm,), in_specs=[rows(IN_W), pl.BlockSpec((16, 128), lambda i: (0, 0))],
        out_specs=[rows(w) for w in widths] + [rows(HEAD)],
        out_shape=[shape(w, BF16) for w in widths] + [shape(HEAD, F32)],
        scratch_shapes=[pltpu.VMEM((8, 128), F32)], compiler_params=_params(1), name=name)(proj, sp)


def _attn_post_bwd(proj, sp, dfq, dfk, dfv, dsq, dsk, dsv, dmq, dc_col, dc_row_t, name):
    t = proj.shape[0]
    tm = min(t, 256)
    nb = t // tm

    def body(p_ref, sp_ref, dfq_r, dfk_r, dfv_r, dsq_r, dsk_r, dsv_r, dmq_r, dcc_r, dcr_r, dp_ref, dsp_ref, carry):
        @pl.when(pl.program_id(0) == 0)
        def _():
            carry[...] = jnp.zeros_like(carry)
            dsp_ref[...] = jnp.zeros_like(dsp_ref)

        def group(n_heads, col, row, d_ref):
            total = None
            for h in range(n_heads):
                dx, dg = _head_norm_bwd(p_ref[:, _hs(h, col)], sp_ref[row:row + 1, :], d_ref[:, _hs(h)])
                dp_ref[:, _hs(h, col)] = dx.astype(BF16)
                total = dg if total is None else total + dg
            dsp_ref[row:row + 1, :] += total

        group(FOX_H, C_FQ, R_FQ, dfq_r)
        group(FOX_H, C_FK, R_FK, dfk_r)
        dp_ref[:, C_FV:C_FV + FOX_W] = dfv_r[...].astype(BF16)
        group(SWA_H, C_SQ, R_SQ, dsq_r)
        group(SWA_KV, C_SK, R_SK, dsk_r)
        dp_ref[:, C_SV:C_SV + SWA_KV_W] = dsv_r[...].astype(BF16)
        group(MEM_H, C_MQ, R_MQ, dmq_r)
        dc = dcc_r[...] - dcr_r[...]
        rc = jnp.dot(_tri(tm, False), dc, precision=lax.Precision.HIGHEST, preferred_element_type=F32)
        rc = rc + carry[0:1, :]
        carry[...] = jnp.broadcast_to(rc[0:1, :], carry.shape)
        z = p_ref[:, C_FL:C_FL + HEAD] + sp_ref[R_FB:R_FB + 1, :]
        dz = rc * _sigmoid(-z)
        dp_ref[:, C_FL:C_FL + HEAD] = dz.astype(BF16)
        dsp_ref[R_FB:R_FB + 1, :] += jnp.sum(dz, axis=0, keepdims=True)

    def rows(w):
        return pl.BlockSpec((tm, w), lambda i: (nb - 1 - i, 0))

    small = pl.BlockSpec((16, 128), lambda i: (0, 0))
    widths = [FOX_W, FOX_W, FOX_W, SWA_W, SWA_KV_W, SWA_KV_W, MEM_W, HEAD, HEAD]
    return pl.pallas_call(
        body, grid=(nb,), in_specs=[rows(IN_W), small] + [rows(w) for w in widths],
        out_specs=[rows(IN_W), small],
        out_shape=[jax.ShapeDtypeStruct((t, IN_W), BF16), jax.ShapeDtypeStruct((16, 128), F32)],
        scratch_shapes=[pltpu.VMEM((8, 128), F32)], compiler_params=_params(1), name=name,
    )(proj, sp, dfq, dfk, dfv, dsq, dsk, dsv, dmq, dc_col, dc_row_t)


def _head_column(values):
    rows = values[0].shape[0]
    lane = lax.broadcasted_iota(jnp.int32, (rows, HEAD), 1)
    out = jnp.zeros((rows, HEAD), F32)
    for h, v in enumerate(values):
        out = jnp.where(lane == h, v, out)
    return out


def _head_row(values, n_rows=8):
    cols = values[0].shape[1]
    sub = lax.broadcasted_iota(jnp.int32, (n_rows, cols), 0)
    out = jnp.zeros((n_rows, cols), F32)
    for h, v in enumerate(values):
        out = jnp.where(sub == h, v, out)
    return out


def _delta(dmixed, o_a, o_b, o_c, name):
    t = dmixed.shape[0]
    tm = min(t, 512)

    def body(d_ref, a_ref, b_ref, c_ref, o_ref, rep_ref):
        cols = []
        for ref, n_heads, base in ((a_ref, FOX_H, 0), (b_ref, SWA_H, FOX_W), (c_ref, MEM_H, FOX_W + SWA_W)):
            for h in range(n_heads):
                cols.append(jnp.sum(d_ref[:, _hs(h, base)] * ref[:, _hs(h)], axis=-1, keepdims=True))
        o_ref[...] = _head_column(cols)
        for h in range(FOX_H):
            rep_ref[h] = jnp.broadcast_to(cols[h], (tm, HEAD))

    def rows(w):
        return pl.BlockSpec((tm, w), lambda i: (i, 0))

    return pl.pallas_call(body, grid=(t // tm,), in_specs=[rows(dmixed.shape[1]), rows(FOX_W), rows(SWA_W), rows(MEM_W)],
                          out_specs=[rows(HEAD), pl.BlockSpec((FOX_H, tm, HEAD), lambda i: (0, i, 0))],
                          out_shape=[jax.ShapeDtypeStruct((t, HEAD), F32), jax.ShapeDtypeStruct((FOX_H, t, HEAD), F32)],
                          compiler_params=_params(1), name=name)(dmixed, o_a, o_b, o_c)


def _fox_fwd(fq, fk, fv, c_rep, c_row, name, rides=()):
    t = fq.shape[0]
    tb = min(t, 512)
    nb = t // tb
    n_tiles = tb // HEAD

    def body(q_ref, k_ref, v_ref, cc_ref, cr_ref, o_ref, lse_ref, m_s, l_s, acc_s):
        qi, ki = pl.program_id(0), pl.program_id(1)

        @pl.when(ki == 0)
        def _():
            m_s[...] = jnp.full_like(m_s, NEG_INF)
            l_s[...] = jnp.zeros_like(l_s)
            acc_s[...] = jnp.zeros_like(acc_s)

        def step(diagonal):
            if diagonal:
                r = lax.broadcasted_iota(jnp.int32, (tb, HEAD), 0)
                c = lax.broadcasted_iota(jnp.int32, (tb, HEAD), 1)
            for h in range(FOX_H):
                s = _dot(q_ref[:, _hs(h)], k_ref[:, _hs(h)], "nt")
                cc = cc_ref[h]
                tiles, m_cur = [], None
                for j in range(n_tiles):
                    st = s[:, _hs(j)] * SCALE + cc - cr_ref[h:h + 1, _hs(j)]
                    if diagonal:
                        st = jnp.where(r >= c + j * HEAD, st, NEG_INF)
                    tiles.append(st)
                    m_cur = st if m_cur is None else jnp.maximum(m_cur, st)
                m_prev = m_s[h]
                m_new = jnp.maximum(m_prev, jnp.max(m_cur, axis=-1, keepdims=True))
                alpha = jnp.exp(m_prev - m_new)
                ps = [jnp.exp(st - m_new) for st in tiles]
                l_cur = ps[0]
                for p in ps[1:]:
                    l_cur = l_cur + p
                l_s[h] = alpha * l_s[h] + jnp.sum(l_cur, axis=-1, keepdims=True)
                p = jnp.concatenate([p.astype(BF16) for p in ps], axis=1)
                acc_s[:, _hs(h)] = alpha * acc_s[:, _hs(h)] + _dot(p, v_ref[:, _hs(h)])
                m_s[h] = m_new

        @pl.when(ki < qi)
        def _():
            step(False)

        @pl.when(ki == qi)
        def _():
            step(True)
            for h in range(FOX_H):
                o_ref[:, _hs(h)] = acc_s[:, _hs(h)] / l_s[h]
                lse_ref[h] = m_s[h] + jnp.log(l_s[h])

    qspec = pl.BlockSpec((tb, FOX_W), lambda i, j: (i, 0))
    kspec = pl.BlockSpec((tb, FOX_W), lambda i, j: (jnp.minimum(i, j), 0))
    rep = pl.BlockSpec((FOX_H, tb, HEAD), lambda i, j: (0, i, 0))
    return _call(
        name, body, (nb, nb),
        [qspec, kspec, kspec, rep, pl.BlockSpec((8, tb), lambda i, j: (0, jnp.minimum(i, j)))],
        [qspec, rep],
        [jax.ShapeDtypeStruct((t, FOX_W), F32), jax.ShapeDtypeStruct((FOX_H, t, HEAD), F32)],
        [fq, fk, fv, c_rep, c_row],
        scratch=[pltpu.VMEM((FOX_H, tb, HEAD), F32), pltpu.VMEM((FOX_H, tb, HEAD), F32), pltpu.VMEM((tb, FOX_W), F32)],
        rides=rides)


def _fox_bwd(fq, fk, fv, c_rep, c_row, dmixed, lse, delta, name, rides=()):
    t = fq.shape[0]
    tb = min(t, 512)
    nb = t // tb
    n_tiles = tb // HEAD

    def body(q_ref, k_ref, v_ref, cc_ref, cr_ref, do_ref, lse_ref, dl_ref,
             dq_ref, dk_ref, dv_ref, dcc_ref, dcr_ref):
        ki, qi = pl.program_id(0), pl.program_id(1)

        @pl.when((ki == 0) & (qi == 0))
        def _():
            dq_ref[...] = jnp.zeros_like(dq_ref)
            dcc_ref[...] = jnp.zeros_like(dcc_ref)

        @pl.when(qi == 0)
        def _():
            dk_ref[...] = jnp.zeros_like(dk_ref)
            dv_ref[...] = jnp.zeros_like(dv_ref)
            dcr_ref[...] = jnp.zeros_like(dcr_ref)

        def step(diagonal):
            rows = pl.ds(pl.multiple_of(qi * tb, tb), tb)
            if diagonal:
                r = lax.broadcasted_iota(jnp.int32, (tb, HEAD), 0)
                c = lax.broadcasted_iota(jnp.int32, (tb, HEAD), 1)
            row_sums, col_sums = [], []
            for h in range(FOX_H):
                q, k, v, do = q_ref[:, _hs(h)], k_ref[:, _hs(h)], v_ref[:, _hs(h)], do_ref[:, _hs(h)]
                s = _dot(q, k, "nt")
                dp = _dot(do, v, "nt")
                cc, lse_h, dl_h = cc_ref[h], lse_ref[h], dl_ref[h]
                ps, dss, row = [], [], None
                for j in range(n_tiles):
                    st = s[:, _hs(j)] * SCALE + cc - cr_ref[h:h + 1, _hs(j)]
                    if diagonal:
                        st = jnp.where(r >= c + j * HEAD, st, NEG_INF)
                    pt = jnp.exp(st - lse_h)
                    dst = pt * (dp[:, _hs(j)] - dl_h)
                    ps.append(pt.astype(BF16))
                    dss.append(dst)
                    row = dst if row is None else row + dst
                p = jnp.concatenate(ps, axis=1)
                ds = jnp.concatenate(dss, axis=1)
                dsb = ds.astype(BF16)
                dv_ref[:, _hs(h)] += _dot(p, do, "tn")
                dk_ref[:, _hs(h)] += _dot(dsb, q, "tn") * SCALE
                dq_ref[rows, _hs(h)] += _dot(dsb, k) * SCALE
                row_sums.append(jnp.sum(row, axis=1, keepdims=True))
                col_sums.append(jnp.sum(ds, axis=0, keepdims=True))
            dcc_ref[rows, :] += _head_column(row_sums)
            dcr_ref[...] += _head_row(col_sums)

        @pl.when(qi > ki)
        def _():
            step(False)

        @pl.when(qi == ki)
        def _():
            step(True)

    def qmap(j, i):
        return (jnp.maximum(i, j), 0)

    qspec = pl.BlockSpec((tb, FOX_W), qmap)
    kspec = pl.BlockSpec((tb, FOX_W), lambda j, i: (j, 0))
    rep = pl.BlockSpec((FOX_H, tb, HEAD), lambda j, i: (0, jnp.maximum(i, j), 0))
    rowspec = pl.BlockSpec((8, tb), lambda j, i: (0, j))
    return _call(
        name, body, (nb, nb), [qspec, kspec, kspec, rep, rowspec, qspec, rep, rep],
        [pl.BlockSpec((t, FOX_W), lambda j, i: (0, 0)), kspec, kspec,
         pl.BlockSpec((t, HEAD), lambda j, i: (0, 0)), rowspec],
        [jax.ShapeDtypeStruct((t, FOX_W), F32)] * 3 + [jax.ShapeDtypeStruct((t, HEAD), F32),
                                                       jax.ShapeDtypeStruct((8, t), F32)],
        [fq, fk, fv, c_rep, c_row, dmixed, lse, delta], rides=rides)


def _swa_logits(q, k_cur, k_prev, slope, first_block):
    w = SWA_BLOCK
    r = lax.broadcasted_iota(jnp.int32, (w, w), 0)
    j = lax.broadcasted_iota(jnp.int32, (w, w), 1)
    dist_cur = r - j
    dist_prev = w + r - j
    s_cur = _dot(q, k_cur, "nt") * SCALE - slope * dist_cur.astype(F32)
    s_cur = jnp.where(dist_cur >= 0, s_cur, NEG_INF)
    s_prev = _dot(q, k_prev, "nt") * SCALE - slope * dist_prev.astype(F32)
    s_prev = jnp.where((j > r) & jnp.logical_not(first_block), s_prev, NEG_INF)
    return s_cur, s_prev


def _slope(h):
    return float(2.0 ** (-8.0 * (h + 1) / SWA_H))


def _swa_fwd(sq, sk, sv, sp, name, rides=()):
    t = sq.shape[0]
    w = SWA_BLOCK
    nb = t // w
    group = SWA_H // SWA_KV

    def body(q_ref, kp_ref, kc_ref, vp_ref, vc_ref, sp_ref, o_ref, lse_ref):
        first = pl.program_id(0) == 0
        lses = []
        for h in range(SWA_H):
            kv = h // group
            s_cur, s_prev = _swa_logits(q_ref[:, _hs(h)], kc_ref[:, _hs(kv)], kp_ref[:, _hs(kv)], _slope(h), first)
            sink = sp_ref[R_SINK:R_SINK + 1, h:h + 1]
            m = jnp.maximum(jnp.maximum(jnp.max(s_cur, axis=-1, keepdims=True),
                                        jnp.max(s_prev, axis=-1, keepdims=True)), sink)
            p_cur = jnp.exp(s_cur - m)
            p_prev = jnp.exp(s_prev - m)
            l = jnp.sum(p_cur, axis=-1, keepdims=True) + jnp.sum(p_prev, axis=-1, keepdims=True) + jnp.exp(sink - m)
            o_ref[:, _hs(h)] = (_dot(p_cur, vc_ref[:, _hs(kv)]) + _dot(p_prev, vp_ref[:, _hs(kv)])) / l
            lses.append(m + jnp.log(l))
        lse_ref[...] = _head_column(lses)

    qspec = pl.BlockSpec((w, SWA_W), lambda n: (n, 0))
    cur = pl.BlockSpec((w, SWA_KV_W), lambda n: (n, 0))
    prev = pl.BlockSpec((w, SWA_KV_W), lambda n: (jnp.maximum(n - 1, 0), 0))
    return _call(
        name, body, (nb,), [qspec, prev, cur, prev, cur, pl.BlockSpec((16, 128), lambda n: (0, 0))],
        [qspec, pl.BlockSpec((w, HEAD), lambda n: (n, 0))],
        [jax.ShapeDtypeStruct((t, SWA_W), F32), jax.ShapeDtypeStruct((t, HEAD), F32)],
        [sq, sk, sk, sv, sv, sp], rides=rides)


def _swa_bwd(sq, sk, sv, sp, dmixed, lse, delta, name):
    t = sq.shape[0]
    w = SWA_BLOCK
    nb = t // w
    group = SWA_H // SWA_KV
    do_block = FOX_W // SWA_W
    assert FOX_W % SWA_W == 0

    def body(q_ref, kp_ref, kc_ref, vp_ref, vc_ref, sp_ref, do_ref, lse_ref, dl_ref,
             dq_ref, dk_ref, dv_ref, dsp_ref, ck, cv):
        step = pl.program_id(0)
        first = step == nb - 1

        @pl.when(step == 0)
        def _():
            ck[...] = jnp.zeros_like(ck)
            cv[...] = jnp.zeros_like(cv)
            dsp_ref[...] = jnp.zeros_like(dsp_ref)

        dk_cur = [None] * SWA_KV
        dk_prev = [None] * SWA_KV
        dv_cur = [None] * SWA_KV
        dv_prev = [None] * SWA_KV
        dsinks = []

        def add(lst, i, v):
            lst[i] = v if lst[i] is None else lst[i] + v

        for h in range(SWA_H):
            kv = h // group
            q, do = q_ref[:, _hs(h)], do_ref[:, _hs(h)]
            kc, kp, vc, vp = kc_ref[:, _hs(kv)], kp_ref[:, _hs(kv)], vc_ref[:, _hs(kv)], vp_ref[:, _hs(kv)]
            s_cur, s_prev = _swa_logits(q, kc, kp, _slope(h), first)
            lse_h = lse_ref[:, h:h + 1]
            dl_h = dl_ref[:, FOX_H + h:FOX_H + h + 1]
            p_cur = jnp.exp(s_cur - lse_h)
            p_prev = jnp.exp(s_prev - lse_h)
            p_sink = jnp.exp(sp_ref[R_SINK:R_SINK + 1, h:h + 1] - lse_h)
            ds_cur = p_cur * (_dot(do, vc, "nt") - dl_h)
            ds_prev = p_prev * (_dot(do, vp, "nt") - dl_h)
            dq_ref[:, _hs(h)] = (_dot(ds_cur, kc) + _dot(ds_prev, kp)) * SCALE
            add(dk_cur, kv, _dot(ds_cur, q, "tn") * SCALE)
            add(dk_prev, kv, _dot(ds_prev, q, "tn") * SCALE)
            add(dv_cur, kv, _dot(p_cur, do, "tn"))
            add(dv_prev, kv, _dot(p_prev, do, "tn"))
            dsinks.append(-jnp.sum(p_sink * dl_h, axis=0, keepdims=True))
        for kv in range(SWA_KV):
            dk_ref[:, _hs(kv)] = dk_cur[kv] + ck[:, _hs(kv)]
            dv_ref[:, _hs(kv)] = dv_cur[kv] + cv[:, _hs(kv)]
            ck[:, _hs(kv)] = dk_prev[kv]
            cv[:, _hs(kv)] = dv_prev[kv]
        lane = lax.broadcasted_iota(jnp.int32, (1, HEAD), 1)
        row = jnp.zeros((1, HEAD), F32)
        for h in range(SWA_H):
            row = jnp.where(lane == h, dsinks[h], row)
        dsp_ref[R_SINK:R_SINK + 1, :] += row

    def rev(n):
        return nb - 1 - n

    qspec = pl.BlockSpec((w, SWA_W), lambda n: (rev(n), 0))
    cur = pl.BlockSpec((w, SWA_KV_W), lambda n: (rev(n), 0))
    prev = pl.BlockSpec((w, SWA_KV_W), lambda n: (jnp.maximum(rev(n) - 1, 0), 0))
    col = pl.BlockSpec((w, HEAD), lambda n: (rev(n), 0))
    small = pl.BlockSpec((16, 128), lambda n: (0, 0))
    return pl.pallas_call(
        body, grid=(nb,),
        in_specs=[qspec, prev, cur, prev, cur, small, pl.BlockSpec((w, SWA_W), lambda n: (rev(n), do_block)), col, col],
        out_specs=[qspec, cur, cur, small],
        out_shape=[jax.ShapeDtypeStruct((t, SWA_W), F32), jax.ShapeDtypeStruct((t, SWA_KV_W), F32),
                   jax.ShapeDtypeStruct((t, SWA_KV_W), F32), jax.ShapeDtypeStruct((16, 128), F32)],
        scratch_shapes=[pltpu.VMEM((w, SWA_KV_W), F32), pltpu.VMEM((w, SWA_KV_W), F32)],
        compiler_params=_params(1), name=name)(sq, sk, sk, sv, sv, sp, dmixed, lse, delta)


def _mem_pre(mkv, sp, name):
    m = mkv.shape[0]

    def body(x_ref, sp_ref, k_ref, v_ref):
        for h in range(MEM_H):
            k_ref[:, _hs(h)] = _head_norm(x_ref[:, _hs(h)], sp_ref[R_MK:R_MK + 1, :]).astype(BF16)
        v_ref[...] = x_ref[:, MEM_W:2 * MEM_W].astype(BF16)

    out = jax.ShapeDtypeStruct((m, MEM_W), BF16)
    return pl.pallas_call(body, out_shape=[out, out], name=name)(mkv, sp)


def _mem_post_bwd(mkv, sp, dmk, dmv, name):
    m = mkv.shape[0]

    def body(x_ref, sp_ref, dk_ref, dv_ref, d_ref, dsp_ref):
        dsp_ref[...] = jnp.zeros_like(dsp_ref)
        total = None
        for h in range(MEM_H):
            dx, dg = _head_norm_bwd(x_ref[:, _hs(h)], sp_ref[R_MK:R_MK + 1, :], dk_ref[:, _hs(h)])
            d_ref[:, _hs(h)] = dx.astype(BF16)
            total = dg if total is None else total + dg
        d_ref[:, MEM_W:2 * MEM_W] = dv_ref[...].astype(BF16)
        dsp_ref[R_MK:R_MK + 1, :] = total

    return pl.pallas_call(body, out_shape=[jax.ShapeDtypeStruct((m, 2 * MEM_W), BF16),
                                           jax.ShapeDtypeStruct((16, 128), F32)], name=name)(mkv, sp, dmk, dmv)


def _mem_fwd(mq, mk, mv, name):
    t = mq.shape[0]
    m = mk.shape[0]
    tq = min(t, 512)

    def body(q_ref, k_ref, v_ref, o_ref, lse_ref):
        lses = []
        for h in range(MEM_H):
            s = _dot(q_ref[:, _hs(h)], k_ref[:, _hs(h)], "nt") * SCALE
            mx = jnp.max(s, axis=-1, keepdims=True)
            p = jnp.exp(s - mx)
            l = jnp.sum(p, axis=-1, keepdims=True)
            o_ref[:, _hs(h)] = _dot(p, v_ref[:, _hs(h)]) / l
            lses.append(mx + jnp.log(l))
        lse_ref[...] = _head_column(lses)

    qspec = pl.BlockSpec((tq, MEM_W), lambda i: (i, 0))
    kspec = pl.BlockSpec((m, MEM_W), lambda i: (0, 0))
    return pl.pallas_call(
        body, grid=(t // tq,), in_specs=[qspec, kspec, kspec],
        out_specs=[qspec, pl.BlockSpec((tq, HEAD), lambda i: (i, 0))],
        out_shape=[jax.ShapeDtypeStruct((t, MEM_W), F32), jax.ShapeDtypeStruct((t, HEAD), F32)],
        compiler_params=_params(1), name=name)(mq, mk, mv)


def _mem_bwd(mq, mk, mv, dmixed, lse, delta, name):
    t = mq.shape[0]
    m = mk.shape[0]
    tq = min(t, 512)
    do_block = (FOX_W + SWA_W) // MEM_W
    assert (FOX_W + SWA_W) % MEM_W == 0

    def body(q_ref, k_ref, v_ref, do_ref, lse_ref, dl_ref, dq_ref, dk_ref, dv_ref):
        @pl.when(pl.program_id(0) == 0)
        def _():
            dk_ref[...] = jnp.zeros_like(dk_ref)
            dv_ref[...] = jnp.zeros_like(dv_ref)

        for h in range(MEM_H):
            q, k, v, do = q_ref[:, _hs(h)], k_ref[:, _hs(h)], v_ref[:, _hs(h)], do_ref[:, _hs(h)]
            s = _dot(q, k, "nt") * SCALE
            p = jnp.exp(s - lse_ref[:, h:h + 1])
            col = FOX_H + SWA_H + h
            ds = p * (_dot(do, v, "nt") - dl_ref[:, col:col + 1])
            dq_ref[:, _hs(h)] = _dot(ds, k) * SCALE
            dk_ref[:, _hs(h)] += _dot(ds, q, "tn") * SCALE
            dv_ref[:, _hs(h)] += _dot(p, do, "tn")

    qspec = pl.BlockSpec((tq, MEM_W), lambda i: (i, 0))
    kspec = pl.BlockSpec((m, MEM_W), lambda i: (0, 0))
    col = pl.BlockSpec((tq, HEAD), lambda i: (i, 0))
    return pl.pallas_call(
        body, grid=(t // tq,),
        in_specs=[qspec, kspec, kspec, pl.BlockSpec((tq, MEM_W), lambda i: (i, do_block)), col, col],
        out_specs=[qspec, kspec, kspec],
        out_shape=[jax.ShapeDtypeStruct((t, MEM_W), F32), jax.ShapeDtypeStruct((m, MEM_W), F32),
                   jax.ShapeDtypeStruct((m, MEM_W), F32)],
        compiler_params=_params(1), name=name)(mq, mk, mv, dmixed, lse, delta)


def _all_gather(xs, name):
    n = len(xs)

    def body(*refs):
        x_refs, o_refs = refs[:n], refs[n:2 * n]
        send_sems, recv_sems, local_sems = refs[2 * n:]
        x, y, c = _me()
        me, sibling = (x, y, c), (x, y, 1 - c)
        x_nb, y_nb, diag = (1 - x, y, c), (x, 1 - y, c), (1 - x, 1 - y, c)
        relay_from = (x + (1 - c) * (1 - 2 * x), y + c * (1 - 2 * y), c)
        relay_to = (x + c * (1 - 2 * x), y + (1 - c) * (1 - 2 * y), c)

        def copy(a, k, block, to, src=None):
            slot = o_refs[a].at[_lin(block)]
            return pltpu.make_async_remote_copy(
                src_ref=slot if src is None else src, dst_ref=slot, send_sem=send_sems.at[a, k],
                recv_sem=recv_sems.at[a, k], device_id=to, device_id_type=MESH)

        mine = [pltpu.make_async_copy(x_refs[a], o_refs[a].at[_lin(me)], local_sems.at[a]) for a in range(n)]
        for cp in mine:
            cp.start()
        sent = []
        for a in range(n):
            sent += [copy(a, 0, me, sibling, src=x_refs[a]), copy(a, 1, me, x_nb, src=x_refs[a]),
                     copy(a, 2, me, y_nb, src=x_refs[a])]
        for cp in sent:
            cp.start()

        def pass_on(cp):
            cp.start()
            sent.append(cp)

        for a in range(n):
            copy(a, 1, x_nb, me).wait_recv()
            copy(a, 2, y_nb, me).wait_recv()
            pass_on(copy(a, 3, relay_from, relay_to))
            pass_on(copy(a, 4, x_nb, sibling))
            pass_on(copy(a, 5, y_nb, sibling))
        for a in range(n):
            copy(a, 3, diag, me).wait_recv()
            pass_on(copy(a, 6, diag, sibling))
        for a in range(n):
            copy(a, 0, sibling, me).wait_recv()
            for k, chip in ((4, (1 - x, y)), (5, (x, 1 - y)), (6, (1 - x, 1 - y))):
                copy(a, k, (*chip, 1 - c), me).wait_recv()
        for cp in sent:
            cp.wait_send()
        for cp in mine:
            cp.wait()

    return pl.pallas_call(
        body, in_specs=[ANY] * n, out_specs=[ANY] * n,
        out_shape=[jax.ShapeDtypeStruct((N_DEV,) + x.shape, x.dtype) for x in xs],
        scratch_shapes=[pltpu.SemaphoreType.DMA((n, 7)), pltpu.SemaphoreType.DMA((n, 7)),
                        pltpu.SemaphoreType.DMA((n,))],
        name=name)(*xs)


def _peers():
    x, y, c = _me()
    out = []
    for k in range(1, N_DEV):
        kx, ky, kc = (k >> 2) & 1, (k >> 1) & 1, k & 1
        out.append(((1 - x) if kx else x, (1 - y) if ky else y, (1 - c) if kc else c))
    return out


def _all_reduce_small(xs, name):
    n = len(xs)

    def body(*refs):
        x_refs, o_refs = refs[:n], refs[n:2 * n]
        bufs = refs[2 * n:3 * n]
        send_sems, recv_sems = refs[3 * n:]
        me = _lin(_me())
        peers = _peers()
        for a in range(n):
            bufs[a][me] = x_refs[a][...]
        sends = []
        for a in range(n):
            for k, peer in enumerate(peers):
                sends.append(pltpu.make_async_remote_copy(
                    src_ref=bufs[a].at[me], dst_ref=bufs[a].at[me], send_sem=send_sems.at[a, k],
                    recv_sem=recv_sems.at[a, k], device_id=peer, device_id_type=MESH))
        for cp in sends:
            cp.start()
        for a in range(n):
            for k, peer in enumerate(peers):
                pltpu.make_async_remote_copy(
                    src_ref=bufs[a].at[me], dst_ref=bufs[a].at[_lin(peer)], send_sem=send_sems.at[a, k],
                    recv_sem=recv_sems.at[a, k], device_id=peer, device_id_type=MESH).wait_recv()
        for cp in sends:
            cp.wait_send()
        for a in range(n):
            total = bufs[a][0]
            for q in range(1, N_DEV):
                total = total + bufs[a][q]
            o_refs[a][...] = total

    vmem = pl.BlockSpec(memory_space=pltpu.VMEM)
    return pl.pallas_call(
        body, in_specs=[vmem] * n, out_specs=[vmem] * n,
        out_shape=[jax.ShapeDtypeStruct(x.shape, F32) for x in xs],
        scratch_shapes=[pltpu.VMEM((N_DEV,) + x.shape, F32) for x in xs]
        + [pltpu.SemaphoreType.DMA((n, 7)), pltpu.SemaphoreType.DMA((n, 7))],
        name=name)(*xs)


def _pair_add(part, got, name):
    _, rows, cols = part.shape
    tm = _rows_tile(rows, cols * 2, budget=2 << 20)
    core = jnp.reshape(lax.axis_index("c"), (1,)).astype(jnp.int32)

    def body(c_ref, p_ref, g_ref, o_ref):
        o_ref[...] = (p_ref[...].astype(F32) + g_ref[...].astype(F32)).astype(BF16)

    spec = pl.BlockSpec((None, tm, cols), lambda q, i, c: (q, i, 0))
    grid_spec = pltpu.PrefetchScalarGridSpec(
        num_scalar_prefetch=1, grid=(4, rows // tm),
        in_specs=[pl.BlockSpec((None, tm, cols), lambda q, i, c: (2 * q + c[0], i, 0)), spec], out_specs=spec)
    return pl.pallas_call(body, grid_spec=grid_spec, out_shape=jax.ShapeDtypeStruct((4, rows, cols), BF16),
                          compiler_params=_params(2), name=name)(core, part, got)


def _adam_math(w, g, m, v):
    nm = ADAM_B1 * m + (1.0 - ADAM_B1) * g
    nv = ADAM_B2 * v + (1.0 - ADAM_B2) * (g * g)
    m_hat = nm / (1.0 - ADAM_B1 ** ADAM_STEP)
    v_hat = nv / (1.0 - ADAM_B2 ** ADAM_STEP)
    return -ADAM_LR * (m_hat / (jnp.sqrt(v_hat) + ADAM_EPS) + ADAM_WD * w), nm, nv


def _sum_chips(got, name):
    _, rows, cols = got.shape
    tm = _rows_tile(rows, cols * 2 * 4, budget=2 << 20)

    def body(r_ref, o_ref):
        o_ref[...] = ((r_ref[0].astype(F32) + r_ref[1].astype(F32)) + r_ref[2].astype(F32)) + r_ref[3].astype(F32)

    return pl.pallas_call(
        body, grid=(rows // tm,), in_specs=[pl.BlockSpec((4, tm, cols), lambda i: (0, i, 0))],
        out_specs=pl.BlockSpec((tm, cols), lambda i: (i, 0)), out_shape=jax.ShapeDtypeStruct((rows, cols), F32),
        compiler_params=_params(1), name=name)(got)


def _sum_adamw(got, col_block, w, m, v, name):
    _, rows, cols = w.shape
    tm = _rows_tile(rows, cols * 4, budget=2 << 20)

    def body(r_ref, w_ref, m_ref, v_ref, g_ref, d_ref, nm_ref, nv_ref):
        g = ((r_ref[0].astype(F32) + r_ref[1].astype(F32)) + r_ref[2].astype(F32)) + r_ref[3].astype(F32)
        g_ref[...] = g
        d_ref[...], nm_ref[...], nv_ref[...] = _adam_math(w_ref[...], g, m_ref[...], v_ref[...])

    spec = pl.BlockSpec((None, tm, cols), lambda i: (0, i, 0))
    out = jax.ShapeDtypeStruct(w.shape, F32)
    return pl.pallas_call(
        body, grid=(rows // tm,), in_specs=[pl.BlockSpec((4, tm, cols), lambda i: (0, i, col_block)), spec, spec, spec],
        out_specs=[spec] * 4, out_shape=[out] * 4, compiler_params=_params(1), name=name)(got, w, m, v)


def _adamw(w, g, m, v, name):
    rows, cols = w.shape

    def body(w_ref, g_ref, m_ref, v_ref, d_ref, nm_ref, nv_ref):
        d_ref[...], nm_ref[...], nv_ref[...] = _adam_math(w_ref[...], g_ref[...], m_ref[...], v_ref[...])

    tm = _rows_tile(rows, cols * 4, budget=2 << 20, mult=8)
    spec = pl.BlockSpec((tm, cols), lambda i: (i, 0))
    out = jax.ShapeDtypeStruct(w.shape, F32)
    return pl.pallas_call(body, grid=(rows // tm,), in_specs=[spec] * 4, out_specs=[spec] * 3,
                          out_shape=[out] * 3, compiler_params=_params(1), name=name)(w, g, m, v)


def _permute_in(w):
    logit0 = 3 * FOX_W
    pad = jnp.zeros(w.shape[:-1] + (HEAD - N_LOGIT,), w.dtype)
    return jnp.concatenate([w[..., :logit0], w[..., logit0 + N_LOGIT:], w[..., logit0:logit0 + N_LOGIT], pad], axis=-1)


def _unpermute_in(w):
    logit0 = 3 * FOX_W
    return jnp.concatenate([w[..., :logit0], w[..., C_FL:C_FL + N_LOGIT], w[..., logit0:C_FL]], axis=-1)


def _pad_row(v, width):
    return jnp.pad(v, ((0, 0), (0, width - v.shape[1])))


def _pack_small(fq, fk, sq, sk, mq, mk, fb, sinks):
    rows = [fq, fk, sq, sk, mq, mk, _pad_row(fb, HEAD), _pad_row(sinks, HEAD)]
    return jnp.concatenate(rows + [jnp.zeros((8, HEAD), F32)], axis=0)


def _pack_norms(a, b, c, d):
    return jnp.concatenate([a, b, c, d, jnp.zeros((4, a.shape[1]), F32)], axis=0)


def kernel(x, mem, ffn1_norm, ffn1_gate, ffn1_up, ffn1_down, mix_norm, mem_norm, w_in, forget_bias, w_mem_k, w_mem_v, fox_q_gain, fox_k_gain, swa_q_gain, swa_k_gain, swa_sinks, mem_q_gain, mem_k_gain, w_out, ffn2_norm, ffn2_gate, ffn2_up, ffn2_down, loss_target, m_ffn1_norm, m_ffn1_gate, m_ffn1_up, m_ffn1_down, m_mix_norm, m_mem_norm, m_w_in, m_forget_bias, m_w_mem_k, m_w_mem_v, m_fox_q_gain, m_fox_k_gain, m_swa_q_gain, m_swa_k_gain, m_swa_sinks, m_mem_q_gain, m_mem_k_gain, m_w_out, m_ffn2_norm, m_ffn2_gate, m_ffn2_up, m_ffn2_down, v_ffn1_norm, v_ffn1_gate, v_ffn1_up, v_ffn1_down, v_mix_norm, v_mem_norm, v_w_in, v_forget_bias, v_w_mem_k, v_w_mem_v, v_fox_q_gain, v_fox_k_gain, v_swa_q_gain, v_swa_k_gain, v_swa_sinks, v_mem_q_gain, v_mem_k_gain, v_w_out, v_ffn2_norm, v_ffn2_gate, v_ffn2_up, v_ffn2_down):
    x0 = x[0]
    mem0 = mem[0]
    target = loss_target[0]
    t, d = x0.shape
    d_shard = w_in.shape[1]
    m_len = mem0.shape[0]
    tm = min(t, 512)
    tk = min(t, 2048)
    tn = IN_W // 3
    tkw, tnw = min(t, 1024), IN_W // 3

    def swap(a):
        return jnp.swapaxes(a, 1, 2)

    gate1, up1, gate2, up2 = swap(ffn1_gate), swap(ffn1_up), swap(ffn2_gate), swap(ffn2_up)

    local = {
        "g1": gate1[0], "u1": up1[0], "d1": ffn1_down[0],
        "g2": gate2[0], "u2": up2[0], "d2": ffn2_down[0],
        "in": _permute_in(w_in[0]), "out": w_out[0],
        "mkv": jnp.concatenate([w_mem_k[0], w_mem_v[0]], axis=1),
    }
    shard = {k: _cast_bf16(v, f"cast_{k}") for k, v in local.items()}
    sp = _pack_small(fox_q_gain, fox_k_gain, swa_q_gain, swa_k_gain, mem_q_gain, mem_k_gain, forget_bias, swa_sinks)
    wt = {}

    wt["g1"], wt["u1"] = _all_gather([shard["g1"], shard["u1"]], "gather_ffn1_in")
    xn1 = _rms_fwd(x0, ffn1_norm, "ffn1_norm")
    half_fs = shard["g2"].shape[0] // 2
    (a1, b1, h1), ((wt["d1"], wt["in"]),) = _ffn_up(
        xn1, wt["g1"], wt["u1"], "ffn1", rides=[_ride_gather([shard["d1"], shard["in"]], 0.87)])
    tc = min(t, CONTRACT_ROWS)
    (x1, hn), ((wt["out"], wt["mkv"]),) = _ffn_down(
        x0, h1, wt["d1"], "ffn1", rides=[_ride_gather([shard["out"], shard["mkv"]], 0.6)],
        tail=_tail_norm(mix_norm, t, d, tc))
    w_in_full = wt["in"].reshape(d, IN_W)

    proj, (half,) = _mm(
        "proj", [(hn, pl.BlockSpec((tm, d), lambda n, i, k: (i, 0)),
                  w_in_full, pl.BlockSpec((d, tn), lambda n, i, k: (0, n)))],
        "nn", (3, t // tm, 1), jax.ShapeDtypeStruct((t, IN_W), F32), pl.BlockSpec((tm, tn), lambda n, i, k: (i, n)),
        rides=[_ride_gather_chips([shard["g2"]], rows=(0, half_fs))])
    w_out_full = wt["out"].reshape(d, d)
    w_mkv_full = wt["mkv"].reshape(d, 2 * MEM_W)
    fq, fk, fv, sq, sk, sv, mq, c_col = _attn_pre(proj, sp, "attn_pre")
    c_row = jnp.transpose(c_col[:, :8])
    c_rep = jnp.broadcast_to(c_row[:FOX_H, :, None], (FOX_H, t, HEAD))

    mn = _rms_fwd(mem0, mem_norm, "mem_norm")
    mkv = _mm("mem_kv", [(mn, pl.BlockSpec((m_len, d), lambda k: (0, 0)),
                          w_mkv_full, pl.BlockSpec((d, 2 * MEM_W), lambda k: (0, 0)))],
              "nn", (1,), jax.ShapeDtypeStruct((m_len, 2 * MEM_W), F32),
              pl.BlockSpec((m_len, 2 * MEM_W), lambda k: (0, 0)))
    mk, mv = _mem_pre(mkv, sp, "mem_pre")

    (o_a, lse_a), (half, half_u2) = _fox_fwd(
        fq, fk, fv, c_rep, c_row, "fox_fwd",
        rides=[_ride_gather_chips([shard["g2"]], rows=(half_fs, half_fs), into=half),
               _ride_gather_chips([shard["u2"]], rows=(0, half_fs))])
    (o_b, lse_b), ((wt["g2"],), half_u2) = _swa_fwd(
        sq, sk, sv, sp, "swa_fwd",
        rides=[_ride_gather_sibling(half), _ride_gather_chips([shard["u2"]], rows=(half_fs, half_fs), into=half_u2)])
    o_c, lse_c = _mem_fwd(mq, mk, mv, "mem_fwd")

    def rows_spec(width):
        return pl.BlockSpec((tm, width), lambda i, k: (i, 0))

    def wout_rows(first, width):
        assert first % width == 0
        return pl.BlockSpec((width, d), lambda i, k: (first // width, 0), pipeline_mode=pl.Buffered(1))

    xspec = pl.BlockSpec((tm, d), lambda i, k: (i, 0))
    (x2, xn2), ((wt["u2"],),) = _mm(
        "mix_out",
        [(o_a, rows_spec(FOX_W), w_out_full, wout_rows(0, FOX_W)),
         (o_b, rows_spec(SWA_W), w_out_full, wout_rows(FOX_W, SWA_W)),
         (o_c, rows_spec(MEM_W), w_out_full, wout_rows(FOX_W + SWA_W, MEM_W))],
        "nn", (t // tm, 1), jax.ShapeDtypeStruct((t, d), F32), xspec, res=x1, res_spec=xspec,
        rides=[_ride_gather_sibling(half_u2)], tail=_tail_norm(ffn2_norm, t, d, tm))

    (a2, b2, h2), ((wt["d2"],),) = _ffn_up(xn2, wt["g2"], wt["u2"], "ffn2", rides=[_ride_gather([shard["d2"]], 0.75)])
    dy, dyb, sq_err = _ffn_down(x2, h2, wt["d2"], "ffn2", tail=_tail_loss(target, t, d, tc))
    loss = lax.psum(0.5 * sq_err[0, 0] / d, ("x", "y", "c"))

    got = {}
    paired = {}
    landed = {}

    def pair(k, part):
        paired[k] = _pair_add(part, got[k], f"pair_{k}")

    (dg2, du2), _ = _ffn_dact(dyb, wt["d2"], a2, b2, "ffn2")
    part_d2 = _ffn_dw(h2, dyb, 0.5, "ffn2_dwd")
    part_g2, ((got["d2"],),) = _ffn_dw(dg2, xn2, 1.0, "ffn2_dwg", rides=[_ride_scatter_sibling([part_d2])])
    pair("d2", part_d2)
    half_rows = part_d2.shape[1] // 2
    first, second = (0, half_rows), (half_rows, half_rows)
    part_u2, (half, (got["g2"],)) = _ffn_dw(
        du2, xn2, 1.0, "ffn2_dwu",
        rides=[_ride_scatter_chips([paired["d2"]], first), _ride_scatter_sibling([part_g2])])
    pair("g2", part_g2)
    dxn2, ((landed["d2"],),) = _ffn_contract(
        dg2, wt["g2"], "ffn2_dxn_g", rides=[_ride_scatter_chips([paired["d2"]], second, into=half)])
    (dx2, dx2b, dgain_ffn2), (half_g2, (got["u2"],)) = _ffn_contract(
        du2, wt["u2"], "ffn2_dxn_u", res=dxn2,
        rides=[_ride_scatter_chips([paired["g2"]], first), _ride_scatter_sibling([part_u2])],
        tail=_tail_norm_bwd(x2, ffn2_norm, dy, t, d, tc))
    pair("u2", part_u2)

    dmixed = _mm("mix_out_dx", [(dx2b, xspec, w_out_full, pl.BlockSpec((d, d), lambda i, k: (0, 0)))],
                 "nt", (t // tm, 1), jax.ShapeDtypeStruct((t, d), F32), xspec)

    def k_rows(width):
        return pl.BlockSpec((tk, width), lambda j, k: (k, 0))

    part_out = [
        _mm(f"mix_out_dw{i}", [(o, k_rows(width), dx2b, k_rows(d))], "tn", (1, t // tk),
            jax.ShapeDtypeStruct((width, d), BF16), pl.BlockSpec((width, d), lambda j, k: (0, 0)))
        for i, (o, width) in enumerate(((o_a, FOX_W), (o_b, SWA_W), (o_c, MEM_W)))
    ]
    part_out = jnp.concatenate(part_out, axis=0).reshape(N_DEV, d_shard, d)

    delta, delta_rep = _delta(dmixed, o_a, o_b, o_c, "attn_delta")
    (dfq, dfk, dfv, dc_col, dc_row), ((landed["g2"],), (landed["u2"],)) = _fox_bwd(
        fq, fk, fv, c_rep, c_row, dmixed, lse_a, delta_rep, "fox_bwd",
        rides=[_ride_scatter_chips([paired["g2"]], second, into=half_g2), _ride_scatter_chips([paired["u2"]])])
    dsq, dsk, dsv, dsp_sink = _swa_bwd(sq, sk, sv, sp, dmixed, lse_b, delta, "swa_bwd")
    dmq, dmk, dmv = _mem_bwd(mq, mk, mv, dmixed, lse_c, delta, "mem_bwd")

    dmkv, dsp_mem = _mem_post_bwd(mkv, sp, dmk, dmv, "mem_post_bwd")
    part_mkv = _mm("mem_kv_dw", [(mn, pl.BlockSpec((m_len, d), lambda k: (0, 0)),
                                  dmkv, pl.BlockSpec((m_len, 2 * MEM_W), lambda k: (0, 0)))],
                   "tn", (1,), jax.ShapeDtypeStruct((d, 2 * MEM_W), BF16),
                   pl.BlockSpec((d, 2 * MEM_W), lambda k: (0, 0))).reshape(N_DEV, d_shard, 2 * MEM_W)
    dmn = _mm("mem_kv_dx", [(dmkv, pl.BlockSpec((m_len, 2 * MEM_W), lambda k: (0, 0)),
                             w_mkv_full, pl.BlockSpec((d, 2 * MEM_W), lambda k: (0, 0)))],
              "nt", (1,), jax.ShapeDtypeStruct((m_len, d), F32), pl.BlockSpec((m_len, d), lambda k: (0, 0)))
    _, _, dgain_mem = _rms_bwd(mem0, mem_norm, dmn, None, "mem_norm_bwd")

    dc_row_t = _pad_row(jnp.transpose(dc_row), HEAD)
    dproj, dsp_attn = _attn_post_bwd(proj, sp, dfq, dfk, dfv, dsq, dsk, dsv, dmq, dc_col, dc_row_t, "attn_post_bwd")
    (dx1, dx1b, dgain_mix), ((got["out"], got["mkv"]),) = _mm(
        "proj_dx", [(dproj, pl.BlockSpec((tc, IN_W), lambda i, k: (i, 0)),
                     w_in_full, pl.BlockSpec((d, IN_W), lambda i, k: (0, 0), pipeline_mode=pl.Buffered(1)))],
        "nt", (t // tc, 1), jax.ShapeDtypeStruct((t, d), F32), pl.BlockSpec((tc, d), lambda i, k: (i, 0)),
        rides=[_ride_scatter_sibling([part_out, part_mkv])], tail=_tail_norm_bwd(x1, mix_norm, dx2, t, d, tc))
    pair("out", part_out)
    pair("mkv", part_mkv)
    part_in, ((landed["out"], landed["mkv"]),) = _mm(
        "proj_dw", [(hn, pl.BlockSpec((tkw, d), lambda n, k: (k, 0)),
                     dproj, pl.BlockSpec((tkw, tnw), lambda n, k: (k, n)))],
        "tn", (IN_W // tnw, t // tkw), jax.ShapeDtypeStruct((d, IN_W), BF16), pl.BlockSpec((d, tnw), lambda n, k: (0, n)),
        rides=[_ride_scatter_chips([paired["out"], paired["mkv"]])])
    part_in = part_in.reshape(N_DEV, d_shard, IN_W)

    part_d1, ((got["in"],),) = _ffn_dw(h1, dx1b, 0.5, "ffn1_dwd", rides=[_ride_scatter_sibling([part_in])])
    pair("in", part_in)
    (dg1, du1), ((landed["in"],), (got["d1"],)) = _ffn_dact(
        dx1b, wt["d1"], a1, b1, "ffn1",
        rides=[_ride_scatter_chips([paired["in"]]), _ride_scatter_sibling([part_d1])])
    pair("d1", part_d1)
    part_g1, (half_d1,) = _ffn_dw(dg1, xn1, 1.0, "ffn1_dwg", rides=[_ride_scatter_chips([paired["d1"]], first)])
    part_u1, ((landed["d1"],), (got["g1"],)) = _ffn_dw(
        du1, xn1, 1.0, "ffn1_dwu",
        rides=[_ride_scatter_chips([paired["d1"]], second, into=half_d1), _ride_scatter_sibling([part_g1])])
    pair("g1", part_g1)
    dxn1, ((landed["g1"],), (got["u1"],)) = _ffn_contract(
        dg1, wt["g1"], "ffn1_dxn_g", rides=[_ride_scatter_chips([paired["g1"]]), _ride_scatter_sibling([part_u1])])
    pair("u1", part_u1)
    (grad_x, _, dgain_ffn1), ((landed["u1"],),) = _ffn_contract(
        du1, wt["u1"], "ffn1_dxn_u", res=dxn1, rides=[_ride_scatter_chips([paired["u1"]])],
        tail=_tail_norm_bwd(x0, ffn1_norm, dx1, t, d, tc))

    norms_sum, small_sum = _all_reduce_small(
        [_pack_norms(dgain_ffn1, dgain_mix, dgain_mem, dgain_ffn2), dsp_attn + dsp_sink + dsp_mem], "reduce_small")

    result = {
        "ffn1_gate": map(swap, _sum_adamw(landed["g1"], 0, gate1, swap(m_ffn1_gate), swap(v_ffn1_gate), "adamw_ffn1_gate")),
        "ffn1_up": map(swap, _sum_adamw(landed["u1"], 0, up1, swap(m_ffn1_up), swap(v_ffn1_up), "adamw_ffn1_up")),
        "ffn1_down": _sum_adamw(landed["d1"], 0, ffn1_down, m_ffn1_down, v_ffn1_down, "adamw_ffn1_down"),
        "w_mem_k": _sum_adamw(landed["mkv"], 0, w_mem_k, m_w_mem_k, v_w_mem_k, "adamw_w_mem_k"),
        "w_mem_v": _sum_adamw(landed["mkv"], 1, w_mem_v, m_w_mem_v, v_w_mem_v, "adamw_w_mem_v"),
        "w_out": _sum_adamw(landed["out"], 0, w_out, m_w_out, v_w_out, "adamw_w_out"),
        "ffn2_gate": map(swap, _sum_adamw(landed["g2"], 0, gate2, swap(m_ffn2_gate), swap(v_ffn2_gate), "adamw_ffn2_gate")),
        "ffn2_up": map(swap, _sum_adamw(landed["u2"], 0, up2, swap(m_ffn2_up), swap(v_ffn2_up), "adamw_ffn2_up")),
        "ffn2_down": _sum_adamw(landed["d2"], 0, ffn2_down, m_ffn2_down, v_ffn2_down, "adamw_ffn2_down"),
    }
    grad_in = _unpermute_in(_sum_chips(landed["in"], "sum_w_in"))
    result["w_in"] = (grad_in[None],) + tuple(
        o[None] for o in _adamw(w_in[0], grad_in, m_w_in[0], v_w_in[0], "adamw_w_in"))

    norm_names = ["ffn1_norm", "mix_norm", "mem_norm", "ffn2_norm"]
    norm_w = _pack_norms(ffn1_norm, mix_norm, mem_norm, ffn2_norm)
    norm_m = _pack_norms(m_ffn1_norm, m_mix_norm, m_mem_norm, m_ffn2_norm)
    norm_v = _pack_norms(v_ffn1_norm, v_mix_norm, v_mem_norm, v_ffn2_norm)
    outs = (norms_sum,) + tuple(_adamw(norm_w, norms_sum, norm_m, norm_v, "adamw_norms"))
    for i, k in enumerate(norm_names):
        result[k] = tuple(o[i:i + 1] for o in outs)

    small_names = ["fox_q_gain", "fox_k_gain", "swa_q_gain", "swa_k_gain", "mem_q_gain", "mem_k_gain",
                   "forget_bias", "swa_sinks"]
    small_m = _pack_small(m_fox_q_gain, m_fox_k_gain, m_swa_q_gain, m_swa_k_gain, m_mem_q_gain, m_mem_k_gain,
                          m_forget_bias, m_swa_sinks)
    small_v = _pack_small(v_fox_q_gain, v_fox_k_gain, v_swa_q_gain, v_swa_k_gain, v_mem_q_gain, v_mem_k_gain,
                          v_forget_bias, v_swa_sinks)
    outs = (small_sum,) + tuple(_adamw(sp, small_sum, small_m, small_v, "adamw_small"))
    for i, k in enumerate(small_names):
        width = N_LOGIT if k in ("forget_bias", "swa_sinks") else HEAD
        result[k] = tuple(o[i:i + 1, :width] for o in outs)

    order = ["ffn1_norm", "ffn1_gate", "ffn1_up", "ffn1_down", "mix_norm", "mem_norm", "w_in", "forget_bias",
             "w_mem_k", "w_mem_v", "fox_q_gain", "fox_k_gain", "swa_q_gain", "swa_k_gain", "swa_sinks",
             "mem_q_gain", "mem_k_gain", "w_out", "ffn2_norm", "ffn2_gate", "ffn2_up", "ffn2_down"]
    result = {k: tuple(v) for k, v in result.items()}
    flat = [loss, grad_x[None]]
    for kind in range(4):
        flat += [result[k][kind] for k in order]
    return tuple(flat)
```

```python
import functools

import jax
import jax.numpy as jnp
from jax import lax
from jax.experimental import pallas as pl
from jax.experimental.pallas import tpu as pltpu

F32 = jnp.float32
BF16 = jnp.bfloat16
MESH = pl.DeviceIdType.MESH
ANY = pl.BlockSpec(memory_space=pl.ANY)

N_DEV = 8
EPS = 1e-6
NEG_INF = -1e30
HEAD = 128
FOX_H, SWA_H, SWA_KV, MEM_H = 6, 6, 2, 4
FOX_W, SWA_W, SWA_KV_W, MEM_W = FOX_H * HEAD, SWA_H * HEAD, SWA_KV * HEAD, MEM_H * HEAD
SCALE = HEAD ** -0.5
SWA_BLOCK = 128
C_FQ, C_FK, C_FV = 0, FOX_W, 2 * FOX_W
C_SQ = 3 * FOX_W
C_SK = C_SQ + SWA_W
C_SV = C_SK + SWA_KV_W
C_MQ = C_SV + SWA_KV_W
C_FL = C_MQ + MEM_W
IN_W = C_FL + HEAD
N_LOGIT = FOX_H
R_FQ, R_FK, R_SQ, R_SK, R_MQ, R_MK, R_FB, R_SINK = range(8)
ADAM_LR, ADAM_B1, ADAM_B2, ADAM_EPS, ADAM_WD, ADAM_STEP = 0.001, 0.9, 0.999, 1e-08, 0.01, 10
VMEM_BYTES = 56 * 1024 * 1024

DN = {
    "nn": (((1,), (0,)), ((), ())),
    "nt": (((1,), (1,)), ((), ())),
    "tn": (((0,), (0,)), ((), ())),
}


def _params(n_axes):
    return pltpu.CompilerParams(dimension_semantics=("arbitrary",) * n_axes, vmem_limit_bytes=VMEM_BYTES)


def _dot(a, b, dims="nn"):
    return lax.dot_general(a.astype(BF16), b.astype(BF16), DN[dims], preferred_element_type=F32)


def _sigmoid(x):
    return 0.5 * jnp.tanh(0.5 * x) + 0.5


def _me():
    return lax.axis_index("x"), lax.axis_index("y"), lax.axis_index("c")


def _lin(p):
    return 4 * p[0] + 2 * p[1] + p[2]


def _rows_tile(rows, row_bytes, budget=4 << 20, mult=16):
    best = None
    for k in range(1, rows + 1):
        if rows % k == 0 and (rows // k) % mult == 0 and (rows // k) * row_bytes <= budget:
            best = rows // k
            break
    assert best is not None, (rows, row_bytes)
    return best


class _Ride:
    def __init__(self, inputs, out_shapes, aliases, n_remote, n_local, start, wait):
        self.inputs, self.out_shapes, self.aliases = list(inputs), list(out_shapes), dict(aliases)
        self.n_remote, self.n_local, self.start, self.wait = n_remote, n_local, start, wait


def _remote(src, dst, send, recv, k, to):
    return pltpu.make_async_remote_copy(src_ref=src, dst_ref=dst, send_sem=send.at[k], recv_sem=recv.at[k],
                                        device_id=to, device_id_type=MESH)


def _other_chips(x, y):
    return [(1 - x, y), (x, 1 - y), (1 - x, 1 - y)]


ALL_CHIPS = [(0, 0), (0, 1), (1, 0), (1, 1)]


def _rows_of(ref, rows, slot=None):
    if slot is None:
        return ref if rows is None else ref.at[pl.ds(rows[0], rows[1])]
    return ref.at[slot] if rows is None else ref.at[slot, pl.ds(rows[0], rows[1])]


def _ride_gather_chips(xs, rows=None, into=None):
    n = len(xs)

    def copies(ins, outs, send, recv):
        x, y, c = _me()
        out = []
        for a in range(n):
            for j, chip in enumerate(_other_chips(x, y)):
                peer = (*chip, c)
                src = _rows_of(ins[a], rows)
                out.append((_remote(src, _rows_of(outs[a], rows, _lin((x, y, c))), send, recv, 3 * a + j, peer),
                            _remote(src, _rows_of(outs[a], rows, _lin(peer)), send, recv, 3 * a + j, peer)))
        return out

    def mine(ins, outs, local):
        me = _lin(_me())
        return [pltpu.make_async_copy(_rows_of(ins[a], rows), _rows_of(outs[a], rows, me), local.at[a])
                for a in range(n)]

    def start(ins, outs, send, recv, local):
        for cp in mine(ins, outs, local):
            cp.start()
        for sent, _ in copies(ins, outs, send, recv):
            sent.start()

    def wait(ins, outs, send, recv, local):
        for sent, landed in copies(ins, outs, send, recv):
            landed.wait_recv()
            sent.wait_send()
        for cp in mine(ins, outs, local):
            cp.wait()

    shapes = [jax.ShapeDtypeStruct((N_DEV,) + x.shape, x.dtype) for x in xs]
    if into is None:
        return _Ride(xs, shapes, {}, 3 * n, n, start, wait)
    return _Ride(list(xs) + list(into), shapes, {n + a: a for a in range(n)}, 3 * n, n, start, wait)


def _ride_gather_sibling(bufs):
    n = len(bufs)

    def copies(outs, send, recv):
        x, y, c = _me()
        out = []
        for a in range(n):
            for q, (px, py) in enumerate(ALL_CHIPS):
                there = outs[a].at[4 * px + 2 * py + c]
                here = outs[a].at[4 * px + 2 * py + 1 - c]
                out.append((_remote(there, there, send, recv, 4 * a + q, (x, y, 1 - c)),
                            _remote(here, here, send, recv, 4 * a + q, (x, y, 1 - c))))
        return out

    def start(ins, outs, send, recv, local):
        for sent, _ in copies(outs, send, recv):
            sent.start()

    def wait(ins, outs, send, recv, local):
        for sent, landed in copies(outs, send, recv):
            landed.wait_recv()
            sent.wait_send()

    shapes = [jax.ShapeDtypeStruct(b.shape, b.dtype) for b in bufs]
    return _Ride(bufs, shapes, {a: a for a in range(n)}, 4 * n, 0, start, wait)


def _ride_gather(xs, mid_frac, rows=None, into=None):
    n = len(xs)
    chips = _ride_gather_chips(xs, rows, into)

    def sibling_copies(outs, send, recv):
        x, y, c = _me()
        out = []
        for a in range(n):
            for q, (px, py) in enumerate(ALL_CHIPS):
                there = _rows_of(outs[a], rows, 4 * px + 2 * py + c)
                here = _rows_of(outs[a], rows, 4 * px + 2 * py + 1 - c)
                k = 3 * n + 4 * a + q
                out.append((_remote(there, there, send, recv, k, (x, y, 1 - c)),
                            _remote(here, here, send, recv, k, (x, y, 1 - c))))
        return out

    def mid(ins, outs, send, recv, local):
        chips.wait(ins, outs, send, recv, local)
        for sent, _ in sibling_copies(outs, send, recv):
            sent.start()

    def wait(ins, outs, send, recv, local):
        for sent, landed in sibling_copies(outs, send, recv):
            landed.wait_recv()
            sent.wait_send()

    ride = _Ride(chips.inputs, chips.out_shapes, chips.aliases, 7 * n, n, chips.start, wait)
    ride.mid, ride.mid_frac = mid, mid_frac
    return ride


def _ride_scatter_sibling(parts):
    n = len(parts)

    def copies(ins, outs, send, recv):
        x, y, c = _me()
        out = []
        for a in range(n):
            for q, (px, py) in enumerate(ALL_CHIPS):
                cp = _remote(ins[a].at[4 * px + 2 * py + 1 - c], outs[a].at[q], send, recv, 4 * a + q, (x, y, 1 - c))
                out.append(cp)
        return out

    def start(ins, outs, send, recv, local):
        for cp in copies(ins, outs, send, recv):
            cp.start()

    def wait(ins, outs, send, recv, local):
        for cp in copies(ins, outs, send, recv):
            cp.wait_recv()
            cp.wait_send()

    shapes = [jax.ShapeDtypeStruct((4,) + p.shape[1:], p.dtype) for p in parts]
    return _Ride(parts, shapes, {}, 4 * n, 0, start, wait)


def _ride_scatter_chips(pairs, rows=None, into=None):
    n = len(pairs)

    def part(ref, slot):
        return _rows_of(ref, rows, slot)

    def copies(ins, outs, send, recv):
        x, y, c = _me()
        out = []
        for a in range(n):
            for j, (px, py) in enumerate(_other_chips(x, y)):
                peer = (px, py, c)
                src = part(ins[a], 2 * px + py)
                out.append((_remote(src, part(outs[a], 2 * x + y), send, recv, 3 * a + j, peer),
                            _remote(src, part(outs[a], 2 * px + py), send, recv, 3 * a + j, peer)))
        return out

    def mine(ins, outs, local):
        x, y, _ = _me()
        return [pltpu.make_async_copy(part(ins[a], 2 * x + y), part(outs[a], 2 * x + y), local.at[a])
                for a in range(n)]

    def start(ins, outs, send, recv, local):
        for cp in mine(ins, outs, local):
            cp.start()
        for sent, _ in copies(ins, outs, send, recv):
            sent.start()

    def wait(ins, outs, send, recv, local):
        for sent, landed in copies(ins, outs, send, recv):
            landed.wait_recv()
            sent.wait_send()
        for cp in mine(ins, outs, local):
            cp.wait()

    shapes = [jax.ShapeDtypeStruct(p.shape, p.dtype) for p in pairs]
    if into is None:
        return _Ride(pairs, shapes, {}, 3 * n, n, start, wait)
    return _Ride(list(pairs) + list(into), shapes, {n + a: a for a in range(n)}, 3 * n, n, start, wait)


def _call(name, body, grid, in_specs, out_specs, out_shape, operands, scratch=(), rides=()):
    n_in, n_out, n_scr = len(operands), len(out_shape), len(scratch)
    ride_in, ride_out, ride_scr, aliases, spans = [], [], [], {}, []
    for r in rides:
        for i, o in r.aliases.items():
            aliases[n_in + len(ride_in) + i] = n_out + len(ride_out) + o
        spans.append((len(ride_in), len(r.inputs), len(ride_out), len(r.out_shapes)))
        ride_in += r.inputs
        ride_out += r.out_shapes
        ride_scr += [pltpu.SemaphoreType.DMA((r.n_remote,)), pltpu.SemaphoreType.DMA((r.n_remote,)),
                     pltpu.SemaphoreType.DMA((max(r.n_local, 1),))]

    def wrapped(*refs):
        c_in, r_in = refs[:n_in], refs[n_in:n_in + len(ride_in)]
        p = n_in + len(ride_in)
        c_out, r_out = refs[p:p + n_out], refs[p + n_out:p + n_out + len(ride_out)]
        p += n_out + len(ride_out)
        c_scr, r_scr = refs[p:p + n_scr], refs[p + n_scr:]

        n_steps = functools.reduce(lambda a, b: a * b, grid, 1)
        step = functools.reduce(lambda acc, ax: acc * grid[ax] + pl.program_id(ax), range(len(grid)), 0)

        def each(method, at):
            for k, (r, (i0, ni, o0, no)) in enumerate(zip(rides, spans)):
                fn = getattr(r, method, None)
                if fn is None:
                    continue
                run = functools.partial(fn, r_in[i0:i0 + ni], r_out[o0:o0 + no], *r_scr[3 * k:3 * k + 3])
                if grid:
                    pl.when(step == at(r))(run)
                else:
                    run()

        each("start", lambda r: 0)
        each("mid", lambda r: int(r.mid_frac * (n_steps - 1)))
        body(*c_in, *c_out, *c_scr)
        each("wait", lambda r: n_steps - 1)

    outs = pl.pallas_call(
        wrapped, grid=grid, in_specs=list(in_specs) + [ANY] * len(ride_in),
        out_specs=list(out_specs) + [ANY] * len(ride_out), out_shape=list(out_shape) + ride_out,
        scratch_shapes=list(scratch) + ride_scr, input_output_aliases=aliases,
        compiler_params=_params(len(grid)), name=name)(*operands, *ride_in)
    outs = list(outs)
    ride_results = [outs[n_out + o0:n_out + o0 + no] for (_, _, o0, no) in spans]
    return outs[:n_out], ride_results


class _Tail:
    def __init__(self, extra, out_shapes, out_specs, fn):
        self.extra, self.out_shapes, self.out_specs, self.fn = list(extra), list(out_shapes), list(out_specs), fn


def _mm(name, pairs, dims, grid, out_shape, out_spec, res=None, res_spec=None, alpha=1.0, rides=(), tail=None):
    n = len(pairs)
    nk = grid[-1]
    kax = len(grid) - 1
    acc_shape = tuple(d for d in out_spec.block_shape if d is not None)
    n_extra = len(tail.extra) if tail else 0
    n_outs = len(tail.out_shapes) if tail else 1

    def body(*refs):
        pos = 2 * n
        r_ref = None
        if res is not None:
            r_ref = refs[pos]
            pos += 1
        x_refs = refs[pos:pos + n_extra]
        o_refs = refs[pos + n_extra:pos + n_extra + n_outs]
        pos += n_extra + n_outs
        part = None
        for p in range(n):
            d = _dot(refs[2 * p][...], refs[2 * p + 1][...], dims)
            part = d if part is None else part + d

        def finish(acc):
            if alpha != 1.0:
                acc = acc * alpha
            if r_ref is not None:
                acc = r_ref[...] + acc
            if tail:
                tail.fn(acc, x_refs, o_refs)
            else:
                o_refs[0][...] = acc.astype(o_refs[0].dtype)

        if nk == 1:
            finish(part)
        else:
            acc_ref = refs[pos]
            k = pl.program_id(kax)

            @pl.when(k == 0)
            def _():
                acc_ref[...] = part

            @pl.when(k > 0)
            def _():
                acc_ref[...] += part

            @pl.when(k == nk - 1)
            def _():
                finish(acc_ref[...])

    operands, in_specs = [], []
    for a, a_spec, b, b_spec in pairs:
        operands += [a, b]
        in_specs += [a_spec, b_spec]
    if res is not None:
        operands.append(res)
        in_specs.append(res_spec)
    for a, a_spec in (tail.extra if tail else []):
        operands.append(a)
        in_specs.append(a_spec)
    outs, ride_results = _call(name, body, grid, in_specs, tail.out_specs if tail else [out_spec],
                               tail.out_shapes if tail else [out_shape], operands,
                               scratch=[pltpu.VMEM(acc_shape, F32)] if nk > 1 else [], rides=rides)
    outs = outs if tail else outs[0]
    return (outs, ride_results) if rides else outs


def _accumulate(ref, part):
    @pl.when(pl.program_id(0) == 0)
    def _():
        ref[...] = part

    @pl.when(pl.program_id(0) > 0)
    def _():
        ref[...] += part


def _tail_norm(gain, t, d, tm):
    def fn(v, x_refs, o_refs):
        o_refs[0][...] = v
        r = lax.rsqrt(jnp.mean(v * v, axis=-1, keepdims=True) + EPS)
        o_refs[1][...] = (v * r * x_refs[0][...]).astype(BF16)

    rows = pl.BlockSpec((tm, d), lambda i, k: (i, 0))
    return _Tail([(gain, pl.BlockSpec((1, d), lambda i, k: (0, 0)))],
                 [jax.ShapeDtypeStruct((t, d), F32), jax.ShapeDtypeStruct((t, d), BF16)], [rows, rows], fn)


def _tail_loss(target, t, d, tm):
    def fn(v, x_refs, o_refs):
        err = v - x_refs[0][...]
        dy = err * (1.0 / d)
        o_refs[0][...] = dy
        o_refs[1][...] = dy.astype(BF16)
        _accumulate(o_refs[2], jnp.zeros((8, 128), F32) + jnp.sum(err * err))

    rows = pl.BlockSpec((tm, d), lambda i, k: (i, 0))
    return _Tail([(target, rows)],
                 [jax.ShapeDtypeStruct((t, d), F32), jax.ShapeDtypeStruct((t, d), BF16),
                  jax.ShapeDtypeStruct((8, 128), F32)],
                 [rows, rows, pl.BlockSpec((8, 128), lambda i, k: (0, 0))], fn)


def _tail_norm_bwd(x, gain, dres, t, d, tm):
    def fn(dy, x_refs, o_refs):
        xv = x_refs[0][...]
        r = lax.rsqrt(jnp.mean(xv * xv, axis=-1, keepdims=True) + EPS)
        xh = xv * r
        dxh = dy * x_refs[1][...]
        dx = r * (dxh - xh * jnp.mean(dxh * xh, axis=-1, keepdims=True)) + x_refs[2][...]
        o_refs[0][...] = dx
        o_refs[1][...] = dx.astype(BF16)
        _accumulate(o_refs[2], jnp.sum(dy * xh, axis=0, keepdims=True))

    rows = pl.BlockSpec((tm, d), lambda i, k: (i, 0))
    vec = pl.BlockSpec((1, d), lambda i, k: (0, 0))
    return _Tail([(x, rows), (gain, vec), (dres, rows)],
                 [jax.ShapeDtypeStruct((t, d), F32), jax.ShapeDtypeStruct((t, d), BF16),
                  jax.ShapeDtypeStruct((1, d), F32)], [rows, rows, vec], fn)


def _cast_bf16(x, name):
    rows, cols = x.shape
    tm = _rows_tile(rows, cols * 4)

    def body(x_ref, o_ref):
        o_ref[...] = x_ref[...].astype(BF16)

    spec = pl.BlockSpec((tm, cols), lambda i: (i, 0))
    return pl.pallas_call(body, grid=(rows // tm,), in_specs=[spec], out_specs=spec,
                          out_shape=jax.ShapeDtypeStruct(x.shape, BF16), compiler_params=_params(1), name=name)(x)


def _rms_fwd(x, gain, name):
    rows, d = x.shape
    tm = min(rows, 512)

    def body(x_ref, g_ref, o_ref):
        xv = x_ref[...]
        r = lax.rsqrt(jnp.mean(xv * xv, axis=-1, keepdims=True) + EPS)
        o_ref[...] = (xv * r * g_ref[...]).astype(BF16)

    spec = pl.BlockSpec((tm, d), lambda i: (i, 0))
    return pl.pallas_call(body, grid=(rows // tm,), in_specs=[spec, pl.BlockSpec((1, d), lambda i: (0, 0))],
                          out_specs=spec, out_shape=jax.ShapeDtypeStruct(x.shape, BF16),
                          compiler_params=_params(1), name=name)(x, gain)


def _rms_bwd(x, gain, dxn, dres, name, rides=()):
    rows, d = x.shape
    tm = min(rows, 256)
    with_res = dres is not None

    def body(*refs):
        if with_res:
            x_ref, g_ref, dy_ref, r_ref, dx_ref, dxb_ref, dg_ref = refs
        else:
            x_ref, g_ref, dy_ref, dx_ref, dxb_ref, dg_ref = refs
        xv = x_ref[...]
        r = lax.rsqrt(jnp.mean(xv * xv, axis=-1, keepdims=True) + EPS)
        xh = xv * r
        dy = dy_ref[...]
        dxh = dy * g_ref[...]
        dx = r * (dxh - xh * jnp.mean(dxh * xh, axis=-1, keepdims=True))
        if with_res:
            dx = dx + r_ref[...]
        dx_ref[...] = dx
        dxb_ref[...] = dx.astype(BF16)
        part = jnp.sum(dy * xh, axis=0, keepdims=True)

        @pl.when(pl.program_id(0) == 0)
        def _():
            dg_ref[...] = part

        @pl.when(pl.program_id(0) > 0)
        def _():
            dg_ref[...] += part

    spec = pl.BlockSpec((tm, d), lambda i: (i, 0))
    vec = pl.BlockSpec((1, d), lambda i: (0, 0))
    ops = [x, gain, dxn] + ([dres] if with_res else [])
    outs, ride_results = _call(
        name, body, (rows // tm,), [spec, vec, spec] + ([spec] if with_res else []), [spec, spec, vec],
        [jax.ShapeDtypeStruct(x.shape, F32), jax.ShapeDtypeStruct(x.shape, BF16), jax.ShapeDtypeStruct((1, d), F32)],
        ops, rides=rides)
    return (outs, ride_results) if rides else outs


ROW_CHUNK = 256
SHARD_ROWS = 2048


def _ffn_up(xn, wg, wu, tag, rides=()):
    t, d = xn.shape
    nd, fs, _ = wg.shape
    tm = min(t, SHARD_ROWS)
    rc = min(tm, ROW_CHUNK)

    def body(x_ref, wg_ref, wu_ref, a_ref, b_ref, h_ref):
        for r in range(0, tm, rc):
            xv = x_ref[r:r + rc, :]
            g = _dot(xv, wg_ref[...], "nt")
            u = _dot(xv, wu_ref[...], "nt")
            sig = _sigmoid(g)
            silu = g * sig
            a_ref[r:r + rc, :] = (0.5 * u * (sig + silu * (1.0 - sig))).astype(BF16)
            b_ref[r:r + rc, :] = (0.5 * silu).astype(BF16)
            h_ref[r:r + rc, :] = (silu * u).astype(BF16)

    wspec = pl.BlockSpec((None, fs, d), lambda j, i: (j, 0, 0))
    hspec = pl.BlockSpec((None, tm, fs), lambda j, i: (j, i, 0))
    hid = jax.ShapeDtypeStruct((nd, t, fs), BF16)
    return _call(f"{tag}_up", body, (nd, t // tm), [pl.BlockSpec((tm, d), lambda j, i: (i, 0)), wspec, wspec],
                 [hspec] * 3, [hid] * 3, [xn, wg, wu], rides=rides)


CONTRACT_ROWS = 256


def _ffn_contract(hid, w, name, res=None, alpha=1.0, rides=(), tail=None):
    nd, t, fs = hid.shape
    d = w.shape[2]
    tm = min(t, CONTRACT_ROWS if tail or res is not None else 2 * CONTRACT_ROWS)
    xspec = pl.BlockSpec((tm, d), lambda i, k: (i, 0))
    pairs = [(hid, pl.BlockSpec((None, tm, fs), lambda i, k, s=s: (s, i, 0)),
              w, pl.BlockSpec((None, fs, d), lambda i, k, s=s: (s, 0, 0), pipeline_mode=pl.Buffered(1)))
             for s in range(nd)]
    return _mm(name, pairs, "nn", (t // tm, 1), jax.ShapeDtypeStruct((t, d), F32), xspec,
               res=res, res_spec=xspec if res is not None else None, alpha=alpha, rides=rides, tail=tail)


def _ffn_down(x, h, wd, tag, rides=(), tail=None):
    return _ffn_contract(h, wd, f"{tag}_down", res=x, alpha=0.5, rides=rides, tail=tail)


def _ffn_dact(dyb, wd, a, b, tag, rides=()):
    nd, t, fs = a.shape
    d = dyb.shape[1]
    tm = min(t, SHARD_ROWS)
    rc = min(tm, ROW_CHUNK)

    def body(dy_ref, wd_ref, a_ref, b_ref, dg_ref, du_ref):
        for r in range(0, tm, rc):
            dh = _dot(dy_ref[r:r + rc, :], wd_ref[...], "nt")
            dg_ref[r:r + rc, :] = (dh * a_ref[r:r + rc, :].astype(F32)).astype(BF16)
            du_ref[r:r + rc, :] = (dh * b_ref[r:r + rc, :].astype(F32)).astype(BF16)

    hspec = pl.BlockSpec((None, tm, fs), lambda j, i: (j, i, 0))
    hid = jax.ShapeDtypeStruct((nd, t, fs), BF16)
    return _call(f"{tag}_dact", body, (nd, t // tm),
                 [pl.BlockSpec((tm, d), lambda j, i: (i, 0)), pl.BlockSpec((None, fs, d), lambda j, i: (j, 0, 0)),
                  hspec, hspec], [hspec] * 2, [hid] * 2, [dyb, wd, a, b], rides=rides)


def _ffn_dw(hid, act, alpha, name, rides=()):
    nd, t, fs = hid.shape
    d = act.shape[1]
    tk = t
    return _mm(name, [(hid, pl.BlockSpec((None, tk, fs), lambda j, k: (j, k, 0)),
                       act, pl.BlockSpec((tk, d), lambda j, k: (k, 0), pipeline_mode=pl.Buffered(1)))],
               "tn", (nd, t // tk),
               jax.ShapeDtypeStruct((nd, fs, d), BF16), pl.BlockSpec((None, fs, d), lambda j, k: (j, 0, 0)),
               alpha=alpha, rides=rides)


def _head_norm(x, gain):
    r = lax.rsqrt(jnp.mean(x * x, axis=-1, keepdims=True) + EPS)
    return x * r * gain


def _head_norm_bwd(x, gain, dy):
    r = lax.rsqrt(jnp.mean(x * x, axis=-1, keepdims=True) + EPS)
    xh = x * r
    dxh = dy * gain
    dx = r * (dxh - xh * jnp.mean(dxh * xh, axis=-1, keepdims=True))
    return dx, jnp.sum(dy * xh, axis=0, keepdims=True)


def _hs(h, base=0):
    return slice(base + h * HEAD, base + (h + 1) * HEAD)


def _tri(n, lower):
    r = lax.broadcasted_iota(jnp.int32, (n, n), 0)
    c = lax.broadcasted_iota(jnp.int32, (n, n), 1)
    return ((r >= c) if lower else (r <= c)).astype(F32)


def _attn_pre(proj, sp, name):
    t = proj.shape[0]
    tm = min(t, 256)

    def body(p_ref, sp_ref, fq, fk, fv, sq, sk, sv, mq, cc, carry):
        @pl.when(pl.program_id(0) == 0)
        def _():
            carry[...] = jnp.zeros_like(carry)

        for h in range(FOX_H):
            fq[:, _hs(h)] = _head_norm(p_ref[:, _hs(h, C_FQ)], sp_ref[R_FQ:R_FQ + 1, :]).astype(BF16)
            fk[:, _hs(h)] = _head_norm(p_ref[:, _hs(h, C_FK)], sp_ref[R_FK:R_FK + 1, :]).astype(BF16)
        fv[...] = p_ref[:, C_FV:C_FV + FOX_W].astype(BF16)
        for h in range(SWA_H):
            sq[:, _hs(h)] = _head_norm(p_ref[:, _hs(h, C_SQ)], sp_ref[R_SQ:R_SQ + 1, :]).astype(BF16)
        for h in range(SWA_KV):
            sk[:, _hs(h)] = _head_norm(p_ref[:, _hs(h, C_SK)], sp_ref[R_SK:R_SK + 1, :]).astype(BF16)
        sv[...] = p_ref[:, C_SV:C_SV + SWA_KV_W].astype(BF16)
        for h in range(MEM_H):
            mq[:, _hs(h)] = _head_norm(p_ref[:, _hs(h, C_MQ)], sp_ref[R_MQ:R_MQ + 1, :]).astype(BF16)
        z = p_ref[:, C_FL:C_FL + HEAD] + sp_ref[R_FB:R_FB + 1, :]
        lane = lax.broadcasted_iota(jnp.int32, z.shape, 1)
        log_f = jnp.minimum(z, 0.0) - jnp.log(1.0 + jnp.exp(-jnp.abs(z)))
        log_f = jnp.where(lane < N_LOGIT, log_f, 0.0)
        c = jnp.dot(_tri(tm, True), log_f, precision=lax.Precision.HIGHEST, preferred_element_type=F32)
        c = c + carry[0:1, :]
        cc[...] = c
        carry[...] = jnp.broadcast_to(c[tm - 1:tm, :], carry.shape)

    def rows(w):
        return pl.BlockSpec((tm, w), lambda i: (i, 0))

    def shape(w, dt):
        return jax.ShapeDtypeStruct((t, w), dt)

    widths = [FOX_W, FOX_W, FOX_W, SWA_W, SWA_KV_W, SWA_KV_W, MEM_W]
    return pl.pallas_call(
        body, grid=(t // tm,), in_specs=[rows(IN_W), pl.BlockSpec((16, 128), lambda i: (0, 0))],
        out_specs=[rows(w) for w in widths] + [rows(HEAD)],
        out_shape=[shape(w, BF16) for w in widths] + [shape(HEAD, F32)],
        scratch_shapes=[pltpu.VMEM((8, 128), F32)], compiler_params=_params(1), name=name)(proj, sp)


def _attn_post_bwd(proj, sp, dfq, dfk, dfv, dsq, dsk, dsv, dmq, dc_col, dc_row_t, name):
    t = proj.shape[0]
    tm = min(t, 256)
    nb = t // tm

    def body(p_ref, sp_ref, dfq_r, dfk_r, dfv_r, dsq_r, dsk_r, dsv_r, dmq_r, dcc_r, dcr_r, dp_ref, dsp_ref, carry):
        @pl.when(pl.program_id(0) == 0)
        def _():
            carry[...] = jnp.zeros_like(carry)
            dsp_ref[...] = jnp.zeros_like(dsp_ref)

        def group(n_heads, col, row, d_ref):
            total = None
            for h in range(n_heads):
                dx, dg = _head_norm_bwd(p_ref[:, _hs(h, col)], sp_ref[row:row + 1, :], d_ref[:, _hs(h)])
                dp_ref[:, _hs(h, col)] = dx.astype(BF16)
                total = dg if total is None else total + dg
            dsp_ref[row:row + 1, :] += total

        group(FOX_H, C_FQ, R_FQ, dfq_r)
        group(FOX_H, C_FK, R_FK, dfk_r)
        dp_ref[:, C_FV:C_FV + FOX_W] = dfv_r[...].astype(BF16)
        group(SWA_H, C_SQ, R_SQ, dsq_r)
        group(SWA_KV, C_SK, R_SK, dsk_r)
        dp_ref[:, C_SV:C_SV + SWA_KV_W] = dsv_r[...].astype(BF16)
        group(MEM_H, C_MQ, R_MQ, dmq_r)
        dc = dcc_r[...] - dcr_r[...]
        rc = jnp.dot(_tri(tm, False), dc, precision=lax.Precision.HIGHEST, preferred_element_type=F32)
        rc = rc + carry[0:1, :]
        carry[...] = jnp.broadcast_to(rc[0:1, :], carry.shape)
        z = p_ref[:, C_FL:C_FL + HEAD] + sp_ref[R_FB:R_FB + 1, :]
        dz = rc * _sigmoid(-z)
        dp_ref[:, C_FL:C_FL + HEAD] = dz.astype(BF16)
        dsp_ref[R_FB:R_FB + 1, :] += jnp.sum(dz, axis=0, keepdims=True)

    def rows(w):
        return pl.BlockSpec((tm, w), lambda i: (nb - 1 - i, 0))

    small = pl.BlockSpec((16, 128), lambda i: (0, 0))
    widths = [FOX_W, FOX_W, FOX_W, SWA_W, SWA_KV_W, SWA_KV_W, MEM_W, HEAD, HEAD]
    return pl.pallas_call(
        body, grid=(nb,), in_specs=[rows(IN_W), small] + [rows(w) for w in widths],
        out_specs=[rows(IN_W), small],
        out_shape=[jax.ShapeDtypeStruct((t, IN_W), BF16), jax.ShapeDtypeStruct((16, 128), F32)],
        scratch_shapes=[pltpu.VMEM((8, 128), F32)], compiler_params=_params(1), name=name,
    )(proj, sp, dfq, dfk, dfv, dsq, dsk, dsv, dmq, dc_col, dc_row_t)


def _head_column(values):
    rows = values[0].shape[0]
    lane = lax.broadcasted_iota(jnp.int32, (rows, HEAD), 1)
    out = jnp.zeros((rows, HEAD), F32)
    for h, v in enumerate(values):
        out = jnp.where(lane == h, v, out)
    return out


def _head_row(values, n_rows=8):
    cols = values[0].shape[1]
    sub = lax.broadcasted_iota(jnp.int32, (n_rows, cols), 0)
    out = jnp.zeros((n_rows, cols), F32)
    for h, v in enumerate(values):
        out = jnp.where(sub == h, v, out)
    return out


def _delta(dmixed, o_a, o_b, o_c, name):
    t = dmixed.shape[0]
    tm = min(t, 512)

    def body(d_ref, a_ref, b_ref, c_ref, o_ref, rep_ref):
        cols = []
        for ref, n_heads, base in ((a_ref, FOX_H, 0), (b_ref, SWA_H, FOX_W), (c_ref, MEM_H, FOX_W + SWA_W)):
            for h in range(n_heads):
                cols.append(jnp.sum(d_ref[:, _hs(h, base)] * ref[:, _hs(h)], axis=-1, keepdims=True))
        o_ref[...] = _head_column(cols)
        for h in range(FOX_H):
            rep_ref[h] = jnp.broadcast_to(cols[h], (tm, HEAD))

    def rows(w):
        return pl.BlockSpec((tm, w), lambda i: (i, 0))

    return pl.pallas_call(body, grid=(t // tm,), in_specs=[rows(dmixed.shape[1]), rows(FOX_W), rows(SWA_W), rows(MEM_W)],
                          out_specs=[rows(HEAD), pl.BlockSpec((FOX_H, tm, HEAD), lambda i: (0, i, 0))],
                          out_shape=[jax.ShapeDtypeStruct((t, HEAD), F32), jax.ShapeDtypeStruct((FOX_H, t, HEAD), F32)],
                          compiler_params=_params(1), name=name)(dmixed, o_a, o_b, o_c)


def _causal_steps(nb, rows_major):
    if nb % 2:
        if rows_major:
            return (nb, nb), lambda a, b: (a, jnp.minimum(a, b), b <= a)
        return (nb, nb), lambda a, b: (jnp.maximum(a, b), a, b >= a)

    def fold(p, k):
        if rows_major:
            low = k <= p
            return jnp.where(low, p, nb - 1 - p), jnp.where(low, k, k - p - 1), True
        low = k < nb - p
        return jnp.where(low, p + k, k - 1), jnp.where(low, p, nb - 1 - p), True

    return (nb // 2, nb + 1), fold


def _fox_fwd(fq, fk, fv, c_rep, c_row, name, rides=()):
    t = fq.shape[0]
    tb = min(t, 512)
    nb = t // tb
    n_tiles = tb // HEAD
    grid, pair_of = _causal_steps(nb, True)

    def body(q_ref, k_ref, v_ref, cc_ref, cr_ref, o_ref, lse_ref, m_s, l_s, acc_s):
        qi, ki, live = pair_of(pl.program_id(0), pl.program_id(1))

        @pl.when(live & (ki == 0))
        def _():
            m_s[...] = jnp.full_like(m_s, NEG_INF)
            l_s[...] = jnp.zeros_like(l_s)
            acc_s[...] = jnp.zeros_like(acc_s)

        def step(diagonal):
            if diagonal:
                r = lax.broadcasted_iota(jnp.int32, (tb, HEAD), 0)
                c = lax.broadcasted_iota(jnp.int32, (tb, HEAD), 1)
            for h in range(FOX_H):
                s = _dot(q_ref[:, _hs(h)], k_ref[:, _hs(h)], "nt")
                cc = cc_ref[h]
                tiles, m_cur = [], None
                for j in range(n_tiles):
                    st = s[:, _hs(j)] * SCALE + cc - cr_ref[h:h + 1, _hs(j)]
                    if diagonal:
                        st = jnp.where(r >= c + j * HEAD, st, NEG_INF)
                    tiles.append(st)
                    m_cur = st if m_cur is None else jnp.maximum(m_cur, st)
                m_prev = m_s[h]
                m_new = jnp.maximum(m_prev, jnp.max(m_cur, axis=-1, keepdims=True))
                alpha = jnp.exp(m_prev - m_new)
                ps = [jnp.exp(st - m_new) for st in tiles]
                l_cur = ps[0]
                for p in ps[1:]:
                    l_cur = l_cur + p
                l_s[h] = alpha * l_s[h] + jnp.sum(l_cur, axis=-1, keepdims=True)
                p = jnp.concatenate([p.astype(BF16) for p in ps], axis=1)
                acc_s[:, _hs(h)] = alpha * acc_s[:, _hs(h)] + _dot(p, v_ref[:, _hs(h)])
                m_s[h] = m_new

        @pl.when(live & (ki < qi))
        def _():
            step(False)

        @pl.when(live & (ki == qi))
        def _():
            step(True)
            for h in range(FOX_H):
                o_ref[:, _hs(h)] = acc_s[:, _hs(h)] / l_s[h]
                lse_ref[h] = m_s[h] + jnp.log(l_s[h])

    qspec = pl.BlockSpec((tb, FOX_W), lambda a, b: (pair_of(a, b)[0], 0))
    kspec = pl.BlockSpec((tb, FOX_W), lambda a, b: (pair_of(a, b)[1], 0))
    rep = pl.BlockSpec((FOX_H, tb, HEAD), lambda a, b: (0, pair_of(a, b)[0], 0))
    return _call(
        name, body, grid,
        [qspec, kspec, kspec, rep, pl.BlockSpec((8, tb), lambda a, b: (0, pair_of(a, b)[1]))],
        [qspec, rep],
        [jax.ShapeDtypeStruct((t, FOX_W), F32), jax.ShapeDtypeStruct((FOX_H, t, HEAD), F32)],
        [fq, fk, fv, c_rep, c_row],
        scratch=[pltpu.VMEM((FOX_H, tb, HEAD), F32), pltpu.VMEM((FOX_H, tb, HEAD), F32), pltpu.VMEM((tb, FOX_W), F32)],
        rides=rides)


def _fox_bwd(fq, fk, fv, c_rep, c_row, dmixed, lse, delta, name, rides=()):
    t = fq.shape[0]
    tb = min(t, 512)
    nb = t // tb
    n_tiles = tb // HEAD
    grid, pair_of = _causal_steps(nb, False)

    def body(q_ref, k_ref, v_ref, cc_ref, cr_ref, do_ref, lse_ref, dl_ref,
             dq_ref, dk_ref, dv_ref, dcc_ref, dcr_ref):
        qi, ki, live = pair_of(pl.program_id(0), pl.program_id(1))

        @pl.when((pl.program_id(0) == 0) & (pl.program_id(1) == 0))
        def _():
            dq_ref[...] = jnp.zeros_like(dq_ref)
            dcc_ref[...] = jnp.zeros_like(dcc_ref)

        @pl.when(live & (qi == ki))
        def _():
            dk_ref[...] = jnp.zeros_like(dk_ref)
            dv_ref[...] = jnp.zeros_like(dv_ref)
            dcr_ref[...] = jnp.zeros_like(dcr_ref)

        def step(diagonal):
            rows = pl.ds(pl.multiple_of(qi * tb, tb), tb)
            if diagonal:
                r = lax.broadcasted_iota(jnp.int32, (tb, HEAD), 0)
                c = lax.broadcasted_iota(jnp.int32, (tb, HEAD), 1)
            row_sums, col_sums = [], []
            for h in range(FOX_H):
                q, k, v, do = q_ref[:, _hs(h)], k_ref[:, _hs(h)], v_ref[:, _hs(h)], do_ref[:, _hs(h)]
                s = _dot(q, k, "nt")
                dp = _dot(do, v, "nt")
                cc, lse_h, dl_h = cc_ref[h], lse_ref[h], dl_ref[h]
                ps, dss, row = [], [], None
                for j in range(n_tiles):
                    st = s[:, _hs(j)] * SCALE + cc - cr_ref[h:h + 1, _hs(j)]
                    if diagonal:
                        st = jnp.where(r >= c + j * HEAD, st, NEG_INF)
                    pt = jnp.exp(st - lse_h)
                    dst = pt * (dp[:, _hs(j)] - dl_h)
                    ps.append(pt.astype(BF16))
                    dss.append(dst)
                    row = dst if row is None else row + dst
                p = jnp.concatenate(ps, axis=1)
                ds = jnp.concatenate(dss, axis=1)
                dsb = ds.astype(BF16)
                dv_ref[:, _hs(h)] += _dot(p, do, "tn")
                dk_ref[:, _hs(h)] += _dot(dsb, q, "tn") * SCALE
                dq_ref[rows, _hs(h)] += _dot(dsb, k) * SCALE
                row_sums.append(jnp.sum(row, axis=1, keepdims=True))
                col_sums.append(jnp.sum(ds, axis=0, keepdims=True))
            dcc_ref[rows, :] += _head_column(row_sums)
            dcr_ref[...] += _head_row(col_sums)

        @pl.when(live & (qi > ki))
        def _():
            step(False)

        @pl.when(live & (qi == ki))
        def _():
            step(True)

    qspec = pl.BlockSpec((tb, FOX_W), lambda a, b: (pair_of(a, b)[0], 0))
    kspec = pl.BlockSpec((tb, FOX_W), lambda a, b: (pair_of(a, b)[1], 0))
    rep = pl.BlockSpec((FOX_H, tb, HEAD), lambda a, b: (0, pair_of(a, b)[0], 0))
    rowspec = pl.BlockSpec((8, tb), lambda a, b: (0, pair_of(a, b)[1]))
    return _call(
        name, body, grid, [qspec, kspec, kspec, rep, rowspec, qspec, rep, rep],
        [pl.BlockSpec((t, FOX_W), lambda a, b: (0, 0)), kspec, kspec,
         pl.BlockSpec((t, HEAD), lambda a, b: (0, 0)), rowspec],
        [jax.ShapeDtypeStruct((t, FOX_W), F32)] * 3 + [jax.ShapeDtypeStruct((t, HEAD), F32),
                                                       jax.ShapeDtypeStruct((8, t), F32)],
        [fq, fk, fv, c_rep, c_row, dmixed, lse, delta], rides=rides)


def _swa_logits(q, k_cur, k_prev, slope, first_block):
    w = SWA_BLOCK
    r = lax.broadcasted_iota(jnp.int32, (w, w), 0)
    j = lax.broadcasted_iota(jnp.int32, (w, w), 1)
    dist_cur = r - j
    dist_prev = w + r - j
    s_cur = _dot(q, k_cur, "nt") * SCALE - slope * dist_cur.astype(F32)
    s_cur = jnp.where(dist_cur >= 0, s_cur, NEG_INF)
    s_prev = _dot(q, k_prev, "nt") * SCALE - slope * dist_prev.astype(F32)
    s_prev = jnp.where((j > r) & jnp.logical_not(first_block), s_prev, NEG_INF)
    return s_cur, s_prev


def _slope(h):
    return float(2.0 ** (-8.0 * (h + 1) / SWA_H))


def _swa_fwd(sq, sk, sv, sp, name, rides=()):
    t = sq.shape[0]
    w = SWA_BLOCK
    nb = t // w
    group = SWA_H // SWA_KV

    def body(q_ref, kp_ref, kc_ref, vp_ref, vc_ref, sp_ref, o_ref, lse_ref):
        first = pl.program_id(0) == 0
        lses = []
        for h in range(SWA_H):
            kv = h // group
            s_cur, s_prev = _swa_logits(q_ref[:, _hs(h)], kc_ref[:, _hs(kv)], kp_ref[:, _hs(kv)], _slope(h), first)
            sink = sp_ref[R_SINK:R_SINK + 1, h:h + 1]
            m = jnp.maximum(jnp.maximum(jnp.max(s_cur, axis=-1, keepdims=True),
                                        jnp.max(s_prev, axis=-1, keepdims=True)), sink)
            p_cur = jnp.exp(s_cur - m)
            p_prev = jnp.exp(s_prev - m)
            l = jnp.sum(p_cur, axis=-1, keepdims=True) + jnp.sum(p_prev, axis=-1, keepdims=True) + jnp.exp(sink - m)
            o_ref[:, _hs(h)] = (_dot(p_cur, vc_ref[:, _hs(kv)]) + _dot(p_prev, vp_ref[:, _hs(kv)])) / l
            lses.append(m + jnp.log(l))
        lse_ref[...] = _head_column(lses)

    qspec = pl.BlockSpec((w, SWA_W), lambda n: (n, 0))
    cur = pl.BlockSpec((w, SWA_KV_W), lambda n: (n, 0))
    prev = pl.BlockSpec((w, SWA_KV_W), lambda n: (jnp.maximum(n - 1, 0), 0))
    return _call(
        name, body, (nb,), [qspec, prev, cur, prev, cur, pl.BlockSpec((16, 128), lambda n: (0, 0))],
        [qspec, pl.BlockSpec((w, HEAD), lambda n: (n, 0))],
        [jax.ShapeDtypeStruct((t, SWA_W), F32), jax.ShapeDtypeStruct((t, HEAD), F32)],
        [sq, sk, sk, sv, sv, sp], rides=rides)


def _swa_bwd(sq, sk, sv, sp, dmixed, lse, delta, name):
    t = sq.shape[0]
    w = SWA_BLOCK
    nb = t // w
    group = SWA_H // SWA_KV
    do_block = FOX_W // SWA_W
    assert FOX_W % SWA_W == 0

    def body(q_ref, kp_ref, kc_ref, vp_ref, vc_ref, sp_ref, do_ref, lse_ref, dl_ref,
             dq_ref, dk_ref, dv_ref, dsp_ref, ck, cv):
        step = pl.program_id(0)
        first = step == nb - 1

        @pl.when(step == 0)
        def _():
            ck[...] = jnp.zeros_like(ck)
            cv[...] = jnp.zeros_like(cv)
            dsp_ref[...] = jnp.zeros_like(dsp_ref)

        dk_cur = [None] * SWA_KV
        dk_prev = [None] * SWA_KV
        dv_cur = [None] * SWA_KV
        dv_prev = [None] * SWA_KV
        dsinks = []

        def add(lst, i, v):
            lst[i] = v if lst[i] is None else lst[i] + v

        for h in range(SWA_H):
            kv = h // group
            q, do = q_ref[:, _hs(h)], do_ref[:, _hs(h)]
            kc, kp, vc, vp = kc_ref[:, _hs(kv)], kp_ref[:, _hs(kv)], vc_ref[:, _hs(kv)], vp_ref[:, _hs(kv)]
            s_cur, s_prev = _swa_logits(q, kc, kp, _slope(h), first)
            lse_h = lse_ref[:, h:h + 1]
            dl_h = dl_ref[:, FOX_H + h:FOX_H + h + 1]
            p_cur = jnp.exp(s_cur - lse_h)
            p_prev = jnp.exp(s_prev - lse_h)
            p_sink = jnp.exp(sp_ref[R_SINK:R_SINK + 1, h:h + 1] - lse_h)
            ds_cur = p_cur * (_dot(do, vc, "nt") - dl_h)
            ds_prev = p_prev * (_dot(do, vp, "nt") - dl_h)
            dq_ref[:, _hs(h)] = (_dot(ds_cur, kc) + _dot(ds_prev, kp)) * SCALE
            add(dk_cur, kv, _dot(ds_cur, q, "tn") * SCALE)
            add(dk_prev, kv, _dot(ds_prev, q, "tn") * SCALE)
            add(dv_cur, kv, _dot(p_cur, do, "tn"))
            add(dv_prev, kv, _dot(p_prev, do, "tn"))
            dsinks.append(-jnp.sum(p_sink * dl_h, axis=0, keepdims=True))
        for kv in range(SWA_KV):
            dk_ref[:, _hs(kv)] = dk_cur[kv] + ck[:, _hs(kv)]
            dv_ref[:, _hs(kv)] = dv_cur[kv] + cv[:, _hs(kv)]
            ck[:, _hs(kv)] = dk_prev[kv]
            cv[:, _hs(kv)] = dv_prev[kv]
        lane = lax.broadcasted_iota(jnp.int32, (1, HEAD), 1)
        row = jnp.zeros((1, HEAD), F32)
        for h in range(SWA_H):
            row = jnp.where(lane == h, dsinks[h], row)
        dsp_ref[R_SINK:R_SINK + 1, :] += row

    def rev(n):
        return nb - 1 - n

    qspec = pl.BlockSpec((w, SWA_W), lambda n: (rev(n), 0))
    cur = pl.BlockSpec((w, SWA_KV_W), lambda n: (rev(n), 0))
    prev = pl.BlockSpec((w, SWA_KV_W), lambda n: (jnp.maximum(rev(n) - 1, 0), 0))
    col = pl.BlockSpec((w, HEAD), lambda n: (rev(n), 0))
    small = pl.BlockSpec((16, 128), lambda n: (0, 0))
    return pl.pallas_call(
        body, grid=(nb,),
        in_specs=[qspec, prev, cur, prev, cur, small, pl.BlockSpec((w, SWA_W), lambda n: (rev(n), do_block)), col, col],
        out_specs=[qspec, cur, cur, small],
        out_shape=[jax.ShapeDtypeStruct((t, SWA_W), F32), jax.ShapeDtypeStruct((t, SWA_KV_W), F32),
                   jax.ShapeDtypeStruct((t, SWA_KV_W), F32), jax.ShapeDtypeStruct((16, 128), F32)],
        scratch_shapes=[pltpu.VMEM((w, SWA_KV_W), F32), pltpu.VMEM((w, SWA_KV_W), F32)],
        compiler_params=_params(1), name=name)(sq, sk, sk, sv, sv, sp, dmixed, lse, delta)


def _mem_pre(mkv, sp, name):
    m = mkv.shape[0]

    def body(x_ref, sp_ref, k_ref, v_ref):
        for h in range(MEM_H):
            k_ref[:, _hs(h)] = _head_norm(x_ref[:, _hs(h)], sp_ref[R_MK:R_MK + 1, :]).astype(BF16)
        v_ref[...] = x_ref[:, MEM_W:2 * MEM_W].astype(BF16)

    out = jax.ShapeDtypeStruct((m, MEM_W), BF16)
    return pl.pallas_call(body, out_shape=[out, out], name=name)(mkv, sp)


def _mem_post_bwd(mkv, sp, dmk, dmv, name):
    m = mkv.shape[0]

    def body(x_ref, sp_ref, dk_ref, dv_ref, d_ref, dsp_ref):
        dsp_ref[...] = jnp.zeros_like(dsp_ref)
        total = None
        for h in range(MEM_H):
            dx, dg = _head_norm_bwd(x_ref[:, _hs(h)], sp_ref[R_MK:R_MK + 1, :], dk_ref[:, _hs(h)])
            d_ref[:, _hs(h)] = dx.astype(BF16)
            total = dg if total is None else total + dg
        d_ref[:, MEM_W:2 * MEM_W] = dv_ref[...].astype(BF16)
        dsp_ref[R_MK:R_MK + 1, :] = total

    return pl.pallas_call(body, out_shape=[jax.ShapeDtypeStruct((m, 2 * MEM_W), BF16),
                                           jax.ShapeDtypeStruct((16, 128), F32)], name=name)(mkv, sp, dmk, dmv)


def _mem_fwd(mq, mk, mv, name):
    t = mq.shape[0]
    m = mk.shape[0]
    tq = min(t, 512)

    def body(q_ref, k_ref, v_ref, o_ref, lse_ref):
        lses = []
        for h in range(MEM_H):
            s = _dot(q_ref[:, _hs(h)], k_ref[:, _hs(h)], "nt") * SCALE
            mx = jnp.max(s, axis=-1, keepdims=True)
            p = jnp.exp(s - mx)
            l = jnp.sum(p, axis=-1, keepdims=True)
            o_ref[:, _hs(h)] = _dot(p, v_ref[:, _hs(h)]) / l
            lses.append(mx + jnp.log(l))
        lse_ref[...] = _head_column(lses)

    qspec = pl.BlockSpec((tq, MEM_W), lambda i: (i, 0))
    kspec = pl.BlockSpec((m, MEM_W), lambda i: (0, 0))
    return pl.pallas_call(
        body, grid=(t // tq,), in_specs=[qspec, kspec, kspec],
        out_specs=[qspec, pl.BlockSpec((tq, HEAD), lambda i: (i, 0))],
        out_shape=[jax.ShapeDtypeStruct((t, MEM_W), F32), jax.ShapeDtypeStruct((t, HEAD), F32)],
        compiler_params=_params(1), name=name)(mq, mk, mv)


def _mem_bwd(mq, mk, mv, dmixed, lse, delta, name):
    t = mq.shape[0]
    m = mk.shape[0]
    tq = min(t, 512)
    do_block = (FOX_W + SWA_W) // MEM_W
    assert (FOX_W + SWA_W) % MEM_W == 0

    def body(q_ref, k_ref, v_ref, do_ref, lse_ref, dl_ref, dq_ref, dk_ref, dv_ref):
        @pl.when(pl.program_id(0) == 0)
        def _():
            dk_ref[...] = jnp.zeros_like(dk_ref)
            dv_ref[...] = jnp.zeros_like(dv_ref)

        for h in range(MEM_H):
            q, k, v, do = q_ref[:, _hs(h)], k_ref[:, _hs(h)], v_ref[:, _hs(h)], do_ref[:, _hs(h)]
            s = _dot(q, k, "nt") * SCALE
            p = jnp.exp(s - lse_ref[:, h:h + 1])
            col = FOX_H + SWA_H + h
            ds = p * (_dot(do, v, "nt") - dl_ref[:, col:col + 1])
            dq_ref[:, _hs(h)] = _dot(ds, k) * SCALE
            dk_ref[:, _hs(h)] += _dot(ds, q, "tn") * SCALE
            dv_ref[:, _hs(h)] += _dot(p, do, "tn")

    qspec = pl.BlockSpec((tq, MEM_W), lambda i: (i, 0))
    kspec = pl.BlockSpec((m, MEM_W), lambda i: (0, 0))
    col = pl.BlockSpec((tq, HEAD), lambda i: (i, 0))
    return pl.pallas_call(
        body, grid=(t // tq,),
        in_specs=[qspec, kspec, kspec, pl.BlockSpec((tq, MEM_W), lambda i: (i, do_block)), col, col],
        out_specs=[qspec, kspec, kspec],
        out_shape=[jax.ShapeDtypeStruct((t, MEM_W), F32), jax.ShapeDtypeStruct((m, MEM_W), F32),
                   jax.ShapeDtypeStruct((m, MEM_W), F32)],
        compiler_params=_params(1), name=name)(mq, mk, mv, dmixed, lse, delta)


def _all_gather(xs, name):
    n = len(xs)

    def body(*refs):
        x_refs, o_refs = refs[:n], refs[n:2 * n]
        send_sems, recv_sems, local_sems = refs[2 * n:]
        x, y, c = _me()
        me, sibling = (x, y, c), (x, y, 1 - c)
        x_nb, y_nb, diag = (1 - x, y, c), (x, 1 - y, c), (1 - x, 1 - y, c)
        relay_from = (x + (1 - c) * (1 - 2 * x), y + c * (1 - 2 * y), c)
        relay_to = (x + c * (1 - 2 * x), y + (1 - c) * (1 - 2 * y), c)

        def copy(a, k, block, to, src=None):
            slot = o_refs[a].at[_lin(block)]
            return pltpu.make_async_remote_copy(
                src_ref=slot if src is None else src, dst_ref=slot, send_sem=send_sems.at[a, k],
                recv_sem=recv_sems.at[a, k], device_id=to, device_id_type=MESH)

        mine = [pltpu.make_async_copy(x_refs[a], o_refs[a].at[_lin(me)], local_sems.at[a]) for a in range(n)]
        for cp in mine:
            cp.start()
        sent = []
        for a in range(n):
            sent += [copy(a, 0, me, sibling, src=x_refs[a]), copy(a, 1, me, x_nb, src=x_refs[a]),
                     copy(a, 2, me, y_nb, src=x_refs[a])]
        for cp in sent:
            cp.start()

        def pass_on(cp):
            cp.start()
            sent.append(cp)

        for a in range(n):
            copy(a, 1, x_nb, me).wait_recv()
            copy(a, 2, y_nb, me).wait_recv()
            pass_on(copy(a, 3, relay_from, relay_to))
            pass_on(copy(a, 4, x_nb, sibling))
            pass_on(copy(a, 5, y_nb, sibling))
        for a in range(n):
            copy(a, 3, diag, me).wait_recv()
            pass_on(copy(a, 6, diag, sibling))
        for a in range(n):
            copy(a, 0, sibling, me).wait_recv()
            for k, chip in ((4, (1 - x, y)), (5, (x, 1 - y)), (6, (1 - x, 1 - y))):
                copy(a, k, (*chip, 1 - c), me).wait_recv()
        for cp in sent:
            cp.wait_send()
        for cp in mine:
            cp.wait()

    return pl.pallas_call(
        body, in_specs=[ANY] * n, out_specs=[ANY] * n,
        out_shape=[jax.ShapeDtypeStruct((N_DEV,) + x.shape, x.dtype) for x in xs],
        scratch_shapes=[pltpu.SemaphoreType.DMA((n, 7)), pltpu.SemaphoreType.DMA((n, 7)),
                        pltpu.SemaphoreType.DMA((n,))],
        name=name)(*xs)


def _peers():
    x, y, c = _me()
    out = []
    for k in range(1, N_DEV):
        kx, ky, kc = (k >> 2) & 1, (k >> 1) & 1, k & 1
        out.append(((1 - x) if kx else x, (1 - y) if ky else y, (1 - c) if kc else c))
    return out


def _all_reduce_small(xs, name):
    n = len(xs)

    def body(*refs):
        x_refs, o_refs = refs[:n], refs[n:2 * n]
        bufs = refs[2 * n:3 * n]
        send_sems, recv_sems = refs[3 * n:]
        me = _lin(_me())
        peers = _peers()
        for a in range(n):
            bufs[a][me] = x_refs[a][...]
        sends = []
        for a in range(n):
            for k, peer in enumerate(peers):
                sends.append(pltpu.make_async_remote_copy(
                    src_ref=bufs[a].at[me], dst_ref=bufs[a].at[me], send_sem=send_sems.at[a, k],
                    recv_sem=recv_sems.at[a, k], device_id=peer, device_id_type=MESH))
        for cp in sends:
            cp.start()
        for a in range(n):
            for k, peer in enumerate(peers):
                pltpu.make_async_remote_copy(
                    src_ref=bufs[a].at[me], dst_ref=bufs[a].at[_lin(peer)], send_sem=send_sems.at[a, k],
                    recv_sem=recv_sems.at[a, k], device_id=peer, device_id_type=MESH).wait_recv()
        for cp in sends:
            cp.wait_send()
        for a in range(n):
            total = bufs[a][0]
            for q in range(1, N_DEV):
                total = total + bufs[a][q]
            o_refs[a][...] = total

    vmem = pl.BlockSpec(memory_space=pltpu.VMEM)
    return pl.pallas_call(
        body, in_specs=[vmem] * n, out_specs=[vmem] * n,
        out_shape=[jax.ShapeDtypeStruct(x.shape, F32) for x in xs],
        scratch_shapes=[pltpu.VMEM((N_DEV,) + x.shape, F32) for x in xs]
        + [pltpu.SemaphoreType.DMA((n, 7)), pltpu.SemaphoreType.DMA((n, 7))],
        name=name)(*xs)


def _pair_add(part, got, name):
    _, rows, cols = part.shape
    tm = _rows_tile(rows, cols * 2, budget=2 << 20)
    core = jnp.reshape(lax.axis_index("c"), (1,)).astype(jnp.int32)

    def body(c_ref, p_ref, g_ref, o_ref):
        o_ref[...] = (p_ref[...].astype(F32) + g_ref[...].astype(F32)).astype(BF16)

    spec = pl.BlockSpec((None, tm, cols), lambda q, i, c: (q, i, 0))
    grid_spec = pltpu.PrefetchScalarGridSpec(
        num_scalar_prefetch=1, grid=(4, rows // tm),
        in_specs=[pl.BlockSpec((None, tm, cols), lambda q, i, c: (2 * q + c[0], i, 0)), spec], out_specs=spec)
    return pl.pallas_call(body, grid_spec=grid_spec, out_shape=jax.ShapeDtypeStruct((4, rows, cols), BF16),
                          compiler_params=_params(2), name=name)(core, part, got)


def _adam_math(w, g, m, v):
    nm = ADAM_B1 * m + (1.0 - ADAM_B1) * g
    nv = ADAM_B2 * v + (1.0 - ADAM_B2) * (g * g)
    m_hat = nm / (1.0 - ADAM_B1 ** ADAM_STEP)
    v_hat = nv / (1.0 - ADAM_B2 ** ADAM_STEP)
    return -ADAM_LR * (m_hat / (jnp.sqrt(v_hat) + ADAM_EPS) + ADAM_WD * w), nm, nv


def _sum_chips(got, name):
    _, rows, cols = got.shape
    tm = _rows_tile(rows, cols * 2 * 4, budget=2 << 20)

    def body(r_ref, o_ref):
        o_ref[...] = ((r_ref[0].astype(F32) + r_ref[1].astype(F32)) + r_ref[2].astype(F32)) + r_ref[3].astype(F32)

    return pl.pallas_call(
        body, grid=(rows // tm,), in_specs=[pl.BlockSpec((4, tm, cols), lambda i: (0, i, 0))],
        out_specs=pl.BlockSpec((tm, cols), lambda i: (i, 0)), out_shape=jax.ShapeDtypeStruct((rows, cols), F32),
        compiler_params=_params(1), name=name)(got)


def _sum_adamw(got, col_block, w, m, v, name):
    _, rows, cols = w.shape
    tm = _rows_tile(rows, cols * 4, budget=2 << 20)

    def body(r_ref, w_ref, m_ref, v_ref, g_ref, d_ref, nm_ref, nv_ref):
        g = ((r_ref[0].astype(F32) + r_ref[1].astype(F32)) + r_ref[2].astype(F32)) + r_ref[3].astype(F32)
        g_ref[...] = g
        d_ref[...], nm_ref[...], nv_ref[...] = _adam_math(w_ref[...], g, m_ref[...], v_ref[...])

    spec = pl.BlockSpec((None, tm, cols), lambda i: (0, i, 0))
    out = jax.ShapeDtypeStruct(w.shape, F32)
    return pl.pallas_call(
        body, grid=(rows // tm,), in_specs=[pl.BlockSpec((4, tm, cols), lambda i: (0, i, col_block)), spec, spec, spec],
        out_specs=[spec] * 4, out_shape=[out] * 4, compiler_params=_params(1), name=name)(got, w, m, v)


def _adamw(w, g, m, v, name):
    rows, cols = w.shape

    def body(w_ref, g_ref, m_ref, v_ref, d_ref, nm_ref, nv_ref):
        d_ref[...], nm_ref[...], nv_ref[...] = _adam_math(w_ref[...], g_ref[...], m_ref[...], v_ref[...])

    tm = _rows_tile(rows, cols * 4, budget=2 << 20, mult=8)
    spec = pl.BlockSpec((tm, cols), lambda i: (i, 0))
    out = jax.ShapeDtypeStruct(w.shape, F32)
    return pl.pallas_call(body, grid=(rows // tm,), in_specs=[spec] * 4, out_specs=[spec] * 3,
                          out_shape=[out] * 3, compiler_params=_params(1), name=name)(w, g, m, v)


def _permute_in(w):
    logit0 = 3 * FOX_W
    pad = jnp.zeros(w.shape[:-1] + (HEAD - N_LOGIT,), w.dtype)
    return jnp.concatenate([w[..., :logit0], w[..., logit0 + N_LOGIT:], w[..., logit0:logit0 + N_LOGIT], pad], axis=-1)


def _unpermute_in(w):
    logit0 = 3 * FOX_W
    return jnp.concatenate([w[..., :logit0], w[..., C_FL:C_FL + N_LOGIT], w[..., logit0:C_FL]], axis=-1)


def _pad_row(v, width):
    return jnp.pad(v, ((0, 0), (0, width - v.shape[1])))


def _pack_small(fq, fk, sq, sk, mq, mk, fb, sinks):
    rows = [fq, fk, sq, sk, mq, mk, _pad_row(fb, HEAD), _pad_row(sinks, HEAD)]
    return jnp.concatenate(rows + [jnp.zeros((8, HEAD), F32)], axis=0)


def _pack_norms(a, b, c, d):
    return jnp.concatenate([a, b, c, d, jnp.zeros((4, a.shape[1]), F32)], axis=0)


def kernel(x, mem, ffn1_norm, ffn1_gate, ffn1_up, ffn1_down, mix_norm, mem_norm, w_in, forget_bias, w_mem_k, w_mem_v, fox_q_gain, fox_k_gain, swa_q_gain, swa_k_gain, swa_sinks, mem_q_gain, mem_k_gain, w_out, ffn2_norm, ffn2_gate, ffn2_up, ffn2_down, loss_target, m_ffn1_norm, m_ffn1_gate, m_ffn1_up, m_ffn1_down, m_mix_norm, m_mem_norm, m_w_in, m_forget_bias, m_w_mem_k, m_w_mem_v, m_fox_q_gain, m_fox_k_gain, m_swa_q_gain, m_swa_k_gain, m_swa_sinks, m_mem_q_gain, m_mem_k_gain, m_w_out, m_ffn2_norm, m_ffn2_gate, m_ffn2_up, m_ffn2_down, v_ffn1_norm, v_ffn1_gate, v_ffn1_up, v_ffn1_down, v_mix_norm, v_mem_norm, v_w_in, v_forget_bias, v_w_mem_k, v_w_mem_v, v_fox_q_gain, v_fox_k_gain, v_swa_q_gain, v_swa_k_gain, v_swa_sinks, v_mem_q_gain, v_mem_k_gain, v_w_out, v_ffn2_norm, v_ffn2_gate, v_ffn2_up, v_ffn2_down):
    x0 = x[0]
    mem0 = mem[0]
    target = loss_target[0]
    t, d = x0.shape
    d_shard = w_in.shape[1]
    m_len = mem0.shape[0]
    tm = min(t, 512)
    tk = min(t, 2048)
    tn = IN_W // 3
    tkw, tnw = min(t, 1024), IN_W // 3

    def swap(a):
        return jnp.swapaxes(a, 1, 2)

    gate1, up1, gate2, up2 = swap(ffn1_gate), swap(ffn1_up), swap(ffn2_gate), swap(ffn2_up)

    local = {
        "g1": gate1[0], "u1": up1[0], "d1": ffn1_down[0],
        "g2": gate2[0], "u2": up2[0], "d2": ffn2_down[0],
        "in": _permute_in(w_in[0]), "out": w_out[0],
        "mkv": jnp.concatenate([w_mem_k[0], w_mem_v[0]], axis=1),
    }
    shard = {k: _cast_bf16(v, f"cast_{k}") for k, v in local.items()}
    sp = _pack_small(fox_q_gain, fox_k_gain, swa_q_gain, swa_k_gain, mem_q_gain, mem_k_gain, forget_bias, swa_sinks)
    wt = {}

    wt["g1"], wt["u1"] = _all_gather([shard["g1"], shard["u1"]], "gather_ffn1_in")
    xn1 = _rms_fwd(x0, ffn1_norm, "ffn1_norm")
    half_fs = shard["g2"].shape[0] // 2
    (a1, b1, h1), ((wt["d1"], wt["in"]),) = _ffn_up(
        xn1, wt["g1"], wt["u1"], "ffn1", rides=[_ride_gather([shard["d1"], shard["in"]], 0.87)])
    tc = min(t, CONTRACT_ROWS)
    (x1, hn), ((wt["out"], wt["mkv"]),) = _ffn_down(
        x0, h1, wt["d1"], "ffn1", rides=[_ride_gather([shard["out"], shard["mkv"]], 0.6)],
        tail=_tail_norm(mix_norm, t, d, tc))
    w_in_full = wt["in"].reshape(d, IN_W)

    proj, (half,) = _mm(
        "proj", [(hn, pl.BlockSpec((tm, d), lambda n, i, k: (i, 0)),
                  w_in_full, pl.BlockSpec((d, tn), lambda n, i, k: (0, n)))],
        "nn", (3, t // tm, 1), jax.ShapeDtypeStruct((t, IN_W), F32), pl.BlockSpec((tm, tn), lambda n, i, k: (i, n)),
        rides=[_ride_gather_chips([shard["g2"]], rows=(0, half_fs))])
    w_out_full = wt["out"].reshape(d, d)
    w_mkv_full = wt["mkv"].reshape(d, 2 * MEM_W)
    fq, fk, fv, sq, sk, sv, mq, c_col = _attn_pre(proj, sp, "attn_pre")
    c_row = jnp.transpose(c_col[:, :8])
    c_rep = jnp.broadcast_to(c_row[:FOX_H, :, None], (FOX_H, t, HEAD))

    mn = _rms_fwd(mem0, mem_norm, "mem_norm")
    mkv = _mm("mem_kv", [(mn, pl.BlockSpec((m_len, d), lambda k: (0, 0)),
                          w_mkv_full, pl.BlockSpec((d, 2 * MEM_W), lambda k: (0, 0)))],
              "nn", (1,), jax.ShapeDtypeStruct((m_len, 2 * MEM_W), F32),
              pl.BlockSpec((m_len, 2 * MEM_W), lambda k: (0, 0)))
    mk, mv = _mem_pre(mkv, sp, "mem_pre")

    (o_a, lse_a), (half, half_u2) = _fox_fwd(
        fq, fk, fv, c_rep, c_row, "fox_fwd",
        rides=[_ride_gather_chips([shard["g2"]], rows=(half_fs, half_fs), into=half),
               _ride_gather_chips([shard["u2"]], rows=(0, half_fs))])
    (o_b, lse_b), ((wt["g2"],), half_u2) = _swa_fwd(
        sq, sk, sv, sp, "swa_fwd",
        rides=[_ride_gather_sibling(half), _ride_gather_chips([shard["u2"]], rows=(half_fs, half_fs), into=half_u2)])
    o_c, lse_c = _mem_fwd(mq, mk, mv, "mem_fwd")

    def rows_spec(width):
        return pl.BlockSpec((tm, width), lambda i, k: (i, 0))

    def wout_rows(first, width):
        assert first % width == 0
        return pl.BlockSpec((width, d), lambda i, k: (first // width, 0), pipeline_mode=pl.Buffered(1))

    xspec = pl.BlockSpec((tm, d), lambda i, k: (i, 0))
    (x2, xn2), ((wt["u2"],),) = _mm(
        "mix_out",
        [(o_a, rows_spec(FOX_W), w_out_full, wout_rows(0, FOX_W)),
         (o_b, rows_spec(SWA_W), w_out_full, wout_rows(FOX_W, SWA_W)),
         (o_c, rows_spec(MEM_W), w_out_full, wout_rows(FOX_W + SWA_W, MEM_W))],
        "nn", (t // tm, 1), jax.ShapeDtypeStruct((t, d), F32), xspec, res=x1, res_spec=xspec,
        rides=[_ride_gather_sibling(half_u2)], tail=_tail_norm(ffn2_norm, t, d, tm))

    (a2, b2, h2), ((wt["d2"],),) = _ffn_up(xn2, wt["g2"], wt["u2"], "ffn2", rides=[_ride_gather([shard["d2"]], 0.75)])
    dy, dyb, sq_err = _ffn_down(x2, h2, wt["d2"], "ffn2", tail=_tail_loss(target, t, d, tc))
    loss = lax.psum(0.5 * sq_err[0, 0] / d, ("x", "y", "c"))

    got = {}
    paired = {}
    landed = {}

    def pair(k, part):
        paired[k] = _pair_add(part, got[k], f"pair_{k}")

    (dg2, du2), _ = _ffn_dact(dyb, wt["d2"], a2, b2, "ffn2")
    part_d2 = _ffn_dw(h2, dyb, 0.5, "ffn2_dwd")
    part_g2, ((got["d2"],),) = _ffn_dw(dg2, xn2, 1.0, "ffn2_dwg", rides=[_ride_scatter_sibling([part_d2])])
    pair("d2", part_d2)
    half_rows = part_d2.shape[1] // 2
    first, second = (0, half_rows), (half_rows, half_rows)
    part_u2, (half, (got["g2"],)) = _ffn_dw(
        du2, xn2, 1.0, "ffn2_dwu",
        rides=[_ride_scatter_chips([paired["d2"]], first), _ride_scatter_sibling([part_g2])])
    pair("g2", part_g2)
    dxn2, ((landed["d2"],),) = _ffn_contract(
        dg2, wt["g2"], "ffn2_dxn_g", rides=[_ride_scatter_chips([paired["d2"]], second, into=half)])
    (dx2, dx2b, dgain_ffn2), (half_g2, (got["u2"],)) = _ffn_contract(
        du2, wt["u2"], "ffn2_dxn_u", res=dxn2,
        rides=[_ride_scatter_chips([paired["g2"]], first), _ride_scatter_sibling([part_u2])],
        tail=_tail_norm_bwd(x2, ffn2_norm, dy, t, d, tc))
    pair("u2", part_u2)

    dmixed = _mm("mix_out_dx", [(dx2b, xspec, w_out_full, pl.BlockSpec((d, d), lambda i, k: (0, 0)))],
                 "nt", (t // tm, 1), jax.ShapeDtypeStruct((t, d), F32), xspec)

    def k_rows(width):
        return pl.BlockSpec((tk, width), lambda j, k: (k, 0))

    part_out = [
        _mm(f"mix_out_dw{i}", [(o, k_rows(width), dx2b, k_rows(d))], "tn", (1, t // tk),
            jax.ShapeDtypeStruct((width, d), BF16), pl.BlockSpec((width, d), lambda j, k: (0, 0)))
        for i, (o, width) in enumerate(((o_a, FOX_W), (o_b, SWA_W), (o_c, MEM_W)))
    ]
    part_out = jnp.concatenate(part_out, axis=0).reshape(N_DEV, d_shard, d)

    delta, delta_rep = _delta(dmixed, o_a, o_b, o_c, "attn_delta")
    (dfq, dfk, dfv, dc_col, dc_row), ((landed["g2"],), (landed["u2"],)) = _fox_bwd(
        fq, fk, fv, c_rep, c_row, dmixed, lse_a, delta_rep, "fox_bwd",
        rides=[_ride_scatter_chips([paired["g2"]], second, into=half_g2), _ride_scatter_chips([paired["u2"]])])
    dsq, dsk, dsv, dsp_sink = _swa_bwd(sq, sk, sv, sp, dmixed, lse_b, delta, "swa_bwd")
    dmq, dmk, dmv = _mem_bwd(mq, mk, mv, dmixed, lse_c, delta, "mem_bwd")

    dmkv, dsp_mem = _mem_post_bwd(mkv, sp, dmk, dmv, "mem_post_bwd")
    part_mkv = _mm("mem_kv_dw", [(mn, pl.BlockSpec((m_len, d), lambda k: (0, 0)),
                                  dmkv, pl.BlockSpec((m_len, 2 * MEM_W), lambda k: (0, 0)))],
                   "tn", (1,), jax.ShapeDtypeStruct((d, 2 * MEM_W), BF16),
                   pl.BlockSpec((d, 2 * MEM_W), lambda k: (0, 0))).reshape(N_DEV, d_shard, 2 * MEM_W)
    dmn = _mm("mem_kv_dx", [(dmkv, pl.BlockSpec((m_len, 2 * MEM_W), lambda k: (0, 0)),
                             w_mkv_full, pl.BlockSpec((d, 2 * MEM_W), lambda k: (0, 0)))],
              "nt", (1,), jax.ShapeDtypeStruct((m_len, d), F32), pl.BlockSpec((m_len, d), lambda k: (0, 0)))
    _, _, dgain_mem = _rms_bwd(mem0, mem_norm, dmn, None, "mem_norm_bwd")

    dc_row_t = _pad_row(jnp.transpose(dc_row), HEAD)
    dproj, dsp_attn = _attn_post_bwd(proj, sp, dfq, dfk, dfv, dsq, dsk, dsv, dmq, dc_col, dc_row_t, "attn_post_bwd")
    (dx1, dx1b, dgain_mix), ((got["out"], got["mkv"]),) = _mm(
        "proj_dx", [(dproj, pl.BlockSpec((tc, IN_W), lambda i, k: (i, 0)),
                     w_in_full, pl.BlockSpec((d, IN_W), lambda i, k: (0, 0), pipeline_mode=pl.Buffered(1)))],
        "nt", (t // tc, 1), jax.ShapeDtypeStruct((t, d), F32), pl.BlockSpec((tc, d), lambda i, k: (i, 0)),
        rides=[_ride_scatter_sibling([part_out, part_mkv])], tail=_tail_norm_bwd(x1, mix_norm, dx2, t, d, tc))
    pair("out", part_out)
    pair("mkv", part_mkv)
    part_in, ((landed["out"], landed["mkv"]),) = _mm(
        "proj_dw", [(hn, pl.BlockSpec((tkw, d), lambda n, k: (k, 0)),
                     dproj, pl.BlockSpec((tkw, tnw), lambda n, k: (k, n)))],
        "tn", (IN_W // tnw, t // tkw), jax.ShapeDtypeStruct((d, IN_W), BF16), pl.BlockSpec((d, tnw), lambda n, k: (0, n)),
        rides=[_ride_scatter_chips([paired["out"], paired["mkv"]])])
    part_in = part_in.reshape(N_DEV, d_shard, IN_W)

    part_d1, ((got["in"],),) = _ffn_dw(h1, dx1b, 0.5, "ffn1_dwd", rides=[_ride_scatter_sibling([part_in])])
    pair("in", part_in)
    (dg1, du1), ((landed["in"],), (got["d1"],)) = _ffn_dact(
        dx1b, wt["d1"], a1, b1, "ffn1",
        rides=[_ride_scatter_chips([paired["in"]]), _ride_scatter_sibling([part_d1])])
    pair("d1", part_d1)
    part_g1, (half_d1,) = _ffn_dw(dg1, xn1, 1.0, "ffn1_dwg", rides=[_ride_scatter_chips([paired["d1"]], first)])
    part_u1, ((landed["d1"],), (got["g1"],)) = _ffn_dw(
        du1, xn1, 1.0, "ffn1_dwu",
        rides=[_ride_scatter_chips([paired["d1"]], second, into=half_d1), _ride_scatter_sibling([part_g1])])
    pair("g1", part_g1)
    dxn1, ((landed["g1"],), (got["u1"],)) = _ffn_contract(
        dg1, wt["g1"], "ffn1_dxn_g", rides=[_ride_scatter_chips([paired["g1"]]), _ride_scatter_sibling([part_u1])])
    pair("u1", part_u1)
    (grad_x, _, dgain_ffn1), ((landed["u1"],),) = _ffn_contract(
        du1, wt["u1"], "ffn1_dxn_u", res=dxn1, rides=[_ride_scatter_chips([paired["u1"]])],
        tail=_tail_norm_bwd(x0, ffn1_norm, dx1, t, d, tc))

    norms_sum, small_sum = _all_reduce_small(
        [_pack_norms(dgain_ffn1, dgain_mix, dgain_mem, dgain_ffn2), dsp_attn + dsp_sink + dsp_mem], "reduce_small")

    result = {
        "ffn1_gate": map(swap, _sum_adamw(landed["g1"], 0, gate1, swap(m_ffn1_gate), swap(v_ffn1_gate), "adamw_ffn1_gate")),
        "ffn1_up": map(swap, _sum_adamw(landed["u1"], 0, up1, swap(m_ffn1_up), swap(v_ffn1_up), "adamw_ffn1_up")),
        "ffn1_down": _sum_adamw(landed["d1"], 0, ffn1_down, m_ffn1_down, v_ffn1_down, "adamw_ffn1_down"),
        "w_mem_k": _sum_adamw(landed["mkv"], 0, w_mem_k, m_w_mem_k, v_w_mem_k, "adamw_w_mem_k"),
        "w_mem_v": _sum_adamw(landed["mkv"], 1, w_mem_v, m_w_mem_v, v_w_mem_v, "adamw_w_mem_v"),
        "w_out": _sum_adamw(landed["out"], 0, w_out, m_w_out, v_w_out, "adamw_w_out"),
        "ffn2_gate": map(swap, _sum_adamw(landed["g2"], 0, gate2, swap(m_ffn2_gate), swap(v_ffn2_gate), "adamw_ffn2_gate")),
        "ffn2_up": map(swap, _sum_adamw(landed["u2"], 0, up2, swap(m_ffn2_up), swap(v_ffn2_up), "adamw_ffn2_up")),
        "ffn2_down": _sum_adamw(landed["d2"], 0, ffn2_down, m_ffn2_down, v_ffn2_down, "adamw_ffn2_down"),
    }
    grad_in = _unpermute_in(_sum_chips(landed["in"], "sum_w_in"))
    result["w_in"] = (grad_in[None],) + tuple(
        o[None] for o in _adamw(w_in[0], grad_in, m_w_in[0], v_w_in[0], "adamw_w_in"))

    norm_names = ["ffn1_norm", "mix_norm", "mem_norm", "ffn2_norm"]
    norm_w = _pack_norms(ffn1_norm, mix_norm, mem_norm, ffn2_norm)
    norm_m = _pack_norms(m_ffn1_norm, m_mix_norm, m_mem_norm, m_ffn2_norm)
    norm_v = _pack_norms(v_ffn1_norm, v_mix_norm, v_mem_norm, v_ffn2_norm)
    outs = (norms_sum,) + tuple(_adamw(norm_w, norms_sum, norm_m, norm_v, "adamw_norms"))
    for i, k in enumerate(norm_names):
        result[k] = tuple(o[i:i + 1] for o in outs)

    small_names = ["fox_q_gain", "fox_k_gain", "swa_q_gain", "swa_k_gain", "mem_q_gain", "mem_k_gain",
                   "forget_bias", "swa_sinks"]
    small_m = _pack_small(m_fox_q_gain, m_fox_k_gain, m_swa_q_gain, m_swa_k_gain, m_mem_q_gain, m_mem_k_gain,
                          m_forget_bias, m_swa_sinks)
    small_v = _pack_small(v_fox_q_gain, v_fox_k_gain, v_swa_q_gain, v_swa_k_gain, v_mem_q_gain, v_mem_k_gain,
                          v_forget_bias, v_swa_sinks)
    outs = (small_sum,) + tuple(_adamw(sp, small_sum, small_m, small_v, "adamw_small"))
    for i, k in enumerate(small_names):
        width = N_LOGIT if k in ("forget_bias", "swa_sinks") else HEAD
        result[k] = tuple(o[i:i + 1, :width] for o in outs)

    order = ["ffn1_norm", "ffn1_gate", "ffn1_up", "ffn1_down", "mix_norm", "mem_norm", "w_in", "forget_bias",
             "w_mem_k", "w_mem_v", "fox_q_gain", "fox_k_gain", "swa_q_gain", "swa_k_gain", "swa_sinks",
             "mem_q_gain", "mem_k_gain", "w_out", "ffn2_norm", "ffn2_gate", "ffn2_up", "ffn2_down"]
    result = {k: tuple(v) for k, v in result.items()}
    flat = [loss, grad_x[None]]
    for kind in range(4):
        flat += [result[k][kind] for k in order]
    return tuple(flat)
```

```python
import functools

import jax
import jax.numpy as jnp
from jax import lax
from jax.experimental import pallas as pl
from jax.experimental.pallas import tpu as pltpu

F32 = jnp.float32
BF16 = jnp.bfloat16
MESH = pl.DeviceIdType.MESH
ANY = pl.BlockSpec(memory_space=pl.ANY)

N_DEV = 8
EPS = 1e-6
NEG_INF = -1e30
HEAD = 128
FOX_H, SWA_H, SWA_KV, MEM_H = 6, 6, 2, 4
FOX_W, SWA_W, SWA_KV_W, MEM_W = FOX_H * HEAD, SWA_H * HEAD, SWA_KV * HEAD, MEM_H * HEAD
SCALE = HEAD ** -0.5
SWA_BLOCK = 128
C_FQ, C_FK, C_FV = 0, FOX_W, 2 * FOX_W
C_SQ = 3 * FOX_W
C_SK = C_SQ + SWA_W
C_SV = C_SK + SWA_KV_W
C_MQ = C_SV + SWA_KV_W
C_FL = C_MQ + MEM_W
IN_W = C_FL + HEAD
N_LOGIT = FOX_H
R_FQ, R_FK, R_SQ, R_SK, R_MQ, R_MK, R_FB, R_SINK = range(8)
ADAM_LR, ADAM_B1, ADAM_B2, ADAM_EPS, ADAM_WD, ADAM_STEP = 0.001, 0.9, 0.999, 1e-08, 0.01, 10
VMEM_BYTES = 56 * 1024 * 1024

DN = {
    "nn": (((1,), (0,)), ((), ())),
    "nt": (((1,), (1,)), ((), ())),
    "tn": (((0,), (0,)), ((), ())),
}


def _params(n_axes):
    return pltpu.CompilerParams(dimension_semantics=("arbitrary",) * n_axes, vmem_limit_bytes=VMEM_BYTES)


def _dot(a, b, dims="nn"):
    return lax.dot_general(a.astype(BF16), b.astype(BF16), DN[dims], preferred_element_type=F32)


def _sigmoid(x):
    return 0.5 * jnp.tanh(0.5 * x) + 0.5


def _me():
    return lax.axis_index("x"), lax.axis_index("y"), lax.axis_index("c")


def _lin(p):
    return 4 * p[0] + 2 * p[1] + p[2]


def _rows_tile(rows, row_bytes, budget=4 << 20, mult=16):
    best = None
    for k in range(1, rows + 1):
        if rows % k == 0 and (rows // k) % mult == 0 and (rows // k) * row_bytes <= budget:
            best = rows // k
            break
    assert best is not None, (rows, row_bytes)
    return best


class _Ride:
    def __init__(self, inputs, out_shapes, aliases, n_remote, n_local, start, wait):
        self.inputs, self.out_shapes, self.aliases = list(inputs), list(out_shapes), dict(aliases)
        self.n_remote, self.n_local, self.start, self.wait = n_remote, n_local, start, wait


def _remote(src, dst, send, recv, k, to):
    return pltpu.make_async_remote_copy(src_ref=src, dst_ref=dst, send_sem=send.at[k], recv_sem=recv.at[k],
                                        device_id=to, device_id_type=MESH)


def _other_chips(x, y):
    return [(1 - x, y), (x, 1 - y), (1 - x, 1 - y)]


ALL_CHIPS = [(0, 0), (0, 1), (1, 0), (1, 1)]


def _rows_of(ref, rows, slot=None):
    if slot is None:
        return ref if rows is None else ref.at[pl.ds(rows[0], rows[1])]
    return ref.at[slot] if rows is None else ref.at[slot, pl.ds(rows[0], rows[1])]


def _ride_gather_chips(xs, rows=None, into=None):
    n = len(xs)

    def copies(ins, outs, send, recv):
        x, y, c = _me()
        out = []
        for a in range(n):
            for j, chip in enumerate(_other_chips(x, y)):
                peer = (*chip, c)
                src = _rows_of(ins[a], rows)
                out.append((_remote(src, _rows_of(outs[a], rows, _lin((x, y, c))), send, recv, 3 * a + j, peer),
                            _remote(src, _rows_of(outs[a], rows, _lin(peer)), send, recv, 3 * a + j, peer)))
        return out

    def mine(ins, outs, local):
        me = _lin(_me())
        return [pltpu.make_async_copy(_rows_of(ins[a], rows), _rows_of(outs[a], rows, me), local.at[a])
                for a in range(n)]

    def start(ins, outs, send, recv, local):
        for cp in mine(ins, outs, local):
            cp.start()
        for sent, _ in copies(ins, outs, send, recv):
            sent.start()

    def wait(ins, outs, send, recv, local):
        for sent, landed in copies(ins, outs, send, recv):
            landed.wait_recv()
            sent.wait_send()
        for cp in mine(ins, outs, local):
            cp.wait()

    shapes = [jax.ShapeDtypeStruct((N_DEV,) + x.shape, x.dtype) for x in xs]
    if into is None:
        return _Ride(xs, shapes, {}, 3 * n, n, start, wait)
    return _Ride(list(xs) + list(into), shapes, {n + a: a for a in range(n)}, 3 * n, n, start, wait)


def _ride_gather_sibling(bufs):
    n = len(bufs)

    def copies(outs, send, recv):
        x, y, c = _me()
        out = []
        for a in range(n):
            for q, (px, py) in enumerate(ALL_CHIPS):
                there = outs[a].at[4 * px + 2 * py + c]
                here = outs[a].at[4 * px + 2 * py + 1 - c]
                out.append((_remote(there, there, send, recv, 4 * a + q, (x, y, 1 - c)),
                            _remote(here, here, send, recv, 4 * a + q, (x, y, 1 - c))))
        return out

    def start(ins, outs, send, recv, local):
        for sent, _ in copies(outs, send, recv):
            sent.start()

    def wait(ins, outs, send, recv, local):
        for sent, landed in copies(outs, send, recv):
            landed.wait_recv()
            sent.wait_send()

    shapes = [jax.ShapeDtypeStruct(b.shape, b.dtype) for b in bufs]
    return _Ride(bufs, shapes, {a: a for a in range(n)}, 4 * n, 0, start, wait)


def _ride_gather(xs, mid_frac, rows=None, into=None):
    n = len(xs)
    chips = _ride_gather_chips(xs, rows, into)

    def sibling_copies(outs, send, recv):
        x, y, c = _me()
        out = []
        for a in range(n):
            for q, (px, py) in enumerate(ALL_CHIPS):
                there = _rows_of(outs[a], rows, 4 * px + 2 * py + c)
                here = _rows_of(outs[a], rows, 4 * px + 2 * py + 1 - c)
                k = 3 * n + 4 * a + q
                out.append((_remote(there, there, send, recv, k, (x, y, 1 - c)),
                            _remote(here, here, send, recv, k, (x, y, 1 - c))))
        return out

    def mid(ins, outs, send, recv, local):
        chips.wait(ins, outs, send, recv, local)
        for sent, _ in sibling_copies(outs, send, recv):
            sent.start()

    def wait(ins, outs, send, recv, local):
        for sent, landed in sibling_copies(outs, send, recv):
            landed.wait_recv()
            sent.wait_send()

    ride = _Ride(chips.inputs, chips.out_shapes, chips.aliases, 7 * n, n, chips.start, wait)
    ride.mid, ride.mid_frac = mid, mid_frac
    return ride


def _ride_scatter_sibling(parts):
    n = len(parts)

    def copies(ins, outs, send, recv):
        x, y, c = _me()
        out = []
        for a in range(n):
            for q, (px, py) in enumerate(ALL_CHIPS):
                cp = _remote(ins[a].at[4 * px + 2 * py + 1 - c], outs[a].at[q], send, recv, 4 * a + q, (x, y, 1 - c))
                out.append(cp)
        return out

    def start(ins, outs, send, recv, local):
        for cp in copies(ins, outs, send, recv):
            cp.start()

    def wait(ins, outs, send, recv, local):
        for cp in copies(ins, outs, send, recv):
            cp.wait_recv()
            cp.wait_send()

    shapes = [jax.ShapeDtypeStruct((4,) + p.shape[1:], p.dtype) for p in parts]
    return _Ride(parts, shapes, {}, 4 * n, 0, start, wait)


def _ride_scatter_chips(pairs, rows=None, into=None):
    n = len(pairs)

    def part(ref, slot):
        return _rows_of(ref, rows, slot)

    def copies(ins, outs, send, recv):
        x, y, c = _me()
        out = []
        for a in range(n):
            for j, (px, py) in enumerate(_other_chips(x, y)):
                peer = (px, py, c)
                src = part(ins[a], 2 * px + py)
                out.append((_remote(src, part(outs[a], 2 * x + y), send, recv, 3 * a + j, peer),
                            _remote(src, part(outs[a], 2 * px + py), send, recv, 3 * a + j, peer)))
        return out

    def mine(ins, outs, local):
        x, y, _ = _me()
        return [pltpu.make_async_copy(part(ins[a], 2 * x + y), part(outs[a], 2 * x + y), local.at[a])
                for a in range(n)]

    def start(ins, outs, send, recv, local):
        for cp in mine(ins, outs, local):
            cp.start()
        for sent, _ in copies(ins, outs, send, recv):
            sent.start()

    def wait(ins, outs, send, recv, local):
        for sent, landed in copies(ins, outs, send, recv):
            landed.wait_recv()
            sent.wait_send()
        for cp in mine(ins, outs, local):
            cp.wait()

    shapes = [jax.ShapeDtypeStruct(p.shape, p.dtype) for p in pairs]
    if into is None:
        return _Ride(pairs, shapes, {}, 3 * n, n, start, wait)
    return _Ride(list(pairs) + list(into), shapes, {n + a: a for a in range(n)}, 3 * n, n, start, wait)


def _call(name, body, grid, in_specs, out_specs, out_shape, operands, scratch=(), rides=()):
    n_in, n_out, n_scr = len(operands), len(out_shape), len(scratch)
    ride_in, ride_out, ride_scr, aliases, spans = [], [], [], {}, []
    for r in rides:
        for i, o in r.aliases.items():
            aliases[n_in + len(ride_in) + i] = n_out + len(ride_out) + o
        spans.append((len(ride_in), len(r.inputs), len(ride_out), len(r.out_shapes)))
        ride_in += r.inputs
        ride_out += r.out_shapes
        ride_scr += [pltpu.SemaphoreType.DMA((r.n_remote,)), pltpu.SemaphoreType.DMA((r.n_remote,)),
                     pltpu.SemaphoreType.DMA((max(r.n_local, 1),))]

    def wrapped(*refs):
        c_in, r_in = refs[:n_in], refs[n_in:n_in + len(ride_in)]
        p = n_in + len(ride_in)
        c_out, r_out = refs[p:p + n_out], refs[p + n_out:p + n_out + len(ride_out)]
        p += n_out + len(ride_out)
        c_scr, r_scr = refs[p:p + n_scr], refs[p + n_scr:]

        n_steps = functools.reduce(lambda a, b: a * b, grid, 1)
        step = functools.reduce(lambda acc, ax: acc * grid[ax] + pl.program_id(ax), range(len(grid)), 0)

        def each(method, at):
            for k, (r, (i0, ni, o0, no)) in enumerate(zip(rides, spans)):
                fn = getattr(r, method, None)
                if fn is None:
                    continue
                run = functools.partial(fn, r_in[i0:i0 + ni], r_out[o0:o0 + no], *r_scr[3 * k:3 * k + 3])
                if grid:
                    pl.when(step == at(r))(run)
                else:
                    run()

        each("start", lambda r: 0)
        each("mid", lambda r: int(r.mid_frac * (n_steps - 1)))
        body(*c_in, *c_out, *c_scr)
        each("wait", lambda r: n_steps - 1)

    outs = pl.pallas_call(
        wrapped, grid=grid, in_specs=list(in_specs) + [ANY] * len(ride_in),
        out_specs=list(out_specs) + [ANY] * len(ride_out), out_shape=list(out_shape) + ride_out,
        scratch_shapes=list(scratch) + ride_scr, input_output_aliases=aliases,
        compiler_params=_params(len(grid)), name=name)(*operands, *ride_in)
    outs = list(outs)
    ride_results = [outs[n_out + o0:n_out + o0 + no] for (_, _, o0, no) in spans]
    return outs[:n_out], ride_results


class _Tail:
    def __init__(self, extra, out_shapes, out_specs, fn):
        self.extra, self.out_shapes, self.out_specs, self.fn = list(extra), list(out_shapes), list(out_specs), fn


def _mm(name, pairs, dims, grid, out_shape, out_spec, res=None, res_spec=None, alpha=1.0, rides=(), tail=None):
    n = len(pairs)
    nk = grid[-1]
    kax = len(grid) - 1
    acc_shape = tuple(d for d in out_spec.block_shape if d is not None)
    n_extra = len(tail.extra) if tail else 0
    n_outs = len(tail.out_shapes) if tail else 1

    def body(*refs):
        pos = 2 * n
        r_ref = None
        if res is not None:
            r_ref = refs[pos]
            pos += 1
        x_refs = refs[pos:pos + n_extra]
        o_refs = refs[pos + n_extra:pos + n_extra + n_outs]
        pos += n_extra + n_outs
        part = None
        for p in range(n):
            d = _dot(refs[2 * p][...], refs[2 * p + 1][...], dims)
            part = d if part is None else part + d

        def finish(acc):
            if alpha != 1.0:
                acc = acc * alpha
            if r_ref is not None:
                acc = r_ref[...] + acc
            if tail:
                tail.fn(acc, x_refs, o_refs)
            else:
                o_refs[0][...] = acc.astype(o_refs[0].dtype)

        if nk == 1:
            finish(part)
        else:
            acc_ref = refs[pos]
            k = pl.program_id(kax)

            @pl.when(k == 0)
            def _():
                acc_ref[...] = part

            @pl.when(k > 0)
            def _():
                acc_ref[...] += part

            @pl.when(k == nk - 1)
            def _():
                finish(acc_ref[...])

    operands, in_specs = [], []
    for a, a_spec, b, b_spec in pairs:
        operands += [a, b]
        in_specs += [a_spec, b_spec]
    if res is not None:
        operands.append(res)
        in_specs.append(res_spec)
    for a, a_spec in (tail.extra if tail else []):
        operands.append(a)
        in_specs.append(a_spec)
    outs, ride_results = _call(name, body, grid, in_specs, tail.out_specs if tail else [out_spec],
                               tail.out_shapes if tail else [out_shape], operands,
                               scratch=[pltpu.VMEM(acc_shape, F32)] if nk > 1 else [], rides=rides)
    outs = outs if tail else outs[0]
    return (outs, ride_results) if rides else outs


def _accumulate(ref, part):
    @pl.when(pl.program_id(0) == 0)
    def _():
        ref[...] = part

    @pl.when(pl.program_id(0) > 0)
    def _():
        ref[...] += part


def _tail_norm(gain, t, d, tm):
    def fn(v, x_refs, o_refs):
        o_refs[0][...] = v
        r = lax.rsqrt(jnp.mean(v * v, axis=-1, keepdims=True) + EPS)
        o_refs[1][...] = (v * r * x_refs[0][...]).astype(BF16)

    rows = pl.BlockSpec((tm, d), lambda i, k: (i, 0))
    return _Tail([(gain, pl.BlockSpec((1, d), lambda i, k: (0, 0)))],
                 [jax.ShapeDtypeStruct((t, d), F32), jax.ShapeDtypeStruct((t, d), BF16)], [rows, rows], fn)


def _tail_loss(target, t, d, tm):
    def fn(v, x_refs, o_refs):
        err = v - x_refs[0][...]
        dy = err * (1.0 / d)
        o_refs[0][...] = dy
        o_refs[1][...] = dy.astype(BF16)
        _accumulate(o_refs[2], jnp.zeros((8, 128), F32) + jnp.sum(err * err))

    rows = pl.BlockSpec((tm, d), lambda i, k: (i, 0))
    return _Tail([(target, rows)],
                 [jax.ShapeDtypeStruct((t, d), F32), jax.ShapeDtypeStruct((t, d), BF16),
                  jax.ShapeDtypeStruct((8, 128), F32)],
                 [rows, rows, pl.BlockSpec((8, 128), lambda i, k: (0, 0))], fn)


def _tail_norm_bwd(x, gain, dres, t, d, tm):
    def fn(dy, x_refs, o_refs):
        xv = x_refs[0][...]
        r = lax.rsqrt(jnp.mean(xv * xv, axis=-1, keepdims=True) + EPS)
        xh = xv * r
        dxh = dy * x_refs[1][...]
        dx = r * (dxh - xh * jnp.mean(dxh * xh, axis=-1, keepdims=True)) + x_refs[2][...]
        o_refs[0][...] = dx
        o_refs[1][...] = dx.astype(BF16)
        _accumulate(o_refs[2], jnp.sum(dy * xh, axis=0, keepdims=True))

    rows = pl.BlockSpec((tm, d), lambda i, k: (i, 0))
    vec = pl.BlockSpec((1, d), lambda i, k: (0, 0))
    return _Tail([(x, rows), (gain, vec), (dres, rows)],
                 [jax.ShapeDtypeStruct((t, d), F32), jax.ShapeDtypeStruct((t, d), BF16),
                  jax.ShapeDtypeStruct((1, d), F32)], [rows, rows, vec], fn)


def _cast_bf16(x, name):
    rows, cols = x.shape
    tm = _rows_tile(rows, cols * 4)

    def body(x_ref, o_ref):
        o_ref[...] = x_ref[...].astype(BF16)

    spec = pl.BlockSpec((tm, cols), lambda i: (i, 0))
    return pl.pallas_call(body, grid=(rows // tm,), in_specs=[spec], out_specs=spec,
                          out_shape=jax.ShapeDtypeStruct(x.shape, BF16), compiler_params=_params(1), name=name)(x)


def _rms_fwd(x, gain, name):
    rows, d = x.shape
    tm = min(rows, 512)

    def body(x_ref, g_ref, o_ref):
        xv = x_ref[...]
        r = lax.rsqrt(jnp.mean(xv * xv, axis=-1, keepdims=True) + EPS)
        o_ref[...] = (xv * r * g_ref[...]).astype(BF16)

    spec = pl.BlockSpec((tm, d), lambda i: (i, 0))
    return pl.pallas_call(body, grid=(rows // tm,), in_specs=[spec, pl.BlockSpec((1, d), lambda i: (0, 0))],
                          out_specs=spec, out_shape=jax.ShapeDtypeStruct(x.shape, BF16),
                          compiler_params=_params(1), name=name)(x, gain)


def _rms_bwd(x, gain, dxn, dres, name, rides=()):
    rows, d = x.shape
    tm = min(rows, 256)
    with_res = dres is not None

    def body(*refs):
        if with_res:
            x_ref, g_ref, dy_ref, r_ref, dx_ref, dxb_ref, dg_ref = refs
        else:
            x_ref, g_ref, dy_ref, dx_ref, dxb_ref, dg_ref = refs
        xv = x_ref[...]
        r = lax.rsqrt(jnp.mean(xv * xv, axis=-1, keepdims=True) + EPS)
        xh = xv * r
        dy = dy_ref[...]
        dxh = dy * g_ref[...]
        dx = r * (dxh - xh * jnp.mean(dxh * xh, axis=-1, keepdims=True))
        if with_res:
            dx = dx + r_ref[...]
        dx_ref[...] = dx
        dxb_ref[...] = dx.astype(BF16)
        part = jnp.sum(dy * xh, axis=0, keepdims=True)

        @pl.when(pl.program_id(0) == 0)
        def _():
            dg_ref[...] = part

        @pl.when(pl.program_id(0) > 0)
        def _():
            dg_ref[...] += part

    spec = pl.BlockSpec((tm, d), lambda i: (i, 0))
    vec = pl.BlockSpec((1, d), lambda i: (0, 0))
    ops = [x, gain, dxn] + ([dres] if with_res else [])
    outs, ride_results = _call(
        name, body, (rows // tm,), [spec, vec, spec] + ([spec] if with_res else []), [spec, spec, vec],
        [jax.ShapeDtypeStruct(x.shape, F32), jax.ShapeDtypeStruct(x.shape, BF16), jax.ShapeDtypeStruct((1, d), F32)],
        ops, rides=rides)
    return (outs, ride_results) if rides else outs


ROW_CHUNK = 256
SHARD_ROWS = 2048


def _ffn_up(xn, wg, wu, tag, rides=(), rows=SHARD_ROWS):
    t, d = xn.shape
    nd, fs, _ = wg.shape
    tm = min(t, rows)
    rc = min(tm, ROW_CHUNK)

    def body(x_ref, wg_ref, wu_ref, a_ref, b_ref, h_ref):
        for r in range(0, tm, rc):
            xv = x_ref[r:r + rc, :]
            g = _dot(xv, wg_ref[...], "nt")
            u = _dot(xv, wu_ref[...], "nt")
            sig = _sigmoid(g)
            silu = g * sig
            a_ref[r:r + rc, :] = (0.5 * u * (sig + silu * (1.0 - sig))).astype(BF16)
            b_ref[r:r + rc, :] = (0.5 * silu).astype(BF16)
            h_ref[r:r + rc, :] = (silu * u).astype(BF16)

    wspec = pl.BlockSpec((None, fs, d), lambda j, i: (j, 0, 0))
    hspec = pl.BlockSpec((None, tm, fs), lambda j, i: (j, i, 0))
    hid = jax.ShapeDtypeStruct((nd, t, fs), BF16)
    return _call(f"{tag}_up", body, (nd, t // tm), [pl.BlockSpec((tm, d), lambda j, i: (i, 0)), wspec, wspec],
                 [hspec] * 3, [hid] * 3, [xn, wg, wu], rides=rides)


CONTRACT_ROWS = 256


def _ffn_contract(hid, w, name, res=None, alpha=1.0, rides=(), tail=None):
    nd, t, fs = hid.shape
    d = w.shape[2]
    tm = min(t, CONTRACT_ROWS if tail or res is not None else 2 * CONTRACT_ROWS)
    xspec = pl.BlockSpec((tm, d), lambda i, k: (i, 0))
    pairs = [(hid, pl.BlockSpec((None, tm, fs), lambda i, k, s=s: (s, i, 0)),
              w, pl.BlockSpec((None, fs, d), lambda i, k, s=s: (s, 0, 0), pipeline_mode=pl.Buffered(1)))
             for s in range(nd)]
    return _mm(name, pairs, "nn", (t // tm, 1), jax.ShapeDtypeStruct((t, d), F32), xspec,
               res=res, res_spec=xspec if res is not None else None, alpha=alpha, rides=rides, tail=tail)


def _ffn_down(x, h, wd, tag, rides=(), tail=None):
    return _ffn_contract(h, wd, f"{tag}_down", res=x, alpha=0.5, rides=rides, tail=tail)


def _ffn_dact(dyb, wd, a, b, tag, rides=()):
    nd, t, fs = a.shape
    d = dyb.shape[1]
    tm = min(t, SHARD_ROWS)
    rc = min(tm, ROW_CHUNK)

    def body(dy_ref, wd_ref, a_ref, b_ref, dg_ref, du_ref):
        for r in range(0, tm, rc):
            dh = _dot(dy_ref[r:r + rc, :], wd_ref[...], "nt")
            dg_ref[r:r + rc, :] = (dh * a_ref[r:r + rc, :].astype(F32)).astype(BF16)
            du_ref[r:r + rc, :] = (dh * b_ref[r:r + rc, :].astype(F32)).astype(BF16)

    hspec = pl.BlockSpec((None, tm, fs), lambda j, i: (j, i, 0))
    hid = jax.ShapeDtypeStruct((nd, t, fs), BF16)
    return _call(f"{tag}_dact", body, (nd, t // tm),
                 [pl.BlockSpec((tm, d), lambda j, i: (i, 0)), pl.BlockSpec((None, fs, d), lambda j, i: (j, 0, 0)),
                  hspec, hspec], [hspec] * 2, [hid] * 2, [dyb, wd, a, b], rides=rides)


def _ffn_dw(hid, act, alpha, name, rides=()):
    nd, t, fs = hid.shape
    d = act.shape[1]
    tk = t
    return _mm(name, [(hid, pl.BlockSpec((None, tk, fs), lambda j, k: (j, k, 0)),
                       act, pl.BlockSpec((tk, d), lambda j, k: (k, 0), pipeline_mode=pl.Buffered(1)))],
               "tn", (nd, t // tk),
               jax.ShapeDtypeStruct((nd, fs, d), BF16), pl.BlockSpec((None, fs, d), lambda j, k: (j, 0, 0)),
               alpha=alpha, rides=rides)


def _head_norm(x, gain):
    r = lax.rsqrt(jnp.mean(x * x, axis=-1, keepdims=True) + EPS)
    return x * r * gain


def _head_norm_bwd(x, gain, dy):
    r = lax.rsqrt(jnp.mean(x * x, axis=-1, keepdims=True) + EPS)
    xh = x * r
    dxh = dy * gain
    dx = r * (dxh - xh * jnp.mean(dxh * xh, axis=-1, keepdims=True))
    return dx, jnp.sum(dy * xh, axis=0, keepdims=True)


def _hs(h, base=0):
    return slice(base + h * HEAD, base + (h + 1) * HEAD)


def _tri(n, lower):
    r = lax.broadcasted_iota(jnp.int32, (n, n), 0)
    c = lax.broadcasted_iota(jnp.int32, (n, n), 1)
    return ((r >= c) if lower else (r <= c)).astype(F32)


def _attn_pre(proj, sp, name):
    t = proj.shape[0]
    tm = min(t, 256)

    def body(p_ref, sp_ref, fq, fk, fv, sq, sk, sv, mq, cc, carry):
        @pl.when(pl.program_id(0) == 0)
        def _():
            carry[...] = jnp.zeros_like(carry)

        for h in range(FOX_H):
            fq[:, _hs(h)] = _head_norm(p_ref[:, _hs(h, C_FQ)], sp_ref[R_FQ:R_FQ + 1, :]).astype(BF16)
            fk[:, _hs(h)] = _head_norm(p_ref[:, _hs(h, C_FK)], sp_ref[R_FK:R_FK + 1, :]).astype(BF16)
        fv[...] = p_ref[:, C_FV:C_FV + FOX_W].astype(BF16)
        for h in range(SWA_H):
            sq[:, _hs(h)] = _head_norm(p_ref[:, _hs(h, C_SQ)], sp_ref[R_SQ:R_SQ + 1, :]).astype(BF16)
        for h in range(SWA_KV):
            sk[:, _hs(h)] = _head_norm(p_ref[:, _hs(h, C_SK)], sp_ref[R_SK:R_SK + 1, :]).astype(BF16)
        sv[...] = p_ref[:, C_SV:C_SV + SWA_KV_W].astype(BF16)
        for h in range(MEM_H):
            mq[:, _hs(h)] = _head_norm(p_ref[:, _hs(h, C_MQ)], sp_ref[R_MQ:R_MQ + 1, :]).astype(BF16)
        z = p_ref[:, C_FL:C_FL + HEAD] + sp_ref[R_FB:R_FB + 1, :]
        lane = lax.broadcasted_iota(jnp.int32, z.shape, 1)
        log_f = jnp.minimum(z, 0.0) - jnp.log(1.0 + jnp.exp(-jnp.abs(z)))
        log_f = jnp.where(lane < N_LOGIT, log_f, 0.0)
        c = jnp.dot(_tri(tm, True), log_f, precision=lax.Precision.HIGHEST, preferred_element_type=F32)
        c = c + carry[0:1, :]
        cc[...] = c
        carry[...] = jnp.broadcast_to(c[tm - 1:tm, :], carry.shape)

    def rows(w):
        return pl.BlockSpec((tm, w), lambda i: (i, 0))

    def shape(w, dt):
        return jax.ShapeDtypeStruct((t, w), dt)

    widths = [FOX_W, FOX_W, FOX_W, SWA_W, SWA_KV_W, SWA_KV_W, MEM_W]
    return pl.pallas_call(
        body, grid=(t // tm,), in_specs=[rows(IN_W), pl.BlockSpec((16, 128), lambda i: (0, 0))],
        out_specs=[rows(w) for w in widths] + [rows(HEAD)],
        out_shape=[shape(w, BF16) for w in widths] + [shape(HEAD, F32)],
        scratch_shapes=[pltpu.VMEM((8, 128), F32)], compiler_params=_params(1), name=name)(proj, sp)


def _attn_post_bwd(proj, sp, dfq, dfk, dfv, dsq, dsk, dsv, dmq, dc_col, dc_row_t, name):
    t = proj.shape[0]
    tm = min(t, 256)
    nb = t // tm

    def body(p_ref, sp_ref, dfq_r, dfk_r, dfv_r, dsq_r, dsk_r, dsv_r, dmq_r, dcc_r, dcr_r, dp_ref, dsp_ref, carry):
        @pl.when(pl.program_id(0) == 0)
        def _():
            carry[...] = jnp.zeros_like(carry)
            dsp_ref[...] = jnp.zeros_like(dsp_ref)

        def group(n_heads, col, row, d_ref):
            total = None
            for h in range(n_heads):
                dx, dg = _head_norm_bwd(p_ref[:, _hs(h, col)], sp_ref[row:row + 1, :], d_ref[:, _hs(h)])
                dp_ref[:, _hs(h, col)] = dx.astype(BF16)
                total = dg if total is None else total + dg
            dsp_ref[row:row + 1, :] += total

        group(FOX_H, C_FQ, R_FQ, dfq_r)
        group(FOX_H, C_FK, R_FK, dfk_r)
        dp_ref[:, C_FV:C_FV + FOX_W] = dfv_r[...].astype(BF16)
        group(SWA_H, C_SQ, R_SQ, dsq_r)
        group(SWA_KV, C_SK, R_SK, dsk_r)
        dp_ref[:, C_SV:C_SV + SWA_KV_W] = dsv_r[...].astype(BF16)
        group(MEM_H, C_MQ, R_MQ, dmq_r)
        dc = dcc_r[...] - dcr_r[...]
        rc = jnp.dot(_tri(tm, False), dc, precision=lax.Precision.HIGHEST, preferred_element_type=F32)
        rc = rc + carry[0:1, :]
        carry[...] = jnp.broadcast_to(rc[0:1, :], carry.shape)
        z = p_ref[:, C_FL:C_FL + HEAD] + sp_ref[R_FB:R_FB + 1, :]
        dz = rc * _sigmoid(-z)
        dp_ref[:, C_FL:C_FL + HEAD] = dz.astype(BF16)
        dsp_ref[R_FB:R_FB + 1, :] += jnp.sum(dz, axis=0, keepdims=True)

    def rows(w):
        return pl.BlockSpec((tm, w), lambda i: (nb - 1 - i, 0))

    small = pl.BlockSpec((16, 128), lambda i: (0, 0))
    widths = [FOX_W, FOX_W, FOX_W, SWA_W, SWA_KV_W, SWA_KV_W, MEM_W, HEAD, HEAD]
    return pl.pallas_call(
        body, grid=(nb,), in_specs=[rows(IN_W), small] + [rows(w) for w in widths],
        out_specs=[rows(IN_W), small],
        out_shape=[jax.ShapeDtypeStruct((t, IN_W), BF16), jax.ShapeDtypeStruct((16, 128), F32)],
        scratch_shapes=[pltpu.VMEM((8, 128), F32)], compiler_params=_params(1), name=name,
    )(proj, sp, dfq, dfk, dfv, dsq, dsk, dsv, dmq, dc_col, dc_row_t)


def _head_column(values):
    rows = values[0].shape[0]
    lane = lax.broadcasted_iota(jnp.int32, (rows, HEAD), 1)
    out = jnp.zeros((rows, HEAD), F32)
    for h, v in enumerate(values):
        out = jnp.where(lane == h, v, out)
    return out


def _head_row(values, n_rows=8):
    cols = values[0].shape[1]
    sub = lax.broadcasted_iota(jnp.int32, (n_rows, cols), 0)
    out = jnp.zeros((n_rows, cols), F32)
    for h, v in enumerate(values):
        out = jnp.where(sub == h, v, out)
    return out


def _delta(dmixed, o_a, o_b, o_c, name):
    t = dmixed.shape[0]
    tm = min(t, 512)

    def body(d_ref, a_ref, b_ref, c_ref, o_ref, rep_ref):
        cols = []
        for ref, n_heads, base in ((a_ref, FOX_H, 0), (b_ref, SWA_H, FOX_W), (c_ref, MEM_H, FOX_W + SWA_W)):
            for h in range(n_heads):
                cols.append(jnp.sum(d_ref[:, _hs(h, base)] * ref[:, _hs(h)], axis=-1, keepdims=True))
        o_ref[...] = _head_column(cols)
        for h in range(FOX_H):
            rep_ref[h] = jnp.broadcast_to(cols[h], (tm, HEAD))

    def rows(w):
        return pl.BlockSpec((tm, w), lambda i: (i, 0))

    return pl.pallas_call(body, grid=(t // tm,), in_specs=[rows(dmixed.shape[1]), rows(FOX_W), rows(SWA_W), rows(MEM_W)],
                          out_specs=[rows(HEAD), pl.BlockSpec((FOX_H, tm, HEAD), lambda i: (0, i, 0))],
                          out_shape=[jax.ShapeDtypeStruct((t, HEAD), F32), jax.ShapeDtypeStruct((FOX_H, t, HEAD), F32)],
                          compiler_params=_params(1), name=name)(dmixed, o_a, o_b, o_c)


def _causal_steps(nb, rows_major):
    if nb % 2:
        if rows_major:
            return (nb, nb), lambda a, b: (a, jnp.minimum(a, b), b <= a)
        return (nb, nb), lambda a, b: (jnp.maximum(a, b), a, b >= a)

    def fold(p, k):
        if rows_major:
            low = k <= p
            return jnp.where(low, p, nb - 1 - p), jnp.where(low, k, k - p - 1), True
        low = k < nb - p
        return jnp.where(low, p + k, k - 1), jnp.where(low, p, nb - 1 - p), True

    return (nb // 2, nb + 1), fold


def _fox_fwd(fq, fk, fv, c_rep, c_row, name, rides=()):
    t = fq.shape[0]
    tb = min(t, 512)
    nb = t // tb
    n_tiles = tb // HEAD
    grid, pair_of = _causal_steps(nb, True)

    def body(q_ref, k_ref, v_ref, cc_ref, cr_ref, o_ref, lse_ref, m_s, l_s, acc_s):
        qi, ki, live = pair_of(pl.program_id(0), pl.program_id(1))

        @pl.when(live & (ki == 0))
        def _():
            m_s[...] = jnp.full_like(m_s, NEG_INF)
            l_s[...] = jnp.zeros_like(l_s)
            acc_s[...] = jnp.zeros_like(acc_s)

        def step(diagonal):
            if diagonal:
                r = lax.broadcasted_iota(jnp.int32, (tb, HEAD), 0)
                c = lax.broadcasted_iota(jnp.int32, (tb, HEAD), 1)
            for h in range(FOX_H):
                s = _dot(q_ref[:, _hs(h)], k_ref[:, _hs(h)], "nt")
                cc = cc_ref[h]
                tiles, m_cur = [], None
                for j in range(n_tiles):
                    st = s[:, _hs(j)] * SCALE + cc - cr_ref[h:h + 1, _hs(j)]
                    if diagonal:
                        st = jnp.where(r >= c + j * HEAD, st, NEG_INF)
                    tiles.append(st)
                    m_cur = st if m_cur is None else jnp.maximum(m_cur, st)
                m_prev = m_s[h]
                m_new = jnp.maximum(m_prev, jnp.max(m_cur, axis=-1, keepdims=True))
                alpha = jnp.exp(m_prev - m_new)
                ps = [jnp.exp(st - m_new) for st in tiles]
                l_cur = ps[0]
                for p in ps[1:]:
                    l_cur = l_cur + p
                l_s[h] = alpha * l_s[h] + jnp.sum(l_cur, axis=-1, keepdims=True)
                p = jnp.concatenate([p.astype(BF16) for p in ps], axis=1)
                acc_s[:, _hs(h)] = alpha * acc_s[:, _hs(h)] + _dot(p, v_ref[:, _hs(h)])
                m_s[h] = m_new

        @pl.when(live & (ki < qi))
        def _():
            step(False)

        @pl.when(live & (ki == qi))
        def _():
            step(True)
            for h in range(FOX_H):
                o_ref[:, _hs(h)] = acc_s[:, _hs(h)] / l_s[h]
                lse_ref[h] = m_s[h] + jnp.log(l_s[h])

    qspec = pl.BlockSpec((tb, FOX_W), lambda a, b: (pair_of(a, b)[0], 0))
    kspec = pl.BlockSpec((tb, FOX_W), lambda a, b: (pair_of(a, b)[1], 0))
    rep = pl.BlockSpec((FOX_H, tb, HEAD), lambda a, b: (0, pair_of(a, b)[0], 0))
    return _call(
        name, body, grid,
        [qspec, kspec, kspec, rep, pl.BlockSpec((8, tb), lambda a, b: (0, pair_of(a, b)[1]))],
        [qspec, rep],
        [jax.ShapeDtypeStruct((t, FOX_W), F32), jax.ShapeDtypeStruct((FOX_H, t, HEAD), F32)],
        [fq, fk, fv, c_rep, c_row],
        scratch=[pltpu.VMEM((FOX_H, tb, HEAD), F32), pltpu.VMEM((FOX_H, tb, HEAD), F32), pltpu.VMEM((tb, FOX_W), F32)],
        rides=rides)


def _fox_bwd(fq, fk, fv, c_rep, c_row, dmixed, lse, delta, name, rides=()):
    t = fq.shape[0]
    tb = min(t, 512)
    nb = t // tb
    n_tiles = tb // HEAD
    grid, pair_of = _causal_steps(nb, False)

    def body(q_ref, k_ref, v_ref, cc_ref, cr_ref, do_ref, lse_ref, dl_ref,
             dq_ref, dk_ref, dv_ref, dcc_ref, dcr_ref):
        qi, ki, live = pair_of(pl.program_id(0), pl.program_id(1))

        @pl.when((pl.program_id(0) == 0) & (pl.program_id(1) == 0))
        def _():
            dq_ref[...] = jnp.zeros_like(dq_ref)
            dcc_ref[...] = jnp.zeros_like(dcc_ref)

        @pl.when(live & (qi == ki))
        def _():
            dk_ref[...] = jnp.zeros_like(dk_ref)
            dv_ref[...] = jnp.zeros_like(dv_ref)
            dcr_ref[...] = jnp.zeros_like(dcr_ref)

        def step(diagonal):
            rows = pl.ds(pl.multiple_of(qi * tb, tb), tb)
            if diagonal:
                r = lax.broadcasted_iota(jnp.int32, (tb, HEAD), 0)
                c = lax.broadcasted_iota(jnp.int32, (tb, HEAD), 1)
            row_sums, col_sums = [], []
            for h in range(FOX_H):
                q, k, v, do = q_ref[:, _hs(h)], k_ref[:, _hs(h)], v_ref[:, _hs(h)], do_ref[:, _hs(h)]
                s = _dot(q, k, "nt")
                dp = _dot(do, v, "nt")
                cc, lse_h, dl_h = cc_ref[h], lse_ref[h], dl_ref[h]
                ps, dss, row = [], [], None
                for j in range(n_tiles):
                    st = s[:, _hs(j)] * SCALE + cc - cr_ref[h:h + 1, _hs(j)]
                    if diagonal:
                        st = jnp.where(r >= c + j * HEAD, st, NEG_INF)
                    pt = jnp.exp(st - lse_h)
                    dst = pt * (dp[:, _hs(j)] - dl_h)
                    ps.append(pt.astype(BF16))
                    dss.append(dst)
                    row = dst if row is None else row + dst
                p = jnp.concatenate(ps, axis=1)
                ds = jnp.concatenate(dss, axis=1)
                dsb = ds.astype(BF16)
                dv_ref[:, _hs(h)] += _dot(p, do, "tn")
                dk_ref[:, _hs(h)] += _dot(dsb, q, "tn") * SCALE
                dq_ref[rows, _hs(h)] += _dot(dsb, k) * SCALE
                row_sums.append(jnp.sum(row, axis=1, keepdims=True))
                col_sums.append(jnp.sum(ds, axis=0, keepdims=True))
            dcc_ref[rows, :] += _head_column(row_sums)
            dcr_ref[...] += _head_row(col_sums)

        @pl.when(live & (qi > ki))
        def _():
            step(False)

        @pl.when(live & (qi == ki))
        def _():
            step(True)

    qspec = pl.BlockSpec((tb, FOX_W), lambda a, b: (pair_of(a, b)[0], 0))
    kspec = pl.BlockSpec((tb, FOX_W), lambda a, b: (pair_of(a, b)[1], 0))
    rep = pl.BlockSpec((FOX_H, tb, HEAD), lambda a, b: (0, pair_of(a, b)[0], 0))
    rowspec = pl.BlockSpec((8, tb), lambda a, b: (0, pair_of(a, b)[1]))
    return _call(
        name, body, grid, [qspec, kspec, kspec, rep, rowspec, qspec, rep, rep],
        [pl.BlockSpec((t, FOX_W), lambda a, b: (0, 0)), kspec, kspec,
         pl.BlockSpec((t, HEAD), lambda a, b: (0, 0)), rowspec],
        [jax.ShapeDtypeStruct((t, FOX_W), F32)] * 3 + [jax.ShapeDtypeStruct((t, HEAD), F32),
                                                       jax.ShapeDtypeStruct((8, t), F32)],
        [fq, fk, fv, c_rep, c_row, dmixed, lse, delta], rides=rides)


def _swa_logits(q, k_cur, k_prev, slope, first_block):
    w = SWA_BLOCK
    r = lax.broadcasted_iota(jnp.int32, (w, w), 0)
    j = lax.broadcasted_iota(jnp.int32, (w, w), 1)
    dist_cur = r - j
    dist_prev = w + r - j
    s_cur = _dot(q, k_cur, "nt") * SCALE - slope * dist_cur.astype(F32)
    s_cur = jnp.where(dist_cur >= 0, s_cur, NEG_INF)
    s_prev = _dot(q, k_prev, "nt") * SCALE - slope * dist_prev.astype(F32)
    s_prev = jnp.where((j > r) & jnp.logical_not(first_block), s_prev, NEG_INF)
    return s_cur, s_prev


def _slope(h):
    return float(2.0 ** (-8.0 * (h + 1) / SWA_H))


def _swa_fwd(sq, sk, sv, sp, name, rides=()):
    t = sq.shape[0]
    w = SWA_BLOCK
    nb = t // w
    group = SWA_H // SWA_KV

    def body(q_ref, kp_ref, kc_ref, vp_ref, vc_ref, sp_ref, o_ref, lse_ref):
        first = pl.program_id(0) == 0
        lses = []
        for h in range(SWA_H):
            kv = h // group
            s_cur, s_prev = _swa_logits(q_ref[:, _hs(h)], kc_ref[:, _hs(kv)], kp_ref[:, _hs(kv)], _slope(h), first)
            sink = sp_ref[R_SINK:R_SINK + 1, h:h + 1]
            m = jnp.maximum(jnp.maximum(jnp.max(s_cur, axis=-1, keepdims=True),
                                        jnp.max(s_prev, axis=-1, keepdims=True)), sink)
            p_cur = jnp.exp(s_cur - m)
            p_prev = jnp.exp(s_prev - m)
            l = jnp.sum(p_cur, axis=-1, keepdims=True) + jnp.sum(p_prev, axis=-1, keepdims=True) + jnp.exp(sink - m)
            o_ref[:, _hs(h)] = (_dot(p_cur, vc_ref[:, _hs(kv)]) + _dot(p_prev, vp_ref[:, _hs(kv)])) / l
            lses.append(m + jnp.log(l))
        lse_ref[...] = _head_column(lses)

    qspec = pl.BlockSpec((w, SWA_W), lambda n: (n, 0))
    cur = pl.BlockSpec((w, SWA_KV_W), lambda n: (n, 0))
    prev = pl.BlockSpec((w, SWA_KV_W), lambda n: (jnp.maximum(n - 1, 0), 0))
    return _call(
        name, body, (nb,), [qspec, prev, cur, prev, cur, pl.BlockSpec((16, 128), lambda n: (0, 0))],
        [qspec, pl.BlockSpec((w, HEAD), lambda n: (n, 0))],
        [jax.ShapeDtypeStruct((t, SWA_W), F32), jax.ShapeDtypeStruct((t, HEAD), F32)],
        [sq, sk, sk, sv, sv, sp], rides=rides)


def _swa_bwd(sq, sk, sv, sp, dmixed, lse, delta, name):
    t = sq.shape[0]
    w = SWA_BLOCK
    nb = t // w
    group = SWA_H // SWA_KV
    do_block = FOX_W // SWA_W
    assert FOX_W % SWA_W == 0

    def body(q_ref, kp_ref, kc_ref, vp_ref, vc_ref, sp_ref, do_ref, lse_ref, dl_ref,
             dq_ref, dk_ref, dv_ref, dsp_ref, ck, cv):
        step = pl.program_id(0)
        first = step == nb - 1

        @pl.when(step == 0)
        def _():
            ck[...] = jnp.zeros_like(ck)
            cv[...] = jnp.zeros_like(cv)
            dsp_ref[...] = jnp.zeros_like(dsp_ref)

        dk_cur = [None] * SWA_KV
        dk_prev = [None] * SWA_KV
        dv_cur = [None] * SWA_KV
        dv_prev = [None] * SWA_KV
        dsinks = []

        def add(lst, i, v):
            lst[i] = v if lst[i] is None else lst[i] + v

        for h in range(SWA_H):
            kv = h // group
            q, do = q_ref[:, _hs(h)], do_ref[:, _hs(h)]
            kc, kp, vc, vp = kc_ref[:, _hs(kv)], kp_ref[:, _hs(kv)], vc_ref[:, _hs(kv)], vp_ref[:, _hs(kv)]
            s_cur, s_prev = _swa_logits(q, kc, kp, _slope(h), first)
            lse_h = lse_ref[:, h:h + 1]
            dl_h = dl_ref[:, FOX_H + h:FOX_H + h + 1]
            p_cur = jnp.exp(s_cur - lse_h)
            p_prev = jnp.exp(s_prev - lse_h)
            p_sink = jnp.exp(sp_ref[R_SINK:R_SINK + 1, h:h + 1] - lse_h)
            ds_cur = p_cur * (_dot(do, vc, "nt") - dl_h)
            ds_prev = p_prev * (_dot(do, vp, "nt") - dl_h)
            dq_ref[:, _hs(h)] = (_dot(ds_cur, kc) + _dot(ds_prev, kp)) * SCALE
            add(dk_cur, kv, _dot(ds_cur, q, "tn") * SCALE)
            add(dk_prev, kv, _dot(ds_prev, q, "tn") * SCALE)
            add(dv_cur, kv, _dot(p_cur, do, "tn"))
            add(dv_prev, kv, _dot(p_prev, do, "tn"))
            dsinks.append(-jnp.sum(p_sink * dl_h, axis=0, keepdims=True))
        for kv in range(SWA_KV):
            dk_ref[:, _hs(kv)] = dk_cur[kv] + ck[:, _hs(kv)]
            dv_ref[:, _hs(kv)] = dv_cur[kv] + cv[:, _hs(kv)]
            ck[:, _hs(kv)] = dk_prev[kv]
            cv[:, _hs(kv)] = dv_prev[kv]
        lane = lax.broadcasted_iota(jnp.int32, (1, HEAD), 1)
        row = jnp.zeros((1, HEAD), F32)
        for h in range(SWA_H):
            row = jnp.where(lane == h, dsinks[h], row)
        dsp_ref[R_SINK:R_SINK + 1, :] += row

    def rev(n):
        return nb - 1 - n

    qspec = pl.BlockSpec((w, SWA_W), lambda n: (rev(n), 0))
    cur = pl.BlockSpec((w, SWA_KV_W), lambda n: (rev(n), 0))
    prev = pl.BlockSpec((w, SWA_KV_W), lambda n: (jnp.maximum(rev(n) - 1, 0), 0))
    col = pl.BlockSpec((w, HEAD), lambda n: (rev(n), 0))
    small = pl.BlockSpec((16, 128), lambda n: (0, 0))
    return pl.pallas_call(
        body, grid=(nb,),
        in_specs=[qspec, prev, cur, prev, cur, small, pl.BlockSpec((w, SWA_W), lambda n: (rev(n), do_block)), col, col],
        out_specs=[qspec, cur, cur, small],
        out_shape=[jax.ShapeDtypeStruct((t, SWA_W), F32), jax.ShapeDtypeStruct((t, SWA_KV_W), F32),
                   jax.ShapeDtypeStruct((t, SWA_KV_W), F32), jax.ShapeDtypeStruct((16, 128), F32)],
        scratch_shapes=[pltpu.VMEM((w, SWA_KV_W), F32), pltpu.VMEM((w, SWA_KV_W), F32)],
        compiler_params=_params(1), name=name)(sq, sk, sk, sv, sv, sp, dmixed, lse, delta)


def _mem_pre(mkv, sp, name):
    m = mkv.shape[0]

    def body(x_ref, sp_ref, k_ref, v_ref):
        for h in range(MEM_H):
            k_ref[:, _hs(h)] = _head_norm(x_ref[:, _hs(h)], sp_ref[R_MK:R_MK + 1, :]).astype(BF16)
        v_ref[...] = x_ref[:, MEM_W:2 * MEM_W].astype(BF16)

    out = jax.ShapeDtypeStruct((m, MEM_W), BF16)
    return pl.pallas_call(body, out_shape=[out, out], name=name)(mkv, sp)


def _mem_post_bwd(mkv, sp, dmk, dmv, name):
    m = mkv.shape[0]

    def body(x_ref, sp_ref, dk_ref, dv_ref, d_ref, dsp_ref):
        dsp_ref[...] = jnp.zeros_like(dsp_ref)
        total = None
        for h in range(MEM_H):
            dx, dg = _head_norm_bwd(x_ref[:, _hs(h)], sp_ref[R_MK:R_MK + 1, :], dk_ref[:, _hs(h)])
            d_ref[:, _hs(h)] = dx.astype(BF16)
            total = dg if total is None else total + dg
        d_ref[:, MEM_W:2 * MEM_W] = dv_ref[...].astype(BF16)
        dsp_ref[R_MK:R_MK + 1, :] = total

    return pl.pallas_call(body, out_shape=[jax.ShapeDtypeStruct((m, 2 * MEM_W), BF16),
                                           jax.ShapeDtypeStruct((16, 128), F32)], name=name)(mkv, sp, dmk, dmv)


def _mem_fwd(mq, mk, mv, name):
    t = mq.shape[0]
    m = mk.shape[0]
    tq = min(t, 512)

    def body(q_ref, k_ref, v_ref, o_ref, lse_ref):
        lses = []
        for h in range(MEM_H):
            s = _dot(q_ref[:, _hs(h)], k_ref[:, _hs(h)], "nt") * SCALE
            mx = jnp.max(s, axis=-1, keepdims=True)
            p = jnp.exp(s - mx)
            l = jnp.sum(p, axis=-1, keepdims=True)
            o_ref[:, _hs(h)] = _dot(p, v_ref[:, _hs(h)]) / l
            lses.append(mx + jnp.log(l))
        lse_ref[...] = _head_column(lses)

    qspec = pl.BlockSpec((tq, MEM_W), lambda i: (i, 0))
    kspec = pl.BlockSpec((m, MEM_W), lambda i: (0, 0))
    return pl.pallas_call(
        body, grid=(t // tq,), in_specs=[qspec, kspec, kspec],
        out_specs=[qspec, pl.BlockSpec((tq, HEAD), lambda i: (i, 0))],
        out_shape=[jax.ShapeDtypeStruct((t, MEM_W), F32), jax.ShapeDtypeStruct((t, HEAD), F32)],
        compiler_params=_params(1), name=name)(mq, mk, mv)


def _mem_bwd(mq, mk, mv, dmixed, lse, delta, name):
    t = mq.shape[0]
    m = mk.shape[0]
    tq = min(t, 512)
    do_block = (FOX_W + SWA_W) // MEM_W
    assert (FOX_W + SWA_W) % MEM_W == 0

    def body(q_ref, k_ref, v_ref, do_ref, lse_ref, dl_ref, dq_ref, dk_ref, dv_ref):
        @pl.when(pl.program_id(0) == 0)
        def _():
            dk_ref[...] = jnp.zeros_like(dk_ref)
            dv_ref[...] = jnp.zeros_like(dv_ref)

        for h in range(MEM_H):
            q, k, v, do = q_ref[:, _hs(h)], k_ref[:, _hs(h)], v_ref[:, _hs(h)], do_ref[:, _hs(h)]
            s = _dot(q, k, "nt") * SCALE
            p = jnp.exp(s - lse_ref[:, h:h + 1])
            col = FOX_H + SWA_H + h
            ds = p * (_dot(do, v, "nt") - dl_ref[:, col:col + 1])
            dq_ref[:, _hs(h)] = _dot(ds, k) * SCALE
            dk_ref[:, _hs(h)] += _dot(ds, q, "tn") * SCALE
            dv_ref[:, _hs(h)] += _dot(p, do, "tn")

    qspec = pl.BlockSpec((tq, MEM_W), lambda i: (i, 0))
    kspec = pl.BlockSpec((m, MEM_W), lambda i: (0, 0))
    col = pl.BlockSpec((tq, HEAD), lambda i: (i, 0))
    return pl.pallas_call(
        body, grid=(t // tq,),
        in_specs=[qspec, kspec, kspec, pl.BlockSpec((tq, MEM_W), lambda i: (i, do_block)), col, col],
        out_specs=[qspec, kspec, kspec],
        out_shape=[jax.ShapeDtypeStruct((t, MEM_W), F32), jax.ShapeDtypeStruct((m, MEM_W), F32),
                   jax.ShapeDtypeStruct((m, MEM_W), F32)],
        compiler_params=_params(1), name=name)(mq, mk, mv, dmixed, lse, delta)


def _all_gather(xs, name):
    n = len(xs)

    def body(*refs):
        x_refs, o_refs = refs[:n], refs[n:2 * n]
        send_sems, recv_sems, local_sems = refs[2 * n:]
        x, y, c = _me()
        me, sibling = (x, y, c), (x, y, 1 - c)
        x_nb, y_nb, diag = (1 - x, y, c), (x, 1 - y, c), (1 - x, 1 - y, c)
        relay_from = (x + (1 - c) * (1 - 2 * x), y + c * (1 - 2 * y), c)
        relay_to = (x + c * (1 - 2 * x), y + (1 - c) * (1 - 2 * y), c)

        def copy(a, k, block, to, src=None):
            slot = o_refs[a].at[_lin(block)]
            return pltpu.make_async_remote_copy(
                src_ref=slot if src is None else src, dst_ref=slot, send_sem=send_sems.at[a, k],
                recv_sem=recv_sems.at[a, k], device_id=to, device_id_type=MESH)

        mine = [pltpu.make_async_copy(x_refs[a], o_refs[a].at[_lin(me)], local_sems.at[a]) for a in range(n)]
        for cp in mine:
            cp.start()
        sent = []
        for a in range(n):
            sent += [copy(a, 0, me, sibling, src=x_refs[a]), copy(a, 1, me, x_nb, src=x_refs[a]),
                     copy(a, 2, me, y_nb, src=x_refs[a])]
        for cp in sent:
            cp.start()

        def pass_on(cp):
            cp.start()
            sent.append(cp)

        for a in range(n):
            copy(a, 1, x_nb, me).wait_recv()
            copy(a, 2, y_nb, me).wait_recv()
            pass_on(copy(a, 3, relay_from, relay_to))
            pass_on(copy(a, 4, x_nb, sibling))
            pass_on(copy(a, 5, y_nb, sibling))
        for a in range(n):
            copy(a, 3, diag, me).wait_recv()
            pass_on(copy(a, 6, diag, sibling))
        for a in range(n):
            copy(a, 0, sibling, me).wait_recv()
            for k, chip in ((4, (1 - x, y)), (5, (x, 1 - y)), (6, (1 - x, 1 - y))):
                copy(a, k, (*chip, 1 - c), me).wait_recv()
        for cp in sent:
            cp.wait_send()
        for cp in mine:
            cp.wait()

    return pl.pallas_call(
        body, in_specs=[ANY] * n, out_specs=[ANY] * n,
        out_shape=[jax.ShapeDtypeStruct((N_DEV,) + x.shape, x.dtype) for x in xs],
        scratch_shapes=[pltpu.SemaphoreType.DMA((n, 7)), pltpu.SemaphoreType.DMA((n, 7)),
                        pltpu.SemaphoreType.DMA((n,))],
        name=name)(*xs)


def _peers():
    x, y, c = _me()
    out = []
    for k in range(1, N_DEV):
        kx, ky, kc = (k >> 2) & 1, (k >> 1) & 1, k & 1
        out.append(((1 - x) if kx else x, (1 - y) if ky else y, (1 - c) if kc else c))
    return out


def _all_reduce_small(xs, name):
    n = len(xs)

    def body(*refs):
        x_refs, o_refs = refs[:n], refs[n:2 * n]
        bufs = refs[2 * n:3 * n]
        send_sems, recv_sems = refs[3 * n:]
        me = _lin(_me())
        peers = _peers()
        for a in range(n):
            bufs[a][me] = x_refs[a][...]
        sends = []
        for a in range(n):
            for k, peer in enumerate(peers):
                sends.append(pltpu.make_async_remote_copy(
                    src_ref=bufs[a].at[me], dst_ref=bufs[a].at[me], send_sem=send_sems.at[a, k],
                    recv_sem=recv_sems.at[a, k], device_id=peer, device_id_type=MESH))
        for cp in sends:
            cp.start()
        for a in range(n):
            for k, peer in enumerate(peers):
                pltpu.make_async_remote_copy(
                    src_ref=bufs[a].at[me], dst_ref=bufs[a].at[_lin(peer)], send_sem=send_sems.at[a, k],
                    recv_sem=recv_sems.at[a, k], device_id=peer, device_id_type=MESH).wait_recv()
        for cp in sends:
            cp.wait_send()
        for a in range(n):
            total = bufs[a][0]
            for q in range(1, N_DEV):
                total = total + bufs[a][q]
            o_refs[a][...] = total

    vmem = pl.BlockSpec(memory_space=pltpu.VMEM)
    return pl.pallas_call(
        body, in_specs=[vmem] * n, out_specs=[vmem] * n,
        out_shape=[jax.ShapeDtypeStruct(x.shape, F32) for x in xs],
        scratch_shapes=[pltpu.VMEM((N_DEV,) + x.shape, F32) for x in xs]
        + [pltpu.SemaphoreType.DMA((n, 7)), pltpu.SemaphoreType.DMA((n, 7))],
        name=name)(*xs)


def _pair_add(part, got, name):
    _, rows, cols = part.shape
    tm = _rows_tile(rows, cols * 2, budget=2 << 20)
    core = jnp.reshape(lax.axis_index("c"), (1,)).astype(jnp.int32)

    def body(c_ref, p_ref, g_ref, o_ref):
        o_ref[...] = (p_ref[...].astype(F32) + g_ref[...].astype(F32)).astype(BF16)

    spec = pl.BlockSpec((None, tm, cols), lambda q, i, c: (q, i, 0))
    grid_spec = pltpu.PrefetchScalarGridSpec(
        num_scalar_prefetch=1, grid=(4, rows // tm),
        in_specs=[pl.BlockSpec((None, tm, cols), lambda q, i, c: (2 * q + c[0], i, 0)), spec], out_specs=spec)
    return pl.pallas_call(body, grid_spec=grid_spec, out_shape=jax.ShapeDtypeStruct((4, rows, cols), BF16),
                          compiler_params=_params(2), name=name)(core, part, got)


def _adam_math(w, g, m, v):
    nm = ADAM_B1 * m + (1.0 - ADAM_B1) * g
    nv = ADAM_B2 * v + (1.0 - ADAM_B2) * (g * g)
    m_hat = nm / (1.0 - ADAM_B1 ** ADAM_STEP)
    v_hat = nv / (1.0 - ADAM_B2 ** ADAM_STEP)
    return -ADAM_LR * (m_hat / (jnp.sqrt(v_hat) + ADAM_EPS) + ADAM_WD * w), nm, nv


def _sum_chips(got, name):
    _, rows, cols = got.shape
    tm = _rows_tile(rows, cols * 2 * 4, budget=2 << 20)

    def body(r_ref, o_ref):
        o_ref[...] = ((r_ref[0].astype(F32) + r_ref[1].astype(F32)) + r_ref[2].astype(F32)) + r_ref[3].astype(F32)

    return pl.pallas_call(
        body, grid=(rows // tm,), in_specs=[pl.BlockSpec((4, tm, cols), lambda i: (0, i, 0))],
        out_specs=pl.BlockSpec((tm, cols), lambda i: (i, 0)), out_shape=jax.ShapeDtypeStruct((rows, cols), F32),
        compiler_params=_params(1), name=name)(got)


def _sum_adamw(got, col_block, w, m, v, name):
    _, rows, cols = w.shape
    tm = _rows_tile(rows, cols * 4, budget=2 << 20)

    def body(r_ref, w_ref, m_ref, v_ref, g_ref, d_ref, nm_ref, nv_ref):
        g = ((r_ref[0].astype(F32) + r_ref[1].astype(F32)) + r_ref[2].astype(F32)) + r_ref[3].astype(F32)
        g_ref[...] = g
        d_ref[...], nm_ref[...], nv_ref[...] = _adam_math(w_ref[...], g, m_ref[...], v_ref[...])

    spec = pl.BlockSpec((None, tm, cols), lambda i: (0, i, 0))
    out = jax.ShapeDtypeStruct(w.shape, F32)
    return pl.pallas_call(
        body, grid=(rows // tm,), in_specs=[pl.BlockSpec((4, tm, cols), lambda i: (0, i, col_block)), spec, spec, spec],
        out_specs=[spec] * 4, out_shape=[out] * 4, compiler_params=_params(1), name=name)(got, w, m, v)


def _adamw(w, g, m, v, name):
    rows, cols = w.shape

    def body(w_ref, g_ref, m_ref, v_ref, d_ref, nm_ref, nv_ref):
        d_ref[...], nm_ref[...], nv_ref[...] = _adam_math(w_ref[...], g_ref[...], m_ref[...], v_ref[...])

    tm = _rows_tile(rows, cols * 4, budget=2 << 20, mult=8)
    spec = pl.BlockSpec((tm, cols), lambda i: (i, 0))
    out = jax.ShapeDtypeStruct(w.shape, F32)
    return pl.pallas_call(body, grid=(rows // tm,), in_specs=[spec] * 4, out_specs=[spec] * 3,
                          out_shape=[out] * 3, compiler_params=_params(1), name=name)(w, g, m, v)


def _permute_in(w):
    logit0 = 3 * FOX_W
    pad = jnp.zeros(w.shape[:-1] + (HEAD - N_LOGIT,), w.dtype)
    return jnp.concatenate([w[..., :logit0], w[..., logit0 + N_LOGIT:], w[..., logit0:logit0 + N_LOGIT], pad], axis=-1)


def _unpermute_in(w):
    logit0 = 3 * FOX_W
    return jnp.concatenate([w[..., :logit0], w[..., C_FL:C_FL + N_LOGIT], w[..., logit0:C_FL]], axis=-1)


def _pad_row(v, width):
    return jnp.pad(v, ((0, 0), (0, width - v.shape[1])))


def _pack_small(fq, fk, sq, sk, mq, mk, fb, sinks):
    rows = [fq, fk, sq, sk, mq, mk, _pad_row(fb, HEAD), _pad_row(sinks, HEAD)]
    return jnp.concatenate(rows + [jnp.zeros((8, HEAD), F32)], axis=0)


def _pack_norms(a, b, c, d):
    return jnp.concatenate([a, b, c, d, jnp.zeros((4, a.shape[1]), F32)], axis=0)


def kernel(x, mem, ffn1_norm, ffn1_gate, ffn1_up, ffn1_down, mix_norm, mem_norm, w_in, forget_bias, w_mem_k, w_mem_v, fox_q_gain, fox_k_gain, swa_q_gain, swa_k_gain, swa_sinks, mem_q_gain, mem_k_gain, w_out, ffn2_norm, ffn2_gate, ffn2_up, ffn2_down, loss_target, m_ffn1_norm, m_ffn1_gate, m_ffn1_up, m_ffn1_down, m_mix_norm, m_mem_norm, m_w_in, m_forget_bias, m_w_mem_k, m_w_mem_v, m_fox_q_gain, m_fox_k_gain, m_swa_q_gain, m_swa_k_gain, m_swa_sinks, m_mem_q_gain, m_mem_k_gain, m_w_out, m_ffn2_norm, m_ffn2_gate, m_ffn2_up, m_ffn2_down, v_ffn1_norm, v_ffn1_gate, v_ffn1_up, v_ffn1_down, v_mix_norm, v_mem_norm, v_w_in, v_forget_bias, v_w_mem_k, v_w_mem_v, v_fox_q_gain, v_fox_k_gain, v_swa_q_gain, v_swa_k_gain, v_swa_sinks, v_mem_q_gain, v_mem_k_gain, v_w_out, v_ffn2_norm, v_ffn2_gate, v_ffn2_up, v_ffn2_down):
    x0 = x[0]
    mem0 = mem[0]
    target = loss_target[0]
    t, d = x0.shape
    d_shard = w_in.shape[1]
    m_len = mem0.shape[0]
    tm = min(t, 512)
    tp = min(t, 1024)
    tk = min(t, 2048)
    tn = IN_W // 3
    tkw, tnw = min(t, 1024), IN_W // 3

    def swap(a):
        return jnp.swapaxes(a, 1, 2)

    gate1, up1, gate2, up2 = swap(ffn1_gate), swap(ffn1_up), swap(ffn2_gate), swap(ffn2_up)

    local = {
        "g1": gate1[0], "u1": up1[0], "d1": ffn1_down[0],
        "g2": gate2[0], "u2": up2[0], "d2": ffn2_down[0],
        "in": _permute_in(w_in[0]), "out": w_out[0],
        "mkv": jnp.concatenate([w_mem_k[0], w_mem_v[0]], axis=1),
    }
    shard = {k: _cast_bf16(v, f"cast_{k}") for k, v in local.items()}
    sp = _pack_small(fox_q_gain, fox_k_gain, swa_q_gain, swa_k_gain, mem_q_gain, mem_k_gain, forget_bias, swa_sinks)
    wt = {}

    wt["g1"], wt["u1"] = _all_gather([shard["g1"], shard["u1"]], "gather_ffn1_in")
    xn1 = _rms_fwd(x0, ffn1_norm, "ffn1_norm")
    half_fs = shard["g2"].shape[0] // 2
    (a1, b1, h1), ((wt["d1"], wt["in"]),) = _ffn_up(
        xn1, wt["g1"], wt["u1"], "ffn1", rides=[_ride_gather([shard["d1"], shard["in"]], 0.87)],
        rows=512)
    tc = min(t, CONTRACT_ROWS)
    (x1, hn), ((wt["out"], wt["mkv"]),) = _ffn_down(
        x0, h1, wt["d1"], "ffn1", rides=[_ride_gather([shard["out"], shard["mkv"]], 0.6)],
        tail=_tail_norm(mix_norm, t, d, tc))
    w_in_full = wt["in"].reshape(d, IN_W)

    proj, (half,) = _mm(
        "proj", [(hn, pl.BlockSpec((tp, d), lambda n, i, k: (i, 0)),
                  w_in_full, pl.BlockSpec((d, tn), lambda n, i, k: (0, n)))],
        "nn", (3, t // tp, 1), jax.ShapeDtypeStruct((t, IN_W), F32), pl.BlockSpec((tp, tn), lambda n, i, k: (i, n)),
        rides=[_ride_gather_chips([shard["g2"]], rows=(0, half_fs))])
    w_out_full = wt["out"].reshape(d, d)
    w_mkv_full = wt["mkv"].reshape(d, 2 * MEM_W)
    fq, fk, fv, sq, sk, sv, mq, c_col = _attn_pre(proj, sp, "attn_pre")
    c_row = jnp.transpose(c_col[:, :8])
    c_rep = jnp.broadcast_to(c_row[:FOX_H, :, None], (FOX_H, t, HEAD))

    mn = _rms_fwd(mem0, mem_norm, "mem_norm")
    mkv = _mm("mem_kv", [(mn, pl.BlockSpec((m_len, d), lambda k: (0, 0)),
                          w_mkv_full, pl.BlockSpec((d, 2 * MEM_W), lambda k: (0, 0)))],
              "nn", (1,), jax.ShapeDtypeStruct((m_len, 2 * MEM_W), F32),
              pl.BlockSpec((m_len, 2 * MEM_W), lambda k: (0, 0)))
    mk, mv = _mem_pre(mkv, sp, "mem_pre")

    (o_a, lse_a), (half, half_u2) = _fox_fwd(
        fq, fk, fv, c_rep, c_row, "fox_fwd",
        rides=[_ride_gather_chips([shard["g2"]], rows=(half_fs, half_fs), into=half),
               _ride_gather_chips([shard["u2"]], rows=(0, half_fs))])
    (o_b, lse_b), ((wt["g2"],), half_u2) = _swa_fwd(
        sq, sk, sv, sp, "swa_fwd",
        rides=[_ride_gather_sibling(half), _ride_gather_chips([shard["u2"]], rows=(half_fs, half_fs), into=half_u2)])
    o_c, lse_c = _mem_fwd(mq, mk, mv, "mem_fwd")

    def rows_spec(width):
        return pl.BlockSpec((tm, width), lambda i, k: (i, 0))

    def wout_rows(first, width):
        assert first % width == 0
        return pl.BlockSpec((width, d), lambda i, k: (first // width, 0), pipeline_mode=pl.Buffered(1))

    xspec = pl.BlockSpec((tm, d), lambda i, k: (i, 0))
    (x2, xn2), ((wt["u2"],),) = _mm(
        "mix_out",
        [(o_a, rows_spec(FOX_W), w_out_full, wout_rows(0, FOX_W)),
         (o_b, rows_spec(SWA_W), w_out_full, wout_rows(FOX_W, SWA_W)),
         (o_c, rows_spec(MEM_W), w_out_full, wout_rows(FOX_W + SWA_W, MEM_W))],
        "nn", (t // tm, 1), jax.ShapeDtypeStruct((t, d), F32), xspec, res=x1, res_spec=xspec,
        rides=[_ride_gather_sibling(half_u2)], tail=_tail_norm(ffn2_norm, t, d, tm))

    (a2, b2, h2), ((wt["d2"],),) = _ffn_up(xn2, wt["g2"], wt["u2"], "ffn2", rides=[_ride_gather([shard["d2"]], 0.75)])
    dy, dyb, sq_err = _ffn_down(x2, h2, wt["d2"], "ffn2", tail=_tail_loss(target, t, d, tc))
    loss = lax.psum(0.5 * sq_err[0, 0] / d, ("x", "y", "c"))

    got = {}
    paired = {}
    landed = {}

    def pair(k, part):
        paired[k] = _pair_add(part, got[k], f"pair_{k}")

    (dg2, du2), _ = _ffn_dact(dyb, wt["d2"], a2, b2, "ffn2")
    part_d2 = _ffn_dw(h2, dyb, 0.5, "ffn2_dwd")
    part_g2, ((got["d2"],),) = _ffn_dw(dg2, xn2, 1.0, "ffn2_dwg", rides=[_ride_scatter_sibling([part_d2])])
    pair("d2", part_d2)
    half_rows = part_d2.shape[1] // 2
    first, second = (0, half_rows), (half_rows, half_rows)
    part_u2, (half, (got["g2"],)) = _ffn_dw(
        du2, xn2, 1.0, "ffn2_dwu",
        rides=[_ride_scatter_chips([paired["d2"]], first), _ride_scatter_sibling([part_g2])])
    pair("g2", part_g2)
    dxn2, ((landed["d2"],),) = _ffn_contract(
        dg2, wt["g2"], "ffn2_dxn_g", rides=[_ride_scatter_chips([paired["d2"]], second, into=half)])
    (dx2, dx2b, dgain_ffn2), (half_g2, (got["u2"],)) = _ffn_contract(
        du2, wt["u2"], "ffn2_dxn_u", res=dxn2,
        rides=[_ride_scatter_chips([paired["g2"]], first), _ride_scatter_sibling([part_u2])],
        tail=_tail_norm_bwd(x2, ffn2_norm, dy, t, d, tc))
    pair("u2", part_u2)

    dmixed = _mm("mix_out_dx", [(dx2b, xspec, w_out_full, pl.BlockSpec((d, d), lambda i, k: (0, 0)))],
                 "nt", (t // tm, 1), jax.ShapeDtypeStruct((t, d), F32), xspec)

    def k_rows(width):
        return pl.BlockSpec((tk, width), lambda j, k: (k, 0))

    part_out = [
        _mm(f"mix_out_dw{i}", [(o, k_rows(width), dx2b, k_rows(d))], "tn", (1, t // tk),
            jax.ShapeDtypeStruct((width, d), BF16), pl.BlockSpec((width, d), lambda j, k: (0, 0)))
        for i, (o, width) in enumerate(((o_a, FOX_W), (o_b, SWA_W), (o_c, MEM_W)))
    ]
    part_out = jnp.concatenate(part_out, axis=0).reshape(N_DEV, d_shard, d)

    delta, delta_rep = _delta(dmixed, o_a, o_b, o_c, "attn_delta")
    (dfq, dfk, dfv, dc_col, dc_row), ((landed["g2"],), (landed["u2"],)) = _fox_bwd(
        fq, fk, fv, c_rep, c_row, dmixed, lse_a, delta_rep, "fox_bwd",
        rides=[_ride_scatter_chips([paired["g2"]], second, into=half_g2), _ride_scatter_chips([paired["u2"]])])
    dsq, dsk, dsv, dsp_sink = _swa_bwd(sq, sk, sv, sp, dmixed, lse_b, delta, "swa_bwd")
    dmq, dmk, dmv = _mem_bwd(mq, mk, mv, dmixed, lse_c, delta, "mem_bwd")

    dmkv, dsp_mem = _mem_post_bwd(mkv, sp, dmk, dmv, "mem_post_bwd")
    part_mkv = _mm("mem_kv_dw", [(mn, pl.BlockSpec((m_len, d), lambda k: (0, 0)),
                                  dmkv, pl.BlockSpec((m_len, 2 * MEM_W), lambda k: (0, 0)))],
                   "tn", (1,), jax.ShapeDtypeStruct((d, 2 * MEM_W), BF16),
                   pl.BlockSpec((d, 2 * MEM_W), lambda k: (0, 0))).reshape(N_DEV, d_shard, 2 * MEM_W)
    dmn = _mm("mem_kv_dx", [(dmkv, pl.BlockSpec((m_len, 2 * MEM_W), lambda k: (0, 0)),
                             w_mkv_full, pl.BlockSpec((d, 2 * MEM_W), lambda k: (0, 0)))],
              "nt", (1,), jax.ShapeDtypeStruct((m_len, d), F32), pl.BlockSpec((m_len, d), lambda k: (0, 0)))
    _, _, dgain_mem = _rms_bwd(mem0, mem_norm, dmn, None, "mem_norm_bwd")

    dc_row_t = _pad_row(jnp.transpose(dc_row), HEAD)
    dproj, dsp_attn = _attn_post_bwd(proj, sp, dfq, dfk, dfv, dsq, dsk, dsv, dmq, dc_col, dc_row_t, "attn_post_bwd")
    (dx1, dx1b, dgain_mix), ((got["out"], got["mkv"]),) = _mm(
        "proj_dx", [(dproj, pl.BlockSpec((tc, IN_W), lambda i, k: (i, 0)),
                     w_in_full, pl.BlockSpec((d, IN_W), lambda i, k: (0, 0), pipeline_mode=pl.Buffered(1)))],
        "nt", (t // tc, 1), jax.ShapeDtypeStruct((t, d), F32), pl.BlockSpec((tc, d), lambda i, k: (i, 0)),
        rides=[_ride_scatter_sibling([part_out, part_mkv])], tail=_tail_norm_bwd(x1, mix_norm, dx2, t, d, tc))
    pair("out", part_out)
    pair("mkv", part_mkv)
    part_in, ((landed["out"], landed["mkv"]),) = _mm(
        "proj_dw", [(hn, pl.BlockSpec((tkw, d), lambda n, k: (k, 0)),
                     dproj, pl.BlockSpec((tkw, tnw), lambda n, k: (k, n)))],
        "tn", (IN_W // tnw, t // tkw), jax.ShapeDtypeStruct((d, IN_W), BF16), pl.BlockSpec((d, tnw), lambda n, k: (0, n)),
        rides=[_ride_scatter_chips([paired["out"], paired["mkv"]])])
    part_in = part_in.reshape(N_DEV, d_shard, IN_W)

    part_d1, ((got["in"],),) = _ffn_dw(h1, dx1b, 0.5, "ffn1_dwd", rides=[_ride_scatter_sibling([part_in])])
    pair("in", part_in)
    (dg1, du1), ((landed["in"],), (got["d1"],)) = _ffn_dact(
        dx1b, wt["d1"], a1, b1, "ffn1",
        rides=[_ride_scatter_chips([paired["in"]]), _ride_scatter_sibling([part_d1])])
    pair("d1", part_d1)
    part_g1, (half_d1,) = _ffn_dw(dg1, xn1, 1.0, "ffn1_dwg", rides=[_ride_scatter_chips([paired["d1"]], first)])
    part_u1, ((landed["d1"],), (got["g1"],)) = _ffn_dw(
        du1, xn1, 1.0, "ffn1_dwu",
        rides=[_ride_scatter_chips([paired["d1"]], second, into=half_d1), _ride_scatter_sibling([part_g1])])
    pair("g1", part_g1)
    dxn1, ((landed["g1"],), (got["u1"],)) = _ffn_contract(
        dg1, wt["g1"], "ffn1_dxn_g", rides=[_ride_scatter_chips([paired["g1"]]), _ride_scatter_sibling([part_u1])])
    pair("u1", part_u1)
    (grad_x, _, dgain_ffn1), ((landed["u1"],),) = _ffn_contract(
        du1, wt["u1"], "ffn1_dxn_u", res=dxn1, rides=[_ride_scatter_chips([paired["u1"]])],
        tail=_tail_norm_bwd(x0, ffn1_norm, dx1, t, d, tc))

    norms_sum, small_sum = _all_reduce_small(
        [_pack_norms(dgain_ffn1, dgain_mix, dgain_mem, dgain_ffn2), dsp_attn + dsp_sink + dsp_mem], "reduce_small")

    result = {
        "ffn1_gate": map(swap, _sum_adamw(landed["g1"], 0, gate1, swap(m_ffn1_gate), swap(v_ffn1_gate), "adamw_ffn1_gate")),
        "ffn1_up": map(swap, _sum_adamw(landed["u1"], 0, up1, swap(m_ffn1_up), swap(v_ffn1_up), "adamw_ffn1_up")),
        "ffn1_down": _sum_adamw(landed["d1"], 0, ffn1_down, m_ffn1_down, v_ffn1_down, "adamw_ffn1_down"),
        "w_mem_k": _sum_adamw(landed["mkv"], 0, w_mem_k, m_w_mem_k, v_w_mem_k, "adamw_w_mem_k"),
        "w_mem_v": _sum_adamw(landed["mkv"], 1, w_mem_v, m_w_mem_v, v_w_mem_v, "adamw_w_mem_v"),
        "w_out": _sum_adamw(landed["out"], 0, w_out, m_w_out, v_w_out, "adamw_w_out"),
        "ffn2_gate": map(swap, _sum_adamw(landed["g2"], 0, gate2, swap(m_ffn2_gate), swap(v_ffn2_gate), "adamw_ffn2_gate")),
        "ffn2_up": map(swap, _sum_adamw(landed["u2"], 0, up2, swap(m_ffn2_up), swap(v_ffn2_up), "adamw_ffn2_up")),
        "ffn2_down": _sum_adamw(landed["d2"], 0, ffn2_down, m_ffn2_down, v_ffn2_down, "adamw_ffn2_down"),
    }
    grad_in = _unpermute_in(_sum_chips(landed["in"], "sum_w_in"))
    result["w_in"] = (grad_in[None],) + tuple(
        o[None] for o in _adamw(w_in[0], grad_in, m_w_in[0], v_w_in[0], "adamw_w_in"))

    norm_names = ["ffn1_norm", "mix_norm", "mem_norm", "ffn2_norm"]
    norm_w = _pack_norms(ffn1_norm, mix_norm, mem_norm, ffn2_norm)
    norm_m = _pack_norms(m_ffn1_norm, m_mix_norm, m_mem_norm, m_ffn2_norm)
    norm_v = _pack_norms(v_ffn1_norm, v_mix_norm, v_mem_norm, v_ffn2_norm)
    outs = (norms_sum,) + tuple(_adamw(norm_w, norms_sum, norm_m, norm_v, "adamw_norms"))
    for i, k in enumerate(norm_names):
        result[k] = tuple(o[i:i + 1] for o in outs)

    small_names = ["fox_q_gain", "fox_k_gain", "swa_q_gain", "swa_k_gain", "mem_q_gain", "mem_k_gain",
                   "forget_bias", "swa_sinks"]
    small_m = _pack_small(m_fox_q_gain, m_fox_k_gain, m_swa_q_gain, m_swa_k_gain, m_mem_q_gain, m_mem_k_gain,
                          m_forget_bias, m_swa_sinks)
    small_v = _pack_small(v_fox_q_gain, v_fox_k_gain, v_swa_q_gain, v_swa_k_gain, v_mem_q_gain, v_mem_k_gain,
                          v_forget_bias, v_swa_sinks)
    outs = (small_sum,) + tuple(_adamw(sp, small_sum, small_m, small_v, "adamw_small"))
    for i, k in enumerate(small_names):
        width = N_LOGIT if k in ("forget_bias", "swa_sinks") else HEAD
        result[k] = tuple(o[i:i + 1, :width] for o in outs)

    order = ["ffn1_norm", "ffn1_gate", "ffn1_up", "ffn1_down", "mix_norm", "mem_norm", "w_in", "forget_bias",
             "w_mem_k", "w_mem_v", "fox_q_gain", "fox_k_gain", "swa_q_gain", "swa_k_gain", "swa_sinks",
             "mem_q_gain", "mem_k_gain", "w_out", "ffn2_norm", "ffn2_gate", "ffn2_up", "ffn2_down"]
    result = {k: tuple(v) for k, v in result.items()}
    flat = [loss, grad_x[None]]
    for kind in range(4):
        flat += [result[k][kind] for k in order]
    return tuple(flat)
```
